```python
import math
import jax, jax.numpy as jnp
from jax import lax
import numpy as np


D_MODEL = 2048
BATCH = 8
SEQ = 8192
DEPTH = 1

CHUNK = 64
S5_WIDTH = 1024
S5_GROUP_WIDTH = 16
S5_GROUPS = S5_WIDTH // S5_GROUP_WIDTH
S5_STATE = 64
SGU_WIDTH = 1024
SGU_HEADS = 8
SGU_HEAD_DIM = SGU_WIDTH // SGU_HEADS
MLP_CHUNK = 128
MIX_IN = S5_WIDTH + 2 * SGU_WIDTH
D_FF = 5632
NORM_EPS = 1e-6
DT_MIN = 1e-3
DT_MAX = 1e-1

kernel_name = 'hybrid_s5_sgu_macaron_block'


def rms_norm(x, g):
    xf = x.astype(jnp.float32)
    y = xf * lax.rsqrt(jnp.mean(xf * xf, axis=-1, keepdims=True) + NORM_EPS)
    return (y * g.astype(jnp.float32)).astype(x.dtype)


def layer_norm(x, g, b):
    xf = x.astype(jnp.float32)
    mu = jnp.mean(xf, axis=-1, keepdims=True)
    var = jnp.mean(jnp.square(xf - mu), axis=-1, keepdims=True)
    y = (xf - mu) * lax.rsqrt(var + NORM_EPS)
    return (y * g.astype(jnp.float32) + b.astype(jnp.float32)).astype(x.dtype)


def swiglu_ffn(h, w_gate, w_up, w_down):
    return (jax.nn.silu(h @ w_gate) * (h @ w_up)) @ w_down


def _complex_affine_combine(e1, e2):
    a1r, a1i, b1r, b1i = e1
    a2r, a2i, b2r, b2i = e2
    ar = a2r * a1r - a2i * a1i
    ai = a2r * a1i + a2i * a1r
    br = a2r * b1r - a2i * b1i + b2r
    bi = a2r * b1i + a2i * b1r + b2i
    return (ar, ai, br, bi)


def s5_mixer(u, a_re, a_im, log_dt, b_re, b_im, c_re, c_im, d_skip, w_glu, b_glu):
    bsz, seq, _ = u.shape
    uf = u.astype(jnp.float32).reshape(bsz, seq, S5_GROUPS, S5_GROUP_WIDTH)
    lam_re = a_re.astype(jnp.float32)
    lam_im = a_im.astype(jnp.float32)
    dt = jnp.exp(log_dt.astype(jnp.float32))[:, None]
    decay = jnp.exp(lam_re * dt)
    abar_re = decay * jnp.cos(lam_im * dt)
    abar_im = decay * jnp.sin(lam_im * dt)
    denom = lam_re * lam_re + lam_im * lam_im
    num_re = abar_re - 1.0
    num_im = abar_im
    k_re = (num_re * lam_re + num_im * lam_im) / denom
    k_im = (num_im * lam_re - num_re * lam_im) / denom
    bu_re = jnp.einsum('blgc,gpc->blgp', uf, b_re.astype(jnp.float32))
    bu_im = jnp.einsum('blgc,gpc->blgp', uf, b_im.astype(jnp.float32))
    in_re = k_re * bu_re - k_im * bu_im
    in_im = k_re * bu_im + k_im * bu_re
    a_r = jnp.broadcast_to(abar_re, in_re.shape)
    a_i = jnp.broadcast_to(abar_im, in_re.shape)
    _, _, x_re, x_im = lax.associative_scan(
        _complex_affine_combine, (a_r, a_i, in_re, in_im), axis=1)
    y = (jnp.einsum('blgp,gcp->blgc', x_re, c_re.astype(jnp.float32))
         - jnp.einsum('blgp,gcp->blgc', x_im, c_im.astype(jnp.float32))
         + d_skip.astype(jnp.float32) * uf)
    y = jax.nn.gelu(y.reshape(bsz, seq, S5_WIDTH)).astype(u.dtype)
    return y * jax.nn.sigmoid(y @ w_glu + b_glu)


def sgu_mixer(uv, ln_g, ln_b, w_s, b_s):
    u, v = jnp.split(jax.nn.gelu(uv), 2, axis=-1)
    v = layer_norm(v, ln_g, ln_b)
    bsz, seq, _ = v.shape
    n_chunks = seq // MLP_CHUNK
    v = v.reshape(bsz, n_chunks, MLP_CHUNK, SGU_HEADS, SGU_HEAD_DIM)
    blk = jnp.arange(MLP_CHUNK) // CHUNK
    mask = blk[:, None] >= blk[None, :]
    ws = jnp.where(mask[None], w_s, jnp.zeros((), w_s.dtype))
    mixed = jnp.einsum('hts,bnshc->bnthc', ws, v) + jnp.transpose(b_s)[:, :, None]
    return u * mixed.reshape(bsz, seq, SGU_WIDTH)


def _fwd_setup_inputs(seed: int = 0) -> dict:
    key = jax.random.key(seed)
    keys = iter(jax.random.split(key, 40))
    L = DEPTH
    D = D_MODEL

    def nrm(shape, scale):
        return scale * jax.random.normal(next(keys), shape, jnp.float32)

    x = nrm((BATCH, SEQ, D), 1.0)
    ffn1_norm = 1.0 + nrm((L, D), 0.02)
    ffn1_w_gate = nrm((L, D, D_FF), D ** -0.5)
    ffn1_w_up = nrm((L, D, D_FF), D ** -0.5)
    ffn1_w_down = nrm((L, D_FF, D), D_FF ** -0.5)
    mix_norm = 1.0 + nrm((L, D), 0.02)
    w_in = nrm((L, D, MIX_IN), D ** -0.5)
    n_idx = jnp.arange(S5_STATE, dtype=jnp.float32)
    s5_a_re = -0.5 + nrm((L, S5_GROUPS, S5_STATE), 0.01)
    s5_a_im = math.pi * n_idx + nrm((L, S5_GROUPS, S5_STATE), 0.01)
    s5_log_dt = math.log(DT_MIN) + jax.random.uniform(
        next(keys), (L, S5_GROUPS), jnp.float32) * (math.log(DT_MAX) - math.log(DT_MIN))
    s5_b_re = nrm((L, S5_GROUPS, S5_STATE, S5_GROUP_WIDTH), (2 * S5_GROUP_WIDTH) ** -0.5)
    s5_b_im = nrm((L, S5_GROUPS, S5_STATE, S5_GROUP_WIDTH), (2 * S5_GROUP_WIDTH) ** -0.5)
    s5_c_re = nrm((L, S5_GROUPS, S5_GROUP_WIDTH, S5_STATE), S5_STATE ** -0.5)
    s5_c_im = nrm((L, S5_GROUPS, S5_GROUP_WIDTH, S5_STATE), S5_STATE ** -0.5)
    s5_d = nrm((L, S5_GROUPS, S5_GROUP_WIDTH), 1.0)
    s5_w_glu = nrm((L, S5_WIDTH, S5_WIDTH), S5_WIDTH ** -0.5)
    s5_b_glu = nrm((L, S5_WIDTH), 0.01)
    sgu_ln_g = 1.0 + nrm((L, SGU_WIDTH), 0.02)
    sgu_ln_b = nrm((L, SGU_WIDTH), 0.01)
    sgu_w_s = nrm((L, SGU_HEADS, MLP_CHUNK, MLP_CHUNK), 0.05)
    sgu_b_s = 1.0 + nrm((L, SGU_HEADS, MLP_CHUNK), 0.05)
    w_branch_a = nrm((L, S5_WIDTH, D), S5_WIDTH ** -0.5)
    w_branch_b = nrm((L, SGU_WIDTH, D), SGU_WIDTH ** -0.5)
    w_gate = nrm((L, D, 2 * D), D ** -0.5)
    b_gate = nrm((L, 2 * D), 0.01)
    w_out = nrm((L, D, D), D ** -0.5)
    ffn2_norm = 1.0 + nrm((L, D), 0.02)
    ffn2_w_gate = nrm((L, D, D_FF), D ** -0.5)
    ffn2_w_up = nrm((L, D, D_FF), D ** -0.5)
    ffn2_w_down = nrm((L, D_FF, D), D_FF ** -0.5)
    final_norm = 1.0 + nrm((D,), 0.02)
    return {
        'x': x,
        'ffn1_norm': ffn1_norm, 'ffn1_w_gate': ffn1_w_gate, 'ffn1_w_up': ffn1_w_up,
        'ffn1_w_down': ffn1_w_down,
        'mix_norm': mix_norm, 'w_in': w_in,
        's5_a_re': s5_a_re, 's5_a_im': s5_a_im, 's5_log_dt': s5_log_dt,
        's5_b_re': s5_b_re, 's5_b_im': s5_b_im, 's5_c_re': s5_c_re, 's5_c_im': s5_c_im,
        's5_d': s5_d, 's5_w_glu': s5_w_glu, 's5_b_glu': s5_b_glu,
        'sgu_ln_g': sgu_ln_g, 'sgu_ln_b': sgu_ln_b, 'sgu_w_s': sgu_w_s, 'sgu_b_s': sgu_b_s,
        'w_branch_a': w_branch_a, 'w_branch_b': w_branch_b,
        'w_gate': w_gate, 'b_gate': b_gate, 'w_out': w_out,
        'ffn2_norm': ffn2_norm, 'ffn2_w_gate': ffn2_w_gate, 'ffn2_w_up': ffn2_w_up,
        'ffn2_w_down': ffn2_w_down,
        'final_norm': final_norm,
    }


def _fwd_reference(x, ffn1_norm, ffn1_w_gate, ffn1_w_up, ffn1_w_down, mix_norm, w_in,
              s5_a_re, s5_a_im, s5_log_dt, s5_b_re, s5_b_im, s5_c_re, s5_c_im, s5_d,
              s5_w_glu, s5_b_glu, sgu_ln_g, sgu_ln_b, sgu_w_s, sgu_b_s,
              w_branch_a, w_branch_b, w_gate, b_gate, w_out,
              ffn2_norm, ffn2_w_gate, ffn2_w_up, ffn2_w_down, final_norm):
    for i in range(DEPTH):
        h = rms_norm(x, ffn1_norm[i])
        x = x + 0.5 * swiglu_ffn(h, ffn1_w_gate[i], ffn1_w_up[i], ffn1_w_down[i])
        h = rms_norm(x, mix_norm[i])
        proj = h @ w_in[i]
        u_a = proj[..., :S5_WIDTH]
        uv_b = proj[..., S5_WIDTH:]
        y_a = s5_mixer(u_a, s5_a_re[i], s5_a_im[i], s5_log_dt[i], s5_b_re[i], s5_b_im[i],
                       s5_c_re[i], s5_c_im[i], s5_d[i], s5_w_glu[i], s5_b_glu[i])
        y_b = sgu_mixer(uv_b, sgu_ln_g[i], sgu_ln_b[i], sgu_w_s[i], sgu_b_s[i])
        gates = jax.nn.sigmoid(h @ w_gate[i] + b_gate[i])
        g_a, g_b = jnp.split(gates, 2, axis=-1)
        merged = g_a * (y_a @ w_branch_a[i]) + g_b * (y_b @ w_branch_b[i])
        x = x + merged @ w_out[i]
        h = rms_norm(x, ffn2_norm[i])
        x = x + 0.5 * swiglu_ffn(h, ffn2_w_gate[i], ffn2_w_up[i], ffn2_w_down[i])
    return rms_norm(x, final_norm)


import jax as _jax
import jax.numpy as _jnp

TWIN_FORMAT = 'train_step'
FWD_PARAMS = ['x', 'ffn1_norm', 'ffn1_w_gate', 'ffn1_w_up', 'ffn1_w_down', 'mix_norm', 'w_in', 's5_a_re', 's5_a_im', 's5_log_dt', 's5_b_re', 's5_b_im', 's5_c_re', 's5_c_im', 's5_d', 's5_w_glu', 's5_b_glu', 'sgu_ln_g', 'sgu_ln_b', 'sgu_w_s', 'sgu_b_s', 'w_branch_a', 'w_branch_b', 'w_gate', 'b_gate', 'w_out', 'ffn2_norm', 'ffn2_w_gate', 'ffn2_w_up', 'ffn2_w_down', 'final_norm']
TWIN_WEIGHTS = ['ffn1_norm', 'ffn1_w_gate', 'ffn1_w_up', 'ffn1_w_down', 'mix_norm', 'w_in', 's5_a_re', 's5_a_im', 's5_log_dt', 's5_b_re', 's5_b_im', 's5_c_re', 's5_c_im', 's5_d', 's5_w_glu', 's5_b_glu', 'sgu_ln_g', 'sgu_ln_b', 'sgu_w_s', 'sgu_b_s', 'w_branch_a', 'w_branch_b', 'w_gate', 'b_gate', 'w_out', 'ffn2_norm', 'ffn2_w_gate', 'ffn2_w_up', 'ffn2_w_down', 'final_norm']
TWIN_DIFF_INPUT = 'x'
TWIN_INPUTS = ['x', 'ffn1_norm', 'ffn1_w_gate', 'ffn1_w_up', 'ffn1_w_down', 'mix_norm', 'w_in', 's5_a_re', 's5_a_im', 's5_log_dt', 's5_b_re', 's5_b_im', 's5_c_re', 's5_c_im', 's5_d', 's5_w_glu', 's5_b_glu', 'sgu_ln_g', 'sgu_ln_b', 'sgu_w_s', 'sgu_b_s', 'w_branch_a', 'w_branch_b', 'w_gate', 'b_gate', 'w_out', 'ffn2_norm', 'ffn2_w_gate', 'ffn2_w_up', 'ffn2_w_down', 'final_norm', 'loss_target', 'm_ffn1_norm', 'm_ffn1_w_gate', 'm_ffn1_w_up', 'm_ffn1_w_down', 'm_mix_norm', 'm_w_in', 'm_s5_a_re', 'm_s5_a_im', 'm_s5_log_dt', 'm_s5_b_re', 'm_s5_b_im', 'm_s5_c_re', 'm_s5_c_im', 'm_s5_d', 'm_s5_w_glu', 'm_s5_b_glu', 'm_sgu_ln_g', 'm_sgu_ln_b', 'm_sgu_w_s', 'm_sgu_b_s', 'm_w_branch_a', 'm_w_branch_b', 'm_w_gate', 'm_b_gate', 'm_w_out', 'm_ffn2_norm', 'm_ffn2_w_gate', 'm_ffn2_w_up', 'm_ffn2_w_down', 'm_final_norm', 'v_ffn1_norm', 'v_ffn1_w_gate', 'v_ffn1_w_up', 'v_ffn1_w_down', 'v_mix_norm', 'v_w_in', 'v_s5_a_re', 'v_s5_a_im', 'v_s5_log_dt', 'v_s5_b_re', 'v_s5_b_im', 'v_s5_c_re', 'v_s5_c_im', 'v_s5_d', 'v_s5_w_glu', 'v_s5_b_glu', 'v_sgu_ln_g', 'v_sgu_ln_b', 'v_sgu_w_s', 'v_sgu_b_s', 'v_w_branch_a', 'v_w_branch_b', 'v_w_gate', 'v_b_gate', 'v_w_out', 'v_ffn2_norm', 'v_ffn2_w_gate', 'v_ffn2_w_up', 'v_ffn2_w_down', 'v_final_norm']
TWIN_OUTPUTS = ['loss', 'grad_x', 'grad_ffn1_norm', 'grad_ffn1_w_gate', 'grad_ffn1_w_up', 'grad_ffn1_w_down', 'grad_mix_norm', 'grad_w_in', 'grad_s5_a_re', 'grad_s5_a_im', 'grad_s5_log_dt', 'grad_s5_b_re', 'grad_s5_b_im', 'grad_s5_c_re', 'grad_s5_c_im', 'grad_s5_d', 'grad_s5_w_glu', 'grad_s5_b_glu', 'grad_sgu_ln_g', 'grad_sgu_ln_b', 'grad_sgu_w_s', 'grad_sgu_b_s', 'grad_w_branch_a', 'grad_w_branch_b', 'grad_w_gate', 'grad_b_gate', 'grad_w_out', 'grad_ffn2_norm', 'grad_ffn2_w_gate', 'grad_ffn2_w_up', 'grad_ffn2_w_down', 'grad_final_norm', 'delta_ffn1_norm', 'delta_ffn1_w_gate', 'delta_ffn1_w_up', 'delta_ffn1_w_down', 'delta_mix_norm', 'delta_w_in', 'delta_s5_a_re', 'delta_s5_a_im', 'delta_s5_log_dt', 'delta_s5_b_re', 'delta_s5_b_im', 'delta_s5_c_re', 'delta_s5_c_im', 'delta_s5_d', 'delta_s5_w_glu', 'delta_s5_b_glu', 'delta_sgu_ln_g', 'delta_sgu_ln_b', 'delta_sgu_w_s', 'delta_sgu_b_s', 'delta_w_branch_a', 'delta_w_branch_b', 'delta_w_gate', 'delta_b_gate', 'delta_w_out', 'delta_ffn2_norm', 'delta_ffn2_w_gate', 'delta_ffn2_w_up', 'delta_ffn2_w_down', 'delta_final_norm', 'new_m_ffn1_norm', 'new_m_ffn1_w_gate', 'new_m_ffn1_w_up', 'new_m_ffn1_w_down', 'new_m_mix_norm', 'new_m_w_in', 'new_m_s5_a_re', 'new_m_s5_a_im', 'new_m_s5_log_dt', 'new_m_s5_b_re', 'new_m_s5_b_im', 'new_m_s5_c_re', 'new_m_s5_c_im', 'new_m_s5_d', 'new_m_s5_w_glu', 'new_m_s5_b_glu', 'new_m_sgu_ln_g', 'new_m_sgu_ln_b', 'new_m_sgu_w_s', 'new_m_sgu_b_s', 'new_m_w_branch_a', 'new_m_w_branch_b', 'new_m_w_gate', 'new_m_b_gate', 'new_m_w_out', 'new_m_ffn2_norm', 'new_m_ffn2_w_gate', 'new_m_ffn2_w_up', 'new_m_ffn2_w_down', 'new_m_final_norm', 'new_v_ffn1_norm', 'new_v_ffn1_w_gate', 'new_v_ffn1_w_up', 'new_v_ffn1_w_down', 'new_v_mix_norm', 'new_v_w_in', 'new_v_s5_a_re', 'new_v_s5_a_im', 'new_v_s5_log_dt', 'new_v_s5_b_re', 'new_v_s5_b_im', 'new_v_s5_c_re', 'new_v_s5_c_im', 'new_v_s5_d', 'new_v_s5_w_glu', 'new_v_s5_b_glu', 'new_v_sgu_ln_g', 'new_v_sgu_ln_b', 'new_v_sgu_w_s', 'new_v_sgu_b_s', 'new_v_w_branch_a', 'new_v_w_branch_b', 'new_v_w_gate', 'new_v_b_gate', 'new_v_w_out', 'new_v_ffn2_norm', 'new_v_ffn2_w_gate', 'new_v_ffn2_w_up', 'new_v_ffn2_w_down', 'new_v_final_norm']
TWIN_LEAF_KINDS = {'loss': 'loss', 'grad_x': 'grad_x', 'grad_ffn1_norm': 'grad_w', 'grad_ffn1_w_gate': 'grad_w', 'grad_ffn1_w_up': 'grad_w', 'grad_ffn1_w_down': 'grad_w', 'grad_mix_norm': 'grad_w', 'grad_w_in': 'grad_w', 'grad_s5_a_re': 'grad_w', 'grad_s5_a_im': 'grad_w', 'grad_s5_log_dt': 'grad_w', 'grad_s5_b_re': 'grad_w', 'grad_s5_b_im': 'grad_w', 'grad_s5_c_re': 'grad_w', 'grad_s5_c_im': 'grad_w', 'grad_s5_d': 'grad_w', 'grad_s5_w_glu': 'grad_w', 'grad_s5_b_glu': 'grad_w', 'grad_sgu_ln_g': 'grad_w', 'grad_sgu_ln_b': 'grad_w', 'grad_sgu_w_s': 'grad_w', 'grad_sgu_b_s': 'grad_w', 'grad_w_branch_a': 'grad_w', 'grad_w_branch_b': 'grad_w', 'grad_w_gate': 'grad_w', 'grad_b_gate': 'grad_w', 'grad_w_out': 'grad_w', 'grad_ffn2_norm': 'grad_w', 'grad_ffn2_w_gate': 'grad_w', 'grad_ffn2_w_up': 'grad_w', 'grad_ffn2_w_down': 'grad_w', 'grad_final_norm': 'grad_w', 'delta_ffn1_norm': 'delta_w', 'delta_ffn1_w_gate': 'delta_w', 'delta_ffn1_w_up': 'delta_w', 'delta_ffn1_w_down': 'delta_w', 'delta_mix_norm': 'delta_w', 'delta_w_in': 'delta_w', 'delta_s5_a_re': 'delta_w', 'delta_s5_a_im': 'delta_w', 'delta_s5_log_dt': 'delta_w', 'delta_s5_b_re': 'delta_w', 'delta_s5_b_im': 'delta_w', 'delta_s5_c_re': 'delta_w', 'delta_s5_c_im': 'delta_w', 'delta_s5_d': 'delta_w', 'delta_s5_w_glu': 'delta_w', 'delta_s5_b_glu': 'delta_w', 'delta_sgu_ln_g': 'delta_w', 'delta_sgu_ln_b': 'delta_w', 'delta_sgu_w_s': 'delta_w', 'delta_sgu_b_s': 'delta_w', 'delta_w_branch_a': 'delta_w', 'delta_w_branch_b': 'delta_w', 'delta_w_gate': 'delta_w', 'delta_b_gate': 'delta_w', 'delta_w_out': 'delta_w', 'delta_ffn2_norm': 'delta_w', 'delta_ffn2_w_gate': 'delta_w', 'delta_ffn2_w_up': 'delta_w', 'delta_ffn2_w_down': 'delta_w', 'delta_final_norm': 'delta_w', 'new_m_ffn1_norm': 'new_m', 'new_m_ffn1_w_gate': 'new_m', 'new_m_ffn1_w_up': 'new_m', 'new_m_ffn1_w_down': 'new_m', 'new_m_mix_norm': 'new_m', 'new_m_w_in': 'new_m', 'new_m_s5_a_re': 'new_m', 'new_m_s5_a_im': 'new_m', 'new_m_s5_log_dt': 'new_m', 'new_m_s5_b_re': 'new_m', 'new_m_s5_b_im': 'new_m', 'new_m_s5_c_re': 'new_m', 'new_m_s5_c_im': 'new_m', 'new_m_s5_d': 'new_m', 'new_m_s5_w_glu': 'new_m', 'new_m_s5_b_glu': 'new_m', 'new_m_sgu_ln_g': 'new_m', 'new_m_sgu_ln_b': 'new_m', 'new_m_sgu_w_s': 'new_m', 'new_m_sgu_b_s': 'new_m', 'new_m_w_branch_a': 'new_m', 'new_m_w_branch_b': 'new_m', 'new_m_w_gate': 'new_m', 'new_m_b_gate': 'new_m', 'new_m_w_out': 'new_m', 'new_m_ffn2_norm': 'new_m', 'new_m_ffn2_w_gate': 'new_m', 'new_m_ffn2_w_up': 'new_m', 'new_m_ffn2_w_down': 'new_m', 'new_m_final_norm': 'new_m', 'new_v_ffn1_norm': 'new_v', 'new_v_ffn1_w_gate': 'new_v', 'new_v_ffn1_w_up': 'new_v', 'new_v_ffn1_w_down': 'new_v', 'new_v_mix_norm': 'new_v', 'new_v_w_in': 'new_v', 'new_v_s5_a_re': 'new_v', 'new_v_s5_a_im': 'new_v', 'new_v_s5_log_dt': 'new_v', 'new_v_s5_b_re': 'new_v', 'new_v_s5_b_im': 'new_v', 'new_v_s5_c_re': 'new_v', 'new_v_s5_c_im': 'new_v', 'new_v_s5_d': 'new_v', 'new_v_s5_w_glu': 'new_v', 'new_v_s5_b_glu': 'new_v', 'new_v_sgu_ln_g': 'new_v', 'new_v_sgu_ln_b': 'new_v', 'new_v_sgu_w_s': 'new_v', 'new_v_sgu_b_s': 'new_v', 'new_v_w_branch_a': 'new_v', 'new_v_w_branch_b': 'new_v', 'new_v_w_gate': 'new_v', 'new_v_b_gate': 'new_v', 'new_v_w_out': 'new_v', 'new_v_ffn2_norm': 'new_v', 'new_v_ffn2_w_gate': 'new_v', 'new_v_ffn2_w_up': 'new_v', 'new_v_ffn2_w_down': 'new_v', 'new_v_final_norm': 'new_v'}


def _forward(args):
    return _fwd_reference(*[args[k] for k in FWD_PARAMS])


def _output_shape():
    def fwd():
        inp = _fwd_setup_inputs(0)
        return _fwd_reference(*[inp[k] for k in FWD_PARAMS])
    out = _jax.eval_shape(fwd)
    return out.shape, out.dtype

N_MICROBATCH = 1
ADAM_LR = 0.001
ADAM_B1 = 0.9
ADAM_B2 = 0.999
ADAM_EPS = 1e-08
ADAM_WD = 0.01
ADAM_STEP = 10
PER_EXAMPLE_BATCH_AXIS = {'x': 0, 'loss_target': 0}
SHARED_INPUTS = []
_WEIGHT_DTYPES = {'ffn1_norm': _jnp.float32, 'ffn1_w_gate': _jnp.float32, 'ffn1_w_up': _jnp.float32, 'ffn1_w_down': _jnp.float32, 'mix_norm': _jnp.float32, 'w_in': _jnp.float32, 's5_a_re': _jnp.float32, 's5_a_im': _jnp.float32, 's5_log_dt': _jnp.float32, 's5_b_re': _jnp.float32, 's5_b_im': _jnp.float32, 's5_c_re': _jnp.float32, 's5_c_im': _jnp.float32, 's5_d': _jnp.float32, 's5_w_glu': _jnp.float32, 's5_b_glu': _jnp.float32, 'sgu_ln_g': _jnp.float32, 'sgu_ln_b': _jnp.float32, 'sgu_w_s': _jnp.float32, 'sgu_b_s': _jnp.float32, 'w_branch_a': _jnp.float32, 'w_branch_b': _jnp.float32, 'w_gate': _jnp.float32, 'b_gate': _jnp.float32, 'w_out': _jnp.float32, 'ffn2_norm': _jnp.float32, 'ffn2_w_gate': _jnp.float32, 'ffn2_w_up': _jnp.float32, 'ffn2_w_down': _jnp.float32, 'final_norm': _jnp.float32}
MOMENT_SCALE = {'ffn1_norm': 5.368404e-02, 'ffn1_w_gate': 2.388666e-02, 'ffn1_w_up': 2.312302e-02, 'ffn1_w_down': 3.831869e-02, 'mix_norm': 6.035253e-02, 'w_in': 4.678866e-02, 's5_a_re': 2.254900e-03, 's5_a_im': 2.515343e-03, 's5_log_dt': 1.458841e+00, 's5_b_re': 1.557083e-03, 's5_b_im': 1.552442e-03, 's5_c_re': 2.134635e-03, 's5_c_im': 2.188376e-03, 's5_d': 3.398067e-02, 's5_w_glu': 9.014223e-03, 's5_b_glu': 1.425693e-02, 'sgu_ln_g': 2.806813e-02, 'sgu_ln_b': 3.286940e-02, 'sgu_w_s': 5.000744e-02, 'sgu_b_s': 5.691764e-02, 'w_branch_a': 2.326315e-02, 'w_branch_b': 4.717825e-02, 'w_gate': 1.385728e-02, 'b_gate': 1.401795e-02, 'w_out': 5.228458e-02, 'ffn2_norm': 4.619037e-02, 'ffn2_w_gate': 1.986805e-02, 'ffn2_w_up': 1.923117e-02, 'ffn2_w_down': 3.191077e-02, 'final_norm': 3.196782e+01}


def _to_microbatches(a, axis):
    t = _jnp.moveaxis(a, axis, 0)
    t = t.reshape((N_MICROBATCH, t.shape[0] // N_MICROBATCH) + t.shape[1:])
    return _jnp.moveaxis(t, 1, axis + 1)


def setup_inputs(seed: int = 0) -> dict:
    inp = _fwd_setup_inputs(seed)
    key = _jax.random.fold_in(_jax.random.key(seed), 7919)
    shape, _ = _output_shape()
    out = dict(inp)
    out["loss_target"] = _jax.random.normal(_jax.random.fold_in(key, 0), shape, _jnp.float32)
    for i, name in enumerate(TWIN_WEIGHTS):
        w = inp[name].astype(_jnp.float32)
        if MOMENT_SCALE is None:
            s = _jnp.sqrt(_jnp.mean(_jnp.square(w)) + 1e-30)
        else:
            s = MOMENT_SCALE[name]
        km, kv = _jax.random.split(_jax.random.fold_in(key, i + 1))
        out[name] = w
        out["m_" + name] = s * _jax.random.normal(km, w.shape, _jnp.float32)
        out["v_" + name] = (s * s) * _jax.random.uniform(kv, w.shape, _jnp.float32, 0.5, 1.5)
    if N_MICROBATCH > 1:
        for name, axis in PER_EXAMPLE_BATCH_AXIS.items():
            out[name] = _to_microbatches(out[name], axis)
    return {'x': out['x'], 'ffn1_norm': out['ffn1_norm'], 'ffn1_w_gate': out['ffn1_w_gate'], 'ffn1_w_up': out['ffn1_w_up'], 'ffn1_w_down': out['ffn1_w_down'], 'mix_norm': out['mix_norm'], 'w_in': out['w_in'], 's5_a_re': out['s5_a_re'], 's5_a_im': out['s5_a_im'], 's5_log_dt': out['s5_log_dt'], 's5_b_re': out['s5_b_re'], 's5_b_im': out['s5_b_im'], 's5_c_re': out['s5_c_re'], 's5_c_im': out['s5_c_im'], 's5_d': out['s5_d'], 's5_w_glu': out['s5_w_glu'], 's5_b_glu': out['s5_b_glu'], 'sgu_ln_g': out['sgu_ln_g'], 'sgu_ln_b': out['sgu_ln_b'], 'sgu_w_s': out['sgu_w_s'], 'sgu_b_s': out['sgu_b_s'], 'w_branch_a': out['w_branch_a'], 'w_branch_b': out['w_branch_b'], 'w_gate': out['w_gate'], 'b_gate': out['b_gate'], 'w_out': out['w_out'], 'ffn2_norm': out['ffn2_norm'], 'ffn2_w_gate': out['ffn2_w_gate'], 'ffn2_w_up': out['ffn2_w_up'], 'ffn2_w_down': out['ffn2_w_down'], 'final_norm': out['final_norm'], 'loss_target': out['loss_target'], 'm_ffn1_norm': out['m_ffn1_norm'], 'm_ffn1_w_gate': out['m_ffn1_w_gate'], 'm_ffn1_w_up': out['m_ffn1_w_up'], 'm_ffn1_w_down': out['m_ffn1_w_down'], 'm_mix_norm': out['m_mix_norm'], 'm_w_in': out['m_w_in'], 'm_s5_a_re': out['m_s5_a_re'], 'm_s5_a_im': out['m_s5_a_im'], 'm_s5_log_dt': out['m_s5_log_dt'], 'm_s5_b_re': out['m_s5_b_re'], 'm_s5_b_im': out['m_s5_b_im'], 'm_s5_c_re': out['m_s5_c_re'], 'm_s5_c_im': out['m_s5_c_im'], 'm_s5_d': out['m_s5_d'], 'm_s5_w_glu': out['m_s5_w_glu'], 'm_s5_b_glu': out['m_s5_b_glu'], 'm_sgu_ln_g': out['m_sgu_ln_g'], 'm_sgu_ln_b': out['m_sgu_ln_b'], 'm_sgu_w_s': out['m_sgu_w_s'], 'm_sgu_b_s': out['m_sgu_b_s'], 'm_w_branch_a': out['m_w_branch_a'], 'm_w_branch_b': out['m_w_branch_b'], 'm_w_gate': out['m_w_gate'], 'm_b_gate': out['m_b_gate'], 'm_w_out': out['m_w_out'], 'm_ffn2_norm': out['m_ffn2_norm'], 'm_ffn2_w_gate': out['m_ffn2_w_gate'], 'm_ffn2_w_up': out['m_ffn2_w_up'], 'm_ffn2_w_down': out['m_ffn2_w_down'], 'm_final_norm': out['m_final_norm'], 'v_ffn1_norm': out['v_ffn1_norm'], 'v_ffn1_w_gate': out['v_ffn1_w_gate'], 'v_ffn1_w_up': out['v_ffn1_w_up'], 'v_ffn1_w_down': out['v_ffn1_w_down'], 'v_mix_norm': out['v_mix_norm'], 'v_w_in': out['v_w_in'], 'v_s5_a_re': out['v_s5_a_re'], 'v_s5_a_im': out['v_s5_a_im'], 'v_s5_log_dt': out['v_s5_log_dt'], 'v_s5_b_re': out['v_s5_b_re'], 'v_s5_b_im': out['v_s5_b_im'], 'v_s5_c_re': out['v_s5_c_re'], 'v_s5_c_im': out['v_s5_c_im'], 'v_s5_d': out['v_s5_d'], 'v_s5_w_glu': out['v_s5_w_glu'], 'v_s5_b_glu': out['v_s5_b_glu'], 'v_sgu_ln_g': out['v_sgu_ln_g'], 'v_sgu_ln_b': out['v_sgu_ln_b'], 'v_sgu_w_s': out['v_sgu_w_s'], 'v_sgu_b_s': out['v_sgu_b_s'], 'v_w_branch_a': out['v_w_branch_a'], 'v_w_branch_b': out['v_w_branch_b'], 'v_w_gate': out['v_w_gate'], 'v_b_gate': out['v_b_gate'], 'v_w_out': out['v_w_out'], 'v_ffn2_norm': out['v_ffn2_norm'], 'v_ffn2_w_gate': out['v_ffn2_w_gate'], 'v_ffn2_w_up': out['v_ffn2_w_up'], 'v_ffn2_w_down': out['v_ffn2_w_down'], 'v_final_norm': out['v_final_norm']}


def _loss(weights, diff, rest, loss_target):
    with _jax.named_scope("forward"):
        args = {**rest, TWIN_DIFF_INPUT: diff, **{k: w.astype(_WEIGHT_DTYPES[k]) for k, w in weights.items()}}
        y = _forward(args)
    with _jax.named_scope("loss_head"):
        err = _jnp.square(y.astype(_jnp.float32) - loss_target)
        return 0.5 * _jnp.sum(_jnp.mean(err, axis=-1)) if err.ndim else 0.5 * err


def _adamw(w, g, m, v):
    m = ADAM_B1 * m + (1.0 - ADAM_B1) * g
    v = ADAM_B2 * v + (1.0 - ADAM_B2) * _jnp.square(g)
    m_hat = m / (1.0 - ADAM_B1 ** ADAM_STEP)
    v_hat = v / (1.0 - ADAM_B2 ** ADAM_STEP)
    delta = -ADAM_LR * (m_hat / (_jnp.sqrt(v_hat) + ADAM_EPS) + ADAM_WD * w)
    return delta, m, v


def reference(x, ffn1_norm, ffn1_w_gate, ffn1_w_up, ffn1_w_down, mix_norm, w_in, s5_a_re, s5_a_im, s5_log_dt, s5_b_re, s5_b_im, s5_c_re, s5_c_im, s5_d, s5_w_glu, s5_b_glu, sgu_ln_g, sgu_ln_b, sgu_w_s, sgu_b_s, w_branch_a, w_branch_b, w_gate, b_gate, w_out, ffn2_norm, ffn2_w_gate, ffn2_w_up, ffn2_w_down, final_norm, loss_target, m_ffn1_norm, m_ffn1_w_gate, m_ffn1_w_up, m_ffn1_w_down, m_mix_norm, m_w_in, m_s5_a_re, m_s5_a_im, m_s5_log_dt, m_s5_b_re, m_s5_b_im, m_s5_c_re, m_s5_c_im, m_s5_d, m_s5_w_glu, m_s5_b_glu, m_sgu_ln_g, m_sgu_ln_b, m_sgu_w_s, m_sgu_b_s, m_w_branch_a, m_w_branch_b, m_w_gate, m_b_gate, m_w_out, m_ffn2_norm, m_ffn2_w_gate, m_ffn2_w_up, m_ffn2_w_down, m_final_norm, v_ffn1_norm, v_ffn1_w_gate, v_ffn1_w_up, v_ffn1_w_down, v_mix_norm, v_w_in, v_s5_a_re, v_s5_a_im, v_s5_log_dt, v_s5_b_re, v_s5_b_im, v_s5_c_re, v_s5_c_im, v_s5_d, v_s5_w_glu, v_s5_b_glu, v_sgu_ln_g, v_sgu_ln_b, v_sgu_w_s, v_sgu_b_s, v_w_branch_a, v_w_branch_b, v_w_gate, v_b_gate, v_w_out, v_ffn2_norm, v_ffn2_w_gate, v_ffn2_w_up, v_ffn2_w_down, v_final_norm):
    given = dict(x=x, ffn1_norm=ffn1_norm, ffn1_w_gate=ffn1_w_gate, ffn1_w_up=ffn1_w_up, ffn1_w_down=ffn1_w_down, mix_norm=mix_norm, w_in=w_in, s5_a_re=s5_a_re, s5_a_im=s5_a_im, s5_log_dt=s5_log_dt, s5_b_re=s5_b_re, s5_b_im=s5_b_im, s5_c_re=s5_c_re, s5_c_im=s5_c_im, s5_d=s5_d, s5_w_glu=s5_w_glu, s5_b_glu=s5_b_glu, sgu_ln_g=sgu_ln_g, sgu_ln_b=sgu_ln_b, sgu_w_s=sgu_w_s, sgu_b_s=sgu_b_s, w_branch_a=w_branch_a, w_branch_b=w_branch_b, w_gate=w_gate, b_gate=b_gate, w_out=w_out, ffn2_norm=ffn2_norm, ffn2_w_gate=ffn2_w_gate, ffn2_w_up=ffn2_w_up, ffn2_w_down=ffn2_w_down, final_norm=final_norm, loss_target=loss_target, m_ffn1_norm=m_ffn1_norm, m_ffn1_w_gate=m_ffn1_w_gate, m_ffn1_w_up=m_ffn1_w_up, m_ffn1_w_down=m_ffn1_w_down, m_mix_norm=m_mix_norm, m_w_in=m_w_in, m_s5_a_re=m_s5_a_re, m_s5_a_im=m_s5_a_im, m_s5_log_dt=m_s5_log_dt, m_s5_b_re=m_s5_b_re, m_s5_b_im=m_s5_b_im, m_s5_c_re=m_s5_c_re, m_s5_c_im=m_s5_c_im, m_s5_d=m_s5_d, m_s5_w_glu=m_s5_w_glu, m_s5_b_glu=m_s5_b_glu, m_sgu_ln_g=m_sgu_ln_g, m_sgu_ln_b=m_sgu_ln_b, m_sgu_w_s=m_sgu_w_s, m_sgu_b_s=m_sgu_b_s, m_w_branch_a=m_w_branch_a, m_w_branch_b=m_w_branch_b, m_w_gate=m_w_gate, m_b_gate=m_b_gate, m_w_out=m_w_out, m_ffn2_norm=m_ffn2_norm, m_ffn2_w_gate=m_ffn2_w_gate, m_ffn2_w_up=m_ffn2_w_up, m_ffn2_w_down=m_ffn2_w_down, m_final_norm=m_final_norm, v_ffn1_norm=v_ffn1_norm, v_ffn1_w_gate=v_ffn1_w_gate, v_ffn1_w_up=v_ffn1_w_up, v_ffn1_w_down=v_ffn1_w_down, v_mix_norm=v_mix_norm, v_w_in=v_w_in, v_s5_a_re=v_s5_a_re, v_s5_a_im=v_s5_a_im, v_s5_log_dt=v_s5_log_dt, v_s5_b_re=v_s5_b_re, v_s5_b_im=v_s5_b_im, v_s5_c_re=v_s5_c_re, v_s5_c_im=v_s5_c_im, v_s5_d=v_s5_d, v_s5_w_glu=v_s5_w_glu, v_s5_b_glu=v_s5_b_glu, v_sgu_ln_g=v_sgu_ln_g, v_sgu_ln_b=v_sgu_ln_b, v_sgu_w_s=v_sgu_w_s, v_sgu_b_s=v_sgu_b_s, v_w_branch_a=v_w_branch_a, v_w_branch_b=v_w_branch_b, v_w_gate=v_w_gate, v_b_gate=v_b_gate, v_w_out=v_w_out, v_ffn2_norm=v_ffn2_norm, v_ffn2_w_gate=v_ffn2_w_gate, v_ffn2_w_up=v_ffn2_w_up, v_ffn2_w_down=v_ffn2_w_down, v_final_norm=v_final_norm)
    weights = {n: given[n] for n in TWIN_WEIGHTS}
    shared = {n: given[n] for n in SHARED_INPUTS}
    per_example = {n: given[n] for n in ['x']}
    grad_fn = _jax.value_and_grad(_loss, argnums=(0, 1))

    def one_microbatch(ex, loss_target):
        ex = dict(ex)
        diff = ex.pop(TWIN_DIFF_INPUT)
        return grad_fn(weights, diff, {**shared, **ex}, loss_target)

    if N_MICROBATCH == 1:
        loss, (grad_w, grad_x) = one_microbatch(per_example, given["loss_target"])
    else:
        def body(carry, xs):
            loss_sum, grad_sum = carry
            l_k, (gw_k, gx_k) = one_microbatch(xs[0], xs[1])
            with _jax.named_scope("update"):
                return (loss_sum + l_k, _jax.tree.map(_jnp.add, grad_sum, gw_k)), gx_k

        init = (_jnp.zeros((), _jnp.float32), _jax.tree.map(_jnp.zeros_like, weights))
        (loss, grad_w), grad_x = _jax.lax.scan(body, init, (per_example, given["loss_target"]))
    with _jax.named_scope("update"):
        delta_w, new_m, new_v = {}, {}, {}
        for n in TWIN_WEIGHTS:
            delta_w[n], new_m[n], new_v[n] = _adamw(weights[n], grad_w[n], given["m_" + n], given["v_" + n])
    return (loss, grad_x, *[grad_w[n] for n in TWIN_WEIGHTS], *[delta_w[n] for n in TWIN_WEIGHTS],
            *[new_m[n] for n in TWIN_WEIGHTS], *[new_v[n] for n in TWIN_WEIGHTS])
```

```python
import functools
import math

import jax
import jax.numpy as jnp
from jax import lax
from jax.experimental import pallas as pl
from jax.experimental.pallas import tpu as pltpu

F32 = jnp.float32
BF16 = jnp.bfloat16
NDEV = 8
NORM_EPS = 1e-6
D_MODEL = 2048
D_FF = 5632
FF_SHARD = D_FF // NDEV
S5_WIDTH = 1024
S5_GROUPS = 64
S5_GROUP_WIDTH = 16
S5_STATE = 64
S5_NS = S5_GROUPS * S5_STATE
SGU_WIDTH = 1024
SGU_HEADS = 8
MLP_CHUNK = 128
CHUNK = 64
ADAM_LR, ADAM_B1, ADAM_B2, ADAM_EPS, ADAM_WD, ADAM_STEP = 0.001, 0.9, 0.999, 1e-08, 0.01, 10
S5_TC = 256
S5_SEG = S5_TC // 8
S5_LG = 512
VMEM_BIG = 56 * 1024 * 1024

MESH = pl.DeviceIdType.MESH
SDS = jax.ShapeDtypeStruct
Block = pl.BlockSpec
ANY = pl.BlockSpec(memory_space=pl.ANY)


def _cparams(sem=None, vmem=None):
    return pltpu.CompilerParams(dimension_semantics=sem, vmem_limit_bytes=vmem)


def _const(shape):
    nd = len(shape)
    return pl.BlockSpec(shape, lambda i: (0,) * nd, pipeline_mode=pl.Buffered(1))


def _sigmoid(x):
    return 1.0 / (1.0 + jnp.exp(-x))


_GELU_C = math.sqrt(2.0 / math.pi)


def _gelu(x):
    return 0.5 * x * (1.0 + jnp.tanh(_GELU_C * (x + 0.044715 * x * x * x)))


def _gelu_grad(x):
    t = jnp.tanh(_GELU_C * (x + 0.044715 * x * x * x))
    return 0.5 * (1.0 + t) + 0.5 * x * (1.0 - t * t) * _GELU_C * (1.0 + 3.0 * 0.044715 * x * x)


NN = (((1,), (0,)), ((), ()))
NT = (((1,), (1,)), ((), ()))
TN = (((0,), (0,)), ((), ()))


def _dot(a, b, dims=NN):
    return lax.dot_general(a, b, dims, preferred_element_type=F32)


def _matmul(name, a, b, extras, *, grid, a_spec, b_spec, extra_specs, out_shapes, out_specs, acc_shape,
            dims, nb, epilogue, vmem=VMEM_BIG):
    nk = grid[2]
    ne, no = len(extras), len(out_shapes)

    def body(*refs):
        a_ref, b_ref = refs[0], refs[1]
        ex = refs[2:2 + ne]
        outs = refs[2 + ne:2 + ne + no]
        av = a_ref[...]
        prods = [_dot(av, b_ref[q] if nb else b_ref[...], dims) for q in range(nb or 1)]
        if nk == 1:
            epilogue(prods, ex, outs)
            return
        acc_ref = refs[2 + ne + no]
        k = pl.program_id(2)

        @pl.when(k == 0)
        def _():
            for q, p in enumerate(prods):
                acc_ref[q] = p

        @pl.when(k > 0)
        def _():
            for q, p in enumerate(prods):
                acc_ref[q] += p

        @pl.when(k == nk - 1)
        def _():
            epilogue([acc_ref[q] for q in range(nb or 1)], ex, outs)

    scratch = [] if nk == 1 else [pltpu.VMEM(((nb or 1),) + tuple(acc_shape), F32)]
    res = pl.pallas_call(
        body, name=name, grid=grid,
        in_specs=[a_spec, b_spec] + list(extra_specs),
        out_specs=list(out_specs), out_shape=list(out_shapes), scratch_shapes=scratch,
        compiler_params=_cparams(("parallel", "parallel", "arbitrary"), vmem),
    )(a, b, *extras)
    return res


def _store(dtype_outs=None):
    def ep(accs, ex, outs):
        outs[0][...] = accs[0].astype(outs[0].dtype)
    return ep


def _tile(n, t):
    t = min(n, t)
    assert n % t == 0, (n, t)
    return t


def _ffn_up(name, h, wgu, base):
    T, D = h.shape
    tm = _tile(T, 1024)

    def ep(accs, ex, outs):
        a, b = accs
        outs[0][0] = a.astype(BF16)
        outs[0][1] = b.astype(BF16)
        outs[1][...] = (a * _sigmoid(a) * b).astype(BF16)

    return _matmul(
        name, h, wgu, (), grid=(NDEV, T // tm, 1),
        a_spec=Block((tm, D), lambda j, i, k: (i, 0)),
        b_spec=Block((None, 2, D, FF_SHARD), lambda j, i, k: (j, base // 2, 0, 0)),
        extra_specs=(),
        out_shapes=[SDS((NDEV, 2, T, FF_SHARD), BF16), SDS((NDEV, T, FF_SHARD), BF16)],
        out_specs=[Block((None, 2, tm, FF_SHARD), lambda j, i, k: (j, 0, i, 0)),
                   Block((None, tm, FF_SHARD), lambda j, i, k: (j, i, 0))],
        acc_shape=(tm, FF_SHARD), dims=NN, nb=2, epilogue=ep)


def _ffn_down(name, f, wd, which, xres):
    _, T, _ = f.shape
    tm, tn = _tile(T, 1024), 1024

    def ep(accs, ex, outs):
        outs[0][...] = ex[0][...] + 0.5 * accs[0]

    return _matmul(
        name, f, wd, (xres,), grid=(T // tm, D_MODEL // tn, NDEV),
        a_spec=Block((None, tm, FF_SHARD), lambda i, j, k: (k, i, 0)),
        b_spec=Block((None, None, FF_SHARD, tn), lambda i, j, k: (k, which, 0, j)),
        extra_specs=[Block((tm, tn), lambda i, j, k: (i, j))],
        out_shapes=[SDS((T, D_MODEL), F32)],
        out_specs=[Block((tm, tn), lambda i, j, k: (i, j))],
        acc_shape=(tm, tn), dims=NN, nb=None, epilogue=ep)[0]


def _ffn_down_bwd_act(name, dyb, wd, which, ab):
    T, D = dyb.shape
    tm = _tile(T, 1024)

    def ep(accs, ex, outs):
        df = 0.5 * accs[0]
        a = ex[0][0].astype(F32)
        b = ex[0][1].astype(F32)
        s = _sigmoid(a)
        outs[0][0] = (df * b * s * (1.0 + a * (1.0 - s))).astype(BF16)
        outs[0][1] = (df * a * s).astype(BF16)

    return _matmul(
        name, dyb, wd, (ab,), grid=(NDEV, T // tm, 1),
        a_spec=Block((tm, D), lambda j, i, k: (i, 0)),
        b_spec=Block((None, None, FF_SHARD, D), lambda j, i, k: (j, which, 0, 0)),
        extra_specs=[Block((None, 2, tm, FF_SHARD), lambda j, i, k: (j, 0, i, 0))],
        out_shapes=[SDS((NDEV, 2, T, FF_SHARD), BF16)],
        out_specs=[Block((None, 2, tm, FF_SHARD), lambda j, i, k: (j, 0, i, 0))],
        acc_shape=(tm, FF_SHARD), dims=NT, nb=None, epilogue=ep)[0]


def _ffn_down_bwd_w(name, f, dyb):
    _, T, _ = f.shape
    tt, tn = _tile(T, 1024), 1024

    def ep(accs, ex, outs):
        outs[0][...] = (0.5 * accs[0]).astype(BF16)

    return _matmul(
        name, f, dyb, (), grid=(NDEV, D_MODEL // tn, T // tt),
        a_spec=Block((None, tt, FF_SHARD), lambda j, n, k: (j, k, 0)),
        b_spec=Block((tt, tn), lambda j, n, k: (k, n)),
        extra_specs=(),
        out_shapes=[SDS((NDEV, FF_SHARD, D_MODEL), BF16)],
        out_specs=[Block((None, FF_SHARD, tn), lambda j, n, k: (j, 0, n))],
        acc_shape=(FF_SHARD, tn), dims=TN, nb=None, epilogue=ep)[0]


def _ffn_up_bwd_h(name, dab, wgu, base):
    _, _, T, _ = dab.shape
    tm = _tile(T, 1024)
    return _matmul(
        name, dab, wgu, (), grid=(T // tm, 1, 2 * NDEV),
        a_spec=Block((None, None, tm, FF_SHARD), lambda i, j, k: (k // 2, k % 2, i, 0)),
        b_spec=Block((None, None, D_MODEL, FF_SHARD), lambda i, j, k: (k // 2, base + k % 2, 0, 0)),
        extra_specs=(),
        out_shapes=[SDS((T, D_MODEL), BF16)],
        out_specs=[Block((tm, D_MODEL), lambda i, j, k: (i, 0))],
        acc_shape=(tm, D_MODEL), dims=NT, nb=None, epilogue=_store())[0]


def _ffn_up_bwd_w(name, h, dab):
    T, D = h.shape
    tt, tr = _tile(T, 1024), 1024

    def ep(accs, ex, outs):
        outs[0][0] = accs[0].astype(BF16)
        outs[0][1] = accs[1].astype(BF16)

    return _matmul(
        name, h, dab, (), grid=(NDEV, D // tr, T // tt),
        a_spec=Block((tt, tr), lambda j, n, k: (k, n)),
        b_spec=Block((None, 2, tt, FF_SHARD), lambda j, n, k: (j, 0, k, 0)),
        extra_specs=(),
        out_shapes=[SDS((NDEV, 2, D, FF_SHARD), BF16)],
        out_specs=[Block((None, 2, tr, FF_SHARD), lambda j, n, k: (j, 0, n, 0))],
        acc_shape=(tr, FF_SHARD), dims=TN, nb=2, epilogue=ep)[0]


def _col_fwd(name, a, w, out_dtype=BF16):
    T, K = a.shape
    ns = w.shape[2]
    tm = _tile(T, 1024)
    return _matmul(
        name, a, w, (), grid=(NDEV, T // tm, 1),
        a_spec=Block((tm, K), lambda j, i, k: (i, 0)),
        b_spec=Block((None, K, ns), lambda j, i, k: (j, 0, 0)),
        extra_specs=(),
        out_shapes=[SDS((T, NDEV * ns), out_dtype)],
        out_specs=[Block((tm, ns), lambda j, i, k: (i, j))],
        acc_shape=(tm, ns), dims=NN, nb=None, epilogue=_store())[0]


def _col_bwd_a(name, dy, w, add=None):
    T = dy.shape[0]
    _, K, ns = w.shape
    tm = _tile(T, 1024)

    def ep(accs, ex, outs):
        r = accs[0]
        if add is not None:
            r = r + ex[0][...].astype(F32)
        outs[0][...] = r.astype(BF16)

    extras = () if add is None else (add,)
    return _matmul(
        name, dy, w, extras, grid=(T // tm, 1, NDEV),
        a_spec=Block((tm, ns), lambda i, j, k: (i, k)),
        b_spec=Block((None, K, ns), lambda i, j, k: (k, 0, 0)),
        extra_specs=[Block((tm, K), lambda i, j, k: (i, 0))] * len(extras),
        out_shapes=[SDS((T, K), BF16)],
        out_specs=[Block((tm, K), lambda i, j, k: (i, 0))],
        acc_shape=(tm, K), dims=NT, nb=None, epilogue=ep)[0]


def _col_bwd_w(name, a, dy, ns):
    T, K = a.shape
    tt, tr = _tile(T, 1024), _tile(K, 1024)
    return _matmul(
        name, a, dy, (), grid=(NDEV, K // tr, T // tt),
        a_spec=Block((tt, tr), lambda j, n, k: (k, n)),
        b_spec=Block((tt, ns), lambda j, n, k: (k, j)),
        extra_specs=(),
        out_shapes=[SDS((NDEV, K, ns), BF16)],
        out_specs=[Block((None, tr, ns), lambda j, n, k: (j, n, 0))],
        acc_shape=(tr, ns), dims=TN, nb=None, epilogue=_store())[0]


def _gate_fwd(name, h, w, bias):
    T, K = h.shape
    ns = w.shape[2]
    per = D_MODEL // ns
    tm = _tile(T, 1024)

    def ep(accs, ex, outs):
        outs[0][...] = (accs[0] + ex[0][...]).astype(BF16)

    return _matmul(
        name, h, w, (bias,), grid=(NDEV, T // tm, 1),
        a_spec=Block((tm, K), lambda j, i, k: (i, 0)),
        b_spec=Block((None, K, ns), lambda j, i, k: (j, 0, 0)),
        extra_specs=[Block((None, 1, ns), lambda j, i, k: (j // per, 0, j % per))],
        out_shapes=[SDS((2, T, D_MODEL), BF16)],
        out_specs=[Block((None, tm, ns), lambda j, i, k: (j // per, i, j % per))],
        acc_shape=(tm, ns), dims=NN, nb=None, epilogue=ep)[0]


def _gate_bwd_a(name, dgl, w):
    _, T, _ = dgl.shape
    _, K, ns = w.shape
    per = D_MODEL // ns
    tm = _tile(T, 1024)
    return _matmul(
        name, dgl, w, (), grid=(T // tm, 1, NDEV),
        a_spec=Block((None, tm, ns), lambda i, j, k: (k // per, i, k % per)),
        b_spec=Block((None, K, ns), lambda i, j, k: (k, 0, 0)),
        extra_specs=(),
        out_shapes=[SDS((T, K), BF16)],
        out_specs=[Block((tm, K), lambda i, j, k: (i, 0))],
        acc_shape=(tm, K), dims=NT, nb=None, epilogue=_store())[0]


def _gate_bwd_w(name, h, dgl, ns):
    T, K = h.shape
    per = D_MODEL // ns
    tt, tr = _tile(T, 1024), 1024
    return _matmul(
        name, h, dgl, (), grid=(NDEV, K // tr, T // tt),
        a_spec=Block((tt, tr), lambda j, n, k: (k, n)),
        b_spec=Block((None, tt, ns), lambda j, n, k: (j // per, k, j % per)),
        extra_specs=(),
        out_shapes=[SDS((NDEV, K, ns), BF16)],
        out_specs=[Block((None, tr, ns), lambda j, n, k: (j, n, 0))],
        acc_shape=(tr, ns), dims=TN, nb=None, epilogue=_store())[0]


def _plain_fwd_res(name, a, w, xres):
    T, K = a.shape
    N = w.shape[1]
    tm, tn = _tile(T, 1024), _tile(N, 1024)

    def ep(accs, ex, outs):
        outs[0][...] = ex[0][...] + accs[0]

    return _matmul(
        name, a, w, (xres,), grid=(T // tm, N // tn, 1),
        a_spec=Block((tm, K), lambda i, j, k: (i, 0)),
        b_spec=Block((K, tn), lambda i, j, k: (0, j)),
        extra_specs=[Block((tm, tn), lambda i, j, k: (i, j))],
        out_shapes=[SDS((T, N), F32)],
        out_specs=[Block((tm, tn), lambda i, j, k: (i, j))],
        acc_shape=(tm, tn), dims=NN, nb=None, epilogue=ep)[0]


def _plain_bwd_a(name, dy, w):
    T, N = dy.shape
    K = w.shape[0]
    tm, tn = _tile(T, 1024), _tile(K, 1024)
    return _matmul(
        name, dy, w, (), grid=(T // tm, K // tn, 1),
        a_spec=Block((tm, N), lambda i, j, k: (i, 0)),
        b_spec=Block((tn, N), lambda i, j, k: (j, 0)),
        extra_specs=(),
        out_shapes=[SDS((T, K), BF16)],
        out_specs=[Block((tm, tn), lambda i, j, k: (i, j))],
        acc_shape=(tm, tn), dims=NT, nb=None, epilogue=_store())[0]


def _plain_bwd_w(name, a, dy):
    T, K = a.shape
    N = dy.shape[1]
    tt, tr, tn = _tile(T, 1024), _tile(K, 1024), _tile(N, 1024)
    return _matmul(
        name, a, dy, (), grid=(K // tr, N // tn, T // tt),
        a_spec=Block((tt, tr), lambda m, n, k: (k, m)),
        b_spec=Block((tt, tn), lambda m, n, k: (k, n)),
        extra_specs=(),
        out_shapes=[SDS((K, N), BF16)],
        out_specs=[Block((tr, tn), lambda m, n, k: (m, n))],
        acc_shape=(tr, tn), dims=TN, nb=None, epilogue=_store())[0]


def _rms_fwd(name, x, g):
    T, D = x.shape
    tm = _tile(T, 512)

    def body(x_ref, g_ref, h_ref):
        xv = x_ref[...]
        r = lax.rsqrt(jnp.mean(xv * xv, axis=-1, keepdims=True) + NORM_EPS)
        h_ref[...] = (xv * r * g_ref[...]).astype(BF16)

    return pl.pallas_call(
        body, name=name, grid=(T // tm,),
        in_specs=[Block((tm, D), lambda i: (i, 0)), Block((1, D), lambda i: (0, 0))],
        out_specs=Block((tm, D), lambda i: (i, 0)), out_shape=SDS((T, D), BF16),
        compiler_params=_cparams(("arbitrary",), VMEM_BIG))(x, g)


def _rms_bwd(name, dh, x, g, dxin):
    T, D = x.shape
    tm = _tile(T, 512)

    def body(dh_ref, x_ref, g_ref, dxin_ref, dx_ref, dxb_ref, dg_ref):
        i = pl.program_id(0)
        xv = x_ref[...]
        dh = dh_ref[...].astype(F32)
        r = lax.rsqrt(jnp.mean(xv * xv, axis=-1, keepdims=True) + NORM_EPS)
        xh = xv * r
        gd = dh * g_ref[...]
        dx = dxin_ref[...] + r * (gd - xh * jnp.mean(gd * xh, axis=-1, keepdims=True))
        dx_ref[...] = dx
        dxb_ref[...] = dx.astype(BF16)
        dgp = jnp.sum(dh * xh, axis=0, keepdims=True)

        @pl.when(i == 0)
        def _():
            dg_ref[...] = dgp

        @pl.when(i > 0)
        def _():
            dg_ref[...] += dgp

    row = Block((tm, D), lambda i: (i, 0))
    vec = Block((1, D), lambda i: (0, 0))
    return pl.pallas_call(
        body, name=name, grid=(T // tm,),
        in_specs=[row, row, vec, row], out_specs=[row, row, vec],
        out_shape=[SDS((T, D), F32), SDS((T, D), BF16), SDS((1, D), F32)],
        compiler_params=_cparams(("arbitrary",), VMEM_BIG))(dh, x, g, dxin)


def _loss_head(name, x, g, tgt):
    T, D = x.shape
    tm = _tile(T, 512)

    def body(x_ref, g_ref, t_ref, loss_ref, dx_ref, dxb_ref, dg_ref):
        i = pl.program_id(0)
        xv = x_ref[...]
        gv = g_ref[...]
        r = lax.rsqrt(jnp.mean(xv * xv, axis=-1, keepdims=True) + NORM_EPS)
        xh = xv * r
        err = xh * gv - t_ref[...]
        lp = 0.5 * jnp.sum(jnp.mean(err * err, axis=-1, keepdims=True), axis=0, keepdims=True)
        dout = err * (1.0 / D)
        gd = dout * gv
        dx = r * (gd - xh * jnp.mean(gd * xh, axis=-1, keepdims=True))
        dx_ref[...] = dx
        dxb_ref[...] = dx.astype(BF16)
        dgp = jnp.sum(dout * xh, axis=0, keepdims=True)
        lpb = jnp.broadcast_to(lp, (1, 128))

        @pl.when(i == 0)
        def _():
            dg_ref[...] = dgp
            loss_ref[...] = lpb

        @pl.when(i > 0)
        def _():
            dg_ref[...] += dgp
            loss_ref[...] += lpb

    row = Block((tm, D), lambda i: (i, 0))
    vec = Block((1, D), lambda i: (0, 0))
    return pl.pallas_call(
        body, name=name, grid=(T // tm,),
        in_specs=[row, vec, row], out_specs=[Block((1, 128), lambda i: (0, 0)), row, row, vec],
        out_shape=[SDS((1, 128), F32), SDS((T, D), F32), SDS((T, D), BF16), SDS((1, D), F32)],
        compiler_params=_cparams(("arbitrary",), VMEM_BIG))(x, g, tgt)


def _merge_fwd(name, pa, pb, gl):
    T, D = pa.shape
    tm = _tile(T, 512)

    def body(pa_ref, pb_ref, gl_ref, o_ref):
        ga = _sigmoid(gl_ref[0].astype(F32))
        gb = _sigmoid(gl_ref[1].astype(F32))
        o_ref[...] = (ga * pa_ref[...].astype(F32) + gb * pb_ref[...].astype(F32)).astype(BF16)

    row = Block((tm, D), lambda i: (i, 0))
    return pl.pallas_call(
        body, name=name, grid=(T // tm,),
        in_specs=[row, row, Block((2, tm, D), lambda i: (0, i, 0))], out_specs=row,
        out_shape=SDS((T, D), BF16), compiler_params=_cparams(("arbitrary",), VMEM_BIG))(pa, pb, gl)


def _merge_bwd(name, dm, pa, pb, gl):
    T, D = pa.shape
    tm = _tile(T, 512)

    def body(dm_ref, pa_ref, pb_ref, gl_ref, dpa_ref, dpb_ref, dgl_ref, db_ref):
        i = pl.program_id(0)
        dmv = dm_ref[...].astype(F32)
        ga = _sigmoid(gl_ref[0].astype(F32))
        gb = _sigmoid(gl_ref[1].astype(F32))
        dpa_ref[...] = (dmv * ga).astype(BF16)
        dpb_ref[...] = (dmv * gb).astype(BF16)
        dga = dmv * pa_ref[...].astype(F32) * ga * (1.0 - ga)
        dgb = dmv * pb_ref[...].astype(F32) * gb * (1.0 - gb)
        dgl_ref[0] = dga.astype(BF16)
        dgl_ref[1] = dgb.astype(BF16)
        sa = jnp.sum(dga, axis=0, keepdims=True)
        sb = jnp.sum(dgb, axis=0, keepdims=True)

        @pl.when(i == 0)
        def _():
            db_ref[0] = sa
            db_ref[1] = sb

        @pl.when(i > 0)
        def _():
            db_ref[0] += sa
            db_ref[1] += sb

    row = Block((tm, D), lambda i: (i, 0))
    two = Block((2, tm, D), lambda i: (0, i, 0))
    return pl.pallas_call(
        body, name=name, grid=(T // tm,),
        in_specs=[row, row, row, two], out_specs=[row, row, two, Block((2, 1, D), lambda i: (0, 0, 0))],
        out_shape=[SDS((T, D), BF16), SDS((T, D), BF16), SDS((2, T, D), BF16), SDS((2, 1, D), F32)],
        compiler_params=_cparams(("arbitrary",), VMEM_BIG))(dm, pa, pb, gl)


def _sgu_core(ur, vr, lng, lnb, ws_ref, bs_ref):
    tm = ur.shape[0]
    gu = _gelu(ur)
    gv = _gelu(vr)
    mu = jnp.mean(gv, axis=-1, keepdims=True)
    cen = gv - mu
    rstd = lax.rsqrt(jnp.mean(cen * cen, axis=-1, keepdims=True) + NORM_EPS)
    xhat = cen * rstd
    vn = (xhat * lng + lnb).astype(BF16)
    rows = []
    for n in range(tm // MLP_CHUNK):
        cols = []
        for h in range(SGU_HEADS):
            blk = vn[n * MLP_CHUNK:(n + 1) * MLP_CHUNK, h * 128:(h + 1) * 128]
            cols.append(_dot(ws_ref[h], blk) + bs_ref[h])
        rows.append(jnp.concatenate(cols, axis=1))
    mixed = jnp.concatenate(rows, axis=0) if len(rows) > 1 else rows[0]
    return gu, xhat, rstd, vn, mixed


def _sgu_fwd(name, proj, lng, lnb, wsm, bst):
    T = proj.shape[0]
    W = SGU_WIDTH
    tm = _tile(T, 512)

    def body(u_ref, v_ref, lng_ref, lnb_ref, ws_ref, bs_ref, o_ref):
        gu, _, _, _, mixed = _sgu_core(u_ref[...].astype(F32), v_ref[...].astype(F32), lng_ref[...], lnb_ref[...],
                                       ws_ref, bs_ref)
        o_ref[...] = (gu * mixed).astype(BF16)

    vec = Block((1, W), lambda i: (0, 0))
    return pl.pallas_call(
        body, name=name, grid=(T // tm,),
        in_specs=[Block((tm, W), lambda i: (i, 1)), Block((tm, W), lambda i: (i, 2)), vec, vec,
                  Block((SGU_HEADS, 128, 128), lambda i: (0, 0, 0)), Block((SGU_HEADS, 128, 128), lambda i: (0, 0, 0))],
        out_specs=Block((tm, W), lambda i: (i, 0)), out_shape=SDS((T, W), BF16),
        compiler_params=_cparams(("arbitrary",), VMEM_BIG))(proj, proj, lng, lnb, wsm, bst)


def _sgu_bwd(name, dyb, proj, lng, lnb, wsm, wsmt, bst):
    T = proj.shape[0]
    W = SGU_WIDTH
    tm = _tile(T, 512)

    def body(dy_ref, u_ref, v_ref, lng_ref, lnb_ref, ws_ref, wst_ref, bs_ref,
             duv_ref, dws_ref, dbs_ref, dlng_ref, dlnb_ref):
        i = pl.program_id(0)
        ur = u_ref[...].astype(F32)
        vr = v_ref[...].astype(F32)
        lng_v = lng_ref[...]
        gu, xhat, rstd, vn, mixed = _sgu_core(ur, vr, lng_v, lnb_ref[...], ws_ref, bs_ref)
        dy = dy_ref[...].astype(F32)
        dgu = dy * mixed
        dmix = dy * gu
        dmb = dmix.astype(BF16)
        dws_p, dbs_p, rows = [], [], []
        for h in range(SGU_HEADS):
            acc_w = jnp.zeros((128, 128), F32)
            acc_b = jnp.zeros((128, 1), F32)
            for n in range(tm // MLP_CHUNK):
                r0 = n * MLP_CHUNK
                dmt = dmb[r0:r0 + MLP_CHUNK, h * 128:(h + 1) * 128]
                acc_w = acc_w + _dot(dmt, vn[r0:r0 + MLP_CHUNK, h * 128:(h + 1) * 128], NT)
                acc_b = acc_b + jnp.sum(dmix[r0:r0 + MLP_CHUNK, h * 128:(h + 1) * 128], axis=1, keepdims=True)
            dws_p.append(acc_w)
            dbs_p.append(jnp.broadcast_to(acc_b, (128, 128)))
        for n in range(tm // MLP_CHUNK):
            r0 = n * MLP_CHUNK
            rows.append(jnp.concatenate(
                [_dot(wst_ref[h], dmb[r0:r0 + MLP_CHUNK, h * 128:(h + 1) * 128]) for h in range(SGU_HEADS)], axis=1))
        dvn = jnp.concatenate(rows, axis=0) if len(rows) > 1 else rows[0]
        dlng_p = jnp.sum(dvn * xhat, axis=0, keepdims=True)
        dlnb_p = jnp.sum(dvn, axis=0, keepdims=True)
        dxh = dvn * lng_v
        dgv = rstd * (dxh - jnp.mean(dxh, axis=-1, keepdims=True)
                      - xhat * jnp.mean(dxh * xhat, axis=-1, keepdims=True))
        duv_ref[:, :W] = (dgu * _gelu_grad(ur)).astype(BF16)
        duv_ref[:, W:] = (dgv * _gelu_grad(vr)).astype(BF16)

        @pl.when(i == 0)
        def _():
            for h in range(SGU_HEADS):
                dws_ref[h] = dws_p[h]
                dbs_ref[h] = dbs_p[h]
            dlng_ref[...] = dlng_p
            dlnb_ref[...] = dlnb_p

        @pl.when(i > 0)
        def _():
            for h in range(SGU_HEADS):
                dws_ref[h] += dws_p[h]
                dbs_ref[h] += dbs_p[h]
            dlng_ref[...] += dlng_p
            dlnb_ref[...] += dlnb_p

    vec = Block((1, W), lambda i: (0, 0))
    wsb = Block((SGU_HEADS, 128, 128), lambda i: (0, 0, 0))
    hsq = SDS((SGU_HEADS, 128, 128), F32)
    return pl.pallas_call(
        body, name=name, grid=(T // tm,),
        in_specs=[Block((tm, W), lambda i: (i, 0)), Block((tm, W), lambda i: (i, 1)), Block((tm, W), lambda i: (i, 2)),
                  vec, vec, wsb, wsb, wsb],
        out_specs=[Block((tm, 2 * W), lambda i: (i, 0)), wsb, wsb, vec, vec],
        out_shape=[SDS((T, 2 * W), BF16), hsq, hsq, SDS((1, W), F32), SDS((1, W), F32)],
        compiler_params=_cparams(("arbitrary",), VMEM_BIG))(dyb, proj, proj, lng, lnb, wsm, wsmt, bst)


def _s5_disc(lr, li, ldt, brt, bit):
    dt = jnp.exp(ldt)
    decay = jnp.exp(lr * dt)
    abr = decay * jnp.cos(li * dt)
    abi = decay * jnp.sin(li * dt)
    dec_s = jnp.exp(lr * dt * S5_SEG)
    asr = dec_s * jnp.cos(li * dt * S5_SEG)
    asi = dec_s * jnp.sin(li * dt * S5_SEG)
    denom = lr * lr + li * li
    nr = abr - 1.0
    ni = abi
    kr = (nr * lr + ni * li) / denom
    ki = (ni * lr - nr * li) / denom
    bkr = kr[None] * brt - ki[None] * bit
    bki = kr[None] * bit + ki[None] * brt
    return abr, abi, asr, asi, bkr, bki


def _s5_prep(lr, li, ldt, brt, bit):
    G, P, C = S5_GROUPS, S5_STATE, S5_GROUP_WIDTH

    def body(lr_ref, li_ref, ldt_ref, br_ref, bi_ref, abr_ref, abi_ref, asr_ref, asi_ref, bkr_ref, bki_ref):
        res = _s5_disc(lr_ref[...], li_ref[...], ldt_ref[...], br_ref[...], bi_ref[...])
        for o, r in zip((abr_ref, abi_ref, asr_ref, asi_ref, bkr_ref, bki_ref), res):
            o[...] = r

    gp = SDS((G, P), F32)
    cgp = SDS((C, G, P), F32)
    return pl.pallas_call(body, name="s5_prep", out_shape=[gp, gp, gp, gp, cgp, cgp])(lr, li, ldt, brt, bit)


def _s5_prep_bwd(lr, li, ldt, brt, bit, dabr, dabi, dbkr, dbki):
    G, P, C = S5_GROUPS, S5_STATE, S5_GROUP_WIDTH

    def body(lr_ref, li_ref, ldt_ref, br_ref, bi_ref, dabr_ref, dabi_ref, dbkr_ref, dbki_ref,
             o_lr, o_li, o_ldt, o_br, o_bi):
        def f(lr_, li_, ldt_, br_, bi_):
            abr, abi, _, _, bkr, bki = _s5_disc(lr_, li_, ldt_, br_, bi_)
            return abr, abi, bkr, bki

        _, pull = jax.vjp(f, lr_ref[...], li_ref[...], ldt_ref[...], br_ref[...], bi_ref[...])
        g = pull((dabr_ref[...], dabi_ref[...], dbkr_ref[...], dbki_ref[...]))
        for o, r in zip((o_lr, o_li, o_ldt, o_br, o_bi), g):
            o[...] = r

    gp = SDS((G, P), F32)
    cgp = SDS((C, G, P), F32)
    return pl.pallas_call(body, name="s5_prep_bwd", out_shape=[gp, gp, SDS((G, 1), F32), cgp, cgp])(
        lr, li, ldt, brt, bit, dabr, dabi, dbkr, dbki)


def _s5_scan(buf_ref, ar_row, ai_row, asr_row, asi_row, carry_ref, LG, xs_ref=None, dar_ref=None, dai_ref=None):
    reverse = xs_ref is not None
    NS, SEG = S5_NS, S5_SEG
    sgn = -1.0 if reverse else 1.0
    for lg in range(NS // LG):
        cr = slice(lg * LG, (lg + 1) * LG)
        ci = slice(NS + lg * LG, NS + (lg + 1) * LG)
        ar1, ai1 = ar_row[:, cr], sgn * ai_row[:, cr]
        asr1, asi1 = asr_row[:, cr], sgn * asi_row[:, cr]
        ar = jnp.broadcast_to(ar1, (8, LG))
        ai = jnp.broadcast_to(ai1, (8, LG))

        def step_of(j):
            return (SEG - 1 - j) if reverse else j

        def p1(j, st):
            sr, si = st
            rows = pl.ds(pl.multiple_of(step_of(j) * 8, 8), 8)
            nr = ar * sr - ai * si + buf_ref[rows, cr]
            ni = ar * si + ai * sr + buf_ref[rows, ci]
            buf_ref[rows, cr] = nr
            buf_ref[rows, ci] = ni
            return nr, ni

        z = jnp.zeros((8, LG), F32)
        er, ei = lax.fori_loop(0, SEG, p1, (z, z), unroll=2)
        c_r = carry_ref[:, cr]
        c_i = carry_ref[:, ci]
        cs_r, cs_i = [None] * 8, [None] * 8
        order = range(7, -1, -1) if reverse else range(8)
        for s in order:
            cs_r[s], cs_i[s] = c_r, c_i
            e_r, e_i = er[s:s + 1], ei[s:s + 1]
            c_r, c_i = e_r + asr1 * c_r - asi1 * c_i, e_i + asr1 * c_i + asi1 * c_r
        carry_ref[:, cr] = c_r
        carry_ref[:, ci] = c_i
        cmr = jnp.concatenate(cs_r, axis=0)
        cmi = jnp.concatenate(cs_i, axis=0)

        if not reverse:
            def p2(j, st):
                wr, wi = st
                rows = pl.ds(pl.multiple_of(j * 8, 8), 8)
                nwr = ar * wr - ai * wi
                nwi = ar * wi + ai * wr
                buf_ref[rows, cr] += nwr
                buf_ref[rows, ci] += nwi
                return nwr, nwi

            lax.fori_loop(0, SEG, p2, (cmr, cmi), unroll=2)
        else:
            def p2(j, st):
                wr, wi, pr, pi, dr, di = st
                rows = pl.ds(pl.multiple_of(step_of(j) * 8, 8), 8)
                xr = xs_ref[rows, cr]
                xi = xs_ref[rows, ci]
                dr = dr + pr * xr + pi * xi
                di = di + pi * xr - pr * xi
                nwr = ar * wr - ai * wi
                nwi = ar * wi + ai * wr
                gr = buf_ref[rows, cr] + nwr
                gi = buf_ref[rows, ci] + nwi
                buf_ref[rows, cr] = gr
                buf_ref[rows, ci] = gi
                return nwr, nwi, gr, gi, dr, di

            st = lax.fori_loop(0, SEG, p2, (cmr, cmi, cmr, cmi, z, z), unroll=2)
            dar_ref[:, cr] += st[4]
            dai_ref[:, cr] += st[5]


def _s5_fwd(proj, perm, permt, bdbr, bdbi, bdcr, bdci, abr, abi, asr, asi, dvec, wglu, bglu):
    T = proj.shape[0]
    TC, NS, W = S5_TC, S5_NS, S5_WIDTH
    nc = T // TC

    def body(u_ref, pm_ref, pmt_ref, bdbr_ref, bdbi_ref, bdcr_ref, bdci_ref, ar_ref, ai_ref, asr_ref, asi_ref,
             d_ref, wglu_ref, bglu_ref, ya_ref, xs_ref, ypre_ref, carry_ref):
        i = pl.program_id(0)

        @pl.when(i == 0)
        def _():
            carry_ref[...] = jnp.zeros_like(carry_ref)

        up = _dot(pm_ref[...], u_ref[...]).astype(BF16)
        for j in range(8):
            ut = up[:, j * 128:(j + 1) * 128]
            xs_ref[:, j * 512:(j + 1) * 512] = _dot(ut, bdbr_ref[j])
            xs_ref[:, NS + j * 512:NS + (j + 1) * 512] = _dot(ut, bdbi_ref[j])
        _s5_scan(xs_ref, ar_ref[...], ai_ref[...], asr_ref[...], asi_ref[...], carry_ref, 512)
        ys = []
        for j in range(8):
            xr = xs_ref[:, j * 512:(j + 1) * 512].astype(BF16)
            xi = xs_ref[:, NS + j * 512:NS + (j + 1) * 512].astype(BF16)
            ys.append(_dot(xr, bdcr_ref[j]) + _dot(xi, bdci_ref[j]))
        ypre = jnp.concatenate(ys, axis=1) + d_ref[...] * up.astype(F32)
        ypre_ref[...] = ypre
        ya = _gelu(ypre)
        zl = _dot(ya.astype(BF16), wglu_ref[...]) + bglu_ref[...]
        outp = (ya * _sigmoid(zl)).astype(BF16)
        ya_ref[...] = _dot(pmt_ref[...], outp).astype(BF16)

    return pl.pallas_call(
        body, name="s5_fwd", grid=(nc,),
        in_specs=[Block((TC, W), lambda i: (i, 0)), _const((TC, TC)), _const((TC, TC)),
                  _const((8, 128, 512)), _const((8, 128, 512)), _const((8, 512, 128)), _const((8, 512, 128)),
                  _const((1, NS)), _const((1, NS)), _const((1, NS)), _const((1, NS)),
                  _const((1, W)), _const((W, W)), _const((1, W))],
        out_specs=[Block((TC, W), lambda i: (i, 0)), Block((TC, 2 * NS), lambda i: (i, 0)),
                   Block((TC, W), lambda i: (i, 0))],
        out_shape=[SDS((T, W), BF16), SDS((T, 2 * NS), F32), SDS((T, W), F32)],
        scratch_shapes=[pltpu.VMEM((1, 2 * NS), F32)],
        compiler_params=_cparams(("arbitrary",), VMEM_BIG),
    )(proj, perm, permt, bdbr, bdbi, bdcr, bdci, abr, abi, asr, asi, dvec, wglu, bglu)


def _s5_bwd(dya, proj, ypre, xs, perm, permt, bdbr, bdbi, bdcr, bdci, abr, abi, asr, asi, dvec, wglu, bglu):
    T = proj.shape[0]
    TC, NS, W = S5_TC, S5_NS, S5_WIDTH
    nc = T // TC

    def body(dya_ref, u_ref, ypre_ref, xs_ref, pm_ref, pmt_ref, bdbr_ref, bdbi_ref, bdcr_ref, bdci_ref,
             ar_ref, ai_ref, asr_ref, asi_ref, d_ref, wglu_ref, bglu_ref,
             du_ref, dar_ref, dai_ref, dd_ref, dbglu_ref, o_dbdbr, o_dbdbi, o_dbdcr, o_dbdci, o_dwglu,
             g_ref, carry_ref, dbdbr_ref, dbdbi_ref, dbdcr_ref, dbdci_ref, dwglu_ref):
        i = pl.program_id(0)

        @pl.when(i == 0)
        def _():
            carry_ref[...] = jnp.zeros_like(carry_ref)
            for r in (dbdbr_ref, dbdbi_ref, dbdcr_ref, dbdci_ref, dar_ref, dai_ref, dd_ref, dwglu_ref, dbglu_ref):
                r[...] = jnp.zeros_like(r)

        pm = pm_ref[...]
        dyo = _dot(pm, dya_ref[...])
        up = _dot(pm, u_ref[...]).astype(BF16)
        upf = up.astype(F32)
        ypre_v = ypre_ref[...]
        ya = _gelu(ypre_v)
        yab = ya.astype(BF16)
        sg = _sigmoid(_dot(yab, wglu_ref[...]) + bglu_ref[...])
        dz = dyo * ya * sg * (1.0 - sg)
        dzb = dz.astype(BF16)
        dya_t = dyo * sg + _dot(dzb, wglu_ref[...], NT)
        dwglu_ref[...] += _dot(yab, dzb, TN)
        dbglu_ref[...] += jnp.sum(dz, axis=0, keepdims=True)
        dy = dya_t * _gelu_grad(ypre_v)
        dd_ref[...] += jnp.sum(dy * upf, axis=0, keepdims=True)
        dyb = dy.astype(BF16)
        for j in range(8):
            dyj = dyb[:, j * 128:(j + 1) * 128]
            g_ref[:, j * 512:(j + 1) * 512] = _dot(dyj, bdcr_ref[j], NT)
            g_ref[:, NS + j * 512:NS + (j + 1) * 512] = _dot(dyj, bdci_ref[j], NT)
            dbdcr_ref[j] += _dot(xs_ref[:, j * 512:(j + 1) * 512].astype(BF16), dyj, TN)
            dbdci_ref[j] += _dot(xs_ref[:, NS + j * 512:NS + (j + 1) * 512].astype(BF16), dyj, TN)
        _s5_scan(g_ref, ar_ref[...], ai_ref[...], asr_ref[...], asi_ref[...], carry_ref, 256,
                 xs_ref=xs_ref, dar_ref=dar_ref, dai_ref=dai_ref)
        dus = []
        for j in range(8):
            ut = up[:, j * 128:(j + 1) * 128]
            gr = g_ref[:, j * 512:(j + 1) * 512].astype(BF16)
            gi = g_ref[:, NS + j * 512:NS + (j + 1) * 512].astype(BF16)
            dbdbr_ref[j] += _dot(ut, gr, TN)
            dbdbi_ref[j] += _dot(ut, gi, TN)
            dus.append(_dot(gr, bdbr_ref[j], NT) + _dot(gi, bdbi_ref[j], NT))
        dup = jnp.concatenate(dus, axis=1) + d_ref[...] * dy
        du_ref[...] = _dot(pmt_ref[...], dup.astype(BF16)).astype(BF16)

        @pl.when(i == nc - 1)
        def _():
            for src, dst in ((dbdbr_ref, o_dbdbr), (dbdbi_ref, o_dbdbi), (dbdcr_ref, o_dbdcr),
                             (dbdci_ref, o_dbdci), (dwglu_ref, o_dwglu)):
                pltpu.sync_copy(src, dst)

    c2 = lambda i: (0, 0)
    rev = lambda i: (nc - 1 - i, 0)
    return pl.pallas_call(
        body, name="s5_bwd", grid=(nc,),
        in_specs=[Block((TC, W), rev), Block((TC, W), rev), Block((TC, W), rev), Block((TC, 2 * NS), rev),
                  _const((TC, TC)), _const((TC, TC)),
                  _const((8, 128, 512)), _const((8, 128, 512)), _const((8, 512, 128)), _const((8, 512, 128)),
                  _const((1, NS)), _const((1, NS)), _const((1, NS)), _const((1, NS)),
                  _const((1, W)), _const((W, W)), _const((1, W))],
        out_specs=[Block((TC, W), rev), Block((8, NS), c2), Block((8, NS), c2), Block((1, W), c2), Block((1, W), c2),
                   ANY, ANY, ANY, ANY, ANY],
        out_shape=[SDS((T, W), BF16), SDS((8, NS), F32), SDS((8, NS), F32), SDS((1, W), F32), SDS((1, W), F32),
                   SDS((8, 128, 512), F32), SDS((8, 128, 512), F32),
                   SDS((8, 512, 128), F32), SDS((8, 512, 128), F32), SDS((W, W), F32)],
        scratch_shapes=[pltpu.VMEM((TC, 2 * NS), F32), pltpu.VMEM((1, 2 * NS), F32),
                        pltpu.VMEM((8, 128, 512), F32), pltpu.VMEM((8, 128, 512), F32),
                        pltpu.VMEM((8, 512, 128), F32), pltpu.VMEM((8, 512, 128), F32), pltpu.VMEM((W, W), F32)],
        compiler_params=_cparams(("arbitrary",), VMEM_BIG),
    )(dya, proj, ypre, xs, perm, permt, bdbr, bdbi, bdcr, bdci, abr, abi, asr, asi, dvec, wglu, bglu)


def _bd_b(bk_t):
    C, P = S5_GROUP_WIDTH, S5_STATE
    t = jnp.transpose(bk_t, (1, 0, 2)).reshape(8, 8, C, P)
    eye = jnp.eye(8, dtype=t.dtype)
    return (t[:, :, :, None, :] * eye[None, :, None, :, None]).reshape(8, 8 * C, 8 * P)


def _bd_b_extract(m):
    C, P = S5_GROUP_WIDTH, S5_STATE
    t = m.reshape(8, 8, C, 8, P)
    d = jnp.stack([t[:, g, :, g, :] for g in range(8)], axis=1)
    return jnp.transpose(d.reshape(S5_GROUPS, C, P), (1, 0, 2))


def _bd_c(c):
    C, P = S5_GROUP_WIDTH, S5_STATE
    t = jnp.transpose(c, (0, 2, 1)).reshape(8, 8, P, C)
    eye = jnp.eye(8, dtype=t.dtype)
    return (t[:, :, :, None, :] * eye[None, :, None, :, None]).reshape(8, 8 * P, 8 * C)


def _bd_c_extract(m):
    C, P = S5_GROUP_WIDTH, S5_STATE
    t = m.reshape(8, 8, P, 8, C)
    d = jnp.stack([t[:, g, :, g, :] for g in range(8)], axis=1)
    return jnp.transpose(d.reshape(S5_GROUPS, P, C), (0, 2, 1))


def _perm_matrix():
    r = jnp.arange(S5_TC)
    src = (r % 8) * S5_SEG + r // 8
    return (src[:, None] == jnp.arange(S5_TC)[None, :]).astype(BF16)


def _coords():
    return lax.axis_index("x"), lax.axis_index("y"), lax.axis_index("c")


def _all_gather(name, arrs):
    n = len(arrs)

    def body(*refs):
        ins, outs = refs[:n], refs[n:2 * n]
        send_sems, recv_sems, local_sems = refs[2 * n:]
        x, y, c = _coords()
        me, sibling = (x, y, c), (x, y, 1 - c)
        chips = [(1 - x, y), (x, 1 - y), (1 - x, 1 - y)]

        def slot(p):
            return 4 * p[0] + 2 * p[1] + p[2]

        def copy(a, k, block, to, src=None):
            dst = outs[a].at[slot(block)]
            return pltpu.make_async_remote_copy(
                src_ref=dst if src is None else src, dst_ref=dst,
                send_sem=send_sems.at[a * 7 + k], recv_sem=recv_sems.at[a * 7 + k],
                device_id=to, device_id_type=MESH)

        mine = [pltpu.make_async_copy(ins[a], outs[a].at[slot(me)], local_sems.at[a]) for a in range(n)]
        for m in mine:
            m.start()
        first = []
        for a in range(n):
            first.append(copy(a, 0, me, sibling, src=ins[a]))
            first += [copy(a, 1 + j, me, (*chip, c), src=ins[a]) for j, chip in enumerate(chips)]
        for cp in first:
            cp.start()
        passed = []
        for j, chip in enumerate(chips):
            for a in range(n):
                copy(a, 1 + j, (*chip, c), me).wait_recv()
                fw = copy(a, 4 + j, (*chip, c), sibling)
                fw.start()
                passed.append(fw)
        for a in range(n):
            copy(a, 0, sibling, me).wait_recv()
            for j, chip in enumerate(chips):
                copy(a, 4 + j, (*chip, 1 - c), me).wait_recv()
        for cp in first + passed:
            cp.wait_send()
        for m in mine:
            m.wait()

    return pl.pallas_call(
        body, name=name,
        in_specs=[ANY] * n, out_specs=[ANY] * n,
        out_shape=[SDS((NDEV,) + a.shape, a.dtype) for a in arrs],
        scratch_shapes=[pltpu.SemaphoreType.DMA((7 * n,)), pltpu.SemaphoreType.DMA((7 * n,)),
                        pltpu.SemaphoreType.DMA((n,))],
    )(*arrs)


def _exchange(name, arrs):
    n = len(arrs)

    def body(*refs):
        ins, outs = refs[:n], refs[n:2 * n]
        send_sems, recv_sems, local_sems = refs[2 * n:]
        x, y, c = _coords()
        me = (x, y, c)

        def slot(p):
            return 4 * p[0] + 2 * p[1] + p[2]

        peers = []
        for fx in (0, 1):
            for fy in (0, 1):
                for fc in (0, 1):
                    if fx or fy or fc:
                        peers.append((1 - x if fx else x, 1 - y if fy else y, 1 - c if fc else c))

        def copy(a, k, to):
            return pltpu.make_async_remote_copy(
                src_ref=ins[a].at[slot(to)], dst_ref=outs[a].at[slot(me)],
                send_sem=send_sems.at[a * 7 + k], recv_sem=recv_sems.at[a * 7 + k],
                device_id=to, device_id_type=MESH)

        mine = [pltpu.make_async_copy(ins[a].at[slot(me)], outs[a].at[slot(me)], local_sems.at[a]) for a in range(n)]
        for m in mine:
            m.start()
        sends = [copy(a, k, p) for a in range(n) for k, p in enumerate(peers)]
        for cp in sends:
            cp.start()
        for a in range(n):
            for k, p in enumerate(peers):
                pltpu.make_async_remote_copy(
                    src_ref=ins[a].at[slot(p)], dst_ref=outs[a].at[slot(p)],
                    send_sem=send_sems.at[a * 7 + k], recv_sem=recv_sems.at[a * 7 + k],
                    device_id=p, device_id_type=MESH).wait_recv()
        for cp in sends:
            cp.wait_send()
        for m in mine:
            m.wait()

    return pl.pallas_call(
        body, name=name,
        in_specs=[ANY] * n, out_specs=[ANY] * n,
        out_shape=[SDS(a.shape, a.dtype) for a in arrs],
        scratch_shapes=[pltpu.SemaphoreType.DMA((7 * n,)), pltpu.SemaphoreType.DMA((7 * n,)),
                        pltpu.SemaphoreType.DMA((n,))],
    )(*arrs)


def _adam_math(w, g, m, v):
    m = ADAM_B1 * m + (1.0 - ADAM_B1) * g
    v = ADAM_B2 * v + (1.0 - ADAM_B2) * (g * g)
    m_hat = m / (1.0 - ADAM_B1 ** ADAM_STEP)
    v_hat = v / (1.0 - ADAM_B2 ** ADAM_STEP)
    delta = -ADAM_LR * (m_hat / (jnp.sqrt(v_hat) + ADAM_EPS) + ADAM_WD * w)
    return delta, m, v


def _adam_sharded(name, recv, sub, w, m, v):
    R, Cc = w.shape
    tr = max(t for t in range(16, R + 1, 16) if R % t == 0 and t * Cc <= 256 * 1024)

    def body(*refs):
        parts = refs[:NDEV]
        w_ref, m_ref, v_ref, g_out, d_out, m_out, v_out = refs[NDEV:]
        g = parts[0][...].astype(F32)
        for p in parts[1:]:
            g = g + p[...].astype(F32)
        delta, mn, vn = _adam_math(w_ref[...], g, m_ref[...], v_ref[...])
        g_out[...] = g
        d_out[...] = delta
        m_out[...] = mn
        v_out[...] = vn

    if sub is None:
        pspecs = [Block((None, tr, Cc), functools.partial(lambda s, i: (s, i, 0), s)) for s in range(NDEV)]
    else:
        pspecs = [Block((None, None, tr, Cc), functools.partial(lambda s, i: (s, sub, i, 0), s)) for s in range(NDEV)]
    row = Block((tr, Cc), lambda i: (i, 0))
    o = SDS((R, Cc), F32)
    return pl.pallas_call(
        body, name=name, grid=(R // tr,),
        in_specs=pspecs + [row, row, row], out_specs=[row, row, row, row], out_shape=[o, o, o, o],
        compiler_params=_cparams(("arbitrary",), VMEM_BIG))(*([recv] * NDEV), w, m, v)


def _adam_small(parts, w, m, v):
    R = w.shape[0]

    def body(p_ref, w_ref, m_ref, v_ref, g_out, d_out, m_out, v_out):
        g = p_ref[0]
        for s in range(1, NDEV):
            g = g + p_ref[s]
        delta, mn, vn = _adam_math(w_ref[...], g, m_ref[...], v_ref[...])
        g_out[...] = g
        d_out[...] = delta
        m_out[...] = mn
        v_out[...] = vn

    o = SDS((R, 128), F32)
    return pl.pallas_call(body, name="adam_small", out_shape=[o, o, o, o],
                          compiler_params=_cparams(None, VMEM_BIG))(parts, w, m, v)


_SMALL = ["ffn1_norm", "mix_norm", "s5_a_re", "s5_a_im", "s5_log_dt", "s5_b_re", "s5_b_im", "s5_c_re", "s5_c_im",
          "s5_d", "s5_b_glu", "sgu_ln_g", "sgu_ln_b", "sgu_w_s", "sgu_b_s", "b_gate", "ffn2_norm", "final_norm"]
_SHARDED = ["ffn1_w_gate", "ffn1_w_up", "ffn1_w_down", "w_in", "s5_w_glu", "w_branch_a", "w_branch_b", "w_gate",
            "w_out", "ffn2_w_gate", "ffn2_w_up", "ffn2_w_down"]
_ORDER = ["ffn1_norm", "ffn1_w_gate", "ffn1_w_up", "ffn1_w_down", "mix_norm", "w_in", "s5_a_re", "s5_a_im",
          "s5_log_dt", "s5_b_re", "s5_b_im", "s5_c_re", "s5_c_im", "s5_d", "s5_w_glu", "s5_b_glu", "sgu_ln_g",
          "sgu_ln_b", "sgu_w_s", "sgu_b_s", "w_branch_a", "w_branch_b", "w_gate", "b_gate", "w_out", "ffn2_norm",
          "ffn2_w_gate", "ffn2_w_up", "ffn2_w_down", "final_norm"]


def _step(x, tgt, W, M, V):
    T = x.shape[1]
    x0 = x[0]
    tgt0 = tgt[0]
    bf = lambda a: a.astype(BF16)

    gu_sh = jnp.stack([bf(W["ffn1_w_gate"][0]), bf(W["ffn1_w_up"][0]), bf(W["ffn2_w_gate"][0]), bf(W["ffn2_w_up"][0])])
    wd_sh = jnp.stack([bf(W["ffn1_w_down"][0]), bf(W["ffn2_w_down"][0])])
    br_sh = jnp.stack([bf(W["w_branch_a"][0]), bf(W["w_branch_b"][0])])
    wgu, wd, wbr, win, wgate, wout, wglu = _all_gather(
        "gather_weights",
        [gu_sh, wd_sh, br_sh, bf(W["w_in"][0]), bf(W["w_gate"][0]), bf(W["w_out"][0]), bf(W["s5_w_glu"][0])])
    wout = wout.reshape(D_MODEL, D_MODEL)
    wglu = wglu.reshape(S5_WIDTH, S5_WIDTH)
    wba, wbb = wbr[:, 0], wbr[:, 1]

    lr_, li_ = W["s5_a_re"][0], W["s5_a_im"][0]
    ldt_ = W["s5_log_dt"][0][:, None]
    brt = jnp.transpose(W["s5_b_re"][0], (2, 0, 1))
    bit = jnp.transpose(W["s5_b_im"][0], (2, 0, 1))
    abr, abi, asr, asi, bkr_t, bki_t = _s5_prep(lr_, li_, ldt_, brt, bit)
    bdbr, bdbi = bf(_bd_b(bkr_t)), bf(_bd_b(bki_t))
    bdcr, bdci = bf(_bd_c(W["s5_c_re"][0])), bf(_bd_c(-W["s5_c_im"][0]))
    flat = lambda a: a.reshape(1, S5_NS)
    s5c = (_perm_matrix(), _perm_matrix().T, bdbr, bdbi, bdcr, bdci, flat(abr), flat(abi), flat(asr), flat(asi),
           W["s5_d"][0].reshape(1, S5_WIDTH), wglu, W["s5_b_glu"])
    blk = jnp.arange(MLP_CHUNK) // CHUNK
    mask = blk[:, None] >= blk[None, :]
    wsm = jnp.where(mask[None], W["sgu_w_s"][0], 0.0)
    wsm_b, wsmt_b = bf(wsm), bf(jnp.transpose(wsm, (0, 2, 1)))
    bst = jnp.broadcast_to(W["sgu_b_s"][0][:, :, None], (SGU_HEADS, MLP_CHUNK, 128))
    bgate2 = W["b_gate"].reshape(2, 1, D_MODEL)

    h1 = _rms_fwd("rms1", x0, W["ffn1_norm"])
    ab1, f1 = _ffn_up("ffn1_up", h1, wgu, 0)
    x1 = _ffn_down("ffn1_down", f1, wd, 0, x0)
    h2 = _rms_fwd("rms2", x1, W["mix_norm"])
    proj = _col_fwd("w_in", h2, win)
    ya, xs, ypre = _s5_fwd(proj, *s5c)
    yb = _sgu_fwd("sgu_fwd", proj, W["sgu_ln_g"], W["sgu_ln_b"], wsm_b, bst)
    pa = _col_fwd("branch_a", ya, wba)
    pb = _col_fwd("branch_b", yb, wbb)
    gl = _gate_fwd("gate", h2, wgate, bgate2)
    merged = _merge_fwd("merge", pa, pb, gl)
    x2 = _plain_fwd_res("w_out", merged, wout, x1)
    h3 = _rms_fwd("rms3", x2, W["ffn2_norm"])
    ab2, f2 = _ffn_up("ffn2_up", h3, wgu, 2)
    x3 = _ffn_down("ffn2_down", f2, wd, 1, x2)
    loss_p, dx3, dx3b, dgf = _loss_head("loss_head", x3, W["final_norm"].reshape(1, D_MODEL), tgt0)

    dab2 = _ffn_down_bwd_act("ffn2_down_bwd_a", dx3b, wd, 1, ab2)
    g_wd2 = _ffn_down_bwd_w("ffn2_down_bwd_w", f2, dx3b)
    dh3 = _ffn_up_bwd_h("ffn2_up_bwd_h", dab2, wgu, 2)
    g_gu2 = _ffn_up_bwd_w("ffn2_up_bwd_w", h3, dab2)
    dx2, dx2b, dg3 = _rms_bwd("rms3_bwd", dh3, x2, W["ffn2_norm"], dx3)

    dmerged = _plain_bwd_a("w_out_bwd_a", dx2b, wout)
    g_wout = _plain_bwd_w("w_out_bwd_w", merged, dx2b)
    dpa, dpb, dgl, dbgate = _merge_bwd("merge_bwd", dmerged, pa, pb, gl)
    dya = _col_bwd_a("branch_a_bwd_a", dpa, wba)
    g_wba = _col_bwd_w("branch_a_bwd_w", ya, dpa, 256)
    dyb = _col_bwd_a("branch_b_bwd_a", dpb, wbb)
    g_wbb = _col_bwd_w("branch_b_bwd_w", yb, dpb, 256)
    dh2g = _gate_bwd_a("gate_bwd_a", dgl, wgate)
    g_wgate = _gate_bwd_w("gate_bwd_w", h2, dgl, 512)
    duv, dws, dbst, dlng, dlnb = _sgu_bwd("sgu_bwd", dyb, proj, W["sgu_ln_g"], W["sgu_ln_b"], wsm_b, wsmt_b, bst)
    (dua, dar8, dai8, ddv, dbglu, dbdbr, dbdbi, dbdcr, dbdci, g_wglu) = _s5_bwd(dya, proj, ypre, xs, *s5c)
    dproj = jnp.concatenate([dua, duv], axis=1)
    dh2 = _col_bwd_a("w_in_bwd_a", dproj, win, add=dh2g)
    g_win = _col_bwd_w("w_in_bwd_w", h2, dproj, 384)
    dx1, dx1b, dgm = _rms_bwd("rms2_bwd", dh2, x1, W["mix_norm"], dx2)

    dab1 = _ffn_down_bwd_act("ffn1_down_bwd_a", dx1b, wd, 0, ab1)
    g_wd1 = _ffn_down_bwd_w("ffn1_down_bwd_w", f1, dx1b)
    dh1 = _ffn_up_bwd_h("ffn1_up_bwd_h", dab1, wgu, 0)
    g_gu1 = _ffn_up_bwd_w("ffn1_up_bwd_w", h1, dab1)
    dx0, _, dg1 = _rms_bwd("rms1_bwd", dh1, x0, W["ffn1_norm"], dx1)

    dabr = jnp.sum(dar8, axis=0).reshape(S5_GROUPS, S5_STATE)
    dabi = jnp.sum(dai8, axis=0).reshape(S5_GROUPS, S5_STATE)
    d_lr, d_li, d_ldt, d_brt, d_bit = _s5_prep_bwd(lr_, li_, ldt_, brt, bit, dabr, dabi,
                                                   _bd_b_extract(dbdbr), _bd_b_extract(dbdbi))
    small_g = {
        "ffn1_norm": dg1, "mix_norm": dgm, "ffn2_norm": dg3, "final_norm": dgf,
        "s5_a_re": d_lr, "s5_a_im": d_li, "s5_log_dt": d_ldt,
        "s5_b_re": jnp.transpose(d_brt, (1, 2, 0)), "s5_b_im": jnp.transpose(d_bit, (1, 2, 0)),
        "s5_c_re": _bd_c_extract(dbdcr), "s5_c_im": -_bd_c_extract(dbdci),
        "s5_d": ddv, "s5_b_glu": dbglu, "sgu_ln_g": dlng, "sgu_ln_b": dlnb,
        "sgu_w_s": jnp.where(mask[None], dws, 0.0), "sgu_b_s": dbst[:, :, 0], "b_gate": dbgate,
    }

    sizes = [W[n].size for n in _SMALL]
    total = sum(sizes) + 1
    rows = -(-total // 128)
    rows = -(-rows // 8) * 8
    pad = rows * 128 - total

    def pack(d, extra):
        return jnp.concatenate([d[n].reshape(-1).astype(F32) for n in _SMALL] + [extra, jnp.zeros((pad,), F32)]
                               ).reshape(rows, 128)

    zero1 = jnp.zeros((1,), F32)
    parts = _all_gather("gather_small_grads", [pack(small_g, loss_p[0, :1])])[0]
    sg, sd, sm, sv = _adam_small(parts, pack(W, zero1), pack(M, zero1), pack(V, zero1))

    def unpack(flat2d):
        flat = flat2d.reshape(-1)
        out, off = {}, 0
        for n, s in zip(_SMALL, sizes):
            out[n] = flat[off:off + s].reshape(W[n].shape)
            off += s
        return out, flat[off]

    G, loss = unpack(sg)
    Dl, _ = unpack(sd)
    Mn, _ = unpack(sm)
    Vn, _ = unpack(sv)

    g_wout3 = g_wout.reshape(NDEV, D_MODEL // NDEV, D_MODEL)
    g_wglu3 = g_wglu.astype(BF16).reshape(NDEV, S5_WIDTH // NDEV, S5_WIDTH)
    (r_gu1, r_gu2, r_wd1, r_wd2, r_win, r_wglu, r_wba, r_wbb, r_wgate, r_wout) = _exchange(
        "exchange_grads", [g_gu1, g_gu2, g_wd1, g_wd2, g_win, g_wglu3, g_wba, g_wbb, g_wgate, g_wout3])
    plan = [("ffn1_w_gate", r_gu1, 0), ("ffn1_w_up", r_gu1, 1), ("ffn1_w_down", r_wd1, None),
            ("w_in", r_win, None), ("s5_w_glu", r_wglu, None), ("w_branch_a", r_wba, None),
            ("w_branch_b", r_wbb, None), ("w_gate", r_wgate, None), ("w_out", r_wout, None),
            ("ffn2_w_gate", r_gu2, 0), ("ffn2_w_up", r_gu2, 1), ("ffn2_w_down", r_wd2, None)]
    for n, recv, sub in plan:
        g, d, mn, vn = _adam_sharded("adam_" + n, recv, sub, W[n][0], M[n][0], V[n][0])
        G[n], Dl[n], Mn[n], Vn[n] = g[None], d[None], mn[None], vn[None]

    return loss, dx0[None], G, Dl, Mn, Vn


def kernel(x, ffn1_norm, ffn1_w_gate, ffn1_w_up, ffn1_w_down, mix_norm, w_in, s5_a_re, s5_a_im, s5_log_dt, s5_b_re, s5_b_im, s5_c_re, s5_c_im, s5_d, s5_w_glu, s5_b_glu, sgu_ln_g, sgu_ln_b, sgu_w_s, sgu_b_s, w_branch_a, w_branch_b, w_gate, b_gate, w_out, ffn2_norm, ffn2_w_gate, ffn2_w_up, ffn2_w_down, final_norm, loss_target, m_ffn1_norm, m_ffn1_w_gate, m_ffn1_w_up, m_ffn1_w_down, m_mix_norm, m_w_in, m_s5_a_re, m_s5_a_im, m_s5_log_dt, m_s5_b_re, m_s5_b_im, m_s5_c_re, m_s5_c_im, m_s5_d, m_s5_w_glu, m_s5_b_glu, m_sgu_ln_g, m_sgu_ln_b, m_sgu_w_s, m_sgu_b_s, m_w_branch_a, m_w_branch_b, m_w_gate, m_b_gate, m_w_out, m_ffn2_norm, m_ffn2_w_gate, m_ffn2_w_up, m_ffn2_w_down, m_final_norm, v_ffn1_norm, v_ffn1_w_gate, v_ffn1_w_up, v_ffn1_w_down, v_mix_norm, v_w_in, v_s5_a_re, v_s5_a_im, v_s5_log_dt, v_s5_b_re, v_s5_b_im, v_s5_c_re, v_s5_c_im, v_s5_d, v_s5_w_glu, v_s5_b_glu, v_sgu_ln_g, v_sgu_ln_b, v_sgu_w_s, v_sgu_b_s, v_w_branch_a, v_w_branch_b, v_w_gate, v_b_gate, v_w_out, v_ffn2_norm, v_ffn2_w_gate, v_ffn2_w_up, v_ffn2_w_down, v_final_norm):
    a = locals()
    W = {n: a[n] for n in _ORDER}
    M = {n: a["m_" + n] for n in _ORDER}
    V = {n: a["v_" + n] for n in _ORDER}
    loss, gx, G, Dl, Mn, Vn = _step(x, loss_target, W, M, V)
    return (loss, gx, *[G[n] for n in _ORDER], *[Dl[n] for n in _ORDER], *[Mn[n] for n in _ORDER],
            *[Vn[n] for n in _ORDER])
```

```python
import functools
import math

import jax
import jax.numpy as jnp
from jax import lax
from jax.experimental import pallas as pl
from jax.experimental.pallas import tpu as pltpu

F32 = jnp.float32
BF16 = jnp.bfloat16
NDEV = 8
NORM_EPS = 1e-6
D_MODEL = 2048
D_FF = 5632
FF_SHARD = D_FF // NDEV
S5_WIDTH = 1024
S5_GROUPS = 64
S5_GROUP_WIDTH = 16
S5_STATE = 64
S5_NS = S5_GROUPS * S5_STATE
SGU_WIDTH = 1024
SGU_HEADS = 8
MLP_CHUNK = 128
CHUNK = 64
ADAM_LR, ADAM_B1, ADAM_B2, ADAM_EPS, ADAM_WD, ADAM_STEP = 0.001, 0.9, 0.999, 1e-08, 0.01, 10
S5_TC = 256
S5_SEG = S5_TC // 8
S5_LG = 512
VMEM_BIG = 56 * 1024 * 1024

MESH = pl.DeviceIdType.MESH
SDS = jax.ShapeDtypeStruct
Block = pl.BlockSpec
ANY = pl.BlockSpec(memory_space=pl.ANY)


def _cparams(sem=None, vmem=None):
    return pltpu.CompilerParams(dimension_semantics=sem, vmem_limit_bytes=vmem)


def _const(shape):
    nd = len(shape)
    return pl.BlockSpec(shape, lambda i: (0,) * nd, pipeline_mode=pl.Buffered(1))


def _sigmoid(x):
    return 1.0 / (1.0 + jnp.exp(-x))


_GELU_C = math.sqrt(2.0 / math.pi)


def _gelu(x):
    return 0.5 * x * (1.0 + jnp.tanh(_GELU_C * (x + 0.044715 * x * x * x)))


def _gelu_grad(x):
    t = jnp.tanh(_GELU_C * (x + 0.044715 * x * x * x))
    return 0.5 * (1.0 + t) + 0.5 * x * (1.0 - t * t) * _GELU_C * (1.0 + 3.0 * 0.044715 * x * x)


NN = (((1,), (0,)), ((), ()))
NT = (((1,), (1,)), ((), ()))
TN = (((0,), (0,)), ((), ()))


def _dot(a, b, dims=NN):
    return lax.dot_general(a, b, dims, preferred_element_type=F32)


def _matmul(name, a, b, extras, *, grid, a_spec, b_spec, extra_specs, out_shapes, out_specs, acc_shape,
            dims, nb, epilogue, vmem=VMEM_BIG):
    nk = grid[2]
    ne, no = len(extras), len(out_shapes)

    def body(*refs):
        a_ref, b_ref = refs[0], refs[1]
        ex = refs[2:2 + ne]
        outs = refs[2 + ne:2 + ne + no]
        av = a_ref[...]
        prods = [_dot(av, b_ref[q] if nb else b_ref[...], dims) for q in range(nb or 1)]
        if nk == 1:
            epilogue(prods, ex, outs)
            return
        acc_ref = refs[2 + ne + no]
        k = pl.program_id(2)

        @pl.when(k == 0)
        def _():
            for q, p in enumerate(prods):
                acc_ref[q] = p

        @pl.when(k > 0)
        def _():
            for q, p in enumerate(prods):
                acc_ref[q] += p

        @pl.when(k == nk - 1)
        def _():
            epilogue([acc_ref[q] for q in range(nb or 1)], ex, outs)

    scratch = [] if nk == 1 else [pltpu.VMEM(((nb or 1),) + tuple(acc_shape), F32)]
    res = pl.pallas_call(
        body, name=name, grid=grid,
        in_specs=[a_spec, b_spec] + list(extra_specs),
        out_specs=list(out_specs), out_shape=list(out_shapes), scratch_shapes=scratch,
        compiler_params=_cparams(("parallel", "parallel", "arbitrary"), vmem),
    )(a, b, *extras)
    return res


def _store(dtype_outs=None):
    def ep(accs, ex, outs):
        outs[0][...] = accs[0].astype(outs[0].dtype)
    return ep


def _tile(n, t):
    t = min(n, t)
    assert n % t == 0, (n, t)
    return t


def _ffn_up(name, h, wgu):
    T, D = h.shape
    tm = _tile(T, 1024)

    def ep(accs, ex, outs):
        a, b = accs
        outs[0][0] = a.astype(BF16)
        outs[0][1] = b.astype(BF16)
        outs[1][...] = (a * _sigmoid(a) * b).astype(BF16)

    return _matmul(
        name, h, wgu, (), grid=(NDEV, T // tm, 1),
        a_spec=Block((tm, D), lambda j, i, k: (i, 0)),
        b_spec=Block((None, 2, D, FF_SHARD), lambda j, i, k: (j, 0, 0, 0)),
        extra_specs=(),
        out_shapes=[SDS((NDEV, 2, T, FF_SHARD), BF16), SDS((NDEV, T, FF_SHARD), BF16)],
        out_specs=[Block((None, 2, tm, FF_SHARD), lambda j, i, k: (j, 0, i, 0)),
                   Block((None, tm, FF_SHARD), lambda j, i, k: (j, i, 0))],
        acc_shape=(tm, FF_SHARD), dims=NN, nb=2, epilogue=ep)


def _ffn_down(name, f, wd, xres):
    _, T, _ = f.shape
    tm, tn = _tile(T, 1024), 1024

    def ep(accs, ex, outs):
        outs[0][...] = ex[0][...] + 0.5 * accs[0]

    return _matmul(
        name, f, wd, (xres,), grid=(T // tm, D_MODEL // tn, NDEV),
        a_spec=Block((None, tm, FF_SHARD), lambda i, j, k: (k, i, 0)),
        b_spec=Block((None, FF_SHARD, tn), lambda i, j, k: (k, 0, j)),
        extra_specs=[Block((tm, tn), lambda i, j, k: (i, j))],
        out_shapes=[SDS((T, D_MODEL), F32)],
        out_specs=[Block((tm, tn), lambda i, j, k: (i, j))],
        acc_shape=(tm, tn), dims=NN, nb=None, epilogue=ep)[0]


def _ffn_down_bwd_act(name, dyb, wd, ab):
    T, D = dyb.shape
    tm = _tile(T, 1024)

    def ep(accs, ex, outs):
        df = 0.5 * accs[0]
        a = ex[0][0].astype(F32)
        b = ex[0][1].astype(F32)
        s = _sigmoid(a)
        outs[0][0] = (df * b * s * (1.0 + a * (1.0 - s))).astype(BF16)
        outs[0][1] = (df * a * s).astype(BF16)

    return _matmul(
        name, dyb, wd, (ab,), grid=(NDEV, T // tm, 1),
        a_spec=Block((tm, D), lambda j, i, k: (i, 0)),
        b_spec=Block((None, FF_SHARD, D), lambda j, i, k: (j, 0, 0)),
        extra_specs=[Block((None, 2, tm, FF_SHARD), lambda j, i, k: (j, 0, i, 0))],
        out_shapes=[SDS((NDEV, 2, T, FF_SHARD), BF16)],
        out_specs=[Block((None, 2, tm, FF_SHARD), lambda j, i, k: (j, 0, i, 0))],
        acc_shape=(tm, FF_SHARD), dims=NT, nb=None, epilogue=ep)[0]


def _ffn_down_bwd_w(name, f, dyb):
    _, T, _ = f.shape
    tt, tn = _tile(T, 1024), 1024

    def ep(accs, ex, outs):
        outs[0][...] = (0.5 * accs[0]).astype(BF16)

    return _matmul(
        name, f, dyb, (), grid=(NDEV, D_MODEL // tn, T // tt),
        a_spec=Block((None, tt, FF_SHARD), lambda j, n, k: (j, k, 0)),
        b_spec=Block((tt, tn), lambda j, n, k: (k, n)),
        extra_specs=(),
        out_shapes=[SDS((NDEV, FF_SHARD, D_MODEL), BF16)],
        out_specs=[Block((None, FF_SHARD, tn), lambda j, n, k: (j, 0, n))],
        acc_shape=(FF_SHARD, tn), dims=TN, nb=None, epilogue=ep)[0]


def _ffn_up_bwd_h(name, dab, wgu, after):
    _, _, T, _ = dab.shape
    tm = _tile(T, 1024)
    return _matmul(
        name, dab, wgu, (after,), grid=(T // tm, 1, 2 * NDEV),
        a_spec=Block((None, None, tm, FF_SHARD), lambda i, j, k: (k // 2, k % 2, i, 0)),
        b_spec=Block((None, None, D_MODEL, FF_SHARD), lambda i, j, k: (k // 2, k % 2, 0, 0)),
        extra_specs=[Block((8, 128), lambda i, j, k: (0, 0))],
        out_shapes=[SDS((T, D_MODEL), BF16)],
        out_specs=[Block((tm, D_MODEL), lambda i, j, k: (i, 0))],
        acc_shape=(tm, D_MODEL), dims=NT, nb=None, epilogue=_store())[0]


def _ffn_up_bwd_w(name, h, dab):
    T, D = h.shape
    tt, tr = _tile(T, 1024), 1024

    def ep(accs, ex, outs):
        outs[0][0] = accs[0].astype(BF16)
        outs[0][1] = accs[1].astype(BF16)

    return _matmul(
        name, h, dab, (), grid=(NDEV, D // tr, T // tt),
        a_spec=Block((tt, tr), lambda j, n, k: (k, n)),
        b_spec=Block((None, 2, tt, FF_SHARD), lambda j, n, k: (j, 0, k, 0)),
        extra_specs=(),
        out_shapes=[SDS((NDEV, 2, D, FF_SHARD), BF16)],
        out_specs=[Block((None, 2, tr, FF_SHARD), lambda j, n, k: (j, 0, n, 0))],
        acc_shape=(tr, FF_SHARD), dims=TN, nb=2, epilogue=ep)[0]


def _col_fwd(name, a, w, out_dtype=BF16):
    T, K = a.shape
    ns = w.shape[2]
    tm = _tile(T, 1024)
    return _matmul(
        name, a, w, (), grid=(NDEV, T // tm, 1),
        a_spec=Block((tm, K), lambda j, i, k: (i, 0)),
        b_spec=Block((None, K, ns), lambda j, i, k: (j, 0, 0)),
        extra_specs=(),
        out_shapes=[SDS((T, NDEV * ns), out_dtype)],
        out_specs=[Block((tm, ns), lambda j, i, k: (i, j))],
        acc_shape=(tm, ns), dims=NN, nb=None, epilogue=_store())[0]


def _col_bwd_a(name, dy, w, add=None):
    T = dy.shape[0]
    _, K, ns = w.shape
    tm = _tile(T, 1024)

    def ep(accs, ex, outs):
        r = accs[0]
        if add is not None:
            r = r + ex[0][...].astype(F32)
        outs[0][...] = r.astype(BF16)

    extras = () if add is None else (add,)
    return _matmul(
        name, dy, w, extras, grid=(T // tm, 1, NDEV),
        a_spec=Block((tm, ns), lambda i, j, k: (i, k)),
        b_spec=Block((None, K, ns), lambda i, j, k: (k, 0, 0)),
        extra_specs=[Block((tm, K), lambda i, j, k: (i, 0))] * len(extras),
        out_shapes=[SDS((T, K), BF16)],
        out_specs=[Block((tm, K), lambda i, j, k: (i, 0))],
        acc_shape=(tm, K), dims=NT, nb=None, epilogue=ep)[0]


def _col_bwd_w(name, a, dy, ns):
    T, K = a.shape
    tt, tr = _tile(T, 1024), _tile(K, 1024)
    return _matmul(
        name, a, dy, (), grid=(NDEV, K // tr, T // tt),
        a_spec=Block((tt, tr), lambda j, n, k: (k, n)),
        b_spec=Block((tt, ns), lambda j, n, k: (k, j)),
        extra_specs=(),
        out_shapes=[SDS((NDEV, K, ns), BF16)],
        out_specs=[Block((None, tr, ns), lambda j, n, k: (j, n, 0))],
        acc_shape=(tr, ns), dims=TN, nb=None, epilogue=_store())[0]


def _gate_fwd(name, h, w, bias):
    T, K = h.shape
    ns = w.shape[2]
    per = D_MODEL // ns
    tm = _tile(T, 1024)

    def ep(accs, ex, outs):
        outs[0][...] = (accs[0] + ex[0][...]).astype(BF16)

    return _matmul(
        name, h, w, (bias,), grid=(NDEV, T // tm, 1),
        a_spec=Block((tm, K), lambda j, i, k: (i, 0)),
        b_spec=Block((None, K, ns), lambda j, i, k: (j, 0, 0)),
        extra_specs=[Block((None, 1, ns), lambda j, i, k: (j // per, 0, j % per))],
        out_shapes=[SDS((2, T, D_MODEL), BF16)],
        out_specs=[Block((None, tm, ns), lambda j, i, k: (j // per, i, j % per))],
        acc_shape=(tm, ns), dims=NN, nb=None, epilogue=ep)[0]


def _gate_bwd_a(name, dgl, w):
    _, T, _ = dgl.shape
    _, K, ns = w.shape
    per = D_MODEL // ns
    tm = _tile(T, 1024)
    return _matmul(
        name, dgl, w, (), grid=(T // tm, 1, NDEV),
        a_spec=Block((None, tm, ns), lambda i, j, k: (k // per, i, k % per)),
        b_spec=Block((None, K, ns), lambda i, j, k: (k, 0, 0)),
        extra_specs=(),
        out_shapes=[SDS((T, K), BF16)],
        out_specs=[Block((tm, K), lambda i, j, k: (i, 0))],
        acc_shape=(tm, K), dims=NT, nb=None, epilogue=_store())[0]


def _gate_bwd_w(name, h, dgl, ns):
    T, K = h.shape
    per = D_MODEL // ns
    tt, tr = _tile(T, 1024), 1024
    return _matmul(
        name, h, dgl, (), grid=(NDEV, K // tr, T // tt),
        a_spec=Block((tt, tr), lambda j, n, k: (k, n)),
        b_spec=Block((None, tt, ns), lambda j, n, k: (j // per, k, j % per)),
        extra_specs=(),
        out_shapes=[SDS((NDEV, K, ns), BF16)],
        out_specs=[Block((None, tr, ns), lambda j, n, k: (j, n, 0))],
        acc_shape=(tr, ns), dims=TN, nb=None, epilogue=_store())[0]


def _plain_fwd_res(name, a, w, xres):
    T, K = a.shape
    N = w.shape[1]
    tm, tn = _tile(T, 1024), _tile(N, 1024)

    def ep(accs, ex, outs):
        outs[0][...] = ex[0][...] + accs[0]

    return _matmul(
        name, a, w, (xres,), grid=(T // tm, N // tn, 1),
        a_spec=Block((tm, K), lambda i, j, k: (i, 0)),
        b_spec=Block((K, tn), lambda i, j, k: (0, j)),
        extra_specs=[Block((tm, tn), lambda i, j, k: (i, j))],
        out_shapes=[SDS((T, N), F32)],
        out_specs=[Block((tm, tn), lambda i, j, k: (i, j))],
        acc_shape=(tm, tn), dims=NN, nb=None, epilogue=ep)[0]


def _plain_bwd_a(name, dy, w):
    T, N = dy.shape
    K = w.shape[0]
    tm, tn = _tile(T, 1024), _tile(K, 1024)
    return _matmul(
        name, dy, w, (), grid=(T // tm, K // tn, 1),
        a_spec=Block((tm, N), lambda i, j, k: (i, 0)),
        b_spec=Block((tn, N), lambda i, j, k: (j, 0)),
        extra_specs=(),
        out_shapes=[SDS((T, K), BF16)],
        out_specs=[Block((tm, tn), lambda i, j, k: (i, j))],
        acc_shape=(tm, tn), dims=NT, nb=None, epilogue=_store())[0]


def _plain_bwd_w(name, a, dy):
    T, K = a.shape
    N = dy.shape[1]
    tt, tr, tn = _tile(T, 1024), _tile(K, 1024), _tile(N, 1024)
    return _matmul(
        name, a, dy, (), grid=(K // tr, N // tn, T // tt),
        a_spec=Block((tt, tr), lambda m, n, k: (k, m)),
        b_spec=Block((tt, tn), lambda m, n, k: (k, n)),
        extra_specs=(),
        out_shapes=[SDS((K, N), BF16)],
        out_specs=[Block((tr, tn), lambda m, n, k: (m, n))],
        acc_shape=(tr, tn), dims=TN, nb=None, epilogue=_store())[0]


def _rms_fwd(name, x, g):
    T, D = x.shape
    tm = _tile(T, 512)

    def body(x_ref, g_ref, h_ref):
        xv = x_ref[...]
        r = lax.rsqrt(jnp.mean(xv * xv, axis=-1, keepdims=True) + NORM_EPS)
        h_ref[...] = (xv * r * g_ref[...]).astype(BF16)

    return pl.pallas_call(
        body, name=name, grid=(T // tm,),
        in_specs=[Block((tm, D), lambda i: (i, 0)), Block((1, D), lambda i: (0, 0))],
        out_specs=Block((tm, D), lambda i: (i, 0)), out_shape=SDS((T, D), BF16),
        compiler_params=_cparams(("arbitrary",), VMEM_BIG))(x, g)


def _rms_bwd(name, dh, x, g, dxin):
    T, D = x.shape
    tm = _tile(T, 512)

    def body(dh_ref, x_ref, g_ref, dxin_ref, dx_ref, dxb_ref, dg_ref):
        i = pl.program_id(0)
        xv = x_ref[...]
        dh = dh_ref[...].astype(F32)
        r = lax.rsqrt(jnp.mean(xv * xv, axis=-1, keepdims=True) + NORM_EPS)
        xh = xv * r
        gd = dh * g_ref[...]
        dx = dxin_ref[...] + r * (gd - xh * jnp.mean(gd * xh, axis=-1, keepdims=True))
        dx_ref[...] = dx
        dxb_ref[...] = dx.astype(BF16)
        dgp = jnp.sum(dh * xh, axis=0, keepdims=True)

        @pl.when(i == 0)
        def _():
            dg_ref[...] = dgp

        @pl.when(i > 0)
        def _():
            dg_ref[...] += dgp

    row = Block((tm, D), lambda i: (i, 0))
    vec = Block((1, D), lambda i: (0, 0))
    return pl.pallas_call(
        body, name=name, grid=(T // tm,),
        in_specs=[row, row, vec, row], out_specs=[row, row, vec],
        out_shape=[SDS((T, D), F32), SDS((T, D), BF16), SDS((1, D), F32)],
        compiler_params=_cparams(("arbitrary",), VMEM_BIG))(dh, x, g, dxin)


def _loss_head(name, x, g, tgt):
    T, D = x.shape
    tm = _tile(T, 512)

    def body(x_ref, g_ref, t_ref, loss_ref, dx_ref, dxb_ref, dg_ref):
        i = pl.program_id(0)
        xv = x_ref[...]
        gv = g_ref[...]
        r = lax.rsqrt(jnp.mean(xv * xv, axis=-1, keepdims=True) + NORM_EPS)
        xh = xv * r
        err = xh * gv - t_ref[...]
        lp = 0.5 * jnp.sum(jnp.mean(err * err, axis=-1, keepdims=True), axis=0, keepdims=True)
        dout = err * (1.0 / D)
        gd = dout * gv
        dx = r * (gd - xh * jnp.mean(gd * xh, axis=-1, keepdims=True))
        dx_ref[...] = dx
        dxb_ref[...] = dx.astype(BF16)
        dgp = jnp.sum(dout * xh, axis=0, keepdims=True)
        lpb = jnp.broadcast_to(lp, (1, 128))

        @pl.when(i == 0)
        def _():
            dg_ref[...] = dgp
            loss_ref[...] = lpb

        @pl.when(i > 0)
        def _():
            dg_ref[...] += dgp
            loss_ref[...] += lpb

    row = Block((tm, D), lambda i: (i, 0))
    vec = Block((1, D), lambda i: (0, 0))
    return pl.pallas_call(
        body, name=name, grid=(T // tm,),
        in_specs=[row, vec, row], out_specs=[Block((1, 128), lambda i: (0, 0)), row, row, vec],
        out_shape=[SDS((1, 128), F32), SDS((T, D), F32), SDS((T, D), BF16), SDS((1, D), F32)],
        compiler_params=_cparams(("arbitrary",), VMEM_BIG))(x, g, tgt)


def _merge_fwd(name, pa, pb, gl):
    T, D = pa.shape
    tm = _tile(T, 512)

    def body(pa_ref, pb_ref, gl_ref, o_ref):
        ga = _sigmoid(gl_ref[0].astype(F32))
        gb = _sigmoid(gl_ref[1].astype(F32))
        o_ref[...] = (ga * pa_ref[...].astype(F32) + gb * pb_ref[...].astype(F32)).astype(BF16)

    row = Block((tm, D), lambda i: (i, 0))
    return pl.pallas_call(
        body, name=name, grid=(T // tm,),
        in_specs=[row, row, Block((2, tm, D), lambda i: (0, i, 0))], out_specs=row,
        out_shape=SDS((T, D), BF16), compiler_params=_cparams(("arbitrary",), VMEM_BIG))(pa, pb, gl)


def _merge_bwd(name, dm, pa, pb, gl):
    T, D = pa.shape
    tm = _tile(T, 512)

    def body(dm_ref, pa_ref, pb_ref, gl_ref, dpa_ref, dpb_ref, dgl_ref, db_ref):
        i = pl.program_id(0)
        dmv = dm_ref[...].astype(F32)
        ga = _sigmoid(gl_ref[0].astype(F32))
        gb = _sigmoid(gl_ref[1].astype(F32))
        dpa_ref[...] = (dmv * ga).astype(BF16)
        dpb_ref[...] = (dmv * gb).astype(BF16)
        dga = dmv * pa_ref[...].astype(F32) * ga * (1.0 - ga)
        dgb = dmv * pb_ref[...].astype(F32) * gb * (1.0 - gb)
        dgl_ref[0] = dga.astype(BF16)
        dgl_ref[1] = dgb.astype(BF16)
        sa = jnp.sum(dga, axis=0, keepdims=True)
        sb = jnp.sum(dgb, axis=0, keepdims=True)

        @pl.when(i == 0)
        def _():
            db_ref[0] = sa
            db_ref[1] = sb

        @pl.when(i > 0)
        def _():
            db_ref[0] += sa
            db_ref[1] += sb

    row = Block((tm, D), lambda i: (i, 0))
    two = Block((2, tm, D), lambda i: (0, i, 0))
    return pl.pallas_call(
        body, name=name, grid=(T // tm,),
        in_specs=[row, row, row, two], out_specs=[row, row, two, Block((2, 1, D), lambda i: (0, 0, 0))],
        out_shape=[SDS((T, D), BF16), SDS((T, D), BF16), SDS((2, T, D), BF16), SDS((2, 1, D), F32)],
        compiler_params=_cparams(("arbitrary",), VMEM_BIG))(dm, pa, pb, gl)


def _sgu_core(ur, vr, lng, lnb, ws_ref, bs_ref):
    tm = ur.shape[0]
    gu = _gelu(ur)
    gv = _gelu(vr)
    mu = jnp.mean(gv, axis=-1, keepdims=True)
    cen = gv - mu
    rstd = lax.rsqrt(jnp.mean(cen * cen, axis=-1, keepdims=True) + NORM_EPS)
    xhat = cen * rstd
    vn = (xhat * lng + lnb).astype(BF16)
    rows = []
    for n in range(tm // MLP_CHUNK):
        cols = []
        for h in range(SGU_HEADS):
            blk = vn[n * MLP_CHUNK:(n + 1) * MLP_CHUNK, h * 128:(h + 1) * 128]
            cols.append(_dot(ws_ref[h], blk) + bs_ref[h])
        rows.append(jnp.concatenate(cols, axis=1))
    mixed = jnp.concatenate(rows, axis=0) if len(rows) > 1 else rows[0]
    return gu, xhat, rstd, vn, mixed


def _sgu_fwd(name, proj, lng, lnb, wsm, bst):
    T = proj.shape[0]
    W = SGU_WIDTH
    tm = _tile(T, 512)

    def body(u_ref, v_ref, lng_ref, lnb_ref, ws_ref, bs_ref, o_ref):
        gu, _, _, _, mixed = _sgu_core(u_ref[...].astype(F32), v_ref[...].astype(F32), lng_ref[...], lnb_ref[...],
                                       ws_ref, bs_ref)
        o_ref[...] = (gu * mixed).astype(BF16)

    vec = Block((1, W), lambda i: (0, 0))
    return pl.pallas_call(
        body, name=name, grid=(T // tm,),
        in_specs=[Block((tm, W), lambda i: (i, 1)), Block((tm, W), lambda i: (i, 2)), vec, vec,
                  Block((SGU_HEADS, 128, 128), lambda i: (0, 0, 0)), Block((SGU_HEADS, 128, 128), lambda i: (0, 0, 0))],
        out_specs=Block((tm, W), lambda i: (i, 0)), out_shape=SDS((T, W), BF16),
        compiler_params=_cparams(("arbitrary",), VMEM_BIG))(proj, proj, lng, lnb, wsm, bst)


def _sgu_bwd(name, dyb, proj, lng, lnb, wsm, wsmt, bst):
    T = proj.shape[0]
    W = SGU_WIDTH
    tm = _tile(T, 512)

    def body(dy_ref, u_ref, v_ref, lng_ref, lnb_ref, ws_ref, wst_ref, bs_ref,
             duv_ref, dws_ref, dbs_ref, dlng_ref, dlnb_ref):
        i = pl.program_id(0)
        ur = u_ref[...].astype(F32)
        vr = v_ref[...].astype(F32)
        lng_v = lng_ref[...]
        gu, xhat, rstd, vn, mixed = _sgu_core(ur, vr, lng_v, lnb_ref[...], ws_ref, bs_ref)
        dy = dy_ref[...].astype(F32)
        dgu = dy * mixed
        dmix = dy * gu
        dmb = dmix.astype(BF16)
        dws_p, dbs_p, rows = [], [], []
        for h in range(SGU_HEADS):
            acc_w = jnp.zeros((128, 128), F32)
            acc_b = jnp.zeros((128, 1), F32)
            for n in range(tm // MLP_CHUNK):
                r0 = n * MLP_CHUNK
                dmt = dmb[r0:r0 + MLP_CHUNK, h * 128:(h + 1) * 128]
                acc_w = acc_w + _dot(dmt, vn[r0:r0 + MLP_CHUNK, h * 128:(h + 1) * 128], NT)
                acc_b = acc_b + jnp.sum(dmix[r0:r0 + MLP_CHUNK, h * 128:(h + 1) * 128], axis=1, keepdims=True)
            dws_p.append(acc_w)
            dbs_p.append(jnp.broadcast_to(acc_b, (128, 128)))
        for n in range(tm // MLP_CHUNK):
            r0 = n * MLP_CHUNK
            rows.append(jnp.concatenate(
                [_dot(wst_ref[h], dmb[r0:r0 + MLP_CHUNK, h * 128:(h + 1) * 128]) for h in range(SGU_HEADS)], axis=1))
        dvn = jnp.concatenate(rows, axis=0) if len(rows) > 1 else rows[0]
        dlng_p = jnp.sum(dvn * xhat, axis=0, keepdims=True)
        dlnb_p = jnp.sum(dvn, axis=0, keepdims=True)
        dxh = dvn * lng_v
        dgv = rstd * (dxh - jnp.mean(dxh, axis=-1, keepdims=True)
                      - xhat * jnp.mean(dxh * xhat, axis=-1, keepdims=True))
        duv_ref[:, :W] = (dgu * _gelu_grad(ur)).astype(BF16)
        duv_ref[:, W:] = (dgv * _gelu_grad(vr)).astype(BF16)

        @pl.when(i == 0)
        def _():
            for h in range(SGU_HEADS):
                dws_ref[h] = dws_p[h]
                dbs_ref[h] = dbs_p[h]
            dlng_ref[...] = dlng_p
            dlnb_ref[...] = dlnb_p

        @pl.when(i > 0)
        def _():
            for h in range(SGU_HEADS):
                dws_ref[h] += dws_p[h]
                dbs_ref[h] += dbs_p[h]
            dlng_ref[...] += dlng_p
            dlnb_ref[...] += dlnb_p

    vec = Block((1, W), lambda i: (0, 0))
    wsb = Block((SGU_HEADS, 128, 128), lambda i: (0, 0, 0))
    hsq = SDS((SGU_HEADS, 128, 128), F32)
    return pl.pallas_call(
        body, name=name, grid=(T // tm,),
        in_specs=[Block((tm, W), lambda i: (i, 0)), Block((tm, W), lambda i: (i, 1)), Block((tm, W), lambda i: (i, 2)),
                  vec, vec, wsb, wsb, wsb],
        out_specs=[Block((tm, 2 * W), lambda i: (i, 0)), wsb, wsb, vec, vec],
        out_shape=[SDS((T, 2 * W), BF16), hsq, hsq, SDS((1, W), F32), SDS((1, W), F32)],
        compiler_params=_cparams(("arbitrary",), VMEM_BIG))(dyb, proj, proj, lng, lnb, wsm, wsmt, bst)


def _s5_disc(lr, li, ldt, brt, bit):
    dt = jnp.exp(ldt)
    decay = jnp.exp(lr * dt)
    abr = decay * jnp.cos(li * dt)
    abi = decay * jnp.sin(li * dt)
    dec_s = jnp.exp(lr * dt * S5_SEG)
    asr = dec_s * jnp.cos(li * dt * S5_SEG)
    asi = dec_s * jnp.sin(li * dt * S5_SEG)
    denom = lr * lr + li * li
    nr = abr - 1.0
    ni = abi
    kr = (nr * lr + ni * li) / denom
    ki = (ni * lr - nr * li) / denom
    bkr = kr[None] * brt - ki[None] * bit
    bki = kr[None] * bit + ki[None] * brt
    return abr, abi, asr, asi, bkr, bki


def _s5_prep(lr, li, ldt, brt, bit):
    G, P, C = S5_GROUPS, S5_STATE, S5_GROUP_WIDTH

    def body(lr_ref, li_ref, ldt_ref, br_ref, bi_ref, abr_ref, abi_ref, asr_ref, asi_ref, bkr_ref, bki_ref):
        res = _s5_disc(lr_ref[...], li_ref[...], ldt_ref[...], br_ref[...], bi_ref[...])
        for o, r in zip((abr_ref, abi_ref, asr_ref, asi_ref, bkr_ref, bki_ref), res):
            o[...] = r

    gp = SDS((G, P), F32)
    cgp = SDS((C, G, P), F32)
    return pl.pallas_call(body, name="s5_prep", out_shape=[gp, gp, gp, gp, cgp, cgp])(lr, li, ldt, brt, bit)


def _s5_prep_bwd(lr, li, ldt, brt, bit, dabr, dabi, dbkr, dbki):
    G, P, C = S5_GROUPS, S5_STATE, S5_GROUP_WIDTH

    def body(lr_ref, li_ref, ldt_ref, br_ref, bi_ref, dabr_ref, dabi_ref, dbkr_ref, dbki_ref,
             o_lr, o_li, o_ldt, o_br, o_bi):
        def f(lr_, li_, ldt_, br_, bi_):
            abr, abi, _, _, bkr, bki = _s5_disc(lr_, li_, ldt_, br_, bi_)
            return abr, abi, bkr, bki

        _, pull = jax.vjp(f, lr_ref[...], li_ref[...], ldt_ref[...], br_ref[...], bi_ref[...])
        g = pull((dabr_ref[...], dabi_ref[...], dbkr_ref[...], dbki_ref[...]))
        for o, r in zip((o_lr, o_li, o_ldt, o_br, o_bi), g):
            o[...] = r

    gp = SDS((G, P), F32)
    cgp = SDS((C, G, P), F32)
    return pl.pallas_call(body, name="s5_prep_bwd", out_shape=[gp, gp, SDS((G, 1), F32), cgp, cgp])(
        lr, li, ldt, brt, bit, dabr, dabi, dbkr, dbki)


def _s5_scan(buf_ref, ar_row, ai_row, asr_row, asi_row, carry_ref, LG, xs_ref=None, dar_ref=None, dai_ref=None):
    reverse = xs_ref is not None
    NS, SEG = S5_NS, S5_SEG
    sgn = -1.0 if reverse else 1.0
    for lg in range(NS // LG):
        cr = slice(lg * LG, (lg + 1) * LG)
        ci = slice(NS + lg * LG, NS + (lg + 1) * LG)
        ar1, ai1 = ar_row[:, cr], sgn * ai_row[:, cr]
        asr1, asi1 = asr_row[:, cr], sgn * asi_row[:, cr]
        ar = jnp.broadcast_to(ar1, (8, LG))
        ai = jnp.broadcast_to(ai1, (8, LG))

        def step_of(j):
            return (SEG - 1 - j) if reverse else j

        def p1(j, st):
            sr, si = st
            rows = pl.ds(pl.multiple_of(step_of(j) * 8, 8), 8)
            nr = ar * sr - ai * si + buf_ref[rows, cr]
            ni = ar * si + ai * sr + buf_ref[rows, ci]
            buf_ref[rows, cr] = nr
            buf_ref[rows, ci] = ni
            return nr, ni

        z = jnp.zeros((8, LG), F32)
        er, ei = lax.fori_loop(0, SEG, p1, (z, z), unroll=2)
        c_r = carry_ref[:, cr]
        c_i = carry_ref[:, ci]
        cs_r, cs_i = [None] * 8, [None] * 8
        order = range(7, -1, -1) if reverse else range(8)
        for s in order:
            cs_r[s], cs_i[s] = c_r, c_i
            e_r, e_i = er[s:s + 1], ei[s:s + 1]
            c_r, c_i = e_r + asr1 * c_r - asi1 * c_i, e_i + asr1 * c_i + asi1 * c_r
        carry_ref[:, cr] = c_r
        carry_ref[:, ci] = c_i
        cmr = jnp.concatenate(cs_r, axis=0)
        cmi = jnp.concatenate(cs_i, axis=0)

        if not reverse:
            def p2(j, st):
                wr, wi = st
                rows = pl.ds(pl.multiple_of(j * 8, 8), 8)
                nwr = ar * wr - ai * wi
                nwi = ar * wi + ai * wr
                buf_ref[rows, cr] += nwr
                buf_ref[rows, ci] += nwi
                return nwr, nwi

            lax.fori_loop(0, SEG, p2, (cmr, cmi), unroll=2)
        else:
            def p2(j, st):
                wr, wi, pr, pi, dr, di = st
                rows = pl.ds(pl.multiple_of(step_of(j) * 8, 8), 8)
                xr = xs_ref[rows, cr]
                xi = xs_ref[rows, ci]
                dr = dr + pr * xr + pi * xi
                di = di + pi * xr - pr * xi
                nwr = ar * wr - ai * wi
                nwi = ar * wi + ai * wr
                gr = buf_ref[rows, cr] + nwr
                gi = buf_ref[rows, ci] + nwi
                buf_ref[rows, cr] = gr
                buf_ref[rows, ci] = gi
                return nwr, nwi, gr, gi, dr, di

            st = lax.fori_loop(0, SEG, p2, (cmr, cmi, cmr, cmi, z, z), unroll=2)
            dar_ref[:, cr] += st[4]
            dai_ref[:, cr] += st[5]


def _s5_fwd(proj, perm, permt, bdbr, bdbi, bdcr, bdci, abr, abi, asr, asi, dvec, wglu, bglu):
    T = proj.shape[0]
    TC, NS, W = S5_TC, S5_NS, S5_WIDTH
    nc = T // TC

    def body(u_ref, pm_ref, pmt_ref, bdbr_ref, bdbi_ref, bdcr_ref, bdci_ref, ar_ref, ai_ref, asr_ref, asi_ref,
             d_ref, wglu_ref, bglu_ref, ya_ref, xs_ref, ypre_ref, carry_ref):
        i = pl.program_id(0)

        @pl.when(i == 0)
        def _():
            carry_ref[...] = jnp.zeros_like(carry_ref)

        up = _dot(pm_ref[...], u_ref[...]).astype(BF16)
        for j in range(8):
            ut = up[:, j * 128:(j + 1) * 128]
            xs_ref[:, j * 512:(j + 1) * 512] = _dot(ut, bdbr_ref[j])
            xs_ref[:, NS + j * 512:NS + (j + 1) * 512] = _dot(ut, bdbi_ref[j])
        _s5_scan(xs_ref, ar_ref[...], ai_ref[...], asr_ref[...], asi_ref[...], carry_ref, 512)
        ys = []
        for j in range(8):
            xr = xs_ref[:, j * 512:(j + 1) * 512].astype(BF16)
            xi = xs_ref[:, NS + j * 512:NS + (j + 1) * 512].astype(BF16)
            ys.append(_dot(xr, bdcr_ref[j]) + _dot(xi, bdci_ref[j]))
        ypre = jnp.concatenate(ys, axis=1) + d_ref[...] * up.astype(F32)
        ypre_ref[...] = ypre
        ya = _gelu(ypre)
        zl = _dot(ya.astype(BF16), wglu_ref[...]) + bglu_ref[...]
        outp = (ya * _sigmoid(zl)).astype(BF16)
        ya_ref[...] = _dot(pmt_ref[...], outp).astype(BF16)

    return pl.pallas_call(
        body, name="s5_fwd", grid=(nc,),
        in_specs=[Block((TC, W), lambda i: (i, 0)), _const((TC, TC)), _const((TC, TC)),
                  _const((8, 128, 512)), _const((8, 128, 512)), _const((8, 512, 128)), _const((8, 512, 128)),
                  _const((1, NS)), _const((1, NS)), _const((1, NS)), _const((1, NS)),
                  _const((1, W)), _const((W, W)), _const((1, W))],
        out_specs=[Block((TC, W), lambda i: (i, 0)), Block((TC, 2 * NS), lambda i: (i, 0)),
                   Block((TC, W), lambda i: (i, 0))],
        out_shape=[SDS((T, W), BF16), SDS((T, 2 * NS), F32), SDS((T, W), F32)],
        scratch_shapes=[pltpu.VMEM((1, 2 * NS), F32)],
        compiler_params=_cparams(("arbitrary",), VMEM_BIG),
    )(proj, perm, permt, bdbr, bdbi, bdcr, bdci, abr, abi, asr, asi, dvec, wglu, bglu)


def _s5_bwd(dya, proj, ypre, xs, perm, permt, bdbr, bdbi, bdcr, bdci, abr, abi, asr, asi, dvec, wglu, bglu):
    T = proj.shape[0]
    TC, NS, W = S5_TC, S5_NS, S5_WIDTH
    nc = T // TC

    def body(dya_ref, u_ref, ypre_ref, xs_ref, pm_ref, pmt_ref, bdbr_ref, bdbi_ref, bdcr_ref, bdci_ref,
             ar_ref, ai_ref, asr_ref, asi_ref, d_ref, wglu_ref, bglu_ref,
             du_ref, dar_ref, dai_ref, dd_ref, dbglu_ref, o_dbdbr, o_dbdbi, o_dbdcr, o_dbdci, o_dwglu,
             g_ref, carry_ref, dbdbr_ref, dbdbi_ref, dbdcr_ref, dbdci_ref, dwglu_ref):
        i = pl.program_id(0)

        @pl.when(i == 0)
        def _():
            carry_ref[...] = jnp.zeros_like(carry_ref)
            for r in (dbdbr_ref, dbdbi_ref, dbdcr_ref, dbdci_ref, dar_ref, dai_ref, dd_ref, dwglu_ref, dbglu_ref):
                r[...] = jnp.zeros_like(r)

        pm = pm_ref[...]
        dyo = _dot(pm, dya_ref[...])
        up = _dot(pm, u_ref[...]).astype(BF16)
        upf = up.astype(F32)
        ypre_v = ypre_ref[...]
        ya = _gelu(ypre_v)
        yab = ya.astype(BF16)
        sg = _sigmoid(_dot(yab, wglu_ref[...]) + bglu_ref[...])
        dz = dyo * ya * sg * (1.0 - sg)
        dzb = dz.astype(BF16)
        dya_t = dyo * sg + _dot(dzb, wglu_ref[...], NT)
        dwglu_ref[...] += _dot(yab, dzb, TN)
        dbglu_ref[...] += jnp.sum(dz, axis=0, keepdims=True)
        dy = dya_t * _gelu_grad(ypre_v)
        dd_ref[...] += jnp.sum(dy * upf, axis=0, keepdims=True)
        dyb = dy.astype(BF16)
        for j in range(8):
            dyj = dyb[:, j * 128:(j + 1) * 128]
            g_ref[:, j * 512:(j + 1) * 512] = _dot(dyj, bdcr_ref[j], NT)
            g_ref[:, NS + j * 512:NS + (j + 1) * 512] = _dot(dyj, bdci_ref[j], NT)
            dbdcr_ref[j] += _dot(xs_ref[:, j * 512:(j + 1) * 512].astype(BF16), dyj, TN)
            dbdci_ref[j] += _dot(xs_ref[:, NS + j * 512:NS + (j + 1) * 512].astype(BF16), dyj, TN)
        _s5_scan(g_ref, ar_ref[...], ai_ref[...], asr_ref[...], asi_ref[...], carry_ref, 256,
                 xs_ref=xs_ref, dar_ref=dar_ref, dai_ref=dai_ref)
        dus = []
        for j in range(8):
            ut = up[:, j * 128:(j + 1) * 128]
            gr = g_ref[:, j * 512:(j + 1) * 512].astype(BF16)
            gi = g_ref[:, NS + j * 512:NS + (j + 1) * 512].astype(BF16)
            dbdbr_ref[j] += _dot(ut, gr, TN)
            dbdbi_ref[j] += _dot(ut, gi, TN)
            dus.append(_dot(gr, bdbr_ref[j], NT) + _dot(gi, bdbi_ref[j], NT))
        dup = jnp.concatenate(dus, axis=1) + d_ref[...] * dy
        du_ref[...] = _dot(pmt_ref[...], dup.astype(BF16)).astype(BF16)

        @pl.when(i == nc - 1)
        def _():
            for src, dst in ((dbdbr_ref, o_dbdbr), (dbdbi_ref, o_dbdbi), (dbdcr_ref, o_dbdcr),
                             (dbdci_ref, o_dbdci), (dwglu_ref, o_dwglu)):
                pltpu.sync_copy(src, dst)

    c2 = lambda i: (0, 0)
    rev = lambda i: (nc - 1 - i, 0)
    return pl.pallas_call(
        body, name="s5_bwd", grid=(nc,),
        in_specs=[Block((TC, W), rev), Block((TC, W), rev), Block((TC, W), rev), Block((TC, 2 * NS), rev),
                  _const((TC, TC)), _const((TC, TC)),
                  _const((8, 128, 512)), _const((8, 128, 512)), _const((8, 512, 128)), _const((8, 512, 128)),
                  _const((1, NS)), _const((1, NS)), _const((1, NS)), _const((1, NS)),
                  _const((1, W)), _const((W, W)), _const((1, W))],
        out_specs=[Block((TC, W), rev), Block((8, NS), c2), Block((8, NS), c2), Block((1, W), c2), Block((1, W), c2),
                   ANY, ANY, ANY, ANY, ANY],
        out_shape=[SDS((T, W), BF16), SDS((8, NS), F32), SDS((8, NS), F32), SDS((1, W), F32), SDS((1, W), F32),
                   SDS((8, 128, 512), F32), SDS((8, 128, 512), F32),
                   SDS((8, 512, 128), F32), SDS((8, 512, 128), F32), SDS((W, W), F32)],
        scratch_shapes=[pltpu.VMEM((TC, 2 * NS), F32), pltpu.VMEM((1, 2 * NS), F32),
                        pltpu.VMEM((8, 128, 512), F32), pltpu.VMEM((8, 128, 512), F32),
                        pltpu.VMEM((8, 512, 128), F32), pltpu.VMEM((8, 512, 128), F32), pltpu.VMEM((W, W), F32)],
        compiler_params=_cparams(("arbitrary",), VMEM_BIG),
    )(dya, proj, ypre, xs, perm, permt, bdbr, bdbi, bdcr, bdci, abr, abi, asr, asi, dvec, wglu, bglu)


def _bd_b(bk_t):
    C, P = S5_GROUP_WIDTH, S5_STATE
    t = jnp.transpose(bk_t, (1, 0, 2)).reshape(8, 8, C, P)
    eye = jnp.eye(8, dtype=t.dtype)
    return (t[:, :, :, None, :] * eye[None, :, None, :, None]).reshape(8, 8 * C, 8 * P)


def _bd_b_extract(m):
    C, P = S5_GROUP_WIDTH, S5_STATE
    t = m.reshape(8, 8, C, 8, P)
    d = jnp.stack([t[:, g, :, g, :] for g in range(8)], axis=1)
    return jnp.transpose(d.reshape(S5_GROUPS, C, P), (1, 0, 2))


def _bd_c(c):
    C, P = S5_GROUP_WIDTH, S5_STATE
    t = jnp.transpose(c, (0, 2, 1)).reshape(8, 8, P, C)
    eye = jnp.eye(8, dtype=t.dtype)
    return (t[:, :, :, None, :] * eye[None, :, None, :, None]).reshape(8, 8 * P, 8 * C)


def _bd_c_extract(m):
    C, P = S5_GROUP_WIDTH, S5_STATE
    t = m.reshape(8, 8, P, 8, C)
    d = jnp.stack([t[:, g, :, g, :] for g in range(8)], axis=1)
    return jnp.transpose(d.reshape(S5_GROUPS, P, C), (0, 2, 1))


def _perm_matrix():
    r = jnp.arange(S5_TC)
    src = (r % 8) * S5_SEG + r // 8
    return (src[:, None] == jnp.arange(S5_TC)[None, :]).astype(BF16)


def _coords():
    return lax.axis_index("x"), lax.axis_index("y"), lax.axis_index("c")


def _all_gather(name, arrs):
    n = len(arrs)

    def body(*refs):
        ins, outs = refs[:n], refs[n:2 * n]
        send_sems, recv_sems, local_sems = refs[2 * n:]
        x, y, c = _coords()
        me, sibling = (x, y, c), (x, y, 1 - c)
        chips = [(1 - x, y), (x, 1 - y), (1 - x, 1 - y)]

        def slot(p):
            return 4 * p[0] + 2 * p[1] + p[2]

        def copy(a, k, block, to, src=None):
            dst = outs[a].at[slot(block)]
            return pltpu.make_async_remote_copy(
                src_ref=dst if src is None else src, dst_ref=dst,
                send_sem=send_sems.at[a * 7 + k], recv_sem=recv_sems.at[a * 7 + k],
                device_id=to, device_id_type=MESH)

        mine = [pltpu.make_async_copy(ins[a], outs[a].at[slot(me)], local_sems.at[a]) for a in range(n)]
        for m in mine:
            m.start()
        first = []
        for a in range(n):
            first.append(copy(a, 0, me, sibling, src=ins[a]))
            first += [copy(a, 1 + j, me, (*chip, c), src=ins[a]) for j, chip in enumerate(chips)]
        for cp in first:
            cp.start()
        passed = []
        for j, chip in enumerate(chips):
            for a in range(n):
                copy(a, 1 + j, (*chip, c), me).wait_recv()
                fw = copy(a, 4 + j, (*chip, c), sibling)
                fw.start()
                passed.append(fw)
        for a in range(n):
            copy(a, 0, sibling, me).wait_recv()
            for j, chip in enumerate(chips):
                copy(a, 4 + j, (*chip, 1 - c), me).wait_recv()
        for cp in first + passed:
            cp.wait_send()
        for m in mine:
            m.wait()

    return pl.pallas_call(
        body, name=name,
        in_specs=[ANY] * n, out_specs=[ANY] * n,
        out_shape=[SDS((NDEV,) + a.shape, a.dtype) for a in arrs],
        scratch_shapes=[pltpu.SemaphoreType.DMA((7 * n,)), pltpu.SemaphoreType.DMA((7 * n,)),
                        pltpu.SemaphoreType.DMA((n,))],
    )(*arrs)


HBM = pl.BlockSpec(memory_space=pltpu.HBM)
SEM = pl.BlockSpec(memory_space=pltpu.SEMAPHORE)
EFFECT = pltpu.SideEffectType.DATAFLOW_SIDE_EFFECTING


def _peers7(x, y, c):
    return [(1 - x if fx else x, 1 - y if fy else y, 1 - c if fc else c)
            for fx in (0, 1) for fy in (0, 1) for fc in (0, 1) if fx or fy or fc]


def _slot(p):
    return 4 * p[0] + 2 * p[1] + p[2]


def _split_copies(src_refs, land_refs, send_sems, recv_sems, gather, mine):
    x, y, c = _coords()
    me = (x, y, c)
    out = []
    for a, (src, land) in enumerate(zip(src_refs, land_refs)):
        for k, p in enumerate(_peers7(x, y, c)):
            s = src if gather else src.at[_slot(p)]
            out.append(pltpu.make_async_remote_copy(
                src_ref=s, dst_ref=land.at[_slot(me) if mine else _slot(p)],
                send_sem=send_sems.at[a * 7 + k], recv_sem=recv_sems.at[a * 7 + k],
                device_id=p, device_id_type=MESH))
    return out


def _own_slab(shard):
    x, y, c = _coords()
    z = lax.empty((NDEV,) + shard.shape, shard.dtype)
    return lax.dynamic_update_slice(z, shard[None], (_slot((x, y, c)),) + (0,) * shard.ndim)


def _split_start(name, srcs, lands, gather):
    n = len(srcs)

    def body(*refs):
        src_refs, land_refs = refs[:n], refs[n:2 * n]
        send_sems, recv_sems = refs[2 * n], refs[2 * n + 1]
        token = refs[-1]
        for cp in _split_copies(src_refs, land_refs, send_sems, recv_sems, gather, True):
            cp.start()
        token[...] = jnp.zeros_like(token)

    thru = [pltpu.HBM(a.shape, a.dtype) for a in list(srcs) + list(lands)]
    res = pl.pallas_call(
        body, name=name,
        out_shape=(pltpu.SemaphoreType.DMA((7 * n,)), pltpu.SemaphoreType.DMA((7 * n,)), *thru, SDS((8, 128), F32)),
        in_specs=[HBM] * (2 * n),
        out_specs=(SEM, SEM, *([HBM] * (2 * n)), pl.BlockSpec(memory_space=pltpu.VMEM)),
        input_output_aliases={i: 2 + i for i in range(2 * n)},
        compiler_params=pltpu.CompilerParams(has_side_effects=EFFECT),
    )(*[pltpu.with_memory_space_constraint(a, pltpu.HBM) for a in list(srcs) + list(lands)])
    return res[0], res[1], list(res[2:2 + n]), list(res[2 + n:2 + 2 * n]), res[-1]


def _split_wait(name, started, after, gather):
    send_sems, recv_sems, srcs, lands, _ = started
    n = len(srcs)

    def body(*refs):
        src_refs, land_refs = refs[:n], refs[n:2 * n]
        s_sems, r_sems = refs[2 * n], refs[2 * n + 1]
        for cp in _split_copies(src_refs, land_refs, s_sems, r_sems, gather, False):
            cp.wait_send()
            cp.wait_recv()

    thru = [pltpu.HBM(a.shape, a.dtype) for a in list(srcs) + list(lands)]
    res = pl.pallas_call(
        body, name=name, out_shape=tuple(thru),
        in_specs=[HBM] * (2 * n) + [SEM, SEM, ANY], out_specs=tuple([HBM] * (2 * n)),
        input_output_aliases={i: i for i in range(2 * n)},
        compiler_params=pltpu.CompilerParams(has_side_effects=EFFECT),
    )(*srcs, *lands, send_sems, recv_sems, after)
    return list(res[n:])


def _adam_math(w, g, m, v):
    m = ADAM_B1 * m + (1.0 - ADAM_B1) * g
    v = ADAM_B2 * v + (1.0 - ADAM_B2) * (g * g)
    m_hat = m / (1.0 - ADAM_B1 ** ADAM_STEP)
    v_hat = v / (1.0 - ADAM_B2 ** ADAM_STEP)
    delta = -ADAM_LR * (m_hat / (jnp.sqrt(v_hat) + ADAM_EPS) + ADAM_WD * w)
    return delta, m, v


def _adam_sharded(name, recv, sub, w, m, v):
    R, Cc = w.shape
    tr = max(t for t in range(16, R + 1, 16) if R % t == 0 and t * Cc <= 256 * 1024)

    def body(*refs):
        parts = refs[:NDEV]
        w_ref, m_ref, v_ref, g_out, d_out, m_out, v_out = refs[NDEV:]
        g = parts[0][...].astype(F32)
        for p in parts[1:]:
            g = g + p[...].astype(F32)
        delta, mn, vn = _adam_math(w_ref[...], g, m_ref[...], v_ref[...])
        g_out[...] = g
        d_out[...] = delta
        m_out[...] = mn
        v_out[...] = vn

    if sub is None:
        pspecs = [Block((None, tr, Cc), functools.partial(lambda s, i: (s, i, 0), s)) for s in range(NDEV)]
    else:
        pspecs = [Block((None, None, tr, Cc), functools.partial(lambda s, i: (s, sub, i, 0), s)) for s in range(NDEV)]
    row = Block((tr, Cc), lambda i: (i, 0))
    o = SDS((R, Cc), F32)
    return pl.pallas_call(
        body, name=name, grid=(R // tr,),
        in_specs=pspecs + [row, row, row], out_specs=[row, row, row, row], out_shape=[o, o, o, o],
        compiler_params=_cparams(("arbitrary",), VMEM_BIG))(*([recv] * NDEV), w, m, v)


def _adam_small(parts, w, m, v):
    R = w.shape[0]

    def body(p_ref, w_ref, m_ref, v_ref, g_out, d_out, m_out, v_out):
        g = p_ref[0]
        for s in range(1, NDEV):
            g = g + p_ref[s]
        delta, mn, vn = _adam_math(w_ref[...], g, m_ref[...], v_ref[...])
        g_out[...] = g
        d_out[...] = delta
        m_out[...] = mn
        v_out[...] = vn

    o = SDS((R, 128), F32)
    return pl.pallas_call(body, name="adam_small", out_shape=[o, o, o, o],
                          compiler_params=_cparams(None, VMEM_BIG))(parts, w, m, v)


_SMALL = ["ffn1_norm", "mix_norm", "s5_a_re", "s5_a_im", "s5_log_dt", "s5_b_re", "s5_b_im", "s5_c_re", "s5_c_im",
          "s5_d", "s5_b_glu", "sgu_ln_g", "sgu_ln_b", "sgu_w_s", "sgu_b_s", "b_gate", "ffn2_norm", "final_norm"]
_SHARDED = ["ffn1_w_gate", "ffn1_w_up", "ffn1_w_down", "w_in", "s5_w_glu", "w_branch_a", "w_branch_b", "w_gate",
            "w_out", "ffn2_w_gate", "ffn2_w_up", "ffn2_w_down"]
_ORDER = ["ffn1_norm", "ffn1_w_gate", "ffn1_w_up", "ffn1_w_down", "mix_norm", "w_in", "s5_a_re", "s5_a_im",
          "s5_log_dt", "s5_b_re", "s5_b_im", "s5_c_re", "s5_c_im", "s5_d", "s5_w_glu", "s5_b_glu", "sgu_ln_g",
          "sgu_ln_b", "sgu_w_s", "sgu_b_s", "w_branch_a", "w_branch_b", "w_gate", "b_gate", "w_out", "ffn2_norm",
          "ffn2_w_gate", "ffn2_w_up", "ffn2_w_down", "final_norm"]


def _step(x, tgt, W, M, V):
    T = x.shape[1]
    x0 = x[0]
    tgt0 = tgt[0]
    bf = lambda a: a.astype(BF16)

    def gather_start(name, shards):
        return _split_start(name, shards, [_own_slab(s) for s in shards], True)

    gs1 = gather_start("gather1_start", [jnp.stack([bf(W["ffn1_w_gate"][0]), bf(W["ffn1_w_up"][0])])])
    gs2 = gather_start("gather2_start", [bf(W["ffn1_w_down"][0]), bf(W["w_in"][0]), bf(W["s5_w_glu"][0])])
    gs3 = gather_start("gather3_start", [bf(W["w_gate"][0]), bf(W["w_branch_a"][0]), bf(W["w_branch_b"][0]),
                                         bf(W["w_out"][0])])
    gs4 = gather_start("gather4_start", [jnp.stack([bf(W["ffn2_w_gate"][0]), bf(W["ffn2_w_up"][0])]),
                                         bf(W["ffn2_w_down"][0])])
    started = gs1[4][:1, :1] + gs2[4][:1, :1] + gs3[4][:1, :1] + gs4[4][:1, :1]

    lr_, li_ = W["s5_a_re"][0], W["s5_a_im"][0]
    ldt_ = W["s5_log_dt"][0][:, None]
    brt = jnp.transpose(W["s5_b_re"][0], (2, 0, 1))
    bit = jnp.transpose(W["s5_b_im"][0], (2, 0, 1))
    abr, abi, asr, asi, bkr_t, bki_t = _s5_prep(lr_, li_, ldt_, brt, bit)
    bdbr, bdbi = bf(_bd_b(bkr_t)), bf(_bd_b(bki_t))
    bdcr, bdci = bf(_bd_c(W["s5_c_re"][0])), bf(_bd_c(-W["s5_c_im"][0]))
    flat = lambda a: a.reshape(1, S5_NS)
    s5a = (_perm_matrix(), _perm_matrix().T, bdbr, bdbi, bdcr, bdci, flat(abr), flat(abi), flat(asr), flat(asi),
           W["s5_d"][0].reshape(1, S5_WIDTH))
    blk = jnp.arange(MLP_CHUNK) // CHUNK
    mask = blk[:, None] >= blk[None, :]
    wsm = jnp.where(mask[None], W["sgu_w_s"][0], 0.0)
    wsm_b, wsmt_b = bf(wsm), bf(jnp.transpose(wsm, (0, 2, 1)))
    bst = jnp.broadcast_to(W["sgu_b_s"][0][:, :, None], (SGU_HEADS, MLP_CHUNK, 128))
    bgate2 = W["b_gate"].reshape(2, 1, D_MODEL)

    h1 = _rms_fwd("rms1", x0, W["ffn1_norm"] + started)
    (wgu1,) = _split_wait("gather1_wait", gs1, h1, True)
    ab1, f1 = _ffn_up("ffn1_up", h1, wgu1)
    wd1, win, wglu = _split_wait("gather2_wait", gs2, f1, True)
    wglu = wglu.reshape(S5_WIDTH, S5_WIDTH)
    s5c = s5a + (wglu, W["s5_b_glu"])
    x1 = _ffn_down("ffn1_down", f1, wd1, x0)
    h2 = _rms_fwd("rms2", x1, W["mix_norm"])
    proj = _col_fwd("w_in", h2, win)
    ya, xs, ypre = _s5_fwd(proj, *s5c)
    yb = _sgu_fwd("sgu_fwd", proj, W["sgu_ln_g"], W["sgu_ln_b"], wsm_b, bst)
    wgate, wba, wbb, wout = _split_wait("gather3_wait", gs3, yb, True)
    wout = wout.reshape(D_MODEL, D_MODEL)
    pa = _col_fwd("branch_a", ya, wba)
    pb = _col_fwd("branch_b", yb, wbb)
    gl = _gate_fwd("gate", h2, wgate, bgate2)
    merged = _merge_fwd("merge", pa, pb, gl)
    x2 = _plain_fwd_res("w_out", merged, wout, x1)
    h3 = _rms_fwd("rms3", x2, W["ffn2_norm"])
    wgu2, wd2 = _split_wait("gather4_wait", gs4, h3, True)
    ab2, f2 = _ffn_up("ffn2_up", h3, wgu2)
    x3 = _ffn_down("ffn2_down", f2, wd2, x2)
    loss_p, dx3, dx3b, dgf = _loss_head("loss_head", x3, W["final_norm"].reshape(1, D_MODEL), tgt0)

    def exchange_start(name, grads):
        x_, y_, c_ = _coords()
        me = _slot((x_, y_, c_))
        return _split_start(name, grads, [_own_slab(lax.dynamic_index_in_dim(g, me, 0, keepdims=False))
                                          for g in grads], False)

    dab2 = _ffn_down_bwd_act("ffn2_down_bwd_a", dx3b, wd2, ab2)
    g_wd2 = _ffn_down_bwd_w("ffn2_down_bwd_w", f2, dx3b)
    g_gu2 = _ffn_up_bwd_w("ffn2_up_bwd_w", h3, dab2)
    es1 = exchange_start("exchange1_start", [g_wd2, g_gu2])
    dh3 = _ffn_up_bwd_h("ffn2_up_bwd_h", dab2, wgu2, es1[4])
    dx2, dx2b, dg3 = _rms_bwd("rms3_bwd", dh3, x2, W["ffn2_norm"], dx3)

    dmerged = _plain_bwd_a("w_out_bwd_a", dx2b, wout)
    g_wout = _plain_bwd_w("w_out_bwd_w", merged, dx2b)
    dpa, dpb, dgl, dbgate = _merge_bwd("merge_bwd", dmerged, pa, pb, gl)
    dya = _col_bwd_a("branch_a_bwd_a", dpa, wba)
    g_wba = _col_bwd_w("branch_a_bwd_w", ya, dpa, 256)
    dyb = _col_bwd_a("branch_b_bwd_a", dpb, wbb)
    g_wbb = _col_bwd_w("branch_b_bwd_w", yb, dpb, 256)
    dh2g = _gate_bwd_a("gate_bwd_a", dgl, wgate)
    g_wgate = _gate_bwd_w("gate_bwd_w", h2, dgl, 512)
    duv, dws, dbst, dlng, dlnb = _sgu_bwd("sgu_bwd", dyb, proj, W["sgu_ln_g"], W["sgu_ln_b"], wsm_b, wsmt_b, bst)
    (dua, dar8, dai8, ddv, dbglu, dbdbr, dbdbi, dbdcr, dbdci, g_wglu) = _s5_bwd(dya, proj, ypre, xs, *s5c)
    dproj = jnp.concatenate([dua, duv], axis=1)
    g_win = _col_bwd_w("w_in_bwd_w", h2, dproj, 384)
    g_wout3 = g_wout.reshape(NDEV, D_MODEL // NDEV, D_MODEL)
    g_wglu3 = g_wglu.astype(BF16).reshape(NDEV, S5_WIDTH // NDEV, S5_WIDTH)
    es2 = exchange_start("exchange2_start", [g_wout3, g_wba, g_wbb, g_wgate, g_wglu3, g_win])
    dh2 = _col_bwd_a("w_in_bwd_a", dproj, win, add=dh2g)
    dx1, dx1b, dgm = _rms_bwd("rms2_bwd", dh2, x1, W["mix_norm"] + es2[4][:1, :1], dx2)

    dab1 = _ffn_down_bwd_act("ffn1_down_bwd_a", dx1b, wd1, ab1)
    g_wd1 = _ffn_down_bwd_w("ffn1_down_bwd_w", f1, dx1b)
    g_gu1 = _ffn_up_bwd_w("ffn1_up_bwd_w", h1, dab1)
    es3 = exchange_start("exchange3_start", [g_wd1, g_gu1])
    dh1 = _ffn_up_bwd_h("ffn1_up_bwd_h", dab1, wgu1, es3[4])
    dx0, _, dg1 = _rms_bwd("rms1_bwd", dh1, x0, W["ffn1_norm"], dx1)

    dabr = jnp.sum(dar8, axis=0).reshape(S5_GROUPS, S5_STATE)
    dabi = jnp.sum(dai8, axis=0).reshape(S5_GROUPS, S5_STATE)
    d_lr, d_li, d_ldt, d_brt, d_bit = _s5_prep_bwd(lr_, li_, ldt_, brt, bit, dabr, dabi,
                                                   _bd_b_extract(dbdbr), _bd_b_extract(dbdbi))
    small_g = {
        "ffn1_norm": dg1, "mix_norm": dgm, "ffn2_norm": dg3, "final_norm": dgf,
        "s5_a_re": d_lr, "s5_a_im": d_li, "s5_log_dt": d_ldt,
        "s5_b_re": jnp.transpose(d_brt, (1, 2, 0)), "s5_b_im": jnp.transpose(d_bit, (1, 2, 0)),
        "s5_c_re": _bd_c_extract(dbdcr), "s5_c_im": -_bd_c_extract(dbdci),
        "s5_d": ddv, "s5_b_glu": dbglu, "sgu_ln_g": dlng, "sgu_ln_b": dlnb,
        "sgu_w_s": jnp.where(mask[None], dws, 0.0), "sgu_b_s": dbst[:, :, 0], "b_gate": dbgate,
    }

    sizes = [W[n].size for n in _SMALL]
    total = sum(sizes) + 1
    rows = -(-total // 128)
    rows = -(-rows // 8) * 8
    pad = rows * 128 - total

    def pack(d, extra):
        return jnp.concatenate([d[n].reshape(-1).astype(F32) for n in _SMALL] + [extra, jnp.zeros((pad,), F32)]
                               ).reshape(rows, 128)

    zero1 = jnp.zeros((1,), F32)
    parts = _all_gather("gather_small_grads", [pack(small_g, loss_p[0, :1])])[0]
    sg, sd, sm, sv = _adam_small(parts, pack(W, zero1), pack(M, zero1), pack(V, zero1))

    def unpack(flat2d):
        flat = flat2d.reshape(-1)
        out, off = {}, 0
        for n, s in zip(_SMALL, sizes):
            out[n] = flat[off:off + s].reshape(W[n].shape)
            off += s
        return out, flat[off]

    G, loss = unpack(sg)
    Dl, _ = unpack(sd)
    Mn, _ = unpack(sm)
    Vn, _ = unpack(sv)

    def adam(plan):
        last = None
        for n, recv, sub in plan:
            g, d, mn, vn = _adam_sharded("adam_" + n, recv, sub, W[n][0], M[n][0], V[n][0])
            G[n], Dl[n], Mn[n], Vn[n] = g[None], d[None], mn[None], vn[None]
            last = g
        return last

    r_wd2, r_gu2 = _split_wait("exchange1_wait", es1, sg, False)
    done = adam([("ffn2_w_down", r_wd2, None), ("ffn2_w_gate", r_gu2, 0), ("ffn2_w_up", r_gu2, 1)])
    r_wout, r_wba, r_wbb, r_wgate, r_wglu, r_win = _split_wait("exchange2_wait", es2, done, False)
    done = adam([("w_out", r_wout, None), ("w_branch_a", r_wba, None), ("w_branch_b", r_wbb, None),
                 ("w_gate", r_wgate, None), ("s5_w_glu", r_wglu, None), ("w_in", r_win, None)])
    r_wd1, r_gu1 = _split_wait("exchange3_wait", es3, done, False)
    adam([("ffn1_w_down", r_wd1, None), ("ffn1_w_gate", r_gu1, 0), ("ffn1_w_up", r_gu1, 1)])

    return loss, dx0[None], G, Dl, Mn, Vn


def kernel(x, ffn1_norm, ffn1_w_gate, ffn1_w_up, ffn1_w_down, mix_norm, w_in, s5_a_re, s5_a_im, s5_log_dt, s5_b_re, s5_b_im, s5_c_re, s5_c_im, s5_d, s5_w_glu, s5_b_glu, sgu_ln_g, sgu_ln_b, sgu_w_s, sgu_b_s, w_branch_a, w_branch_b, w_gate, b_gate, w_out, ffn2_norm, ffn2_w_gate, ffn2_w_up, ffn2_w_down, final_norm, loss_target, m_ffn1_norm, m_ffn1_w_gate, m_ffn1_w_up, m_ffn1_w_down, m_mix_norm, m_w_in, m_s5_a_re, m_s5_a_im, m_s5_log_dt, m_s5_b_re, m_s5_b_im, m_s5_c_re, m_s5_c_im, m_s5_d, m_s5_w_glu, m_s5_b_glu, m_sgu_ln_g, m_sgu_ln_b, m_sgu_w_s, m_sgu_b_s, m_w_branch_a, m_w_branch_b, m_w_gate, m_b_gate, m_w_out, m_ffn2_norm, m_ffn2_w_gate, m_ffn2_w_up, m_ffn2_w_down, m_final_norm, v_ffn1_norm, v_ffn1_w_gate, v_ffn1_w_up, v_ffn1_w_down, v_mix_norm, v_w_in, v_s5_a_re, v_s5_a_im, v_s5_log_dt, v_s5_b_re, v_s5_b_im, v_s5_c_re, v_s5_c_im, v_s5_d, v_s5_w_glu, v_s5_b_glu, v_sgu_ln_g, v_sgu_ln_b, v_sgu_w_s, v_sgu_b_s, v_w_branch_a, v_w_branch_b, v_w_gate, v_b_gate, v_w_out, v_ffn2_norm, v_ffn2_w_gate, v_ffn2_w_up, v_ffn2_w_down, v_final_norm):
    a = locals()
    W = {n: a[n] for n in _ORDER}
    M = {n: a["m_" + n] for n in _ORDER}
    V = {n: a["v_" + n] for n in _ORDER}
    loss, gx, G, Dl, Mn, Vn = _step(x, loss_target, W, M, V)
    return (loss, gx, *[G[n] for n in _ORDER], *[Dl[n] for n in _ORDER], *[Mn[n] for n in _ORDER],
            *[Vn[n] for n in _ORDER])
```

```python
import functools
import math

import jax
import jax.numpy as jnp
from jax import lax
from jax.experimental import pallas as pl
from jax.experimental.pallas import tpu as pltpu

F32 = jnp.float32
BF16 = jnp.bfloat16
NDEV = 8
NORM_EPS = 1e-6
D_MODEL = 2048
D_FF = 5632
FF_SHARD = D_FF // NDEV
S5_WIDTH = 1024
S5_GROUPS = 64
S5_GROUP_WIDTH = 16
S5_STATE = 64
S5_NS = S5_GROUPS * S5_STATE
SGU_WIDTH = 1024
SGU_HEADS = 8
MLP_CHUNK = 128
CHUNK = 64
ADAM_LR, ADAM_B1, ADAM_B2, ADAM_EPS, ADAM_WD, ADAM_STEP = 0.001, 0.9, 0.999, 1e-08, 0.01, 10
S5_TC = 256
S5_SEG = S5_TC // 8
S5_LG = 512
VMEM_BIG = 56 * 1024 * 1024

MESH = pl.DeviceIdType.MESH
SDS = jax.ShapeDtypeStruct
Block = pl.BlockSpec
ANY = pl.BlockSpec(memory_space=pl.ANY)


def _cparams(sem=None, vmem=None):
    return pltpu.CompilerParams(dimension_semantics=sem, vmem_limit_bytes=vmem)


def _const(shape):
    nd = len(shape)
    return pl.BlockSpec(shape, lambda i: (0,) * nd, pipeline_mode=pl.Buffered(1))


def _sigmoid(x):
    return 1.0 / (1.0 + jnp.exp(-x))


_GELU_C = math.sqrt(2.0 / math.pi)


def _gelu(x):
    return 0.5 * x * (1.0 + jnp.tanh(_GELU_C * (x + 0.044715 * x * x * x)))


def _gelu_grad(x):
    t = jnp.tanh(_GELU_C * (x + 0.044715 * x * x * x))
    return 0.5 * (1.0 + t) + 0.5 * x * (1.0 - t * t) * _GELU_C * (1.0 + 3.0 * 0.044715 * x * x)


NN = (((1,), (0,)), ((), ()))
NT = (((1,), (1,)), ((), ()))
TN = (((0,), (0,)), ((), ()))


def _dot(a, b, dims=NN):
    return lax.dot_general(a, b, dims, preferred_element_type=F32)


def _matmul(name, a, b, extras, *, grid, a_spec, b_spec, extra_specs, out_shapes, out_specs, acc_shape,
            epilogue, dims=NN, nb=None, compute=None, vmem=VMEM_BIG):
    nk = grid[2]
    ne, no = len(extras), len(out_shapes)
    nacc = nb or 1
    if compute is None:
        def compute(a_ref, b_ref, q):
            return _dot(a_ref[...], b_ref[q] if nb else b_ref[...], dims)

    def body(*refs):
        a_ref, b_ref = refs[0], refs[1]
        ex = refs[2:2 + ne]
        outs = refs[2 + ne:2 + ne + no]
        if nk == 1:
            epilogue([compute(a_ref, b_ref, q) for q in range(nacc)], ex, outs)
            return
        acc_ref = refs[2 + ne + no]
        k = pl.program_id(2)

        @pl.when(k == 0)
        def _():
            acc_ref[...] = jnp.zeros_like(acc_ref)

        for q in range(nacc):
            acc_ref[q] += compute(a_ref, b_ref, q)

        @pl.when(k == nk - 1)
        def _():
            epilogue([acc_ref[q] for q in range(nacc)], ex, outs)

    scratch = [] if nk == 1 else [pltpu.VMEM((nacc,) + tuple(acc_shape), F32)]
    res = pl.pallas_call(
        body, name=name, grid=grid,
        in_specs=[a_spec, b_spec] + list(extra_specs),
        out_specs=list(out_specs), out_shape=list(out_shapes), scratch_shapes=scratch,
        compiler_params=_cparams(("parallel", "parallel", "arbitrary"), vmem),
    )(a, b, *extras)
    return res


def _store(dtype_outs=None):
    def ep(accs, ex, outs):
        outs[0][...] = accs[0].astype(outs[0].dtype)
    return ep


def _tile(n, t):
    t = min(n, t)
    assert n % t == 0, (n, t)
    return t


def _ksum(kq, dims):
    def compute(a_ref, b_ref, _):
        part = _dot(a_ref[0], b_ref[0], dims)
        for q in range(1, kq):
            part = part + _dot(a_ref[q], b_ref[q], dims)
        return part
    return compute


def _ksum_lanes(kq, ns, dims):
    def compute(a_ref, b_ref, _):
        part = _dot(a_ref[:, 0:ns], b_ref[0], dims)
        for q in range(1, kq):
            part = part + _dot(a_ref[:, q * ns:(q + 1) * ns], b_ref[q], dims)
        return part
    return compute


def _wide_b(g):
    def compute(a_ref, b_ref, _):
        bw = b_ref[0] if g == 1 else jnp.concatenate([b_ref[q] for q in range(g)], axis=1)
        return _dot(a_ref[...], bw, NN)
    return compute


KQ_DOWN = 4
TT_DEEP = 2048


def _ffn_up(name, h, wgu):
    T, D = h.shape
    tm = _tile(T, 1024)

    def ep(accs, ex, outs):
        a, b = accs
        outs[0][0] = a.astype(BF16)
        outs[0][1] = b.astype(BF16)
        outs[1][...] = (a * _sigmoid(a) * b).astype(BF16)

    return _matmul(
        name, h, wgu, (), grid=(NDEV, T // tm, 1),
        a_spec=Block((tm, D), lambda j, i, k: (i, 0)),
        b_spec=Block((None, 2, D, FF_SHARD), lambda j, i, k: (j, 0, 0, 0)),
        extra_specs=(),
        out_shapes=[SDS((NDEV, 2, T, FF_SHARD), BF16), SDS((NDEV, T, FF_SHARD), BF16)],
        out_specs=[Block((None, 2, tm, FF_SHARD), lambda j, i, k: (j, 0, i, 0)),
                   Block((None, tm, FF_SHARD), lambda j, i, k: (j, i, 0))],
        acc_shape=(tm, FF_SHARD), dims=NN, nb=2, epilogue=ep)


def _ffn_down(name, f, wd, xres):
    _, T, _ = f.shape
    tm, tn = _tile(T, 1024), 1024

    def ep(accs, ex, outs):
        outs[0][...] = ex[0][...] + 0.5 * accs[0]

    return _matmul(
        name, f, wd, (xres,), grid=(T // tm, D_MODEL // tn, NDEV // KQ_DOWN),
        a_spec=Block((KQ_DOWN, tm, FF_SHARD), lambda i, j, k: (k, i, 0)),
        b_spec=Block((KQ_DOWN, FF_SHARD, tn), lambda i, j, k: (k, 0, j)),
        extra_specs=[Block((tm, tn), lambda i, j, k: (i, j))],
        out_shapes=[SDS((T, D_MODEL), F32)],
        out_specs=[Block((tm, tn), lambda i, j, k: (i, j))],
        acc_shape=(tm, tn), compute=_ksum(KQ_DOWN, NN), epilogue=ep)[0]


def _ffn_down_bwd_act(name, dyb, wd, ab):
    T, D = dyb.shape
    tm = _tile(T, 1024)

    def ep(accs, ex, outs):
        df = 0.5 * accs[0]
        a = ex[0][0].astype(F32)
        b = ex[0][1].astype(F32)
        s = _sigmoid(a)
        outs[0][0] = (df * b * s * (1.0 + a * (1.0 - s))).astype(BF16)
        outs[0][1] = (df * a * s).astype(BF16)

    return _matmul(
        name, dyb, wd, (ab,), grid=(NDEV, T // tm, 1),
        a_spec=Block((tm, D), lambda j, i, k: (i, 0)),
        b_spec=Block((None, FF_SHARD, D), lambda j, i, k: (j, 0, 0)),
        extra_specs=[Block((None, 2, tm, FF_SHARD), lambda j, i, k: (j, 0, i, 0))],
        out_shapes=[SDS((NDEV, 2, T, FF_SHARD), BF16)],
        out_specs=[Block((None, 2, tm, FF_SHARD), lambda j, i, k: (j, 0, i, 0))],
        acc_shape=(tm, FF_SHARD), dims=NT, nb=None, epilogue=ep)[0]


def _ffn_down_bwd_w(name, f, dyb):
    _, T, _ = f.shape
    tt, tn = _tile(T, TT_DEEP), 1024

    def ep(accs, ex, outs):
        outs[0][...] = (0.5 * accs[0]).astype(BF16)

    return _matmul(
        name, f, dyb, (), grid=(NDEV, D_MODEL // tn, T // tt),
        a_spec=Block((None, tt, FF_SHARD), lambda j, n, k: (j, k, 0)),
        b_spec=Block((tt, tn), lambda j, n, k: (k, n)),
        extra_specs=(),
        out_shapes=[SDS((NDEV, FF_SHARD, D_MODEL), BF16)],
        out_specs=[Block((None, FF_SHARD, tn), lambda j, n, k: (j, 0, n))],
        acc_shape=(FF_SHARD, tn), dims=TN, epilogue=ep)[0]


def _ffn_up_bwd_h(name, dab, wgu, after):
    _, _, T, _ = dab.shape
    tm = _tile(T, 1024)
    return _matmul(
        name, dab, wgu, (after,), grid=(T // tm, 1, NDEV),
        a_spec=Block((None, 2, tm, FF_SHARD), lambda i, j, k: (k, 0, i, 0)),
        b_spec=Block((None, 2, D_MODEL, FF_SHARD), lambda i, j, k: (k, 0, 0, 0)),
        extra_specs=[Block((8, 128), lambda i, j, k: (0, 0))],
        out_shapes=[SDS((T, D_MODEL), BF16)],
        out_specs=[Block((tm, D_MODEL), lambda i, j, k: (i, 0))],
        acc_shape=(tm, D_MODEL), compute=_ksum(2, NT), epilogue=_store())[0]


def _ffn_up_bwd_w(name, h, dab):
    T, D = h.shape
    tt, tr = _tile(T, TT_DEEP), 1024

    def ep(accs, ex, outs):
        outs[0][0] = accs[0].astype(BF16)
        outs[0][1] = accs[1].astype(BF16)

    return _matmul(
        name, h, dab, (), grid=(NDEV, D // tr, T // tt),
        a_spec=Block((tt, tr), lambda j, n, k: (k, n)),
        b_spec=Block((None, 2, tt, FF_SHARD), lambda j, n, k: (j, 0, k, 0)),
        extra_specs=(),
        out_shapes=[SDS((NDEV, 2, D, FF_SHARD), BF16)],
        out_specs=[Block((None, 2, tr, FF_SHARD), lambda j, n, k: (j, 0, n, 0))],
        acc_shape=(tr, FF_SHARD), dims=TN, nb=2, epilogue=ep)[0]


def _shards_per_step(ns):
    return max(g for g in (1, 2, 4, 8) if g * ns <= 2048)


def _split_lanes(g, ns):
    def ep(accs, ex, outs):
        for q in range(g):
            outs[0][q] = accs[0][:, q * ns:(q + 1) * ns].astype(outs[0].dtype)
    return ep


def _col_fwd(name, a, w, out_dtype=BF16):
    T, K = a.shape
    ns = w.shape[2]
    g = _shards_per_step(ns)
    tm = _tile(T, 1024)
    return _matmul(
        name, a, w, (), grid=(NDEV // g, T // tm, 1),
        a_spec=Block((tm, K), lambda j, i, k: (i, 0)),
        b_spec=Block((g, K, ns), lambda j, i, k: (j, 0, 0)),
        extra_specs=(),
        out_shapes=[SDS((T, NDEV * ns), out_dtype)],
        out_specs=[Block((tm, g * ns), lambda j, i, k: (i, j))],
        acc_shape=(tm, g * ns), compute=_wide_b(g), epilogue=_store())[0]


def _col_bwd_a(name, dy, w, add=None):
    T = dy.shape[0]
    _, K, ns = w.shape
    tm, tn = _tile(T, 1024), _tile(K, 1024)

    def ep(accs, ex, outs):
        r = accs[0]
        if add is not None:
            r = r + ex[0][...].astype(F32)
        outs[0][...] = r.astype(BF16)

    extras = () if add is None else (add,)
    return _matmul(
        name, dy, w, extras, grid=(T // tm, K // tn, 1),
        a_spec=Block((tm, NDEV * ns), lambda i, j, k: (i, 0)),
        b_spec=Block((NDEV, tn, ns), lambda i, j, k: (0, j, 0)),
        extra_specs=[Block((tm, tn), lambda i, j, k: (i, j))] * len(extras),
        out_shapes=[SDS((T, K), BF16)],
        out_specs=[Block((tm, tn), lambda i, j, k: (i, j))],
        acc_shape=(tm, tn), compute=_ksum_lanes(NDEV, ns, NT), epilogue=ep)[0]


def _col_bwd_w(name, a, dy, ns):
    T, K = a.shape
    g = _shards_per_step(ns)
    tt, tr = _tile(T, TT_DEEP), _tile(K, 1024)
    return _matmul(
        name, a, dy, (), grid=(NDEV // g, K // tr, T // tt),
        a_spec=Block((tt, tr), lambda j, n, k: (k, n)),
        b_spec=Block((tt, g * ns), lambda j, n, k: (k, j)),
        extra_specs=(),
        out_shapes=[SDS((NDEV, K, ns), BF16)],
        out_specs=[Block((g, tr, ns), lambda j, n, k: (j, n, 0))],
        acc_shape=(tr, g * ns), dims=TN, epilogue=_split_lanes(g, ns))[0]


def _gate_fwd(name, h, w, bias):
    T, K = h.shape
    ns = w.shape[2]
    g = 2
    per = D_MODEL // (g * ns)
    tm = _tile(T, 1024)

    def ep(accs, ex, outs):
        outs[0][...] = (accs[0] + ex[0][...]).astype(BF16)

    return _matmul(
        name, h, w, (bias,), grid=(NDEV // g, T // tm, 1),
        a_spec=Block((tm, K), lambda j, i, k: (i, 0)),
        b_spec=Block((g, K, ns), lambda j, i, k: (j, 0, 0)),
        extra_specs=[Block((None, 1, g * ns), lambda j, i, k: (j // per, 0, j % per))],
        out_shapes=[SDS((2, T, D_MODEL), BF16)],
        out_specs=[Block((None, tm, g * ns), lambda j, i, k: (j // per, i, j % per))],
        acc_shape=(tm, g * ns), compute=_wide_b(g), epilogue=ep)[0]


def _gate_bwd_a(name, dgl, w):
    _, T, _ = dgl.shape
    _, K, ns = w.shape
    per = D_MODEL // ns
    tm, tn = _tile(T, 1024), 1024

    def compute(a_ref, b_ref, _):
        part = None
        for q in range(NDEV):
            d = _dot(a_ref[q // per, :, (q % per) * ns:(q % per + 1) * ns], b_ref[q], NT)
            part = d if part is None else part + d
        return part

    return _matmul(
        name, dgl, w, (), grid=(T // tm, K // tn, 1),
        a_spec=Block((2, tm, D_MODEL), lambda i, j, k: (0, i, 0)),
        b_spec=Block((NDEV, tn, ns), lambda i, j, k: (0, j, 0)),
        extra_specs=(),
        out_shapes=[SDS((T, K), BF16)],
        out_specs=[Block((tm, tn), lambda i, j, k: (i, j))],
        acc_shape=(tm, tn), compute=compute, epilogue=_store())[0]


def _gate_bwd_w(name, h, dgl, ns):
    T, K = h.shape
    g = 2
    per = D_MODEL // (g * ns)
    tt, tr = _tile(T, TT_DEEP), 1024
    return _matmul(
        name, h, dgl, (), grid=(NDEV // g, K // tr, T // tt),
        a_spec=Block((tt, tr), lambda j, n, k: (k, n)),
        b_spec=Block((None, tt, g * ns), lambda j, n, k: (j // per, k, j % per)),
        extra_specs=(),
        out_shapes=[SDS((NDEV, K, ns), BF16)],
        out_specs=[Block((g, tr, ns), lambda j, n, k: (j, n, 0))],
        acc_shape=(tr, g * ns), dims=TN, epilogue=_split_lanes(g, ns))[0]


def _plain_fwd_res(name, a, w, xres):
    T, K = a.shape
    N = w.shape[1]
    tm, tn = _tile(T, 1024), _tile(N, 1024)

    def ep(accs, ex, outs):
        outs[0][...] = ex[0][...] + accs[0]

    return _matmul(
        name, a, w, (xres,), grid=(T // tm, N // tn, 1),
        a_spec=Block((tm, K), lambda i, j, k: (i, 0)),
        b_spec=Block((K, tn), lambda i, j, k: (0, j)),
        extra_specs=[Block((tm, tn), lambda i, j, k: (i, j))],
        out_shapes=[SDS((T, N), F32)],
        out_specs=[Block((tm, tn), lambda i, j, k: (i, j))],
        acc_shape=(tm, tn), dims=NN, nb=None, epilogue=ep)[0]


def _plain_bwd_a(name, dy, w):
    T, N = dy.shape
    K = w.shape[0]
    tm, tn = _tile(T, 1024), _tile(K, 1024)
    return _matmul(
        name, dy, w, (), grid=(T // tm, K // tn, 1),
        a_spec=Block((tm, N), lambda i, j, k: (i, 0)),
        b_spec=Block((tn, N), lambda i, j, k: (j, 0)),
        extra_specs=(),
        out_shapes=[SDS((T, K), BF16)],
        out_specs=[Block((tm, tn), lambda i, j, k: (i, j))],
        acc_shape=(tm, tn), dims=NT, nb=None, epilogue=_store())[0]


def _plain_bwd_w(name, a, dy):
    T, K = a.shape
    N = dy.shape[1]
    tt, tr, tn = _tile(T, TT_DEEP), _tile(K, 1024), _tile(N, 1024)
    return _matmul(
        name, a, dy, (), grid=(K // tr, N // tn, T // tt),
        a_spec=Block((tt, tr), lambda m, n, k: (k, m)),
        b_spec=Block((tt, tn), lambda m, n, k: (k, n)),
        extra_specs=(),
        out_shapes=[SDS((K, N), BF16)],
        out_specs=[Block((tr, tn), lambda m, n, k: (m, n))],
        acc_shape=(tr, tn), dims=TN, nb=None, epilogue=_store())[0]


def _rms_fwd(name, x, g):
    T, D = x.shape
    tm = _tile(T, 512)

    def body(x_ref, g_ref, h_ref):
        xv = x_ref[...]
        r = lax.rsqrt(jnp.mean(xv * xv, axis=-1, keepdims=True) + NORM_EPS)
        h_ref[...] = (xv * r * g_ref[...]).astype(BF16)

    return pl.pallas_call(
        body, name=name, grid=(T // tm,),
        in_specs=[Block((tm, D), lambda i: (i, 0)), Block((1, D), lambda i: (0, 0))],
        out_specs=Block((tm, D), lambda i: (i, 0)), out_shape=SDS((T, D), BF16),
        compiler_params=_cparams(("arbitrary",), VMEM_BIG))(x, g)


def _rms_bwd(name, dh, x, g, dxin):
    T, D = x.shape
    tm = _tile(T, 512)

    def body(dh_ref, x_ref, g_ref, dxin_ref, dx_ref, dxb_ref, dg_ref):
        i = pl.program_id(0)
        xv = x_ref[...]
        dh = dh_ref[...].astype(F32)
        r = lax.rsqrt(jnp.mean(xv * xv, axis=-1, keepdims=True) + NORM_EPS)
        xh = xv * r
        gd = dh * g_ref[...]
        dx = dxin_ref[...] + r * (gd - xh * jnp.mean(gd * xh, axis=-1, keepdims=True))
        dx_ref[...] = dx
        dxb_ref[...] = dx.astype(BF16)
        dgp = jnp.sum(dh * xh, axis=0, keepdims=True)

        @pl.when(i == 0)
        def _():
            dg_ref[...] = dgp

        @pl.when(i > 0)
        def _():
            dg_ref[...] += dgp

    row = Block((tm, D), lambda i: (i, 0))
    vec = Block((1, D), lambda i: (0, 0))
    return pl.pallas_call(
        body, name=name, grid=(T // tm,),
        in_specs=[row, row, vec, row], out_specs=[row, row, vec],
        out_shape=[SDS((T, D), F32), SDS((T, D), BF16), SDS((1, D), F32)],
        compiler_params=_cparams(("arbitrary",), VMEM_BIG))(dh, x, g, dxin)


def _loss_head(name, x, g, tgt):
    T, D = x.shape
    tm = _tile(T, 512)

    def body(x_ref, g_ref, t_ref, loss_ref, dx_ref, dxb_ref, dg_ref):
        i = pl.program_id(0)
        xv = x_ref[...]
        gv = g_ref[...]
        r = lax.rsqrt(jnp.mean(xv * xv, axis=-1, keepdims=True) + NORM_EPS)
        xh = xv * r
        err = xh * gv - t_ref[...]
        lp = 0.5 * jnp.sum(jnp.mean(err * err, axis=-1, keepdims=True), axis=0, keepdims=True)
        dout = err * (1.0 / D)
        gd = dout * gv
        dx = r * (gd - xh * jnp.mean(gd * xh, axis=-1, keepdims=True))
        dx_ref[...] = dx
        dxb_ref[...] = dx.astype(BF16)
        dgp = jnp.sum(dout * xh, axis=0, keepdims=True)
        lpb = jnp.broadcast_to(lp, (1, 128))

        @pl.when(i == 0)
        def _():
            dg_ref[...] = dgp
            loss_ref[...] = lpb

        @pl.when(i > 0)
        def _():
            dg_ref[...] += dgp
            loss_ref[...] += lpb

    row = Block((tm, D), lambda i: (i, 0))
    vec = Block((1, D), lambda i: (0, 0))
    return pl.pallas_call(
        body, name=name, grid=(T // tm,),
        in_specs=[row, vec, row], out_specs=[Block((1, 128), lambda i: (0, 0)), row, row, vec],
        out_shape=[SDS((1, 128), F32), SDS((T, D), F32), SDS((T, D), BF16), SDS((1, D), F32)],
        compiler_params=_cparams(("arbitrary",), VMEM_BIG))(x, g, tgt)


def _merge_fwd(name, pa, pb, gl):
    T, D = pa.shape
    tm = _tile(T, 512)

    def body(pa_ref, pb_ref, gl_ref, o_ref):
        ga = _sigmoid(gl_ref[0].astype(F32))
        gb = _sigmoid(gl_ref[1].astype(F32))
        o_ref[...] = (ga * pa_ref[...].astype(F32) + gb * pb_ref[...].astype(F32)).astype(BF16)

    row = Block((tm, D), lambda i: (i, 0))
    return pl.pallas_call(
        body, name=name, grid=(T // tm,),
        in_specs=[row, row, Block((2, tm, D), lambda i: (0, i, 0))], out_specs=row,
        out_shape=SDS((T, D), BF16), compiler_params=_cparams(("arbitrary",), VMEM_BIG))(pa, pb, gl)


def _merge_bwd(name, dm, pa, pb, gl):
    T, D = pa.shape
    tm = _tile(T, 512)

    def body(dm_ref, pa_ref, pb_ref, gl_ref, dpa_ref, dpb_ref, dgl_ref, db_ref):
        i = pl.program_id(0)
        dmv = dm_ref[...].astype(F32)
        ga = _sigmoid(gl_ref[0].astype(F32))
        gb = _sigmoid(gl_ref[1].astype(F32))
        dpa_ref[...] = (dmv * ga).astype(BF16)
        dpb_ref[...] = (dmv * gb).astype(BF16)
        dga = dmv * pa_ref[...].astype(F32) * ga * (1.0 - ga)
        dgb = dmv * pb_ref[...].astype(F32) * gb * (1.0 - gb)
        dgl_ref[0] = dga.astype(BF16)
        dgl_ref[1] = dgb.astype(BF16)
        sa = jnp.sum(dga, axis=0, keepdims=True)
        sb = jnp.sum(dgb, axis=0, keepdims=True)

        @pl.when(i == 0)
        def _():
            db_ref[0] = sa
            db_ref[1] = sb

        @pl.when(i > 0)
        def _():
            db_ref[0] += sa
            db_ref[1] += sb

    row = Block((tm, D), lambda i: (i, 0))
    two = Block((2, tm, D), lambda i: (0, i, 0))
    return pl.pallas_call(
        body, name=name, grid=(T // tm,),
        in_specs=[row, row, row, two], out_specs=[row, row, two, Block((2, 1, D), lambda i: (0, 0, 0))],
        out_shape=[SDS((T, D), BF16), SDS((T, D), BF16), SDS((2, T, D), BF16), SDS((2, 1, D), F32)],
        compiler_params=_cparams(("arbitrary",), VMEM_BIG))(dm, pa, pb, gl)


def _sgu_core(ur, vr, lng, lnb, ws_ref, bs_ref):
    tm = ur.shape[0]
    gu = _gelu(ur)
    gv = _gelu(vr)
    mu = jnp.mean(gv, axis=-1, keepdims=True)
    cen = gv - mu
    rstd = lax.rsqrt(jnp.mean(cen * cen, axis=-1, keepdims=True) + NORM_EPS)
    xhat = cen * rstd
    vn = (xhat * lng + lnb).astype(BF16)
    rows = []
    for n in range(tm // MLP_CHUNK):
        cols = []
        for h in range(SGU_HEADS):
            blk = vn[n * MLP_CHUNK:(n + 1) * MLP_CHUNK, h * 128:(h + 1) * 128]
            cols.append(_dot(ws_ref[h], blk) + bs_ref[h])
        rows.append(jnp.concatenate(cols, axis=1))
    mixed = jnp.concatenate(rows, axis=0) if len(rows) > 1 else rows[0]
    return gu, xhat, rstd, vn, mixed


def _sgu_fwd(name, proj, lng, lnb, wsm, bst):
    T = proj.shape[0]
    W = SGU_WIDTH
    tm = _tile(T, 512)

    def body(u_ref, v_ref, lng_ref, lnb_ref, ws_ref, bs_ref, o_ref):
        gu, _, _, _, mixed = _sgu_core(u_ref[...].astype(F32), v_ref[...].astype(F32), lng_ref[...], lnb_ref[...],
                                       ws_ref, bs_ref)
        o_ref[...] = (gu * mixed).astype(BF16)

    vec = Block((1, W), lambda i: (0, 0))
    return pl.pallas_call(
        body, name=name, grid=(T // tm,),
        in_specs=[Block((tm, W), lambda i: (i, 1)), Block((tm, W), lambda i: (i, 2)), vec, vec,
                  Block((SGU_HEADS, 128, 128), lambda i: (0, 0, 0)), Block((SGU_HEADS, 128, 128), lambda i: (0, 0, 0))],
        out_specs=Block((tm, W), lambda i: (i, 0)), out_shape=SDS((T, W), BF16),
        compiler_params=_cparams(("arbitrary",), VMEM_BIG))(proj, proj, lng, lnb, wsm, bst)


def _sgu_bwd(name, dyb, proj, lng, lnb, wsm, wsmt, bst):
    T = proj.shape[0]
    W = SGU_WIDTH
    tm = _tile(T, 512)

    def body(dy_ref, u_ref, v_ref, lng_ref, lnb_ref, ws_ref, wst_ref, bs_ref,
             duv_ref, dws_ref, dbs_ref, dlng_ref, dlnb_ref):
        i = pl.program_id(0)
        ur = u_ref[...].astype(F32)
        vr = v_ref[...].astype(F32)
        lng_v = lng_ref[...]
        gu, xhat, rstd, vn, mixed = _sgu_core(ur, vr, lng_v, lnb_ref[...], ws_ref, bs_ref)
        dy = dy_ref[...].astype(F32)
        dgu = dy * mixed
        dmix = dy * gu
        dmb = dmix.astype(BF16)
        dws_p, dbs_p, rows = [], [], []
        for h in range(SGU_HEADS):
            acc_w = jnp.zeros((128, 128), F32)
            acc_b = jnp.zeros((128, 1), F32)
            for n in range(tm // MLP_CHUNK):
                r0 = n * MLP_CHUNK
                dmt = dmb[r0:r0 + MLP_CHUNK, h * 128:(h + 1) * 128]
                acc_w = acc_w + _dot(dmt, vn[r0:r0 + MLP_CHUNK, h * 128:(h + 1) * 128], NT)
                acc_b = acc_b + jnp.sum(dmix[r0:r0 + MLP_CHUNK, h * 128:(h + 1) * 128], axis=1, keepdims=True)
            dws_p.append(acc_w)
            dbs_p.append(jnp.broadcast_to(acc_b, (128, 128)))
        for n in range(tm // MLP_CHUNK):
            r0 = n * MLP_CHUNK
            rows.append(jnp.concatenate(
                [_dot(wst_ref[h], dmb[r0:r0 + MLP_CHUNK, h * 128:(h + 1) * 128]) for h in range(SGU_HEADS)], axis=1))
        dvn = jnp.concatenate(rows, axis=0) if len(rows) > 1 else rows[0]
        dlng_p = jnp.sum(dvn * xhat, axis=0, keepdims=True)
        dlnb_p = jnp.sum(dvn, axis=0, keepdims=True)
        dxh = dvn * lng_v
        dgv = rstd * (dxh - jnp.mean(dxh, axis=-1, keepdims=True)
                      - xhat * jnp.mean(dxh * xhat, axis=-1, keepdims=True))
        duv_ref[:, :W] = (dgu * _gelu_grad(ur)).astype(BF16)
        duv_ref[:, W:] = (dgv * _gelu_grad(vr)).astype(BF16)

        @pl.when(i == 0)
        def _():
            for h in range(SGU_HEADS):
                dws_ref[h] = dws_p[h]
                dbs_ref[h] = dbs_p[h]
            dlng_ref[...] = dlng_p
            dlnb_ref[...] = dlnb_p

        @pl.when(i > 0)
        def _():
            for h in range(SGU_HEADS):
                dws_ref[h] += dws_p[h]
                dbs_ref[h] += dbs_p[h]
            dlng_ref[...] += dlng_p
            dlnb_ref[...] += dlnb_p

    vec = Block((1, W), lambda i: (0, 0))
    wsb = Block((SGU_HEADS, 128, 128), lambda i: (0, 0, 0))
    hsq = SDS((SGU_HEADS, 128, 128), F32)
    return pl.pallas_call(
        body, name=name, grid=(T // tm,),
        in_specs=[Block((tm, W), lambda i: (i, 0)), Block((tm, W), lambda i: (i, 1)), Block((tm, W), lambda i: (i, 2)),
                  vec, vec, wsb, wsb, wsb],
        out_specs=[Block((tm, 2 * W), lambda i: (i, 0)), wsb, wsb, vec, vec],
        out_shape=[SDS((T, 2 * W), BF16), hsq, hsq, SDS((1, W), F32), SDS((1, W), F32)],
        compiler_params=_cparams(("arbitrary",), VMEM_BIG))(dyb, proj, proj, lng, lnb, wsm, wsmt, bst)


def _s5_disc(lr, li, ldt, brt, bit):
    dt = jnp.exp(ldt)
    decay = jnp.exp(lr * dt)
    abr = decay * jnp.cos(li * dt)
    abi = decay * jnp.sin(li * dt)
    dec_s = jnp.exp(lr * dt * S5_SEG)
    asr = dec_s * jnp.cos(li * dt * S5_SEG)
    asi = dec_s * jnp.sin(li * dt * S5_SEG)
    denom = lr * lr + li * li
    nr = abr - 1.0
    ni = abi
    kr = (nr * lr + ni * li) / denom
    ki = (ni * lr - nr * li) / denom
    bkr = kr[None] * brt - ki[None] * bit
    bki = kr[None] * bit + ki[None] * brt
    return abr, abi, asr, asi, bkr, bki


def _s5_prep(lr, li, ldt, brt, bit):
    G, P, C = S5_GROUPS, S5_STATE, S5_GROUP_WIDTH

    def body(lr_ref, li_ref, ldt_ref, br_ref, bi_ref, abr_ref, abi_ref, asr_ref, asi_ref, bkr_ref, bki_ref):
        res = _s5_disc(lr_ref[...], li_ref[...], ldt_ref[...], br_ref[...], bi_ref[...])
        for o, r in zip((abr_ref, abi_ref, asr_ref, asi_ref, bkr_ref, bki_ref), res):
            o[...] = r

    gp = SDS((G, P), F32)
    cgp = SDS((C, G, P), F32)
    return pl.pallas_call(body, name="s5_prep", out_shape=[gp, gp, gp, gp, cgp, cgp])(lr, li, ldt, brt, bit)


def _s5_prep_bwd(lr, li, ldt, brt, bit, dabr, dabi, dbkr, dbki):
    G, P, C = S5_GROUPS, S5_STATE, S5_GROUP_WIDTH

    def body(lr_ref, li_ref, ldt_ref, br_ref, bi_ref, dabr_ref, dabi_ref, dbkr_ref, dbki_ref,
             o_lr, o_li, o_ldt, o_br, o_bi):
        def f(lr_, li_, ldt_, br_, bi_):
            abr, abi, _, _, bkr, bki = _s5_disc(lr_, li_, ldt_, br_, bi_)
            return abr, abi, bkr, bki

        _, pull = jax.vjp(f, lr_ref[...], li_ref[...], ldt_ref[...], br_ref[...], bi_ref[...])
        g = pull((dabr_ref[...], dabi_ref[...], dbkr_ref[...], dbki_ref[...]))
        for o, r in zip((o_lr, o_li, o_ldt, o_br, o_bi), g):
            o[...] = r

    gp = SDS((G, P), F32)
    cgp = SDS((C, G, P), F32)
    return pl.pallas_call(body, name="s5_prep_bwd", out_shape=[gp, gp, SDS((G, 1), F32), cgp, cgp])(
        lr, li, ldt, brt, bit, dabr, dabi, dbkr, dbki)


def _s5_scan(buf_ref, ar_row, ai_row, asr_row, asi_row, carry_ref, LG, xs_ref=None, dar_ref=None, dai_ref=None):
    reverse = xs_ref is not None
    NS, SEG = S5_NS, S5_SEG
    sgn = -1.0 if reverse else 1.0
    for lg in range(NS // LG):
        cr = slice(lg * LG, (lg + 1) * LG)
        ci = slice(NS + lg * LG, NS + (lg + 1) * LG)
        ar1, ai1 = ar_row[:, cr], sgn * ai_row[:, cr]
        asr1, asi1 = asr_row[:, cr], sgn * asi_row[:, cr]
        ar = jnp.broadcast_to(ar1, (8, LG))
        ai = jnp.broadcast_to(ai1, (8, LG))

        def step_of(j):
            return (SEG - 1 - j) if reverse else j

        def p1(j, st):
            sr, si = st
            rows = pl.ds(pl.multiple_of(step_of(j) * 8, 8), 8)
            nr = ar * sr - ai * si + buf_ref[rows, cr]
            ni = ar * si + ai * sr + buf_ref[rows, ci]
            buf_ref[rows, cr] = nr
            buf_ref[rows, ci] = ni
            return nr, ni

        z = jnp.zeros((8, LG), F32)
        er, ei = lax.fori_loop(0, SEG, p1, (z, z), unroll=2)
        c_r = carry_ref[:, cr]
        c_i = carry_ref[:, ci]
        cs_r, cs_i = [None] * 8, [None] * 8
        order = range(7, -1, -1) if reverse else range(8)
        for s in order:
            cs_r[s], cs_i[s] = c_r, c_i
            e_r, e_i = er[s:s + 1], ei[s:s + 1]
            c_r, c_i = e_r + asr1 * c_r - asi1 * c_i, e_i + asr1 * c_i + asi1 * c_r
        carry_ref[:, cr] = c_r
        carry_ref[:, ci] = c_i
        cmr = jnp.concatenate(cs_r, axis=0)
        cmi = jnp.concatenate(cs_i, axis=0)

        if not reverse:
            def p2(j, st):
                wr, wi = st
                rows = pl.ds(pl.multiple_of(j * 8, 8), 8)
                nwr = ar * wr - ai * wi
                nwi = ar * wi + ai * wr
                buf_ref[rows, cr] += nwr
                buf_ref[rows, ci] += nwi
                return nwr, nwi

            lax.fori_loop(0, SEG, p2, (cmr, cmi), unroll=2)
        else:
            def p2(j, st):
                wr, wi, pr, pi, dr, di = st
                rows = pl.ds(pl.multiple_of(step_of(j) * 8, 8), 8)
                xr = xs_ref[rows, cr]
                xi = xs_ref[rows, ci]
                dr = dr + pr * xr + pi * xi
                di = di + pi * xr - pr * xi
                nwr = ar * wr - ai * wi
                nwi = ar * wi + ai * wr
                gr = buf_ref[rows, cr] + nwr
                gi = buf_ref[rows, ci] + nwi
                buf_ref[rows, cr] = gr
                buf_ref[rows, ci] = gi
                return nwr, nwi, gr, gi, dr, di

            st = lax.fori_loop(0, SEG, p2, (cmr, cmi, cmr, cmi, z, z), unroll=2)
            dar_ref[:, cr] += st[4]
            dai_ref[:, cr] += st[5]


def _s5_fwd(proj, perm, permt, bdbr, bdbi, bdcr, bdci, abr, abi, asr, asi, dvec, wglu, bglu):
    T = proj.shape[0]
    TC, NS, W = S5_TC, S5_NS, S5_WIDTH
    nc = T // TC

    def body(u_ref, pm_ref, pmt_ref, bdbr_ref, bdbi_ref, bdcr_ref, bdci_ref, ar_ref, ai_ref, asr_ref, asi_ref,
             d_ref, wglu_ref, bglu_ref, ya_ref, xs_ref, ypre_ref, carry_ref):
        i = pl.program_id(0)

        @pl.when(i == 0)
        def _():
            carry_ref[...] = jnp.zeros_like(carry_ref)

        up = _dot(pm_ref[...], u_ref[...]).astype(BF16)
        for j in range(8):
            ut = up[:, j * 128:(j + 1) * 128]
            xs_ref[:, j * 512:(j + 1) * 512] = _dot(ut, bdbr_ref[j])
            xs_ref[:, NS + j * 512:NS + (j + 1) * 512] = _dot(ut, bdbi_ref[j])
        _s5_scan(xs_ref, ar_ref[...], ai_ref[...], asr_ref[...], asi_ref[...], carry_ref, 512)
        ys = []
        for j in range(8):
            xr = xs_ref[:, j * 512:(j + 1) * 512].astype(BF16)
            xi = xs_ref[:, NS + j * 512:NS + (j + 1) * 512].astype(BF16)
            ys.append(_dot(xr, bdcr_ref[j]) + _dot(xi, bdci_ref[j]))
        ypre = jnp.concatenate(ys, axis=1) + d_ref[...] * up.astype(F32)
        ypre_ref[...] = ypre
        ya = _gelu(ypre)
        zl = _dot(ya.astype(BF16), wglu_ref[...]) + bglu_ref[...]
        outp = (ya * _sigmoid(zl)).astype(BF16)
        ya_ref[...] = _dot(pmt_ref[...], outp).astype(BF16)

    return pl.pallas_call(
        body, name="s5_fwd", grid=(nc,),
        in_specs=[Block((TC, W), lambda i: (i, 0)), _const((TC, TC)), _const((TC, TC)),
                  _const((8, 128, 512)), _const((8, 128, 512)), _const((8, 512, 128)), _const((8, 512, 128)),
                  _const((1, NS)), _const((1, NS)), _const((1, NS)), _const((1, NS)),
                  _const((1, W)), _const((W, W)), _const((1, W))],
        out_specs=[Block((TC, W), lambda i: (i, 0)), Block((TC, 2 * NS), lambda i: (i, 0)),
                   Block((TC, W), lambda i: (i, 0))],
        out_shape=[SDS((T, W), BF16), SDS((T, 2 * NS), F32), SDS((T, W), F32)],
        scratch_shapes=[pltpu.VMEM((1, 2 * NS), F32)],
        compiler_params=_cparams(("arbitrary",), VMEM_BIG),
    )(proj, perm, permt, bdbr, bdbi, bdcr, bdci, abr, abi, asr, asi, dvec, wglu, bglu)


def _s5_bwd(dya, proj, ypre, xs, perm, permt, bdbr, bdbi, bdcr, bdci, abr, abi, asr, asi, dvec, wglu, bglu):
    T = proj.shape[0]
    TC, NS, W = S5_TC, S5_NS, S5_WIDTH
    nc = T // TC

    def body(dya_ref, u_ref, ypre_ref, xs_ref, pm_ref, pmt_ref, bdbr_ref, bdbi_ref, bdcr_ref, bdci_ref,
             ar_ref, ai_ref, asr_ref, asi_ref, d_ref, wglu_ref, bglu_ref,
             du_ref, dar_ref, dai_ref, dd_ref, dbglu_ref, o_dbdbr, o_dbdbi, o_dbdcr, o_dbdci, o_dwglu,
             g_ref, carry_ref, dbdbr_ref, dbdbi_ref, dbdcr_ref, dbdci_ref, dwglu_ref):
        i = pl.program_id(0)

        @pl.when(i == 0)
        def _():
            carry_ref[...] = jnp.zeros_like(carry_ref)
            for r in (dbdbr_ref, dbdbi_ref, dbdcr_ref, dbdci_ref, dar_ref, dai_ref, dd_ref, dwglu_ref, dbglu_ref):
                r[...] = jnp.zeros_like(r)

        pm = pm_ref[...]
        dyo = _dot(pm, dya_ref[...])
        up = _dot(pm, u_ref[...]).astype(BF16)
        upf = up.astype(F32)
        ypre_v = ypre_ref[...]
        ya = _gelu(ypre_v)
        yab = ya.astype(BF16)
        sg = _sigmoid(_dot(yab, wglu_ref[...]) + bglu_ref[...])
        dz = dyo * ya * sg * (1.0 - sg)
        dzb = dz.astype(BF16)
        dya_t = dyo * sg + _dot(dzb, wglu_ref[...], NT)
        dwglu_ref[...] += _dot(yab, dzb, TN)
        dbglu_ref[...] += jnp.sum(dz, axis=0, keepdims=True)
        dy = dya_t * _gelu_grad(ypre_v)
        dd_ref[...] += jnp.sum(dy * upf, axis=0, keepdims=True)
        dyb = dy.astype(BF16)
        for j in range(8):
            dyj = dyb[:, j * 128:(j + 1) * 128]
            g_ref[:, j * 512:(j + 1) * 512] = _dot(dyj, bdcr_ref[j], NT)
            g_ref[:, NS + j * 512:NS + (j + 1) * 512] = _dot(dyj, bdci_ref[j], NT)
            dbdcr_ref[j] += _dot(xs_ref[:, j * 512:(j + 1) * 512].astype(BF16), dyj, TN)
            dbdci_ref[j] += _dot(xs_ref[:, NS + j * 512:NS + (j + 1) * 512].astype(BF16), dyj, TN)
        _s5_scan(g_ref, ar_ref[...], ai_ref[...], asr_ref[...], asi_ref[...], carry_ref, 256,
                 xs_ref=xs_ref, dar_ref=dar_ref, dai_ref=dai_ref)
        dus = []
        for j in range(8):
            ut = up[:, j * 128:(j + 1) * 128]
            gr = g_ref[:, j * 512:(j + 1) * 512].astype(BF16)
            gi = g_ref[:, NS + j * 512:NS + (j + 1) * 512].astype(BF16)
            dbdbr_ref[j] += _dot(ut, gr, TN)
            dbdbi_ref[j] += _dot(ut, gi, TN)
            dus.append(_dot(gr, bdbr_ref[j], NT) + _dot(gi, bdbi_ref[j], NT))
        dup = jnp.concatenate(dus, axis=1) + d_ref[...] * dy
        du_ref[...] = _dot(pmt_ref[...], dup.astype(BF16)).astype(BF16)

        @pl.when(i == nc - 1)
        def _():
            for src, dst in ((dbdbr_ref, o_dbdbr), (dbdbi_ref, o_dbdbi), (dbdcr_ref, o_dbdcr),
                             (dbdci_ref, o_dbdci), (dwglu_ref, o_dwglu)):
                pltpu.sync_copy(src, dst)

    c2 = lambda i: (0, 0)
    rev = lambda i: (nc - 1 - i, 0)
    return pl.pallas_call(
        body, name="s5_bwd", grid=(nc,),
        in_specs=[Block((TC, W), rev), Block((TC, W), rev), Block((TC, W), rev), Block((TC, 2 * NS), rev),
                  _const((TC, TC)), _const((TC, TC)),
                  _const((8, 128, 512)), _const((8, 128, 512)), _const((8, 512, 128)), _const((8, 512, 128)),
                  _const((1, NS)), _const((1, NS)), _const((1, NS)), _const((1, NS)),
                  _const((1, W)), _const((W, W)), _const((1, W))],
        out_specs=[Block((TC, W), rev), Block((8, NS), c2), Block((8, NS), c2), Block((1, W), c2), Block((1, W), c2),
                   ANY, ANY, ANY, ANY, ANY],
        out_shape=[SDS((T, W), BF16), SDS((8, NS), F32), SDS((8, NS), F32), SDS((1, W), F32), SDS((1, W), F32),
                   SDS((8, 128, 512), F32), SDS((8, 128, 512), F32),
                   SDS((8, 512, 128), F32), SDS((8, 512, 128), F32), SDS((W, W), F32)],
        scratch_shapes=[pltpu.VMEM((TC, 2 * NS), F32), pltpu.VMEM((1, 2 * NS), F32),
                        pltpu.VMEM((8, 128, 512), F32), pltpu.VMEM((8, 128, 512), F32),
                        pltpu.VMEM((8, 512, 128), F32), pltpu.VMEM((8, 512, 128), F32), pltpu.VMEM((W, W), F32)],
        compiler_params=_cparams(("arbitrary",), VMEM_BIG),
    )(dya, proj, ypre, xs, perm, permt, bdbr, bdbi, bdcr, bdci, abr, abi, asr, asi, dvec, wglu, bglu)


def _bd_b(bk_t):
    C, P = S5_GROUP_WIDTH, S5_STATE
    t = jnp.transpose(bk_t, (1, 0, 2)).reshape(8, 8, C, P)
    eye = jnp.eye(8, dtype=t.dtype)
    return (t[:, :, :, None, :] * eye[None, :, None, :, None]).reshape(8, 8 * C, 8 * P)


def _bd_b_extract(m):
    C, P = S5_GROUP_WIDTH, S5_STATE
    t = m.reshape(8, 8, C, 8, P)
    d = jnp.stack([t[:, g, :, g, :] for g in range(8)], axis=1)
    return jnp.transpose(d.reshape(S5_GROUPS, C, P), (1, 0, 2))


def _bd_c(c):
    C, P = S5_GROUP_WIDTH, S5_STATE
    t = jnp.transpose(c, (0, 2, 1)).reshape(8, 8, P, C)
    eye = jnp.eye(8, dtype=t.dtype)
    return (t[:, :, :, None, :] * eye[None, :, None, :, None]).reshape(8, 8 * P, 8 * C)


def _bd_c_extract(m):
    C, P = S5_GROUP_WIDTH, S5_STATE
    t = m.reshape(8, 8, P, 8, C)
    d = jnp.stack([t[:, g, :, g, :] for g in range(8)], axis=1)
    return jnp.transpose(d.reshape(S5_GROUPS, P, C), (0, 2, 1))


def _perm_matrix():
    r = jnp.arange(S5_TC)
    src = (r % 8) * S5_SEG + r // 8
    return (src[:, None] == jnp.arange(S5_TC)[None, :]).astype(BF16)


def _coords():
    return lax.axis_index("x"), lax.axis_index("y"), lax.axis_index("c")


def _all_gather(name, arrs):
    n = len(arrs)

    def body(*refs):
        ins, outs = refs[:n], refs[n:2 * n]
        send_sems, recv_sems, local_sems = refs[2 * n:]
        x, y, c = _coords()
        me, sibling = (x, y, c), (x, y, 1 - c)
        chips = [(1 - x, y), (x, 1 - y), (1 - x, 1 - y)]

        def slot(p):
            return 4 * p[0] + 2 * p[1] + p[2]

        def copy(a, k, block, to, src=None):
            dst = outs[a].at[slot(block)]
            return pltpu.make_async_remote_copy(
                src_ref=dst if src is None else src, dst_ref=dst,
                send_sem=send_sems.at[a * 7 + k], recv_sem=recv_sems.at[a * 7 + k],
                device_id=to, device_id_type=MESH)

        mine = [pltpu.make_async_copy(ins[a], outs[a].at[slot(me)], local_sems.at[a]) for a in range(n)]
        for m in mine:
            m.start()
        first = []
        for a in range(n):
            first.append(copy(a, 0, me, sibling, src=ins[a]))
            first += [copy(a, 1 + j, me, (*chip, c), src=ins[a]) for j, chip in enumerate(chips)]
        for cp in first:
            cp.start()
        passed = []
        for j, chip in enumerate(chips):
            for a in range(n):
                copy(a, 1 + j, (*chip, c), me).wait_recv()
                fw = copy(a, 4 + j, (*chip, c), sibling)
                fw.start()
                passed.append(fw)
        for a in range(n):
            copy(a, 0, sibling, me).wait_recv()
            for j, chip in enumerate(chips):
                copy(a, 4 + j, (*chip, 1 - c), me).wait_recv()
        for cp in first + passed:
            cp.wait_send()
        for m in mine:
            m.wait()

    return pl.pallas_call(
        body, name=name,
        in_specs=[ANY] * n, out_specs=[ANY] * n,
        out_shape=[SDS((NDEV,) + a.shape, a.dtype) for a in arrs],
        scratch_shapes=[pltpu.SemaphoreType.DMA((7 * n,)), pltpu.SemaphoreType.DMA((7 * n,)),
                        pltpu.SemaphoreType.DMA((n,))],
    )(*arrs)


HBM = pl.BlockSpec(memory_space=pltpu.HBM)
SEM = pl.BlockSpec(memory_space=pltpu.SEMAPHORE)
EFFECT = pltpu.SideEffectType.DATAFLOW_SIDE_EFFECTING


def _peers7(x, y, c):
    return [(1 - x if fx else x, 1 - y if fy else y, 1 - c if fc else c)
            for fx in (0, 1) for fy in (0, 1) for fc in (0, 1) if fx or fy or fc]


def _slot(p):
    return 4 * p[0] + 2 * p[1] + p[2]


def _split_copies(src_refs, land_refs, send_sems, recv_sems, gather, mine):
    x, y, c = _coords()
    me = (x, y, c)
    out = []
    for a, (src, land) in enumerate(zip(src_refs, land_refs)):
        for k, p in enumerate(_peers7(x, y, c)):
            s = src if gather else src.at[_slot(p)]
            out.append(pltpu.make_async_remote_copy(
                src_ref=s, dst_ref=land.at[_slot(me) if mine else _slot(p)],
                send_sem=send_sems.at[a * 7 + k], recv_sem=recv_sems.at[a * 7 + k],
                device_id=p, device_id_type=MESH))
    return out


def _own_slab(shard):
    x, y, c = _coords()
    z = lax.empty((NDEV,) + shard.shape, shard.dtype)
    return lax.dynamic_update_slice(z, shard[None], (_slot((x, y, c)),) + (0,) * shard.ndim)


def _split_start(name, srcs, lands, gather):
    n = len(srcs)

    def body(*refs):
        src_refs, land_refs = refs[:n], refs[n:2 * n]
        send_sems, recv_sems = refs[2 * n], refs[2 * n + 1]
        token = refs[-1]
        for cp in _split_copies(src_refs, land_refs, send_sems, recv_sems, gather, True):
            cp.start()
        token[...] = jnp.zeros_like(token)

    thru = [pltpu.HBM(a.shape, a.dtype) for a in list(srcs) + list(lands)]
    res = pl.pallas_call(
        body, name=name,
        out_shape=(pltpu.SemaphoreType.DMA((7 * n,)), pltpu.SemaphoreType.DMA((7 * n,)), *thru, SDS((8, 128), F32)),
        in_specs=[HBM] * (2 * n),
        out_specs=(SEM, SEM, *([HBM] * (2 * n)), pl.BlockSpec(memory_space=pltpu.VMEM)),
        input_output_aliases={i: 2 + i for i in range(2 * n)},
        compiler_params=pltpu.CompilerParams(has_side_effects=EFFECT),
    )(*[pltpu.with_memory_space_constraint(a, pltpu.HBM) for a in list(srcs) + list(lands)])
    return res[0], res[1], list(res[2:2 + n]), list(res[2 + n:2 + 2 * n]), res[-1]


def _split_wait(name, started, after, gather):
    send_sems, recv_sems, srcs, lands, _ = started
    n = len(srcs)

    def body(*refs):
        src_refs, land_refs = refs[:n], refs[n:2 * n]
        s_sems, r_sems = refs[2 * n], refs[2 * n + 1]
        for cp in _split_copies(src_refs, land_refs, s_sems, r_sems, gather, False):
            cp.wait_send()
            cp.wait_recv()

    thru = [pltpu.HBM(a.shape, a.dtype) for a in list(srcs) + list(lands)]
    res = pl.pallas_call(
        body, name=name, out_shape=tuple(thru),
        in_specs=[HBM] * (2 * n) + [SEM, SEM, ANY], out_specs=tuple([HBM] * (2 * n)),
        input_output_aliases={i: i for i in range(2 * n)},
        compiler_params=pltpu.CompilerParams(has_side_effects=EFFECT),
    )(*srcs, *lands, send_sems, recv_sems, after)
    return list(res[n:])


def _adam_math(w, g, m, v):
    m = ADAM_B1 * m + (1.0 - ADAM_B1) * g
    v = ADAM_B2 * v + (1.0 - ADAM_B2) * (g * g)
    m_hat = m / (1.0 - ADAM_B1 ** ADAM_STEP)
    v_hat = v / (1.0 - ADAM_B2 ** ADAM_STEP)
    delta = -ADAM_LR * (m_hat / (jnp.sqrt(v_hat) + ADAM_EPS) + ADAM_WD * w)
    return delta, m, v


def _adam_sharded(name, recv, sub, w, m, v):
    R, Cc = w.shape
    tr = max(t for t in range(16, R + 1, 16) if R % t == 0 and t * Cc <= 256 * 1024)

    def body(*refs):
        parts = refs[:NDEV]
        w_ref, m_ref, v_ref, g_out, d_out, m_out, v_out = refs[NDEV:]
        g = parts[0][...].astype(F32)
        for p in parts[1:]:
            g = g + p[...].astype(F32)
        delta, mn, vn = _adam_math(w_ref[...], g, m_ref[...], v_ref[...])
        g_out[...] = g
        d_out[...] = delta
        m_out[...] = mn
        v_out[...] = vn

    if sub is None:
        pspecs = [Block((None, tr, Cc), functools.partial(lambda s, i: (s, i, 0), s)) for s in range(NDEV)]
    else:
        pspecs = [Block((None, None, tr, Cc), functools.partial(lambda s, i: (s, sub, i, 0), s)) for s in range(NDEV)]
    row = Block((tr, Cc), lambda i: (i, 0))
    o = SDS((R, Cc), F32)
    return pl.pallas_call(
        body, name=name, grid=(R // tr,),
        in_specs=pspecs + [row, row, row], out_specs=[row, row, row, row], out_shape=[o, o, o, o],
        compiler_params=_cparams(("arbitrary",), VMEM_BIG))(*([recv] * NDEV), w, m, v)


def _adam_small(parts, w, m, v):
    R = w.shape[0]

    def body(p_ref, w_ref, m_ref, v_ref, g_out, d_out, m_out, v_out):
        g = p_ref[0]
        for s in range(1, NDEV):
            g = g + p_ref[s]
        delta, mn, vn = _adam_math(w_ref[...], g, m_ref[...], v_ref[...])
        g_out[...] = g
        d_out[...] = delta
        m_out[...] = mn
        v_out[...] = vn

    o = SDS((R, 128), F32)
    return pl.pallas_call(body, name="adam_small", out_shape=[o, o, o, o],
                          compiler_params=_cparams(None, VMEM_BIG))(parts, w, m, v)


_SMALL = ["ffn1_norm", "mix_norm", "s5_a_re", "s5_a_im", "s5_log_dt", "s5_b_re", "s5_b_im", "s5_c_re", "s5_c_im",
          "s5_d", "s5_b_glu", "sgu_ln_g", "sgu_ln_b", "sgu_w_s", "sgu_b_s", "b_gate", "ffn2_norm", "final_norm"]
_SHARDED = ["ffn1_w_gate", "ffn1_w_up", "ffn1_w_down", "w_in", "s5_w_glu", "w_branch_a", "w_branch_b", "w_gate",
            "w_out", "ffn2_w_gate", "ffn2_w_up", "ffn2_w_down"]
_ORDER = ["ffn1_norm", "ffn1_w_gate", "ffn1_w_up", "ffn1_w_down", "mix_norm", "w_in", "s5_a_re", "s5_a_im",
          "s5_log_dt", "s5_b_re", "s5_b_im", "s5_c_re", "s5_c_im", "s5_d", "s5_w_glu", "s5_b_glu", "sgu_ln_g",
          "sgu_ln_b", "sgu_w_s", "sgu_b_s", "w_branch_a", "w_branch_b", "w_gate", "b_gate", "w_out", "ffn2_norm",
          "ffn2_w_gate", "ffn2_w_up", "ffn2_w_down", "final_norm"]


def _step(x, tgt, W, M, V):
    T = x.shape[1]
    x0 = x[0]
    tgt0 = tgt[0]
    bf = lambda a: a.astype(BF16)

    def gather_start(name, shards):
        return _split_start(name, shards, [_own_slab(s) for s in shards], True)

    gs1 = gather_start("gather1_start", [jnp.stack([bf(W["ffn1_w_gate"][0]), bf(W["ffn1_w_up"][0])])])
    gs2 = gather_start("gather2_start", [bf(W["ffn1_w_down"][0]), bf(W["w_in"][0]), bf(W["s5_w_glu"][0])])
    gs3 = gather_start("gather3_start", [bf(W["w_gate"][0]), bf(W["w_branch_a"][0]), bf(W["w_branch_b"][0]),
                                         bf(W["w_out"][0])])
    gs4 = gather_start("gather4_start", [jnp.stack([bf(W["ffn2_w_gate"][0]), bf(W["ffn2_w_up"][0])]),
                                         bf(W["ffn2_w_down"][0])])
    started = gs1[4][:1, :1] + gs2[4][:1, :1] + gs3[4][:1, :1] + gs4[4][:1, :1]

    lr_, li_ = W["s5_a_re"][0], W["s5_a_im"][0]
    ldt_ = W["s5_log_dt"][0][:, None]
    brt = jnp.transpose(W["s5_b_re"][0], (2, 0, 1))
    bit = jnp.transpose(W["s5_b_im"][0], (2, 0, 1))
    abr, abi, asr, asi, bkr_t, bki_t = _s5_prep(lr_, li_, ldt_, brt, bit)
    bdbr, bdbi = bf(_bd_b(bkr_t)), bf(_bd_b(bki_t))
    bdcr, bdci = bf(_bd_c(W["s5_c_re"][0])), bf(_bd_c(-W["s5_c_im"][0]))
    flat = lambda a: a.reshape(1, S5_NS)
    s5a = (_perm_matrix(), _perm_matrix().T, bdbr, bdbi, bdcr, bdci, flat(abr), flat(abi), flat(asr), flat(asi),
           W["s5_d"][0].reshape(1, S5_WIDTH))
    blk = jnp.arange(MLP_CHUNK) // CHUNK
    mask = blk[:, None] >= blk[None, :]
    wsm = jnp.where(mask[None], W["sgu_w_s"][0], 0.0)
    wsm_b, wsmt_b = bf(wsm), bf(jnp.transpose(wsm, (0, 2, 1)))
    bst = jnp.broadcast_to(W["sgu_b_s"][0][:, :, None], (SGU_HEADS, MLP_CHUNK, 128))
    bgate2 = W["b_gate"].reshape(2, 1, D_MODEL)

    h1 = _rms_fwd("rms1", x0, W["ffn1_norm"] + started)
    (wgu1,) = _split_wait("gather1_wait", gs1, h1, True)
    ab1, f1 = _ffn_up("ffn1_up", h1, wgu1)
    wd1, win, wglu = _split_wait("gather2_wait", gs2, f1, True)
    wglu = wglu.reshape(S5_WIDTH, S5_WIDTH)
    s5c = s5a + (wglu, W["s5_b_glu"])
    x1 = _ffn_down("ffn1_down", f1, wd1, x0)
    h2 = _rms_fwd("rms2", x1, W["mix_norm"])
    proj = _col_fwd("w_in", h2, win)
    ya, xs, ypre = _s5_fwd(proj, *s5c)
    yb = _sgu_fwd("sgu_fwd", proj, W["sgu_ln_g"], W["sgu_ln_b"], wsm_b, bst)
    wgate, wba, wbb, wout = _split_wait("gather3_wait", gs3, yb, True)
    wout = wout.reshape(D_MODEL, D_MODEL)
    pa = _col_fwd("branch_a", ya, wba)
    pb = _col_fwd("branch_b", yb, wbb)
    gl = _gate_fwd("gate", h2, wgate, bgate2)
    merged = _merge_fwd("merge", pa, pb, gl)
    x2 = _plain_fwd_res("w_out", merged, wout, x1)
    h3 = _rms_fwd("rms3", x2, W["ffn2_norm"])
    wgu2, wd2 = _split_wait("gather4_wait", gs4, h3, True)
    ab2, f2 = _ffn_up("ffn2_up", h3, wgu2)
    x3 = _ffn_down("ffn2_down", f2, wd2, x2)
    loss_p, dx3, dx3b, dgf = _loss_head("loss_head", x3, W["final_norm"].reshape(1, D_MODEL), tgt0)

    def exchange_start(name, grads):
        x_, y_, c_ = _coords()
        me = _slot((x_, y_, c_))
        return _split_start(name, grads, [_own_slab(lax.dynamic_index_in_dim(g, me, 0, keepdims=False))
                                          for g in grads], False)

    dab2 = _ffn_down_bwd_act("ffn2_down_bwd_a", dx3b, wd2, ab2)
    g_wd2 = _ffn_down_bwd_w("ffn2_down_bwd_w", f2, dx3b)
    g_gu2 = _ffn_up_bwd_w("ffn2_up_bwd_w", h3, dab2)
    es1 = exchange_start("exchange1_start", [g_wd2, g_gu2])
    dh3 = _ffn_up_bwd_h("ffn2_up_bwd_h", dab2, wgu2, es1[4])
    dx2, dx2b, dg3 = _rms_bwd("rms3_bwd", dh3, x2, W["ffn2_norm"], dx3)

    dmerged = _plain_bwd_a("w_out_bwd_a", dx2b, wout)
    g_wout = _plain_bwd_w("w_out_bwd_w", merged, dx2b)
    dpa, dpb, dgl, dbgate = _merge_bwd("merge_bwd", dmerged, pa, pb, gl)
    dya = _col_bwd_a("branch_a_bwd_a", dpa, wba)
    g_wba = _col_bwd_w("branch_a_bwd_w", ya, dpa, 256)
    dyb = _col_bwd_a("branch_b_bwd_a", dpb, wbb)
    g_wbb = _col_bwd_w("branch_b_bwd_w", yb, dpb, 256)
    dh2g = _gate_bwd_a("gate_bwd_a", dgl, wgate)
    g_wgate = _gate_bwd_w("gate_bwd_w", h2, dgl, 512)
    duv, dws, dbst, dlng, dlnb = _sgu_bwd("sgu_bwd", dyb, proj, W["sgu_ln_g"], W["sgu_ln_b"], wsm_b, wsmt_b, bst)
    (dua, dar8, dai8, ddv, dbglu, dbdbr, dbdbi, dbdcr, dbdci, g_wglu) = _s5_bwd(dya, proj, ypre, xs, *s5c)
    dproj = jnp.concatenate([dua, duv], axis=1)
    g_win = _col_bwd_w("w_in_bwd_w", h2, dproj, 384)
    g_wout3 = g_wout.reshape(NDEV, D_MODEL // NDEV, D_MODEL)
    g_wglu3 = g_wglu.astype(BF16).reshape(NDEV, S5_WIDTH // NDEV, S5_WIDTH)
    es2 = exchange_start("exchange2_start", [g_wout3, g_wba, g_wbb, g_wgate, g_wglu3, g_win])
    dh2 = _col_bwd_a("w_in_bwd_a", dproj, win, add=dh2g)
    dx1, dx1b, dgm = _rms_bwd("rms2_bwd", dh2, x1, W["mix_norm"] + es2[4][:1, :1], dx2)

    dab1 = _ffn_down_bwd_act("ffn1_down_bwd_a", dx1b, wd1, ab1)
    g_wd1 = _ffn_down_bwd_w("ffn1_down_bwd_w", f1, dx1b)
    g_gu1 = _ffn_up_bwd_w("ffn1_up_bwd_w", h1, dab1)
    es3 = exchange_start("exchange3_start", [g_wd1, g_gu1])
    dh1 = _ffn_up_bwd_h("ffn1_up_bwd_h", dab1, wgu1, es3[4])
    dx0, _, dg1 = _rms_bwd("rms1_bwd", dh1, x0, W["ffn1_norm"], dx1)

    dabr = jnp.sum(dar8, axis=0).reshape(S5_GROUPS, S5_STATE)
    dabi = jnp.sum(dai8, axis=0).reshape(S5_GROUPS, S5_STATE)
    d_lr, d_li, d_ldt, d_brt, d_bit = _s5_prep_bwd(lr_, li_, ldt_, brt, bit, dabr, dabi,
                                                   _bd_b_extract(dbdbr), _bd_b_extract(dbdbi))
    small_g = {
        "ffn1_norm": dg1, "mix_norm": dgm, "ffn2_norm": dg3, "final_norm": dgf,
        "s5_a_re": d_lr, "s5_a_im": d_li, "s5_log_dt": d_ldt,
        "s5_b_re": jnp.transpose(d_brt, (1, 2, 0)), "s5_b_im": jnp.transpose(d_bit, (1, 2, 0)),
        "s5_c_re": _bd_c_extract(dbdcr), "s5_c_im": -_bd_c_extract(dbdci),
        "s5_d": ddv, "s5_b_glu": dbglu, "sgu_ln_g": dlng, "sgu_ln_b": dlnb,
        "sgu_w_s": jnp.where(mask[None], dws, 0.0), "sgu_b_s": dbst[:, :, 0], "b_gate": dbgate,
    }

    sizes = [W[n].size for n in _SMALL]
    total = sum(sizes) + 1
    rows = -(-total // 128)
    rows = -(-rows // 8) * 8
    pad = rows * 128 - total

    def pack(d, extra):
        return jnp.concatenate([d[n].reshape(-1).astype(F32) for n in _SMALL] + [extra, jnp.zeros((pad,), F32)]
                               ).reshape(rows, 128)

    zero1 = jnp.zeros((1,), F32)
    parts = _all_gather("gather_small_grads", [pack(small_g, loss_p[0, :1])])[0]
    sg, sd, sm, sv = _adam_small(parts, pack(W, zero1), pack(M, zero1), pack(V, zero1))

    def unpack(flat2d):
        flat = flat2d.reshape(-1)
        out, off = {}, 0
        for n, s in zip(_SMALL, sizes):
            out[n] = flat[off:off + s].reshape(W[n].shape)
            off += s
        return out, flat[off]

    G, loss = unpack(sg)
    Dl, _ = unpack(sd)
    Mn, _ = unpack(sm)
    Vn, _ = unpack(sv)

    def adam(plan):
        last = None
        for n, recv, sub in plan:
            g, d, mn, vn = _adam_sharded("adam_" + n, recv, sub, W[n][0], M[n][0], V[n][0])
            G[n], Dl[n], Mn[n], Vn[n] = g[None], d[None], mn[None], vn[None]
            last = g
        return last

    r_wd2, r_gu2 = _split_wait("exchange1_wait", es1, sg, False)
    done = adam([("ffn2_w_down", r_wd2, None), ("ffn2_w_gate", r_gu2, 0), ("ffn2_w_up", r_gu2, 1)])
    r_wout, r_wba, r_wbb, r_wgate, r_wglu, r_win = _split_wait("exchange2_wait", es2, done, False)
    done = adam([("w_out", r_wout, None), ("w_branch_a", r_wba, None), ("w_branch_b", r_wbb, None),
                 ("w_gate", r_wgate, None), ("s5_w_glu", r_wglu, None), ("w_in", r_win, None)])
    r_wd1, r_gu1 = _split_wait("exchange3_wait", es3, done, False)
    adam([("ffn1_w_down", r_wd1, None), ("ffn1_w_gate", r_gu1, 0), ("ffn1_w_up", r_gu1, 1)])

    return loss, dx0[None], G, Dl, Mn, Vn


def kernel(x, ffn1_norm, ffn1_w_gate, ffn1_w_up, ffn1_w_down, mix_norm, w_in, s5_a_re, s5_a_im, s5_log_dt, s5_b_re, s5_b_im, s5_c_re, s5_c_im, s5_d, s5_w_glu, s5_b_glu, sgu_ln_g, sgu_ln_b, sgu_w_s, sgu_b_s, w_branch_a, w_branch_b, w_gate, b_gate, w_out, ffn2_norm, ffn2_w_gate, ffn2_w_up, ffn2_w_down, final_norm, loss_target, m_ffn1_norm, m_ffn1_w_gate, m_ffn1_w_up, m_ffn1_w_down, m_mix_norm, m_w_in, m_s5_a_re, m_s5_a_im, m_s5_log_dt, m_s5_b_re, m_s5_b_im, m_s5_c_re, m_s5_c_im, m_s5_d, m_s5_w_glu, m_s5_b_glu, m_sgu_ln_g, m_sgu_ln_b, m_sgu_w_s, m_sgu_b_s, m_w_branch_a, m_w_branch_b, m_w_gate, m_b_gate, m_w_out, m_ffn2_norm, m_ffn2_w_gate, m_ffn2_w_up, m_ffn2_w_down, m_final_norm, v_ffn1_norm, v_ffn1_w_gate, v_ffn1_w_up, v_ffn1_w_down, v_mix_norm, v_w_in, v_s5_a_re, v_s5_a_im, v_s5_log_dt, v_s5_b_re, v_s5_b_im, v_s5_c_re, v_s5_c_im, v_s5_d, v_s5_w_glu, v_s5_b_glu, v_sgu_ln_g, v_sgu_ln_b, v_sgu_w_s, v_sgu_b_s, v_w_branch_a, v_w_branch_b, v_w_gate, v_b_gate, v_w_out, v_ffn2_norm, v_ffn2_w_gate, v_ffn2_w_up, v_ffn2_w_down, v_final_norm):
    a = locals()
    W = {n: a[n] for n in _ORDER}
    M = {n: a["m_" + n] for n in _ORDER}
    V = {n: a["v_" + n] for n in _ORDER}
    loss, gx, G, Dl, Mn, Vn = _step(x, loss_target, W, M, V)
    return (loss, gx, *[G[n] for n in _ORDER], *[Dl[n] for n in _ORDER], *[Mn[n] for n in _ORDER],
            *[Vn[n] for n in _ORDER])
```

```python
import functools
import math

import jax
import jax.numpy as jnp
from jax import lax
from jax.experimental import pallas as pl
from jax.experimental.pallas import tpu as pltpu

F32 = jnp.float32
BF16 = jnp.bfloat16
NDEV = 8
NORM_EPS = 1e-6
D_MODEL = 2048
D_FF = 5632
FF_SHARD = D_FF // NDEV
S5_WIDTH = 1024
S5_GROUPS = 64
S5_GROUP_WIDTH = 16
S5_STATE = 64
S5_NS = S5_GROUPS * S5_STATE
SGU_WIDTH = 1024
SGU_HEADS = 8
MLP_CHUNK = 128
CHUNK = 64
ADAM_LR, ADAM_B1, ADAM_B2, ADAM_EPS, ADAM_WD, ADAM_STEP = 0.001, 0.9, 0.999, 1e-08, 0.01, 10
S5_TC = 256
S5_SEG = S5_TC // 8
S5_LG = 512
VMEM_BIG = 56 * 1024 * 1024

MESH = pl.DeviceIdType.MESH
SDS = jax.ShapeDtypeStruct
Block = pl.BlockSpec
ANY = pl.BlockSpec(memory_space=pl.ANY)


def _cparams(sem=None, vmem=None):
    return pltpu.CompilerParams(dimension_semantics=sem, vmem_limit_bytes=vmem)


def _const(shape):
    nd = len(shape)
    return pl.BlockSpec(shape, lambda i: (0,) * nd, pipeline_mode=pl.Buffered(1))


def _sigmoid(x):
    return 0.5 * jnp.tanh(0.5 * x) + 0.5


_GELU_C = math.sqrt(2.0 / math.pi)


def _gelu(x):
    return 0.5 * x * (1.0 + jnp.tanh(_GELU_C * (x + 0.044715 * x * x * x)))


def _gelu_grad(x):
    t = jnp.tanh(_GELU_C * (x + 0.044715 * x * x * x))
    return 0.5 * (1.0 + t) + 0.5 * x * (1.0 - t * t) * _GELU_C * (1.0 + 3.0 * 0.044715 * x * x)


NN = (((1,), (0,)), ((), ()))
NT = (((1,), (1,)), ((), ()))
TN = (((0,), (0,)), ((), ()))


def _dot(a, b, dims=NN):
    return lax.dot_general(a, b, dims, preferred_element_type=F32)


def _matmul(name, a, b, extras, *, grid, a_spec, b_spec, extra_specs, out_shapes, out_specs, acc_shape,
            epilogue, dims=NN, nb=None, compute=None, vmem=VMEM_BIG):
    nk = grid[2]
    ne, no = len(extras), len(out_shapes)
    nacc = nb or 1
    if compute is None:
        def compute(a_ref, b_ref, q):
            return _dot(a_ref[...], b_ref[q] if nb else b_ref[...], dims)

    def body(*refs):
        a_ref, b_ref = refs[0], refs[1]
        ex = refs[2:2 + ne]
        outs = refs[2 + ne:2 + ne + no]
        if nk == 1:
            epilogue([compute(a_ref, b_ref, q) for q in range(nacc)], ex, outs)
            return
        acc_ref = refs[2 + ne + no]
        k = pl.program_id(2)

        @pl.when(k == 0)
        def _():
            acc_ref[...] = jnp.zeros_like(acc_ref)

        for q in range(nacc):
            acc_ref[q] += compute(a_ref, b_ref, q)

        @pl.when(k == nk - 1)
        def _():
            epilogue([acc_ref[q] for q in range(nacc)], ex, outs)

    scratch = [] if nk == 1 else [pltpu.VMEM((nacc,) + tuple(acc_shape), F32)]
    res = pl.pallas_call(
        body, name=name, grid=grid,
        in_specs=[a_spec, b_spec] + list(extra_specs),
        out_specs=list(out_specs), out_shape=list(out_shapes), scratch_shapes=scratch,
        compiler_params=_cparams(("parallel", "parallel", "arbitrary"), vmem),
    )(a, b, *extras)
    return res


def _store(dtype_outs=None):
    def ep(accs, ex, outs):
        outs[0][...] = accs[0].astype(outs[0].dtype)
    return ep


def _tile(n, t):
    t = min(n, t)
    assert n % t == 0, (n, t)
    return t


def _ksum(kq, dims):
    def compute(a_ref, b_ref, _):
        part = _dot(a_ref[0], b_ref[0], dims)
        for q in range(1, kq):
            part = part + _dot(a_ref[q], b_ref[q], dims)
        return part
    return compute


def _ksum_lanes(kq, ns, dims):
    def compute(a_ref, b_ref, _):
        part = _dot(a_ref[:, 0:ns], b_ref[0], dims)
        for q in range(1, kq):
            part = part + _dot(a_ref[:, q * ns:(q + 1) * ns], b_ref[q], dims)
        return part
    return compute


def _wide_b(g):
    def compute(a_ref, b_ref, _):
        bw = b_ref[0] if g == 1 else jnp.concatenate([b_ref[q] for q in range(g)], axis=1)
        return _dot(a_ref[...], bw, NN)
    return compute


KQ_DOWN = 4
TT_DEEP = 2048


def _ffn_up(name, h, wgu, after=None):
    T, D = h.shape
    tm = _tile(T, 1024)

    def ep(accs, ex, outs):
        a, b = accs
        outs[0][0] = a.astype(BF16)
        outs[0][1] = b.astype(BF16)
        outs[1][...] = (a * _sigmoid(a) * b).astype(BF16)

    extras = () if after is None else (after,)
    return _matmul(
        name, h, wgu, extras, grid=(NDEV, T // tm, 1),
        a_spec=Block((tm, D), lambda j, i, k: (i, 0)),
        b_spec=Block((None, 2, D, FF_SHARD), lambda j, i, k: (j, 0, 0, 0)),
        extra_specs=[Block((8, 128), lambda j, i, k: (0, 0))] * len(extras),
        out_shapes=[SDS((NDEV, 2, T, FF_SHARD), BF16), SDS((NDEV, T, FF_SHARD), BF16)],
        out_specs=[Block((None, 2, tm, FF_SHARD), lambda j, i, k: (j, 0, i, 0)),
                   Block((None, tm, FF_SHARD), lambda j, i, k: (j, i, 0))],
        acc_shape=(tm, FF_SHARD), dims=NN, nb=2, epilogue=ep)


def _ffn_down(name, f, wd, xres):
    _, T, _ = f.shape
    tm, tn = _tile(T, 1024), 1024

    def ep(accs, ex, outs):
        outs[0][...] = ex[0][...] + 0.5 * accs[0]

    return _matmul(
        name, f, wd, (xres,), grid=(T // tm, D_MODEL // tn, NDEV // KQ_DOWN),
        a_spec=Block((KQ_DOWN, tm, FF_SHARD), lambda i, j, k: (k, i, 0)),
        b_spec=Block((KQ_DOWN, FF_SHARD, tn), lambda i, j, k: (k, 0, j)),
        extra_specs=[Block((tm, tn), lambda i, j, k: (i, j))],
        out_shapes=[SDS((T, D_MODEL), F32)],
        out_specs=[Block((tm, tn), lambda i, j, k: (i, j))],
        acc_shape=(tm, tn), compute=_ksum(KQ_DOWN, NN), epilogue=ep)[0]


def _ffn_down_bwd_act(name, dyb, wd, ab):
    T, D = dyb.shape
    tm = _tile(T, 1024)
    def ep(accs, ex, outs):
        df = accs[0]
        a = ex[0][0].astype(F32)
        b = ex[0][1].astype(F32)
        hs = 0.5 * _sigmoid(a)
        outs[0][0] = (df * b * hs * (1.0 + a * (1.0 - 2.0 * hs))).astype(BF16)
        outs[0][1] = (df * a * hs).astype(BF16)

    return _matmul(
        name, dyb, wd, (ab,), grid=(NDEV, T // tm, 1),
        a_spec=Block((tm, D), lambda j, i, k: (i, 0)),
        b_spec=Block((None, FF_SHARD, D), lambda j, i, k: (j, 0, 0)),
        extra_specs=[Block((None, 2, tm, FF_SHARD), lambda j, i, k: (j, 0, i, 0))],
        out_shapes=[SDS((NDEV, 2, T, FF_SHARD), BF16)],
        out_specs=[Block((None, 2, tm, FF_SHARD), lambda j, i, k: (j, 0, i, 0))],
        acc_shape=(tm, FF_SHARD), dims=NT, epilogue=ep)[0]


def _ffn_down_bwd_w(name, f, dyb):
    _, T, _ = f.shape
    tt, tn = _tile(T, TT_DEEP), 1024

    def ep(accs, ex, outs):
        outs[0][...] = (0.5 * accs[0]).astype(BF16)

    return _matmul(
        name, f, dyb, (), grid=(NDEV, D_MODEL // tn, T // tt),
        a_spec=Block((None, tt, FF_SHARD), lambda j, n, k: (j, k, 0)),
        b_spec=Block((tt, tn), lambda j, n, k: (k, n)),
        extra_specs=(),
        out_shapes=[SDS((NDEV, FF_SHARD, D_MODEL), BF16)],
        out_specs=[Block((None, FF_SHARD, tn), lambda j, n, k: (j, 0, n))],
        acc_shape=(FF_SHARD, tn), dims=TN, epilogue=ep)[0]


def _ffn_up_bwd_h(name, dab, wgu, after):
    _, _, T, _ = dab.shape
    tm = _tile(T, 1024)
    return _matmul(
        name, dab, wgu, (after,), grid=(T // tm, 1, NDEV),
        a_spec=Block((None, 2, tm, FF_SHARD), lambda i, j, k: (k, 0, i, 0)),
        b_spec=Block((None, 2, D_MODEL, FF_SHARD), lambda i, j, k: (k, 0, 0, 0)),
        extra_specs=[Block((8, 128), lambda i, j, k: (0, 0))],
        out_shapes=[SDS((T, D_MODEL), BF16)],
        out_specs=[Block((tm, D_MODEL), lambda i, j, k: (i, 0))],
        acc_shape=(tm, D_MODEL), compute=_ksum(2, NT), epilogue=_store())[0]


def _ffn_up_bwd_w(name, h, dab):
    T, D = h.shape
    tt, tr = _tile(T, TT_DEEP), 1024

    def ep(accs, ex, outs):
        outs[0][0] = accs[0].astype(BF16)
        outs[0][1] = accs[1].astype(BF16)

    return _matmul(
        name, h, dab, (), grid=(NDEV, D // tr, T // tt),
        a_spec=Block((tt, tr), lambda j, n, k: (k, n)),
        b_spec=Block((None, 2, tt, FF_SHARD), lambda j, n, k: (j, 0, k, 0)),
        extra_specs=(),
        out_shapes=[SDS((NDEV, 2, D, FF_SHARD), BF16)],
        out_specs=[Block((None, 2, tr, FF_SHARD), lambda j, n, k: (j, 0, n, 0))],
        acc_shape=(tr, FF_SHARD), dims=TN, nb=2, epilogue=ep)[0]


def _shards_per_step(ns):
    return max(g for g in (1, 2, 4, 8) if g * ns <= 2048)


def _split_lanes(g, ns):
    def ep(accs, ex, outs):
        for q in range(g):
            outs[0][q] = accs[0][:, q * ns:(q + 1) * ns].astype(outs[0].dtype)
    return ep


def _col_fwd(name, a, w, out_dtype=BF16):
    T, K = a.shape
    ns = w.shape[2]
    g = _shards_per_step(ns)
    tm = _tile(T, 1024)
    return _matmul(
        name, a, w, (), grid=(NDEV // g, T // tm, 1),
        a_spec=Block((tm, K), lambda j, i, k: (i, 0)),
        b_spec=Block((g, K, ns), lambda j, i, k: (j, 0, 0)),
        extra_specs=(),
        out_shapes=[SDS((T, NDEV * ns), out_dtype)],
        out_specs=[Block((tm, g * ns), lambda j, i, k: (i, j))],
        acc_shape=(tm, g * ns), compute=_wide_b(g), epilogue=_store())[0]


def _col_bwd_a(name, dy, w, add=None):
    T = dy.shape[0]
    _, K, ns = w.shape
    tm, tn = _tile(T, 1024), _tile(K, 1024)

    def ep(accs, ex, outs):
        r = accs[0]
        if add is not None:
            r = r + ex[0][...].astype(F32)
        outs[0][...] = r.astype(BF16)

    extras = () if add is None else (add,)
    return _matmul(
        name, dy, w, extras, grid=(T // tm, K // tn, 1),
        a_spec=Block((tm, NDEV * ns), lambda i, j, k: (i, 0)),
        b_spec=Block((NDEV, tn, ns), lambda i, j, k: (0, j, 0)),
        extra_specs=[Block((tm, tn), lambda i, j, k: (i, j))] * len(extras),
        out_shapes=[SDS((T, K), BF16)],
        out_specs=[Block((tm, tn), lambda i, j, k: (i, j))],
        acc_shape=(tm, tn), compute=_ksum_lanes(NDEV, ns, NT), epilogue=ep)[0]


def _col_bwd_w(name, a, dy, ns):
    T, K = a.shape
    g = _shards_per_step(ns)
    tt, tr = _tile(T, TT_DEEP), _tile(K, 1024)
    return _matmul(
        name, a, dy, (), grid=(NDEV // g, K // tr, T // tt),
        a_spec=Block((tt, tr), lambda j, n, k: (k, n)),
        b_spec=Block((tt, g * ns), lambda j, n, k: (k, j)),
        extra_specs=(),
        out_shapes=[SDS((NDEV, K, ns), BF16)],
        out_specs=[Block((g, tr, ns), lambda j, n, k: (j, n, 0))],
        acc_shape=(tr, g * ns), dims=TN, epilogue=_split_lanes(g, ns))[0]


def _gate_fwd(name, h, w, bias):
    T, K = h.shape
    ns = w.shape[2]
    g = 2
    per = D_MODEL // (g * ns)
    tm = _tile(T, 1024)

    def ep(accs, ex, outs):
        outs[0][...] = (accs[0] + ex[0][...]).astype(BF16)

    return _matmul(
        name, h, w, (bias,), grid=(NDEV // g, T // tm, 1),
        a_spec=Block((tm, K), lambda j, i, k: (i, 0)),
        b_spec=Block((g, K, ns), lambda j, i, k: (j, 0, 0)),
        extra_specs=[Block((None, 1, g * ns), lambda j, i, k: (j // per, 0, j % per))],
        out_shapes=[SDS((2, T, D_MODEL), BF16)],
        out_specs=[Block((None, tm, g * ns), lambda j, i, k: (j // per, i, j % per))],
        acc_shape=(tm, g * ns), compute=_wide_b(g), epilogue=ep)[0]


def _gate_bwd_a(name, dgl, w):
    _, T, _ = dgl.shape
    _, K, ns = w.shape
    per = D_MODEL // ns
    tm, tn = _tile(T, 1024), 1024

    def compute(a_ref, b_ref, _):
        part = None
        for q in range(NDEV):
            d = _dot(a_ref[q // per, :, (q % per) * ns:(q % per + 1) * ns], b_ref[q], NT)
            part = d if part is None else part + d
        return part

    return _matmul(
        name, dgl, w, (), grid=(T // tm, K // tn, 1),
        a_spec=Block((2, tm, D_MODEL), lambda i, j, k: (0, i, 0)),
        b_spec=Block((NDEV, tn, ns), lambda i, j, k: (0, j, 0)),
        extra_specs=(),
        out_shapes=[SDS((T, K), BF16)],
        out_specs=[Block((tm, tn), lambda i, j, k: (i, j))],
        acc_shape=(tm, tn), compute=compute, epilogue=_store())[0]


def _gate_bwd_w(name, h, dgl, ns):
    T, K = h.shape
    g = 2
    per = D_MODEL // (g * ns)
    tt, tr = _tile(T, TT_DEEP), 1024
    return _matmul(
        name, h, dgl, (), grid=(NDEV // g, K // tr, T // tt),
        a_spec=Block((tt, tr), lambda j, n, k: (k, n)),
        b_spec=Block((None, tt, g * ns), lambda j, n, k: (j // per, k, j % per)),
        extra_specs=(),
        out_shapes=[SDS((NDEV, K, ns), BF16)],
        out_specs=[Block((g, tr, ns), lambda j, n, k: (j, n, 0))],
        acc_shape=(tr, g * ns), dims=TN, epilogue=_split_lanes(g, ns))[0]


def _plain_fwd_res(name, a, w, xres):
    T, K = a.shape
    N = w.shape[1]
    tm, tn = _tile(T, 1024), _tile(N, 1024)

    def ep(accs, ex, outs):
        outs[0][...] = ex[0][...] + accs[0]

    return _matmul(
        name, a, w, (xres,), grid=(T // tm, N // tn, 1),
        a_spec=Block((tm, K), lambda i, j, k: (i, 0)),
        b_spec=Block((K, tn), lambda i, j, k: (0, j)),
        extra_specs=[Block((tm, tn), lambda i, j, k: (i, j))],
        out_shapes=[SDS((T, N), F32)],
        out_specs=[Block((tm, tn), lambda i, j, k: (i, j))],
        acc_shape=(tm, tn), dims=NN, nb=None, epilogue=ep)[0]


def _plain_bwd_a(name, dy, w):
    T, N = dy.shape
    K = w.shape[0]
    tm, tn = _tile(T, 1024), _tile(K, 1024)
    return _matmul(
        name, dy, w, (), grid=(T // tm, K // tn, 1),
        a_spec=Block((tm, N), lambda i, j, k: (i, 0)),
        b_spec=Block((tn, N), lambda i, j, k: (j, 0)),
        extra_specs=(),
        out_shapes=[SDS((T, K), BF16)],
        out_specs=[Block((tm, tn), lambda i, j, k: (i, j))],
        acc_shape=(tm, tn), dims=NT, nb=None, epilogue=_store())[0]


def _plain_bwd_w(name, a, dy):
    T, K = a.shape
    N = dy.shape[1]
    tt, tr, tn = _tile(T, TT_DEEP), _tile(K, 1024), _tile(N, 1024)
    return _matmul(
        name, a, dy, (), grid=(K // tr, N // tn, T // tt),
        a_spec=Block((tt, tr), lambda m, n, k: (k, m)),
        b_spec=Block((tt, tn), lambda m, n, k: (k, n)),
        extra_specs=(),
        out_shapes=[SDS((K, N), BF16)],
        out_specs=[Block((tr, tn), lambda m, n, k: (m, n))],
        acc_shape=(tr, tn), dims=TN, nb=None, epilogue=_store())[0]


def _rms_fwd(name, x, g):
    T, D = x.shape
    tm = _tile(T, 512)

    def body(x_ref, g_ref, h_ref):
        xv = x_ref[...]
        r = lax.rsqrt(jnp.mean(xv * xv, axis=-1, keepdims=True) + NORM_EPS)
        h_ref[...] = (xv * r * g_ref[...]).astype(BF16)

    return pl.pallas_call(
        body, name=name, grid=(T // tm,),
        in_specs=[Block((tm, D), lambda i: (i, 0)), Block((1, D), lambda i: (0, 0))],
        out_specs=Block((tm, D), lambda i: (i, 0)), out_shape=SDS((T, D), BF16),
        compiler_params=_cparams(("arbitrary",), VMEM_BIG))(x, g)


def _rms_bwd(name, dh, x, g, dxin):
    T, D = x.shape
    tm = _tile(T, 512)

    def body(dh_ref, x_ref, g_ref, dxin_ref, dx_ref, dxb_ref, dg_ref):
        i = pl.program_id(0)
        xv = x_ref[...]
        dh = dh_ref[...].astype(F32)
        r = lax.rsqrt(jnp.mean(xv * xv, axis=-1, keepdims=True) + NORM_EPS)
        xh = xv * r
        gd = dh * g_ref[...]
        dx = dxin_ref[...] + r * (gd - xh * jnp.mean(gd * xh, axis=-1, keepdims=True))
        dx_ref[...] = dx
        dxb_ref[...] = dx.astype(BF16)
        dgp = jnp.sum(dh * xh, axis=0, keepdims=True)

        @pl.when(i == 0)
        def _():
            dg_ref[...] = dgp

        @pl.when(i > 0)
        def _():
            dg_ref[...] += dgp

    row = Block((tm, D), lambda i: (i, 0))
    vec = Block((1, D), lambda i: (0, 0))
    return pl.pallas_call(
        body, name=name, grid=(T // tm,),
        in_specs=[row, row, vec, row], out_specs=[row, row, vec],
        out_shape=[SDS((T, D), F32), SDS((T, D), BF16), SDS((1, D), F32)],
        compiler_params=_cparams(("arbitrary",), VMEM_BIG))(dh, x, g, dxin)


def _loss_head(name, x, g, tgt):
    T, D = x.shape
    tm = _tile(T, 512)

    def body(x_ref, g_ref, t_ref, loss_ref, dx_ref, dxb_ref, dg_ref):
        i = pl.program_id(0)
        xv = x_ref[...]
        gv = g_ref[...]
        r = lax.rsqrt(jnp.mean(xv * xv, axis=-1, keepdims=True) + NORM_EPS)
        xh = xv * r
        err = xh * gv - t_ref[...]
        lp = 0.5 * jnp.sum(jnp.mean(err * err, axis=-1, keepdims=True), axis=0, keepdims=True)
        dout = err * (1.0 / D)
        gd = dout * gv
        dx = r * (gd - xh * jnp.mean(gd * xh, axis=-1, keepdims=True))
        dx_ref[...] = dx
        dxb_ref[...] = dx.astype(BF16)
        dgp = jnp.sum(dout * xh, axis=0, keepdims=True)
        lpb = jnp.broadcast_to(lp, (1, 128))

        @pl.when(i == 0)
        def _():
            dg_ref[...] = dgp
            loss_ref[...] = lpb

        @pl.when(i > 0)
        def _():
            dg_ref[...] += dgp
            loss_ref[...] += lpb

    row = Block((tm, D), lambda i: (i, 0))
    vec = Block((1, D), lambda i: (0, 0))
    return pl.pallas_call(
        body, name=name, grid=(T // tm,),
        in_specs=[row, vec, row], out_specs=[Block((1, 128), lambda i: (0, 0)), row, row, vec],
        out_shape=[SDS((1, 128), F32), SDS((T, D), F32), SDS((T, D), BF16), SDS((1, D), F32)],
        compiler_params=_cparams(("arbitrary",), VMEM_BIG))(x, g, tgt)


def _merge_fwd(name, pa, pb, gl):
    T, D = pa.shape
    tm = _tile(T, 512)

    def body(pa_ref, pb_ref, gl_ref, o_ref):
        ga = _sigmoid(gl_ref[0].astype(F32))
        gb = _sigmoid(gl_ref[1].astype(F32))
        o_ref[...] = (ga * pa_ref[...].astype(F32) + gb * pb_ref[...].astype(F32)).astype(BF16)

    row = Block((tm, D), lambda i: (i, 0))
    return pl.pallas_call(
        body, name=name, grid=(T // tm,),
        in_specs=[row, row, Block((2, tm, D), lambda i: (0, i, 0))], out_specs=row,
        out_shape=SDS((T, D), BF16), compiler_params=_cparams(("arbitrary",), VMEM_BIG))(pa, pb, gl)


def _merge_bwd(name, dm, pa, pb, gl):
    T, D = pa.shape
    tm = _tile(T, 512)

    def body(dm_ref, pa_ref, pb_ref, gl_ref, dpa_ref, dpb_ref, dgl_ref, db_ref):
        i = pl.program_id(0)
        dmv = dm_ref[...].astype(F32)
        ga = _sigmoid(gl_ref[0].astype(F32))
        gb = _sigmoid(gl_ref[1].astype(F32))
        dpa_ref[...] = (dmv * ga).astype(BF16)
        dpb_ref[...] = (dmv * gb).astype(BF16)
        dga = dmv * pa_ref[...].astype(F32) * ga * (1.0 - ga)
        dgb = dmv * pb_ref[...].astype(F32) * gb * (1.0 - gb)
        dgl_ref[0] = dga.astype(BF16)
        dgl_ref[1] = dgb.astype(BF16)
        sa = jnp.sum(dga, axis=0, keepdims=True)
        sb = jnp.sum(dgb, axis=0, keepdims=True)

        @pl.when(i == 0)
        def _():
            db_ref[0] = sa
            db_ref[1] = sb

        @pl.when(i > 0)
        def _():
            db_ref[0] += sa
            db_ref[1] += sb

    row = Block((tm, D), lambda i: (i, 0))
    two = Block((2, tm, D), lambda i: (0, i, 0))
    return pl.pallas_call(
        body, name=name, grid=(T // tm,),
        in_specs=[row, row, row, two], out_specs=[row, row, two, Block((2, 1, D), lambda i: (0, 0, 0))],
        out_shape=[SDS((T, D), BF16), SDS((T, D), BF16), SDS((2, T, D), BF16), SDS((2, 1, D), F32)],
        compiler_params=_cparams(("arbitrary",), VMEM_BIG))(dm, pa, pb, gl)


def _sgu_core(ur, vr, lng, lnb, ws_ref, bs_ref):
    tm = ur.shape[0]
    gu = _gelu(ur)
    gv = _gelu(vr)
    mu = jnp.mean(gv, axis=-1, keepdims=True)
    cen = gv - mu
    rstd = lax.rsqrt(jnp.mean(cen * cen, axis=-1, keepdims=True) + NORM_EPS)
    xhat = cen * rstd
    vn = (xhat * lng + lnb).astype(BF16)
    rows = []
    for n in range(tm // MLP_CHUNK):
        cols = []
        for h in range(SGU_HEADS):
            blk = vn[n * MLP_CHUNK:(n + 1) * MLP_CHUNK, h * 128:(h + 1) * 128]
            cols.append(_dot(ws_ref[h], blk) + bs_ref[h])
        rows.append(jnp.concatenate(cols, axis=1))
    mixed = jnp.concatenate(rows, axis=0) if len(rows) > 1 else rows[0]
    return gu, xhat, rstd, vn, mixed


def _sgu_fwd(name, proj, lng, lnb, wsm, bst):
    T = proj.shape[0]
    W = SGU_WIDTH
    tm = _tile(T, 512)

    def body(u_ref, v_ref, lng_ref, lnb_ref, ws_ref, bs_ref, o_ref):
        gu, _, _, _, mixed = _sgu_core(u_ref[...].astype(F32), v_ref[...].astype(F32), lng_ref[...], lnb_ref[...],
                                       ws_ref, bs_ref)
        o_ref[...] = (gu * mixed).astype(BF16)

    vec = Block((1, W), lambda i: (0, 0))
    return pl.pallas_call(
        body, name=name, grid=(T // tm,),
        in_specs=[Block((tm, W), lambda i: (i, 1)), Block((tm, W), lambda i: (i, 2)), vec, vec,
                  Block((SGU_HEADS, 128, 128), lambda i: (0, 0, 0)), Block((SGU_HEADS, 128, 128), lambda i: (0, 0, 0))],
        out_specs=Block((tm, W), lambda i: (i, 0)), out_shape=SDS((T, W), BF16),
        compiler_params=_cparams(("arbitrary",), VMEM_BIG))(proj, proj, lng, lnb, wsm, bst)


def _sgu_bwd(name, dyb, proj, lng, lnb, wsm, wsmt, bst):
    T = proj.shape[0]
    W = SGU_WIDTH
    tm = _tile(T, 512)

    def body(dy_ref, u_ref, v_ref, lng_ref, lnb_ref, ws_ref, wst_ref, bs_ref,
             duv_ref, dws_ref, dbs_ref, dlng_ref, dlnb_ref):
        i = pl.program_id(0)
        ur = u_ref[...].astype(F32)
        vr = v_ref[...].astype(F32)
        lng_v = lng_ref[...]
        gu, xhat, rstd, vn, mixed = _sgu_core(ur, vr, lng_v, lnb_ref[...], ws_ref, bs_ref)
        dy = dy_ref[...].astype(F32)
        dgu = dy * mixed
        dmix = dy * gu
        dmb = dmix.astype(BF16)
        dws_p, dbs_p, rows = [], [], []
        for h in range(SGU_HEADS):
            acc_w = jnp.zeros((128, 128), F32)
            acc_b = jnp.zeros((128, 1), F32)
            for n in range(tm // MLP_CHUNK):
                r0 = n * MLP_CHUNK
                dmt = dmb[r0:r0 + MLP_CHUNK, h * 128:(h + 1) * 128]
                acc_w = acc_w + _dot(dmt, vn[r0:r0 + MLP_CHUNK, h * 128:(h + 1) * 128], NT)
                acc_b = acc_b + jnp.sum(dmix[r0:r0 + MLP_CHUNK, h * 128:(h + 1) * 128], axis=1, keepdims=True)
            dws_p.append(acc_w)
            dbs_p.append(jnp.broadcast_to(acc_b, (128, 128)))
        for n in range(tm // MLP_CHUNK):
            r0 = n * MLP_CHUNK
            rows.append(jnp.concatenate(
                [_dot(wst_ref[h], dmb[r0:r0 + MLP_CHUNK, h * 128:(h + 1) * 128]) for h in range(SGU_HEADS)], axis=1))
        dvn = jnp.concatenate(rows, axis=0) if len(rows) > 1 else rows[0]
        dlng_p = jnp.sum(dvn * xhat, axis=0, keepdims=True)
        dlnb_p = jnp.sum(dvn, axis=0, keepdims=True)
        dxh = dvn * lng_v
        dgv = rstd * (dxh - jnp.mean(dxh, axis=-1, keepdims=True)
                      - xhat * jnp.mean(dxh * xhat, axis=-1, keepdims=True))
        duv_ref[:, :W] = (dgu * _gelu_grad(ur)).astype(BF16)
        duv_ref[:, W:] = (dgv * _gelu_grad(vr)).astype(BF16)

        @pl.when(i == 0)
        def _():
            for h in range(SGU_HEADS):
                dws_ref[h] = dws_p[h]
                dbs_ref[h] = dbs_p[h]
            dlng_ref[...] = dlng_p
            dlnb_ref[...] = dlnb_p

        @pl.when(i > 0)
        def _():
            for h in range(SGU_HEADS):
                dws_ref[h] += dws_p[h]
                dbs_ref[h] += dbs_p[h]
            dlng_ref[...] += dlng_p
            dlnb_ref[...] += dlnb_p

    vec = Block((1, W), lambda i: (0, 0))
    wsb = Block((SGU_HEADS, 128, 128), lambda i: (0, 0, 0))
    hsq = SDS((SGU_HEADS, 128, 128), F32)
    return pl.pallas_call(
        body, name=name, grid=(T // tm,),
        in_specs=[Block((tm, W), lambda i: (i, 0)), Block((tm, W), lambda i: (i, 1)), Block((tm, W), lambda i: (i, 2)),
                  vec, vec, wsb, wsb, wsb],
        out_specs=[Block((tm, 2 * W), lambda i: (i, 0)), wsb, wsb, vec, vec],
        out_shape=[SDS((T, 2 * W), BF16), hsq, hsq, SDS((1, W), F32), SDS((1, W), F32)],
        compiler_params=_cparams(("arbitrary",), VMEM_BIG))(dyb, proj, proj, lng, lnb, wsm, wsmt, bst)


def _s5_disc(lr, li, ldt, brt, bit):
    dt = jnp.exp(ldt)
    decay = jnp.exp(lr * dt)
    abr = decay * jnp.cos(li * dt)
    abi = decay * jnp.sin(li * dt)
    denom = lr * lr + li * li
    nr = abr - 1.0
    ni = abi
    kr = (nr * lr + ni * li) / denom
    ki = (ni * lr - nr * li) / denom
    bkr = kr[None] * brt - ki[None] * bit
    bki = kr[None] * bit + ki[None] * brt
    return abr, abi, bkr, bki


def _s5_prep(lr, li, ldt, brt, bit):
    G, P, C = S5_GROUPS, S5_STATE, S5_GROUP_WIDTH

    def body(lr_ref, li_ref, ldt_ref, br_ref, bi_ref, abr_ref, abi_ref, pwr_ref, pwi_ref, bkr_ref, bki_ref):
        lr_, li_, ldt_ = lr_ref[...], li_ref[...], ldt_ref[...]
        res = _s5_disc(lr_, li_, ldt_, br_ref[...], bi_ref[...])
        for o, r in zip((abr_ref, abi_ref, bkr_ref, bki_ref), res):
            o[...] = r
        dt = jnp.exp(ldt_)
        n = lax.broadcasted_iota(jnp.int32, (S5_SEG, G, P), 0).astype(F32) + 1.0
        dec = jnp.exp((lr_ * dt)[None] * n)
        ang = (li_ * dt)[None] * n
        pwr_ref[...] = dec * jnp.cos(ang)
        pwi_ref[...] = dec * jnp.sin(ang)

    gp = SDS((G, P), F32)
    sgp = SDS((S5_SEG, G, P), F32)
    cgp = SDS((C, G, P), F32)
    return pl.pallas_call(body, name="s5_prep", out_shape=[gp, gp, sgp, sgp, cgp, cgp])(lr, li, ldt, brt, bit)


def _s5_prep_bwd(lr, li, ldt, brt, bit, dabr, dabi, dbkr, dbki):
    G, P, C = S5_GROUPS, S5_STATE, S5_GROUP_WIDTH

    def body(lr_ref, li_ref, ldt_ref, br_ref, bi_ref, dabr_ref, dabi_ref, dbkr_ref, dbki_ref,
             o_lr, o_li, o_ldt, o_br, o_bi):
        _, pull = jax.vjp(_s5_disc, lr_ref[...], li_ref[...], ldt_ref[...], br_ref[...], bi_ref[...])
        g = pull((dabr_ref[...], dabi_ref[...], dbkr_ref[...], dbki_ref[...]))
        for o, r in zip((o_lr, o_li, o_ldt, o_br, o_bi), g):
            o[...] = r

    gp = SDS((G, P), F32)
    cgp = SDS((C, G, P), F32)
    return pl.pallas_call(body, name="s5_prep_bwd", out_shape=[gp, gp, SDS((G, 1), F32), cgp, cgp])(
        lr, li, ldt, brt, bit, dabr, dabi, dbkr, dbki)


def _s5_scan(buf_ref, ar_row, ai_row, pwr_ref, pwi_ref, carry_ref, LG, xs_ref=None, dar_ref=None, dai_ref=None):
    reverse = xs_ref is not None
    NS, SEG = S5_NS, S5_SEG
    sgn = -1.0 if reverse else 1.0
    for lg in range(NS // LG):
        cr = slice(lg * LG, (lg + 1) * LG)
        ci = slice(NS + lg * LG, NS + (lg + 1) * LG)
        ar1, ai1 = ar_row[:, cr], sgn * ai_row[:, cr]
        asr1, asi1 = pwr_ref[SEG - 1:SEG, cr], sgn * pwi_ref[SEG - 1:SEG, cr]
        ar = jnp.broadcast_to(ar1, (8, LG))
        ai = jnp.broadcast_to(ai1, (8, LG))

        def step_of(j):
            return (SEG - 1 - j) if reverse else j

        def p1(j, st):
            sr, si = st
            rows = pl.ds(pl.multiple_of(step_of(j) * 8, 8), 8)
            nr = ar * sr - ai * si + buf_ref[rows, cr]
            ni = ar * si + ai * sr + buf_ref[rows, ci]
            buf_ref[rows, cr] = nr
            buf_ref[rows, ci] = ni
            return nr, ni

        z = jnp.zeros((8, LG), F32)
        er, ei = lax.fori_loop(0, SEG, p1, (z, z), unroll=2)
        c_r = carry_ref[:, cr]
        c_i = carry_ref[:, ci]
        cs_r, cs_i = [None] * 8, [None] * 8
        order = range(7, -1, -1) if reverse else range(8)
        for s in order:
            cs_r[s], cs_i[s] = c_r, c_i
            e_r, e_i = er[s:s + 1], ei[s:s + 1]
            c_r, c_i = e_r + asr1 * c_r - asi1 * c_i, e_i + asr1 * c_i + asi1 * c_r
        carry_ref[:, cr] = c_r
        carry_ref[:, ci] = c_i
        cmr = jnp.concatenate(cs_r, axis=0)
        cmi = jnp.concatenate(cs_i, axis=0)

        def carried(j):
            pr = pwr_ref[pl.ds(j, 1), cr]
            pi = sgn * pwi_ref[pl.ds(j, 1), cr]
            return pr * cmr - pi * cmi, pr * cmi + pi * cmr

        if not reverse:
            def p2(j, st):
                rows = pl.ds(pl.multiple_of(j * 8, 8), 8)
                wr, wi = carried(j)
                buf_ref[rows, cr] += wr
                buf_ref[rows, ci] += wi
                return st

            lax.fori_loop(0, SEG, p2, 0, unroll=4)
        else:
            def p2(j, st):
                pr, pi, dr, di = st
                rows = pl.ds(pl.multiple_of(step_of(j) * 8, 8), 8)
                xr = xs_ref[rows, cr]
                xi = xs_ref[rows, ci]
                dr = dr + pr * xr + pi * xi
                di = di + pi * xr - pr * xi
                wr, wi = carried(j)
                gr = buf_ref[rows, cr] + wr
                gi = buf_ref[rows, ci] + wi
                buf_ref[rows, cr] = gr
                buf_ref[rows, ci] = gi
                return gr, gi, dr, di

            st = lax.fori_loop(0, SEG, p2, (cmr, cmi, z, z), unroll=4)
            dar_ref[:, cr] += st[2]
            dai_ref[:, cr] += st[3]


def _s5_fwd(proj, perm, permt, bdbr, bdbi, bdcr, bdci, abr, abi, asr, asi, dvec, wglu, bglu):
    T = proj.shape[0]
    TC, NS, W = S5_TC, S5_NS, S5_WIDTH
    nc = T // TC

    def body(u_ref, pm_ref, pmt_ref, bdbr_ref, bdbi_ref, bdcr_ref, bdci_ref, ar_ref, ai_ref, asr_ref, asi_ref,
             d_ref, wglu_ref, bglu_ref, ya_ref, xs_ref, ypre_ref, carry_ref):
        i = pl.program_id(0)

        @pl.when(i == 0)
        def _():
            carry_ref[...] = jnp.zeros_like(carry_ref)

        up = _dot(pm_ref[...], u_ref[...]).astype(BF16)
        for j in range(8):
            ut = up[:, j * 128:(j + 1) * 128]
            xs_ref[:, j * 512:(j + 1) * 512] = _dot(ut, bdbr_ref[j])
            xs_ref[:, NS + j * 512:NS + (j + 1) * 512] = _dot(ut, bdbi_ref[j])
        _s5_scan(xs_ref, ar_ref[...], ai_ref[...], asr_ref, asi_ref, carry_ref, 512)
        ys = []
        for j in range(8):
            xr = xs_ref[:, j * 512:(j + 1) * 512].astype(BF16)
            xi = xs_ref[:, NS + j * 512:NS + (j + 1) * 512].astype(BF16)
            ys.append(_dot(xr, bdcr_ref[j]) + _dot(xi, bdci_ref[j]))
        ypre = jnp.concatenate(ys, axis=1) + d_ref[...] * up.astype(F32)
        ypre_ref[...] = ypre
        ya = _gelu(ypre)
        zl = _dot(ya.astype(BF16), wglu_ref[...]) + bglu_ref[...]
        outp = (ya * _sigmoid(zl)).astype(BF16)
        ya_ref[...] = _dot(pmt_ref[...], outp).astype(BF16)

    return pl.pallas_call(
        body, name="s5_fwd", grid=(nc,),
        in_specs=[Block((TC, W), lambda i: (i, 0)), _const((TC, TC)), _const((TC, TC)),
                  _const((8, 128, 512)), _const((8, 128, 512)), _const((8, 512, 128)), _const((8, 512, 128)),
                  _const((1, NS)), _const((1, NS)), _const((S5_SEG, NS)), _const((S5_SEG, NS)),
                  _const((1, W)), _const((W, W)), _const((1, W))],
        out_specs=[Block((TC, W), lambda i: (i, 0)), Block((TC, 2 * NS), lambda i: (i, 0)),
                   Block((TC, W), lambda i: (i, 0))],
        out_shape=[SDS((T, W), BF16), SDS((T, 2 * NS), F32), SDS((T, W), F32)],
        scratch_shapes=[pltpu.VMEM((1, 2 * NS), F32)],
        compiler_params=_cparams(("arbitrary",), VMEM_BIG),
    )(proj, perm, permt, bdbr, bdbi, bdcr, bdci, abr, abi, asr, asi, dvec, wglu, bglu)


def _s5_bwd(dya, proj, ypre, xs, perm, permt, bdbr, bdbi, bdcr, bdci, abr, abi, asr, asi, dvec, wglu, bglu):
    T = proj.shape[0]
    TC, NS, W = S5_TC, S5_NS, S5_WIDTH
    nc = T // TC

    def body(dya_ref, u_ref, ypre_ref, xs_ref, pm_ref, pmt_ref, bdbr_ref, bdbi_ref, bdcr_ref, bdci_ref,
             ar_ref, ai_ref, asr_ref, asi_ref, d_ref, wglu_ref, bglu_ref,
             du_ref, dar_ref, dai_ref, dd_ref, dbglu_ref, o_dbdbr, o_dbdbi, o_dbdcr, o_dbdci, o_dwglu,
             g_ref, carry_ref, dbdbr_ref, dbdbi_ref, dbdcr_ref, dbdci_ref, dwglu_ref):
        i = pl.program_id(0)

        @pl.when(i == 0)
        def _():
            carry_ref[...] = jnp.zeros_like(carry_ref)
            for r in (dbdbr_ref, dbdbi_ref, dbdcr_ref, dbdci_ref, dar_ref, dai_ref, dd_ref, dwglu_ref, dbglu_ref):
                r[...] = jnp.zeros_like(r)

        pm = pm_ref[...]
        dyo = _dot(pm, dya_ref[...])
        up = _dot(pm, u_ref[...]).astype(BF16)
        upf = up.astype(F32)
        ypre_v = ypre_ref[...]
        ya = _gelu(ypre_v)
        yab = ya.astype(BF16)
        sg = _sigmoid(_dot(yab, wglu_ref[...]) + bglu_ref[...])
        dz = dyo * ya * sg * (1.0 - sg)
        dzb = dz.astype(BF16)
        dya_t = dyo * sg + _dot(dzb, wglu_ref[...], NT)
        dwglu_ref[...] += _dot(yab, dzb, TN)
        dbglu_ref[...] += jnp.sum(dz, axis=0, keepdims=True)
        dy = dya_t * _gelu_grad(ypre_v)
        dd_ref[...] += jnp.sum(dy * upf, axis=0, keepdims=True)
        dyb = dy.astype(BF16)
        for j in range(8):
            dyj = dyb[:, j * 128:(j + 1) * 128]
            g_ref[:, j * 512:(j + 1) * 512] = _dot(dyj, bdcr_ref[j], NT)
            g_ref[:, NS + j * 512:NS + (j + 1) * 512] = _dot(dyj, bdci_ref[j], NT)
            dbdcr_ref[j] += _dot(xs_ref[:, j * 512:(j + 1) * 512].astype(BF16), dyj, TN)
            dbdci_ref[j] += _dot(xs_ref[:, NS + j * 512:NS + (j + 1) * 512].astype(BF16), dyj, TN)
        _s5_scan(g_ref, ar_ref[...], ai_ref[...], asr_ref, asi_ref, carry_ref, 512,
                 xs_ref=xs_ref, dar_ref=dar_ref, dai_ref=dai_ref)
        dus = []
        for j in range(8):
            ut = up[:, j * 128:(j + 1) * 128]
            gr = g_ref[:, j * 512:(j + 1) * 512].astype(BF16)
            gi = g_ref[:, NS + j * 512:NS + (j + 1) * 512].astype(BF16)
            dbdbr_ref[j] += _dot(ut, gr, TN)
            dbdbi_ref[j] += _dot(ut, gi, TN)
            dus.append(_dot(gr, bdbr_ref[j], NT) + _dot(gi, bdbi_ref[j], NT))
        dup = jnp.concatenate(dus, axis=1) + d_ref[...] * dy
        du_ref[...] = _dot(pmt_ref[...], dup.astype(BF16)).astype(BF16)

        @pl.when(i == nc - 1)
        def _():
            for src, dst in ((dbdbr_ref, o_dbdbr), (dbdbi_ref, o_dbdbi), (dbdcr_ref, o_dbdcr),
                             (dbdci_ref, o_dbdci), (dwglu_ref, o_dwglu)):
                pltpu.sync_copy(src, dst)

    c2 = lambda i: (0, 0)
    rev = lambda i: (nc - 1 - i, 0)
    return pl.pallas_call(
        body, name="s5_bwd", grid=(nc,),
        in_specs=[Block((TC, W), rev), Block((TC, W), rev), Block((TC, W), rev), Block((TC, 2 * NS), rev),
                  _const((TC, TC)), _const((TC, TC)),
                  _const((8, 128, 512)), _const((8, 128, 512)), _const((8, 512, 128)), _const((8, 512, 128)),
                  _const((1, NS)), _const((1, NS)), _const((S5_SEG, NS)), _const((S5_SEG, NS)),
                  _const((1, W)), _const((W, W)), _const((1, W))],
        out_specs=[Block((TC, W), rev), Block((8, NS), c2), Block((8, NS), c2), Block((1, W), c2), Block((1, W), c2),
                   ANY, ANY, ANY, ANY, ANY],
        out_shape=[SDS((T, W), BF16), SDS((8, NS), F32), SDS((8, NS), F32), SDS((1, W), F32), SDS((1, W), F32),
                   SDS((8, 128, 512), F32), SDS((8, 128, 512), F32),
                   SDS((8, 512, 128), F32), SDS((8, 512, 128), F32), SDS((W, W), F32)],
        scratch_shapes=[pltpu.VMEM((TC, 2 * NS), F32), pltpu.VMEM((1, 2 * NS), F32),
                        pltpu.VMEM((8, 128, 512), F32), pltpu.VMEM((8, 128, 512), F32),
                        pltpu.VMEM((8, 512, 128), F32), pltpu.VMEM((8, 512, 128), F32), pltpu.VMEM((W, W), F32)],
        compiler_params=_cparams(("arbitrary",), VMEM_BIG),
    )(dya, proj, ypre, xs, perm, permt, bdbr, bdbi, bdcr, bdci, abr, abi, asr, asi, dvec, wglu, bglu)


def _bd_b(bk_t):
    C, P = S5_GROUP_WIDTH, S5_STATE
    t = jnp.transpose(bk_t, (1, 0, 2)).reshape(8, 8, C, P)
    eye = jnp.eye(8, dtype=t.dtype)
    return (t[:, :, :, None, :] * eye[None, :, None, :, None]).reshape(8, 8 * C, 8 * P)


def _bd_b_extract(m):
    C, P = S5_GROUP_WIDTH, S5_STATE
    t = m.reshape(8, 8, C, 8, P)
    d = jnp.stack([t[:, g, :, g, :] for g in range(8)], axis=1)
    return jnp.transpose(d.reshape(S5_GROUPS, C, P), (1, 0, 2))


def _bd_c(c):
    C, P = S5_GROUP_WIDTH, S5_STATE
    t = jnp.transpose(c, (0, 2, 1)).reshape(8, 8, P, C)
    eye = jnp.eye(8, dtype=t.dtype)
    return (t[:, :, :, None, :] * eye[None, :, None, :, None]).reshape(8, 8 * P, 8 * C)


def _bd_c_extract(m):
    C, P = S5_GROUP_WIDTH, S5_STATE
    t = m.reshape(8, 8, P, 8, C)
    d = jnp.stack([t[:, g, :, g, :] for g in range(8)], axis=1)
    return jnp.transpose(d.reshape(S5_GROUPS, P, C), (0, 2, 1))


def _perm_matrix():
    r = jnp.arange(S5_TC)
    src = (r % 8) * S5_SEG + r // 8
    return (src[:, None] == jnp.arange(S5_TC)[None, :]).astype(BF16)


def _coords():
    return lax.axis_index("x"), lax.axis_index("y"), lax.axis_index("c")


def _all_gather(name, arrs):
    n = len(arrs)

    def body(*refs):
        ins, outs = refs[:n], refs[n:2 * n]
        send_sems, recv_sems, local_sems = refs[2 * n:]
        x, y, c = _coords()
        me, sibling = (x, y, c), (x, y, 1 - c)
        chips = [(1 - x, y), (x, 1 - y), (1 - x, 1 - y)]

        def slot(p):
            return 4 * p[0] + 2 * p[1] + p[2]

        def copy(a, k, block, to, src=None):
            dst = outs[a].at[slot(block)]
            return pltpu.make_async_remote_copy(
                src_ref=dst if src is None else src, dst_ref=dst,
                send_sem=send_sems.at[a * 7 + k], recv_sem=recv_sems.at[a * 7 + k],
                device_id=to, device_id_type=MESH)

        mine = [pltpu.make_async_copy(ins[a], outs[a].at[slot(me)], local_sems.at[a]) for a in range(n)]
        for m in mine:
            m.start()
        first = []
        for a in range(n):
            first.append(copy(a, 0, me, sibling, src=ins[a]))
            first += [copy(a, 1 + j, me, (*chip, c), src=ins[a]) for j, chip in enumerate(chips)]
        for cp in first:
            cp.start()
        passed = []
        for j, chip in enumerate(chips):
            for a in range(n):
                copy(a, 1 + j, (*chip, c), me).wait_recv()
                fw = copy(a, 4 + j, (*chip, c), sibling)
                fw.start()
                passed.append(fw)
        for a in range(n):
            copy(a, 0, sibling, me).wait_recv()
            for j, chip in enumerate(chips):
                copy(a, 4 + j, (*chip, 1 - c), me).wait_recv()
        for cp in first + passed:
            cp.wait_send()
        for m in mine:
            m.wait()

    return pl.pallas_call(
        body, name=name,
        in_specs=[ANY] * n, out_specs=[ANY] * n,
        out_shape=[SDS((NDEV,) + a.shape, a.dtype) for a in arrs],
        scratch_shapes=[pltpu.SemaphoreType.DMA((7 * n,)), pltpu.SemaphoreType.DMA((7 * n,)),
                        pltpu.SemaphoreType.DMA((n,))],
    )(*arrs)


HBM = pl.BlockSpec(memory_space=pltpu.HBM)
SEM = pl.BlockSpec(memory_space=pltpu.SEMAPHORE)
EFFECT = pltpu.SideEffectType.DATAFLOW_SIDE_EFFECTING


def _peers7(x, y, c):
    return [(1 - x if fx else x, 1 - y if fy else y, 1 - c if fc else c)
            for fx in (0, 1) for fy in (0, 1) for fc in (0, 1) if fx or fy or fc]


def _slot(p):
    return 4 * p[0] + 2 * p[1] + p[2]


def _split_copies(src_refs, land_refs, send_sems, recv_sems, gather, mine):
    x, y, c = _coords()
    me = (x, y, c)
    out = []
    for a, (src, land) in enumerate(zip(src_refs, land_refs)):
        for k, p in enumerate(_peers7(x, y, c)):
            s = src if gather else src.at[_slot(p)]
            out.append(pltpu.make_async_remote_copy(
                src_ref=s, dst_ref=land.at[_slot(me) if mine else _slot(p)],
                send_sem=send_sems.at[a * 7 + k], recv_sem=recv_sems.at[a * 7 + k],
                device_id=p, device_id_type=MESH))
    return out


def _own_slab(shard):
    x, y, c = _coords()
    z = lax.empty((NDEV,) + shard.shape, shard.dtype)
    return lax.dynamic_update_slice(z, shard[None], (_slot((x, y, c)),) + (0,) * shard.ndim)


def _split_start(name, srcs, lands, gather):
    n = len(srcs)

    def body(*refs):
        src_refs, land_refs = refs[:n], refs[n:2 * n]
        send_sems, recv_sems = refs[2 * n], refs[2 * n + 1]
        token = refs[-1]
        for cp in _split_copies(src_refs, land_refs, send_sems, recv_sems, gather, True):
            cp.start()
        token[...] = jnp.zeros_like(token)

    thru = [pltpu.HBM(a.shape, a.dtype) for a in list(srcs) + list(lands)]
    res = pl.pallas_call(
        body, name=name,
        out_shape=(pltpu.SemaphoreType.DMA((7 * n,)), pltpu.SemaphoreType.DMA((7 * n,)), *thru, SDS((8, 128), F32)),
        in_specs=[HBM] * (2 * n),
        out_specs=(SEM, SEM, *([HBM] * (2 * n)), pl.BlockSpec(memory_space=pltpu.VMEM)),
        input_output_aliases={i: 2 + i for i in range(2 * n)},
        compiler_params=pltpu.CompilerParams(has_side_effects=EFFECT),
    )(*[pltpu.with_memory_space_constraint(a, pltpu.HBM) for a in list(srcs) + list(lands)])
    return res[0], res[1], list(res[2:2 + n]), list(res[2 + n:2 + 2 * n]), res[-1]


def _split_wait(name, started, after, gather):
    send_sems, recv_sems, srcs, lands, _ = started
    n = len(srcs)

    def body(*refs):
        src_refs, land_refs = refs[:n], refs[n:2 * n]
        s_sems, r_sems = refs[2 * n], refs[2 * n + 1]
        for cp in _split_copies(src_refs, land_refs, s_sems, r_sems, gather, False):
            cp.wait_send()
            cp.wait_recv()

    thru = [pltpu.HBM(a.shape, a.dtype) for a in list(srcs) + list(lands)]
    res = pl.pallas_call(
        body, name=name, out_shape=tuple(thru),
        in_specs=[HBM] * (2 * n) + [SEM, SEM, ANY], out_specs=tuple([HBM] * (2 * n)),
        input_output_aliases={i: i for i in range(2 * n)},
        compiler_params=pltpu.CompilerParams(has_side_effects=EFFECT),
    )(*srcs, *lands, send_sems, recv_sems, after)
    return list(res[n:])


def _adam_math(w, g, m, v):
    m = ADAM_B1 * m + (1.0 - ADAM_B1) * g
    v = ADAM_B2 * v + (1.0 - ADAM_B2) * (g * g)
    m_hat = m / (1.0 - ADAM_B1 ** ADAM_STEP)
    v_hat = v / (1.0 - ADAM_B2 ** ADAM_STEP)
    delta = -ADAM_LR * (m_hat / (jnp.sqrt(v_hat) + ADAM_EPS) + ADAM_WD * w)
    return delta, m, v


def _adam_sharded(name, recv, sub, w, m, v):
    R, Cc = w.shape
    tr = max(t for t in range(16, R + 1, 16) if R % t == 0 and t * Cc <= 256 * 1024)

    def body(*refs):
        parts = refs[:NDEV]
        w_ref, m_ref, v_ref, g_out, d_out, m_out, v_out = refs[NDEV:]
        g = parts[0][...].astype(F32)
        for p in parts[1:]:
            g = g + p[...].astype(F32)
        delta, mn, vn = _adam_math(w_ref[...], g, m_ref[...], v_ref[...])
        g_out[...] = g
        d_out[...] = delta
        m_out[...] = mn
        v_out[...] = vn

    if sub is None:
        pspecs = [Block((None, tr, Cc), functools.partial(lambda s, i: (s, i, 0), s)) for s in range(NDEV)]
    else:
        pspecs = [Block((None, None, tr, Cc), functools.partial(lambda s, i: (s, sub, i, 0), s)) for s in range(NDEV)]
    row = Block((tr, Cc), lambda i: (i, 0))
    o = SDS((R, Cc), F32)
    return pl.pallas_call(
        body, name=name, grid=(R // tr,),
        in_specs=pspecs + [row, row, row], out_specs=[row, row, row, row], out_shape=[o, o, o, o],
        compiler_params=_cparams(("arbitrary",), VMEM_BIG))(*([recv] * NDEV), w, m, v)


def _adam_small(parts, w, m, v):
    R = w.shape[0]

    def body(p_ref, w_ref, m_ref, v_ref, g_out, d_out, m_out, v_out):
        g = p_ref[0]
        for s in range(1, NDEV):
            g = g + p_ref[s]
        delta, mn, vn = _adam_math(w_ref[...], g, m_ref[...], v_ref[...])
        g_out[...] = g
        d_out[...] = delta
        m_out[...] = mn
        v_out[...] = vn

    o = SDS((R, 128), F32)
    return pl.pallas_call(body, name="adam_small", out_shape=[o, o, o, o],
                          compiler_params=_cparams(None, VMEM_BIG))(parts, w, m, v)


_SMALL = ["ffn1_norm", "mix_norm", "s5_a_re", "s5_a_im", "s5_log_dt", "s5_b_re", "s5_b_im", "s5_c_re", "s5_c_im",
          "s5_d", "s5_b_glu", "sgu_ln_g", "sgu_ln_b", "sgu_w_s", "sgu_b_s", "b_gate", "ffn2_norm", "final_norm"]
_SHARDED = ["ffn1_w_gate", "ffn1_w_up", "ffn1_w_down", "w_in", "s5_w_glu", "w_branch_a", "w_branch_b", "w_gate",
            "w_out", "ffn2_w_gate", "ffn2_w_up", "ffn2_w_down"]
_ORDER = ["ffn1_norm", "ffn1_w_gate", "ffn1_w_up", "ffn1_w_down", "mix_norm", "w_in", "s5_a_re", "s5_a_im",
          "s5_log_dt", "s5_b_re", "s5_b_im", "s5_c_re", "s5_c_im", "s5_d", "s5_w_glu", "s5_b_glu", "sgu_ln_g",
          "sgu_ln_b", "sgu_w_s", "sgu_b_s", "w_branch_a", "w_branch_b", "w_gate", "b_gate", "w_out", "ffn2_norm",
          "ffn2_w_gate", "ffn2_w_up", "ffn2_w_down", "final_norm"]


def _step(x, tgt, W, M, V):
    T = x.shape[1]
    x0 = x[0]
    tgt0 = tgt[0]
    bf = lambda a: a.astype(BF16)

    def gather_start(name, shards):
        return _split_start(name, shards, [_own_slab(s) for s in shards], True)

    gs1 = gather_start("gather1_start", [jnp.stack([bf(W["ffn1_w_gate"][0]), bf(W["ffn1_w_up"][0])])])
    started = gs1[4][:1, :1]

    lr_, li_ = W["s5_a_re"][0], W["s5_a_im"][0]
    ldt_ = W["s5_log_dt"][0][:, None]
    brt = jnp.transpose(W["s5_b_re"][0], (2, 0, 1))
    bit = jnp.transpose(W["s5_b_im"][0], (2, 0, 1))
    abr, abi, pwr, pwi, bkr_t, bki_t = _s5_prep(lr_, li_, ldt_, brt, bit)
    bdbr, bdbi = bf(_bd_b(bkr_t)), bf(_bd_b(bki_t))
    bdcr, bdci = bf(_bd_c(W["s5_c_re"][0])), bf(_bd_c(-W["s5_c_im"][0]))
    flat = lambda a: a.reshape(1, S5_NS)
    s5a = (_perm_matrix(), _perm_matrix().T, bdbr, bdbi, bdcr, bdci, flat(abr), flat(abi),
           pwr.reshape(S5_SEG, S5_NS), pwi.reshape(S5_SEG, S5_NS),
           W["s5_d"][0].reshape(1, S5_WIDTH))
    blk = jnp.arange(MLP_CHUNK) // CHUNK
    mask = blk[:, None] >= blk[None, :]
    wsm = jnp.where(mask[None], W["sgu_w_s"][0], 0.0)
    wsm_b, wsmt_b = bf(wsm), bf(jnp.transpose(wsm, (0, 2, 1)))
    bst = jnp.broadcast_to(W["sgu_b_s"][0][:, :, None], (SGU_HEADS, MLP_CHUNK, 128))
    bgate2 = W["b_gate"].reshape(2, 1, D_MODEL)

    h1 = _rms_fwd("rms1", x0, W["ffn1_norm"] + started)
    (wgu1,) = _split_wait("gather1_wait", gs1, h1, True)
    dep = (wgu1[0, 0, :1, :1] * 0).astype(BF16)

    def later(a):
        return bf(a) + dep[0]

    gs2 = gather_start("gather2_start", [later(W["ffn1_w_down"][0]), later(W["w_in"][0]), later(W["s5_w_glu"][0])])
    gs3 = gather_start("gather3_start", [later(W["w_gate"][0]), later(W["w_branch_a"][0]), later(W["w_branch_b"][0]),
                                         later(W["w_out"][0])])
    gs4 = gather_start("gather4_start", [jnp.stack([later(W["ffn2_w_gate"][0]), later(W["ffn2_w_up"][0])]),
                                         later(W["ffn2_w_down"][0])])
    ab1, f1 = _ffn_up("ffn1_up", h1, wgu1, gs2[4] + gs3[4] + gs4[4])
    wd1, win, wglu = _split_wait("gather2_wait", gs2, f1, True)
    wglu = wglu.reshape(S5_WIDTH, S5_WIDTH)
    s5c = s5a + (wglu, W["s5_b_glu"])
    x1 = _ffn_down("ffn1_down", f1, wd1, x0)
    h2 = _rms_fwd("rms2", x1, W["mix_norm"])
    proj = _col_fwd("w_in", h2, win)
    ya, xs, ypre = _s5_fwd(proj, *s5c)
    yb = _sgu_fwd("sgu_fwd", proj, W["sgu_ln_g"], W["sgu_ln_b"], wsm_b, bst)
    wgate, wba, wbb, wout = _split_wait("gather3_wait", gs3, yb, True)
    wout = wout.reshape(D_MODEL, D_MODEL)
    pa = _col_fwd("branch_a", ya, wba)
    pb = _col_fwd("branch_b", yb, wbb)
    gl = _gate_fwd("gate", h2, wgate, bgate2)
    merged = _merge_fwd("merge", pa, pb, gl)
    x2 = _plain_fwd_res("w_out", merged, wout, x1)
    h3 = _rms_fwd("rms3", x2, W["ffn2_norm"])
    wgu2, wd2 = _split_wait("gather4_wait", gs4, h3, True)
    ab2, f2 = _ffn_up("ffn2_up", h3, wgu2)
    x3 = _ffn_down("ffn2_down", f2, wd2, x2)
    loss_p, dx3, dx3b, dgf = _loss_head("loss_head", x3, W["final_norm"].reshape(1, D_MODEL), tgt0)

    def exchange_start(name, grads):
        x_, y_, c_ = _coords()
        me = _slot((x_, y_, c_))
        return _split_start(name, grads, [_own_slab(lax.dynamic_index_in_dim(g, me, 0, keepdims=False))
                                          for g in grads], False)

    dab2 = _ffn_down_bwd_act("ffn2_down_bwd_a", dx3b, wd2, ab2)
    g_wd2 = _ffn_down_bwd_w("ffn2_down_bwd_w", f2, dx3b)
    g_gu2 = _ffn_up_bwd_w("ffn2_up_bwd_w", h3, dab2)
    es1 = exchange_start("exchange1_start", [g_wd2, g_gu2])
    dh3 = _ffn_up_bwd_h("ffn2_up_bwd_h", dab2, wgu2, es1[4])
    dx2, dx2b, dg3 = _rms_bwd("rms3_bwd", dh3, x2, W["ffn2_norm"], dx3)

    dmerged = _plain_bwd_a("w_out_bwd_a", dx2b, wout)
    g_wout = _plain_bwd_w("w_out_bwd_w", merged, dx2b)
    dpa, dpb, dgl, dbgate = _merge_bwd("merge_bwd", dmerged, pa, pb, gl)
    dya = _col_bwd_a("branch_a_bwd_a", dpa, wba)
    g_wba = _col_bwd_w("branch_a_bwd_w", ya, dpa, 256)
    dyb = _col_bwd_a("branch_b_bwd_a", dpb, wbb)
    g_wbb = _col_bwd_w("branch_b_bwd_w", yb, dpb, 256)
    dh2g = _gate_bwd_a("gate_bwd_a", dgl, wgate)
    g_wgate = _gate_bwd_w("gate_bwd_w", h2, dgl, 512)
    duv, dws, dbst, dlng, dlnb = _sgu_bwd("sgu_bwd", dyb, proj, W["sgu_ln_g"], W["sgu_ln_b"], wsm_b, wsmt_b, bst)
    (dua, dar8, dai8, ddv, dbglu, dbdbr, dbdbi, dbdcr, dbdci, g_wglu) = _s5_bwd(dya, proj, ypre, xs, *s5c)
    dproj = jnp.concatenate([dua, duv], axis=1)
    g_win = _col_bwd_w("w_in_bwd_w", h2, dproj, 384)
    g_wout3 = g_wout.reshape(NDEV, D_MODEL // NDEV, D_MODEL)
    g_wglu3 = g_wglu.astype(BF16).reshape(NDEV, S5_WIDTH // NDEV, S5_WIDTH)
    es2 = exchange_start("exchange2_start", [g_wout3, g_wba, g_wbb, g_wgate, g_wglu3, g_win])
    dh2 = _col_bwd_a("w_in_bwd_a", dproj, win, add=dh2g)
    dx1, dx1b, dgm = _rms_bwd("rms2_bwd", dh2, x1, W["mix_norm"] + es2[4][:1, :1], dx2)

    dab1 = _ffn_down_bwd_act("ffn1_down_bwd_a", dx1b, wd1, ab1)
    g_wd1 = _ffn_down_bwd_w("ffn1_down_bwd_w", f1, dx1b)
    g_gu1 = _ffn_up_bwd_w("ffn1_up_bwd_w", h1, dab1)
    es3 = exchange_start("exchange3_start", [g_wd1, g_gu1])
    dh1 = _ffn_up_bwd_h("ffn1_up_bwd_h", dab1, wgu1, es3[4])
    dx0, _, dg1 = _rms_bwd("rms1_bwd", dh1, x0, W["ffn1_norm"], dx1)

    dabr = jnp.sum(dar8, axis=0).reshape(S5_GROUPS, S5_STATE)
    dabi = jnp.sum(dai8, axis=0).reshape(S5_GROUPS, S5_STATE)
    d_lr, d_li, d_ldt, d_brt, d_bit = _s5_prep_bwd(lr_, li_, ldt_, brt, bit, dabr, dabi,
                                                   _bd_b_extract(dbdbr), _bd_b_extract(dbdbi))
    small_g = {
        "ffn1_norm": dg1, "mix_norm": dgm, "ffn2_norm": dg3, "final_norm": dgf,
        "s5_a_re": d_lr, "s5_a_im": d_li, "s5_log_dt": d_ldt,
        "s5_b_re": jnp.transpose(d_brt, (1, 2, 0)), "s5_b_im": jnp.transpose(d_bit, (1, 2, 0)),
        "s5_c_re": _bd_c_extract(dbdcr), "s5_c_im": -_bd_c_extract(dbdci),
        "s5_d": ddv, "s5_b_glu": dbglu, "sgu_ln_g": dlng, "sgu_ln_b": dlnb,
        "sgu_w_s": jnp.where(mask[None], dws, 0.0), "sgu_b_s": dbst[:, :, 0], "b_gate": dbgate,
    }

    sizes = [W[n].size for n in _SMALL]
    total = sum(sizes) + 1
    rows = -(-total // 128)
    rows = -(-rows // 8) * 8
    pad = rows * 128 - total

    def pack(d, extra):
        return jnp.concatenate([d[n].reshape(-1).astype(F32) for n in _SMALL] + [extra, jnp.zeros((pad,), F32)]
                               ).reshape(rows, 128)

    G, Dl, Mn, Vn = {}, {}, {}, {}

    def adam(plan):
        last = None
        for n, recv, sub in plan:
            g, d, mn, vn = _adam_sharded("adam_" + n, recv, sub, W[n][0], M[n][0], V[n][0])
            G[n], Dl[n], Mn[n], Vn[n] = g[None], d[None], mn[None], vn[None]
            last = g
        return last

    r_wd2, r_gu2 = _split_wait("exchange1_wait", es1, dx0, False)
    done = adam([("ffn2_w_down", r_wd2, None), ("ffn2_w_gate", r_gu2, 0), ("ffn2_w_up", r_gu2, 1)])
    r_wout, r_wba, r_wbb, r_wgate, r_wglu, r_win = _split_wait("exchange2_wait", es2, done, False)
    done = adam([("w_out", r_wout, None), ("w_branch_a", r_wba, None), ("w_branch_b", r_wbb, None),
                 ("w_gate", r_wgate, None), ("s5_w_glu", r_wglu, None), ("w_in", r_win, None)])

    late = loss_p[0, :1] + 0.0 * done.reshape(-1)[:1]
    zero1 = jnp.zeros((1,), F32)
    parts = _all_gather("gather_small_grads", [pack(small_g, late)])[0]
    sg, sd, sm, sv = _adam_small(parts, pack(W, zero1), pack(M, zero1), pack(V, zero1))

    def unpack(flat2d, into):
        flat = flat2d.reshape(-1)
        off = 0
        for n, s in zip(_SMALL, sizes):
            into[n] = flat[off:off + s].reshape(W[n].shape)
            off += s
        return flat[off]

    loss = unpack(sg, G)
    unpack(sd, Dl)
    unpack(sm, Mn)
    unpack(sv, Vn)

    r_wd1, r_gu1 = _split_wait("exchange3_wait", es3, sg, False)
    adam([("ffn1_w_down", r_wd1, None), ("ffn1_w_gate", r_gu1, 0), ("ffn1_w_up", r_gu1, 1)])

    return loss, dx0[None], G, Dl, Mn, Vn


def kernel(x, ffn1_norm, ffn1_w_gate, ffn1_w_up, ffn1_w_down, mix_norm, w_in, s5_a_re, s5_a_im, s5_log_dt, s5_b_re, s5_b_im, s5_c_re, s5_c_im, s5_d, s5_w_glu, s5_b_glu, sgu_ln_g, sgu_ln_b, sgu_w_s, sgu_b_s, w_branch_a, w_branch_b, w_gate, b_gate, w_out, ffn2_norm, ffn2_w_gate, ffn2_w_up, ffn2_w_down, final_norm, loss_target, m_ffn1_norm, m_ffn1_w_gate, m_ffn1_w_up, m_ffn1_w_down, m_mix_norm, m_w_in, m_s5_a_re, m_s5_a_im, m_s5_log_dt, m_s5_b_re, m_s5_b_im, m_s5_c_re, m_s5_c_im, m_s5_d, m_s5_w_glu, m_s5_b_glu, m_sgu_ln_g, m_sgu_ln_b, m_sgu_w_s, m_sgu_b_s, m_w_branch_a, m_w_branch_b, m_w_gate, m_b_gate, m_w_out, m_ffn2_norm, m_ffn2_w_gate, m_ffn2_w_up, m_ffn2_w_down, m_final_norm, v_ffn1_norm, v_ffn1_w_gate, v_ffn1_w_up, v_ffn1_w_down, v_mix_norm, v_w_in, v_s5_a_re, v_s5_a_im, v_s5_log_dt, v_s5_b_re, v_s5_b_im, v_s5_c_re, v_s5_c_im, v_s5_d, v_s5_w_glu, v_s5_b_glu, v_sgu_ln_g, v_sgu_ln_b, v_sgu_w_s, v_sgu_b_s, v_w_branch_a, v_w_branch_b, v_w_gate, v_b_gate, v_w_out, v_ffn2_norm, v_ffn2_w_gate, v_ffn2_w_up, v_ffn2_w_down, v_final_norm):
    a = locals()
    W = {n: a[n] for n in _ORDER}
    M = {n: a["m_" + n] for n in _ORDER}
    V = {n: a["v_" + n] for n in _ORDER}
    loss, gx, G, Dl, Mn, Vn = _step(x, loss_target, W, M, V)
    return (loss, gx, *[G[n] for n in _ORDER], *[Dl[n] for n in _ORDER], *[Mn[n] for n in _ORDER],
            *[Vn[n] for n in _ORDER])
```

```python
import functools
import math

import jax
import jax.numpy as jnp
from jax import lax
from jax.experimental import pallas as pl
from jax.experimental.pallas import tpu as pltpu

F32 = jnp.float32
BF16 = jnp.bfloat16
NDEV = 8
NORM_EPS = 1e-6
D_MODEL = 2048
D_FF = 5632
FF_SHARD = D_FF // NDEV
S5_WIDTH = 1024
S5_GROUPS = 64
S5_GROUP_WIDTH = 16
S5_STATE = 64
S5_NS = S5_GROUPS * S5_STATE
SGU_WIDTH = 1024
SGU_HEADS = 8
MLP_CHUNK = 128
CHUNK = 64
ADAM_LR, ADAM_B1, ADAM_B2, ADAM_EPS, ADAM_WD, ADAM_STEP = 0.001, 0.9, 0.999, 1e-08, 0.01, 10
S5_TC = 256
S5_SEG = S5_TC // 8
S5_LG = 512
VMEM_BIG = 56 * 1024 * 1024

MESH = pl.DeviceIdType.MESH
SDS = jax.ShapeDtypeStruct
Block = pl.BlockSpec
ANY = pl.BlockSpec(memory_space=pl.ANY)


def _cparams(sem=None, vmem=None):
    return pltpu.CompilerParams(dimension_semantics=sem, vmem_limit_bytes=vmem)


def _const(shape):
    nd = len(shape)
    return pl.BlockSpec(shape, lambda i: (0,) * nd, pipeline_mode=pl.Buffered(1))


def _sigmoid(x):
    return 0.5 * jnp.tanh(0.5 * x) + 0.5


_GELU_C = math.sqrt(2.0 / math.pi)


def _gelu(x):
    return 0.5 * x * (1.0 + jnp.tanh(_GELU_C * (x + 0.044715 * x * x * x)))


def _gelu_grad(x):
    t = jnp.tanh(_GELU_C * (x + 0.044715 * x * x * x))
    return 0.5 * (1.0 + t) + 0.5 * x * (1.0 - t * t) * _GELU_C * (1.0 + 3.0 * 0.044715 * x * x)


NN = (((1,), (0,)), ((), ()))
NT = (((1,), (1,)), ((), ()))
TN = (((0,), (0,)), ((), ()))


def _dot(a, b, dims=NN):
    return lax.dot_general(a, b, dims, preferred_element_type=F32)


def _matmul(name, a, b, extras, *, grid, a_spec, b_spec, extra_specs, out_shapes, out_specs, acc_shape,
            epilogue, dims=NN, nb=None, compute=None, after=None, vmem=VMEM_BIG):
    nk = grid[2]
    if after is not None:
        extras = tuple(extras) + (after,)
        extra_specs = list(extra_specs) + [Block((8, 128), lambda i, j, k: (0, 0))]
    ne, no = len(extras), len(out_shapes)
    nacc = nb or 1
    if compute is None:
        def compute(a_ref, b_ref, q):
            return _dot(a_ref[...], b_ref[q] if nb else b_ref[...], dims)

    def body(*refs):
        a_ref, b_ref = refs[0], refs[1]
        ex = refs[2:2 + ne]
        outs = refs[2 + ne:2 + ne + no]
        if nk == 1:
            epilogue([compute(a_ref, b_ref, q) for q in range(nacc)], ex, outs)
            return
        acc_ref = refs[2 + ne + no]
        k = pl.program_id(2)

        @pl.when(k == 0)
        def _():
            acc_ref[...] = jnp.zeros_like(acc_ref)

        for q in range(nacc):
            acc_ref[q] += compute(a_ref, b_ref, q)

        @pl.when(k == nk - 1)
        def _():
            epilogue([acc_ref[q] for q in range(nacc)], ex, outs)

    scratch = [] if nk == 1 else [pltpu.VMEM((nacc,) + tuple(acc_shape), F32)]
    res = pl.pallas_call(
        body, name=name, grid=grid,
        in_specs=[a_spec, b_spec] + list(extra_specs),
        out_specs=list(out_specs), out_shape=list(out_shapes), scratch_shapes=scratch,
        compiler_params=_cparams(("parallel", "parallel", "arbitrary"), vmem),
    )(a, b, *extras)
    return res


def _store(dtype_outs=None):
    def ep(accs, ex, outs):
        outs[0][...] = accs[0].astype(outs[0].dtype)
    return ep


def _tile(n, t):
    t = min(n, t)
    assert n % t == 0, (n, t)
    return t


def _ksum(kq, dims):
    def compute(a_ref, b_ref, _):
        part = _dot(a_ref[0], b_ref[0], dims)
        for q in range(1, kq):
            part = part + _dot(a_ref[q], b_ref[q], dims)
        return part
    return compute


def _ksum_lanes(kq, ns, dims):
    def compute(a_ref, b_ref, _):
        part = _dot(a_ref[:, 0:ns], b_ref[0], dims)
        for q in range(1, kq):
            part = part + _dot(a_ref[:, q * ns:(q + 1) * ns], b_ref[q], dims)
        return part
    return compute


def _wide_b(g):
    def compute(a_ref, b_ref, _):
        bw = b_ref[0] if g == 1 else jnp.concatenate([b_ref[q] for q in range(g)], axis=1)
        return _dot(a_ref[...], bw, NN)
    return compute


KQ_DOWN = 4
TT_DEEP = 2048


def _ffn_up(name, h, wgu, after=None):
    T, D = h.shape
    tm = _tile(T, 1024)

    def ep(accs, ex, outs):
        a, b = accs
        outs[0][0] = a.astype(BF16)
        outs[0][1] = b.astype(BF16)
        outs[1][...] = (a * _sigmoid(a) * b).astype(BF16)

    return _matmul(
        name, h, wgu, (), after=after, grid=(NDEV, T // tm, 1),
        a_spec=Block((tm, D), lambda j, i, k: (i, 0)),
        b_spec=Block((None, 2, D, FF_SHARD), lambda j, i, k: (j, 0, 0, 0)),
        extra_specs=(),
        out_shapes=[SDS((NDEV, 2, T, FF_SHARD), BF16), SDS((NDEV, T, FF_SHARD), BF16)],
        out_specs=[Block((None, 2, tm, FF_SHARD), lambda j, i, k: (j, 0, i, 0)),
                   Block((None, tm, FF_SHARD), lambda j, i, k: (j, i, 0))],
        acc_shape=(tm, FF_SHARD), dims=NN, nb=2, epilogue=ep)


def _ffn_down(name, f, wd, xres, after=None):
    _, T, _ = f.shape
    tm, tn = _tile(T, 1024), 1024

    def ep(accs, ex, outs):
        outs[0][...] = ex[0][...] + 0.5 * accs[0]

    return _matmul(
        name, f, wd, (xres,), after=after, grid=(T // tm, D_MODEL // tn, NDEV // KQ_DOWN),
        a_spec=Block((KQ_DOWN, tm, FF_SHARD), lambda i, j, k: (k, i, 0)),
        b_spec=Block((KQ_DOWN, FF_SHARD, tn), lambda i, j, k: (k, 0, j)),
        extra_specs=[Block((tm, tn), lambda i, j, k: (i, j))],
        out_shapes=[SDS((T, D_MODEL), F32)],
        out_specs=[Block((tm, tn), lambda i, j, k: (i, j))],
        acc_shape=(tm, tn), compute=_ksum(KQ_DOWN, NN), epilogue=ep)[0]


def _ffn_down_bwd_act(name, dyb, wd, ab):
    T, D = dyb.shape
    tm = _tile(T, 1024)
    def ep(accs, ex, outs):
        df = accs[0]
        a = ex[0][0].astype(F32)
        b = ex[0][1].astype(F32)
        hs = 0.5 * _sigmoid(a)
        outs[0][0] = (df * b * hs * (1.0 + a * (1.0 - 2.0 * hs))).astype(BF16)
        outs[0][1] = (df * a * hs).astype(BF16)

    return _matmul(
        name, dyb, wd, (ab,), grid=(NDEV, T // tm, 1),
        a_spec=Block((tm, D), lambda j, i, k: (i, 0)),
        b_spec=Block((None, FF_SHARD, D), lambda j, i, k: (j, 0, 0)),
        extra_specs=[Block((None, 2, tm, FF_SHARD), lambda j, i, k: (j, 0, i, 0))],
        out_shapes=[SDS((NDEV, 2, T, FF_SHARD), BF16)],
        out_specs=[Block((None, 2, tm, FF_SHARD), lambda j, i, k: (j, 0, i, 0))],
        acc_shape=(tm, FF_SHARD), dims=NT, epilogue=ep)[0]


def _ffn_down_bwd_w(name, f, dyb, after=None):
    _, T, _ = f.shape
    tt, tn = _tile(T, TT_DEEP), 1024

    def ep(accs, ex, outs):
        outs[0][...] = (0.5 * accs[0]).astype(BF16)

    return _matmul(
        name, f, dyb, (), after=after, grid=(NDEV, D_MODEL // tn, T // tt),
        a_spec=Block((None, tt, FF_SHARD), lambda j, n, k: (j, k, 0)),
        b_spec=Block((tt, tn), lambda j, n, k: (k, n)),
        extra_specs=(),
        out_shapes=[SDS((NDEV, FF_SHARD, D_MODEL), BF16)],
        out_specs=[Block((None, FF_SHARD, tn), lambda j, n, k: (j, 0, n))],
        acc_shape=(FF_SHARD, tn), dims=TN, epilogue=ep)[0]


def _ffn_up_bwd_h(name, dab, wgu, after):
    _, _, T, _ = dab.shape
    tm = _tile(T, 1024)
    return _matmul(
        name, dab, wgu, (), after=after, grid=(T // tm, 1, NDEV),
        a_spec=Block((None, 2, tm, FF_SHARD), lambda i, j, k: (k, 0, i, 0)),
        b_spec=Block((None, 2, D_MODEL, FF_SHARD), lambda i, j, k: (k, 0, 0, 0)),
        extra_specs=(),
        out_shapes=[SDS((T, D_MODEL), BF16)],
        out_specs=[Block((tm, D_MODEL), lambda i, j, k: (i, 0))],
        acc_shape=(tm, D_MODEL), compute=_ksum(2, NT), epilogue=_store())[0]


def _ffn_up_bwd_w(name, h, dab):
    T, D = h.shape
    tt, tr = _tile(T, TT_DEEP), 1024

    def ep(accs, ex, outs):
        outs[0][0] = accs[0].astype(BF16)
        outs[0][1] = accs[1].astype(BF16)

    return _matmul(
        name, h, dab, (), grid=(NDEV, D // tr, T // tt),
        a_spec=Block((tt, tr), lambda j, n, k: (k, n)),
        b_spec=Block((None, 2, tt, FF_SHARD), lambda j, n, k: (j, 0, k, 0)),
        extra_specs=(),
        out_shapes=[SDS((NDEV, 2, D, FF_SHARD), BF16)],
        out_specs=[Block((None, 2, tr, FF_SHARD), lambda j, n, k: (j, 0, n, 0))],
        acc_shape=(tr, FF_SHARD), dims=TN, nb=2, epilogue=ep)[0]


def _shards_per_step(ns):
    return max(g for g in (1, 2, 4, 8) if g * ns <= 2048)


def _split_lanes(g, ns):
    def ep(accs, ex, outs):
        for q in range(g):
            outs[0][q] = accs[0][:, q * ns:(q + 1) * ns].astype(outs[0].dtype)
    return ep


def _col_fwd(name, a, w, out_dtype=BF16, after=None):
    T, K = a.shape
    ns = w.shape[2]
    g = _shards_per_step(ns)
    tm = _tile(T, 1024)
    return _matmul(
        name, a, w, (), after=after, grid=(NDEV // g, T // tm, 1),
        a_spec=Block((tm, K), lambda j, i, k: (i, 0)),
        b_spec=Block((g, K, ns), lambda j, i, k: (j, 0, 0)),
        extra_specs=(),
        out_shapes=[SDS((T, NDEV * ns), out_dtype)],
        out_specs=[Block((tm, g * ns), lambda j, i, k: (i, j))],
        acc_shape=(tm, g * ns), compute=_wide_b(g), epilogue=_store())[0]


def _col_bwd_a(name, dy, w, add=None):
    T = dy.shape[0]
    _, K, ns = w.shape
    tm, tn = _tile(T, 1024), _tile(K, 1024)

    def ep(accs, ex, outs):
        r = accs[0]
        if add is not None:
            r = r + ex[0][...].astype(F32)
        outs[0][...] = r.astype(BF16)

    extras = () if add is None else (add,)
    return _matmul(
        name, dy, w, extras, grid=(T // tm, K // tn, 1),
        a_spec=Block((tm, NDEV * ns), lambda i, j, k: (i, 0)),
        b_spec=Block((NDEV, tn, ns), lambda i, j, k: (0, j, 0)),
        extra_specs=[Block((tm, tn), lambda i, j, k: (i, j))] * len(extras),
        out_shapes=[SDS((T, K), BF16)],
        out_specs=[Block((tm, tn), lambda i, j, k: (i, j))],
        acc_shape=(tm, tn), compute=_ksum_lanes(NDEV, ns, NT), epilogue=ep)[0]


def _col_bwd_w(name, a, dy, ns):
    T, K = a.shape
    g = _shards_per_step(ns)
    tt, tr = _tile(T, TT_DEEP), _tile(K, 1024)
    return _matmul(
        name, a, dy, (), grid=(NDEV // g, K // tr, T // tt),
        a_spec=Block((tt, tr), lambda j, n, k: (k, n)),
        b_spec=Block((tt, g * ns), lambda j, n, k: (k, j)),
        extra_specs=(),
        out_shapes=[SDS((NDEV, K, ns), BF16)],
        out_specs=[Block((g, tr, ns), lambda j, n, k: (j, n, 0))],
        acc_shape=(tr, g * ns), dims=TN, epilogue=_split_lanes(g, ns))[0]


def _gate_fwd(name, h, w, bias):
    T, K = h.shape
    ns = w.shape[2]
    g = 2
    per = D_MODEL // (g * ns)
    tm = _tile(T, 1024)

    def ep(accs, ex, outs):
        outs[0][...] = (accs[0] + ex[0][...]).astype(BF16)

    return _matmul(
        name, h, w, (bias,), grid=(NDEV // g, T // tm, 1),
        a_spec=Block((tm, K), lambda j, i, k: (i, 0)),
        b_spec=Block((g, K, ns), lambda j, i, k: (j, 0, 0)),
        extra_specs=[Block((None, 1, g * ns), lambda j, i, k: (j // per, 0, j % per))],
        out_shapes=[SDS((2, T, D_MODEL), BF16)],
        out_specs=[Block((None, tm, g * ns), lambda j, i, k: (j // per, i, j % per))],
        acc_shape=(tm, g * ns), compute=_wide_b(g), epilogue=ep)[0]


def _gate_bwd_a(name, dgl, w):
    _, T, _ = dgl.shape
    _, K, ns = w.shape
    per = D_MODEL // ns
    tm, tn = _tile(T, 1024), 1024

    def compute(a_ref, b_ref, _):
        part = None
        for q in range(NDEV):
            d = _dot(a_ref[q // per, :, (q % per) * ns:(q % per + 1) * ns], b_ref[q], NT)
            part = d if part is None else part + d
        return part

    return _matmul(
        name, dgl, w, (), grid=(T // tm, K // tn, 1),
        a_spec=Block((2, tm, D_MODEL), lambda i, j, k: (0, i, 0)),
        b_spec=Block((NDEV, tn, ns), lambda i, j, k: (0, j, 0)),
        extra_specs=(),
        out_shapes=[SDS((T, K), BF16)],
        out_specs=[Block((tm, tn), lambda i, j, k: (i, j))],
        acc_shape=(tm, tn), compute=compute, epilogue=_store())[0]


def _gate_bwd_w(name, h, dgl, ns):
    T, K = h.shape
    g = 2
    per = D_MODEL // (g * ns)
    tt, tr = _tile(T, TT_DEEP), 1024
    return _matmul(
        name, h, dgl, (), grid=(NDEV // g, K // tr, T // tt),
        a_spec=Block((tt, tr), lambda j, n, k: (k, n)),
        b_spec=Block((None, tt, g * ns), lambda j, n, k: (j // per, k, j % per)),
        extra_specs=(),
        out_shapes=[SDS((NDEV, K, ns), BF16)],
        out_specs=[Block((g, tr, ns), lambda j, n, k: (j, n, 0))],
        acc_shape=(tr, g * ns), dims=TN, epilogue=_split_lanes(g, ns))[0]


def _plain_fwd_res(name, a, w, xres):
    T, K = a.shape
    N = w.shape[1]
    tm, tn = _tile(T, 1024), _tile(N, 1024)

    def ep(accs, ex, outs):
        outs[0][...] = ex[0][...] + accs[0]

    return _matmul(
        name, a, w, (xres,), grid=(T // tm, N // tn, 1),
        a_spec=Block((tm, K), lambda i, j, k: (i, 0)),
        b_spec=Block((K, tn), lambda i, j, k: (0, j)),
        extra_specs=[Block((tm, tn), lambda i, j, k: (i, j))],
        out_shapes=[SDS((T, N), F32)],
        out_specs=[Block((tm, tn), lambda i, j, k: (i, j))],
        acc_shape=(tm, tn), dims=NN, nb=None, epilogue=ep)[0]


def _plain_bwd_a(name, dy, w):
    T, N = dy.shape
    K = w.shape[0]
    tm, tn = _tile(T, 1024), _tile(K, 1024)
    return _matmul(
        name, dy, w, (), grid=(T // tm, K // tn, 1),
        a_spec=Block((tm, N), lambda i, j, k: (i, 0)),
        b_spec=Block((tn, N), lambda i, j, k: (j, 0)),
        extra_specs=(),
        out_shapes=[SDS((T, K), BF16)],
        out_specs=[Block((tm, tn), lambda i, j, k: (i, j))],
        acc_shape=(tm, tn), dims=NT, nb=None, epilogue=_store())[0]


def _plain_bwd_w(name, a, dy):
    T, K = a.shape
    N = dy.shape[1]
    tt, tr, tn = _tile(T, TT_DEEP), _tile(K, 1024), _tile(N, 1024)
    return _matmul(
        name, a, dy, (), grid=(K // tr, N // tn, T // tt),
        a_spec=Block((tt, tr), lambda m, n, k: (k, m)),
        b_spec=Block((tt, tn), lambda m, n, k: (k, n)),
        extra_specs=(),
        out_shapes=[SDS((K, N), BF16)],
        out_specs=[Block((tr, tn), lambda m, n, k: (m, n))],
        acc_shape=(tr, tn), dims=TN, nb=None, epilogue=_store())[0]


def _rms_fwd(name, x, g):
    T, D = x.shape
    tm = _tile(T, 512)

    def body(x_ref, g_ref, h_ref):
        xv = x_ref[...]
        r = lax.rsqrt(jnp.mean(xv * xv, axis=-1, keepdims=True) + NORM_EPS)
        h_ref[...] = (xv * r * g_ref[...]).astype(BF16)

    return pl.pallas_call(
        body, name=name, grid=(T // tm,),
        in_specs=[Block((tm, D), lambda i: (i, 0)), Block((1, D), lambda i: (0, 0))],
        out_specs=Block((tm, D), lambda i: (i, 0)), out_shape=SDS((T, D), BF16),
        compiler_params=_cparams(("arbitrary",), VMEM_BIG))(x, g)


def _rms_bwd(name, dh, x, g, dxin):
    T, D = x.shape
    tm = _tile(T, 512)

    def body(dh_ref, x_ref, g_ref, dxin_ref, dx_ref, dxb_ref, dg_ref):
        i = pl.program_id(0)
        xv = x_ref[...]
        dh = dh_ref[...].astype(F32)
        r = lax.rsqrt(jnp.mean(xv * xv, axis=-1, keepdims=True) + NORM_EPS)
        xh = xv * r
        gd = dh * g_ref[...]
        dx = dxin_ref[...] + r * (gd - xh * jnp.mean(gd * xh, axis=-1, keepdims=True))
        dx_ref[...] = dx
        dxb_ref[...] = dx.astype(BF16)
        dgp = jnp.sum(dh * xh, axis=0, keepdims=True)

        @pl.when(i == 0)
        def _():
            dg_ref[...] = dgp

        @pl.when(i > 0)
        def _():
            dg_ref[...] += dgp

    row = Block((tm, D), lambda i: (i, 0))
    vec = Block((1, D), lambda i: (0, 0))
    return pl.pallas_call(
        body, name=name, grid=(T // tm,),
        in_specs=[row, row, vec, row], out_specs=[row, row, vec],
        out_shape=[SDS((T, D), F32), SDS((T, D), BF16), SDS((1, D), F32)],
        compiler_params=_cparams(("arbitrary",), VMEM_BIG))(dh, x, g, dxin)


def _loss_head(name, x, g, tgt):
    T, D = x.shape
    tm = _tile(T, 512)

    def body(x_ref, g_ref, t_ref, loss_ref, dx_ref, dxb_ref, dg_ref):
        i = pl.program_id(0)
        xv = x_ref[...]
        gv = g_ref[...]
        r = lax.rsqrt(jnp.mean(xv * xv, axis=-1, keepdims=True) + NORM_EPS)
        xh = xv * r
        err = xh * gv - t_ref[...]
        lp = 0.5 * jnp.sum(jnp.mean(err * err, axis=-1, keepdims=True), axis=0, keepdims=True)
        dout = err * (1.0 / D)
        gd = dout * gv
        dx = r * (gd - xh * jnp.mean(gd * xh, axis=-1, keepdims=True))
        dx_ref[...] = dx
        dxb_ref[...] = dx.astype(BF16)
        dgp = jnp.sum(dout * xh, axis=0, keepdims=True)
        lpb = jnp.broadcast_to(lp, (1, 128))

        @pl.when(i == 0)
        def _():
            dg_ref[...] = dgp
            loss_ref[...] = lpb

        @pl.when(i > 0)
        def _():
            dg_ref[...] += dgp
            loss_ref[...] += lpb

    row = Block((tm, D), lambda i: (i, 0))
    vec = Block((1, D), lambda i: (0, 0))
    return pl.pallas_call(
        body, name=name, grid=(T // tm,),
        in_specs=[row, vec, row], out_specs=[Block((1, 128), lambda i: (0, 0)), row, row, vec],
        out_shape=[SDS((1, 128), F32), SDS((T, D), F32), SDS((T, D), BF16), SDS((1, D), F32)],
        compiler_params=_cparams(("arbitrary",), VMEM_BIG))(x, g, tgt)


def _merge_fwd(name, pa, pb, gl):
    T, D = pa.shape
    tm = _tile(T, 512)

    def body(pa_ref, pb_ref, gl_ref, o_ref):
        ga = _sigmoid(gl_ref[0].astype(F32))
        gb = _sigmoid(gl_ref[1].astype(F32))
        o_ref[...] = (ga * pa_ref[...].astype(F32) + gb * pb_ref[...].astype(F32)).astype(BF16)

    row = Block((tm, D), lambda i: (i, 0))
    return pl.pallas_call(
        body, name=name, grid=(T // tm,),
        in_specs=[row, row, Block((2, tm, D), lambda i: (0, i, 0))], out_specs=row,
        out_shape=SDS((T, D), BF16), compiler_params=_cparams(("arbitrary",), VMEM_BIG))(pa, pb, gl)


def _merge_bwd(name, dm, pa, pb, gl):
    T, D = pa.shape
    tm = _tile(T, 512)

    def body(dm_ref, pa_ref, pb_ref, gl_ref, dpa_ref, dpb_ref, dgl_ref, db_ref):
        i = pl.program_id(0)
        dmv = dm_ref[...].astype(F32)
        ga = _sigmoid(gl_ref[0].astype(F32))
        gb = _sigmoid(gl_ref[1].astype(F32))
        dpa_ref[...] = (dmv * ga).astype(BF16)
        dpb_ref[...] = (dmv * gb).astype(BF16)
        dga = dmv * pa_ref[...].astype(F32) * ga * (1.0 - ga)
        dgb = dmv * pb_ref[...].astype(F32) * gb * (1.0 - gb)
        dgl_ref[0] = dga.astype(BF16)
        dgl_ref[1] = dgb.astype(BF16)
        sa = jnp.sum(dga, axis=0, keepdims=True)
        sb = jnp.sum(dgb, axis=0, keepdims=True)

        @pl.when(i == 0)
        def _():
            db_ref[0] = sa
            db_ref[1] = sb

        @pl.when(i > 0)
        def _():
            db_ref[0] += sa
            db_ref[1] += sb

    row = Block((tm, D), lambda i: (i, 0))
    two = Block((2, tm, D), lambda i: (0, i, 0))
    return pl.pallas_call(
        body, name=name, grid=(T // tm,),
        in_specs=[row, row, row, two], out_specs=[row, row, two, Block((2, 1, D), lambda i: (0, 0, 0))],
        out_shape=[SDS((T, D), BF16), SDS((T, D), BF16), SDS((2, T, D), BF16), SDS((2, 1, D), F32)],
        compiler_params=_cparams(("arbitrary",), VMEM_BIG))(dm, pa, pb, gl)


def _sgu_core(ur, vr, lng, lnb, ws_ref, bs_ref):
    tm = ur.shape[0]
    gu = _gelu(ur)
    gv = _gelu(vr)
    mu = jnp.mean(gv, axis=-1, keepdims=True)
    cen = gv - mu
    rstd = lax.rsqrt(jnp.mean(cen * cen, axis=-1, keepdims=True) + NORM_EPS)
    xhat = cen * rstd
    vn = (xhat * lng + lnb).astype(BF16)
    rows = []
    for n in range(tm // MLP_CHUNK):
        cols = []
        for h in range(SGU_HEADS):
            blk = vn[n * MLP_CHUNK:(n + 1) * MLP_CHUNK, h * 128:(h + 1) * 128]
            cols.append(_dot(ws_ref[h], blk) + bs_ref[h])
        rows.append(jnp.concatenate(cols, axis=1))
    mixed = jnp.concatenate(rows, axis=0) if len(rows) > 1 else rows[0]
    return gu, xhat, rstd, vn, mixed


def _sgu_fwd(name, proj, lng, lnb, wsm, bst):
    T = proj.shape[0]
    W = SGU_WIDTH
    tm = _tile(T, 512)

    def body(u_ref, v_ref, lng_ref, lnb_ref, ws_ref, bs_ref, o_ref):
        gu, _, _, _, mixed = _sgu_core(u_ref[...].astype(F32), v_ref[...].astype(F32), lng_ref[...], lnb_ref[...],
                                       ws_ref, bs_ref)
        o_ref[...] = (gu * mixed).astype(BF16)

    vec = Block((1, W), lambda i: (0, 0))
    return pl.pallas_call(
        body, name=name, grid=(T // tm,),
        in_specs=[Block((tm, W), lambda i: (i, 1)), Block((tm, W), lambda i: (i, 2)), vec, vec,
                  Block((SGU_HEADS, 128, 128), lambda i: (0, 0, 0)), Block((SGU_HEADS, 128, 128), lambda i: (0, 0, 0))],
        out_specs=Block((tm, W), lambda i: (i, 0)), out_shape=SDS((T, W), BF16),
        compiler_params=_cparams(("arbitrary",), VMEM_BIG))(proj, proj, lng, lnb, wsm, bst)


def _sgu_bwd(name, dyb, proj, lng, lnb, wsm, wsmt, bst):
    T = proj.shape[0]
    W = SGU_WIDTH
    tm = _tile(T, 512)

    def body(dy_ref, u_ref, v_ref, lng_ref, lnb_ref, ws_ref, wst_ref, bs_ref,
             duv_ref, dws_ref, dbs_ref, dlng_ref, dlnb_ref):
        i = pl.program_id(0)
        ur = u_ref[...].astype(F32)
        vr = v_ref[...].astype(F32)
        lng_v = lng_ref[...]
        gu, xhat, rstd, vn, mixed = _sgu_core(ur, vr, lng_v, lnb_ref[...], ws_ref, bs_ref)
        dy = dy_ref[...].astype(F32)
        dgu = dy * mixed
        dmix = dy * gu
        dmb = dmix.astype(BF16)
        dws_p, dbs_p, rows = [], [], []
        for h in range(SGU_HEADS):
            acc_w = jnp.zeros((128, 128), F32)
            acc_b = jnp.zeros((128, 1), F32)
            for n in range(tm // MLP_CHUNK):
                r0 = n * MLP_CHUNK
                dmt = dmb[r0:r0 + MLP_CHUNK, h * 128:(h + 1) * 128]
                acc_w = acc_w + _dot(dmt, vn[r0:r0 + MLP_CHUNK, h * 128:(h + 1) * 128], NT)
                acc_b = acc_b + jnp.sum(dmix[r0:r0 + MLP_CHUNK, h * 128:(h + 1) * 128], axis=1, keepdims=True)
            dws_p.append(acc_w)
            dbs_p.append(jnp.broadcast_to(acc_b, (128, 128)))
        for n in range(tm // MLP_CHUNK):
            r0 = n * MLP_CHUNK
            rows.append(jnp.concatenate(
                [_dot(wst_ref[h], dmb[r0:r0 + MLP_CHUNK, h * 128:(h + 1) * 128]) for h in range(SGU_HEADS)], axis=1))
        dvn = jnp.concatenate(rows, axis=0) if len(rows) > 1 else rows[0]
        dlng_p = jnp.sum(dvn * xhat, axis=0, keepdims=True)
        dlnb_p = jnp.sum(dvn, axis=0, keepdims=True)
        dxh = dvn * lng_v
        dgv = rstd * (dxh - jnp.mean(dxh, axis=-1, keepdims=True)
                      - xhat * jnp.mean(dxh * xhat, axis=-1, keepdims=True))
        duv_ref[:, :W] = (dgu * _gelu_grad(ur)).astype(BF16)
        duv_ref[:, W:] = (dgv * _gelu_grad(vr)).astype(BF16)

        @pl.when(i == 0)
        def _():
            for h in range(SGU_HEADS):
                dws_ref[h] = dws_p[h]
                dbs_ref[h] = dbs_p[h]
            dlng_ref[...] = dlng_p
            dlnb_ref[...] = dlnb_p

        @pl.when(i > 0)
        def _():
            for h in range(SGU_HEADS):
                dws_ref[h] += dws_p[h]
                dbs_ref[h] += dbs_p[h]
            dlng_ref[...] += dlng_p
            dlnb_ref[...] += dlnb_p

    vec = Block((1, W), lambda i: (0, 0))
    wsb = Block((SGU_HEADS, 128, 128), lambda i: (0, 0, 0))
    hsq = SDS((SGU_HEADS, 128, 128), F32)
    return pl.pallas_call(
        body, name=name, grid=(T // tm,),
        in_specs=[Block((tm, W), lambda i: (i, 0)), Block((tm, W), lambda i: (i, 1)), Block((tm, W), lambda i: (i, 2)),
                  vec, vec, wsb, wsb, wsb],
        out_specs=[Block((tm, 2 * W), lambda i: (i, 0)), wsb, wsb, vec, vec],
        out_shape=[SDS((T, 2 * W), BF16), hsq, hsq, SDS((1, W), F32), SDS((1, W), F32)],
        compiler_params=_cparams(("arbitrary",), VMEM_BIG))(dyb, proj, proj, lng, lnb, wsm, wsmt, bst)


def _s5_disc(lr, li, ldt, brt, bit):
    dt = jnp.exp(ldt)
    decay = jnp.exp(lr * dt)
    abr = decay * jnp.cos(li * dt)
    abi = decay * jnp.sin(li * dt)
    denom = lr * lr + li * li
    nr = abr - 1.0
    ni = abi
    kr = (nr * lr + ni * li) / denom
    ki = (ni * lr - nr * li) / denom
    bkr = kr[None] * brt - ki[None] * bit
    bki = kr[None] * bit + ki[None] * brt
    return abr, abi, bkr, bki


def _s5_prep(lr, li, ldt, brt, bit):
    G, P, C = S5_GROUPS, S5_STATE, S5_GROUP_WIDTH

    def body(lr_ref, li_ref, ldt_ref, br_ref, bi_ref, abr_ref, abi_ref, pwr_ref, pwi_ref, bkr_ref, bki_ref):
        lr_, li_, ldt_ = lr_ref[...], li_ref[...], ldt_ref[...]
        res = _s5_disc(lr_, li_, ldt_, br_ref[...], bi_ref[...])
        for o, r in zip((abr_ref, abi_ref, bkr_ref, bki_ref), res):
            o[...] = r
        dt = jnp.exp(ldt_)
        n = lax.broadcasted_iota(jnp.int32, (S5_SEG, G, P), 0).astype(F32) + 1.0
        dec = jnp.exp((lr_ * dt)[None] * n)
        ang = (li_ * dt)[None] * n
        pwr_ref[...] = dec * jnp.cos(ang)
        pwi_ref[...] = dec * jnp.sin(ang)

    gp = SDS((G, P), F32)
    sgp = SDS((S5_SEG, G, P), F32)
    cgp = SDS((C, G, P), F32)
    return pl.pallas_call(body, name="s5_prep", out_shape=[gp, gp, sgp, sgp, cgp, cgp])(lr, li, ldt, brt, bit)


def _s5_prep_bwd(lr, li, ldt, brt, bit, dabr, dabi, dbkr, dbki):
    G, P, C = S5_GROUPS, S5_STATE, S5_GROUP_WIDTH

    def body(lr_ref, li_ref, ldt_ref, br_ref, bi_ref, dabr_ref, dabi_ref, dbkr_ref, dbki_ref,
             o_lr, o_li, o_ldt, o_br, o_bi):
        _, pull = jax.vjp(_s5_disc, lr_ref[...], li_ref[...], ldt_ref[...], br_ref[...], bi_ref[...])
        g = pull((dabr_ref[...], dabi_ref[...], dbkr_ref[...], dbki_ref[...]))
        for o, r in zip((o_lr, o_li, o_ldt, o_br, o_bi), g):
            o[...] = r

    gp = SDS((G, P), F32)
    cgp = SDS((C, G, P), F32)
    return pl.pallas_call(body, name="s5_prep_bwd", out_shape=[gp, gp, SDS((G, 1), F32), cgp, cgp])(
        lr, li, ldt, brt, bit, dabr, dabi, dbkr, dbki)


def _s5_scan(buf_ref, ar_row, ai_row, pwr_ref, pwi_ref, carry_ref, LG, xs_ref=None, dar_ref=None, dai_ref=None):
    reverse = xs_ref is not None
    NS, SEG = S5_NS, S5_SEG
    sgn = -1.0 if reverse else 1.0
    for lg in range(NS // LG):
        cr = slice(lg * LG, (lg + 1) * LG)
        ci = slice(NS + lg * LG, NS + (lg + 1) * LG)
        ar1, ai1 = ar_row[:, cr], sgn * ai_row[:, cr]
        asr1, asi1 = pwr_ref[SEG - 1:SEG, cr], sgn * pwi_ref[SEG - 1:SEG, cr]
        ar = jnp.broadcast_to(ar1, (8, LG))
        ai = jnp.broadcast_to(ai1, (8, LG))

        def step_of(j):
            return (SEG - 1 - j) if reverse else j

        def p1(j, st):
            sr, si = st
            rows = pl.ds(pl.multiple_of(step_of(j) * 8, 8), 8)
            nr = ar * sr - ai * si + buf_ref[rows, cr]
            ni = ar * si + ai * sr + buf_ref[rows, ci]
            buf_ref[rows, cr] = nr
            buf_ref[rows, ci] = ni
            return nr, ni

        z = jnp.zeros((8, LG), F32)
        er, ei = lax.fori_loop(0, SEG, p1, (z, z), unroll=2)
        c_r = carry_ref[:, cr]
        c_i = carry_ref[:, ci]
        cs_r, cs_i = [None] * 8, [None] * 8
        order = range(7, -1, -1) if reverse else range(8)
        for s in order:
            cs_r[s], cs_i[s] = c_r, c_i
            e_r, e_i = er[s:s + 1], ei[s:s + 1]
            c_r, c_i = e_r + asr1 * c_r - asi1 * c_i, e_i + asr1 * c_i + asi1 * c_r
        carry_ref[:, cr] = c_r
        carry_ref[:, ci] = c_i
        cmr = jnp.concatenate(cs_r, axis=0)
        cmi = jnp.concatenate(cs_i, axis=0)

        def carried(j):
            pr = pwr_ref[pl.ds(j, 1), cr]
            pi = sgn * pwi_ref[pl.ds(j, 1), cr]
            return pr * cmr - pi * cmi, pr * cmi + pi * cmr

        if not reverse:
            def p2(j, st):
                rows = pl.ds(pl.multiple_of(j * 8, 8), 8)
                wr, wi = carried(j)
                buf_ref[rows, cr] += wr
                buf_ref[rows, ci] += wi
                return st

            lax.fori_loop(0, SEG, p2, 0, unroll=4)
        else:
            def p2(j, st):
                pr, pi, dr, di = st
                rows = pl.ds(pl.multiple_of(step_of(j) * 8, 8), 8)
                xr = xs_ref[rows, cr]
                xi = xs_ref[rows, ci]
                dr = dr + pr * xr + pi * xi
                di = di + pi * xr - pr * xi
                wr, wi = carried(j)
                gr = buf_ref[rows, cr] + wr
                gi = buf_ref[rows, ci] + wi
                buf_ref[rows, cr] = gr
                buf_ref[rows, ci] = gi
                return gr, gi, dr, di

            st = lax.fori_loop(0, SEG, p2, (cmr, cmi, z, z), unroll=4)
            dar_ref[:, cr] += st[2]
            dai_ref[:, cr] += st[3]


def _s5_fwd(proj, perm, permt, bdbr, bdbi, bdcr, bdci, abr, abi, asr, asi, dvec, wglu, bglu):
    T = proj.shape[0]
    TC, NS, W = S5_TC, S5_NS, S5_WIDTH
    nc = T // TC

    def body(u_ref, pm_ref, pmt_ref, bdbr_ref, bdbi_ref, bdcr_ref, bdci_ref, ar_ref, ai_ref, asr_ref, asi_ref,
             d_ref, wglu_ref, bglu_ref, ya_ref, xs_ref, ypre_ref, carry_ref):
        i = pl.program_id(0)

        @pl.when(i == 0)
        def _():
            carry_ref[...] = jnp.zeros_like(carry_ref)

        up = _dot(pm_ref[...], u_ref[...]).astype(BF16)
        for j in range(8):
            ut = up[:, j * 128:(j + 1) * 128]
            xs_ref[:, j * 512:(j + 1) * 512] = _dot(ut, bdbr_ref[j])
            xs_ref[:, NS + j * 512:NS + (j + 1) * 512] = _dot(ut, bdbi_ref[j])
        _s5_scan(xs_ref, ar_ref[...], ai_ref[...], asr_ref, asi_ref, carry_ref, 512)
        ys = []
        for j in range(8):
            xr = xs_ref[:, j * 512:(j + 1) * 512].astype(BF16)
            xi = xs_ref[:, NS + j * 512:NS + (j + 1) * 512].astype(BF16)
            ys.append(_dot(xr, bdcr_ref[j]) + _dot(xi, bdci_ref[j]))
        ypre = jnp.concatenate(ys, axis=1) + d_ref[...] * up.astype(F32)
        ypre_ref[...] = ypre
        ya = _gelu(ypre)
        zl = _dot(ya.astype(BF16), wglu_ref[...]) + bglu_ref[...]
        outp = (ya * _sigmoid(zl)).astype(BF16)
        ya_ref[...] = _dot(pmt_ref[...], outp).astype(BF16)

    return pl.pallas_call(
        body, name="s5_fwd", grid=(nc,),
        in_specs=[Block((TC, W), lambda i: (i, 0)), _const((TC, TC)), _const((TC, TC)),
                  _const((8, 128, 512)), _const((8, 128, 512)), _const((8, 512, 128)), _const((8, 512, 128)),
                  _const((1, NS)), _const((1, NS)), _const((S5_SEG, NS)), _const((S5_SEG, NS)),
                  _const((1, W)), _const((W, W)), _const((1, W))],
        out_specs=[Block((TC, W), lambda i: (i, 0)), Block((TC, 2 * NS), lambda i: (i, 0)),
                   Block((TC, W), lambda i: (i, 0))],
        out_shape=[SDS((T, W), BF16), SDS((T, 2 * NS), F32), SDS((T, W), F32)],
        scratch_shapes=[pltpu.VMEM((1, 2 * NS), F32)],
        compiler_params=_cparams(("arbitrary",), VMEM_BIG),
    )(proj, perm, permt, bdbr, bdbi, bdcr, bdci, abr, abi, asr, asi, dvec, wglu, bglu)


def _s5_bwd(dya, proj, ypre, xs, perm, permt, bdbr, bdbi, bdcr, bdci, abr, abi, asr, asi, dvec, wglu, bglu):
    T = proj.shape[0]
    TC, NS, W = S5_TC, S5_NS, S5_WIDTH
    nc = T // TC

    def body(dya_ref, u_ref, ypre_ref, xs_ref, pm_ref, pmt_ref, bdbr_ref, bdbi_ref, bdcr_ref, bdci_ref,
             ar_ref, ai_ref, asr_ref, asi_ref, d_ref, wglu_ref, bglu_ref,
             du_ref, dar_ref, dai_ref, dd_ref, dbglu_ref, o_dbdbr, o_dbdbi, o_dbdcr, o_dbdci, o_dwglu,
             g_ref, carry_ref, dbdbr_ref, dbdbi_ref, dbdcr_ref, dbdci_ref, dwglu_ref):
        i = pl.program_id(0)

        @pl.when(i == 0)
        def _():
            carry_ref[...] = jnp.zeros_like(carry_ref)
            for r in (dbdbr_ref, dbdbi_ref, dbdcr_ref, dbdci_ref, dar_ref, dai_ref, dd_ref, dwglu_ref, dbglu_ref):
                r[...] = jnp.zeros_like(r)

        pm = pm_ref[...]
        dyo = _dot(pm, dya_ref[...])
        up = _dot(pm, u_ref[...]).astype(BF16)
        upf = up.astype(F32)
        ypre_v = ypre_ref[...]
        ya = _gelu(ypre_v)
        yab = ya.astype(BF16)
        sg = _sigmoid(_dot(yab, wglu_ref[...]) + bglu_ref[...])
        dz = dyo * ya * sg * (1.0 - sg)
        dzb = dz.astype(BF16)
        dya_t = dyo * sg + _dot(dzb, wglu_ref[...], NT)
        dwglu_ref[...] += _dot(yab, dzb, TN)
        dbglu_ref[...] += jnp.sum(dz, axis=0, keepdims=True)
        dy = dya_t * _gelu_grad(ypre_v)
        dd_ref[...] += jnp.sum(dy * upf, axis=0, keepdims=True)
        dyb = dy.astype(BF16)
        for j in range(8):
            dyj = dyb[:, j * 128:(j + 1) * 128]
            g_ref[:, j * 512:(j + 1) * 512] = _dot(dyj, bdcr_ref[j], NT)
            g_ref[:, NS + j * 512:NS + (j + 1) * 512] = _dot(dyj, bdci_ref[j], NT)
            dbdcr_ref[j] += _dot(xs_ref[:, j * 512:(j + 1) * 512].astype(BF16), dyj, TN)
            dbdci_ref[j] += _dot(xs_ref[:, NS + j * 512:NS + (j + 1) * 512].astype(BF16), dyj, TN)
        _s5_scan(g_ref, ar_ref[...], ai_ref[...], asr_ref, asi_ref, carry_ref, 512,
                 xs_ref=xs_ref, dar_ref=dar_ref, dai_ref=dai_ref)
        dus = []
        for j in range(8):
            ut = up[:, j * 128:(j + 1) * 128]
            gr = g_ref[:, j * 512:(j + 1) * 512].astype(BF16)
            gi = g_ref[:, NS + j * 512:NS + (j + 1) * 512].astype(BF16)
            dbdbr_ref[j] += _dot(ut, gr, TN)
            dbdbi_ref[j] += _dot(ut, gi, TN)
            dus.append(_dot(gr, bdbr_ref[j], NT) + _dot(gi, bdbi_ref[j], NT))
        dup = jnp.concatenate(dus, axis=1) + d_ref[...] * dy
        du_ref[...] = _dot(pmt_ref[...], dup.astype(BF16)).astype(BF16)

        @pl.when(i == nc - 1)
        def _():
            for src, dst in ((dbdbr_ref, o_dbdbr), (dbdbi_ref, o_dbdbi), (dbdcr_ref, o_dbdcr),
                             (dbdci_ref, o_dbdci), (dwglu_ref, o_dwglu)):
                pltpu.sync_copy(src, dst)

    c2 = lambda i: (0, 0)
    rev = lambda i: (nc - 1 - i, 0)
    return pl.pallas_call(
        body, name="s5_bwd", grid=(nc,),
        in_specs=[Block((TC, W), rev), Block((TC, W), rev), Block((TC, W), rev), Block((TC, 2 * NS), rev),
                  _const((TC, TC)), _const((TC, TC)),
                  _const((8, 128, 512)), _const((8, 128, 512)), _const((8, 512, 128)), _const((8, 512, 128)),
                  _const((1, NS)), _const((1, NS)), _const((S5_SEG, NS)), _const((S5_SEG, NS)),
                  _const((1, W)), _const((W, W)), _const((1, W))],
        out_specs=[Block((TC, W), rev), Block((8, NS), c2), Block((8, NS), c2), Block((1, W), c2), Block((1, W), c2),
                   ANY, ANY, ANY, ANY, ANY],
        out_shape=[SDS((T, W), BF16), SDS((8, NS), F32), SDS((8, NS), F32), SDS((1, W), F32), SDS((1, W), F32),
                   SDS((8, 128, 512), F32), SDS((8, 128, 512), F32),
                   SDS((8, 512, 128), F32), SDS((8, 512, 128), F32), SDS((W, W), F32)],
        scratch_shapes=[pltpu.VMEM((TC, 2 * NS), F32), pltpu.VMEM((1, 2 * NS), F32),
                        pltpu.VMEM((8, 128, 512), F32), pltpu.VMEM((8, 128, 512), F32),
                        pltpu.VMEM((8, 512, 128), F32), pltpu.VMEM((8, 512, 128), F32), pltpu.VMEM((W, W), F32)],
        compiler_params=_cparams(("arbitrary",), VMEM_BIG),
    )(dya, proj, ypre, xs, perm, permt, bdbr, bdbi, bdcr, bdci, abr, abi, asr, asi, dvec, wglu, bglu)


def _bd_b(bk_t):
    C, P = S5_GROUP_WIDTH, S5_STATE
    t = jnp.transpose(bk_t, (1, 0, 2)).reshape(8, 8, C, P)
    eye = jnp.eye(8, dtype=t.dtype)
    return (t[:, :, :, None, :] * eye[None, :, None, :, None]).reshape(8, 8 * C, 8 * P)


def _bd_b_extract(m):
    C, P = S5_GROUP_WIDTH, S5_STATE
    t = m.reshape(8, 8, C, 8, P)
    d = jnp.stack([t[:, g, :, g, :] for g in range(8)], axis=1)
    return jnp.transpose(d.reshape(S5_GROUPS, C, P), (1, 0, 2))


def _bd_c(c):
    C, P = S5_GROUP_WIDTH, S5_STATE
    t = jnp.transpose(c, (0, 2, 1)).reshape(8, 8, P, C)
    eye = jnp.eye(8, dtype=t.dtype)
    return (t[:, :, :, None, :] * eye[None, :, None, :, None]).reshape(8, 8 * P, 8 * C)


def _bd_c_extract(m):
    C, P = S5_GROUP_WIDTH, S5_STATE
    t = m.reshape(8, 8, P, 8, C)
    d = jnp.stack([t[:, g, :, g, :] for g in range(8)], axis=1)
    return jnp.transpose(d.reshape(S5_GROUPS, P, C), (0, 2, 1))


def _perm_matrix():
    r = jnp.arange(S5_TC)
    src = (r % 8) * S5_SEG + r // 8
    return (src[:, None] == jnp.arange(S5_TC)[None, :]).astype(BF16)


def _coords():
    return lax.axis_index("x"), lax.axis_index("y"), lax.axis_index("c")


def _all_gather(name, arrs):
    n = len(arrs)

    def body(*refs):
        ins, outs = refs[:n], refs[n:2 * n]
        send_sems, recv_sems, local_sems = refs[2 * n:]
        x, y, c = _coords()
        me, sibling = (x, y, c), (x, y, 1 - c)
        chips = [(1 - x, y), (x, 1 - y), (1 - x, 1 - y)]

        def slot(p):
            return 4 * p[0] + 2 * p[1] + p[2]

        def copy(a, k, block, to, src=None):
            dst = outs[a].at[slot(block)]
            return pltpu.make_async_remote_copy(
                src_ref=dst if src is None else src, dst_ref=dst,
                send_sem=send_sems.at[a * 7 + k], recv_sem=recv_sems.at[a * 7 + k],
                device_id=to, device_id_type=MESH)

        mine = [pltpu.make_async_copy(ins[a], outs[a].at[slot(me)], local_sems.at[a]) for a in range(n)]
        for m in mine:
            m.start()
        first = []
        for a in range(n):
            first.append(copy(a, 0, me, sibling, src=ins[a]))
            first += [copy(a, 1 + j, me, (*chip, c), src=ins[a]) for j, chip in enumerate(chips)]
        for cp in first:
            cp.start()
        passed = []
        for j, chip in enumerate(chips):
            for a in range(n):
                copy(a, 1 + j, (*chip, c), me).wait_recv()
                fw = copy(a, 4 + j, (*chip, c), sibling)
                fw.start()
                passed.append(fw)
        for a in range(n):
            copy(a, 0, sibling, me).wait_recv()
            for j, chip in enumerate(chips):
                copy(a, 4 + j, (*chip, 1 - c), me).wait_recv()
        for cp in first + passed:
            cp.wait_send()
        for m in mine:
            m.wait()

    return pl.pallas_call(
        body, name=name,
        in_specs=[ANY] * n, out_specs=[ANY] * n,
        out_shape=[SDS((NDEV,) + a.shape, a.dtype) for a in arrs],
        scratch_shapes=[pltpu.SemaphoreType.DMA((7 * n,)), pltpu.SemaphoreType.DMA((7 * n,)),
                        pltpu.SemaphoreType.DMA((n,))],
    )(*arrs)


HBM = pl.BlockSpec(memory_space=pltpu.HBM)
SEM = pl.BlockSpec(memory_space=pltpu.SEMAPHORE)
EFFECT = pltpu.SideEffectType.DATAFLOW_SIDE_EFFECTING


def _peers7(x, y, c):
    return [(1 - x if fx else x, 1 - y if fy else y, 1 - c if fc else c)
            for fx in (0, 1) for fy in (0, 1) for fc in (0, 1) if fx or fy or fc]


def _slot(p):
    return 4 * p[0] + 2 * p[1] + p[2]


def _split_copies(src_refs, land_refs, send_sems, recv_sems, gather, mine):
    x, y, c = _coords()
    me = (x, y, c)
    out = []
    for a, (src, land) in enumerate(zip(src_refs, land_refs)):
        for k, p in enumerate(_peers7(x, y, c)):
            s = src if gather else src.at[_slot(p)]
            out.append(pltpu.make_async_remote_copy(
                src_ref=s, dst_ref=land.at[_slot(me) if mine else _slot(p)],
                send_sem=send_sems.at[a * 7 + k], recv_sem=recv_sems.at[a * 7 + k],
                device_id=p, device_id_type=MESH))
    return out


def _own_slab(shard):
    x, y, c = _coords()
    z = lax.empty((NDEV,) + shard.shape, shard.dtype)
    return lax.dynamic_update_slice(z, shard[None], (_slot((x, y, c)),) + (0,) * shard.ndim)


def _split_start(name, srcs, lands, gather):
    n = len(srcs)

    def body(*refs):
        src_refs, land_refs = refs[:n], refs[n:2 * n]
        send_sems, recv_sems = refs[2 * n], refs[2 * n + 1]
        token = refs[-1]
        for cp in _split_copies(src_refs, land_refs, send_sems, recv_sems, gather, True):
            cp.start()
        token[...] = jnp.zeros_like(token)

    thru = [pltpu.HBM(a.shape, a.dtype) for a in list(srcs) + list(lands)]
    res = pl.pallas_call(
        body, name=name,
        out_shape=(pltpu.SemaphoreType.DMA((7 * n,)), pltpu.SemaphoreType.DMA((7 * n,)), *thru, SDS((8, 128), F32)),
        in_specs=[HBM] * (2 * n),
        out_specs=(SEM, SEM, *([HBM] * (2 * n)), pl.BlockSpec(memory_space=pltpu.VMEM)),
        input_output_aliases={i: 2 + i for i in range(2 * n)},
        compiler_params=pltpu.CompilerParams(has_side_effects=EFFECT),
    )(*[pltpu.with_memory_space_constraint(a, pltpu.HBM) for a in list(srcs) + list(lands)])
    return res[0], res[1], list(res[2:2 + n]), list(res[2 + n:2 + 2 * n]), res[-1]


def _split_wait(name, started, after, gather):
    send_sems, recv_sems, srcs, lands, _ = started
    n = len(srcs)

    def body(*refs):
        src_refs, land_refs = refs[:n], refs[n:2 * n]
        s_sems, r_sems = refs[2 * n], refs[2 * n + 1]
        for cp in _split_copies(src_refs, land_refs, s_sems, r_sems, gather, False):
            cp.wait_send()
            cp.wait_recv()

    thru = [pltpu.HBM(a.shape, a.dtype) for a in list(srcs) + list(lands)]
    res = pl.pallas_call(
        body, name=name, out_shape=tuple(thru),
        in_specs=[HBM] * (2 * n) + [SEM, SEM, ANY], out_specs=tuple([HBM] * (2 * n)),
        input_output_aliases={i: i for i in range(2 * n)},
        compiler_params=pltpu.CompilerParams(has_side_effects=EFFECT),
    )(*srcs, *lands, send_sems, recv_sems, after)
    return list(res[n:])


def _adam_math(w, g, m, v):
    m = ADAM_B1 * m + (1.0 - ADAM_B1) * g
    v = ADAM_B2 * v + (1.0 - ADAM_B2) * (g * g)
    m_hat = m / (1.0 - ADAM_B1 ** ADAM_STEP)
    v_hat = v / (1.0 - ADAM_B2 ** ADAM_STEP)
    delta = -ADAM_LR * (m_hat / (jnp.sqrt(v_hat) + ADAM_EPS) + ADAM_WD * w)
    return delta, m, v


def _adam_sharded(name, recv, sub, w, m, v):
    R, Cc = w.shape
    tr = max(t for t in range(16, R + 1, 16) if R % t == 0 and t * Cc <= 256 * 1024)

    def body(*refs):
        parts = refs[:NDEV]
        w_ref, m_ref, v_ref, g_out, d_out, m_out, v_out = refs[NDEV:]
        g = parts[0][...].astype(F32)
        for p in parts[1:]:
            g = g + p[...].astype(F32)
        delta, mn, vn = _adam_math(w_ref[...], g, m_ref[...], v_ref[...])
        g_out[...] = g
        d_out[...] = delta
        m_out[...] = mn
        v_out[...] = vn

    if sub is None:
        pspecs = [Block((None, tr, Cc), functools.partial(lambda s, i: (s, i, 0), s)) for s in range(NDEV)]
    else:
        pspecs = [Block((None, None, tr, Cc), functools.partial(lambda s, i: (s, sub, i, 0), s)) for s in range(NDEV)]
    row = Block((tr, Cc), lambda i: (i, 0))
    o = SDS((R, Cc), F32)
    return pl.pallas_call(
        body, name=name, grid=(R // tr,),
        in_specs=pspecs + [row, row, row], out_specs=[row, row, row, row], out_shape=[o, o, o, o],
        compiler_params=_cparams(("arbitrary",), VMEM_BIG))(*([recv] * NDEV), w, m, v)


def _adam_small(parts, w, m, v):
    R = w.shape[0]

    def body(p_ref, w_ref, m_ref, v_ref, g_out, d_out, m_out, v_out):
        g = p_ref[0]
        for s in range(1, NDEV):
            g = g + p_ref[s]
        delta, mn, vn = _adam_math(w_ref[...], g, m_ref[...], v_ref[...])
        g_out[...] = g
        d_out[...] = delta
        m_out[...] = mn
        v_out[...] = vn

    o = SDS((R, 128), F32)
    return pl.pallas_call(body, name="adam_small", out_shape=[o, o, o, o],
                          compiler_params=_cparams(None, VMEM_BIG))(parts, w, m, v)


_SMALL = ["ffn1_norm", "mix_norm", "s5_a_re", "s5_a_im", "s5_log_dt", "s5_b_re", "s5_b_im", "s5_c_re", "s5_c_im",
          "s5_d", "s5_b_glu", "sgu_ln_g", "sgu_ln_b", "sgu_w_s", "sgu_b_s", "b_gate", "ffn2_norm", "final_norm"]
_SHARDED = ["ffn1_w_gate", "ffn1_w_up", "ffn1_w_down", "w_in", "s5_w_glu", "w_branch_a", "w_branch_b", "w_gate",
            "w_out", "ffn2_w_gate", "ffn2_w_up", "ffn2_w_down"]
_ORDER = ["ffn1_norm", "ffn1_w_gate", "ffn1_w_up", "ffn1_w_down", "mix_norm", "w_in", "s5_a_re", "s5_a_im",
          "s5_log_dt", "s5_b_re", "s5_b_im", "s5_c_re", "s5_c_im", "s5_d", "s5_w_glu", "s5_b_glu", "sgu_ln_g",
          "sgu_ln_b", "sgu_w_s", "sgu_b_s", "w_branch_a", "w_branch_b", "w_gate", "b_gate", "w_out", "ffn2_norm",
          "ffn2_w_gate", "ffn2_w_up", "ffn2_w_down", "final_norm"]


def _step(x, tgt, W, M, V):
    T = x.shape[1]
    x0 = x[0]
    tgt0 = tgt[0]
    bf = lambda a: a.astype(BF16)

    def gather_start(name, shards):
        return _split_start(name, shards, [_own_slab(s) for s in shards], True)

    gs1 = gather_start("gather1_start", [jnp.stack([bf(W["ffn1_w_gate"][0]), bf(W["ffn1_w_up"][0])])])
    started = gs1[4][:1, :1]

    lr_, li_ = W["s5_a_re"][0], W["s5_a_im"][0]
    ldt_ = W["s5_log_dt"][0][:, None]
    brt = jnp.transpose(W["s5_b_re"][0], (2, 0, 1))
    bit = jnp.transpose(W["s5_b_im"][0], (2, 0, 1))
    abr, abi, pwr, pwi, bkr_t, bki_t = _s5_prep(lr_, li_, ldt_, brt, bit)
    bdbr, bdbi = bf(_bd_b(bkr_t)), bf(_bd_b(bki_t))
    bdcr, bdci = bf(_bd_c(W["s5_c_re"][0])), bf(_bd_c(-W["s5_c_im"][0]))
    flat = lambda a: a.reshape(1, S5_NS)
    s5a = (_perm_matrix(), _perm_matrix().T, bdbr, bdbi, bdcr, bdci, flat(abr), flat(abi),
           pwr.reshape(S5_SEG, S5_NS), pwi.reshape(S5_SEG, S5_NS),
           W["s5_d"][0].reshape(1, S5_WIDTH))
    blk = jnp.arange(MLP_CHUNK) // CHUNK
    mask = blk[:, None] >= blk[None, :]
    wsm = jnp.where(mask[None], W["sgu_w_s"][0], 0.0)
    wsm_b, wsmt_b = bf(wsm), bf(jnp.transpose(wsm, (0, 2, 1)))
    bst = jnp.broadcast_to(W["sgu_b_s"][0][:, :, None], (SGU_HEADS, MLP_CHUNK, 128))
    bgate2 = W["b_gate"].reshape(2, 1, D_MODEL)

    h1 = _rms_fwd("rms1", x0, W["ffn1_norm"] + started)
    (wgu1,) = _split_wait("gather1_wait", gs1, h1, True)
    dep = (wgu1[0, 0, :1, :1] * 0).astype(BF16)

    def later(a):
        return bf(a) + dep[0]

    gs2 = gather_start("gather2_start", [later(W["ffn1_w_down"][0])])
    ab1, f1 = _ffn_up("ffn1_up", h1, wgu1, gs2[4])
    (wd1,) = _split_wait("gather2_wait", gs2, f1, True)
    dep = (wd1[0, :1, :1] * 0).astype(BF16)
    gs3 = gather_start("gather3_start", [later(W["w_in"][0]), later(W["s5_w_glu"][0])])
    x1 = _ffn_down("ffn1_down", f1, wd1, x0, after=gs3[4])
    h2 = _rms_fwd("rms2", x1, W["mix_norm"])
    win, wglu = _split_wait("gather3_wait", gs3, h2, True)
    wglu = wglu.reshape(S5_WIDTH, S5_WIDTH)
    s5c = s5a + (wglu, W["s5_b_glu"])
    dep = (win[0, :1, :1] * 0).astype(BF16)
    gs4 = gather_start("gather4_start", [later(W["w_gate"][0]), later(W["w_branch_a"][0]),
                                         later(W["w_branch_b"][0]), later(W["w_out"][0])])
    proj = _col_fwd("w_in", h2, win, after=gs4[4])
    ya, xs, ypre = _s5_fwd(proj, *s5c)
    dep = (ya[:1, :1] * 0).astype(BF16)
    gs5 = gather_start("gather5_start", [jnp.stack([later(W["ffn2_w_gate"][0]), later(W["ffn2_w_up"][0])])])
    yb = _sgu_fwd("sgu_fwd", proj, W["sgu_ln_g"] + gs5[4][:1, :1], W["sgu_ln_b"], wsm_b, bst)
    wgate, wba, wbb, wout = _split_wait("gather4_wait", gs4, yb, True)
    wout = wout.reshape(D_MODEL, D_MODEL)
    pa = _col_fwd("branch_a", ya, wba)
    pb = _col_fwd("branch_b", yb, wbb)
    gl = _gate_fwd("gate", h2, wgate, bgate2)
    merged = _merge_fwd("merge", pa, pb, gl)
    x2 = _plain_fwd_res("w_out", merged, wout, x1)
    h3 = _rms_fwd("rms3", x2, W["ffn2_norm"])
    (wgu2,) = _split_wait("gather5_wait", gs5, h3, True)
    dep = (wgu2[0, 0, :1, :1] * 0).astype(BF16)
    gs6 = gather_start("gather6_start", [later(W["ffn2_w_down"][0])])
    ab2, f2 = _ffn_up("ffn2_up", h3, wgu2, gs6[4])
    (wd2,) = _split_wait("gather6_wait", gs6, f2, True)
    x3 = _ffn_down("ffn2_down", f2, wd2, x2)
    loss_p, dx3, dx3b, dgf = _loss_head("loss_head", x3, W["final_norm"].reshape(1, D_MODEL), tgt0)

    def exchange_start(name, grads):
        x_, y_, c_ = _coords()
        me = _slot((x_, y_, c_))
        return _split_start(name, grads, [_own_slab(lax.dynamic_index_in_dim(g, me, 0, keepdims=False))
                                          for g in grads], False)

    dab2 = _ffn_down_bwd_act("ffn2_down_bwd_a", dx3b, wd2, ab2)
    g_wd2 = _ffn_down_bwd_w("ffn2_down_bwd_w", f2, dx3b)
    g_gu2 = _ffn_up_bwd_w("ffn2_up_bwd_w", h3, dab2)
    es1 = exchange_start("exchange1_start", [g_wd2, g_gu2])
    dh3 = _ffn_up_bwd_h("ffn2_up_bwd_h", dab2, wgu2, es1[4])
    dx2, dx2b, dg3 = _rms_bwd("rms3_bwd", dh3, x2, W["ffn2_norm"], dx3)

    dmerged = _plain_bwd_a("w_out_bwd_a", dx2b, wout)
    g_wout = _plain_bwd_w("w_out_bwd_w", merged, dx2b)
    dpa, dpb, dgl, dbgate = _merge_bwd("merge_bwd", dmerged, pa, pb, gl)
    dya = _col_bwd_a("branch_a_bwd_a", dpa, wba)
    g_wba = _col_bwd_w("branch_a_bwd_w", ya, dpa, 256)
    dyb = _col_bwd_a("branch_b_bwd_a", dpb, wbb)
    g_wbb = _col_bwd_w("branch_b_bwd_w", yb, dpb, 256)
    dh2g = _gate_bwd_a("gate_bwd_a", dgl, wgate)
    g_wgate = _gate_bwd_w("gate_bwd_w", h2, dgl, 512)
    duv, dws, dbst, dlng, dlnb = _sgu_bwd("sgu_bwd", dyb, proj, W["sgu_ln_g"], W["sgu_ln_b"], wsm_b, wsmt_b, bst)
    (dua, dar8, dai8, ddv, dbglu, dbdbr, dbdbi, dbdcr, dbdci, g_wglu) = _s5_bwd(dya, proj, ypre, xs, *s5c)
    dproj = jnp.concatenate([dua, duv], axis=1)
    g_win = _col_bwd_w("w_in_bwd_w", h2, dproj, 384)
    g_wout3 = g_wout.reshape(NDEV, D_MODEL // NDEV, D_MODEL)
    g_wglu3 = g_wglu.astype(BF16).reshape(NDEV, S5_WIDTH // NDEV, S5_WIDTH)
    es2 = exchange_start("exchange2_start", [g_wout3, g_wba, g_wbb, g_wgate, g_wglu3, g_win])
    dh2 = _col_bwd_a("w_in_bwd_a", dproj, win, add=dh2g)
    dx1, dx1b, dgm = _rms_bwd("rms2_bwd", dh2, x1, W["mix_norm"] + es2[4][:1, :1], dx2)

    dab1 = _ffn_down_bwd_act("ffn1_down_bwd_a", dx1b, wd1, ab1)
    g_gu1 = _ffn_up_bwd_w("ffn1_up_bwd_w", h1, dab1)
    es3 = exchange_start("exchange3_start", [g_gu1])
    g_wd1 = _ffn_down_bwd_w("ffn1_down_bwd_w", f1, dx1b, after=es3[4])
    es4 = exchange_start("exchange4_start", [g_wd1])
    dh1 = _ffn_up_bwd_h("ffn1_up_bwd_h", dab1, wgu1, es4[4])
    dx0, _, dg1 = _rms_bwd("rms1_bwd", dh1, x0, W["ffn1_norm"], dx1)

    dabr = jnp.sum(dar8, axis=0).reshape(S5_GROUPS, S5_STATE)
    dabi = jnp.sum(dai8, axis=0).reshape(S5_GROUPS, S5_STATE)
    d_lr, d_li, d_ldt, d_brt, d_bit = _s5_prep_bwd(lr_, li_, ldt_, brt, bit, dabr, dabi,
                                                   _bd_b_extract(dbdbr), _bd_b_extract(dbdbi))
    small_g = {
        "ffn1_norm": dg1, "mix_norm": dgm, "ffn2_norm": dg3, "final_norm": dgf,
        "s5_a_re": d_lr, "s5_a_im": d_li, "s5_log_dt": d_ldt,
        "s5_b_re": jnp.transpose(d_brt, (1, 2, 0)), "s5_b_im": jnp.transpose(d_bit, (1, 2, 0)),
        "s5_c_re": _bd_c_extract(dbdcr), "s5_c_im": -_bd_c_extract(dbdci),
        "s5_d": ddv, "s5_b_glu": dbglu, "sgu_ln_g": dlng, "sgu_ln_b": dlnb,
        "sgu_w_s": jnp.where(mask[None], dws, 0.0), "sgu_b_s": dbst[:, :, 0], "b_gate": dbgate,
    }

    sizes = [W[n].size for n in _SMALL]
    total = sum(sizes) + 1
    rows = -(-total // 128)
    rows = -(-rows // 8) * 8
    pad = rows * 128 - total

    def pack(d, extra):
        return jnp.concatenate([d[n].reshape(-1).astype(F32) for n in _SMALL] + [extra, jnp.zeros((pad,), F32)]
                               ).reshape(rows, 128)

    G, Dl, Mn, Vn = {}, {}, {}, {}

    def adam(plan):
        last = None
        for n, recv, sub in plan:
            g, d, mn, vn = _adam_sharded("adam_" + n, recv, sub, W[n][0], M[n][0], V[n][0])
            G[n], Dl[n], Mn[n], Vn[n] = g[None], d[None], mn[None], vn[None]
            last = g
        return last

    r_wd2, r_gu2 = _split_wait("exchange1_wait", es1, dx0, False)
    done = adam([("ffn2_w_down", r_wd2, None), ("ffn2_w_gate", r_gu2, 0), ("ffn2_w_up", r_gu2, 1)])
    r_wout, r_wba, r_wbb, r_wgate, r_wglu, r_win = _split_wait("exchange2_wait", es2, done, False)
    done = adam([("w_out", r_wout, None), ("w_branch_a", r_wba, None), ("w_branch_b", r_wbb, None),
                 ("w_gate", r_wgate, None), ("s5_w_glu", r_wglu, None), ("w_in", r_win, None)])

    late = loss_p[0, :1] + 0.0 * done.reshape(-1)[:1]
    zero1 = jnp.zeros((1,), F32)
    parts = _all_gather("gather_small_grads", [pack(small_g, late)])[0]
    sg, sd, sm, sv = _adam_small(parts, pack(W, zero1), pack(M, zero1), pack(V, zero1))

    def unpack(flat2d, into):
        flat = flat2d.reshape(-1)
        off = 0
        for n, s in zip(_SMALL, sizes):
            into[n] = flat[off:off + s].reshape(W[n].shape)
            off += s
        return flat[off]

    loss = unpack(sg, G)
    unpack(sd, Dl)
    unpack(sm, Mn)
    unpack(sv, Vn)

    (r_gu1,) = _split_wait("exchange3_wait", es3, sg, False)
    done = adam([("ffn1_w_gate", r_gu1, 0), ("ffn1_w_up", r_gu1, 1)])
    (r_wd1,) = _split_wait("exchange4_wait", es4, done, False)
    adam([("ffn1_w_down", r_wd1, None)])

    return loss, dx0[None], G, Dl, Mn, Vn


def kernel(x, ffn1_norm, ffn1_w_gate, ffn1_w_up, ffn1_w_down, mix_norm, w_in, s5_a_re, s5_a_im, s5_log_dt, s5_b_re, s5_b_im, s5_c_re, s5_c_im, s5_d, s5_w_glu, s5_b_glu, sgu_ln_g, sgu_ln_b, sgu_w_s, sgu_b_s, w_branch_a, w_branch_b, w_gate, b_gate, w_out, ffn2_norm, ffn2_w_gate, ffn2_w_up, ffn2_w_down, final_norm, loss_target, m_ffn1_norm, m_ffn1_w_gate, m_ffn1_w_up, m_ffn1_w_down, m_mix_norm, m_w_in, m_s5_a_re, m_s5_a_im, m_s5_log_dt, m_s5_b_re, m_s5_b_im, m_s5_c_re, m_s5_c_im, m_s5_d, m_s5_w_glu, m_s5_b_glu, m_sgu_ln_g, m_sgu_ln_b, m_sgu_w_s, m_sgu_b_s, m_w_branch_a, m_w_branch_b, m_w_gate, m_b_gate, m_w_out, m_ffn2_norm, m_ffn2_w_gate, m_ffn2_w_up, m_ffn2_w_down, m_final_norm, v_ffn1_norm, v_ffn1_w_gate, v_ffn1_w_up, v_ffn1_w_down, v_mix_norm, v_w_in, v_s5_a_re, v_s5_a_im, v_s5_log_dt, v_s5_b_re, v_s5_b_im, v_s5_c_re, v_s5_c_im, v_s5_d, v_s5_w_glu, v_s5_b_glu, v_sgu_ln_g, v_sgu_ln_b, v_sgu_w_s, v_sgu_b_s, v_w_branch_a, v_w_branch_b, v_w_gate, v_b_gate, v_w_out, v_ffn2_norm, v_ffn2_w_gate, v_ffn2_w_up, v_ffn2_w_down, v_final_norm):
    a = locals()
    W = {n: a[n] for n in _ORDER}
    M = {n: a["m_" + n] for n in _ORDER}
    V = {n: a["v_" + n] for n in _ORDER}
    loss, gx, G, Dl, Mn, Vn = _step(x, loss_target, W, M, V)
    return (loss, gx, *[G[n] for n in _ORDER], *[Dl[n] for n in _ORDER], *[Mn[n] for n in _ORDER],
            *[Vn[n] for n in _ORDER])
```

```python
import functools
import math

import jax
import jax.numpy as jnp
from jax import lax
from jax.experimental import pallas as pl
from jax.experimental.pallas import tpu as pltpu

F32 = jnp.float32
BF16 = jnp.bfloat16
NDEV = 8
NORM_EPS = 1e-6
D_MODEL = 2048
D_FF = 5632
FF_SHARD = D_FF // NDEV
S5_WIDTH = 1024
S5_GROUPS = 64
S5_GROUP_WIDTH = 16
S5_STATE = 64
S5_NS = S5_GROUPS * S5_STATE
SGU_WIDTH = 1024
SGU_HEADS = 8
MLP_CHUNK = 128
CHUNK = 64
ADAM_LR, ADAM_B1, ADAM_B2, ADAM_EPS, ADAM_WD, ADAM_STEP = 0.001, 0.9, 0.999, 1e-08, 0.01, 10
S5_TC = 256
S5_SEG = S5_TC // 8
S5_LG = 512
VMEM_BIG = 56 * 1024 * 1024

MESH = pl.DeviceIdType.MESH
SDS = jax.ShapeDtypeStruct
Block = pl.BlockSpec
ANY = pl.BlockSpec(memory_space=pl.ANY)


def _cparams(sem=None, vmem=None):
    return pltpu.CompilerParams(dimension_semantics=sem, vmem_limit_bytes=vmem)


def _const(shape):
    nd = len(shape)
    return pl.BlockSpec(shape, lambda i: (0,) * nd, pipeline_mode=pl.Buffered(1))


def _sigmoid(x):
    return 0.5 * jnp.tanh(0.5 * x) + 0.5


_GELU_C = math.sqrt(2.0 / math.pi)


def _gelu(x):
    return 0.5 * x * (1.0 + jnp.tanh(_GELU_C * (x + 0.044715 * x * x * x)))


def _gelu_grad(x):
    t = jnp.tanh(_GELU_C * (x + 0.044715 * x * x * x))
    return 0.5 * (1.0 + t) + 0.5 * x * (1.0 - t * t) * _GELU_C * (1.0 + 3.0 * 0.044715 * x * x)


NN = (((1,), (0,)), ((), ()))
NT = (((1,), (1,)), ((), ()))
TN = (((0,), (0,)), ((), ()))


def _dot(a, b, dims=NN):
    return lax.dot_general(a, b, dims, preferred_element_type=F32)


def _matmul(name, a, b, extras, *, grid, a_spec, b_spec, extra_specs, out_shapes, out_specs, acc_shape,
            epilogue, dims=NN, nb=None, compute=None, after=None, vmem=VMEM_BIG):
    nk = grid[2]
    if after is not None:
        extras = tuple(extras) + (after,)
        extra_specs = list(extra_specs) + [Block((8, 128), lambda i, j, k: (0, 0))]
    ne, no = len(extras), len(out_shapes)
    nacc = nb or 1
    if compute is None:
        def compute(a_ref, b_ref, q):
            return _dot(a_ref[...], b_ref[q] if nb else b_ref[...], dims)

    def body(*refs):
        a_ref, b_ref = refs[0], refs[1]
        ex = refs[2:2 + ne]
        outs = refs[2 + ne:2 + ne + no]
        if nk == 1:
            epilogue([compute(a_ref, b_ref, q) for q in range(nacc)], ex, outs)
            return
        acc_ref = refs[2 + ne + no]
        k = pl.program_id(2)

        @pl.when(k == 0)
        def _():
            acc_ref[...] = jnp.zeros_like(acc_ref)

        for q in range(nacc):
            acc_ref[q] += compute(a_ref, b_ref, q)

        @pl.when(k == nk - 1)
        def _():
            epilogue([acc_ref[q] for q in range(nacc)], ex, outs)

    scratch = [] if nk == 1 else [pltpu.VMEM((nacc,) + tuple(acc_shape), F32)]
    res = pl.pallas_call(
        body, name=name, grid=grid,
        in_specs=[a_spec, b_spec] + list(extra_specs),
        out_specs=list(out_specs), out_shape=list(out_shapes), scratch_shapes=scratch,
        compiler_params=_cparams(("parallel", "parallel", "arbitrary"), vmem),
    )(a, b, *extras)
    return res


def _store(dtype_outs=None):
    def ep(accs, ex, outs):
        outs[0][...] = accs[0].astype(outs[0].dtype)
    return ep


def _tile(n, t):
    t = min(n, t)
    assert n % t == 0, (n, t)
    return t


def _ksum(kq, dims):
    def compute(a_ref, b_ref, _):
        part = _dot(a_ref[0], b_ref[0], dims)
        for q in range(1, kq):
            part = part + _dot(a_ref[q], b_ref[q], dims)
        return part
    return compute


def _ksum_lanes(kq, ns, dims):
    def compute(a_ref, b_ref, _):
        part = _dot(a_ref[:, 0:ns], b_ref[0], dims)
        for q in range(1, kq):
            part = part + _dot(a_ref[:, q * ns:(q + 1) * ns], b_ref[q], dims)
        return part
    return compute


def _wide_b(g):
    def compute(a_ref, b_ref, _):
        bw = b_ref[0] if g == 1 else jnp.concatenate([b_ref[q] for q in range(g)], axis=1)
        return _dot(a_ref[...], bw, NN)
    return compute


KQ_DOWN = 4
TT_DEEP = 2048


def _ffn_up(name, h, wgu, after=None):
    T, D = h.shape
    tm = _tile(T, 1024)

    def ep(accs, ex, outs):
        a, b = accs
        outs[0][0] = a.astype(BF16)
        outs[0][1] = b.astype(BF16)
        outs[1][...] = (a * _sigmoid(a) * b).astype(BF16)

    return _matmul(
        name, h, wgu, (), after=after, grid=(NDEV, T // tm, 1),
        a_spec=Block((tm, D), lambda j, i, k: (i, 0)),
        b_spec=Block((None, 2, D, FF_SHARD), lambda j, i, k: (j, 0, 0, 0)),
        extra_specs=(),
        out_shapes=[SDS((NDEV, 2, T, FF_SHARD), BF16), SDS((NDEV, T, FF_SHARD), BF16)],
        out_specs=[Block((None, 2, tm, FF_SHARD), lambda j, i, k: (j, 0, i, 0)),
                   Block((None, tm, FF_SHARD), lambda j, i, k: (j, i, 0))],
        acc_shape=(tm, FF_SHARD), dims=NN, nb=2, epilogue=ep)


def _ffn_down(name, f, wd, xres, after=None):
    _, T, _ = f.shape
    tm, tn = _tile(T, 1024), 1024

    def ep(accs, ex, outs):
        outs[0][...] = ex[0][...] + 0.5 * accs[0]

    return _matmul(
        name, f, wd, (xres,), after=after, grid=(T // tm, D_MODEL // tn, NDEV // KQ_DOWN),
        a_spec=Block((KQ_DOWN, tm, FF_SHARD), lambda i, j, k: (k, i, 0)),
        b_spec=Block((KQ_DOWN, FF_SHARD, tn), lambda i, j, k: (k, 0, j)),
        extra_specs=[Block((tm, tn), lambda i, j, k: (i, j))],
        out_shapes=[SDS((T, D_MODEL), F32)],
        out_specs=[Block((tm, tn), lambda i, j, k: (i, j))],
        acc_shape=(tm, tn), compute=_ksum(KQ_DOWN, NN), epilogue=ep)[0]


def _ffn_down_bwd_act(name, dyb, wd, ab):
    T, D = dyb.shape
    tm = _tile(T, 1024)
    def ep(accs, ex, outs):
        df = accs[0]
        a = ex[0][0].astype(F32)
        b = ex[0][1].astype(F32)
        hs = 0.5 * _sigmoid(a)
        outs[0][0] = (df * b * hs * (1.0 + a * (1.0 - 2.0 * hs))).astype(BF16)
        outs[0][1] = (df * a * hs).astype(BF16)

    return _matmul(
        name, dyb, wd, (ab,), grid=(NDEV, T // tm, 1),
        a_spec=Block((tm, D), lambda j, i, k: (i, 0)),
        b_spec=Block((None, FF_SHARD, D), lambda j, i, k: (j, 0, 0)),
        extra_specs=[Block((None, 2, tm, FF_SHARD), lambda j, i, k: (j, 0, i, 0))],
        out_shapes=[SDS((NDEV, 2, T, FF_SHARD), BF16)],
        out_specs=[Block((None, 2, tm, FF_SHARD), lambda j, i, k: (j, 0, i, 0))],
        acc_shape=(tm, FF_SHARD), dims=NT, epilogue=ep)[0]


def _ffn_down_bwd_w(name, f, dyb, after=None):
    _, T, _ = f.shape
    tt, tn = _tile(T, TT_DEEP), 1024

    def ep(accs, ex, outs):
        outs[0][...] = (0.5 * accs[0]).astype(BF16)

    return _matmul(
        name, f, dyb, (), after=after, grid=(NDEV, D_MODEL // tn, T // tt),
        a_spec=Block((None, tt, FF_SHARD), lambda j, n, k: (j, k, 0)),
        b_spec=Block((tt, tn), lambda j, n, k: (k, n)),
        extra_specs=(),
        out_shapes=[SDS((NDEV, FF_SHARD, D_MODEL), BF16)],
        out_specs=[Block((None, FF_SHARD, tn), lambda j, n, k: (j, 0, n))],
        acc_shape=(FF_SHARD, tn), dims=TN, epilogue=ep)[0]


def _ffn_up_bwd_h(name, dab, wgu, after):
    _, _, T, _ = dab.shape
    tm = _tile(T, 1024)
    return _matmul(
        name, dab, wgu, (), after=after, grid=(T // tm, 1, NDEV),
        a_spec=Block((None, 2, tm, FF_SHARD), lambda i, j, k: (k, 0, i, 0)),
        b_spec=Block((None, 2, D_MODEL, FF_SHARD), lambda i, j, k: (k, 0, 0, 0)),
        extra_specs=(),
        out_shapes=[SDS((T, D_MODEL), BF16)],
        out_specs=[Block((tm, D_MODEL), lambda i, j, k: (i, 0))],
        acc_shape=(tm, D_MODEL), compute=_ksum(2, NT), epilogue=_store())[0]


def _ffn_up_bwd_w(name, h, dab):
    T, D = h.shape
    tt, tn = _tile(T, TT_DEEP), 1024

    def ep(accs, ex, outs):
        outs[0][0] = accs[0].astype(BF16)
        outs[0][1] = accs[1].astype(BF16)

    return _matmul(
        name, dab, h, (), grid=(NDEV, D // tn, T // tt),
        a_spec=Block((None, 2, tt, FF_SHARD), lambda j, n, k: (j, 0, k, 0)),
        b_spec=Block((tt, tn), lambda j, n, k: (k, n)),
        extra_specs=(),
        out_shapes=[SDS((NDEV, 2, FF_SHARD, D), BF16)],
        out_specs=[Block((None, 2, FF_SHARD, tn), lambda j, n, k: (j, 0, 0, n))],
        acc_shape=(FF_SHARD, tn), nb=2,
        compute=lambda a_ref, b_ref, q: _dot(a_ref[q], b_ref[...], TN), epilogue=ep)[0]


def _shards_per_step(ns):
    return max(g for g in (1, 2, 4, 8) if g * ns <= 2048)


def _split_lanes(g, ns):
    def ep(accs, ex, outs):
        for q in range(g):
            outs[0][q] = accs[0][:, q * ns:(q + 1) * ns].astype(outs[0].dtype)
    return ep


def _col_fwd(name, a, w, out_dtype=BF16, after=None):
    T, K = a.shape
    ns = w.shape[2]
    g = _shards_per_step(ns)
    tm = _tile(T, 1024)
    return _matmul(
        name, a, w, (), after=after, grid=(NDEV // g, T // tm, 1),
        a_spec=Block((tm, K), lambda j, i, k: (i, 0)),
        b_spec=Block((g, K, ns), lambda j, i, k: (j, 0, 0)),
        extra_specs=(),
        out_shapes=[SDS((T, NDEV * ns), out_dtype)],
        out_specs=[Block((tm, g * ns), lambda j, i, k: (i, j))],
        acc_shape=(tm, g * ns), compute=_wide_b(g), epilogue=_store())[0]


def _col_bwd_a(name, dy, w, add=None):
    T = dy.shape[0]
    _, K, ns = w.shape
    tm, tn = _tile(T, 1024), _tile(K, 1024)

    def ep(accs, ex, outs):
        r = accs[0]
        if add is not None:
            r = r + ex[0][...].astype(F32)
        outs[0][...] = r.astype(BF16)

    extras = () if add is None else (add,)
    return _matmul(
        name, dy, w, extras, grid=(T // tm, K // tn, 1),
        a_spec=Block((tm, NDEV * ns), lambda i, j, k: (i, 0)),
        b_spec=Block((NDEV, tn, ns), lambda i, j, k: (0, j, 0)),
        extra_specs=[Block((tm, tn), lambda i, j, k: (i, j))] * len(extras),
        out_shapes=[SDS((T, K), BF16)],
        out_specs=[Block((tm, tn), lambda i, j, k: (i, j))],
        acc_shape=(tm, tn), compute=_ksum_lanes(NDEV, ns, NT), epilogue=ep)[0]


def _col_bwd_w(name, a, dy, ns):
    T, K = a.shape
    g = _shards_per_step(ns)
    tt, tr = _tile(T, TT_DEEP), _tile(K, 1024)
    return _matmul(
        name, a, dy, (), grid=(NDEV // g, K // tr, T // tt),
        a_spec=Block((tt, tr), lambda j, n, k: (k, n)),
        b_spec=Block((tt, g * ns), lambda j, n, k: (k, j)),
        extra_specs=(),
        out_shapes=[SDS((NDEV, K, ns), BF16)],
        out_specs=[Block((g, tr, ns), lambda j, n, k: (j, n, 0))],
        acc_shape=(tr, g * ns), dims=TN, epilogue=_split_lanes(g, ns))[0]


def _gate_fwd(name, h, w, bias):
    T, K = h.shape
    ns = w.shape[2]
    g = 2
    per = D_MODEL // (g * ns)
    tm = _tile(T, 1024)

    def ep(accs, ex, outs):
        outs[0][...] = (accs[0] + ex[0][...]).astype(BF16)

    return _matmul(
        name, h, w, (bias,), grid=(NDEV // g, T // tm, 1),
        a_spec=Block((tm, K), lambda j, i, k: (i, 0)),
        b_spec=Block((g, K, ns), lambda j, i, k: (j, 0, 0)),
        extra_specs=[Block((None, 1, g * ns), lambda j, i, k: (j // per, 0, j % per))],
        out_shapes=[SDS((2, T, D_MODEL), BF16)],
        out_specs=[Block((None, tm, g * ns), lambda j, i, k: (j // per, i, j % per))],
        acc_shape=(tm, g * ns), compute=_wide_b(g), epilogue=ep)[0]


def _gate_bwd_a(name, dgl, w):
    _, T, _ = dgl.shape
    _, K, ns = w.shape
    per = D_MODEL // ns
    tm, tn = _tile(T, 1024), 1024

    def compute(a_ref, b_ref, _):
        part = None
        for q in range(NDEV):
            d = _dot(a_ref[q // per, :, (q % per) * ns:(q % per + 1) * ns], b_ref[q], NT)
            part = d if part is None else part + d
        return part

    return _matmul(
        name, dgl, w, (), grid=(T // tm, K // tn, 1),
        a_spec=Block((2, tm, D_MODEL), lambda i, j, k: (0, i, 0)),
        b_spec=Block((NDEV, tn, ns), lambda i, j, k: (0, j, 0)),
        extra_specs=(),
        out_shapes=[SDS((T, K), BF16)],
        out_specs=[Block((tm, tn), lambda i, j, k: (i, j))],
        acc_shape=(tm, tn), compute=compute, epilogue=_store())[0]


def _gate_bwd_w(name, h, dgl, ns):
    T, K = h.shape
    g = 2
    per = D_MODEL // (g * ns)
    tt, tr = _tile(T, TT_DEEP), 1024
    return _matmul(
        name, h, dgl, (), grid=(NDEV // g, K // tr, T // tt),
        a_spec=Block((tt, tr), lambda j, n, k: (k, n)),
        b_spec=Block((None, tt, g * ns), lambda j, n, k: (j // per, k, j % per)),
        extra_specs=(),
        out_shapes=[SDS((NDEV, K, ns), BF16)],
        out_specs=[Block((g, tr, ns), lambda j, n, k: (j, n, 0))],
        acc_shape=(tr, g * ns), dims=TN, epilogue=_split_lanes(g, ns))[0]


def _plain_fwd_res(name, a, w, xres):
    T, K = a.shape
    N = w.shape[1]
    tm, tn = _tile(T, 1024), _tile(N, 1024)

    def ep(accs, ex, outs):
        outs[0][...] = ex[0][...] + accs[0]

    return _matmul(
        name, a, w, (xres,), grid=(T // tm, N // tn, 1),
        a_spec=Block((tm, K), lambda i, j, k: (i, 0)),
        b_spec=Block((K, tn), lambda i, j, k: (0, j)),
        extra_specs=[Block((tm, tn), lambda i, j, k: (i, j))],
        out_shapes=[SDS((T, N), F32)],
        out_specs=[Block((tm, tn), lambda i, j, k: (i, j))],
        acc_shape=(tm, tn), dims=NN, nb=None, epilogue=ep)[0]


def _plain_bwd_a(name, dy, w):
    T, N = dy.shape
    K = w.shape[0]
    tm, tn = _tile(T, 1024), _tile(K, 1024)
    return _matmul(
        name, dy, w, (), grid=(T // tm, K // tn, 1),
        a_spec=Block((tm, N), lambda i, j, k: (i, 0)),
        b_spec=Block((tn, N), lambda i, j, k: (j, 0)),
        extra_specs=(),
        out_shapes=[SDS((T, K), BF16)],
        out_specs=[Block((tm, tn), lambda i, j, k: (i, j))],
        acc_shape=(tm, tn), dims=NT, nb=None, epilogue=_store())[0]


def _plain_bwd_w(name, a, dy):
    T, K = a.shape
    N = dy.shape[1]
    tt, tr, tn = _tile(T, TT_DEEP), _tile(K, 1024), _tile(N, 1024)
    return _matmul(
        name, a, dy, (), grid=(K // tr, N // tn, T // tt),
        a_spec=Block((tt, tr), lambda m, n, k: (k, m)),
        b_spec=Block((tt, tn), lambda m, n, k: (k, n)),
        extra_specs=(),
        out_shapes=[SDS((K, N), BF16)],
        out_specs=[Block((tr, tn), lambda m, n, k: (m, n))],
        acc_shape=(tr, tn), dims=TN, nb=None, epilogue=_store())[0]


def _rms_fwd(name, x, g):
    T, D = x.shape
    tm = _tile(T, 512)

    def body(x_ref, g_ref, h_ref):
        xv = x_ref[...]
        r = lax.rsqrt(jnp.mean(xv * xv, axis=-1, keepdims=True) + NORM_EPS)
        h_ref[...] = (xv * r * g_ref[...]).astype(BF16)

    return pl.pallas_call(
        body, name=name, grid=(T // tm,),
        in_specs=[Block((tm, D), lambda i: (i, 0)), Block((1, D), lambda i: (0, 0))],
        out_specs=Block((tm, D), lambda i: (i, 0)), out_shape=SDS((T, D), BF16),
        compiler_params=_cparams(("arbitrary",), VMEM_BIG))(x, g)


def _rms_bwd(name, dh, x, g, dxin):
    T, D = x.shape
    tm = _tile(T, 512)

    def body(dh_ref, x_ref, g_ref, dxin_ref, dx_ref, dxb_ref, dg_ref):
        i = pl.program_id(0)
        xv = x_ref[...]
        dh = dh_ref[...].astype(F32)
        r = lax.rsqrt(jnp.mean(xv * xv, axis=-1, keepdims=True) + NORM_EPS)
        xh = xv * r
        gd = dh * g_ref[...]
        dx = dxin_ref[...] + r * (gd - xh * jnp.mean(gd * xh, axis=-1, keepdims=True))
        dx_ref[...] = dx
        dxb_ref[...] = dx.astype(BF16)
        dgp = jnp.sum(dh * xh, axis=0, keepdims=True)

        @pl.when(i == 0)
        def _():
            dg_ref[...] = dgp

        @pl.when(i > 0)
        def _():
            dg_ref[...] += dgp

    row = Block((tm, D), lambda i: (i, 0))
    vec = Block((1, D), lambda i: (0, 0))
    return pl.pallas_call(
        body, name=name, grid=(T // tm,),
        in_specs=[row, row, vec, row], out_specs=[row, row, vec],
        out_shape=[SDS((T, D), F32), SDS((T, D), BF16), SDS((1, D), F32)],
        compiler_params=_cparams(("arbitrary",), VMEM_BIG))(dh, x, g, dxin)


def _loss_head(name, x, g, tgt):
    T, D = x.shape
    tm = _tile(T, 512)

    def body(x_ref, g_ref, t_ref, loss_ref, dx_ref, dxb_ref, dg_ref):
        i = pl.program_id(0)
        xv = x_ref[...]
        gv = g_ref[...]
        r = lax.rsqrt(jnp.mean(xv * xv, axis=-1, keepdims=True) + NORM_EPS)
        xh = xv * r
        err = xh * gv - t_ref[...]
        lp = 0.5 * jnp.sum(jnp.mean(err * err, axis=-1, keepdims=True), axis=0, keepdims=True)
        dout = err * (1.0 / D)
        gd = dout * gv
        dx = r * (gd - xh * jnp.mean(gd * xh, axis=-1, keepdims=True))
        dx_ref[...] = dx
        dxb_ref[...] = dx.astype(BF16)
        dgp = jnp.sum(dout * xh, axis=0, keepdims=True)
        lpb = jnp.broadcast_to(lp, (1, 128))

        @pl.when(i == 0)
        def _():
            dg_ref[...] = dgp
            loss_ref[...] = lpb

        @pl.when(i > 0)
        def _():
            dg_ref[...] += dgp
            loss_ref[...] += lpb

    row = Block((tm, D), lambda i: (i, 0))
    vec = Block((1, D), lambda i: (0, 0))
    return pl.pallas_call(
        body, name=name, grid=(T // tm,),
        in_specs=[row, vec, row], out_specs=[Block((1, 128), lambda i: (0, 0)), row, row, vec],
        out_shape=[SDS((1, 128), F32), SDS((T, D), F32), SDS((T, D), BF16), SDS((1, D), F32)],
        compiler_params=_cparams(("arbitrary",), VMEM_BIG))(x, g, tgt)


def _merge_fwd(name, pa, pb, gl):
    T, D = pa.shape
    tm = _tile(T, 512)

    def body(pa_ref, pb_ref, gl_ref, o_ref):
        ga = _sigmoid(gl_ref[0].astype(F32))
        gb = _sigmoid(gl_ref[1].astype(F32))
        o_ref[...] = (ga * pa_ref[...].astype(F32) + gb * pb_ref[...].astype(F32)).astype(BF16)

    row = Block((tm, D), lambda i: (i, 0))
    return pl.pallas_call(
        body, name=name, grid=(T // tm,),
        in_specs=[row, row, Block((2, tm, D), lambda i: (0, i, 0))], out_specs=row,
        out_shape=SDS((T, D), BF16), compiler_params=_cparams(("arbitrary",), VMEM_BIG))(pa, pb, gl)


def _merge_bwd(name, dm, pa, pb, gl):
    T, D = pa.shape
    tm = _tile(T, 512)

    def body(dm_ref, pa_ref, pb_ref, gl_ref, dpa_ref, dpb_ref, dgl_ref, db_ref):
        i = pl.program_id(0)
        dmv = dm_ref[...].astype(F32)
        ga = _sigmoid(gl_ref[0].astype(F32))
        gb = _sigmoid(gl_ref[1].astype(F32))
        dpa_ref[...] = (dmv * ga).astype(BF16)
        dpb_ref[...] = (dmv * gb).astype(BF16)
        dga = dmv * pa_ref[...].astype(F32) * ga * (1.0 - ga)
        dgb = dmv * pb_ref[...].astype(F32) * gb * (1.0 - gb)
        dgl_ref[0] = dga.astype(BF16)
        dgl_ref[1] = dgb.astype(BF16)
        sa = jnp.sum(dga, axis=0, keepdims=True)
        sb = jnp.sum(dgb, axis=0, keepdims=True)

        @pl.when(i == 0)
        def _():
            db_ref[0] = sa
            db_ref[1] = sb

        @pl.when(i > 0)
        def _():
            db_ref[0] += sa
            db_ref[1] += sb

    row = Block((tm, D), lambda i: (i, 0))
    two = Block((2, tm, D), lambda i: (0, i, 0))
    return pl.pallas_call(
        body, name=name, grid=(T // tm,),
        in_specs=[row, row, row, two], out_specs=[row, row, two, Block((2, 1, D), lambda i: (0, 0, 0))],
        out_shape=[SDS((T, D), BF16), SDS((T, D), BF16), SDS((2, T, D), BF16), SDS((2, 1, D), F32)],
        compiler_params=_cparams(("arbitrary",), VMEM_BIG))(dm, pa, pb, gl)


def _sgu_core(ur, vr, lng, lnb, ws_ref, bs_ref):
    tm = ur.shape[0]
    gu = _gelu(ur)
    gv = _gelu(vr)
    mu = jnp.mean(gv, axis=-1, keepdims=True)
    cen = gv - mu
    rstd = lax.rsqrt(jnp.mean(cen * cen, axis=-1, keepdims=True) + NORM_EPS)
    xhat = cen * rstd
    vn = (xhat * lng + lnb).astype(BF16)
    rows = []
    for n in range(tm // MLP_CHUNK):
        cols = []
        for h in range(SGU_HEADS):
            blk = vn[n * MLP_CHUNK:(n + 1) * MLP_CHUNK, h * 128:(h + 1) * 128]
            cols.append(_dot(ws_ref[h], blk) + bs_ref[h])
        rows.append(jnp.concatenate(cols, axis=1))
    mixed = jnp.concatenate(rows, axis=0) if len(rows) > 1 else rows[0]
    return gu, xhat, rstd, vn, mixed


def _sgu_fwd(name, proj, lng, lnb, wsm, bst):
    T = proj.shape[0]
    W = SGU_WIDTH
    tm = _tile(T, 512)

    def body(u_ref, v_ref, lng_ref, lnb_ref, ws_ref, bs_ref, o_ref):
        gu, _, _, _, mixed = _sgu_core(u_ref[...].astype(F32), v_ref[...].astype(F32), lng_ref[...], lnb_ref[...],
                                       ws_ref, bs_ref)
        o_ref[...] = (gu * mixed).astype(BF16)

    vec = Block((1, W), lambda i: (0, 0))
    return pl.pallas_call(
        body, name=name, grid=(T // tm,),
        in_specs=[Block((tm, W), lambda i: (i, 1)), Block((tm, W), lambda i: (i, 2)), vec, vec,
                  Block((SGU_HEADS, 128, 128), lambda i: (0, 0, 0)), Block((SGU_HEADS, 128, 128), lambda i: (0, 0, 0))],
        out_specs=Block((tm, W), lambda i: (i, 0)), out_shape=SDS((T, W), BF16),
        compiler_params=_cparams(("arbitrary",), VMEM_BIG))(proj, proj, lng, lnb, wsm, bst)


def _sgu_bwd(name, dyb, proj, lng, lnb, wsm, wsmt, bst):
    T = proj.shape[0]
    W = SGU_WIDTH
    tm = _tile(T, 512)

    def body(dy_ref, u_ref, v_ref, lng_ref, lnb_ref, ws_ref, wst_ref, bs_ref,
             duv_ref, dws_ref, dbs_ref, dlng_ref, dlnb_ref):
        i = pl.program_id(0)
        ur = u_ref[...].astype(F32)
        vr = v_ref[...].astype(F32)
        lng_v = lng_ref[...]
        gu, xhat, rstd, vn, mixed = _sgu_core(ur, vr, lng_v, lnb_ref[...], ws_ref, bs_ref)
        dy = dy_ref[...].astype(F32)
        dgu = dy * mixed
        dmix = dy * gu
        dmb = dmix.astype(BF16)
        dws_p, dbs_p, rows = [], [], []
        for h in range(SGU_HEADS):
            acc_w = jnp.zeros((128, 128), F32)
            acc_b = jnp.zeros((128, 1), F32)
            for n in range(tm // MLP_CHUNK):
                r0 = n * MLP_CHUNK
                dmt = dmb[r0:r0 + MLP_CHUNK, h * 128:(h + 1) * 128]
                acc_w = acc_w + _dot(dmt, vn[r0:r0 + MLP_CHUNK, h * 128:(h + 1) * 128], NT)
                acc_b = acc_b + jnp.sum(dmix[r0:r0 + MLP_CHUNK, h * 128:(h + 1) * 128], axis=1, keepdims=True)
            dws_p.append(acc_w)
            dbs_p.append(jnp.broadcast_to(acc_b, (128, 128)))
        for n in range(tm // MLP_CHUNK):
            r0 = n * MLP_CHUNK
            rows.append(jnp.concatenate(
                [_dot(wst_ref[h], dmb[r0:r0 + MLP_CHUNK, h * 128:(h + 1) * 128]) for h in range(SGU_HEADS)], axis=1))
        dvn = jnp.concatenate(rows, axis=0) if len(rows) > 1 else rows[0]
        dlng_p = jnp.sum(dvn * xhat, axis=0, keepdims=True)
        dlnb_p = jnp.sum(dvn, axis=0, keepdims=True)
        dxh = dvn * lng_v
        dgv = rstd * (dxh - jnp.mean(dxh, axis=-1, keepdims=True)
                      - xhat * jnp.mean(dxh * xhat, axis=-1, keepdims=True))
        duv_ref[:, :W] = (dgu * _gelu_grad(ur)).astype(BF16)
        duv_ref[:, W:] = (dgv * _gelu_grad(vr)).astype(BF16)

        @pl.when(i == 0)
        def _():
            for h in range(SGU_HEADS):
                dws_ref[h] = dws_p[h]
                dbs_ref[h] = dbs_p[h]
            dlng_ref[...] = dlng_p
            dlnb_ref[...] = dlnb_p

        @pl.when(i > 0)
        def _():
            for h in range(SGU_HEADS):
                dws_ref[h] += dws_p[h]
                dbs_ref[h] += dbs_p[h]
            dlng_ref[...] += dlng_p
            dlnb_ref[...] += dlnb_p

    vec = Block((1, W), lambda i: (0, 0))
    wsb = Block((SGU_HEADS, 128, 128), lambda i: (0, 0, 0))
    hsq = SDS((SGU_HEADS, 128, 128), F32)
    return pl.pallas_call(
        body, name=name, grid=(T // tm,),
        in_specs=[Block((tm, W), lambda i: (i, 0)), Block((tm, W), lambda i: (i, 1)), Block((tm, W), lambda i: (i, 2)),
                  vec, vec, wsb, wsb, wsb],
        out_specs=[Block((tm, 2 * W), lambda i: (i, 0)), wsb, wsb, vec, vec],
        out_shape=[SDS((T, 2 * W), BF16), hsq, hsq, SDS((1, W), F32), SDS((1, W), F32)],
        compiler_params=_cparams(("arbitrary",), VMEM_BIG))(dyb, proj, proj, lng, lnb, wsm, wsmt, bst)


def _s5_disc(lr, li, ldt, brt, bit):
    dt = jnp.exp(ldt)
    decay = jnp.exp(lr * dt)
    abr = decay * jnp.cos(li * dt)
    abi = decay * jnp.sin(li * dt)
    denom = lr * lr + li * li
    nr = abr - 1.0
    ni = abi
    kr = (nr * lr + ni * li) / denom
    ki = (ni * lr - nr * li) / denom
    bkr = kr[None] * brt - ki[None] * bit
    bki = kr[None] * bit + ki[None] * brt
    return abr, abi, bkr, bki


def _s5_prep(lr, li, ldt, brt, bit):
    G, P, C = S5_GROUPS, S5_STATE, S5_GROUP_WIDTH

    def body(lr_ref, li_ref, ldt_ref, br_ref, bi_ref, abr_ref, abi_ref, pwr_ref, pwi_ref, bkr_ref, bki_ref):
        lr_, li_, ldt_ = lr_ref[...], li_ref[...], ldt_ref[...]
        res = _s5_disc(lr_, li_, ldt_, br_ref[...], bi_ref[...])
        for o, r in zip((abr_ref, abi_ref, bkr_ref, bki_ref), res):
            o[...] = r
        dt = jnp.exp(ldt_)
        n = lax.broadcasted_iota(jnp.int32, (S5_SEG, G, P), 0).astype(F32) + 1.0
        dec = jnp.exp((lr_ * dt)[None] * n)
        ang = (li_ * dt)[None] * n
        pwr_ref[...] = dec * jnp.cos(ang)
        pwi_ref[...] = dec * jnp.sin(ang)

    gp = SDS((G, P), F32)
    sgp = SDS((S5_SEG, G, P), F32)
    cgp = SDS((C, G, P), F32)
    return pl.pallas_call(body, name="s5_prep", out_shape=[gp, gp, sgp, sgp, cgp, cgp])(lr, li, ldt, brt, bit)


def _s5_prep_bwd(lr, li, ldt, brt, bit, dabr, dabi, dbkr, dbki):
    G, P, C = S5_GROUPS, S5_STATE, S5_GROUP_WIDTH

    def body(lr_ref, li_ref, ldt_ref, br_ref, bi_ref, dabr_ref, dabi_ref, dbkr_ref, dbki_ref,
             o_lr, o_li, o_ldt, o_br, o_bi):
        _, pull = jax.vjp(_s5_disc, lr_ref[...], li_ref[...], ldt_ref[...], br_ref[...], bi_ref[...])
        g = pull((dabr_ref[...], dabi_ref[...], dbkr_ref[...], dbki_ref[...]))
        for o, r in zip((o_lr, o_li, o_ldt, o_br, o_bi), g):
            o[...] = r

    gp = SDS((G, P), F32)
    cgp = SDS((C, G, P), F32)
    return pl.pallas_call(body, name="s5_prep_bwd", out_shape=[gp, gp, SDS((G, 1), F32), cgp, cgp])(
        lr, li, ldt, brt, bit, dabr, dabi, dbkr, dbki)


def _s5_scan(buf_ref, ar_row, ai_row, pwr_ref, pwi_ref, carry_ref, LG, xs_ref=None, dar_ref=None, dai_ref=None):
    reverse = xs_ref is not None
    NS, SEG = S5_NS, S5_SEG
    sgn = -1.0 if reverse else 1.0
    for lg in range(NS // LG):
        cr = slice(lg * LG, (lg + 1) * LG)
        ci = slice(NS + lg * LG, NS + (lg + 1) * LG)
        ar1, ai1 = ar_row[:, cr], sgn * ai_row[:, cr]
        asr1, asi1 = pwr_ref[SEG - 1:SEG, cr], sgn * pwi_ref[SEG - 1:SEG, cr]
        ar = jnp.broadcast_to(ar1, (8, LG))
        ai = jnp.broadcast_to(ai1, (8, LG))

        def step_of(j):
            return (SEG - 1 - j) if reverse else j

        def p1(j, st):
            sr, si = st
            rows = pl.ds(pl.multiple_of(step_of(j) * 8, 8), 8)
            nr = ar * sr - ai * si + buf_ref[rows, cr]
            ni = ar * si + ai * sr + buf_ref[rows, ci]
            buf_ref[rows, cr] = nr
            buf_ref[rows, ci] = ni
            return nr, ni

        z = jnp.zeros((8, LG), F32)
        er, ei = lax.fori_loop(0, SEG, p1, (z, z), unroll=2)
        c_r = carry_ref[:, cr]
        c_i = carry_ref[:, ci]
        cs_r, cs_i = [None] * 8, [None] * 8
        order = range(7, -1, -1) if reverse else range(8)
        for s in order:
            cs_r[s], cs_i[s] = c_r, c_i
            e_r, e_i = er[s:s + 1], ei[s:s + 1]
            c_r, c_i = e_r + asr1 * c_r - asi1 * c_i, e_i + asr1 * c_i + asi1 * c_r
        carry_ref[:, cr] = c_r
        carry_ref[:, ci] = c_i
        cmr = jnp.concatenate(cs_r, axis=0)
        cmi = jnp.concatenate(cs_i, axis=0)

        def carried(j):
            pr = pwr_ref[pl.ds(j, 1), cr]
            pi = sgn * pwi_ref[pl.ds(j, 1), cr]
            return pr * cmr - pi * cmi, pr * cmi + pi * cmr

        if not reverse:
            def p2(j, st):
                rows = pl.ds(pl.multiple_of(j * 8, 8), 8)
                wr, wi = carried(j)
                buf_ref[rows, cr] += wr
                buf_ref[rows, ci] += wi
                return st

            lax.fori_loop(0, SEG, p2, 0, unroll=4)
        else:
            def p2(j, st):
                pr, pi, dr, di = st
                rows = pl.ds(pl.multiple_of(step_of(j) * 8, 8), 8)
                xr = xs_ref[rows, cr]
                xi = xs_ref[rows, ci]
                dr = dr + pr * xr + pi * xi
                di = di + pi * xr - pr * xi
                wr, wi = carried(j)
                gr = buf_ref[rows, cr] + wr
                gi = buf_ref[rows, ci] + wi
                buf_ref[rows, cr] = gr
                buf_ref[rows, ci] = gi
                return gr, gi, dr, di

            st = lax.fori_loop(0, SEG, p2, (cmr, cmi, z, z), unroll=4)
            dar_ref[:, cr] += st[2]
            dai_ref[:, cr] += st[3]


def _s5_fwd(proj, perm, permt, bdbr, bdbi, bdcr, bdci, abr, abi, asr, asi, dvec, wglu, bglu):
    T = proj.shape[0]
    TC, NS, W = S5_TC, S5_NS, S5_WIDTH
    nc = T // TC

    def body(u_ref, pm_ref, pmt_ref, bdbr_ref, bdbi_ref, bdcr_ref, bdci_ref, ar_ref, ai_ref, asr_ref, asi_ref,
             d_ref, wglu_ref, bglu_ref, ya_ref, xs_ref, ypre_ref, carry_ref):
        i = pl.program_id(0)

        @pl.when(i == 0)
        def _():
            carry_ref[...] = jnp.zeros_like(carry_ref)

        up = _dot(pm_ref[...], u_ref[...]).astype(BF16)
        for j in range(8):
            ut = up[:, j * 128:(j + 1) * 128]
            xs_ref[:, j * 512:(j + 1) * 512] = _dot(ut, bdbr_ref[j])
            xs_ref[:, NS + j * 512:NS + (j + 1) * 512] = _dot(ut, bdbi_ref[j])
        _s5_scan(xs_ref, ar_ref[...], ai_ref[...], asr_ref, asi_ref, carry_ref, 512)
        ys = []
        for j in range(8):
            xr = xs_ref[:, j * 512:(j + 1) * 512].astype(BF16)
            xi = xs_ref[:, NS + j * 512:NS + (j + 1) * 512].astype(BF16)
            ys.append(_dot(xr, bdcr_ref[j]) + _dot(xi, bdci_ref[j]))
        ypre = jnp.concatenate(ys, axis=1) + d_ref[...] * up.astype(F32)
        ypre_ref[...] = ypre
        ya = _gelu(ypre)
        zl = _dot(ya.astype(BF16), wglu_ref[...]) + bglu_ref[...]
        outp = (ya * _sigmoid(zl)).astype(BF16)
        ya_ref[...] = _dot(pmt_ref[...], outp).astype(BF16)

    return pl.pallas_call(
        body, name="s5_fwd", grid=(nc,),
        in_specs=[Block((TC, W), lambda i: (i, 0)), _const((TC, TC)), _const((TC, TC)),
                  _const((8, 128, 512)), _const((8, 128, 512)), _const((8, 512, 128)), _const((8, 512, 128)),
                  _const((1, NS)), _const((1, NS)), _const((S5_SEG, NS)), _const((S5_SEG, NS)),
                  _const((1, W)), _const((W, W)), _const((1, W))],
        out_specs=[Block((TC, W), lambda i: (i, 0)), Block((TC, 2 * NS), lambda i: (i, 0)),
                   Block((TC, W), lambda i: (i, 0))],
        out_shape=[SDS((T, W), BF16), SDS((T, 2 * NS), F32), SDS((T, W), F32)],
        scratch_shapes=[pltpu.VMEM((1, 2 * NS), F32)],
        compiler_params=_cparams(("arbitrary",), VMEM_BIG),
    )(proj, perm, permt, bdbr, bdbi, bdcr, bdci, abr, abi, asr, asi, dvec, wglu, bglu)


def _s5_bwd(dya, proj, ypre, xs, perm, permt, bdbr, bdbi, bdcr, bdci, abr, abi, asr, asi, dvec, wglu, bglu):
    T = proj.shape[0]
    TC, NS, W = S5_TC, S5_NS, S5_WIDTH
    nc = T // TC

    def body(dya_ref, u_ref, ypre_ref, xs_ref, pm_ref, pmt_ref, bdbr_ref, bdbi_ref, bdcr_ref, bdci_ref,
             ar_ref, ai_ref, asr_ref, asi_ref, d_ref, wglu_ref, bglu_ref,
             du_ref, dar_ref, dai_ref, dd_ref, dbglu_ref, o_dbdbr, o_dbdbi, o_dbdcr, o_dbdci, o_dwglu,
             g_ref, carry_ref, dbdbr_ref, dbdbi_ref, dbdcr_ref, dbdci_ref, dwglu_ref):
        i = pl.program_id(0)

        @pl.when(i == 0)
        def _():
            carry_ref[...] = jnp.zeros_like(carry_ref)
            for r in (dbdbr_ref, dbdbi_ref, dbdcr_ref, dbdci_ref, dar_ref, dai_ref, dd_ref, dwglu_ref, dbglu_ref):
                r[...] = jnp.zeros_like(r)

        pm = pm_ref[...]
        dyo = _dot(pm, dya_ref[...])
        up = _dot(pm, u_ref[...]).astype(BF16)
        upf = up.astype(F32)
        ypre_v = ypre_ref[...]
        ya = _gelu(ypre_v)
        yab = ya.astype(BF16)
        sg = _sigmoid(_dot(yab, wglu_ref[...]) + bglu_ref[...])
        dz = dyo * ya * sg * (1.0 - sg)
        dzb = dz.astype(BF16)
        dya_t = dyo * sg + _dot(dzb, wglu_ref[...], NT)
        dwglu_ref[...] += _dot(yab, dzb, TN)
        dbglu_ref[...] += jnp.sum(dz, axis=0, keepdims=True)
        dy = dya_t * _gelu_grad(ypre_v)
        dd_ref[...] += jnp.sum(dy * upf, axis=0, keepdims=True)
        dyb = dy.astype(BF16)
        for j in range(8):
            dyj = dyb[:, j * 128:(j + 1) * 128]
            g_ref[:, j * 512:(j + 1) * 512] = _dot(dyj, bdcr_ref[j], NT)
            g_ref[:, NS + j * 512:NS + (j + 1) * 512] = _dot(dyj, bdci_ref[j], NT)
            dbdcr_ref[j] += _dot(xs_ref[:, j * 512:(j + 1) * 512].astype(BF16), dyj, TN)
            dbdci_ref[j] += _dot(xs_ref[:, NS + j * 512:NS + (j + 1) * 512].astype(BF16), dyj, TN)
        _s5_scan(g_ref, ar_ref[...], ai_ref[...], asr_ref, asi_ref, carry_ref, 512,
                 xs_ref=xs_ref, dar_ref=dar_ref, dai_ref=dai_ref)
        dus = []
        for j in range(8):
            ut = up[:, j * 128:(j + 1) * 128]
            gr = g_ref[:, j * 512:(j + 1) * 512].astype(BF16)
            gi = g_ref[:, NS + j * 512:NS + (j + 1) * 512].astype(BF16)
            dbdbr_ref[j] += _dot(ut, gr, TN)
            dbdbi_ref[j] += _dot(ut, gi, TN)
            dus.append(_dot(gr, bdbr_ref[j], NT) + _dot(gi, bdbi_ref[j], NT))
        dup = jnp.concatenate(dus, axis=1) + d_ref[...] * dy
        du_ref[...] = _dot(pmt_ref[...], dup.astype(BF16)).astype(BF16)

        @pl.when(i == nc - 1)
        def _():
            for src, dst in ((dbdbr_ref, o_dbdbr), (dbdbi_ref, o_dbdbi), (dbdcr_ref, o_dbdcr),
                             (dbdci_ref, o_dbdci), (dwglu_ref, o_dwglu)):
                pltpu.sync_copy(src, dst)

    c2 = lambda i: (0, 0)
    rev = lambda i: (nc - 1 - i, 0)
    return pl.pallas_call(
        body, name="s5_bwd", grid=(nc,),
        in_specs=[Block((TC, W), rev), Block((TC, W), rev), Block((TC, W), rev), Block((TC, 2 * NS), rev),
                  _const((TC, TC)), _const((TC, TC)),
                  _const((8, 128, 512)), _const((8, 128, 512)), _const((8, 512, 128)), _const((8, 512, 128)),
                  _const((1, NS)), _const((1, NS)), _const((S5_SEG, NS)), _const((S5_SEG, NS)),
                  _const((1, W)), _const((W, W)), _const((1, W))],
        out_specs=[Block((TC, W), rev), Block((8, NS), c2), Block((8, NS), c2), Block((1, W), c2), Block((1, W), c2),
                   ANY, ANY, ANY, ANY, ANY],
        out_shape=[SDS((T, W), BF16), SDS((8, NS), F32), SDS((8, NS), F32), SDS((1, W), F32), SDS((1, W), F32),
                   SDS((8, 128, 512), F32), SDS((8, 128, 512), F32),
                   SDS((8, 512, 128), F32), SDS((8, 512, 128), F32), SDS((W, W), F32)],
        scratch_shapes=[pltpu.VMEM((TC, 2 * NS), F32), pltpu.VMEM((1, 2 * NS), F32),
                        pltpu.VMEM((8, 128, 512), F32), pltpu.VMEM((8, 128, 512), F32),
                        pltpu.VMEM((8, 512, 128), F32), pltpu.VMEM((8, 512, 128), F32), pltpu.VMEM((W, W), F32)],
        compiler_params=_cparams(("arbitrary",), VMEM_BIG),
    )(dya, proj, ypre, xs, perm, permt, bdbr, bdbi, bdcr, bdci, abr, abi, asr, asi, dvec, wglu, bglu)


def _bd_b(bk_t):
    C, P = S5_GROUP_WIDTH, S5_STATE
    t = jnp.transpose(bk_t, (1, 0, 2)).reshape(8, 8, C, P)
    eye = jnp.eye(8, dtype=t.dtype)
    return (t[:, :, :, None, :] * eye[None, :, None, :, None]).reshape(8, 8 * C, 8 * P)


def _bd_b_extract(m):
    C, P = S5_GROUP_WIDTH, S5_STATE
    t = m.reshape(8, 8, C, 8, P)
    d = jnp.stack([t[:, g, :, g, :] for g in range(8)], axis=1)
    return jnp.transpose(d.reshape(S5_GROUPS, C, P), (1, 0, 2))


def _bd_c(c):
    C, P = S5_GROUP_WIDTH, S5_STATE
    t = jnp.transpose(c, (0, 2, 1)).reshape(8, 8, P, C)
    eye = jnp.eye(8, dtype=t.dtype)
    return (t[:, :, :, None, :] * eye[None, :, None, :, None]).reshape(8, 8 * P, 8 * C)


def _bd_c_extract(m):
    C, P = S5_GROUP_WIDTH, S5_STATE
    t = m.reshape(8, 8, P, 8, C)
    d = jnp.stack([t[:, g, :, g, :] for g in range(8)], axis=1)
    return jnp.transpose(d.reshape(S5_GROUPS, P, C), (0, 2, 1))


def _perm_matrix():
    r = jnp.arange(S5_TC)
    src = (r % 8) * S5_SEG + r // 8
    return (src[:, None] == jnp.arange(S5_TC)[None, :]).astype(BF16)


def _coords():
    return lax.axis_index("x"), lax.axis_index("y"), lax.axis_index("c")


def _all_gather(name, arrs):
    n = len(arrs)

    def body(*refs):
        ins, outs = refs[:n], refs[n:2 * n]
        send_sems, recv_sems, local_sems = refs[2 * n:]
        x, y, c = _coords()
        me, sibling = (x, y, c), (x, y, 1 - c)
        chips = [(1 - x, y), (x, 1 - y), (1 - x, 1 - y)]

        def slot(p):
            return 4 * p[0] + 2 * p[1] + p[2]

        def copy(a, k, block, to, src=None):
            dst = outs[a].at[slot(block)]
            return pltpu.make_async_remote_copy(
                src_ref=dst if src is None else src, dst_ref=dst,
                send_sem=send_sems.at[a * 7 + k], recv_sem=recv_sems.at[a * 7 + k],
                device_id=to, device_id_type=MESH)

        mine = [pltpu.make_async_copy(ins[a], outs[a].at[slot(me)], local_sems.at[a]) for a in range(n)]
        for m in mine:
            m.start()
        first = []
        for a in range(n):
            first.append(copy(a, 0, me, sibling, src=ins[a]))
            first += [copy(a, 1 + j, me, (*chip, c), src=ins[a]) for j, chip in enumerate(chips)]
        for cp in first:
            cp.start()
        passed = []
        for j, chip in enumerate(chips):
            for a in range(n):
                copy(a, 1 + j, (*chip, c), me).wait_recv()
                fw = copy(a, 4 + j, (*chip, c), sibling)
                fw.start()
                passed.append(fw)
        for a in range(n):
            copy(a, 0, sibling, me).wait_recv()
            for j, chip in enumerate(chips):
                copy(a, 4 + j, (*chip, 1 - c), me).wait_recv()
        for cp in first + passed:
            cp.wait_send()
        for m in mine:
            m.wait()

    return pl.pallas_call(
        body, name=name,
        in_specs=[ANY] * n, out_specs=[ANY] * n,
        out_shape=[SDS((NDEV,) + a.shape, a.dtype) for a in arrs],
        scratch_shapes=[pltpu.SemaphoreType.DMA((7 * n,)), pltpu.SemaphoreType.DMA((7 * n,)),
                        pltpu.SemaphoreType.DMA((n,))],
    )(*arrs)


HBM = pl.BlockSpec(memory_space=pltpu.HBM)
SEM = pl.BlockSpec(memory_space=pltpu.SEMAPHORE)
EFFECT = pltpu.SideEffectType.DATAFLOW_SIDE_EFFECTING


def _peers7(x, y, c):
    return [(1 - x if fx else x, 1 - y if fy else y, 1 - c if fc else c)
            for fx in (0, 1) for fy in (0, 1) for fc in (0, 1) if fx or fy or fc]


def _slot(p):
    return 4 * p[0] + 2 * p[1] + p[2]


def _split_copies(src_refs, land_refs, send_sems, recv_sems, gather, mine):
    x, y, c = _coords()
    me = (x, y, c)
    out = []
    for a, (src, land) in enumerate(zip(src_refs, land_refs)):
        for k, p in enumerate(_peers7(x, y, c)):
            s = src if gather else src.at[_slot(p)]
            out.append(pltpu.make_async_remote_copy(
                src_ref=s, dst_ref=land.at[_slot(me) if mine else _slot(p)],
                send_sem=send_sems.at[a * 7 + k], recv_sem=recv_sems.at[a * 7 + k],
                device_id=p, device_id_type=MESH))
    return out


def _own_slab(shard):
    x, y, c = _coords()
    z = lax.empty((NDEV,) + shard.shape, shard.dtype)
    return lax.dynamic_update_slice(z, shard[None], (_slot((x, y, c)),) + (0,) * shard.ndim)


def _split_start(name, srcs, lands, gather):
    n = len(srcs)

    def body(*refs):
        src_refs, land_refs = refs[:n], refs[n:2 * n]
        send_sems, recv_sems = refs[2 * n], refs[2 * n + 1]
        token = refs[-1]
        for cp in _split_copies(src_refs, land_refs, send_sems, recv_sems, gather, True):
            cp.start()
        token[...] = jnp.zeros_like(token)

    thru = [pltpu.HBM(a.shape, a.dtype) for a in list(srcs) + list(lands)]
    res = pl.pallas_call(
        body, name=name,
        out_shape=(pltpu.SemaphoreType.DMA((7 * n,)), pltpu.SemaphoreType.DMA((7 * n,)), *thru, SDS((8, 128), F32)),
        in_specs=[HBM] * (2 * n),
        out_specs=(SEM, SEM, *([HBM] * (2 * n)), pl.BlockSpec(memory_space=pltpu.VMEM)),
        input_output_aliases={i: 2 + i for i in range(2 * n)},
        compiler_params=pltpu.CompilerParams(has_side_effects=EFFECT),
    )(*[pltpu.with_memory_space_constraint(a, pltpu.HBM) for a in list(srcs) + list(lands)])
    return res[0], res[1], list(res[2:2 + n]), list(res[2 + n:2 + 2 * n]), res[-1]


def _split_wait(name, started, after, gather):
    send_sems, recv_sems, srcs, lands, _ = started
    n = len(srcs)

    def body(*refs):
        src_refs, land_refs = refs[:n], refs[n:2 * n]
        s_sems, r_sems = refs[2 * n], refs[2 * n + 1]
        for cp in _split_copies(src_refs, land_refs, s_sems, r_sems, gather, False):
            cp.wait_send()
            cp.wait_recv()

    thru = [pltpu.HBM(a.shape, a.dtype) for a in list(srcs) + list(lands)]
    res = pl.pallas_call(
        body, name=name, out_shape=tuple(thru),
        in_specs=[HBM] * (2 * n) + [SEM, SEM, ANY], out_specs=tuple([HBM] * (2 * n)),
        input_output_aliases={i: i for i in range(2 * n)},
        compiler_params=pltpu.CompilerParams(has_side_effects=EFFECT),
    )(*srcs, *lands, send_sems, recv_sems, after)
    return list(res[n:])


def _adam_math(w, g, m, v):
    m = ADAM_B1 * m + (1.0 - ADAM_B1) * g
    v = ADAM_B2 * v + (1.0 - ADAM_B2) * (g * g)
    m_hat = m / (1.0 - ADAM_B1 ** ADAM_STEP)
    v_hat = v / (1.0 - ADAM_B2 ** ADAM_STEP)
    delta = -ADAM_LR * (m_hat / (jnp.sqrt(v_hat) + ADAM_EPS) + ADAM_WD * w)
    return delta, m, v


def _adam_sharded(name, recv, sub, w, m, v):
    R, Cc = w.shape
    tr = max(t for t in range(16, R + 1, 16) if R % t == 0 and t * Cc <= 256 * 1024)

    def body(*refs):
        parts = refs[:NDEV]
        w_ref, m_ref, v_ref, g_out, d_out, m_out, v_out = refs[NDEV:]
        g = parts[0][...].astype(F32)
        for p in parts[1:]:
            g = g + p[...].astype(F32)
        delta, mn, vn = _adam_math(w_ref[...], g, m_ref[...], v_ref[...])
        g_out[...] = g
        d_out[...] = delta
        m_out[...] = mn
        v_out[...] = vn

    if sub is None:
        pspecs = [Block((None, tr, Cc), functools.partial(lambda s, i: (s, i, 0), s)) for s in range(NDEV)]
    else:
        pspecs = [Block((None, None, tr, Cc), functools.partial(lambda s, i: (s, sub, i, 0), s)) for s in range(NDEV)]
    row = Block((tr, Cc), lambda i: (i, 0))
    o = SDS((R, Cc), F32)
    return pl.pallas_call(
        body, name=name, grid=(R // tr,),
        in_specs=pspecs + [row, row, row], out_specs=[row, row, row, row], out_shape=[o, o, o, o],
        compiler_params=_cparams(("arbitrary",), VMEM_BIG))(*([recv] * NDEV), w, m, v)


def _adam_small(parts, w, m, v):
    R = w.shape[0]

    def body(p_ref, w_ref, m_ref, v_ref, g_out, d_out, m_out, v_out):
        g = p_ref[0]
        for s in range(1, NDEV):
            g = g + p_ref[s]
        delta, mn, vn = _adam_math(w_ref[...], g, m_ref[...], v_ref[...])
        g_out[...] = g
        d_out[...] = delta
        m_out[...] = mn
        v_out[...] = vn

    o = SDS((R, 128), F32)
    return pl.pallas_call(body, name="adam_small", out_shape=[o, o, o, o],
                          compiler_params=_cparams(None, VMEM_BIG))(parts, w, m, v)


_SMALL = ["ffn1_norm", "mix_norm", "s5_a_re", "s5_a_im", "s5_log_dt", "s5_b_re", "s5_b_im", "s5_c_re", "s5_c_im",
          "s5_d", "s5_b_glu", "sgu_ln_g", "sgu_ln_b", "sgu_w_s", "sgu_b_s", "b_gate", "ffn2_norm", "final_norm"]
_SHARDED = ["ffn1_w_gate", "ffn1_w_up", "ffn1_w_down", "w_in", "s5_w_glu", "w_branch_a", "w_branch_b", "w_gate",
            "w_out", "ffn2_w_gate", "ffn2_w_up", "ffn2_w_down"]
_ORDER = ["ffn1_norm", "ffn1_w_gate", "ffn1_w_up", "ffn1_w_down", "mix_norm", "w_in", "s5_a_re", "s5_a_im",
          "s5_log_dt", "s5_b_re", "s5_b_im", "s5_c_re", "s5_c_im", "s5_d", "s5_w_glu", "s5_b_glu", "sgu_ln_g",
          "sgu_ln_b", "sgu_w_s", "sgu_b_s", "w_branch_a", "w_branch_b", "w_gate", "b_gate", "w_out", "ffn2_norm",
          "ffn2_w_gate", "ffn2_w_up", "ffn2_w_down", "final_norm"]


def _step(x, tgt, W, M, V):
    T = x.shape[1]
    x0 = x[0]
    tgt0 = tgt[0]
    bf = lambda a: a.astype(BF16)

    def gather_start(name, shards):
        return _split_start(name, shards, [_own_slab(s) for s in shards], True)

    (wgu1,) = _all_gather("gather1", [jnp.stack([bf(W["ffn1_w_gate"][0]), bf(W["ffn1_w_up"][0])])])

    lr_, li_ = W["s5_a_re"][0], W["s5_a_im"][0]
    ldt_ = W["s5_log_dt"][0][:, None]
    brt = jnp.transpose(W["s5_b_re"][0], (2, 0, 1))
    bit = jnp.transpose(W["s5_b_im"][0], (2, 0, 1))
    abr, abi, pwr, pwi, bkr_t, bki_t = _s5_prep(lr_, li_, ldt_, brt, bit)
    bdbr, bdbi = bf(_bd_b(bkr_t)), bf(_bd_b(bki_t))
    bdcr, bdci = bf(_bd_c(W["s5_c_re"][0])), bf(_bd_c(-W["s5_c_im"][0]))
    flat = lambda a: a.reshape(1, S5_NS)
    s5a = (_perm_matrix(), _perm_matrix().T, bdbr, bdbi, bdcr, bdci, flat(abr), flat(abi),
           pwr.reshape(S5_SEG, S5_NS), pwi.reshape(S5_SEG, S5_NS),
           W["s5_d"][0].reshape(1, S5_WIDTH))
    blk = jnp.arange(MLP_CHUNK) // CHUNK
    mask = blk[:, None] >= blk[None, :]
    wsm = jnp.where(mask[None], W["sgu_w_s"][0], 0.0)
    wsm_b, wsmt_b = bf(wsm), bf(jnp.transpose(wsm, (0, 2, 1)))
    bst = jnp.broadcast_to(W["sgu_b_s"][0][:, :, None], (SGU_HEADS, MLP_CHUNK, 128))
    bgate2 = W["b_gate"].reshape(2, 1, D_MODEL)

    h1 = _rms_fwd("rms1", x0, W["ffn1_norm"])
    dep = (wgu1[0, 0, :1, :1] * 0).astype(BF16)

    def later(a):
        return bf(a) + dep[0]

    gs2 = gather_start("gather2_start", [later(W["ffn1_w_down"][0])])
    ab1, f1 = _ffn_up("ffn1_up", h1, wgu1, gs2[4])
    (wd1,) = _split_wait("gather2_wait", gs2, f1, True)
    dep = (wd1[0, :1, :1] * 0).astype(BF16)
    gs3 = gather_start("gather3_start", [later(W["w_in"][0]), later(W["s5_w_glu"][0])])
    x1 = _ffn_down("ffn1_down", f1, wd1, x0, after=gs3[4])
    h2 = _rms_fwd("rms2", x1, W["mix_norm"])
    win, wglu = _split_wait("gather3_wait", gs3, h2, True)
    wglu = wglu.reshape(S5_WIDTH, S5_WIDTH)
    s5c = s5a + (wglu, W["s5_b_glu"])
    dep = (win[0, :1, :1] * 0).astype(BF16)
    gs4 = gather_start("gather4_start", [later(W["w_gate"][0]), later(W["w_branch_a"][0]),
                                         later(W["w_branch_b"][0]), later(W["w_out"][0])])
    proj = _col_fwd("w_in", h2, win, after=gs4[4])
    ya, xs, ypre = _s5_fwd(proj, *s5c)
    dep = (ya[:1, :1] * 0).astype(BF16)
    gs5 = gather_start("gather5_start", [jnp.stack([later(W["ffn2_w_gate"][0]), later(W["ffn2_w_up"][0])])])
    yb = _sgu_fwd("sgu_fwd", proj, W["sgu_ln_g"] + gs5[4][:1, :1], W["sgu_ln_b"], wsm_b, bst)
    wgate, wba, wbb, wout = _split_wait("gather4_wait", gs4, yb, True)
    wout = wout.reshape(D_MODEL, D_MODEL)
    pa = _col_fwd("branch_a", ya, wba)
    pb = _col_fwd("branch_b", yb, wbb)
    gl = _gate_fwd("gate", h2, wgate, bgate2)
    merged = _merge_fwd("merge", pa, pb, gl)
    x2 = _plain_fwd_res("w_out", merged, wout, x1)
    h3 = _rms_fwd("rms3", x2, W["ffn2_norm"])
    (wgu2,) = _split_wait("gather5_wait", gs5, h3, True)
    dep = (wgu2[0, 0, :1, :1] * 0).astype(BF16)
    gs6 = gather_start("gather6_start", [later(W["ffn2_w_down"][0])])
    ab2, f2 = _ffn_up("ffn2_up", h3, wgu2, gs6[4])
    (wd2,) = _split_wait("gather6_wait", gs6, f2, True)
    x3 = _ffn_down("ffn2_down", f2, wd2, x2)
    loss_p, dx3, dx3b, dgf = _loss_head("loss_head", x3, W["final_norm"].reshape(1, D_MODEL), tgt0)

    def exchange_start(name, grads):
        x_, y_, c_ = _coords()
        me = _slot((x_, y_, c_))
        return _split_start(name, grads, [_own_slab(lax.dynamic_index_in_dim(g, me, 0, keepdims=False))
                                          for g in grads], False)

    dab2 = _ffn_down_bwd_act("ffn2_down_bwd_a", dx3b, wd2, ab2)
    g_wd2 = _ffn_down_bwd_w("ffn2_down_bwd_w", f2, dx3b)
    g_gu2 = _ffn_up_bwd_w("ffn2_up_bwd_w", h3, dab2)
    es1 = exchange_start("exchange1_start", [g_wd2, g_gu2])
    dh3 = _ffn_up_bwd_h("ffn2_up_bwd_h", dab2, wgu2, es1[4])
    dx2, dx2b, dg3 = _rms_bwd("rms3_bwd", dh3, x2, W["ffn2_norm"], dx3)

    dmerged = _plain_bwd_a("w_out_bwd_a", dx2b, wout)
    g_wout = _plain_bwd_w("w_out_bwd_w", merged, dx2b)
    dpa, dpb, dgl, dbgate = _merge_bwd("merge_bwd", dmerged, pa, pb, gl)
    dya = _col_bwd_a("branch_a_bwd_a", dpa, wba)
    g_wba = _col_bwd_w("branch_a_bwd_w", ya, dpa, 256)
    dyb = _col_bwd_a("branch_b_bwd_a", dpb, wbb)
    g_wbb = _col_bwd_w("branch_b_bwd_w", yb, dpb, 256)
    dh2g = _gate_bwd_a("gate_bwd_a", dgl, wgate)
    g_wgate = _gate_bwd_w("gate_bwd_w", h2, dgl, 512)
    duv, dws, dbst, dlng, dlnb = _sgu_bwd("sgu_bwd", dyb, proj, W["sgu_ln_g"], W["sgu_ln_b"], wsm_b, wsmt_b, bst)
    (dua, dar8, dai8, ddv, dbglu, dbdbr, dbdbi, dbdcr, dbdci, g_wglu) = _s5_bwd(dya, proj, ypre, xs, *s5c)
    dproj = jnp.concatenate([dua, duv], axis=1)
    g_win = _col_bwd_w("w_in_bwd_w", h2, dproj, 384)
    g_wout3 = g_wout.reshape(NDEV, D_MODEL // NDEV, D_MODEL)
    g_wglu3 = g_wglu.astype(BF16).reshape(NDEV, S5_WIDTH // NDEV, S5_WIDTH)
    es2 = exchange_start("exchange2_start", [g_wout3, g_wba, g_wbb, g_wgate, g_wglu3, g_win])
    dh2 = _col_bwd_a("w_in_bwd_a", dproj, win, add=dh2g)
    dx1, dx1b, dgm = _rms_bwd("rms2_bwd", dh2, x1, W["mix_norm"] + es2[4][:1, :1], dx2)

    dab1 = _ffn_down_bwd_act("ffn1_down_bwd_a", dx1b, wd1, ab1)
    g_gu1 = _ffn_up_bwd_w("ffn1_up_bwd_w", h1, dab1)
    es3 = exchange_start("exchange3_start", [g_gu1])
    g_wd1 = _ffn_down_bwd_w("ffn1_down_bwd_w", f1, dx1b, after=es3[4])
    es4 = exchange_start("exchange4_start", [g_wd1])
    dh1 = _ffn_up_bwd_h("ffn1_up_bwd_h", dab1, wgu1, es4[4])
    dx0, _, dg1 = _rms_bwd("rms1_bwd", dh1, x0, W["ffn1_norm"], dx1)

    dabr = jnp.sum(dar8, axis=0).reshape(S5_GROUPS, S5_STATE)
    dabi = jnp.sum(dai8, axis=0).reshape(S5_GROUPS, S5_STATE)
    d_lr, d_li, d_ldt, d_brt, d_bit = _s5_prep_bwd(lr_, li_, ldt_, brt, bit, dabr, dabi,
                                                   _bd_b_extract(dbdbr), _bd_b_extract(dbdbi))
    small_g = {
        "ffn1_norm": dg1, "mix_norm": dgm, "ffn2_norm": dg3, "final_norm": dgf,
        "s5_a_re": d_lr, "s5_a_im": d_li, "s5_log_dt": d_ldt,
        "s5_b_re": jnp.transpose(d_brt, (1, 2, 0)), "s5_b_im": jnp.transpose(d_bit, (1, 2, 0)),
        "s5_c_re": _bd_c_extract(dbdcr), "s5_c_im": -_bd_c_extract(dbdci),
        "s5_d": ddv, "s5_b_glu": dbglu, "sgu_ln_g": dlng, "sgu_ln_b": dlnb,
        "sgu_w_s": jnp.where(mask[None], dws, 0.0), "sgu_b_s": dbst[:, :, 0], "b_gate": dbgate,
    }

    sizes = [W[n].size for n in _SMALL]
    total = sum(sizes) + 1
    rows = -(-total // 128)
    rows = -(-rows // 8) * 8
    pad = rows * 128 - total

    def pack(d, extra):
        return jnp.concatenate([d[n].reshape(-1).astype(F32) for n in _SMALL] + [extra, jnp.zeros((pad,), F32)]
                               ).reshape(rows, 128)

    G, Dl, Mn, Vn = {}, {}, {}, {}

    def adam(plan):
        last = None
        for n, recv, sub in plan:
            if sub is None:
                g, d, mn, vn = _adam_sharded("adam_" + n, recv, sub, W[n][0], M[n][0], V[n][0])
                G[n], Dl[n], Mn[n], Vn[n] = g[None], d[None], mn[None], vn[None]
            else:
                tr = jnp.transpose
                g, d, mn, vn = _adam_sharded("adam_" + n, recv, sub, tr(W[n][0]), tr(M[n][0]), tr(V[n][0]))
                G[n], Dl[n], Mn[n], Vn[n] = tr(g)[None], tr(d)[None], tr(mn)[None], tr(vn)[None]
            last = g
        return last

    r_wd2, r_gu2 = _split_wait("exchange1_wait", es1, dx0, False)
    done = adam([("ffn2_w_down", r_wd2, None), ("ffn2_w_gate", r_gu2, 0), ("ffn2_w_up", r_gu2, 1)])
    r_wout, r_wba, r_wbb, r_wgate, r_wglu, r_win = _split_wait("exchange2_wait", es2, done, False)
    done = adam([("w_out", r_wout, None), ("w_branch_a", r_wba, None), ("w_branch_b", r_wbb, None),
                 ("w_gate", r_wgate, None), ("s5_w_glu", r_wglu, None), ("w_in", r_win, None)])

    late = loss_p[0, :1] + 0.0 * done.reshape(-1)[:1]
    zero1 = jnp.zeros((1,), F32)
    parts = _all_gather("gather_small_grads", [pack(small_g, late)])[0]
    sg, sd, sm, sv = _adam_small(parts, pack(W, zero1), pack(M, zero1), pack(V, zero1))

    def unpack(flat2d, into):
        flat = flat2d.reshape(-1)
        off = 0
        for n, s in zip(_SMALL, sizes):
            into[n] = flat[off:off + s].reshape(W[n].shape)
            off += s
        return flat[off]

    loss = unpack(sg, G)
    unpack(sd, Dl)
    unpack(sm, Mn)
    unpack(sv, Vn)

    (r_gu1,) = _split_wait("exchange3_wait", es3, sg, False)
    done = adam([("ffn1_w_gate", r_gu1, 0), ("ffn1_w_up", r_gu1, 1)])
    (r_wd1,) = _split_wait("exchange4_wait", es4, done, False)
    adam([("ffn1_w_down", r_wd1, None)])

    return loss, dx0[None], G, Dl, Mn, Vn


def kernel(x, ffn1_norm, ffn1_w_gate, ffn1_w_up, ffn1_w_down, mix_norm, w_in, s5_a_re, s5_a_im, s5_log_dt, s5_b_re, s5_b_im, s5_c_re, s5_c_im, s5_d, s5_w_glu, s5_b_glu, sgu_ln_g, sgu_ln_b, sgu_w_s, sgu_b_s, w_branch_a, w_branch_b, w_gate, b_gate, w_out, ffn2_norm, ffn2_w_gate, ffn2_w_up, ffn2_w_down, final_norm, loss_target, m_ffn1_norm, m_ffn1_w_gate, m_ffn1_w_up, m_ffn1_w_down, m_mix_norm, m_w_in, m_s5_a_re, m_s5_a_im, m_s5_log_dt, m_s5_b_re, m_s5_b_im, m_s5_c_re, m_s5_c_im, m_s5_d, m_s5_w_glu, m_s5_b_glu, m_sgu_ln_g, m_sgu_ln_b, m_sgu_w_s, m_sgu_b_s, m_w_branch_a, m_w_branch_b, m_w_gate, m_b_gate, m_w_out, m_ffn2_norm, m_ffn2_w_gate, m_ffn2_w_up, m_ffn2_w_down, m_final_norm, v_ffn1_norm, v_ffn1_w_gate, v_ffn1_w_up, v_ffn1_w_down, v_mix_norm, v_w_in, v_s5_a_re, v_s5_a_im, v_s5_log_dt, v_s5_b_re, v_s5_b_im, v_s5_c_re, v_s5_c_im, v_s5_d, v_s5_w_glu, v_s5_b_glu, v_sgu_ln_g, v_sgu_ln_b, v_sgu_w_s, v_sgu_b_s, v_w_branch_a, v_w_branch_b, v_w_gate, v_b_gate, v_w_out, v_ffn2_norm, v_ffn2_w_gate, v_ffn2_w_up, v_ffn2_w_down, v_final_norm):
    a = locals()
    W = {n: a[n] for n in _ORDER}
    M = {n: a["m_" + n] for n in _ORDER}
    V = {n: a["v_" + n] for n in _ORDER}
    loss, gx, G, Dl, Mn, Vn = _step(x, loss_target, W, M, V)
    return (loss, gx, *[G[n] for n in _ORDER], *[Dl[n] for n in _ORDER], *[Mn[n] for n in _ORDER],
            *[Vn[n] for n in _ORDER])
```

```python
import functools
import math

import jax
import jax.numpy as jnp
from jax import lax
from jax.experimental import pallas as pl
from jax.experimental.pallas import tpu as pltpu

F32 = jnp.float32
BF16 = jnp.bfloat16
NDEV = 8
NORM_EPS = 1e-6
D_MODEL = 2048
D_FF = 5632
FF_SHARD = D_FF // NDEV
S5_WIDTH = 1024
S5_GROUPS = 64
S5_GROUP_WIDTH = 16
S5_STATE = 64
S5_NS = S5_GROUPS * S5_STATE
SGU_WIDTH = 1024
SGU_HEADS = 8
MLP_CHUNK = 128
CHUNK = 64
ADAM_LR, ADAM_B1, ADAM_B2, ADAM_EPS, ADAM_WD, ADAM_STEP = 0.001, 0.9, 0.999, 1e-08, 0.01, 10
S5_TC = 256
S5_SEG = S5_TC // 8
S5_LG = 512
VMEM_BIG = 56 * 1024 * 1024

MESH = pl.DeviceIdType.MESH
SDS = jax.ShapeDtypeStruct
Block = pl.BlockSpec
ANY = pl.BlockSpec(memory_space=pl.ANY)


def _cparams(sem=None, vmem=None):
    return pltpu.CompilerParams(dimension_semantics=sem, vmem_limit_bytes=vmem)


def _const(shape):
    nd = len(shape)
    return pl.BlockSpec(shape, lambda i: (0,) * nd, pipeline_mode=pl.Buffered(1))


def _sigmoid(x):
    return 0.5 * jnp.tanh(0.5 * x) + 0.5


_GELU_C = math.sqrt(2.0 / math.pi)


def _gelu(x):
    return 0.5 * x * (1.0 + jnp.tanh(_GELU_C * (x + 0.044715 * x * x * x)))


def _gelu_grad(x):
    t = jnp.tanh(_GELU_C * (x + 0.044715 * x * x * x))
    return 0.5 * (1.0 + t) + 0.5 * x * (1.0 - t * t) * _GELU_C * (1.0 + 3.0 * 0.044715 * x * x)


NN = (((1,), (0,)), ((), ()))
NT = (((1,), (1,)), ((), ()))
TN = (((0,), (0,)), ((), ()))


def _dot(a, b, dims=NN):
    return lax.dot_general(a, b, dims, preferred_element_type=F32)


def _matmul(name, a, b, extras, *, grid, a_spec, b_spec, extra_specs, out_shapes, out_specs, acc_shape,
            epilogue, dims=NN, nb=None, compute=None, after=None, vmem=VMEM_BIG):
    nk = grid[2]
    if after is not None:
        extras = tuple(extras) + (after,)
        extra_specs = list(extra_specs) + [Block((8, 128), lambda i, j, k: (0, 0))]
    ne, no = len(extras), len(out_shapes)
    nacc = nb or 1
    if compute is None:
        def compute(a_ref, b_ref, q):
            return _dot(a_ref[...], b_ref[q] if nb else b_ref[...], dims)

    def body(*refs):
        a_ref, b_ref = refs[0], refs[1]
        ex = refs[2:2 + ne]
        outs = refs[2 + ne:2 + ne + no]
        if nk == 1:
            epilogue([compute(a_ref, b_ref, q) for q in range(nacc)], ex, outs)
            return
        acc_ref = refs[2 + ne + no]
        k = pl.program_id(2)

        @pl.when(k == 0)
        def _():
            acc_ref[...] = jnp.zeros_like(acc_ref)

        for q in range(nacc):
            acc_ref[q] += compute(a_ref, b_ref, q)

        @pl.when(k == nk - 1)
        def _():
            epilogue([acc_ref[q] for q in range(nacc)], ex, outs)

    scratch = [] if nk == 1 else [pltpu.VMEM((nacc,) + tuple(acc_shape), F32)]
    res = pl.pallas_call(
        body, name=name, grid=grid,
        in_specs=[a_spec, b_spec] + list(extra_specs),
        out_specs=list(out_specs), out_shape=list(out_shapes), scratch_shapes=scratch,
        compiler_params=_cparams(("parallel", "parallel", "arbitrary"), vmem),
    )(a, b, *extras)
    return res


def _store(dtype_outs=None):
    def ep(accs, ex, outs):
        outs[0][...] = accs[0].astype(outs[0].dtype)
    return ep


def _tile(n, t):
    t = min(n, t)
    assert n % t == 0, (n, t)
    return t


def _ksum(kq, dims):
    def compute(a_ref, b_ref, _):
        part = _dot(a_ref[0], b_ref[0], dims)
        for q in range(1, kq):
            part = part + _dot(a_ref[q], b_ref[q], dims)
        return part
    return compute


def _ksum_lanes(kq, ns, dims):
    def compute(a_ref, b_ref, _):
        part = _dot(a_ref[:, 0:ns], b_ref[0], dims)
        for q in range(1, kq):
            part = part + _dot(a_ref[:, q * ns:(q + 1) * ns], b_ref[q], dims)
        return part
    return compute


def _wide_b(g):
    def compute(a_ref, b_ref, _):
        bw = b_ref[0] if g == 1 else jnp.concatenate([b_ref[q] for q in range(g)], axis=1)
        return _dot(a_ref[...], bw, NN)
    return compute


KQ_DOWN = 4
TT_DEEP = 2048


def _ffn_up(name, h, wgu, after=None):
    T, D = h.shape
    tm = _tile(T, 1024)

    def ep(accs, ex, outs):
        a, b = accs
        outs[0][0] = a.astype(BF16)
        outs[0][1] = b.astype(BF16)
        outs[1][...] = (a * _sigmoid(a) * b).astype(BF16)

    return _matmul(
        name, h, wgu, (), after=after, grid=(NDEV, T // tm, 1),
        a_spec=Block((tm, D), lambda j, i, k: (i, 0)),
        b_spec=Block((None, 2, D, FF_SHARD), lambda j, i, k: (j, 0, 0, 0)),
        extra_specs=(),
        out_shapes=[SDS((NDEV, 2, T, FF_SHARD), BF16), SDS((NDEV, T, FF_SHARD), BF16)],
        out_specs=[Block((None, 2, tm, FF_SHARD), lambda j, i, k: (j, 0, i, 0)),
                   Block((None, tm, FF_SHARD), lambda j, i, k: (j, i, 0))],
        acc_shape=(tm, FF_SHARD), dims=NN, nb=2, epilogue=ep)


def _ffn_down(name, f, wd, xres, after=None):
    _, T, _ = f.shape
    tm, tn = _tile(T, 1024), 1024

    def ep(accs, ex, outs):
        outs[0][...] = ex[0][...] + 0.5 * accs[0]

    return _matmul(
        name, f, wd, (xres,), after=after, grid=(T // tm, D_MODEL // tn, NDEV // KQ_DOWN),
        a_spec=Block((KQ_DOWN, tm, FF_SHARD), lambda i, j, k: (k, i, 0)),
        b_spec=Block((KQ_DOWN, FF_SHARD, tn), lambda i, j, k: (k, 0, j)),
        extra_specs=[Block((tm, tn), lambda i, j, k: (i, j))],
        out_shapes=[SDS((T, D_MODEL), F32)],
        out_specs=[Block((tm, tn), lambda i, j, k: (i, j))],
        acc_shape=(tm, tn), compute=_ksum(KQ_DOWN, NN), epilogue=ep)[0]


def _ffn_down_bwd_act(name, dyb, wd, ab):
    T, D = dyb.shape
    tm = _tile(T, 1024)
    def ep(accs, ex, outs):
        df = accs[0]
        a = ex[0][0].astype(F32)
        b = ex[0][1].astype(F32)
        hs = 0.5 * _sigmoid(a)
        outs[0][0] = (df * b * hs * (1.0 + a * (1.0 - 2.0 * hs))).astype(BF16)
        outs[0][1] = (df * a * hs).astype(BF16)

    return _matmul(
        name, dyb, wd, (ab,), grid=(NDEV, T // tm, 1),
        a_spec=Block((tm, D), lambda j, i, k: (i, 0)),
        b_spec=Block((None, FF_SHARD, D), lambda j, i, k: (j, 0, 0)),
        extra_specs=[Block((None, 2, tm, FF_SHARD), lambda j, i, k: (j, 0, i, 0))],
        out_shapes=[SDS((NDEV, 2, T, FF_SHARD), BF16)],
        out_specs=[Block((None, 2, tm, FF_SHARD), lambda j, i, k: (j, 0, i, 0))],
        acc_shape=(tm, FF_SHARD), dims=NT, epilogue=ep)[0]


def _ffn_down_bwd_w(name, f, dyb, after=None):
    _, T, _ = f.shape
    tt, tn = _tile(T, TT_DEEP), 1024

    def ep(accs, ex, outs):
        outs[0][...] = (0.5 * accs[0]).astype(BF16)

    return _matmul(
        name, f, dyb, (), after=after, grid=(NDEV, D_MODEL // tn, T // tt),
        a_spec=Block((None, tt, FF_SHARD), lambda j, n, k: (j, k, 0)),
        b_spec=Block((tt, tn), lambda j, n, k: (k, n)),
        extra_specs=(),
        out_shapes=[SDS((NDEV, FF_SHARD, D_MODEL), BF16)],
        out_specs=[Block((None, FF_SHARD, tn), lambda j, n, k: (j, 0, n))],
        acc_shape=(FF_SHARD, tn), dims=TN, epilogue=ep)[0]


def _ffn_up_bwd_h(name, dab, wgu, after):
    _, _, T, _ = dab.shape
    tm = _tile(T, 1024)
    return _matmul(
        name, dab, wgu, (), after=after, grid=(T // tm, 1, NDEV),
        a_spec=Block((None, 2, tm, FF_SHARD), lambda i, j, k: (k, 0, i, 0)),
        b_spec=Block((None, 2, D_MODEL, FF_SHARD), lambda i, j, k: (k, 0, 0, 0)),
        extra_specs=(),
        out_shapes=[SDS((T, D_MODEL), BF16)],
        out_specs=[Block((tm, D_MODEL), lambda i, j, k: (i, 0))],
        acc_shape=(tm, D_MODEL), compute=_ksum(2, NT), epilogue=_store())[0]


def _ffn_up_bwd_w(name, h, dab):
    T, D = h.shape
    tt, tn = _tile(T, TT_DEEP), 1024

    def ep(accs, ex, outs):
        outs[0][0] = accs[0].astype(BF16)
        outs[0][1] = accs[1].astype(BF16)

    return _matmul(
        name, dab, h, (), grid=(NDEV, D // tn, T // tt),
        a_spec=Block((None, 2, tt, FF_SHARD), lambda j, n, k: (j, 0, k, 0)),
        b_spec=Block((tt, tn), lambda j, n, k: (k, n)),
        extra_specs=(),
        out_shapes=[SDS((NDEV, 2, FF_SHARD, D), BF16)],
        out_specs=[Block((None, 2, FF_SHARD, tn), lambda j, n, k: (j, 0, 0, n))],
        acc_shape=(FF_SHARD, tn), nb=2,
        compute=lambda a_ref, b_ref, q: _dot(a_ref[q], b_ref[...], TN), epilogue=ep)[0]


def _shards_per_step(ns):
    return max(g for g in (1, 2, 4, 8) if g * ns <= 2048)


def _split_lanes(g, ns):
    def ep(accs, ex, outs):
        for q in range(g):
            outs[0][q] = accs[0][:, q * ns:(q + 1) * ns].astype(outs[0].dtype)
    return ep


def _col_fwd(name, a, w, out_dtype=BF16, after=None):
    T, K = a.shape
    ns = w.shape[2]
    g = _shards_per_step(ns)
    tm = _tile(T, 1024)
    return _matmul(
        name, a, w, (), after=after, grid=(NDEV // g, T // tm, 1),
        a_spec=Block((tm, K), lambda j, i, k: (i, 0)),
        b_spec=Block((g, K, ns), lambda j, i, k: (j, 0, 0)),
        extra_specs=(),
        out_shapes=[SDS((T, NDEV * ns), out_dtype)],
        out_specs=[Block((tm, g * ns), lambda j, i, k: (i, j))],
        acc_shape=(tm, g * ns), compute=_wide_b(g), epilogue=_store())[0]


def _col_bwd_a(name, dy, w, add=None):
    T = dy.shape[0]
    _, K, ns = w.shape
    tm, tn = _tile(T, 1024), _tile(K, 1024)

    def ep(accs, ex, outs):
        r = accs[0]
        if add is not None:
            r = r + ex[0][...].astype(F32)
        outs[0][...] = r.astype(BF16)

    extras = () if add is None else (add,)
    return _matmul(
        name, dy, w, extras, grid=(T // tm, K // tn, 1),
        a_spec=Block((tm, NDEV * ns), lambda i, j, k: (i, 0)),
        b_spec=Block((NDEV, tn, ns), lambda i, j, k: (0, j, 0)),
        extra_specs=[Block((tm, tn), lambda i, j, k: (i, j))] * len(extras),
        out_shapes=[SDS((T, K), BF16)],
        out_specs=[Block((tm, tn), lambda i, j, k: (i, j))],
        acc_shape=(tm, tn), compute=_ksum_lanes(NDEV, ns, NT), epilogue=ep)[0]


def _col_bwd_w(name, a, dy, ns):
    T, K = a.shape
    g = _shards_per_step(ns)
    tt, tr = _tile(T, TT_DEEP), _tile(K, 1024)
    return _matmul(
        name, a, dy, (), grid=(NDEV // g, K // tr, T // tt),
        a_spec=Block((tt, tr), lambda j, n, k: (k, n)),
        b_spec=Block((tt, g * ns), lambda j, n, k: (k, j)),
        extra_specs=(),
        out_shapes=[SDS((NDEV, K, ns), BF16)],
        out_specs=[Block((g, tr, ns), lambda j, n, k: (j, n, 0))],
        acc_shape=(tr, g * ns), dims=TN, epilogue=_split_lanes(g, ns))[0]


def _gate_fwd(name, h, w, bias):
    T, K = h.shape
    ns = w.shape[2]
    g = 2
    per = D_MODEL // (g * ns)
    tm = _tile(T, 1024)

    def ep(accs, ex, outs):
        outs[0][...] = (accs[0] + ex[0][...]).astype(BF16)

    return _matmul(
        name, h, w, (bias,), grid=(NDEV // g, T // tm, 1),
        a_spec=Block((tm, K), lambda j, i, k: (i, 0)),
        b_spec=Block((g, K, ns), lambda j, i, k: (j, 0, 0)),
        extra_specs=[Block((None, 1, g * ns), lambda j, i, k: (j // per, 0, j % per))],
        out_shapes=[SDS((2, T, D_MODEL), BF16)],
        out_specs=[Block((None, tm, g * ns), lambda j, i, k: (j // per, i, j % per))],
        acc_shape=(tm, g * ns), compute=_wide_b(g), epilogue=ep)[0]


def _gate_bwd_a(name, dgl, w):
    _, T, _ = dgl.shape
    _, K, ns = w.shape
    per = D_MODEL // ns
    tm, tn = _tile(T, 1024), 1024

    def compute(a_ref, b_ref, _):
        part = None
        for q in range(NDEV):
            d = _dot(a_ref[q // per, :, (q % per) * ns:(q % per + 1) * ns], b_ref[q], NT)
            part = d if part is None else part + d
        return part

    return _matmul(
        name, dgl, w, (), grid=(T // tm, K // tn, 1),
        a_spec=Block((2, tm, D_MODEL), lambda i, j, k: (0, i, 0)),
        b_spec=Block((NDEV, tn, ns), lambda i, j, k: (0, j, 0)),
        extra_specs=(),
        out_shapes=[SDS((T, K), BF16)],
        out_specs=[Block((tm, tn), lambda i, j, k: (i, j))],
        acc_shape=(tm, tn), compute=compute, epilogue=_store())[0]


def _gate_bwd_w(name, h, dgl, ns):
    T, K = h.shape
    g = 2
    per = D_MODEL // (g * ns)
    tt, tr = _tile(T, TT_DEEP), 1024
    return _matmul(
        name, h, dgl, (), grid=(NDEV // g, K // tr, T // tt),
        a_spec=Block((tt, tr), lambda j, n, k: (k, n)),
        b_spec=Block((None, tt, g * ns), lambda j, n, k: (j // per, k, j % per)),
        extra_specs=(),
        out_shapes=[SDS((NDEV, K, ns), BF16)],
        out_specs=[Block((g, tr, ns), lambda j, n, k: (j, n, 0))],
        acc_shape=(tr, g * ns), dims=TN, epilogue=_split_lanes(g, ns))[0]


def _plain_fwd_res(name, a, w, xres):
    T, K = a.shape
    N = w.shape[1]
    tm, tn = _tile(T, 1024), _tile(N, 1024)

    def ep(accs, ex, outs):
        outs[0][...] = ex[0][...] + accs[0]

    return _matmul(
        name, a, w, (xres,), grid=(T // tm, N // tn, 1),
        a_spec=Block((tm, K), lambda i, j, k: (i, 0)),
        b_spec=Block((K, tn), lambda i, j, k: (0, j)),
        extra_specs=[Block((tm, tn), lambda i, j, k: (i, j))],
        out_shapes=[SDS((T, N), F32)],
        out_specs=[Block((tm, tn), lambda i, j, k: (i, j))],
        acc_shape=(tm, tn), dims=NN, nb=None, epilogue=ep)[0]


def _plain_bwd_a(name, dy, w):
    T, N = dy.shape
    K = w.shape[0]
    tm, tn = _tile(T, 1024), _tile(K, 1024)
    return _matmul(
        name, dy, w, (), grid=(T // tm, K // tn, 1),
        a_spec=Block((tm, N), lambda i, j, k: (i, 0)),
        b_spec=Block((tn, N), lambda i, j, k: (j, 0)),
        extra_specs=(),
        out_shapes=[SDS((T, K), BF16)],
        out_specs=[Block((tm, tn), lambda i, j, k: (i, j))],
        acc_shape=(tm, tn), dims=NT, nb=None, epilogue=_store())[0]


def _plain_bwd_w(name, a, dy):
    T, K = a.shape
    N = dy.shape[1]
    tt, tr, tn = _tile(T, TT_DEEP), _tile(K, 1024), _tile(N, 1024)
    return _matmul(
        name, a, dy, (), grid=(K // tr, N // tn, T // tt),
        a_spec=Block((tt, tr), lambda m, n, k: (k, m)),
        b_spec=Block((tt, tn), lambda m, n, k: (k, n)),
        extra_specs=(),
        out_shapes=[SDS((K, N), BF16)],
        out_specs=[Block((tr, tn), lambda m, n, k: (m, n))],
        acc_shape=(tr, tn), dims=TN, nb=None, epilogue=_store())[0]


def _rms_fwd(name, x, g):
    T, D = x.shape
    tm = _tile(T, 512)

    def body(x_ref, g_ref, h_ref):
        xv = x_ref[...]
        r = lax.rsqrt(jnp.mean(xv * xv, axis=-1, keepdims=True) + NORM_EPS)
        h_ref[...] = (xv * r * g_ref[...]).astype(BF16)

    return pl.pallas_call(
        body, name=name, grid=(T // tm,),
        in_specs=[Block((tm, D), lambda i: (i, 0)), Block((1, D), lambda i: (0, 0))],
        out_specs=Block((tm, D), lambda i: (i, 0)), out_shape=SDS((T, D), BF16),
        compiler_params=_cparams(("arbitrary",), VMEM_BIG))(x, g)


def _rms_bwd(name, dh, x, g, dxin, out_dtype):
    T, D = x.shape
    tm = _tile(T, 512)

    def body(dh_ref, x_ref, g_ref, dxin_ref, dx_ref, dg_ref):
        i = pl.program_id(0)
        xv = x_ref[...]
        dh = dh_ref[...].astype(F32)
        r = lax.rsqrt(jnp.mean(xv * xv, axis=-1, keepdims=True) + NORM_EPS)
        xh = xv * r
        gd = dh * g_ref[...]
        dx = dxin_ref[...].astype(F32) + r * (gd - xh * jnp.mean(gd * xh, axis=-1, keepdims=True))
        dx_ref[...] = dx.astype(out_dtype)
        dgp = jnp.sum(dh * xh, axis=0, keepdims=True)

        @pl.when(i == 0)
        def _():
            dg_ref[...] = dgp

        @pl.when(i > 0)
        def _():
            dg_ref[...] += dgp

    row = Block((tm, D), lambda i: (i, 0))
    vec = Block((1, D), lambda i: (0, 0))
    return pl.pallas_call(
        body, name=name, grid=(T // tm,),
        in_specs=[row, row, vec, row], out_specs=[row, vec],
        out_shape=[SDS((T, D), out_dtype), SDS((1, D), F32)],
        compiler_params=_cparams(("arbitrary",), VMEM_BIG))(dh, x, g, dxin)


def _loss_head(name, x, g, tgt):
    T, D = x.shape
    tm = _tile(T, 512)

    def body(x_ref, g_ref, t_ref, loss_ref, dxb_ref, dg_ref):
        i = pl.program_id(0)
        xv = x_ref[...]
        gv = g_ref[...]
        r = lax.rsqrt(jnp.mean(xv * xv, axis=-1, keepdims=True) + NORM_EPS)
        xh = xv * r
        err = xh * gv - t_ref[...]
        lp = 0.5 * jnp.sum(jnp.mean(err * err, axis=-1, keepdims=True), axis=0, keepdims=True)
        dout = err * (1.0 / D)
        gd = dout * gv
        dx = r * (gd - xh * jnp.mean(gd * xh, axis=-1, keepdims=True))
        dxb_ref[...] = dx.astype(BF16)
        dgp = jnp.sum(dout * xh, axis=0, keepdims=True)
        lpb = jnp.broadcast_to(lp, (1, 128))

        @pl.when(i == 0)
        def _():
            dg_ref[...] = dgp
            loss_ref[...] = lpb

        @pl.when(i > 0)
        def _():
            dg_ref[...] += dgp
            loss_ref[...] += lpb

    row = Block((tm, D), lambda i: (i, 0))
    vec = Block((1, D), lambda i: (0, 0))
    return pl.pallas_call(
        body, name=name, grid=(T // tm,),
        in_specs=[row, vec, row], out_specs=[Block((1, 128), lambda i: (0, 0)), row, vec],
        out_shape=[SDS((1, 128), F32), SDS((T, D), BF16), SDS((1, D), F32)],
        compiler_params=_cparams(("arbitrary",), VMEM_BIG))(x, g, tgt)


def _merge_fwd(name, pa, pb, gl):
    T, D = pa.shape
    tm = _tile(T, 512)

    def body(pa_ref, pb_ref, gl_ref, o_ref):
        ga = _sigmoid(gl_ref[0].astype(F32))
        gb = _sigmoid(gl_ref[1].astype(F32))
        o_ref[...] = (ga * pa_ref[...].astype(F32) + gb * pb_ref[...].astype(F32)).astype(BF16)

    row = Block((tm, D), lambda i: (i, 0))
    return pl.pallas_call(
        body, name=name, grid=(T // tm,),
        in_specs=[row, row, Block((2, tm, D), lambda i: (0, i, 0))], out_specs=row,
        out_shape=SDS((T, D), BF16), compiler_params=_cparams(("arbitrary",), VMEM_BIG))(pa, pb, gl)


def _merge_bwd(name, dm, pa, pb, gl):
    T, D = pa.shape
    tm = _tile(T, 512)

    def body(dm_ref, pa_ref, pb_ref, gl_ref, dpa_ref, dpb_ref, dgl_ref, db_ref):
        i = pl.program_id(0)
        dmv = dm_ref[...].astype(F32)
        ga = _sigmoid(gl_ref[0].astype(F32))
        gb = _sigmoid(gl_ref[1].astype(F32))
        dpa_ref[...] = (dmv * ga).astype(BF16)
        dpb_ref[...] = (dmv * gb).astype(BF16)
        dga = dmv * pa_ref[...].astype(F32) * ga * (1.0 - ga)
        dgb = dmv * pb_ref[...].astype(F32) * gb * (1.0 - gb)
        dgl_ref[0] = dga.astype(BF16)
        dgl_ref[1] = dgb.astype(BF16)
        sa = jnp.sum(dga, axis=0, keepdims=True)
        sb = jnp.sum(dgb, axis=0, keepdims=True)

        @pl.when(i == 0)
        def _():
            db_ref[0] = sa
            db_ref[1] = sb

        @pl.when(i > 0)
        def _():
            db_ref[0] += sa
            db_ref[1] += sb

    row = Block((tm, D), lambda i: (i, 0))
    two = Block((2, tm, D), lambda i: (0, i, 0))
    return pl.pallas_call(
        body, name=name, grid=(T // tm,),
        in_specs=[row, row, row, two], out_specs=[row, row, two, Block((2, 1, D), lambda i: (0, 0, 0))],
        out_shape=[SDS((T, D), BF16), SDS((T, D), BF16), SDS((2, T, D), BF16), SDS((2, 1, D), F32)],
        compiler_params=_cparams(("arbitrary",), VMEM_BIG))(dm, pa, pb, gl)


def _sgu_core(ur, vr, lng, lnb, ws_ref, bs_ref):
    tm = ur.shape[0]
    gu = _gelu(ur)
    gv = _gelu(vr)
    mu = jnp.mean(gv, axis=-1, keepdims=True)
    cen = gv - mu
    rstd = lax.rsqrt(jnp.mean(cen * cen, axis=-1, keepdims=True) + NORM_EPS)
    xhat = cen * rstd
    vn = (xhat * lng + lnb).astype(BF16)
    rows = []
    for n in range(tm // MLP_CHUNK):
        cols = []
        for h in range(SGU_HEADS):
            blk = vn[n * MLP_CHUNK:(n + 1) * MLP_CHUNK, h * 128:(h + 1) * 128]
            cols.append(_dot(ws_ref[h], blk) + bs_ref[h])
        rows.append(jnp.concatenate(cols, axis=1))
    mixed = jnp.concatenate(rows, axis=0) if len(rows) > 1 else rows[0]
    return gu, xhat, rstd, vn, mixed


def _sgu_fwd(name, proj, lng, lnb, wsm, bst):
    T = proj.shape[0]
    W = SGU_WIDTH
    tm = _tile(T, 512)

    def body(u_ref, v_ref, lng_ref, lnb_ref, ws_ref, bs_ref, o_ref):
        gu, _, _, _, mixed = _sgu_core(u_ref[...].astype(F32), v_ref[...].astype(F32), lng_ref[...], lnb_ref[...],
                                       ws_ref, bs_ref)
        o_ref[...] = (gu * mixed).astype(BF16)

    vec = Block((1, W), lambda i: (0, 0))
    return pl.pallas_call(
        body, name=name, grid=(T // tm,),
        in_specs=[Block((tm, W), lambda i: (i, 1)), Block((tm, W), lambda i: (i, 2)), vec, vec,
                  Block((SGU_HEADS, 128, 128), lambda i: (0, 0, 0)), Block((SGU_HEADS, 128, 128), lambda i: (0, 0, 0))],
        out_specs=Block((tm, W), lambda i: (i, 0)), out_shape=SDS((T, W), BF16),
        compiler_params=_cparams(("arbitrary",), VMEM_BIG))(proj, proj, lng, lnb, wsm, bst)


def _sgu_bwd(name, dyb, proj, lng, lnb, wsm, wsmt, bst):
    T = proj.shape[0]
    W = SGU_WIDTH
    tm = _tile(T, 512)

    def body(dy_ref, u_ref, v_ref, lng_ref, lnb_ref, ws_ref, wst_ref, bs_ref,
             duv_ref, dws_ref, dbs_ref, dlng_ref, dlnb_ref):
        i = pl.program_id(0)
        ur = u_ref[...].astype(F32)
        vr = v_ref[...].astype(F32)
        lng_v = lng_ref[...]
        gu, xhat, rstd, vn, mixed = _sgu_core(ur, vr, lng_v, lnb_ref[...], ws_ref, bs_ref)
        dy = dy_ref[...].astype(F32)
        dgu = dy * mixed
        dmix = dy * gu
        dmb = dmix.astype(BF16)
        dws_p, dbs_p, rows = [], [], []
        for h in range(SGU_HEADS):
            acc_w = jnp.zeros((128, 128), F32)
            acc_b = jnp.zeros((128, 1), F32)
            for n in range(tm // MLP_CHUNK):
                r0 = n * MLP_CHUNK
                dmt = dmb[r0:r0 + MLP_CHUNK, h * 128:(h + 1) * 128]
                acc_w = acc_w + _dot(dmt, vn[r0:r0 + MLP_CHUNK, h * 128:(h + 1) * 128], NT)
                acc_b = acc_b + jnp.sum(dmix[r0:r0 + MLP_CHUNK, h * 128:(h + 1) * 128], axis=1, keepdims=True)
            dws_p.append(acc_w)
            dbs_p.append(jnp.broadcast_to(acc_b, (128, 128)))
        for n in range(tm // MLP_CHUNK):
            r0 = n * MLP_CHUNK
            rows.append(jnp.concatenate(
                [_dot(wst_ref[h], dmb[r0:r0 + MLP_CHUNK, h * 128:(h + 1) * 128]) for h in range(SGU_HEADS)], axis=1))
        dvn = jnp.concatenate(rows, axis=0) if len(rows) > 1 else rows[0]
        dlng_p = jnp.sum(dvn * xhat, axis=0, keepdims=True)
        dlnb_p = jnp.sum(dvn, axis=0, keepdims=True)
        dxh = dvn * lng_v
        dgv = rstd * (dxh - jnp.mean(dxh, axis=-1, keepdims=True)
                      - xhat * jnp.mean(dxh * xhat, axis=-1, keepdims=True))
        duv_ref[:, :W] = (dgu * _gelu_grad(ur)).astype(BF16)
        duv_ref[:, W:] = (dgv * _gelu_grad(vr)).astype(BF16)

        @pl.when(i == 0)
        def _():
            for h in range(SGU_HEADS):
                dws_ref[h] = dws_p[h]
                dbs_ref[h] = dbs_p[h]
            dlng_ref[...] = dlng_p
            dlnb_ref[...] = dlnb_p

        @pl.when(i > 0)
        def _():
            for h in range(SGU_HEADS):
                dws_ref[h] += dws_p[h]
                dbs_ref[h] += dbs_p[h]
            dlng_ref[...] += dlng_p
            dlnb_ref[...] += dlnb_p

    vec = Block((1, W), lambda i: (0, 0))
    wsb = Block((SGU_HEADS, 128, 128), lambda i: (0, 0, 0))
    hsq = SDS((SGU_HEADS, 128, 128), F32)
    return pl.pallas_call(
        body, name=name, grid=(T // tm,),
        in_specs=[Block((tm, W), lambda i: (i, 0)), Block((tm, W), lambda i: (i, 1)), Block((tm, W), lambda i: (i, 2)),
                  vec, vec, wsb, wsb, wsb],
        out_specs=[Block((tm, 2 * W), lambda i: (i, 0)), wsb, wsb, vec, vec],
        out_shape=[SDS((T, 2 * W), BF16), hsq, hsq, SDS((1, W), F32), SDS((1, W), F32)],
        compiler_params=_cparams(("arbitrary",), VMEM_BIG))(dyb, proj, proj, lng, lnb, wsm, wsmt, bst)


def _s5_disc(lr, li, ldt, brt, bit):
    dt = jnp.exp(ldt)
    decay = jnp.exp(lr * dt)
    abr = decay * jnp.cos(li * dt)
    abi = decay * jnp.sin(li * dt)
    denom = lr * lr + li * li
    nr = abr - 1.0
    ni = abi
    kr = (nr * lr + ni * li) / denom
    ki = (ni * lr - nr * li) / denom
    bkr = kr[None] * brt - ki[None] * bit
    bki = kr[None] * bit + ki[None] * brt
    return abr, abi, bkr, bki


def _s5_prep(lr, li, ldt, brt, bit):
    G, P, C = S5_GROUPS, S5_STATE, S5_GROUP_WIDTH

    def body(lr_ref, li_ref, ldt_ref, br_ref, bi_ref, abr_ref, abi_ref, pwr_ref, pwi_ref, bkr_ref, bki_ref):
        lr_, li_, ldt_ = lr_ref[...], li_ref[...], ldt_ref[...]
        res = _s5_disc(lr_, li_, ldt_, br_ref[...], bi_ref[...])
        for o, r in zip((abr_ref, abi_ref, bkr_ref, bki_ref), res):
            o[...] = r
        dt = jnp.exp(ldt_)
        n = lax.broadcasted_iota(jnp.int32, (S5_SEG, G, P), 0).astype(F32) + 1.0
        dec = jnp.exp((lr_ * dt)[None] * n)
        ang = (li_ * dt)[None] * n
        pwr_ref[...] = dec * jnp.cos(ang)
        pwi_ref[...] = dec * jnp.sin(ang)

    gp = SDS((G, P), F32)
    sgp = SDS((S5_SEG, G, P), F32)
    cgp = SDS((C, G, P), F32)
    return pl.pallas_call(body, name="s5_prep", out_shape=[gp, gp, sgp, sgp, cgp, cgp])(lr, li, ldt, brt, bit)


def _s5_prep_bwd(lr, li, ldt, brt, bit, dabr, dabi, dbkr, dbki):
    G, P, C = S5_GROUPS, S5_STATE, S5_GROUP_WIDTH

    def body(lr_ref, li_ref, ldt_ref, br_ref, bi_ref, dabr_ref, dabi_ref, dbkr_ref, dbki_ref,
             o_lr, o_li, o_ldt, o_br, o_bi):
        _, pull = jax.vjp(_s5_disc, lr_ref[...], li_ref[...], ldt_ref[...], br_ref[...], bi_ref[...])
        g = pull((dabr_ref[...], dabi_ref[...], dbkr_ref[...], dbki_ref[...]))
        for o, r in zip((o_lr, o_li, o_ldt, o_br, o_bi), g):
            o[...] = r

    gp = SDS((G, P), F32)
    cgp = SDS((C, G, P), F32)
    return pl.pallas_call(body, name="s5_prep_bwd", out_shape=[gp, gp, SDS((G, 1), F32), cgp, cgp])(
        lr, li, ldt, brt, bit, dabr, dabi, dbkr, dbki)


def _s5_scan(buf_ref, ar_row, ai_row, pwr_ref, pwi_ref, carry_ref, LG, xs_ref=None, dar_ref=None, dai_ref=None):
    reverse = xs_ref is not None
    NS, SEG = S5_NS, S5_SEG
    sgn = -1.0 if reverse else 1.0
    for lg in range(NS // LG):
        cr = slice(lg * LG, (lg + 1) * LG)
        ci = slice(NS + lg * LG, NS + (lg + 1) * LG)
        ar1, ai1 = ar_row[:, cr], sgn * ai_row[:, cr]
        asr1, asi1 = pwr_ref[SEG - 1:SEG, cr], sgn * pwi_ref[SEG - 1:SEG, cr]
        ar = jnp.broadcast_to(ar1, (8, LG))
        ai = jnp.broadcast_to(ai1, (8, LG))

        def step_of(j):
            return (SEG - 1 - j) if reverse else j

        def p1(j, st):
            sr, si = st
            rows = pl.ds(pl.multiple_of(step_of(j) * 8, 8), 8)
            nr = ar * sr - ai * si + buf_ref[rows, cr]
            ni = ar * si + ai * sr + buf_ref[rows, ci]
            buf_ref[rows, cr] = nr
            buf_ref[rows, ci] = ni
            return nr, ni

        z = jnp.zeros((8, LG), F32)
        er, ei = lax.fori_loop(0, SEG, p1, (z, z), unroll=True)
        c_r = carry_ref[:, cr]
        c_i = carry_ref[:, ci]
        cs_r, cs_i = [None] * 8, [None] * 8
        order = range(7, -1, -1) if reverse else range(8)
        for s in order:
            cs_r[s], cs_i[s] = c_r, c_i
            e_r, e_i = er[s:s + 1], ei[s:s + 1]
            c_r, c_i = e_r + asr1 * c_r - asi1 * c_i, e_i + asr1 * c_i + asi1 * c_r
        carry_ref[:, cr] = c_r
        carry_ref[:, ci] = c_i
        cmr = jnp.concatenate(cs_r, axis=0)
        cmi = jnp.concatenate(cs_i, axis=0)

        def carried(j):
            pr = pwr_ref[pl.ds(j, 1), cr]
            pi = sgn * pwi_ref[pl.ds(j, 1), cr]
            return pr * cmr - pi * cmi, pr * cmi + pi * cmr

        if not reverse:
            def p2(j, st):
                rows = pl.ds(pl.multiple_of(j * 8, 8), 8)
                wr, wi = carried(j)
                buf_ref[rows, cr] += wr
                buf_ref[rows, ci] += wi
                return st

            lax.fori_loop(0, SEG, p2, 0, unroll=True)
        else:
            def p2(j, st):
                pr, pi, dr, di = st
                rows = pl.ds(pl.multiple_of(step_of(j) * 8, 8), 8)
                xr = xs_ref[rows, cr]
                xi = xs_ref[rows, ci]
                dr = dr + pr * xr + pi * xi
                di = di + pi * xr - pr * xi
                wr, wi = carried(j)
                gr = buf_ref[rows, cr] + wr
                gi = buf_ref[rows, ci] + wi
                buf_ref[rows, cr] = gr
                buf_ref[rows, ci] = gi
                return gr, gi, dr, di

            st = lax.fori_loop(0, SEG, p2, (cmr, cmi, z, z), unroll=True)
            dar_ref[:, cr] += st[2]
            dai_ref[:, cr] += st[3]


def _s5_fwd(proj, perm, permt, bdbr, bdbi, bdcr, bdci, abr, abi, asr, asi, dvec, wglu, bglu):
    T = proj.shape[0]
    TC, NS, W = S5_TC, S5_NS, S5_WIDTH
    nc = T // TC

    def body(u_ref, pm_ref, pmt_ref, bdbr_ref, bdbi_ref, bdcr_ref, bdci_ref, ar_ref, ai_ref, asr_ref, asi_ref,
             d_ref, wglu_ref, bglu_ref, ya_ref, xs_ref, ypre_ref, carry_ref):
        i = pl.program_id(0)

        @pl.when(i == 0)
        def _():
            carry_ref[...] = jnp.zeros_like(carry_ref)

        up = _dot(pm_ref[...], u_ref[...]).astype(BF16)
        for j in range(8):
            ut = up[:, j * 128:(j + 1) * 128]
            xs_ref[:, j * 512:(j + 1) * 512] = _dot(ut, bdbr_ref[j])
            xs_ref[:, NS + j * 512:NS + (j + 1) * 512] = _dot(ut, bdbi_ref[j])
        _s5_scan(xs_ref, ar_ref[...], ai_ref[...], asr_ref, asi_ref, carry_ref, S5_LG)
        ys = []
        for j in range(8):
            xr = xs_ref[:, j * 512:(j + 1) * 512].astype(BF16)
            xi = xs_ref[:, NS + j * 512:NS + (j + 1) * 512].astype(BF16)
            ys.append(_dot(xr, bdcr_ref[j]) + _dot(xi, bdci_ref[j]))
        ypre = jnp.concatenate(ys, axis=1) + d_ref[...] * up.astype(F32)
        ypre_ref[...] = ypre
        ya = _gelu(ypre)
        zl = _dot(ya.astype(BF16), wglu_ref[...]) + bglu_ref[...]
        outp = (ya * _sigmoid(zl)).astype(BF16)
        ya_ref[...] = _dot(pmt_ref[...], outp).astype(BF16)

    return pl.pallas_call(
        body, name="s5_fwd", grid=(nc,),
        in_specs=[Block((TC, W), lambda i: (i, 0)), _const((TC, TC)), _const((TC, TC)),
                  _const((8, 128, 512)), _const((8, 128, 512)), _const((8, 512, 128)), _const((8, 512, 128)),
                  _const((1, NS)), _const((1, NS)), _const((S5_SEG, NS)), _const((S5_SEG, NS)),
                  _const((1, W)), _const((W, W)), _const((1, W))],
        out_specs=[Block((TC, W), lambda i: (i, 0)), Block((TC, 2 * NS), lambda i: (i, 0)),
                   Block((TC, W), lambda i: (i, 0))],
        out_shape=[SDS((T, W), BF16), SDS((T, 2 * NS), F32), SDS((T, W), F32)],
        scratch_shapes=[pltpu.VMEM((1, 2 * NS), F32)],
        compiler_params=_cparams(("arbitrary",), VMEM_BIG),
    )(proj, perm, permt, bdbr, bdbi, bdcr, bdci, abr, abi, asr, asi, dvec, wglu, bglu)


def _s5_bwd(dya, proj, ypre, xs, perm, permt, bdbr, bdbi, bdcr, bdci, abr, abi, asr, asi, dvec, wglu, bglu):
    T = proj.shape[0]
    TC, NS, W = S5_TC, S5_NS, S5_WIDTH
    nc = T // TC

    def body(dya_ref, u_ref, ypre_ref, xs_ref, pm_ref, pmt_ref, bdbr_ref, bdbi_ref, bdcr_ref, bdci_ref,
             ar_ref, ai_ref, asr_ref, asi_ref, d_ref, wglu_ref, bglu_ref,
             du_ref, dar_ref, dai_ref, dd_ref, dbglu_ref, o_dbdbr, o_dbdbi, o_dbdcr, o_dbdci, o_dwglu,
             g_ref, carry_ref, dbdbr_ref, dbdbi_ref, dbdcr_ref, dbdci_ref, dwglu_ref):
        i = pl.program_id(0)

        @pl.when(i == 0)
        def _():
            carry_ref[...] = jnp.zeros_like(carry_ref)
            for r in (dbdbr_ref, dbdbi_ref, dbdcr_ref, dbdci_ref, dar_ref, dai_ref, dd_ref, dwglu_ref, dbglu_ref):
                r[...] = jnp.zeros_like(r)

        pm = pm_ref[...]
        dyo = _dot(pm, dya_ref[...])
        up = _dot(pm, u_ref[...]).astype(BF16)
        upf = up.astype(F32)
        ypre_v = ypre_ref[...]
        ya = _gelu(ypre_v)
        yab = ya.astype(BF16)
        sg = _sigmoid(_dot(yab, wglu_ref[...]) + bglu_ref[...])
        dz = dyo * ya * sg * (1.0 - sg)
        dzb = dz.astype(BF16)
        dya_t = dyo * sg + _dot(dzb, wglu_ref[...], NT)
        dwglu_ref[...] += _dot(yab, dzb, TN)
        dbglu_ref[...] += jnp.sum(dz, axis=0, keepdims=True)
        dy = dya_t * _gelu_grad(ypre_v)
        dd_ref[...] += jnp.sum(dy * upf, axis=0, keepdims=True)
        dyb = dy.astype(BF16)
        for j in range(8):
            dyj = dyb[:, j * 128:(j + 1) * 128]
            g_ref[:, j * 512:(j + 1) * 512] = _dot(dyj, bdcr_ref[j], NT)
            g_ref[:, NS + j * 512:NS + (j + 1) * 512] = _dot(dyj, bdci_ref[j], NT)
            dbdcr_ref[j] += _dot(xs_ref[:, j * 512:(j + 1) * 512].astype(BF16), dyj, TN)
            dbdci_ref[j] += _dot(xs_ref[:, NS + j * 512:NS + (j + 1) * 512].astype(BF16), dyj, TN)
        _s5_scan(g_ref, ar_ref[...], ai_ref[...], asr_ref, asi_ref, carry_ref, S5_LG,
                 xs_ref=xs_ref, dar_ref=dar_ref, dai_ref=dai_ref)
        dus = []
        for j in range(8):
            ut = up[:, j * 128:(j + 1) * 128]
            gr = g_ref[:, j * 512:(j + 1) * 512].astype(BF16)
            gi = g_ref[:, NS + j * 512:NS + (j + 1) * 512].astype(BF16)
            dbdbr_ref[j] += _dot(ut, gr, TN)
            dbdbi_ref[j] += _dot(ut, gi, TN)
            dus.append(_dot(gr, bdbr_ref[j], NT) + _dot(gi, bdbi_ref[j], NT))
        dup = jnp.concatenate(dus, axis=1) + d_ref[...] * dy
        du_ref[...] = _dot(pmt_ref[...], dup.astype(BF16)).astype(BF16)

        @pl.when(i == nc - 1)
        def _():
            for src, dst in ((dbdbr_ref, o_dbdbr), (dbdbi_ref, o_dbdbi), (dbdcr_ref, o_dbdcr),
                             (dbdci_ref, o_dbdci), (dwglu_ref, o_dwglu)):
                pltpu.sync_copy(src, dst)

    c2 = lambda i: (0, 0)
    rev = lambda i: (nc - 1 - i, 0)
    return pl.pallas_call(
        body, name="s5_bwd", grid=(nc,),
        in_specs=[Block((TC, W), rev), Block((TC, W), rev), Block((TC, W), rev), Block((TC, 2 * NS), rev),
                  _const((TC, TC)), _const((TC, TC)),
                  _const((8, 128, 512)), _const((8, 128, 512)), _const((8, 512, 128)), _const((8, 512, 128)),
                  _const((1, NS)), _const((1, NS)), _const((S5_SEG, NS)), _const((S5_SEG, NS)),
                  _const((1, W)), _const((W, W)), _const((1, W))],
        out_specs=[Block((TC, W), rev), Block((8, NS), c2), Block((8, NS), c2), Block((1, W), c2), Block((1, W), c2),
                   ANY, ANY, ANY, ANY, ANY],
        out_shape=[SDS((T, W), BF16), SDS((8, NS), F32), SDS((8, NS), F32), SDS((1, W), F32), SDS((1, W), F32),
                   SDS((8, 128, 512), F32), SDS((8, 128, 512), F32),
                   SDS((8, 512, 128), F32), SDS((8, 512, 128), F32), SDS((W, W), F32)],
        scratch_shapes=[pltpu.VMEM((TC, 2 * NS), F32), pltpu.VMEM((1, 2 * NS), F32),
                        pltpu.VMEM((8, 128, 512), F32), pltpu.VMEM((8, 128, 512), F32),
                        pltpu.VMEM((8, 512, 128), F32), pltpu.VMEM((8, 512, 128), F32), pltpu.VMEM((W, W), F32)],
        compiler_params=_cparams(("arbitrary",), VMEM_BIG),
    )(dya, proj, ypre, xs, perm, permt, bdbr, bdbi, bdcr, bdci, abr, abi, asr, asi, dvec, wglu, bglu)


def _bd_b(bk_t):
    C, P = S5_GROUP_WIDTH, S5_STATE
    t = jnp.transpose(bk_t, (1, 0, 2)).reshape(8, 8, C, P)
    eye = jnp.eye(8, dtype=t.dtype)
    return (t[:, :, :, None, :] * eye[None, :, None, :, None]).reshape(8, 8 * C, 8 * P)


def _bd_b_extract(m):
    C, P = S5_GROUP_WIDTH, S5_STATE
    t = m.reshape(8, 8, C, 8, P)
    d = jnp.stack([t[:, g, :, g, :] for g in range(8)], axis=1)
    return jnp.transpose(d.reshape(S5_GROUPS, C, P), (1, 0, 2))


def _bd_c(c):
    C, P = S5_GROUP_WIDTH, S5_STATE
    t = jnp.transpose(c, (0, 2, 1)).reshape(8, 8, P, C)
    eye = jnp.eye(8, dtype=t.dtype)
    return (t[:, :, :, None, :] * eye[None, :, None, :, None]).reshape(8, 8 * P, 8 * C)


def _bd_c_extract(m):
    C, P = S5_GROUP_WIDTH, S5_STATE
    t = m.reshape(8, 8, P, 8, C)
    d = jnp.stack([t[:, g, :, g, :] for g in range(8)], axis=1)
    return jnp.transpose(d.reshape(S5_GROUPS, P, C), (0, 2, 1))


def _perm_matrix():
    r = jnp.arange(S5_TC)
    src = (r % 8) * S5_SEG + r // 8
    return (src[:, None] == jnp.arange(S5_TC)[None, :]).astype(BF16)


def _coords():
    return lax.axis_index("x"), lax.axis_index("y"), lax.axis_index("c")


def _all_gather(name, arrs):
    n = len(arrs)

    def body(*refs):
        ins, outs = refs[:n], refs[n:2 * n]
        send_sems, recv_sems, local_sems = refs[2 * n:]
        x, y, c = _coords()
        me, sibling = (x, y, c), (x, y, 1 - c)
        chips = [(1 - x, y), (x, 1 - y), (1 - x, 1 - y)]

        def slot(p):
            return 4 * p[0] + 2 * p[1] + p[2]

        def copy(a, k, block, to, src=None):
            dst = outs[a].at[slot(block)]
            return pltpu.make_async_remote_copy(
                src_ref=dst if src is None else src, dst_ref=dst,
                send_sem=send_sems.at[a * 7 + k], recv_sem=recv_sems.at[a * 7 + k],
                device_id=to, device_id_type=MESH)

        mine = [pltpu.make_async_copy(ins[a], outs[a].at[slot(me)], local_sems.at[a]) for a in range(n)]
        for m in mine:
            m.start()
        first = []
        for a in range(n):
            first.append(copy(a, 0, me, sibling, src=ins[a]))
            first += [copy(a, 1 + j, me, (*chip, c), src=ins[a]) for j, chip in enumerate(chips)]
        for cp in first:
            cp.start()
        passed = []
        for j, chip in enumerate(chips):
            for a in range(n):
                copy(a, 1 + j, (*chip, c), me).wait_recv()
                fw = copy(a, 4 + j, (*chip, c), sibling)
                fw.start()
                passed.append(fw)
        for a in range(n):
            copy(a, 0, sibling, me).wait_recv()
            for j, chip in enumerate(chips):
                copy(a, 4 + j, (*chip, 1 - c), me).wait_recv()
        for cp in first + passed:
            cp.wait_send()
        for m in mine:
            m.wait()

    return pl.pallas_call(
        body, name=name,
        in_specs=[ANY] * n, out_specs=[ANY] * n,
        out_shape=[SDS((NDEV,) + a.shape, a.dtype) for a in arrs],
        scratch_shapes=[pltpu.SemaphoreType.DMA((7 * n,)), pltpu.SemaphoreType.DMA((7 * n,)),
                        pltpu.SemaphoreType.DMA((n,))],
    )(*arrs)


HBM = pl.BlockSpec(memory_space=pltpu.HBM)
SEM = pl.BlockSpec(memory_space=pltpu.SEMAPHORE)
EFFECT = pltpu.SideEffectType.DATAFLOW_SIDE_EFFECTING


def _peers7(x, y, c):
    return [(1 - x if fx else x, 1 - y if fy else y, 1 - c if fc else c)
            for fx in (0, 1) for fy in (0, 1) for fc in (0, 1) if fx or fy or fc]


def _slot(p):
    return 4 * p[0] + 2 * p[1] + p[2]


def _split_copies(src_refs, land_refs, send_sems, recv_sems, gather, mine):
    x, y, c = _coords()
    me = (x, y, c)
    out = []
    for a, (src, land) in enumerate(zip(src_refs, land_refs)):
        for k, p in enumerate(_peers7(x, y, c)):
            s = src if gather else src.at[_slot(p)]
            out.append(pltpu.make_async_remote_copy(
                src_ref=s, dst_ref=land.at[_slot(me) if mine else _slot(p)],
                send_sem=send_sems.at[a * 7 + k], recv_sem=recv_sems.at[a * 7 + k],
                device_id=p, device_id_type=MESH))
    return out


def _own_slab(shard):
    x, y, c = _coords()
    z = lax.empty((NDEV,) + shard.shape, shard.dtype)
    return lax.dynamic_update_slice(z, shard[None], (_slot((x, y, c)),) + (0,) * shard.ndim)


def _split_start(name, srcs, lands, gather):
    n = len(srcs)

    def body(*refs):
        src_refs, land_refs = refs[:n], refs[n:2 * n]
        send_sems, recv_sems = refs[2 * n], refs[2 * n + 1]
        token = refs[-1]
        for cp in _split_copies(src_refs, land_refs, send_sems, recv_sems, gather, True):
            cp.start()
        token[...] = jnp.zeros_like(token)

    thru = [pltpu.HBM(a.shape, a.dtype) for a in list(srcs) + list(lands)]
    res = pl.pallas_call(
        body, name=name,
        out_shape=(pltpu.SemaphoreType.DMA((7 * n,)), pltpu.SemaphoreType.DMA((7 * n,)), *thru, SDS((8, 128), F32)),
        in_specs=[HBM] * (2 * n),
        out_specs=(SEM, SEM, *([HBM] * (2 * n)), pl.BlockSpec(memory_space=pltpu.VMEM)),
        input_output_aliases={i: 2 + i for i in range(2 * n)},
        compiler_params=pltpu.CompilerParams(has_side_effects=EFFECT),
    )(*[pltpu.with_memory_space_constraint(a, pltpu.HBM) for a in list(srcs) + list(lands)])
    return res[0], res[1], list(res[2:2 + n]), list(res[2 + n:2 + 2 * n]), res[-1]


def _split_wait(name, started, after, gather):
    send_sems, recv_sems, srcs, lands, _ = started
    n = len(srcs)

    def body(*refs):
        src_refs, land_refs = refs[:n], refs[n:2 * n]
        s_sems, r_sems = refs[2 * n], refs[2 * n + 1]
        for cp in _split_copies(src_refs, land_refs, s_sems, r_sems, gather, False):
            cp.wait_send()
            cp.wait_recv()

    thru = [pltpu.HBM(a.shape, a.dtype) for a in list(srcs) + list(lands)]
    res = pl.pallas_call(
        body, name=name, out_shape=tuple(thru),
        in_specs=[HBM] * (2 * n) + [SEM, SEM, ANY], out_specs=tuple([HBM] * (2 * n)),
        input_output_aliases={i: i for i in range(2 * n)},
        compiler_params=pltpu.CompilerParams(has_side_effects=EFFECT),
    )(*srcs, *lands, send_sems, recv_sems, after)
    return list(res[n:])


def _adam_math(w, g, m, v):
    m = ADAM_B1 * m + (1.0 - ADAM_B1) * g
    v = ADAM_B2 * v + (1.0 - ADAM_B2) * (g * g)
    m_hat = m / (1.0 - ADAM_B1 ** ADAM_STEP)
    v_hat = v / (1.0 - ADAM_B2 ** ADAM_STEP)
    delta = -ADAM_LR * (m_hat / (jnp.sqrt(v_hat) + ADAM_EPS) + ADAM_WD * w)
    return delta, m, v


def _adam_sharded(name, recv, sub, w, m, v):
    R, Cc = w.shape
    tr = max(t for t in range(16, R + 1, 16) if R % t == 0 and t * Cc <= 256 * 1024)

    def body(*refs):
        parts = refs[:NDEV]
        w_ref, m_ref, v_ref, g_out, d_out, m_out, v_out = refs[NDEV:]
        g = parts[0][...].astype(F32)
        for p in parts[1:]:
            g = g + p[...].astype(F32)
        delta, mn, vn = _adam_math(w_ref[...], g, m_ref[...], v_ref[...])
        g_out[...] = g
        d_out[...] = delta
        m_out[...] = mn
        v_out[...] = vn

    if sub is None:
        pspecs = [Block((None, tr, Cc), functools.partial(lambda s, i: (s, i, 0), s)) for s in range(NDEV)]
    else:
        pspecs = [Block((None, None, tr, Cc), functools.partial(lambda s, i: (s, sub, i, 0), s)) for s in range(NDEV)]
    row = Block((tr, Cc), lambda i: (i, 0))
    o = SDS((R, Cc), F32)
    return pl.pallas_call(
        body, name=name, grid=(R // tr,),
        in_specs=pspecs + [row, row, row], out_specs=[row, row, row, row], out_shape=[o, o, o, o],
        compiler_params=_cparams(("arbitrary",), VMEM_BIG))(*([recv] * NDEV), w, m, v)


def _adam_small(parts, w, m, v):
    R = w.shape[0]

    def body(p_ref, w_ref, m_ref, v_ref, g_out, d_out, m_out, v_out):
        g = p_ref[0]
        for s in range(1, NDEV):
            g = g + p_ref[s]
        delta, mn, vn = _adam_math(w_ref[...], g, m_ref[...], v_ref[...])
        g_out[...] = g
        d_out[...] = delta
        m_out[...] = mn
        v_out[...] = vn

    o = SDS((R, 128), F32)
    return pl.pallas_call(body, name="adam_small", out_shape=[o, o, o, o],
                          compiler_params=_cparams(None, VMEM_BIG))(parts, w, m, v)


_SMALL = ["ffn1_norm", "mix_norm", "s5_a_re", "s5_a_im", "s5_log_dt", "s5_b_re", "s5_b_im", "s5_c_re", "s5_c_im",
          "s5_d", "s5_b_glu", "sgu_ln_g", "sgu_ln_b", "sgu_w_s", "sgu_b_s", "b_gate", "ffn2_norm", "final_norm"]
_SHARDED = ["ffn1_w_gate", "ffn1_w_up", "ffn1_w_down", "w_in", "s5_w_glu", "w_branch_a", "w_branch_b", "w_gate",
            "w_out", "ffn2_w_gate", "ffn2_w_up", "ffn2_w_down"]
_ORDER = ["ffn1_norm", "ffn1_w_gate", "ffn1_w_up", "ffn1_w_down", "mix_norm", "w_in", "s5_a_re", "s5_a_im",
          "s5_log_dt", "s5_b_re", "s5_b_im", "s5_c_re", "s5_c_im", "s5_d", "s5_w_glu", "s5_b_glu", "sgu_ln_g",
          "sgu_ln_b", "sgu_w_s", "sgu_b_s", "w_branch_a", "w_branch_b", "w_gate", "b_gate", "w_out", "ffn2_norm",
          "ffn2_w_gate", "ffn2_w_up", "ffn2_w_down", "final_norm"]


def _step(x, tgt, W, M, V):
    T = x.shape[1]
    x0 = x[0]
    tgt0 = tgt[0]
    bf = lambda a: a.astype(BF16)

    def gather_start(name, shards):
        return _split_start(name, shards, [_own_slab(s) for s in shards], True)

    (wgu1,) = _all_gather("gather1", [jnp.stack([bf(W["ffn1_w_gate"][0]), bf(W["ffn1_w_up"][0])])])

    lr_, li_ = W["s5_a_re"][0], W["s5_a_im"][0]
    ldt_ = W["s5_log_dt"][0][:, None]
    brt = jnp.transpose(W["s5_b_re"][0], (2, 0, 1))
    bit = jnp.transpose(W["s5_b_im"][0], (2, 0, 1))
    abr, abi, pwr, pwi, bkr_t, bki_t = _s5_prep(lr_, li_, ldt_, brt, bit)
    bdbr, bdbi = bf(_bd_b(bkr_t)), bf(_bd_b(bki_t))
    bdcr, bdci = bf(_bd_c(W["s5_c_re"][0])), bf(_bd_c(-W["s5_c_im"][0]))
    flat = lambda a: a.reshape(1, S5_NS)
    s5a = (_perm_matrix(), _perm_matrix().T, bdbr, bdbi, bdcr, bdci, flat(abr), flat(abi),
           pwr.reshape(S5_SEG, S5_NS), pwi.reshape(S5_SEG, S5_NS),
           W["s5_d"][0].reshape(1, S5_WIDTH))
    blk = jnp.arange(MLP_CHUNK) // CHUNK
    mask = blk[:, None] >= blk[None, :]
    wsm = jnp.where(mask[None], W["sgu_w_s"][0], 0.0)
    wsm_b, wsmt_b = bf(wsm), bf(jnp.transpose(wsm, (0, 2, 1)))
    bst = jnp.broadcast_to(W["sgu_b_s"][0][:, :, None], (SGU_HEADS, MLP_CHUNK, 128))
    bgate2 = W["b_gate"].reshape(2, 1, D_MODEL)

    h1 = _rms_fwd("rms1", x0, W["ffn1_norm"])
    dep = (wgu1[0, 0, :1, :1] * 0).astype(BF16)

    def later(a):
        return bf(a) + dep[0]

    gs2 = gather_start("gather2_start", [later(W["ffn1_w_down"][0])])
    ab1, f1 = _ffn_up("ffn1_up", h1, wgu1, gs2[4])
    (wd1,) = _split_wait("gather2_wait", gs2, f1, True)
    dep = (wd1[0, :1, :1] * 0).astype(BF16)
    gs3 = gather_start("gather3_start", [later(W["w_in"][0]), later(W["s5_w_glu"][0])])
    x1 = _ffn_down("ffn1_down", f1, wd1, x0, after=gs3[4])
    h2 = _rms_fwd("rms2", x1, W["mix_norm"])
    win, wglu = _split_wait("gather3_wait", gs3, h2, True)
    wglu = wglu.reshape(S5_WIDTH, S5_WIDTH)
    s5c = s5a + (wglu, W["s5_b_glu"])
    dep = (win[0, :1, :1] * 0).astype(BF16)
    gs4 = gather_start("gather4_start", [later(W["w_gate"][0]), later(W["w_branch_a"][0]),
                                         later(W["w_branch_b"][0]), later(W["w_out"][0])])
    proj = _col_fwd("w_in", h2, win, after=gs4[4])
    ya, xs, ypre = _s5_fwd(proj, *s5c)
    dep = (ya[:1, :1] * 0).astype(BF16)
    gs5 = gather_start("gather5_start", [jnp.stack([later(W["ffn2_w_gate"][0]), later(W["ffn2_w_up"][0])])])
    yb = _sgu_fwd("sgu_fwd", proj, W["sgu_ln_g"] + gs5[4][:1, :1], W["sgu_ln_b"], wsm_b, bst)
    wgate, wba, wbb, wout = _split_wait("gather4_wait", gs4, yb, True)
    wout = wout.reshape(D_MODEL, D_MODEL)
    pa = _col_fwd("branch_a", ya, wba)
    pb = _col_fwd("branch_b", yb, wbb)
    gl = _gate_fwd("gate", h2, wgate, bgate2)
    merged = _merge_fwd("merge", pa, pb, gl)
    x2 = _plain_fwd_res("w_out", merged, wout, x1)
    h3 = _rms_fwd("rms3", x2, W["ffn2_norm"])
    (wgu2,) = _split_wait("gather5_wait", gs5, h3, True)
    dep = (wgu2[0, 0, :1, :1] * 0).astype(BF16)
    gs6 = gather_start("gather6_start", [later(W["ffn2_w_down"][0])])
    ab2, f2 = _ffn_up("ffn2_up", h3, wgu2, gs6[4])
    (wd2,) = _split_wait("gather6_wait", gs6, f2, True)
    x3 = _ffn_down("ffn2_down", f2, wd2, x2)
    loss_p, dx3b, dgf = _loss_head("loss_head", x3, W["final_norm"].reshape(1, D_MODEL), tgt0)

    def exchange_start(name, grads):
        x_, y_, c_ = _coords()
        me = _slot((x_, y_, c_))
        return _split_start(name, grads, [_own_slab(lax.dynamic_index_in_dim(g, me, 0, keepdims=False))
                                          for g in grads], False)

    dab2 = _ffn_down_bwd_act("ffn2_down_bwd_a", dx3b, wd2, ab2)
    g_wd2 = _ffn_down_bwd_w("ffn2_down_bwd_w", f2, dx3b)
    g_gu2 = _ffn_up_bwd_w("ffn2_up_bwd_w", h3, dab2)
    es1 = exchange_start("exchange1_start", [g_wd2, g_gu2])
    dh3 = _ffn_up_bwd_h("ffn2_up_bwd_h", dab2, wgu2, es1[4])
    dx2b, dg3 = _rms_bwd("rms3_bwd", dh3, x2, W["ffn2_norm"], dx3b, BF16)

    dmerged = _plain_bwd_a("w_out_bwd_a", dx2b, wout)
    g_wout = _plain_bwd_w("w_out_bwd_w", merged, dx2b)
    dpa, dpb, dgl, dbgate = _merge_bwd("merge_bwd", dmerged, pa, pb, gl)
    dya = _col_bwd_a("branch_a_bwd_a", dpa, wba)
    g_wba = _col_bwd_w("branch_a_bwd_w", ya, dpa, 256)
    dyb = _col_bwd_a("branch_b_bwd_a", dpb, wbb)
    g_wbb = _col_bwd_w("branch_b_bwd_w", yb, dpb, 256)
    dh2g = _gate_bwd_a("gate_bwd_a", dgl, wgate)
    g_wgate = _gate_bwd_w("gate_bwd_w", h2, dgl, 512)
    duv, dws, dbst, dlng, dlnb = _sgu_bwd("sgu_bwd", dyb, proj, W["sgu_ln_g"], W["sgu_ln_b"], wsm_b, wsmt_b, bst)
    (dua, dar8, dai8, ddv, dbglu, dbdbr, dbdbi, dbdcr, dbdci, g_wglu) = _s5_bwd(dya, proj, ypre, xs, *s5c)
    dproj = jnp.concatenate([dua, duv], axis=1)
    g_win = _col_bwd_w("w_in_bwd_w", h2, dproj, 384)
    g_wout3 = g_wout.reshape(NDEV, D_MODEL // NDEV, D_MODEL)
    g_wglu3 = g_wglu.astype(BF16).reshape(NDEV, S5_WIDTH // NDEV, S5_WIDTH)
    es2 = exchange_start("exchange2_start", [g_wout3, g_wba, g_wbb, g_wgate, g_wglu3, g_win])
    dh2 = _col_bwd_a("w_in_bwd_a", dproj, win, add=dh2g)
    dx1b, dgm = _rms_bwd("rms2_bwd", dh2, x1, W["mix_norm"] + es2[4][:1, :1], dx2b, BF16)

    dab1 = _ffn_down_bwd_act("ffn1_down_bwd_a", dx1b, wd1, ab1)
    g_gu1 = _ffn_up_bwd_w("ffn1_up_bwd_w", h1, dab1)
    es3 = exchange_start("exchange3_start", [g_gu1])
    g_wd1 = _ffn_down_bwd_w("ffn1_down_bwd_w", f1, dx1b, after=es3[4])
    es4 = exchange_start("exchange4_start", [g_wd1])
    dh1 = _ffn_up_bwd_h("ffn1_up_bwd_h", dab1, wgu1, es4[4])
    dx0, dg1 = _rms_bwd("rms1_bwd", dh1, x0, W["ffn1_norm"], dx1b, F32)

    dabr = jnp.sum(dar8, axis=0).reshape(S5_GROUPS, S5_STATE)
    dabi = jnp.sum(dai8, axis=0).reshape(S5_GROUPS, S5_STATE)
    d_lr, d_li, d_ldt, d_brt, d_bit = _s5_prep_bwd(lr_, li_, ldt_, brt, bit, dabr, dabi,
                                                   _bd_b_extract(dbdbr), _bd_b_extract(dbdbi))
    small_g = {
        "ffn1_norm": dg1, "mix_norm": dgm, "ffn2_norm": dg3, "final_norm": dgf,
        "s5_a_re": d_lr, "s5_a_im": d_li, "s5_log_dt": d_ldt,
        "s5_b_re": jnp.transpose(d_brt, (1, 2, 0)), "s5_b_im": jnp.transpose(d_bit, (1, 2, 0)),
        "s5_c_re": _bd_c_extract(dbdcr), "s5_c_im": -_bd_c_extract(dbdci),
        "s5_d": ddv, "s5_b_glu": dbglu, "sgu_ln_g": dlng, "sgu_ln_b": dlnb,
        "sgu_w_s": jnp.where(mask[None], dws, 0.0), "sgu_b_s": dbst[:, :, 0], "b_gate": dbgate,
    }

    sizes = [W[n].size for n in _SMALL]
    total = sum(sizes) + 1
    rows = -(-total // 128)
    rows = -(-rows // 8) * 8
    pad = rows * 128 - total

    def pack(d, extra):
        return jnp.concatenate([d[n].reshape(-1).astype(F32) for n in _SMALL] + [extra, jnp.zeros((pad,), F32)]
                               ).reshape(rows, 128)

    G, Dl, Mn, Vn = {}, {}, {}, {}

    def adam(plan):
        last = None
        for n, recv, sub in plan:
            if sub is None:
                g, d, mn, vn = _adam_sharded("adam_" + n, recv, sub, W[n][0], M[n][0], V[n][0])
                G[n], Dl[n], Mn[n], Vn[n] = g[None], d[None], mn[None], vn[None]
            else:
                tr = jnp.transpose
                g, d, mn, vn = _adam_sharded("adam_" + n, recv, sub, tr(W[n][0]), tr(M[n][0]), tr(V[n][0]))
                G[n], Dl[n], Mn[n], Vn[n] = tr(g)[None], tr(d)[None], tr(mn)[None], tr(vn)[None]
            last = g
        return last

    r_wd2, r_gu2 = _split_wait("exchange1_wait", es1, dx0, False)
    done = adam([("ffn2_w_down", r_wd2, None), ("ffn2_w_gate", r_gu2, 0), ("ffn2_w_up", r_gu2, 1)])
    r_wout, r_wba, r_wbb, r_wgate, r_wglu, r_win = _split_wait("exchange2_wait", es2, done, False)
    done = adam([("w_out", r_wout, None), ("w_branch_a", r_wba, None), ("w_branch_b", r_wbb, None),
                 ("w_gate", r_wgate, None), ("s5_w_glu", r_wglu, None), ("w_in", r_win, None)])

    late = loss_p[0, :1] + 0.0 * done.reshape(-1)[:1]
    zero1 = jnp.zeros((1,), F32)
    parts = _all_gather("gather_small_grads", [pack(small_g, late)])[0]
    sg, sd, sm, sv = _adam_small(parts, pack(W, zero1), pack(M, zero1), pack(V, zero1))

    def unpack(flat2d, into):
        flat = flat2d.reshape(-1)
        off = 0
        for n, s in zip(_SMALL, sizes):
            into[n] = flat[off:off + s].reshape(W[n].shape)
            off += s
        return flat[off]

    loss = unpack(sg, G)
    unpack(sd, Dl)
    unpack(sm, Mn)
    unpack(sv, Vn)

    (r_gu1,) = _split_wait("exchange3_wait", es3, sg, False)
    done = adam([("ffn1_w_gate", r_gu1, 0), ("ffn1_w_up", r_gu1, 1)])
    (r_wd1,) = _split_wait("exchange4_wait", es4, done, False)
    adam([("ffn1_w_down", r_wd1, None)])

    return loss, dx0[None], G, Dl, Mn, Vn


def kernel(x, ffn1_norm, ffn1_w_gate, ffn1_w_up, ffn1_w_down, mix_norm, w_in, s5_a_re, s5_a_im, s5_log_dt, s5_b_re, s5_b_im, s5_c_re, s5_c_im, s5_d, s5_w_glu, s5_b_glu, sgu_ln_g, sgu_ln_b, sgu_w_s, sgu_b_s, w_branch_a, w_branch_b, w_gate, b_gate, w_out, ffn2_norm, ffn2_w_gate, ffn2_w_up, ffn2_w_down, final_norm, loss_target, m_ffn1_norm, m_ffn1_w_gate, m_ffn1_w_up, m_ffn1_w_down, m_mix_norm, m_w_in, m_s5_a_re, m_s5_a_im, m_s5_log_dt, m_s5_b_re, m_s5_b_im, m_s5_c_re, m_s5_c_im, m_s5_d, m_s5_w_glu, m_s5_b_glu, m_sgu_ln_g, m_sgu_ln_b, m_sgu_w_s, m_sgu_b_s, m_w_branch_a, m_w_branch_b, m_w_gate, m_b_gate, m_w_out, m_ffn2_norm, m_ffn2_w_gate, m_ffn2_w_up, m_ffn2_w_down, m_final_norm, v_ffn1_norm, v_ffn1_w_gate, v_ffn1_w_up, v_ffn1_w_down, v_mix_norm, v_w_in, v_s5_a_re, v_s5_a_im, v_s5_log_dt, v_s5_b_re, v_s5_b_im, v_s5_c_re, v_s5_c_im, v_s5_d, v_s5_w_glu, v_s5_b_glu, v_sgu_ln_g, v_sgu_ln_b, v_sgu_w_s, v_sgu_b_s, v_w_branch_a, v_w_branch_b, v_w_gate, v_b_gate, v_w_out, v_ffn2_norm, v_ffn2_w_gate, v_ffn2_w_up, v_ffn2_w_down, v_final_norm):
    a = locals()
    W = {n: a[n] for n in _ORDER}
    M = {n: a["m_" + n] for n in _ORDER}
    V = {n: a["v_" + n] for n in _ORDER}
    loss, gx, G, Dl, Mn, Vn = _step(x, loss_target, W, M, V)
    return (loss, gx, *[G[n] for n in _ORDER], *[Dl[n] for n in _ORDER], *[Mn[n] for n in _ORDER],
            *[Vn[n] for n in _ORDER])
```

```python
import functools
import math

import jax
import jax.numpy as jnp
from jax import lax
from jax.experimental import pallas as pl
from jax.experimental.pallas import tpu as pltpu

F32 = jnp.float32
BF16 = jnp.bfloat16
NDEV = 8
NORM_EPS = 1e-6
D_MODEL = 2048
D_FF = 5632
FF_SHARD = D_FF // NDEV
S5_WIDTH = 1024
S5_GROUPS = 64
S5_GROUP_WIDTH = 16
S5_STATE = 64
S5_NS = S5_GROUPS * S5_STATE
SGU_WIDTH = 1024
SGU_HEADS = 8
MLP_CHUNK = 128
CHUNK = 64
ADAM_LR, ADAM_B1, ADAM_B2, ADAM_EPS, ADAM_WD, ADAM_STEP = 0.001, 0.9, 0.999, 1e-08, 0.01, 10
S5_TC = 256
S5_SEG = S5_TC // 8
S5_LG = 512
VMEM_BIG = 56 * 1024 * 1024

MESH = pl.DeviceIdType.MESH
SDS = jax.ShapeDtypeStruct
Block = pl.BlockSpec
ANY = pl.BlockSpec(memory_space=pl.ANY)


def _cparams(sem=None, vmem=None):
    return pltpu.CompilerParams(dimension_semantics=sem, vmem_limit_bytes=vmem)


def _const(shape):
    nd = len(shape)
    return pl.BlockSpec(shape, lambda i: (0,) * nd, pipeline_mode=pl.Buffered(1))


def _sigmoid(x):
    return 0.5 * jnp.tanh(0.5 * x) + 0.5


_GELU_C = math.sqrt(2.0 / math.pi)


def _gelu(x):
    return 0.5 * x * (1.0 + jnp.tanh(_GELU_C * (x + 0.044715 * x * x * x)))


def _gelu_grad(x):
    t = jnp.tanh(_GELU_C * (x + 0.044715 * x * x * x))
    return 0.5 * (1.0 + t) + 0.5 * x * (1.0 - t * t) * _GELU_C * (1.0 + 3.0 * 0.044715 * x * x)


NN = (((1,), (0,)), ((), ()))
NT = (((1,), (1,)), ((), ()))
TN = (((0,), (0,)), ((), ()))


def _dot(a, b, dims=NN):
    return lax.dot_general(a, b, dims, preferred_element_type=F32)


def _matmul(name, a, b, extras, *, grid, a_spec, b_spec, extra_specs, out_shapes, out_specs, acc_shape,
            epilogue, dims=NN, nb=None, compute=None, after=None, vmem=VMEM_BIG):
    nk = grid[2]
    if after is not None:
        extras = tuple(extras) + (after,)
        extra_specs = list(extra_specs) + [Block((8, 128), lambda i, j, k: (0, 0))]
    ne, no = len(extras), len(out_shapes)
    nacc = nb or 1
    if compute is None:
        def compute(a_ref, b_ref, q):
            return _dot(a_ref[...], b_ref[q] if nb else b_ref[...], dims)

    def body(*refs):
        a_ref, b_ref = refs[0], refs[1]
        ex = refs[2:2 + ne]
        outs = refs[2 + ne:2 + ne + no]
        if nk == 1:
            epilogue([compute(a_ref, b_ref, q) for q in range(nacc)], ex, outs)
            return
        acc_ref = refs[2 + ne + no]
        k = pl.program_id(2)

        @pl.when(k == 0)
        def _():
            acc_ref[...] = jnp.zeros_like(acc_ref)

        for q in range(nacc):
            acc_ref[q] += compute(a_ref, b_ref, q)

        @pl.when(k == nk - 1)
        def _():
            epilogue([acc_ref[q] for q in range(nacc)], ex, outs)

    scratch = [] if nk == 1 else [pltpu.VMEM((nacc,) + tuple(acc_shape), F32)]
    res = pl.pallas_call(
        body, name=name, grid=grid,
        in_specs=[a_spec, b_spec] + list(extra_specs),
        out_specs=list(out_specs), out_shape=list(out_shapes), scratch_shapes=scratch,
        compiler_params=_cparams(("parallel", "parallel", "arbitrary"), vmem),
    )(a, b, *extras)
    return res


def _store(dtype_outs=None):
    def ep(accs, ex, outs):
        outs[0][...] = accs[0].astype(outs[0].dtype)
    return ep


def _tile(n, t):
    t = min(n, t)
    assert n % t == 0, (n, t)
    return t


def _ksum(kq, dims):
    def compute(a_ref, b_ref, _):
        part = _dot(a_ref[0], b_ref[0], dims)
        for q in range(1, kq):
            part = part + _dot(a_ref[q], b_ref[q], dims)
        return part
    return compute


def _ksum_lanes(kq, ns, dims):
    def compute(a_ref, b_ref, _):
        part = _dot(a_ref[:, 0:ns], b_ref[0], dims)
        for q in range(1, kq):
            part = part + _dot(a_ref[:, q * ns:(q + 1) * ns], b_ref[q], dims)
        return part
    return compute


def _wide_b(g):
    def compute(a_ref, b_ref, _):
        bw = b_ref[0] if g == 1 else jnp.concatenate([b_ref[q] for q in range(g)], axis=1)
        return _dot(a_ref[...], bw, NN)
    return compute


KQ_DOWN = 4
TT_DEEP = 2048


def _ffn_up(name, h, wgu, after=None):
    T, D = h.shape
    tm = _tile(T, 1024)

    def ep(accs, ex, outs):
        a, b = accs
        outs[0][0] = a.astype(BF16)
        outs[0][1] = b.astype(BF16)
        outs[1][...] = (a * _sigmoid(a) * b).astype(BF16)

    return _matmul(
        name, h, wgu, (), after=after, grid=(NDEV, T // tm, 1),
        a_spec=Block((tm, D), lambda j, i, k: (i, 0)),
        b_spec=Block((None, 2, D, FF_SHARD), lambda j, i, k: (j, 0, 0, 0)),
        extra_specs=(),
        out_shapes=[SDS((NDEV, 2, T, FF_SHARD), BF16), SDS((NDEV, T, FF_SHARD), BF16)],
        out_specs=[Block((None, 2, tm, FF_SHARD), lambda j, i, k: (j, 0, i, 0)),
                   Block((None, tm, FF_SHARD), lambda j, i, k: (j, i, 0))],
        acc_shape=(tm, FF_SHARD), dims=NN, nb=2, epilogue=ep)


def _ffn_down(name, f, wd, xres, after=None):
    _, T, _ = f.shape
    tm, tn = _tile(T, 1024), 1024

    def ep(accs, ex, outs):
        outs[0][...] = ex[0][...] + 0.5 * accs[0]

    return _matmul(
        name, f, wd, (xres,), after=after, grid=(T // tm, D_MODEL // tn, NDEV // KQ_DOWN),
        a_spec=Block((KQ_DOWN, tm, FF_SHARD), lambda i, j, k: (k, i, 0)),
        b_spec=Block((KQ_DOWN, FF_SHARD, tn), lambda i, j, k: (k, 0, j)),
        extra_specs=[Block((tm, tn), lambda i, j, k: (i, j))],
        out_shapes=[SDS((T, D_MODEL), F32)],
        out_specs=[Block((tm, tn), lambda i, j, k: (i, j))],
        acc_shape=(tm, tn), compute=_ksum(KQ_DOWN, NN), epilogue=ep)[0]


def _ffn_down_bwd_act(name, dyb, wd, ab):
    T, D = dyb.shape
    tm = _tile(T, 1024)
    def ep(accs, ex, outs):
        df = accs[0]
        a = ex[0][0].astype(F32)
        b = ex[0][1].astype(F32)
        hs = 0.5 * _sigmoid(a)
        outs[0][0] = (df * b * hs * (1.0 + a * (1.0 - 2.0 * hs))).astype(BF16)
        outs[0][1] = (df * a * hs).astype(BF16)

    return _matmul(
        name, dyb, wd, (ab,), grid=(NDEV, T // tm, 1),
        a_spec=Block((tm, D), lambda j, i, k: (i, 0)),
        b_spec=Block((None, FF_SHARD, D), lambda j, i, k: (j, 0, 0)),
        extra_specs=[Block((None, 2, tm, FF_SHARD), lambda j, i, k: (j, 0, i, 0))],
        out_shapes=[SDS((NDEV, 2, T, FF_SHARD), BF16)],
        out_specs=[Block((None, 2, tm, FF_SHARD), lambda j, i, k: (j, 0, i, 0))],
        acc_shape=(tm, FF_SHARD), dims=NT, epilogue=ep)[0]


def _ffn_down_bwd_w(name, f, dyb, after=None):
    _, T, _ = f.shape
    tt, tn = _tile(T, TT_DEEP), 1024

    def ep(accs, ex, outs):
        outs[0][...] = (0.5 * accs[0]).astype(BF16)

    return _matmul(
        name, f, dyb, (), after=after, grid=(NDEV, D_MODEL // tn, T // tt),
        a_spec=Block((None, tt, FF_SHARD), lambda j, n, k: (j, k, 0)),
        b_spec=Block((tt, tn), lambda j, n, k: (k, n)),
        extra_specs=(),
        out_shapes=[SDS((NDEV, FF_SHARD, D_MODEL), BF16)],
        out_specs=[Block((None, FF_SHARD, tn), lambda j, n, k: (j, 0, n))],
        acc_shape=(FF_SHARD, tn), dims=TN, epilogue=ep)[0]


def _ffn_up_bwd_h(name, dab, wgu, after):
    _, _, T, _ = dab.shape
    tm = _tile(T, 1024)
    return _matmul(
        name, dab, wgu, (), after=after, grid=(T // tm, 1, NDEV),
        a_spec=Block((None, 2, tm, FF_SHARD), lambda i, j, k: (k, 0, i, 0)),
        b_spec=Block((None, 2, D_MODEL, FF_SHARD), lambda i, j, k: (k, 0, 0, 0)),
        extra_specs=(),
        out_shapes=[SDS((T, D_MODEL), BF16)],
        out_specs=[Block((tm, D_MODEL), lambda i, j, k: (i, 0))],
        acc_shape=(tm, D_MODEL), compute=_ksum(2, NT), epilogue=_store())[0]


def _ffn_up_bwd_w(name, h, dab):
    T, D = h.shape
    tt, tn = _tile(T, TT_DEEP), 1024

    def ep(accs, ex, outs):
        outs[0][0] = accs[0].astype(BF16)
        outs[0][1] = accs[1].astype(BF16)

    return _matmul(
        name, dab, h, (), grid=(NDEV, D // tn, T // tt),
        a_spec=Block((None, 2, tt, FF_SHARD), lambda j, n, k: (j, 0, k, 0)),
        b_spec=Block((tt, tn), lambda j, n, k: (k, n)),
        extra_specs=(),
        out_shapes=[SDS((NDEV, 2, FF_SHARD, D), BF16)],
        out_specs=[Block((None, 2, FF_SHARD, tn), lambda j, n, k: (j, 0, 0, n))],
        acc_shape=(FF_SHARD, tn), nb=2,
        compute=lambda a_ref, b_ref, q: _dot(a_ref[q], b_ref[...], TN), epilogue=ep)[0]


def _shards_per_step(ns):
    return max(g for g in (1, 2, 4, 8) if g * ns <= 2048)


def _split_lanes(g, ns):
    def ep(accs, ex, outs):
        for q in range(g):
            outs[0][q] = accs[0][:, q * ns:(q + 1) * ns].astype(outs[0].dtype)
    return ep


def _col_fwd(name, a, w, out_dtype=BF16, after=None):
    T, K = a.shape
    ns = w.shape[2]
    g = _shards_per_step(ns)
    tm = _tile(T, 1024)
    return _matmul(
        name, a, w, (), after=after, grid=(NDEV // g, T // tm, 1),
        a_spec=Block((tm, K), lambda j, i, k: (i, 0)),
        b_spec=Block((g, K, ns), lambda j, i, k: (j, 0, 0)),
        extra_specs=(),
        out_shapes=[SDS((T, NDEV * ns), out_dtype)],
        out_specs=[Block((tm, g * ns), lambda j, i, k: (i, j))],
        acc_shape=(tm, g * ns), compute=_wide_b(g), epilogue=_store())[0]


def _col_bwd_a(name, dy, w, add=None):
    T = dy.shape[0]
    _, K, ns = w.shape
    tm, tn = _tile(T, 1024), _tile(K, 1024)

    def ep(accs, ex, outs):
        r = accs[0]
        if add is not None:
            r = r + ex[0][...].astype(F32)
        outs[0][...] = r.astype(BF16)

    extras = () if add is None else (add,)
    return _matmul(
        name, dy, w, extras, grid=(T // tm, K // tn, 1),
        a_spec=Block((tm, NDEV * ns), lambda i, j, k: (i, 0)),
        b_spec=Block((NDEV, tn, ns), lambda i, j, k: (0, j, 0)),
        extra_specs=[Block((tm, tn), lambda i, j, k: (i, j))] * len(extras),
        out_shapes=[SDS((T, K), BF16)],
        out_specs=[Block((tm, tn), lambda i, j, k: (i, j))],
        acc_shape=(tm, tn), compute=_ksum_lanes(NDEV, ns, NT), epilogue=ep)[0]


def _col_bwd_w(name, a, dy, ns):
    T, K = a.shape
    g = _shards_per_step(ns)
    tt, tr = _tile(T, TT_DEEP), _tile(K, 1024)
    return _matmul(
        name, a, dy, (), grid=(NDEV // g, K // tr, T // tt),
        a_spec=Block((tt, tr), lambda j, n, k: (k, n)),
        b_spec=Block((tt, g * ns), lambda j, n, k: (k, j)),
        extra_specs=(),
        out_shapes=[SDS((NDEV, K, ns), BF16)],
        out_specs=[Block((g, tr, ns), lambda j, n, k: (j, n, 0))],
        acc_shape=(tr, g * ns), dims=TN, epilogue=_split_lanes(g, ns))[0]


def _gate_fwd(name, h, w, bias):
    T, K = h.shape
    ns = w.shape[2]
    g = 2
    per = D_MODEL // (g * ns)
    tm = _tile(T, 1024)

    def ep(accs, ex, outs):
        outs[0][...] = (accs[0] + ex[0][...]).astype(BF16)

    return _matmul(
        name, h, w, (bias,), grid=(NDEV // g, T // tm, 1),
        a_spec=Block((tm, K), lambda j, i, k: (i, 0)),
        b_spec=Block((g, K, ns), lambda j, i, k: (j, 0, 0)),
        extra_specs=[Block((None, 1, g * ns), lambda j, i, k: (j // per, 0, j % per))],
        out_shapes=[SDS((2, T, D_MODEL), BF16)],
        out_specs=[Block((None, tm, g * ns), lambda j, i, k: (j // per, i, j % per))],
        acc_shape=(tm, g * ns), compute=_wide_b(g), epilogue=ep)[0]


def _gate_bwd_a(name, dgl, w):
    _, T, _ = dgl.shape
    _, K, ns = w.shape
    per = D_MODEL // ns
    tm, tn = _tile(T, 1024), 1024

    def compute(a_ref, b_ref, _):
        part = None
        for q in range(NDEV):
            d = _dot(a_ref[q // per, :, (q % per) * ns:(q % per + 1) * ns], b_ref[q], NT)
            part = d if part is None else part + d
        return part

    return _matmul(
        name, dgl, w, (), grid=(T // tm, K // tn, 1),
        a_spec=Block((2, tm, D_MODEL), lambda i, j, k: (0, i, 0)),
        b_spec=Block((NDEV, tn, ns), lambda i, j, k: (0, j, 0)),
        extra_specs=(),
        out_shapes=[SDS((T, K), BF16)],
        out_specs=[Block((tm, tn), lambda i, j, k: (i, j))],
        acc_shape=(tm, tn), compute=compute, epilogue=_store())[0]


def _gate_bwd_w(name, h, dgl, ns):
    T, K = h.shape
    g = 2
    per = D_MODEL // (g * ns)
    tt, tr = _tile(T, TT_DEEP), 1024
    return _matmul(
        name, h, dgl, (), grid=(NDEV // g, K // tr, T // tt),
        a_spec=Block((tt, tr), lambda j, n, k: (k, n)),
        b_spec=Block((None, tt, g * ns), lambda j, n, k: (j // per, k, j % per)),
        extra_specs=(),
        out_shapes=[SDS((NDEV, K, ns), BF16)],
        out_specs=[Block((g, tr, ns), lambda j, n, k: (j, n, 0))],
        acc_shape=(tr, g * ns), dims=TN, epilogue=_split_lanes(g, ns))[0]


def _plain_fwd_res(name, a, w, xres):
    T, K = a.shape
    N = w.shape[1]
    tm, tn = _tile(T, 1024), _tile(N, 1024)

    def ep(accs, ex, outs):
        outs[0][...] = ex[0][...] + accs[0]

    return _matmul(
        name, a, w, (xres,), grid=(T // tm, N // tn, 1),
        a_spec=Block((tm, K), lambda i, j, k: (i, 0)),
        b_spec=Block((K, tn), lambda i, j, k: (0, j)),
        extra_specs=[Block((tm, tn), lambda i, j, k: (i, j))],
        out_shapes=[SDS((T, N), F32)],
        out_specs=[Block((tm, tn), lambda i, j, k: (i, j))],
        acc_shape=(tm, tn), dims=NN, nb=None, epilogue=ep)[0]


def _plain_bwd_a(name, dy, w):
    T, N = dy.shape
    K = w.shape[0]
    tm, tn = _tile(T, 1024), _tile(K, 1024)
    return _matmul(
        name, dy, w, (), grid=(T // tm, K // tn, 1),
        a_spec=Block((tm, N), lambda i, j, k: (i, 0)),
        b_spec=Block((tn, N), lambda i, j, k: (j, 0)),
        extra_specs=(),
        out_shapes=[SDS((T, K), BF16)],
        out_specs=[Block((tm, tn), lambda i, j, k: (i, j))],
        acc_shape=(tm, tn), dims=NT, nb=None, epilogue=_store())[0]


def _plain_bwd_w(name, a, dy):
    T, K = a.shape
    N = dy.shape[1]
    tt, tr, tn = _tile(T, TT_DEEP), _tile(K, 1024), _tile(N, 1024)
    return _matmul(
        name, a, dy, (), grid=(K // tr, N // tn, T // tt),
        a_spec=Block((tt, tr), lambda m, n, k: (k, m)),
        b_spec=Block((tt, tn), lambda m, n, k: (k, n)),
        extra_specs=(),
        out_shapes=[SDS((K, N), BF16)],
        out_specs=[Block((tr, tn), lambda m, n, k: (m, n))],
        acc_shape=(tr, tn), dims=TN, nb=None, epilogue=_store())[0]


def _rms_fwd(name, x, g):
    T, D = x.shape
    tm = _tile(T, 512)

    def body(x_ref, g_ref, h_ref):
        xv = x_ref[...]
        r = lax.rsqrt(jnp.mean(xv * xv, axis=-1, keepdims=True) + NORM_EPS)
        h_ref[...] = (xv * r * g_ref[...]).astype(BF16)

    return pl.pallas_call(
        body, name=name, grid=(T // tm,),
        in_specs=[Block((tm, D), lambda i: (i, 0)), Block((1, D), lambda i: (0, 0))],
        out_specs=Block((tm, D), lambda i: (i, 0)), out_shape=SDS((T, D), BF16),
        compiler_params=_cparams(("arbitrary",), VMEM_BIG))(x, g)


def _rms_bwd(name, dh, x, g, dxin, out_dtype):
    T, D = x.shape
    tm = _tile(T, 512)

    def body(dh_ref, x_ref, g_ref, dxin_ref, dx_ref, dg_ref):
        i = pl.program_id(0)
        xv = x_ref[...]
        dh = dh_ref[...].astype(F32)
        r = lax.rsqrt(jnp.mean(xv * xv, axis=-1, keepdims=True) + NORM_EPS)
        xh = xv * r
        gd = dh * g_ref[...]
        dx = dxin_ref[...].astype(F32) + r * (gd - xh * jnp.mean(gd * xh, axis=-1, keepdims=True))
        dx_ref[...] = dx.astype(out_dtype)
        dgp = jnp.sum(dh * xh, axis=0, keepdims=True)

        @pl.when(i == 0)
        def _():
            dg_ref[...] = dgp

        @pl.when(i > 0)
        def _():
            dg_ref[...] += dgp

    row = Block((tm, D), lambda i: (i, 0))
    vec = Block((1, D), lambda i: (0, 0))
    return pl.pallas_call(
        body, name=name, grid=(T // tm,),
        in_specs=[row, row, vec, row], out_specs=[row, vec],
        out_shape=[SDS((T, D), out_dtype), SDS((1, D), F32)],
        compiler_params=_cparams(("arbitrary",), VMEM_BIG))(dh, x, g, dxin)


def _loss_head(name, x, g, tgt):
    T, D = x.shape
    tm = _tile(T, 512)

    def body(x_ref, g_ref, t_ref, loss_ref, dxb_ref, dg_ref):
        i = pl.program_id(0)
        xv = x_ref[...]
        gv = g_ref[...]
        r = lax.rsqrt(jnp.mean(xv * xv, axis=-1, keepdims=True) + NORM_EPS)
        xh = xv * r
        err = xh * gv - t_ref[...]
        lp = 0.5 * jnp.sum(jnp.mean(err * err, axis=-1, keepdims=True), axis=0, keepdims=True)
        dout = err * (1.0 / D)
        gd = dout * gv
        dx = r * (gd - xh * jnp.mean(gd * xh, axis=-1, keepdims=True))
        dxb_ref[...] = dx.astype(BF16)
        dgp = jnp.sum(dout * xh, axis=0, keepdims=True)
        lpb = jnp.broadcast_to(lp, (1, 128))

        @pl.when(i == 0)
        def _():
            dg_ref[...] = dgp
            loss_ref[...] = lpb

        @pl.when(i > 0)
        def _():
            dg_ref[...] += dgp
            loss_ref[...] += lpb

    row = Block((tm, D), lambda i: (i, 0))
    vec = Block((1, D), lambda i: (0, 0))
    return pl.pallas_call(
        body, name=name, grid=(T // tm,),
        in_specs=[row, vec, row], out_specs=[Block((1, 128), lambda i: (0, 0)), row, vec],
        out_shape=[SDS((1, 128), F32), SDS((T, D), BF16), SDS((1, D), F32)],
        compiler_params=_cparams(("arbitrary",), VMEM_BIG))(x, g, tgt)


def _merge_fwd(name, pa, pb, gl):
    T, D = pa.shape
    tm = _tile(T, 512)

    def body(pa_ref, pb_ref, gl_ref, o_ref):
        ga = _sigmoid(gl_ref[0].astype(F32))
        gb = _sigmoid(gl_ref[1].astype(F32))
        o_ref[...] = (ga * pa_ref[...].astype(F32) + gb * pb_ref[...].astype(F32)).astype(BF16)

    row = Block((tm, D), lambda i: (i, 0))
    return pl.pallas_call(
        body, name=name, grid=(T // tm,),
        in_specs=[row, row, Block((2, tm, D), lambda i: (0, i, 0))], out_specs=row,
        out_shape=SDS((T, D), BF16), compiler_params=_cparams(("arbitrary",), VMEM_BIG))(pa, pb, gl)


def _merge_bwd(name, dm, pa, pb, gl):
    T, D = pa.shape
    tm = _tile(T, 512)

    def body(dm_ref, pa_ref, pb_ref, gl_ref, dpa_ref, dpb_ref, dgl_ref, db_ref):
        i = pl.program_id(0)
        dmv = dm_ref[...].astype(F32)
        ga = _sigmoid(gl_ref[0].astype(F32))
        gb = _sigmoid(gl_ref[1].astype(F32))
        dpa_ref[...] = (dmv * ga).astype(BF16)
        dpb_ref[...] = (dmv * gb).astype(BF16)
        dga = dmv * pa_ref[...].astype(F32) * ga * (1.0 - ga)
        dgb = dmv * pb_ref[...].astype(F32) * gb * (1.0 - gb)
        dgl_ref[0] = dga.astype(BF16)
        dgl_ref[1] = dgb.astype(BF16)
        sa = jnp.sum(dga, axis=0, keepdims=True)
        sb = jnp.sum(dgb, axis=0, keepdims=True)

        @pl.when(i == 0)
        def _():
            db_ref[0] = sa
            db_ref[1] = sb

        @pl.when(i > 0)
        def _():
            db_ref[0] += sa
            db_ref[1] += sb

    row = Block((tm, D), lambda i: (i, 0))
    two = Block((2, tm, D), lambda i: (0, i, 0))
    return pl.pallas_call(
        body, name=name, grid=(T // tm,),
        in_specs=[row, row, row, two], out_specs=[row, row, two, Block((2, 1, D), lambda i: (0, 0, 0))],
        out_shape=[SDS((T, D), BF16), SDS((T, D), BF16), SDS((2, T, D), BF16), SDS((2, 1, D), F32)],
        compiler_params=_cparams(("arbitrary",), VMEM_BIG))(dm, pa, pb, gl)


def _sgu_core(ur, vr, lng, lnb, ws_ref, bs_ref):
    tm = ur.shape[0]
    gu = _gelu(ur)
    gv = _gelu(vr)
    mu = jnp.mean(gv, axis=-1, keepdims=True)
    cen = gv - mu
    rstd = lax.rsqrt(jnp.mean(cen * cen, axis=-1, keepdims=True) + NORM_EPS)
    xhat = cen * rstd
    vn = (xhat * lng + lnb).astype(BF16)
    rows = []
    for n in range(tm // MLP_CHUNK):
        cols = []
        for h in range(SGU_HEADS):
            blk = vn[n * MLP_CHUNK:(n + 1) * MLP_CHUNK, h * 128:(h + 1) * 128]
            cols.append(_dot(ws_ref[h], blk) + bs_ref[h])
        rows.append(jnp.concatenate(cols, axis=1))
    mixed = jnp.concatenate(rows, axis=0) if len(rows) > 1 else rows[0]
    return gu, xhat, rstd, vn, mixed


def _sgu_fwd(name, proj, lng, lnb, wsm, bst):
    T = proj.shape[0]
    W = SGU_WIDTH
    tm = _tile(T, 512)

    def body(u_ref, v_ref, lng_ref, lnb_ref, ws_ref, bs_ref, o_ref):
        gu, _, _, _, mixed = _sgu_core(u_ref[...].astype(F32), v_ref[...].astype(F32), lng_ref[...], lnb_ref[...],
                                       ws_ref, bs_ref)
        o_ref[...] = (gu * mixed).astype(BF16)

    vec = Block((1, W), lambda i: (0, 0))
    return pl.pallas_call(
        body, name=name, grid=(T // tm,),
        in_specs=[Block((tm, W), lambda i: (i, 1)), Block((tm, W), lambda i: (i, 2)), vec, vec,
                  Block((SGU_HEADS, 128, 128), lambda i: (0, 0, 0)), Block((SGU_HEADS, 128, 128), lambda i: (0, 0, 0))],
        out_specs=Block((tm, W), lambda i: (i, 0)), out_shape=SDS((T, W), BF16),
        compiler_params=_cparams(("arbitrary",), VMEM_BIG))(proj, proj, lng, lnb, wsm, bst)


def _sgu_bwd(name, dyb, dua, proj, lng, lnb, wsm, wsmt, bst):
    T = proj.shape[0]
    W = SGU_WIDTH
    tm = _tile(T, 512)

    def body(dy_ref, dua_ref, u_ref, v_ref, lng_ref, lnb_ref, ws_ref, wst_ref, bs_ref,
             duv_ref, dws_ref, dbs_ref, dlng_ref, dlnb_ref):
        i = pl.program_id(0)
        duv_ref[:, :W] = dua_ref[...]
        ur = u_ref[...].astype(F32)
        vr = v_ref[...].astype(F32)
        lng_v = lng_ref[...]
        gu, xhat, rstd, vn, mixed = _sgu_core(ur, vr, lng_v, lnb_ref[...], ws_ref, bs_ref)
        dy = dy_ref[...].astype(F32)
        dgu = dy * mixed
        dmix = dy * gu
        dmb = dmix.astype(BF16)
        dws_p, dbs_p, rows = [], [], []
        for h in range(SGU_HEADS):
            acc_w = jnp.zeros((128, 128), F32)
            acc_b = jnp.zeros((128, 1), F32)
            for n in range(tm // MLP_CHUNK):
                r0 = n * MLP_CHUNK
                dmt = dmb[r0:r0 + MLP_CHUNK, h * 128:(h + 1) * 128]
                acc_w = acc_w + _dot(dmt, vn[r0:r0 + MLP_CHUNK, h * 128:(h + 1) * 128], NT)
                acc_b = acc_b + jnp.sum(dmix[r0:r0 + MLP_CHUNK, h * 128:(h + 1) * 128], axis=1, keepdims=True)
            dws_p.append(acc_w)
            dbs_p.append(jnp.broadcast_to(acc_b, (128, 128)))
        for n in range(tm // MLP_CHUNK):
            r0 = n * MLP_CHUNK
            rows.append(jnp.concatenate(
                [_dot(wst_ref[h], dmb[r0:r0 + MLP_CHUNK, h * 128:(h + 1) * 128]) for h in range(SGU_HEADS)], axis=1))
        dvn = jnp.concatenate(rows, axis=0) if len(rows) > 1 else rows[0]
        dlng_p = jnp.sum(dvn * xhat, axis=0, keepdims=True)
        dlnb_p = jnp.sum(dvn, axis=0, keepdims=True)
        dxh = dvn * lng_v
        dgv = rstd * (dxh - jnp.mean(dxh, axis=-1, keepdims=True)
                      - xhat * jnp.mean(dxh * xhat, axis=-1, keepdims=True))
        duv_ref[:, W:2 * W] = (dgu * _gelu_grad(ur)).astype(BF16)
        duv_ref[:, 2 * W:] = (dgv * _gelu_grad(vr)).astype(BF16)

        @pl.when(i == 0)
        def _():
            for h in range(SGU_HEADS):
                dws_ref[h] = dws_p[h]
                dbs_ref[h] = dbs_p[h]
            dlng_ref[...] = dlng_p
            dlnb_ref[...] = dlnb_p

        @pl.when(i > 0)
        def _():
            for h in range(SGU_HEADS):
                dws_ref[h] += dws_p[h]
                dbs_ref[h] += dbs_p[h]
            dlng_ref[...] += dlng_p
            dlnb_ref[...] += dlnb_p

    vec = Block((1, W), lambda i: (0, 0))
    wsb = Block((SGU_HEADS, 128, 128), lambda i: (0, 0, 0))
    hsq = SDS((SGU_HEADS, 128, 128), F32)
    return pl.pallas_call(
        body, name=name, grid=(T // tm,),
        in_specs=[Block((tm, W), lambda i: (i, 0)), Block((tm, W), lambda i: (i, 0)),
                  Block((tm, W), lambda i: (i, 1)), Block((tm, W), lambda i: (i, 2)),
                  vec, vec, wsb, wsb, wsb],
        out_specs=[Block((tm, 3 * W), lambda i: (i, 0)), wsb, wsb, vec, vec],
        out_shape=[SDS((T, 3 * W), BF16), hsq, hsq, SDS((1, W), F32), SDS((1, W), F32)],
        compiler_params=_cparams(("arbitrary",), VMEM_BIG))(dyb, dua, proj, proj, lng, lnb, wsm, wsmt, bst)


def _s5_disc(lr, li, ldt, brt, bit):
    dt = jnp.exp(ldt)
    decay = jnp.exp(lr * dt)
    abr = decay * jnp.cos(li * dt)
    abi = decay * jnp.sin(li * dt)
    denom = lr * lr + li * li
    nr = abr - 1.0
    ni = abi
    kr = (nr * lr + ni * li) / denom
    ki = (ni * lr - nr * li) / denom
    bkr = kr[None] * brt - ki[None] * bit
    bki = kr[None] * bit + ki[None] * brt
    return abr, abi, bkr, bki


def _s5_prep(lr, li, ldt, brt, bit):
    G, P, C = S5_GROUPS, S5_STATE, S5_GROUP_WIDTH

    def body(lr_ref, li_ref, ldt_ref, br_ref, bi_ref, abr_ref, abi_ref, pwr_ref, pwi_ref, bkr_ref, bki_ref):
        lr_, li_, ldt_ = lr_ref[...], li_ref[...], ldt_ref[...]
        res = _s5_disc(lr_, li_, ldt_, br_ref[...], bi_ref[...])
        for o, r in zip((abr_ref, abi_ref, bkr_ref, bki_ref), res):
            o[...] = r
        dt = jnp.exp(ldt_)
        n = lax.broadcasted_iota(jnp.int32, (S5_SEG, G, P), 0).astype(F32) + 1.0
        dec = jnp.exp((lr_ * dt)[None] * n)
        ang = (li_ * dt)[None] * n
        pwr_ref[...] = dec * jnp.cos(ang)
        pwi_ref[...] = dec * jnp.sin(ang)

    gp = SDS((G, P), F32)
    sgp = SDS((S5_SEG, G, P), F32)
    cgp = SDS((C, G, P), F32)
    return pl.pallas_call(body, name="s5_prep", out_shape=[gp, gp, sgp, sgp, cgp, cgp])(lr, li, ldt, brt, bit)


def _s5_prep_bwd(lr, li, ldt, brt, bit, dabr, dabi, dbkr, dbki):
    G, P, C = S5_GROUPS, S5_STATE, S5_GROUP_WIDTH

    def body(lr_ref, li_ref, ldt_ref, br_ref, bi_ref, dabr_ref, dabi_ref, dbkr_ref, dbki_ref,
             o_lr, o_li, o_ldt, o_br, o_bi):
        _, pull = jax.vjp(_s5_disc, lr_ref[...], li_ref[...], ldt_ref[...], br_ref[...], bi_ref[...])
        g = pull((dabr_ref[...], dabi_ref[...], dbkr_ref[...], dbki_ref[...]))
        for o, r in zip((o_lr, o_li, o_ldt, o_br, o_bi), g):
            o[...] = r

    gp = SDS((G, P), F32)
    cgp = SDS((C, G, P), F32)
    return pl.pallas_call(body, name="s5_prep_bwd", out_shape=[gp, gp, SDS((G, 1), F32), cgp, cgp])(
        lr, li, ldt, brt, bit, dabr, dabi, dbkr, dbki)


def _s5_scan(buf_ref, ar_row, ai_row, pwr_ref, pwi_ref, carry_ref, LG, xs_ref=None, dar_ref=None, dai_ref=None):
    reverse = xs_ref is not None
    NS, SEG = S5_NS, S5_SEG
    sgn = -1.0 if reverse else 1.0
    for lg in range(NS // LG):
        cr = slice(lg * LG, (lg + 1) * LG)
        ci = slice(NS + lg * LG, NS + (lg + 1) * LG)
        ar1, ai1 = ar_row[:, cr], sgn * ai_row[:, cr]
        asr1, asi1 = pwr_ref[SEG - 1:SEG, cr], sgn * pwi_ref[SEG - 1:SEG, cr]
        ar = jnp.broadcast_to(ar1, (8, LG))
        ai = jnp.broadcast_to(ai1, (8, LG))

        def step_of(j):
            return (SEG - 1 - j) if reverse else j

        def p1(j, st):
            sr, si = st
            rows = pl.ds(pl.multiple_of(step_of(j) * 8, 8), 8)
            nr = ar * sr - ai * si + buf_ref[rows, cr]
            ni = ar * si + ai * sr + buf_ref[rows, ci]
            buf_ref[rows, cr] = nr
            buf_ref[rows, ci] = ni
            return nr, ni

        z = jnp.zeros((8, LG), F32)
        er, ei = lax.fori_loop(0, SEG, p1, (z, z), unroll=True)
        c_r = carry_ref[:, cr]
        c_i = carry_ref[:, ci]
        cs_r, cs_i = [None] * 8, [None] * 8
        order = range(7, -1, -1) if reverse else range(8)
        for s in order:
            cs_r[s], cs_i[s] = c_r, c_i
            e_r, e_i = er[s:s + 1], ei[s:s + 1]
            c_r, c_i = e_r + asr1 * c_r - asi1 * c_i, e_i + asr1 * c_i + asi1 * c_r
        carry_ref[:, cr] = c_r
        carry_ref[:, ci] = c_i
        cmr = jnp.concatenate(cs_r, axis=0)
        cmi = jnp.concatenate(cs_i, axis=0)

        def carried(j):
            pr = pwr_ref[pl.ds(j, 1), cr]
            pi = sgn * pwi_ref[pl.ds(j, 1), cr]
            return pr * cmr - pi * cmi, pr * cmi + pi * cmr

        if not reverse:
            def p2(j, st):
                rows = pl.ds(pl.multiple_of(j * 8, 8), 8)
                wr, wi = carried(j)
                buf_ref[rows, cr] += wr
                buf_ref[rows, ci] += wi
                return st

            lax.fori_loop(0, SEG, p2, 0, unroll=True)
        else:
            def p2(j, st):
                pr, pi, dr, di = st
                rows = pl.ds(pl.multiple_of(step_of(j) * 8, 8), 8)
                xr = xs_ref[rows, cr]
                xi = xs_ref[rows, ci]
                dr = dr + pr * xr + pi * xi
                di = di + pi * xr - pr * xi
                wr, wi = carried(j)
                gr = buf_ref[rows, cr] + wr
                gi = buf_ref[rows, ci] + wi
                buf_ref[rows, cr] = gr
                buf_ref[rows, ci] = gi
                return gr, gi, dr, di

            st = lax.fori_loop(0, SEG, p2, (cmr, cmi, z, z), unroll=True)
            dar_ref[:, cr] += st[2]
            dai_ref[:, cr] += st[3]


def _s5_fwd(proj, perm, permt, bdbr, bdbi, bdcr, bdci, abr, abi, asr, asi, dvec, wglu, bglu):
    T = proj.shape[0]
    TC, NS, W = S5_TC, S5_NS, S5_WIDTH
    nc = T // TC

    def body(u_ref, pm_ref, pmt_ref, bdbr_ref, bdbi_ref, bdcr_ref, bdci_ref, ar_ref, ai_ref, asr_ref, asi_ref,
             d_ref, wglu_ref, bglu_ref, ya_ref, xs_ref, ypre_ref, carry_ref):
        i = pl.program_id(0)

        @pl.when(i == 0)
        def _():
            carry_ref[...] = jnp.zeros_like(carry_ref)

        up = _dot(pm_ref[...], u_ref[...]).astype(BF16)
        for j in range(8):
            ut = up[:, j * 128:(j + 1) * 128]
            xs_ref[:, j * 512:(j + 1) * 512] = _dot(ut, bdbr_ref[j])
            xs_ref[:, NS + j * 512:NS + (j + 1) * 512] = _dot(ut, bdbi_ref[j])
        _s5_scan(xs_ref, ar_ref[...], ai_ref[...], asr_ref, asi_ref, carry_ref, S5_LG)
        ys = []
        for j in range(8):
            xr = xs_ref[:, j * 512:(j + 1) * 512].astype(BF16)
            xi = xs_ref[:, NS + j * 512:NS + (j + 1) * 512].astype(BF16)
            ys.append(_dot(xr, bdcr_ref[j]) + _dot(xi, bdci_ref[j]))
        ypre = jnp.concatenate(ys, axis=1) + d_ref[...] * up.astype(F32)
        ypre_ref[...] = ypre
        ya = _gelu(ypre)
        zl = _dot(ya.astype(BF16), wglu_ref[...]) + bglu_ref[...]
        outp = (ya * _sigmoid(zl)).astype(BF16)
        ya_ref[...] = _dot(pmt_ref[...], outp).astype(BF16)

    return pl.pallas_call(
        body, name="s5_fwd", grid=(nc,),
        in_specs=[Block((TC, W), lambda i: (i, 0)), _const((TC, TC)), _const((TC, TC)),
                  _const((8, 128, 512)), _const((8, 128, 512)), _const((8, 512, 128)), _const((8, 512, 128)),
                  _const((1, NS)), _const((1, NS)), _const((S5_SEG, NS)), _const((S5_SEG, NS)),
                  _const((1, W)), _const((W, W)), _const((1, W))],
        out_specs=[Block((TC, W), lambda i: (i, 0)), Block((TC, 2 * NS), lambda i: (i, 0)),
                   Block((TC, W), lambda i: (i, 0))],
        out_shape=[SDS((T, W), BF16), SDS((T, 2 * NS), F32), SDS((T, W), F32)],
        scratch_shapes=[pltpu.VMEM((1, 2 * NS), F32)],
        compiler_params=_cparams(("arbitrary",), VMEM_BIG),
    )(proj, perm, permt, bdbr, bdbi, bdcr, bdci, abr, abi, asr, asi, dvec, wglu, bglu)


def _s5_bwd(dya, proj, ypre, xs, perm, permt, bdbr, bdbi, bdcr, bdci, abr, abi, asr, asi, dvec, wglu, bglu):
    T = proj.shape[0]
    TC, NS, W = S5_TC, S5_NS, S5_WIDTH
    nc = T // TC

    def body(dya_ref, u_ref, ypre_ref, xs_ref, pm_ref, pmt_ref, bdbr_ref, bdbi_ref, bdcr_ref, bdci_ref,
             ar_ref, ai_ref, asr_ref, asi_ref, d_ref, wglu_ref, bglu_ref,
             du_ref, dar_ref, dai_ref, dd_ref, dbglu_ref, o_dbdbr, o_dbdbi, o_dbdcr, o_dbdci, o_dwglu,
             g_ref, carry_ref, dbdbr_ref, dbdbi_ref, dbdcr_ref, dbdci_ref, dwglu_ref):
        i = pl.program_id(0)

        @pl.when(i == 0)
        def _():
            carry_ref[...] = jnp.zeros_like(carry_ref)
            for r in (dbdbr_ref, dbdbi_ref, dbdcr_ref, dbdci_ref, dar_ref, dai_ref, dd_ref, dwglu_ref, dbglu_ref):
                r[...] = jnp.zeros_like(r)

        pm = pm_ref[...]
        dyo = _dot(pm, dya_ref[...])
        up = _dot(pm, u_ref[...]).astype(BF16)
        upf = up.astype(F32)
        ypre_v = ypre_ref[...]
        ya = _gelu(ypre_v)
        yab = ya.astype(BF16)
        sg = _sigmoid(_dot(yab, wglu_ref[...]) + bglu_ref[...])
        dz = dyo * ya * sg * (1.0 - sg)
        dzb = dz.astype(BF16)
        dya_t = dyo * sg + _dot(dzb, wglu_ref[...], NT)
        dwglu_ref[...] += _dot(yab, dzb, TN)
        dbglu_ref[...] += jnp.sum(dz, axis=0, keepdims=True)
        dy = dya_t * _gelu_grad(ypre_v)
        dd_ref[...] += jnp.sum(dy * upf, axis=0, keepdims=True)
        dyb = dy.astype(BF16)
        for j in range(8):
            dyj = dyb[:, j * 128:(j + 1) * 128]
            g_ref[:, j * 512:(j + 1) * 512] = _dot(dyj, bdcr_ref[j], NT)
            g_ref[:, NS + j * 512:NS + (j + 1) * 512] = _dot(dyj, bdci_ref[j], NT)
            dbdcr_ref[j] += _dot(xs_ref[:, j * 512:(j + 1) * 512].astype(BF16), dyj, TN)
            dbdci_ref[j] += _dot(xs_ref[:, NS + j * 512:NS + (j + 1) * 512].astype(BF16), dyj, TN)
        _s5_scan(g_ref, ar_ref[...], ai_ref[...], asr_ref, asi_ref, carry_ref, S5_LG,
                 xs_ref=xs_ref, dar_ref=dar_ref, dai_ref=dai_ref)
        dus = []
        for j in range(8):
            ut = up[:, j * 128:(j + 1) * 128]
            gr = g_ref[:, j * 512:(j + 1) * 512].astype(BF16)
            gi = g_ref[:, NS + j * 512:NS + (j + 1) * 512].astype(BF16)
            dbdbr_ref[j] += _dot(ut, gr, TN)
            dbdbi_ref[j] += _dot(ut, gi, TN)
            dus.append(_dot(gr, bdbr_ref[j], NT) + _dot(gi, bdbi_ref[j], NT))
        dup = jnp.concatenate(dus, axis=1) + d_ref[...] * dy
        du_ref[...] = _dot(pmt_ref[...], dup.astype(BF16)).astype(BF16)

        @pl.when(i == nc - 1)
        def _():
            for src, dst in ((dbdbr_ref, o_dbdbr), (dbdbi_ref, o_dbdbi), (dbdcr_ref, o_dbdcr),
                             (dbdci_ref, o_dbdci), (dwglu_ref, o_dwglu)):
                pltpu.sync_copy(src, dst)

    c2 = lambda i: (0, 0)
    rev = lambda i: (nc - 1 - i, 0)
    return pl.pallas_call(
        body, name="s5_bwd", grid=(nc,),
        in_specs=[Block((TC, W), rev), Block((TC, W), rev), Block((TC, W), rev), Block((TC, 2 * NS), rev),
                  _const((TC, TC)), _const((TC, TC)),
                  _const((8, 128, 512)), _const((8, 128, 512)), _const((8, 512, 128)), _const((8, 512, 128)),
                  _const((1, NS)), _const((1, NS)), _const((S5_SEG, NS)), _const((S5_SEG, NS)),
                  _const((1, W)), _const((W, W)), _const((1, W))],
        out_specs=[Block((TC, W), rev), Block((8, NS), c2), Block((8, NS), c2), Block((1, W), c2), Block((1, W), c2),
                   ANY, ANY, ANY, ANY, ANY],
        out_shape=[SDS((T, W), BF16), SDS((8, NS), F32), SDS((8, NS), F32), SDS((1, W), F32), SDS((1, W), F32),
                   SDS((8, 128, 512), F32), SDS((8, 128, 512), F32),
                   SDS((8, 512, 128), F32), SDS((8, 512, 128), F32), SDS((W, W), F32)],
        scratch_shapes=[pltpu.VMEM((TC, 2 * NS), F32), pltpu.VMEM((1, 2 * NS), F32),
                        pltpu.VMEM((8, 128, 512), F32), pltpu.VMEM((8, 128, 512), F32),
                        pltpu.VMEM((8, 512, 128), F32), pltpu.VMEM((8, 512, 128), F32), pltpu.VMEM((W, W), F32)],
        compiler_params=_cparams(("arbitrary",), VMEM_BIG),
    )(dya, proj, ypre, xs, perm, permt, bdbr, bdbi, bdcr, bdci, abr, abi, asr, asi, dvec, wglu, bglu)


def _bd_b(bk_t):
    C, P = S5_GROUP_WIDTH, S5_STATE
    t = jnp.transpose(bk_t, (1, 0, 2)).reshape(8, 8, C, P)
    eye = jnp.eye(8, dtype=t.dtype)
    return (t[:, :, :, None, :] * eye[None, :, None, :, None]).reshape(8, 8 * C, 8 * P)


def _bd_b_extract(m):
    C, P = S5_GROUP_WIDTH, S5_STATE
    t = m.reshape(8, 8, C, 8, P)
    d = jnp.stack([t[:, g, :, g, :] for g in range(8)], axis=1)
    return jnp.transpose(d.reshape(S5_GROUPS, C, P), (1, 0, 2))


def _bd_c(c):
    C, P = S5_GROUP_WIDTH, S5_STATE
    t = jnp.transpose(c, (0, 2, 1)).reshape(8, 8, P, C)
    eye = jnp.eye(8, dtype=t.dtype)
    return (t[:, :, :, None, :] * eye[None, :, None, :, None]).reshape(8, 8 * P, 8 * C)


def _bd_c_extract(m):
    C, P = S5_GROUP_WIDTH, S5_STATE
    t = m.reshape(8, 8, P, 8, C)
    d = jnp.stack([t[:, g, :, g, :] for g in range(8)], axis=1)
    return jnp.transpose(d.reshape(S5_GROUPS, P, C), (0, 2, 1))


def _perm_matrix():
    r = jnp.arange(S5_TC)
    src = (r % 8) * S5_SEG + r // 8
    return (src[:, None] == jnp.arange(S5_TC)[None, :]).astype(BF16)


def _coords():
    return lax.axis_index("x"), lax.axis_index("y"), lax.axis_index("c")


def _all_gather(name, arrs):
    n = len(arrs)

    def body(*refs):
        ins, outs = refs[:n], refs[n:2 * n]
        send_sems, recv_sems, local_sems = refs[2 * n:]
        x, y, c = _coords()
        me, sibling = (x, y, c), (x, y, 1 - c)
        chips = [(1 - x, y), (x, 1 - y), (1 - x, 1 - y)]

        def slot(p):
            return 4 * p[0] + 2 * p[1] + p[2]

        def copy(a, k, block, to, src=None):
            dst = outs[a].at[slot(block)]
            return pltpu.make_async_remote_copy(
                src_ref=dst if src is None else src, dst_ref=dst,
                send_sem=send_sems.at[a * 7 + k], recv_sem=recv_sems.at[a * 7 + k],
                device_id=to, device_id_type=MESH)

        mine = [pltpu.make_async_copy(ins[a], outs[a].at[slot(me)], local_sems.at[a]) for a in range(n)]
        for m in mine:
            m.start()
        first = []
        for a in range(n):
            first.append(copy(a, 0, me, sibling, src=ins[a]))
            first += [copy(a, 1 + j, me, (*chip, c), src=ins[a]) for j, chip in enumerate(chips)]
        for cp in first:
            cp.start()
        passed = []
        for j, chip in enumerate(chips):
            for a in range(n):
                copy(a, 1 + j, (*chip, c), me).wait_recv()
                fw = copy(a, 4 + j, (*chip, c), sibling)
                fw.start()
                passed.append(fw)
        for a in range(n):
            copy(a, 0, sibling, me).wait_recv()
            for j, chip in enumerate(chips):
                copy(a, 4 + j, (*chip, 1 - c), me).wait_recv()
        for cp in first + passed:
            cp.wait_send()
        for m in mine:
            m.wait()

    return pl.pallas_call(
        body, name=name,
        in_specs=[ANY] * n, out_specs=[ANY] * n,
        out_shape=[SDS((NDEV,) + a.shape, a.dtype) for a in arrs],
        scratch_shapes=[pltpu.SemaphoreType.DMA((7 * n,)), pltpu.SemaphoreType.DMA((7 * n,)),
                        pltpu.SemaphoreType.DMA((n,))],
    )(*arrs)


HBM = pl.BlockSpec(memory_space=pltpu.HBM)
SEM = pl.BlockSpec(memory_space=pltpu.SEMAPHORE)
EFFECT = pltpu.SideEffectType.DATAFLOW_SIDE_EFFECTING


def _peers7(x, y, c):
    return [(1 - x if fx else x, 1 - y if fy else y, 1 - c if fc else c)
            for fx in (0, 1) for fy in (0, 1) for fc in (0, 1) if fx or fy or fc]


def _slot(p):
    return 4 * p[0] + 2 * p[1] + p[2]


def _split_copies(src_refs, land_refs, send_sems, recv_sems, gather, mine):
    x, y, c = _coords()
    me = (x, y, c)
    out = []
    for a, (src, land) in enumerate(zip(src_refs, land_refs)):
        for k, p in enumerate(_peers7(x, y, c)):
            s = src if gather else src.at[_slot(p)]
            out.append(pltpu.make_async_remote_copy(
                src_ref=s, dst_ref=land.at[_slot(me) if mine else _slot(p)],
                send_sem=send_sems.at[a * 7 + k], recv_sem=recv_sems.at[a * 7 + k],
                device_id=p, device_id_type=MESH))
    return out


def _own_slab(shard):
    x, y, c = _coords()
    z = lax.empty((NDEV,) + shard.shape, shard.dtype)
    return lax.dynamic_update_slice(z, shard[None], (_slot((x, y, c)),) + (0,) * shard.ndim)


def _split_start(name, srcs, lands, gather):
    n = len(srcs)

    def body(*refs):
        src_refs, land_refs = refs[:n], refs[n:2 * n]
        send_sems, recv_sems = refs[2 * n], refs[2 * n + 1]
        token = refs[-1]
        for cp in _split_copies(src_refs, land_refs, send_sems, recv_sems, gather, True):
            cp.start()
        token[...] = jnp.zeros_like(token)

    thru = [pltpu.HBM(a.shape, a.dtype) for a in list(srcs) + list(lands)]
    res = pl.pallas_call(
        body, name=name,
        out_shape=(pltpu.SemaphoreType.DMA((7 * n,)), pltpu.SemaphoreType.DMA((7 * n,)), *thru, SDS((8, 128), F32)),
        in_specs=[HBM] * (2 * n),
        out_specs=(SEM, SEM, *([HBM] * (2 * n)), pl.BlockSpec(memory_space=pltpu.VMEM)),
        input_output_aliases={i: 2 + i for i in range(2 * n)},
        compiler_params=pltpu.CompilerParams(has_side_effects=EFFECT),
    )(*[pltpu.with_memory_space_constraint(a, pltpu.HBM) for a in list(srcs) + list(lands)])
    return res[0], res[1], list(res[2:2 + n]), list(res[2 + n:2 + 2 * n]), res[-1]


def _split_wait(name, started, after, gather):
    send_sems, recv_sems, srcs, lands, _ = started
    n = len(srcs)

    def body(*refs):
        src_refs, land_refs = refs[:n], refs[n:2 * n]
        s_sems, r_sems = refs[2 * n], refs[2 * n + 1]
        for cp in _split_copies(src_refs, land_refs, s_sems, r_sems, gather, False):
            cp.wait_send()
            cp.wait_recv()

    thru = [pltpu.HBM(a.shape, a.dtype) for a in list(srcs) + list(lands)]
    res = pl.pallas_call(
        body, name=name, out_shape=tuple(thru),
        in_specs=[HBM] * (2 * n) + [SEM, SEM, ANY], out_specs=tuple([HBM] * (2 * n)),
        input_output_aliases={i: i for i in range(2 * n)},
        compiler_params=pltpu.CompilerParams(has_side_effects=EFFECT),
    )(*srcs, *lands, send_sems, recv_sems, after)
    return list(res[n:])


def _adam_math(w, g, m, v):
    m = ADAM_B1 * m + (1.0 - ADAM_B1) * g
    v = ADAM_B2 * v + (1.0 - ADAM_B2) * (g * g)
    m_hat = m / (1.0 - ADAM_B1 ** ADAM_STEP)
    v_hat = v / (1.0 - ADAM_B2 ** ADAM_STEP)
    delta = -ADAM_LR * (m_hat / (jnp.sqrt(v_hat) + ADAM_EPS) + ADAM_WD * w)
    return delta, m, v


def _adam_sharded(name, recv, sub, w, m, v):
    R, Cc = w.shape
    tr = max(t for t in range(16, R + 1, 16) if R % t == 0 and t * Cc <= 256 * 1024)

    def body(*refs):
        parts = refs[:NDEV]
        w_ref, m_ref, v_ref, g_out, d_out, m_out, v_out = refs[NDEV:]
        g = parts[0][...].astype(F32)
        for p in parts[1:]:
            g = g + p[...].astype(F32)
        delta, mn, vn = _adam_math(w_ref[...], g, m_ref[...], v_ref[...])
        g_out[...] = g
        d_out[...] = delta
        m_out[...] = mn
        v_out[...] = vn

    if sub is None:
        pspecs = [Block((None, tr, Cc), functools.partial(lambda s, i: (s, i, 0), s)) for s in range(NDEV)]
    else:
        pspecs = [Block((None, None, tr, Cc), functools.partial(lambda s, i: (s, sub, i, 0), s)) for s in range(NDEV)]
    row = Block((tr, Cc), lambda i: (i, 0))
    o = SDS((R, Cc), F32)
    return pl.pallas_call(
        body, name=name, grid=(R // tr,),
        in_specs=pspecs + [row, row, row], out_specs=[row, row, row, row], out_shape=[o, o, o, o],
        compiler_params=_cparams(("arbitrary",), VMEM_BIG))(*([recv] * NDEV), w, m, v)


def _adam_small(groups):
    n = len(groups)

    def body(*refs):
        ins, outs = refs[:4 * n], refs[4 * n:]
        for a in range(n):
            p_ref, w_ref, m_ref, v_ref = ins[4 * a:4 * a + 4]
            g = p_ref[0]
            for s in range(1, NDEV):
                g = g + p_ref[s]
            delta, mn, vn = _adam_math(w_ref[...], g, m_ref[...], v_ref[...])
            for o, r in zip(outs[4 * a:4 * a + 4], (g, delta, mn, vn)):
                o[...] = r

    flat_in = [t for grp in groups for t in grp]
    out_shape = [SDS(grp[1].shape, F32) for grp in groups for _ in range(4)]
    res = pl.pallas_call(body, name="adam_small", out_shape=out_shape,
                         compiler_params=_cparams(None, VMEM_BIG))(*flat_in)
    return [tuple(res[4 * a:4 * a + 4]) for a in range(n)]


_TINY = ["ffn1_norm", "mix_norm", "s5_a_re", "s5_a_im", "s5_log_dt", "s5_d", "s5_b_glu", "sgu_ln_g", "sgu_ln_b",
         "sgu_b_s", "b_gate", "ffn2_norm", "final_norm"]
_ORDER = ["ffn1_norm", "ffn1_w_gate", "ffn1_w_up", "ffn1_w_down", "mix_norm", "w_in", "s5_a_re", "s5_a_im",
          "s5_log_dt", "s5_b_re", "s5_b_im", "s5_c_re", "s5_c_im", "s5_d", "s5_w_glu", "s5_b_glu", "sgu_ln_g",
          "sgu_ln_b", "sgu_w_s", "sgu_b_s", "w_branch_a", "w_branch_b", "w_gate", "b_gate", "w_out", "ffn2_norm",
          "ffn2_w_gate", "ffn2_w_up", "ffn2_w_down", "final_norm"]


def _step(x, tgt, W, M, V):
    T = x.shape[1]
    x0 = x[0]
    tgt0 = tgt[0]
    bf = lambda a: a.astype(BF16)

    def gather_start(name, shards):
        return _split_start(name, shards, [_own_slab(s) for s in shards], True)

    (wgu1,) = _all_gather("gather1", [jnp.stack([bf(W["ffn1_w_gate"][0]), bf(W["ffn1_w_up"][0])])])

    lr_, li_ = W["s5_a_re"][0], W["s5_a_im"][0]
    ldt_ = W["s5_log_dt"][0][:, None]
    brt = jnp.transpose(W["s5_b_re"][0], (2, 0, 1))
    bit = jnp.transpose(W["s5_b_im"][0], (2, 0, 1))
    abr, abi, pwr, pwi, bkr_t, bki_t = _s5_prep(lr_, li_, ldt_, brt, bit)
    bdbr, bdbi = bf(_bd_b(bkr_t)), bf(_bd_b(bki_t))
    bdcr, bdci = bf(_bd_c(W["s5_c_re"][0])), bf(_bd_c(-W["s5_c_im"][0]))
    flat = lambda a: a.reshape(1, S5_NS)
    s5a = (_perm_matrix(), _perm_matrix().T, bdbr, bdbi, bdcr, bdci, flat(abr), flat(abi),
           pwr.reshape(S5_SEG, S5_NS), pwi.reshape(S5_SEG, S5_NS),
           W["s5_d"][0].reshape(1, S5_WIDTH))
    blk = jnp.arange(MLP_CHUNK) // CHUNK
    mask = blk[:, None] >= blk[None, :]
    wsm = jnp.where(mask[None], W["sgu_w_s"][0], 0.0)
    wsm_b, wsmt_b = bf(wsm), bf(jnp.transpose(wsm, (0, 2, 1)))
    bst = jnp.broadcast_to(W["sgu_b_s"][0][:, :, None], (SGU_HEADS, MLP_CHUNK, 128))
    bgate2 = W["b_gate"].reshape(2, 1, D_MODEL)

    h1 = _rms_fwd("rms1", x0, W["ffn1_norm"])
    dep = (wgu1[0, 0, :1, :1] * 0).astype(BF16)

    def later(a):
        return bf(a) + dep[0]

    gs2 = gather_start("gather2_start", [later(W["ffn1_w_down"][0])])
    ab1, f1 = _ffn_up("ffn1_up", h1, wgu1, gs2[4])
    (wd1,) = _split_wait("gather2_wait", gs2, f1, True)
    dep = (wd1[0, :1, :1] * 0).astype(BF16)
    gs3 = gather_start("gather3_start", [later(W["w_in"][0]), later(W["s5_w_glu"][0])])
    x1 = _ffn_down("ffn1_down", f1, wd1, x0, after=gs3[4])
    h2 = _rms_fwd("rms2", x1, W["mix_norm"])
    win, wglu = _split_wait("gather3_wait", gs3, h2, True)
    wglu = wglu.reshape(S5_WIDTH, S5_WIDTH)
    s5c = s5a + (wglu, W["s5_b_glu"])
    dep = (win[0, :1, :1] * 0).astype(BF16)
    gs4 = gather_start("gather4_start", [later(W["w_gate"][0]), later(W["w_branch_a"][0]),
                                         later(W["w_branch_b"][0]), later(W["w_out"][0])])
    proj = _col_fwd("w_in", h2, win, after=gs4[4])
    ya, xs, ypre = _s5_fwd(proj, *s5c)
    dep = (ya[:1, :1] * 0).astype(BF16)
    gs5 = gather_start("gather5_start", [jnp.stack([later(W["ffn2_w_gate"][0]), later(W["ffn2_w_up"][0])])])
    yb = _sgu_fwd("sgu_fwd", proj, W["sgu_ln_g"] + gs5[4][:1, :1], W["sgu_ln_b"], wsm_b, bst)
    wgate, wba, wbb, wout = _split_wait("gather4_wait", gs4, yb, True)
    wout = wout.reshape(D_MODEL, D_MODEL)
    pa = _col_fwd("branch_a", ya, wba)
    pb = _col_fwd("branch_b", yb, wbb)
    gl = _gate_fwd("gate", h2, wgate, bgate2)
    merged = _merge_fwd("merge", pa, pb, gl)
    x2 = _plain_fwd_res("w_out", merged, wout, x1)
    h3 = _rms_fwd("rms3", x2, W["ffn2_norm"])
    (wgu2,) = _split_wait("gather5_wait", gs5, h3, True)
    dep = (wgu2[0, 0, :1, :1] * 0).astype(BF16)
    gs6 = gather_start("gather6_start", [later(W["ffn2_w_down"][0])])
    ab2, f2 = _ffn_up("ffn2_up", h3, wgu2, gs6[4])
    (wd2,) = _split_wait("gather6_wait", gs6, f2, True)
    x3 = _ffn_down("ffn2_down", f2, wd2, x2)
    loss_p, dx3b, dgf = _loss_head("loss_head", x3, W["final_norm"].reshape(1, D_MODEL), tgt0)

    def exchange_start(name, grads):
        x_, y_, c_ = _coords()
        me = _slot((x_, y_, c_))
        return _split_start(name, grads, [_own_slab(lax.dynamic_index_in_dim(g, me, 0, keepdims=False))
                                          for g in grads], False)

    dab2 = _ffn_down_bwd_act("ffn2_down_bwd_a", dx3b, wd2, ab2)
    g_wd2 = _ffn_down_bwd_w("ffn2_down_bwd_w", f2, dx3b)
    g_gu2 = _ffn_up_bwd_w("ffn2_up_bwd_w", h3, dab2)
    es1 = exchange_start("exchange1_start", [g_wd2, g_gu2])
    dh3 = _ffn_up_bwd_h("ffn2_up_bwd_h", dab2, wgu2, es1[4])
    dx2b, dg3 = _rms_bwd("rms3_bwd", dh3, x2, W["ffn2_norm"], dx3b, BF16)

    dmerged = _plain_bwd_a("w_out_bwd_a", dx2b, wout)
    g_wout = _plain_bwd_w("w_out_bwd_w", merged, dx2b)
    dpa, dpb, dgl, dbgate = _merge_bwd("merge_bwd", dmerged, pa, pb, gl)
    dya = _col_bwd_a("branch_a_bwd_a", dpa, wba)
    g_wba = _col_bwd_w("branch_a_bwd_w", ya, dpa, 256)
    dyb = _col_bwd_a("branch_b_bwd_a", dpb, wbb)
    g_wbb = _col_bwd_w("branch_b_bwd_w", yb, dpb, 256)
    dh2g = _gate_bwd_a("gate_bwd_a", dgl, wgate)
    g_wgate = _gate_bwd_w("gate_bwd_w", h2, dgl, 512)
    (dua, dar8, dai8, ddv, dbglu, dbdbr, dbdbi, dbdcr, dbdci, g_wglu) = _s5_bwd(dya, proj, ypre, xs, *s5c)
    dproj, dws, dbst, dlng, dlnb = _sgu_bwd("sgu_bwd", dyb, dua, proj, W["sgu_ln_g"], W["sgu_ln_b"],
                                            wsm_b, wsmt_b, bst)
    g_win = _col_bwd_w("w_in_bwd_w", h2, dproj, 384)
    g_wout3 = g_wout.reshape(NDEV, D_MODEL // NDEV, D_MODEL)
    g_wglu3 = g_wglu.astype(BF16).reshape(NDEV, S5_WIDTH // NDEV, S5_WIDTH)
    es2 = exchange_start("exchange2_start", [g_wout3, g_wba, g_wbb, g_wgate, g_wglu3, g_win])
    dh2 = _col_bwd_a("w_in_bwd_a", dproj, win, add=dh2g)
    dx1b, dgm = _rms_bwd("rms2_bwd", dh2, x1, W["mix_norm"] + es2[4][:1, :1], dx2b, BF16)

    dab1 = _ffn_down_bwd_act("ffn1_down_bwd_a", dx1b, wd1, ab1)
    g_gu1 = _ffn_up_bwd_w("ffn1_up_bwd_w", h1, dab1)
    es3 = exchange_start("exchange3_start", [g_gu1])
    g_wd1 = _ffn_down_bwd_w("ffn1_down_bwd_w", f1, dx1b, after=es3[4])
    es4 = exchange_start("exchange4_start", [g_wd1])
    dh1 = _ffn_up_bwd_h("ffn1_up_bwd_h", dab1, wgu1, es4[4])
    dx0, dg1 = _rms_bwd("rms1_bwd", dh1, x0, W["ffn1_norm"], dx1b, F32)

    dabr = jnp.sum(dar8, axis=0).reshape(S5_GROUPS, S5_STATE)
    dabi = jnp.sum(dai8, axis=0).reshape(S5_GROUPS, S5_STATE)
    d_lr, d_li, d_ldt, d_brt, d_bit = _s5_prep_bwd(lr_, li_, ldt_, brt, bit, dabr, dabi,
                                                   _bd_b_extract(dbdbr), _bd_b_extract(dbdbi))
    small_g = {
        "ffn1_norm": dg1, "mix_norm": dgm, "ffn2_norm": dg3, "final_norm": dgf,
        "s5_a_re": d_lr, "s5_a_im": d_li, "s5_log_dt": d_ldt,
        "s5_d": ddv, "s5_b_glu": dbglu, "sgu_ln_g": dlng, "sgu_ln_b": dlnb,
        "sgu_b_s": dbst[:, :, 0], "b_gate": dbgate,
    }
    to_cgp = lambda a: jnp.transpose(a[0], (2, 0, 1))
    from_cgp = lambda a: jnp.transpose(a, (1, 2, 0))[None]
    natural = [
        ("s5_b_re", d_brt, to_cgp, from_cgp), ("s5_b_im", d_bit, to_cgp, from_cgp),
        ("s5_c_re", _bd_c_extract(dbdcr), lambda a: a[0], lambda a: a[None]),
        ("s5_c_im", -_bd_c_extract(dbdci), lambda a: a[0], lambda a: a[None]),
        ("sgu_w_s", jnp.where(mask[None], dws, 0.0), lambda a: a[0], lambda a: a[None]),
    ]

    sizes = [W[n].size for n in _TINY]
    total = sum(sizes) + 1
    rows = -(-total // 128)
    rows = -(-rows // 8) * 8
    pad = rows * 128 - total

    def pack(d, extra):
        return jnp.concatenate([d[n].reshape(-1).astype(F32) for n in _TINY] + [extra, jnp.zeros((pad,), F32)]
                               ).reshape(rows, 128)

    G, Dl, Mn, Vn = {}, {}, {}, {}

    def adam(plan):
        last = None
        for n, recv, sub in plan:
            if sub is None:
                g, d, mn, vn = _adam_sharded("adam_" + n, recv, sub, W[n][0], M[n][0], V[n][0])
                G[n], Dl[n], Mn[n], Vn[n] = g[None], d[None], mn[None], vn[None]
            else:
                tr = jnp.transpose
                g, d, mn, vn = _adam_sharded("adam_" + n, recv, sub, tr(W[n][0]), tr(M[n][0]), tr(V[n][0]))
                G[n], Dl[n], Mn[n], Vn[n] = tr(g)[None], tr(d)[None], tr(mn)[None], tr(vn)[None]
            last = g
        return last

    r_wd2, r_gu2 = _split_wait("exchange1_wait", es1, dx0, False)
    done = adam([("ffn2_w_down", r_wd2, None), ("ffn2_w_gate", r_gu2, 0), ("ffn2_w_up", r_gu2, 1)])
    r_wout, r_wba, r_wbb, r_wgate, r_wglu, r_win = _split_wait("exchange2_wait", es2, done, False)
    done = adam([("w_out", r_wout, None), ("w_branch_a", r_wba, None), ("w_branch_b", r_wbb, None),
                 ("w_gate", r_wgate, None), ("s5_w_glu", r_wglu, None), ("w_in", r_win, None)])

    late = loss_p[0, :1] + 0.0 * done.reshape(-1)[:1]
    zero1 = jnp.zeros((1,), F32)
    parts = _all_gather("gather_small_grads", [pack(small_g, late)] + [g for _, g, _, _ in natural])
    groups = [(parts[0], pack(W, zero1), pack(M, zero1), pack(V, zero1))]
    groups += [(parts[1 + a], view(W[n]), view(M[n]), view(V[n])) for a, (n, _, view, _) in enumerate(natural)]
    res = _adam_small(groups)
    sg, sd, sm, sv = res[0]
    for (n, _, _, back), (g, d, mn, vn) in zip(natural, res[1:]):
        G[n], Dl[n], Mn[n], Vn[n] = back(g), back(d), back(mn), back(vn)

    def unpack(flat2d, into):
        flat = flat2d.reshape(-1)
        off = 0
        for n, s in zip(_TINY, sizes):
            into[n] = flat[off:off + s].reshape(W[n].shape)
            off += s
        return flat[off]

    loss = unpack(sg, G)
    unpack(sd, Dl)
    unpack(sm, Mn)
    unpack(sv, Vn)

    (r_gu1,) = _split_wait("exchange3_wait", es3, sg, False)
    done = adam([("ffn1_w_gate", r_gu1, 0), ("ffn1_w_up", r_gu1, 1)])
    (r_wd1,) = _split_wait("exchange4_wait", es4, done, False)
    adam([("ffn1_w_down", r_wd1, None)])

    return loss, dx0[None], G, Dl, Mn, Vn


def kernel(x, ffn1_norm, ffn1_w_gate, ffn1_w_up, ffn1_w_down, mix_norm, w_in, s5_a_re, s5_a_im, s5_log_dt, s5_b_re, s5_b_im, s5_c_re, s5_c_im, s5_d, s5_w_glu, s5_b_glu, sgu_ln_g, sgu_ln_b, sgu_w_s, sgu_b_s, w_branch_a, w_branch_b, w_gate, b_gate, w_out, ffn2_norm, ffn2_w_gate, ffn2_w_up, ffn2_w_down, final_norm, loss_target, m_ffn1_norm, m_ffn1_w_gate, m_ffn1_w_up, m_ffn1_w_down, m_mix_norm, m_w_in, m_s5_a_re, m_s5_a_im, m_s5_log_dt, m_s5_b_re, m_s5_b_im, m_s5_c_re, m_s5_c_im, m_s5_d, m_s5_w_glu, m_s5_b_glu, m_sgu_ln_g, m_sgu_ln_b, m_sgu_w_s, m_sgu_b_s, m_w_branch_a, m_w_branch_b, m_w_gate, m_b_gate, m_w_out, m_ffn2_norm, m_ffn2_w_gate, m_ffn2_w_up, m_ffn2_w_down, m_final_norm, v_ffn1_norm, v_ffn1_w_gate, v_ffn1_w_up, v_ffn1_w_down, v_mix_norm, v_w_in, v_s5_a_re, v_s5_a_im, v_s5_log_dt, v_s5_b_re, v_s5_b_im, v_s5_c_re, v_s5_c_im, v_s5_d, v_s5_w_glu, v_s5_b_glu, v_sgu_ln_g, v_sgu_ln_b, v_sgu_w_s, v_sgu_b_s, v_w_branch_a, v_w_branch_b, v_w_gate, v_b_gate, v_w_out, v_ffn2_norm, v_ffn2_w_gate, v_ffn2_w_up, v_ffn2_w_down, v_final_norm):
    a = locals()
    W = {n: a[n] for n in _ORDER}
    M = {n: a["m_" + n] for n in _ORDER}
    V = {n: a["v_" + n] for n in _ORDER}
    loss, gx, G, Dl, Mn, Vn = _step(x, loss_target, W, M, V)
    return (loss, gx, *[G[n] for n in _ORDER], *[Dl[n] for n in _ORDER], *[Mn[n] for n in _ORDER],
            *[Vn[n] for n in _ORDER])
```

```python
import functools
import math

import jax
import jax.numpy as jnp
from jax import lax
from jax.experimental import pallas as pl
from jax.experimental.pallas import tpu as pltpu

F32 = jnp.float32
BF16 = jnp.bfloat16
NDEV = 8
NORM_EPS = 1e-6
D_MODEL = 2048
D_FF = 5632
FF_SHARD = D_FF // NDEV
S5_WIDTH = 1024
S5_GROUPS = 64
S5_GROUP_WIDTH = 16
S5_STATE = 64
S5_NS = S5_GROUPS * S5_STATE
SGU_WIDTH = 1024
SGU_HEADS = 8
MLP_CHUNK = 128
CHUNK = 64
ADAM_LR, ADAM_B1, ADAM_B2, ADAM_EPS, ADAM_WD, ADAM_STEP = 0.001, 0.9, 0.999, 1e-08, 0.01, 10
S5_TC = 256
S5_SEG = S5_TC // 8
S5_LG = 512
VMEM_BIG = 56 * 1024 * 1024

MESH = pl.DeviceIdType.MESH
SDS = jax.ShapeDtypeStruct
Block = pl.BlockSpec
ANY = pl.BlockSpec(memory_space=pl.ANY)


def _cparams(sem=None, vmem=None):
    return pltpu.CompilerParams(dimension_semantics=sem, vmem_limit_bytes=vmem)


def _const(shape):
    nd = len(shape)
    return pl.BlockSpec(shape, lambda i: (0,) * nd, pipeline_mode=pl.Buffered(1))


def _sigmoid(x):
    return 0.5 * jnp.tanh(0.5 * x) + 0.5


_GELU_C = math.sqrt(2.0 / math.pi)


def _gelu(x):
    return 0.5 * x * (1.0 + jnp.tanh(_GELU_C * (x + 0.044715 * x * x * x)))


def _gelu_grad(x):
    t = jnp.tanh(_GELU_C * (x + 0.044715 * x * x * x))
    return 0.5 * (1.0 + t) + 0.5 * x * (1.0 - t * t) * _GELU_C * (1.0 + 3.0 * 0.044715 * x * x)


NN = (((1,), (0,)), ((), ()))
NT = (((1,), (1,)), ((), ()))
TN = (((0,), (0,)), ((), ()))


def _dot(a, b, dims=NN):
    return lax.dot_general(a, b, dims, preferred_element_type=F32)


def _matmul(name, a, b, extras, *, grid, a_spec, b_spec, extra_specs, out_shapes, out_specs, acc_shape,
            epilogue, dims=NN, nb=None, compute=None, after=None, vmem=VMEM_BIG):
    nk = grid[2]
    if after is not None:
        extras = tuple(extras) + (after,)
        extra_specs = list(extra_specs) + [Block((8, 128), lambda i, j, k: (0, 0))]
    ne, no = len(extras), len(out_shapes)
    nacc = nb or 1
    if compute is None:
        def compute(a_ref, b_ref, q):
            return _dot(a_ref[...], b_ref[q] if nb else b_ref[...], dims)

    def body(*refs):
        a_ref, b_ref = refs[0], refs[1]
        ex = refs[2:2 + ne]
        outs = refs[2 + ne:2 + ne + no]
        if nk == 1:
            epilogue([compute(a_ref, b_ref, q) for q in range(nacc)], ex, outs)
            return
        acc_ref = refs[2 + ne + no]
        k = pl.program_id(2)

        @pl.when(k == 0)
        def _():
            acc_ref[...] = jnp.zeros_like(acc_ref)

        for q in range(nacc):
            acc_ref[q] += compute(a_ref, b_ref, q)

        @pl.when(k == nk - 1)
        def _():
            epilogue([acc_ref[q] for q in range(nacc)], ex, outs)

    scratch = [] if nk == 1 else [pltpu.VMEM((nacc,) + tuple(acc_shape), F32)]
    res = pl.pallas_call(
        body, name=name, grid=grid,
        in_specs=[a_spec, b_spec] + list(extra_specs),
        out_specs=list(out_specs), out_shape=list(out_shapes), scratch_shapes=scratch,
        compiler_params=_cparams(("parallel", "parallel", "arbitrary"), vmem),
    )(a, b, *extras)
    return res


def _store(dtype_outs=None):
    def ep(accs, ex, outs):
        outs[0][...] = accs[0].astype(outs[0].dtype)
    return ep


def _tile(n, t):
    t = min(n, t)
    assert n % t == 0, (n, t)
    return t


def _ksum(kq, dims):
    def compute(a_ref, b_ref, _):
        part = _dot(a_ref[0], b_ref[0], dims)
        for q in range(1, kq):
            part = part + _dot(a_ref[q], b_ref[q], dims)
        return part
    return compute


def _ksum_lanes(kq, ns, dims):
    def compute(a_ref, b_ref, _):
        part = _dot(a_ref[:, 0:ns], b_ref[0], dims)
        for q in range(1, kq):
            part = part + _dot(a_ref[:, q * ns:(q + 1) * ns], b_ref[q], dims)
        return part
    return compute


def _wide_b(g):
    def compute(a_ref, b_ref, _):
        bw = b_ref[0] if g == 1 else jnp.concatenate([b_ref[q] for q in range(g)], axis=1)
        return _dot(a_ref[...], bw, NN)
    return compute


KQ_DOWN = 4
TT_DEEP = 2048


def _ffn_up(name, h, wgu, after=None):
    T, D = h.shape
    tm = _tile(T, 1024)

    def ep(accs, ex, outs):
        a, b = accs
        outs[0][0] = a.astype(BF16)
        outs[0][1] = b.astype(BF16)
        outs[1][...] = (a * _sigmoid(a) * b).astype(BF16)

    return _matmul(
        name, h, wgu, (), after=after, grid=(NDEV, T // tm, 1),
        a_spec=Block((tm, D), lambda j, i, k: (i, 0)),
        b_spec=Block((None, 2, D, FF_SHARD), lambda j, i, k: (j, 0, 0, 0)),
        extra_specs=(),
        out_shapes=[SDS((NDEV, 2, T, FF_SHARD), BF16), SDS((NDEV, T, FF_SHARD), BF16)],
        out_specs=[Block((None, 2, tm, FF_SHARD), lambda j, i, k: (j, 0, i, 0)),
                   Block((None, tm, FF_SHARD), lambda j, i, k: (j, i, 0))],
        acc_shape=(tm, FF_SHARD), dims=NN, nb=2, epilogue=ep)


def _ffn_down(name, f, wd, xres, after=None):
    _, T, _ = f.shape
    tm, tn = _tile(T, 1024), 1024

    def ep(accs, ex, outs):
        outs[0][...] = ex[0][...] + 0.5 * accs[0]

    return _matmul(
        name, f, wd, (xres,), after=after, grid=(T // tm, D_MODEL // tn, NDEV // KQ_DOWN),
        a_spec=Block((KQ_DOWN, tm, FF_SHARD), lambda i, j, k: (k, i, 0)),
        b_spec=Block((KQ_DOWN, FF_SHARD, tn), lambda i, j, k: (k, 0, j)),
        extra_specs=[Block((tm, tn), lambda i, j, k: (i, j))],
        out_shapes=[SDS((T, D_MODEL), F32)],
        out_specs=[Block((tm, tn), lambda i, j, k: (i, j))],
        acc_shape=(tm, tn), compute=_ksum(KQ_DOWN, NN), epilogue=ep)[0]


def _ffn_down_bwd_act(name, dyb, wd, ab, after=None):
    T, D = dyb.shape
    tm = _tile(T, 1024)
    def ep(accs, ex, outs):
        df = accs[0]
        a = ex[0][0].astype(F32)
        b = ex[0][1].astype(F32)
        hs = 0.5 * _sigmoid(a)
        outs[0][0] = (df * b * hs * (1.0 + a * (1.0 - 2.0 * hs))).astype(BF16)
        outs[0][1] = (df * a * hs).astype(BF16)

    return _matmul(
        name, dyb, wd, (ab,), after=after, grid=(NDEV, T // tm, 1),
        a_spec=Block((tm, D), lambda j, i, k: (i, 0)),
        b_spec=Block((None, FF_SHARD, D), lambda j, i, k: (j, 0, 0)),
        extra_specs=[Block((None, 2, tm, FF_SHARD), lambda j, i, k: (j, 0, i, 0))],
        out_shapes=[SDS((NDEV, 2, T, FF_SHARD), BF16)],
        out_specs=[Block((None, 2, tm, FF_SHARD), lambda j, i, k: (j, 0, i, 0))],
        acc_shape=(tm, FF_SHARD), dims=NT, epilogue=ep)[0]


def _ffn_down_bwd_w(name, f, dyb, after=None):
    _, T, _ = f.shape
    tt, tn = _tile(T, TT_DEEP), 1024

    def ep(accs, ex, outs):
        outs[0][...] = (0.5 * accs[0]).astype(BF16)

    return _matmul(
        name, f, dyb, (), after=after, grid=(NDEV, D_MODEL // tn, T // tt),
        a_spec=Block((None, tt, FF_SHARD), lambda j, n, k: (j, k, 0)),
        b_spec=Block((tt, tn), lambda j, n, k: (k, n)),
        extra_specs=(),
        out_shapes=[SDS((NDEV, FF_SHARD, D_MODEL), BF16)],
        out_specs=[Block((None, FF_SHARD, tn), lambda j, n, k: (j, 0, n))],
        acc_shape=(FF_SHARD, tn), dims=TN, epilogue=ep)[0]


def _ffn_up_bwd_h(name, dab, wgu, after):
    _, _, T, _ = dab.shape
    tm = _tile(T, 1024)
    return _matmul(
        name, dab, wgu, (), after=after, grid=(T // tm, 1, NDEV),
        a_spec=Block((None, 2, tm, FF_SHARD), lambda i, j, k: (k, 0, i, 0)),
        b_spec=Block((None, 2, D_MODEL, FF_SHARD), lambda i, j, k: (k, 0, 0, 0)),
        extra_specs=(),
        out_shapes=[SDS((T, D_MODEL), BF16)],
        out_specs=[Block((tm, D_MODEL), lambda i, j, k: (i, 0))],
        acc_shape=(tm, D_MODEL), compute=_ksum(2, NT), epilogue=_store())[0]


def _ffn_up_bwd_w(name, h, dab):
    T, D = h.shape
    tt, tn = _tile(T, TT_DEEP), 1024

    def ep(accs, ex, outs):
        outs[0][0] = accs[0].astype(BF16)
        outs[0][1] = accs[1].astype(BF16)

    return _matmul(
        name, dab, h, (), grid=(NDEV, D // tn, T // tt),
        a_spec=Block((None, 2, tt, FF_SHARD), lambda j, n, k: (j, 0, k, 0)),
        b_spec=Block((tt, tn), lambda j, n, k: (k, n)),
        extra_specs=(),
        out_shapes=[SDS((NDEV, 2, FF_SHARD, D), BF16)],
        out_specs=[Block((None, 2, FF_SHARD, tn), lambda j, n, k: (j, 0, 0, n))],
        acc_shape=(FF_SHARD, tn), nb=2,
        compute=lambda a_ref, b_ref, q: _dot(a_ref[q], b_ref[...], TN), epilogue=ep)[0]


def _shards_per_step(ns):
    return max(g for g in (1, 2, 4, 8) if g * ns <= 2048)


def _split_lanes(g, ns):
    def ep(accs, ex, outs):
        for q in range(g):
            outs[0][q] = accs[0][:, q * ns:(q + 1) * ns].astype(outs[0].dtype)
    return ep


def _col_fwd(name, a, w, out_dtype=BF16, after=None):
    T, K = a.shape
    ns = w.shape[2]
    g = _shards_per_step(ns)
    tm = _tile(T, 1024)
    return _matmul(
        name, a, w, (), after=after, grid=(NDEV // g, T // tm, 1),
        a_spec=Block((tm, K), lambda j, i, k: (i, 0)),
        b_spec=Block((g, K, ns), lambda j, i, k: (j, 0, 0)),
        extra_specs=(),
        out_shapes=[SDS((T, NDEV * ns), out_dtype)],
        out_specs=[Block((tm, g * ns), lambda j, i, k: (i, j))],
        acc_shape=(tm, g * ns), compute=_wide_b(g), epilogue=_store())[0]


def _col_bwd_a(name, dy, w, add=None):
    T = dy.shape[0]
    _, K, ns = w.shape
    tm, tn = _tile(T, 1024), _tile(K, 1024)

    def ep(accs, ex, outs):
        r = accs[0]
        if add is not None:
            r = r + ex[0][...].astype(F32)
        outs[0][...] = r.astype(BF16)

    extras = () if add is None else (add,)
    return _matmul(
        name, dy, w, extras, grid=(T // tm, K // tn, 1),
        a_spec=Block((tm, NDEV * ns), lambda i, j, k: (i, 0)),
        b_spec=Block((NDEV, tn, ns), lambda i, j, k: (0, j, 0)),
        extra_specs=[Block((tm, tn), lambda i, j, k: (i, j))] * len(extras),
        out_shapes=[SDS((T, K), BF16)],
        out_specs=[Block((tm, tn), lambda i, j, k: (i, j))],
        acc_shape=(tm, tn), compute=_ksum_lanes(NDEV, ns, NT), epilogue=ep)[0]


def _col_bwd_w(name, a, dy, ns):
    T, K = a.shape
    g = _shards_per_step(ns)
    tt, tr = _tile(T, TT_DEEP), _tile(K, 1024)
    return _matmul(
        name, a, dy, (), grid=(NDEV // g, K // tr, T // tt),
        a_spec=Block((tt, tr), lambda j, n, k: (k, n)),
        b_spec=Block((tt, g * ns), lambda j, n, k: (k, j)),
        extra_specs=(),
        out_shapes=[SDS((NDEV, K, ns), BF16)],
        out_specs=[Block((g, tr, ns), lambda j, n, k: (j, n, 0))],
        acc_shape=(tr, g * ns), dims=TN, epilogue=_split_lanes(g, ns))[0]


def _gate_fwd(name, h, w, bias):
    T, K = h.shape
    ns = w.shape[2]
    g = 2
    per = D_MODEL // (g * ns)
    tm = _tile(T, 1024)

    def ep(accs, ex, outs):
        outs[0][...] = (accs[0] + ex[0][...]).astype(BF16)

    return _matmul(
        name, h, w, (bias,), grid=(NDEV // g, T // tm, 1),
        a_spec=Block((tm, K), lambda j, i, k: (i, 0)),
        b_spec=Block((g, K, ns), lambda j, i, k: (j, 0, 0)),
        extra_specs=[Block((None, 1, g * ns), lambda j, i, k: (j // per, 0, j % per))],
        out_shapes=[SDS((2, T, D_MODEL), BF16)],
        out_specs=[Block((None, tm, g * ns), lambda j, i, k: (j // per, i, j % per))],
        acc_shape=(tm, g * ns), compute=_wide_b(g), epilogue=ep)[0]


def _gate_bwd_a(name, dgl, w):
    _, T, _ = dgl.shape
    _, K, ns = w.shape
    per = D_MODEL // ns
    tm, tn = _tile(T, 1024), 1024

    def compute(a_ref, b_ref, _):
        part = None
        for q in range(NDEV):
            d = _dot(a_ref[q // per, :, (q % per) * ns:(q % per + 1) * ns], b_ref[q], NT)
            part = d if part is None else part + d
        return part

    return _matmul(
        name, dgl, w, (), grid=(T // tm, K // tn, 1),
        a_spec=Block((2, tm, D_MODEL), lambda i, j, k: (0, i, 0)),
        b_spec=Block((NDEV, tn, ns), lambda i, j, k: (0, j, 0)),
        extra_specs=(),
        out_shapes=[SDS((T, K), BF16)],
        out_specs=[Block((tm, tn), lambda i, j, k: (i, j))],
        acc_shape=(tm, tn), compute=compute, epilogue=_store())[0]


def _gate_bwd_w(name, h, dgl, ns):
    T, K = h.shape
    g = 2
    per = D_MODEL // (g * ns)
    tt, tr = _tile(T, TT_DEEP), 1024
    return _matmul(
        name, h, dgl, (), grid=(NDEV // g, K // tr, T // tt),
        a_spec=Block((tt, tr), lambda j, n, k: (k, n)),
        b_spec=Block((None, tt, g * ns), lambda j, n, k: (j // per, k, j % per)),
        extra_specs=(),
        out_shapes=[SDS((NDEV, K, ns), BF16)],
        out_specs=[Block((g, tr, ns), lambda j, n, k: (j, n, 0))],
        acc_shape=(tr, g * ns), dims=TN, epilogue=_split_lanes(g, ns))[0]


def _plain_fwd_res(name, a, w, xres):
    T, K = a.shape
    N = w.shape[1]
    tm, tn = _tile(T, 1024), _tile(N, 1024)

    def ep(accs, ex, outs):
        outs[0][...] = ex[0][...] + accs[0]

    return _matmul(
        name, a, w, (xres,), grid=(T // tm, N // tn, 1),
        a_spec=Block((tm, K), lambda i, j, k: (i, 0)),
        b_spec=Block((K, tn), lambda i, j, k: (0, j)),
        extra_specs=[Block((tm, tn), lambda i, j, k: (i, j))],
        out_shapes=[SDS((T, N), F32)],
        out_specs=[Block((tm, tn), lambda i, j, k: (i, j))],
        acc_shape=(tm, tn), dims=NN, nb=None, epilogue=ep)[0]


def _plain_bwd_a(name, dy, w):
    T, N = dy.shape
    K = w.shape[0]
    tm, tn = _tile(T, 1024), _tile(K, 1024)
    return _matmul(
        name, dy, w, (), grid=(T // tm, K // tn, 1),
        a_spec=Block((tm, N), lambda i, j, k: (i, 0)),
        b_spec=Block((tn, N), lambda i, j, k: (j, 0)),
        extra_specs=(),
        out_shapes=[SDS((T, K), BF16)],
        out_specs=[Block((tm, tn), lambda i, j, k: (i, j))],
        acc_shape=(tm, tn), dims=NT, nb=None, epilogue=_store())[0]


def _plain_bwd_w(name, a, dy):
    T, K = a.shape
    N = dy.shape[1]
    tt, tr, tn = _tile(T, TT_DEEP), _tile(K, 1024), _tile(N, 1024)
    return _matmul(
        name, a, dy, (), grid=(K // tr, N // tn, T // tt),
        a_spec=Block((tt, tr), lambda m, n, k: (k, m)),
        b_spec=Block((tt, tn), lambda m, n, k: (k, n)),
        extra_specs=(),
        out_shapes=[SDS((K, N), BF16)],
        out_specs=[Block((tr, tn), lambda m, n, k: (m, n))],
        acc_shape=(tr, tn), dims=TN, nb=None, epilogue=_store())[0]


def _rms_fwd(name, x, g):
    T, D = x.shape
    tm = _tile(T, 512)

    def body(x_ref, g_ref, h_ref):
        xv = x_ref[...]
        r = lax.rsqrt(jnp.mean(xv * xv, axis=-1, keepdims=True) + NORM_EPS)
        h_ref[...] = (xv * r * g_ref[...]).astype(BF16)

    return pl.pallas_call(
        body, name=name, grid=(T // tm,),
        in_specs=[Block((tm, D), lambda i: (i, 0)), Block((1, D), lambda i: (0, 0))],
        out_specs=Block((tm, D), lambda i: (i, 0)), out_shape=SDS((T, D), BF16),
        compiler_params=_cparams(("arbitrary",), VMEM_BIG))(x, g)


def _rms_bwd(name, dh, x, g, dxin, out_dtype):
    T, D = x.shape
    tm = _tile(T, 512)

    def body(dh_ref, x_ref, g_ref, dxin_ref, dx_ref, dg_ref):
        i = pl.program_id(0)
        xv = x_ref[...]
        dh = dh_ref[...].astype(F32)
        r = lax.rsqrt(jnp.mean(xv * xv, axis=-1, keepdims=True) + NORM_EPS)
        xh = xv * r
        gd = dh * g_ref[...]
        dx = dxin_ref[...].astype(F32) + r * (gd - xh * jnp.mean(gd * xh, axis=-1, keepdims=True))
        dx_ref[...] = dx.astype(out_dtype)
        dgp = jnp.sum(dh * xh, axis=0, keepdims=True)

        @pl.when(i == 0)
        def _():
            dg_ref[...] = dgp

        @pl.when(i > 0)
        def _():
            dg_ref[...] += dgp

    row = Block((tm, D), lambda i: (i, 0))
    vec = Block((1, D), lambda i: (0, 0))
    return pl.pallas_call(
        body, name=name, grid=(T // tm,),
        in_specs=[row, row, vec, row], out_specs=[row, vec],
        out_shape=[SDS((T, D), out_dtype), SDS((1, D), F32)],
        compiler_params=_cparams(("arbitrary",), VMEM_BIG))(dh, x, g, dxin)


def _loss_head(name, x, g, tgt):
    T, D = x.shape
    tm = _tile(T, 512)

    def body(x_ref, g_ref, t_ref, loss_ref, dxb_ref, dg_ref):
        i = pl.program_id(0)
        xv = x_ref[...]
        gv = g_ref[...]
        r = lax.rsqrt(jnp.mean(xv * xv, axis=-1, keepdims=True) + NORM_EPS)
        xh = xv * r
        err = xh * gv - t_ref[...]
        lp = 0.5 * jnp.sum(jnp.mean(err * err, axis=-1, keepdims=True), axis=0, keepdims=True)
        dout = err * (1.0 / D)
        gd = dout * gv
        dx = r * (gd - xh * jnp.mean(gd * xh, axis=-1, keepdims=True))
        dxb_ref[...] = dx.astype(BF16)
        dgp = jnp.sum(dout * xh, axis=0, keepdims=True)
        lpb = jnp.broadcast_to(lp, (1, 128))

        @pl.when(i == 0)
        def _():
            dg_ref[...] = dgp
            loss_ref[...] = lpb

        @pl.when(i > 0)
        def _():
            dg_ref[...] += dgp
            loss_ref[...] += lpb

    row = Block((tm, D), lambda i: (i, 0))
    vec = Block((1, D), lambda i: (0, 0))
    return pl.pallas_call(
        body, name=name, grid=(T // tm,),
        in_specs=[row, vec, row], out_specs=[Block((1, 128), lambda i: (0, 0)), row, vec],
        out_shape=[SDS((1, 128), F32), SDS((T, D), BF16), SDS((1, D), F32)],
        compiler_params=_cparams(("arbitrary",), VMEM_BIG))(x, g, tgt)


def _merge_fwd(name, pa, pb, gl):
    T, D = pa.shape
    tm = _tile(T, 512)

    def body(pa_ref, pb_ref, gl_ref, o_ref):
        ga = _sigmoid(gl_ref[0].astype(F32))
        gb = _sigmoid(gl_ref[1].astype(F32))
        o_ref[...] = (ga * pa_ref[...].astype(F32) + gb * pb_ref[...].astype(F32)).astype(BF16)

    row = Block((tm, D), lambda i: (i, 0))
    return pl.pallas_call(
        body, name=name, grid=(T // tm,),
        in_specs=[row, row, Block((2, tm, D), lambda i: (0, i, 0))], out_specs=row,
        out_shape=SDS((T, D), BF16), compiler_params=_cparams(("arbitrary",), VMEM_BIG))(pa, pb, gl)


def _merge_bwd(name, dm, pa, pb, gl):
    T, D = pa.shape
    tm = _tile(T, 512)

    def body(dm_ref, pa_ref, pb_ref, gl_ref, dpa_ref, dpb_ref, dgl_ref, db_ref):
        i = pl.program_id(0)
        dmv = dm_ref[...].astype(F32)
        ga = _sigmoid(gl_ref[0].astype(F32))
        gb = _sigmoid(gl_ref[1].astype(F32))
        dpa_ref[...] = (dmv * ga).astype(BF16)
        dpb_ref[...] = (dmv * gb).astype(BF16)
        dga = dmv * pa_ref[...].astype(F32) * ga * (1.0 - ga)
        dgb = dmv * pb_ref[...].astype(F32) * gb * (1.0 - gb)
        dgl_ref[0] = dga.astype(BF16)
        dgl_ref[1] = dgb.astype(BF16)
        sa = jnp.sum(dga, axis=0, keepdims=True)
        sb = jnp.sum(dgb, axis=0, keepdims=True)

        @pl.when(i == 0)
        def _():
            db_ref[0] = sa
            db_ref[1] = sb

        @pl.when(i > 0)
        def _():
            db_ref[0] += sa
            db_ref[1] += sb

    row = Block((tm, D), lambda i: (i, 0))
    two = Block((2, tm, D), lambda i: (0, i, 0))
    return pl.pallas_call(
        body, name=name, grid=(T // tm,),
        in_specs=[row, row, row, two], out_specs=[row, row, two, Block((2, 1, D), lambda i: (0, 0, 0))],
        out_shape=[SDS((T, D), BF16), SDS((T, D), BF16), SDS((2, T, D), BF16), SDS((2, 1, D), F32)],
        compiler_params=_cparams(("arbitrary",), VMEM_BIG))(dm, pa, pb, gl)


def _sgu_core(ur, vr, lng, lnb, ws_ref, bs_ref):
    tm = ur.shape[0]
    gu = _gelu(ur)
    gv = _gelu(vr)
    mu = jnp.mean(gv, axis=-1, keepdims=True)
    cen = gv - mu
    rstd = lax.rsqrt(jnp.mean(cen * cen, axis=-1, keepdims=True) + NORM_EPS)
    xhat = cen * rstd
    vn = (xhat * lng + lnb).astype(BF16)
    rows = []
    for n in range(tm // MLP_CHUNK):
        cols = []
        for h in range(SGU_HEADS):
            blk = vn[n * MLP_CHUNK:(n + 1) * MLP_CHUNK, h * 128:(h + 1) * 128]
            cols.append(_dot(ws_ref[h], blk) + bs_ref[h])
        rows.append(jnp.concatenate(cols, axis=1))
    mixed = jnp.concatenate(rows, axis=0) if len(rows) > 1 else rows[0]
    return gu, xhat, rstd, vn, mixed


def _sgu_fwd(name, proj, lng, lnb, wsm, bst):
    T = proj.shape[0]
    W = SGU_WIDTH
    tm = _tile(T, 512)

    def body(u_ref, v_ref, lng_ref, lnb_ref, ws_ref, bs_ref, o_ref):
        gu, _, _, _, mixed = _sgu_core(u_ref[...].astype(F32), v_ref[...].astype(F32), lng_ref[...], lnb_ref[...],
                                       ws_ref, bs_ref)
        o_ref[...] = (gu * mixed).astype(BF16)

    vec = Block((1, W), lambda i: (0, 0))
    return pl.pallas_call(
        body, name=name, grid=(T // tm,),
        in_specs=[Block((tm, W), lambda i: (i, 1)), Block((tm, W), lambda i: (i, 2)), vec, vec,
                  Block((SGU_HEADS, 128, 128), lambda i: (0, 0, 0)), Block((SGU_HEADS, 128, 128), lambda i: (0, 0, 0))],
        out_specs=Block((tm, W), lambda i: (i, 0)), out_shape=SDS((T, W), BF16),
        compiler_params=_cparams(("arbitrary",), VMEM_BIG))(proj, proj, lng, lnb, wsm, bst)


def _sgu_bwd(name, dyb, dua, proj, lng, lnb, wsm, wsmt, bst):
    T = proj.shape[0]
    W = SGU_WIDTH
    tm = _tile(T, 512)

    def body(dy_ref, dua_ref, u_ref, v_ref, lng_ref, lnb_ref, ws_ref, wst_ref, bs_ref,
             duv_ref, dws_ref, dbs_ref, dlng_ref, dlnb_ref):
        i = pl.program_id(0)
        duv_ref[:, :W] = dua_ref[...]
        ur = u_ref[...].astype(F32)
        vr = v_ref[...].astype(F32)
        lng_v = lng_ref[...]
        gu, xhat, rstd, vn, mixed = _sgu_core(ur, vr, lng_v, lnb_ref[...], ws_ref, bs_ref)
        dy = dy_ref[...].astype(F32)
        dgu = dy * mixed
        dmix = dy * gu
        dmb = dmix.astype(BF16)
        dws_p, dbs_p, rows = [], [], []
        for h in range(SGU_HEADS):
            acc_w = jnp.zeros((128, 128), F32)
            acc_b = jnp.zeros((128, 1), F32)
            for n in range(tm // MLP_CHUNK):
                r0 = n * MLP_CHUNK
                dmt = dmb[r0:r0 + MLP_CHUNK, h * 128:(h + 1) * 128]
                acc_w = acc_w + _dot(dmt, vn[r0:r0 + MLP_CHUNK, h * 128:(h + 1) * 128], NT)
                acc_b = acc_b + jnp.sum(dmix[r0:r0 + MLP_CHUNK, h * 128:(h + 1) * 128], axis=1, keepdims=True)
            dws_p.append(acc_w)
            dbs_p.append(jnp.broadcast_to(acc_b, (128, 128)))
        for n in range(tm // MLP_CHUNK):
            r0 = n * MLP_CHUNK
            rows.append(jnp.concatenate(
                [_dot(wst_ref[h], dmb[r0:r0 + MLP_CHUNK, h * 128:(h + 1) * 128]) for h in range(SGU_HEADS)], axis=1))
        dvn = jnp.concatenate(rows, axis=0) if len(rows) > 1 else rows[0]
        dlng_p = jnp.sum(dvn * xhat, axis=0, keepdims=True)
        dlnb_p = jnp.sum(dvn, axis=0, keepdims=True)
        dxh = dvn * lng_v
        dgv = rstd * (dxh - jnp.mean(dxh, axis=-1, keepdims=True)
                      - xhat * jnp.mean(dxh * xhat, axis=-1, keepdims=True))
        duv_ref[:, W:2 * W] = (dgu * _gelu_grad(ur)).astype(BF16)
        duv_ref[:, 2 * W:] = (dgv * _gelu_grad(vr)).astype(BF16)

        @pl.when(i == 0)
        def _():
            for h in range(SGU_HEADS):
                dws_ref[h] = dws_p[h]
                dbs_ref[h] = dbs_p[h]
            dlng_ref[...] = dlng_p
            dlnb_ref[...] = dlnb_p

        @pl.when(i > 0)
        def _():
            for h in range(SGU_HEADS):
                dws_ref[h] += dws_p[h]
                dbs_ref[h] += dbs_p[h]
            dlng_ref[...] += dlng_p
            dlnb_ref[...] += dlnb_p

    vec = Block((1, W), lambda i: (0, 0))
    wsb = Block((SGU_HEADS, 128, 128), lambda i: (0, 0, 0))
    hsq = SDS((SGU_HEADS, 128, 128), F32)
    return pl.pallas_call(
        body, name=name, grid=(T // tm,),
        in_specs=[Block((tm, W), lambda i: (i, 0)), Block((tm, W), lambda i: (i, 0)),
                  Block((tm, W), lambda i: (i, 1)), Block((tm, W), lambda i: (i, 2)),
                  vec, vec, wsb, wsb, wsb],
        out_specs=[Block((tm, 3 * W), lambda i: (i, 0)), wsb, wsb, vec, vec],
        out_shape=[SDS((T, 3 * W), BF16), hsq, hsq, SDS((1, W), F32), SDS((1, W), F32)],
        compiler_params=_cparams(("arbitrary",), VMEM_BIG))(dyb, dua, proj, proj, lng, lnb, wsm, wsmt, bst)


def _s5_disc(lr, li, ldt, brt, bit):
    dt = jnp.exp(ldt)
    decay = jnp.exp(lr * dt)
    abr = decay * jnp.cos(li * dt)
    abi = decay * jnp.sin(li * dt)
    denom = lr * lr + li * li
    nr = abr - 1.0
    ni = abi
    kr = (nr * lr + ni * li) / denom
    ki = (ni * lr - nr * li) / denom
    bkr = kr[None] * brt - ki[None] * bit
    bki = kr[None] * bit + ki[None] * brt
    return abr, abi, bkr, bki


def _s5_prep(lr, li, ldt, brt, bit):
    G, P, C = S5_GROUPS, S5_STATE, S5_GROUP_WIDTH

    def body(lr_ref, li_ref, ldt_ref, br_ref, bi_ref, abr_ref, abi_ref, pwr_ref, pwi_ref, bkr_ref, bki_ref):
        lr_, li_, ldt_ = lr_ref[...], li_ref[...], ldt_ref[...]
        res = _s5_disc(lr_, li_, ldt_, br_ref[...], bi_ref[...])
        for o, r in zip((abr_ref, abi_ref, bkr_ref, bki_ref), res):
            o[...] = r
        dt = jnp.exp(ldt_)
        n = lax.broadcasted_iota(jnp.int32, (S5_SEG, G, P), 0).astype(F32) + 1.0
        dec = jnp.exp((lr_ * dt)[None] * n)
        ang = (li_ * dt)[None] * n
        pwr_ref[...] = dec * jnp.cos(ang)
        pwi_ref[...] = dec * jnp.sin(ang)

    gp = SDS((G, P), F32)
    sgp = SDS((S5_SEG, G, P), F32)
    cgp = SDS((C, G, P), F32)
    return pl.pallas_call(body, name="s5_prep", out_shape=[gp, gp, sgp, sgp, cgp, cgp])(lr, li, ldt, brt, bit)


def _s5_prep_bwd(lr, li, ldt, brt, bit, dabr, dabi, dbkr, dbki):
    G, P, C = S5_GROUPS, S5_STATE, S5_GROUP_WIDTH

    def body(lr_ref, li_ref, ldt_ref, br_ref, bi_ref, dabr_ref, dabi_ref, dbkr_ref, dbki_ref,
             o_lr, o_li, o_ldt, o_br, o_bi):
        _, pull = jax.vjp(_s5_disc, lr_ref[...], li_ref[...], ldt_ref[...], br_ref[...], bi_ref[...])
        g = pull((dabr_ref[...], dabi_ref[...], dbkr_ref[...], dbki_ref[...]))
        for o, r in zip((o_lr, o_li, o_ldt, o_br, o_bi), g):
            o[...] = r

    gp = SDS((G, P), F32)
    cgp = SDS((C, G, P), F32)
    return pl.pallas_call(body, name="s5_prep_bwd", out_shape=[gp, gp, SDS((G, 1), F32), cgp, cgp])(
        lr, li, ldt, brt, bit, dabr, dabi, dbkr, dbki)


def _s5_scan(buf_ref, ar_row, ai_row, pwr_ref, pwi_ref, carry_ref, LG, xs_ref=None, dar_ref=None, dai_ref=None):
    reverse = xs_ref is not None
    NS, SEG = S5_NS, S5_SEG
    sgn = -1.0 if reverse else 1.0
    for lg in range(NS // LG):
        cr = slice(lg * LG, (lg + 1) * LG)
        ci = slice(NS + lg * LG, NS + (lg + 1) * LG)
        ar1, ai1 = ar_row[:, cr], sgn * ai_row[:, cr]
        asr1, asi1 = pwr_ref[SEG - 1:SEG, cr], sgn * pwi_ref[SEG - 1:SEG, cr]
        ar = jnp.broadcast_to(ar1, (8, LG))
        ai = jnp.broadcast_to(ai1, (8, LG))

        def step_of(j):
            return (SEG - 1 - j) if reverse else j

        def p1(j, st):
            sr, si = st
            rows = pl.ds(pl.multiple_of(step_of(j) * 8, 8), 8)
            nr = ar * sr - ai * si + buf_ref[rows, cr]
            ni = ar * si + ai * sr + buf_ref[rows, ci]
            buf_ref[rows, cr] = nr
            buf_ref[rows, ci] = ni
            return nr, ni

        z = jnp.zeros((8, LG), F32)
        er, ei = lax.fori_loop(0, SEG, p1, (z, z), unroll=True)
        c_r = carry_ref[:, cr]
        c_i = carry_ref[:, ci]
        cs_r, cs_i = [None] * 8, [None] * 8
        order = range(7, -1, -1) if reverse else range(8)
        for s in order:
            cs_r[s], cs_i[s] = c_r, c_i
            e_r, e_i = er[s:s + 1], ei[s:s + 1]
            c_r, c_i = e_r + asr1 * c_r - asi1 * c_i, e_i + asr1 * c_i + asi1 * c_r
        carry_ref[:, cr] = c_r
        carry_ref[:, ci] = c_i
        cmr = jnp.concatenate(cs_r, axis=0)
        cmi = jnp.concatenate(cs_i, axis=0)

        def carried(j):
            pr = pwr_ref[pl.ds(j, 1), cr]
            pi = sgn * pwi_ref[pl.ds(j, 1), cr]
            return pr * cmr - pi * cmi, pr * cmi + pi * cmr

        if not reverse:
            def p2(j, st):
                rows = pl.ds(pl.multiple_of(j * 8, 8), 8)
                wr, wi = carried(j)
                buf_ref[rows, cr] += wr
                buf_ref[rows, ci] += wi
                return st

            lax.fori_loop(0, SEG, p2, 0, unroll=True)
        else:
            def p2(j, st):
                pr, pi, dr, di = st
                rows = pl.ds(pl.multiple_of(step_of(j) * 8, 8), 8)
                xr = xs_ref[rows, cr]
                xi = xs_ref[rows, ci]
                dr = dr + pr * xr + pi * xi
                di = di + pi * xr - pr * xi
                wr, wi = carried(j)
                gr = buf_ref[rows, cr] + wr
                gi = buf_ref[rows, ci] + wi
                buf_ref[rows, cr] = gr
                buf_ref[rows, ci] = gi
                return gr, gi, dr, di

            st = lax.fori_loop(0, SEG, p2, (cmr, cmi, z, z), unroll=True)
            dar_ref[:, cr] += st[2]
            dai_ref[:, cr] += st[3]


def _s5_fwd(proj, perm, permt, bdbr, bdbi, bdcr, bdci, abr, abi, asr, asi, dvec, wglu, bglu):
    T = proj.shape[0]
    TC, NS, W = S5_TC, S5_NS, S5_WIDTH
    nc = T // TC

    def body(u_ref, pm_ref, pmt_ref, bdbr_ref, bdbi_ref, bdcr_ref, bdci_ref, ar_ref, ai_ref, asr_ref, asi_ref,
             d_ref, wglu_ref, bglu_ref, ya_ref, xs_ref, ypre_ref, carry_ref):
        i = pl.program_id(0)

        @pl.when(i == 0)
        def _():
            carry_ref[...] = jnp.zeros_like(carry_ref)

        up = _dot(pm_ref[...], u_ref[...]).astype(BF16)
        for j in range(8):
            ut = up[:, j * 128:(j + 1) * 128]
            xs_ref[:, j * 512:(j + 1) * 512] = _dot(ut, bdbr_ref[j])
            xs_ref[:, NS + j * 512:NS + (j + 1) * 512] = _dot(ut, bdbi_ref[j])
        _s5_scan(xs_ref, ar_ref[...], ai_ref[...], asr_ref, asi_ref, carry_ref, S5_LG)
        ys = []
        for j in range(8):
            xr = xs_ref[:, j * 512:(j + 1) * 512].astype(BF16)
            xi = xs_ref[:, NS + j * 512:NS + (j + 1) * 512].astype(BF16)
            ys.append(_dot(xr, bdcr_ref[j]) + _dot(xi, bdci_ref[j]))
        ypre = jnp.concatenate(ys, axis=1) + d_ref[...] * up.astype(F32)
        ypre_ref[...] = ypre
        ya = _gelu(ypre)
        zl = _dot(ya.astype(BF16), wglu_ref[...]) + bglu_ref[...]
        outp = (ya * _sigmoid(zl)).astype(BF16)
        ya_ref[...] = _dot(pmt_ref[...], outp).astype(BF16)

    return pl.pallas_call(
        body, name="s5_fwd", grid=(nc,),
        in_specs=[Block((TC, W), lambda i: (i, 0)), _const((TC, TC)), _const((TC, TC)),
                  _const((8, 128, 512)), _const((8, 128, 512)), _const((8, 512, 128)), _const((8, 512, 128)),
                  _const((1, NS)), _const((1, NS)), _const((S5_SEG, NS)), _const((S5_SEG, NS)),
                  _const((1, W)), _const((W, W)), _const((1, W))],
        out_specs=[Block((TC, W), lambda i: (i, 0)), Block((TC, 2 * NS), lambda i: (i, 0)),
                   Block((TC, W), lambda i: (i, 0))],
        out_shape=[SDS((T, W), BF16), SDS((T, 2 * NS), F32), SDS((T, W), F32)],
        scratch_shapes=[pltpu.VMEM((1, 2 * NS), F32)],
        compiler_params=_cparams(("arbitrary",), VMEM_BIG),
    )(proj, perm, permt, bdbr, bdbi, bdcr, bdci, abr, abi, asr, asi, dvec, wglu, bglu)


def _s5_bwd(dya, proj, ypre, xs, perm, permt, bdbr, bdbi, bdcr, bdci, abr, abi, asr, asi, dvec, wglu, bglu):
    T = proj.shape[0]
    TC, NS, W = S5_TC, S5_NS, S5_WIDTH
    nc = T // TC

    def body(dya_ref, u_ref, ypre_ref, xs_ref, pm_ref, pmt_ref, bdbr_ref, bdbi_ref, bdcr_ref, bdci_ref,
             ar_ref, ai_ref, asr_ref, asi_ref, d_ref, wglu_ref, bglu_ref,
             du_ref, dar_ref, dai_ref, dd_ref, dbglu_ref, o_dbdbr, o_dbdbi, o_dbdcr, o_dbdci, o_dwglu,
             g_ref, carry_ref, dbdbr_ref, dbdbi_ref, dbdcr_ref, dbdci_ref, dwglu_ref):
        i = pl.program_id(0)

        @pl.when(i == 0)
        def _():
            carry_ref[...] = jnp.zeros_like(carry_ref)
            for r in (dbdbr_ref, dbdbi_ref, dbdcr_ref, dbdci_ref, dar_ref, dai_ref, dd_ref, dwglu_ref, dbglu_ref):
                r[...] = jnp.zeros_like(r)

        pm = pm_ref[...]
        dyo = _dot(pm, dya_ref[...])
        up = _dot(pm, u_ref[...]).astype(BF16)
        upf = up.astype(F32)
        ypre_v = ypre_ref[...]
        ya = _gelu(ypre_v)
        yab = ya.astype(BF16)
        sg = _sigmoid(_dot(yab, wglu_ref[...]) + bglu_ref[...])
        dz = dyo * ya * sg * (1.0 - sg)
        dzb = dz.astype(BF16)
        dya_t = dyo * sg + _dot(dzb, wglu_ref[...], NT)
        dwglu_ref[...] += _dot(yab, dzb, TN)
        dbglu_ref[...] += jnp.sum(dz, axis=0, keepdims=True)
        dy = dya_t * _gelu_grad(ypre_v)
        dd_ref[...] += jnp.sum(dy * upf, axis=0, keepdims=True)
        dyb = dy.astype(BF16)
        for j in range(8):
            dyj = dyb[:, j * 128:(j + 1) * 128]
            g_ref[:, j * 512:(j + 1) * 512] = _dot(dyj, bdcr_ref[j], NT)
            g_ref[:, NS + j * 512:NS + (j + 1) * 512] = _dot(dyj, bdci_ref[j], NT)
            dbdcr_ref[j] += _dot(xs_ref[:, j * 512:(j + 1) * 512].astype(BF16), dyj, TN)
            dbdci_ref[j] += _dot(xs_ref[:, NS + j * 512:NS + (j + 1) * 512].astype(BF16), dyj, TN)
        _s5_scan(g_ref, ar_ref[...], ai_ref[...], asr_ref, asi_ref, carry_ref, S5_LG,
                 xs_ref=xs_ref, dar_ref=dar_ref, dai_ref=dai_ref)
        dus = []
        for j in range(8):
            ut = up[:, j * 128:(j + 1) * 128]
            gr = g_ref[:, j * 512:(j + 1) * 512].astype(BF16)
            gi = g_ref[:, NS + j * 512:NS + (j + 1) * 512].astype(BF16)
            dbdbr_ref[j] += _dot(ut, gr, TN)
            dbdbi_ref[j] += _dot(ut, gi, TN)
            dus.append(_dot(gr, bdbr_ref[j], NT) + _dot(gi, bdbi_ref[j], NT))
        dup = jnp.concatenate(dus, axis=1) + d_ref[...] * dy
        du_ref[...] = _dot(pmt_ref[...], dup.astype(BF16)).astype(BF16)

        @pl.when(i == nc - 1)
        def _():
            for src, dst in ((dbdbr_ref, o_dbdbr), (dbdbi_ref, o_dbdbi), (dbdcr_ref, o_dbdcr),
                             (dbdci_ref, o_dbdci), (dwglu_ref, o_dwglu)):
                pltpu.sync_copy(src, dst)

    c2 = lambda i: (0, 0)
    rev = lambda i: (nc - 1 - i, 0)
    return pl.pallas_call(
        body, name="s5_bwd", grid=(nc,),
        in_specs=[Block((TC, W), rev), Block((TC, W), rev), Block((TC, W), rev), Block((TC, 2 * NS), rev),
                  _const((TC, TC)), _const((TC, TC)),
                  _const((8, 128, 512)), _const((8, 128, 512)), _const((8, 512, 128)), _const((8, 512, 128)),
                  _const((1, NS)), _const((1, NS)), _const((S5_SEG, NS)), _const((S5_SEG, NS)),
                  _const((1, W)), _const((W, W)), _const((1, W))],
        out_specs=[Block((TC, W), rev), Block((8, NS), c2), Block((8, NS), c2), Block((1, W), c2), Block((1, W), c2),
                   ANY, ANY, ANY, ANY, ANY],
        out_shape=[SDS((T, W), BF16), SDS((8, NS), F32), SDS((8, NS), F32), SDS((1, W), F32), SDS((1, W), F32),
                   SDS((8, 128, 512), F32), SDS((8, 128, 512), F32),
                   SDS((8, 512, 128), F32), SDS((8, 512, 128), F32), SDS((W, W), F32)],
        scratch_shapes=[pltpu.VMEM((TC, 2 * NS), F32), pltpu.VMEM((1, 2 * NS), F32),
                        pltpu.VMEM((8, 128, 512), F32), pltpu.VMEM((8, 128, 512), F32),
                        pltpu.VMEM((8, 512, 128), F32), pltpu.VMEM((8, 512, 128), F32), pltpu.VMEM((W, W), F32)],
        compiler_params=_cparams(("arbitrary",), VMEM_BIG),
    )(dya, proj, ypre, xs, perm, permt, bdbr, bdbi, bdcr, bdci, abr, abi, asr, asi, dvec, wglu, bglu)


def _bd_b(bk_t):
    C, P = S5_GROUP_WIDTH, S5_STATE
    t = jnp.transpose(bk_t, (1, 0, 2)).reshape(8, 8, C, P)
    eye = jnp.eye(8, dtype=t.dtype)
    return (t[:, :, :, None, :] * eye[None, :, None, :, None]).reshape(8, 8 * C, 8 * P)


def _bd_b_extract(m):
    C, P = S5_GROUP_WIDTH, S5_STATE
    t = m.reshape(8, 8, C, 8, P)
    d = jnp.stack([t[:, g, :, g, :] for g in range(8)], axis=1)
    return jnp.transpose(d.reshape(S5_GROUPS, C, P), (1, 0, 2))


def _bd_c(c):
    C, P = S5_GROUP_WIDTH, S5_STATE
    t = jnp.transpose(c, (0, 2, 1)).reshape(8, 8, P, C)
    eye = jnp.eye(8, dtype=t.dtype)
    return (t[:, :, :, None, :] * eye[None, :, None, :, None]).reshape(8, 8 * P, 8 * C)


def _bd_c_extract(m):
    C, P = S5_GROUP_WIDTH, S5_STATE
    t = m.reshape(8, 8, P, 8, C)
    d = jnp.stack([t[:, g, :, g, :] for g in range(8)], axis=1)
    return jnp.transpose(d.reshape(S5_GROUPS, P, C), (0, 2, 1))


def _perm_matrix():
    r = jnp.arange(S5_TC)
    src = (r % 8) * S5_SEG + r // 8
    return (src[:, None] == jnp.arange(S5_TC)[None, :]).astype(BF16)


def _coords():
    return lax.axis_index("x"), lax.axis_index("y"), lax.axis_index("c")


def _all_gather(name, arrs):
    n = len(arrs)

    def body(*refs):
        ins, outs = refs[:n], refs[n:2 * n]
        send_sems, recv_sems, local_sems = refs[2 * n:]
        x, y, c = _coords()
        me, sibling = (x, y, c), (x, y, 1 - c)
        chips = [(1 - x, y), (x, 1 - y), (1 - x, 1 - y)]

        def slot(p):
            return 4 * p[0] + 2 * p[1] + p[2]

        def copy(a, k, block, to, src=None):
            dst = outs[a].at[slot(block)]
            return pltpu.make_async_remote_copy(
                src_ref=dst if src is None else src, dst_ref=dst,
                send_sem=send_sems.at[a * 7 + k], recv_sem=recv_sems.at[a * 7 + k],
                device_id=to, device_id_type=MESH)

        mine = [pltpu.make_async_copy(ins[a], outs[a].at[slot(me)], local_sems.at[a]) for a in range(n)]
        for m in mine:
            m.start()
        first = []
        for a in range(n):
            first.append(copy(a, 0, me, sibling, src=ins[a]))
            first += [copy(a, 1 + j, me, (*chip, c), src=ins[a]) for j, chip in enumerate(chips)]
        for cp in first:
            cp.start()
        passed = []
        for j, chip in enumerate(chips):
            for a in range(n):
                copy(a, 1 + j, (*chip, c), me).wait_recv()
                fw = copy(a, 4 + j, (*chip, c), sibling)
                fw.start()
                passed.append(fw)
        for a in range(n):
            copy(a, 0, sibling, me).wait_recv()
            for j, chip in enumerate(chips):
                copy(a, 4 + j, (*chip, 1 - c), me).wait_recv()
        for cp in first + passed:
            cp.wait_send()
        for m in mine:
            m.wait()

    return pl.pallas_call(
        body, name=name,
        in_specs=[ANY] * n, out_specs=[ANY] * n,
        out_shape=[SDS((NDEV,) + a.shape, a.dtype) for a in arrs],
        scratch_shapes=[pltpu.SemaphoreType.DMA((7 * n,)), pltpu.SemaphoreType.DMA((7 * n,)),
                        pltpu.SemaphoreType.DMA((n,))],
    )(*arrs)


HBM = pl.BlockSpec(memory_space=pltpu.HBM)
SEM = pl.BlockSpec(memory_space=pltpu.SEMAPHORE)
EFFECT = pltpu.SideEffectType.DATAFLOW_SIDE_EFFECTING


def _peers7(x, y, c):
    return [(1 - x if fx else x, 1 - y if fy else y, 1 - c if fc else c)
            for fx in (0, 1) for fy in (0, 1) for fc in (0, 1) if fx or fy or fc]


def _slot(p):
    return 4 * p[0] + 2 * p[1] + p[2]


def _split_copies(src_refs, land_refs, send_sems, recv_sems, gather, mine):
    x, y, c = _coords()
    me = (x, y, c)
    out = []
    for a, (src, land) in enumerate(zip(src_refs, land_refs)):
        for k, p in enumerate(_peers7(x, y, c)):
            s = src if gather else src.at[_slot(p)]
            out.append(pltpu.make_async_remote_copy(
                src_ref=s, dst_ref=land.at[_slot(me) if mine else _slot(p)],
                send_sem=send_sems.at[a * 7 + k], recv_sem=recv_sems.at[a * 7 + k],
                device_id=p, device_id_type=MESH))
    return out


def _own_slab(shard):
    x, y, c = _coords()
    z = lax.empty((NDEV,) + shard.shape, shard.dtype)
    return lax.dynamic_update_slice(z, shard[None], (_slot((x, y, c)),) + (0,) * shard.ndim)


def _split_start(name, srcs, lands, gather):
    n = len(srcs)

    def body(*refs):
        src_refs, land_refs = refs[:n], refs[n:2 * n]
        send_sems, recv_sems = refs[2 * n], refs[2 * n + 1]
        token = refs[-1]
        for cp in _split_copies(src_refs, land_refs, send_sems, recv_sems, gather, True):
            cp.start()
        token[...] = jnp.zeros_like(token)

    thru = [pltpu.HBM(a.shape, a.dtype) for a in list(srcs) + list(lands)]
    res = pl.pallas_call(
        body, name=name,
        out_shape=(pltpu.SemaphoreType.DMA((7 * n,)), pltpu.SemaphoreType.DMA((7 * n,)), *thru, SDS((8, 128), F32)),
        in_specs=[HBM] * (2 * n),
        out_specs=(SEM, SEM, *([HBM] * (2 * n)), pl.BlockSpec(memory_space=pltpu.VMEM)),
        input_output_aliases={i: 2 + i for i in range(2 * n)},
        compiler_params=pltpu.CompilerParams(has_side_effects=EFFECT),
    )(*[pltpu.with_memory_space_constraint(a, pltpu.HBM) for a in list(srcs) + list(lands)])
    return res[0], res[1], list(res[2:2 + n]), list(res[2 + n:2 + 2 * n]), res[-1]


def _split_wait(name, started, after, gather):
    send_sems, recv_sems, srcs, lands, _ = started
    n = len(srcs)

    def body(*refs):
        src_refs, land_refs = refs[:n], refs[n:2 * n]
        s_sems, r_sems = refs[2 * n], refs[2 * n + 1]
        for cp in _split_copies(src_refs, land_refs, s_sems, r_sems, gather, False):
            cp.wait_send()
            cp.wait_recv()

    thru = [pltpu.HBM(a.shape, a.dtype) for a in list(srcs) + list(lands)]
    res = pl.pallas_call(
        body, name=name, out_shape=tuple(thru),
        in_specs=[HBM] * (2 * n) + [SEM, SEM, ANY], out_specs=tuple([HBM] * (2 * n)),
        input_output_aliases={i: i for i in range(2 * n)},
        compiler_params=pltpu.CompilerParams(has_side_effects=EFFECT),
    )(*srcs, *lands, send_sems, recv_sems, after)
    return list(res[n:])


def _adam_math(w, g, m, v):
    m = ADAM_B1 * m + (1.0 - ADAM_B1) * g
    v = ADAM_B2 * v + (1.0 - ADAM_B2) * (g * g)
    m_hat = m / (1.0 - ADAM_B1 ** ADAM_STEP)
    v_hat = v / (1.0 - ADAM_B2 ** ADAM_STEP)
    delta = -ADAM_LR * (m_hat / (jnp.sqrt(v_hat) + ADAM_EPS) + ADAM_WD * w)
    return delta, m, v


def _adam_sharded(name, recv, sub, w, m, v):
    R, Cc = w.shape
    tr = max(t for t in range(16, R + 1, 16) if R % t == 0 and t * Cc <= 256 * 1024)

    def body(*refs):
        parts = refs[:NDEV]
        w_ref, m_ref, v_ref, g_out, d_out, m_out, v_out = refs[NDEV:]
        g = parts[0][...].astype(F32)
        for p in parts[1:]:
            g = g + p[...].astype(F32)
        delta, mn, vn = _adam_math(w_ref[...], g, m_ref[...], v_ref[...])
        g_out[...] = g
        d_out[...] = delta
        m_out[...] = mn
        v_out[...] = vn

    if sub is None:
        pspecs = [Block((None, tr, Cc), functools.partial(lambda s, i: (s, i, 0), s)) for s in range(NDEV)]
    else:
        pspecs = [Block((None, None, tr, Cc), functools.partial(lambda s, i: (s, sub, i, 0), s)) for s in range(NDEV)]
    row = Block((tr, Cc), lambda i: (i, 0))
    o = SDS((R, Cc), F32)
    return pl.pallas_call(
        body, name=name, grid=(R // tr,),
        in_specs=pspecs + [row, row, row], out_specs=[row, row, row, row], out_shape=[o, o, o, o],
        compiler_params=_cparams(("arbitrary",), VMEM_BIG))(*([recv] * NDEV), w, m, v)


def _adam_small(groups):
    n = len(groups)

    def body(*refs):
        ins, outs = refs[:4 * n], refs[4 * n:]
        for a in range(n):
            p_ref, w_ref, m_ref, v_ref = ins[4 * a:4 * a + 4]
            g = p_ref[0]
            for s in range(1, NDEV):
                g = g + p_ref[s]
            delta, mn, vn = _adam_math(w_ref[...], g, m_ref[...], v_ref[...])
            for o, r in zip(outs[4 * a:4 * a + 4], (g, delta, mn, vn)):
                o[...] = r

    flat_in = [t for grp in groups for t in grp]
    out_shape = [SDS(grp[1].shape, F32) for grp in groups for _ in range(4)]
    res = pl.pallas_call(body, name="adam_small", out_shape=out_shape,
                         compiler_params=_cparams(None, VMEM_BIG))(*flat_in)
    return [tuple(res[4 * a:4 * a + 4]) for a in range(n)]


_TINY = ["mix_norm", "s5_a_re", "s5_a_im", "s5_log_dt", "s5_d", "s5_b_glu", "sgu_ln_g", "sgu_ln_b",
         "sgu_b_s", "b_gate", "ffn2_norm", "final_norm"]
_ORDER = ["ffn1_norm", "ffn1_w_gate", "ffn1_w_up", "ffn1_w_down", "mix_norm", "w_in", "s5_a_re", "s5_a_im",
          "s5_log_dt", "s5_b_re", "s5_b_im", "s5_c_re", "s5_c_im", "s5_d", "s5_w_glu", "s5_b_glu", "sgu_ln_g",
          "sgu_ln_b", "sgu_w_s", "sgu_b_s", "w_branch_a", "w_branch_b", "w_gate", "b_gate", "w_out", "ffn2_norm",
          "ffn2_w_gate", "ffn2_w_up", "ffn2_w_down", "final_norm"]


def _step(x, tgt, W, M, V):
    T = x.shape[1]
    x0 = x[0]
    tgt0 = tgt[0]
    bf = lambda a: a.astype(BF16)

    def gather_start(name, shards):
        return _split_start(name, shards, [_own_slab(s) for s in shards], True)

    (wgu1,) = _all_gather("gather1", [jnp.stack([bf(W["ffn1_w_gate"][0]), bf(W["ffn1_w_up"][0])])])

    lr_, li_ = W["s5_a_re"][0], W["s5_a_im"][0]
    ldt_ = W["s5_log_dt"][0][:, None]
    brt = jnp.transpose(W["s5_b_re"][0], (2, 0, 1))
    bit = jnp.transpose(W["s5_b_im"][0], (2, 0, 1))
    abr, abi, pwr, pwi, bkr_t, bki_t = _s5_prep(lr_, li_, ldt_, brt, bit)
    bdbr, bdbi = bf(_bd_b(bkr_t)), bf(_bd_b(bki_t))
    bdcr, bdci = bf(_bd_c(W["s5_c_re"][0])), bf(_bd_c(-W["s5_c_im"][0]))
    flat = lambda a: a.reshape(1, S5_NS)
    s5a = (_perm_matrix(), _perm_matrix().T, bdbr, bdbi, bdcr, bdci, flat(abr), flat(abi),
           pwr.reshape(S5_SEG, S5_NS), pwi.reshape(S5_SEG, S5_NS),
           W["s5_d"][0].reshape(1, S5_WIDTH))
    blk = jnp.arange(MLP_CHUNK) // CHUNK
    mask = blk[:, None] >= blk[None, :]
    wsm = jnp.where(mask[None], W["sgu_w_s"][0], 0.0)
    wsm_b, wsmt_b = bf(wsm), bf(jnp.transpose(wsm, (0, 2, 1)))
    bst = jnp.broadcast_to(W["sgu_b_s"][0][:, :, None], (SGU_HEADS, MLP_CHUNK, 128))
    bgate2 = W["b_gate"].reshape(2, 1, D_MODEL)

    h1 = _rms_fwd("rms1", x0, W["ffn1_norm"])
    dep = (wgu1[0, 0, :1, :1] * 0).astype(BF16)

    def later(a):
        return bf(a) + dep[0]

    gs2 = gather_start("gather2_start", [later(W["ffn1_w_down"][0])])
    ab1, f1 = _ffn_up("ffn1_up", h1, wgu1, gs2[4])
    (wd1,) = _split_wait("gather2_wait", gs2, f1, True)
    dep = (wd1[0, :1, :1] * 0).astype(BF16)
    gs3 = gather_start("gather3_start", [later(W["w_in"][0]), later(W["s5_w_glu"][0])])
    x1 = _ffn_down("ffn1_down", f1, wd1, x0, after=gs3[4])
    h2 = _rms_fwd("rms2", x1, W["mix_norm"])
    win, wglu = _split_wait("gather3_wait", gs3, h2, True)
    wglu = wglu.reshape(S5_WIDTH, S5_WIDTH)
    s5c = s5a + (wglu, W["s5_b_glu"])
    dep = (win[0, :1, :1] * 0).astype(BF16)
    gs4 = gather_start("gather4_start", [later(W["w_gate"][0]), later(W["w_branch_a"][0]),
                                         later(W["w_branch_b"][0]), later(W["w_out"][0])])
    proj = _col_fwd("w_in", h2, win, after=gs4[4])
    ya, xs, ypre = _s5_fwd(proj, *s5c)
    dep = (ya[:1, :1] * 0).astype(BF16)
    gs5 = gather_start("gather5_start", [jnp.stack([later(W["ffn2_w_gate"][0]), later(W["ffn2_w_up"][0])])])
    yb = _sgu_fwd("sgu_fwd", proj, W["sgu_ln_g"] + gs5[4][:1, :1], W["sgu_ln_b"], wsm_b, bst)
    wgate, wba, wbb, wout = _split_wait("gather4_wait", gs4, yb, True)
    wout = wout.reshape(D_MODEL, D_MODEL)
    pa = _col_fwd("branch_a", ya, wba)
    pb = _col_fwd("branch_b", yb, wbb)
    gl = _gate_fwd("gate", h2, wgate, bgate2)
    merged = _merge_fwd("merge", pa, pb, gl)
    x2 = _plain_fwd_res("w_out", merged, wout, x1)
    h3 = _rms_fwd("rms3", x2, W["ffn2_norm"])
    (wgu2,) = _split_wait("gather5_wait", gs5, h3, True)
    dep = (wgu2[0, 0, :1, :1] * 0).astype(BF16)
    gs6 = gather_start("gather6_start", [later(W["ffn2_w_down"][0])])
    ab2, f2 = _ffn_up("ffn2_up", h3, wgu2, gs6[4])
    (wd2,) = _split_wait("gather6_wait", gs6, f2, True)
    x3 = _ffn_down("ffn2_down", f2, wd2, x2)
    loss_p, dx3b, dgf = _loss_head("loss_head", x3, W["final_norm"].reshape(1, D_MODEL), tgt0)

    def exchange_start(name, grads):
        x_, y_, c_ = _coords()
        me = _slot((x_, y_, c_))
        return _split_start(name, grads, [_own_slab(lax.dynamic_index_in_dim(g, me, 0, keepdims=False))
                                          for g in grads], False)

    dab2 = _ffn_down_bwd_act("ffn2_down_bwd_a", dx3b, wd2, ab2)
    g_wd2 = _ffn_down_bwd_w("ffn2_down_bwd_w", f2, dx3b)
    g_gu2 = _ffn_up_bwd_w("ffn2_up_bwd_w", h3, dab2)
    es1 = exchange_start("exchange1_start", [g_wd2, g_gu2])
    dh3 = _ffn_up_bwd_h("ffn2_up_bwd_h", dab2, wgu2, es1[4])
    dx2b, dg3 = _rms_bwd("rms3_bwd", dh3, x2, W["ffn2_norm"], dx3b, BF16)

    dmerged = _plain_bwd_a("w_out_bwd_a", dx2b, wout)
    g_wout = _plain_bwd_w("w_out_bwd_w", merged, dx2b)
    dpa, dpb, dgl, dbgate = _merge_bwd("merge_bwd", dmerged, pa, pb, gl)
    dya = _col_bwd_a("branch_a_bwd_a", dpa, wba)
    g_wba = _col_bwd_w("branch_a_bwd_w", ya, dpa, 256)
    dyb = _col_bwd_a("branch_b_bwd_a", dpb, wbb)
    g_wbb = _col_bwd_w("branch_b_bwd_w", yb, dpb, 256)
    dh2g = _gate_bwd_a("gate_bwd_a", dgl, wgate)
    g_wgate = _gate_bwd_w("gate_bwd_w", h2, dgl, 512)
    (dua, dar8, dai8, ddv, dbglu, dbdbr, dbdbi, dbdcr, dbdci, g_wglu) = _s5_bwd(dya, proj, ypre, xs, *s5c)
    dproj, dws, dbst, dlng, dlnb = _sgu_bwd("sgu_bwd", dyb, dua, proj, W["sgu_ln_g"], W["sgu_ln_b"],
                                            wsm_b, wsmt_b, bst)
    g_win = _col_bwd_w("w_in_bwd_w", h2, dproj, 384)
    g_wout3 = g_wout.reshape(NDEV, D_MODEL // NDEV, D_MODEL)
    g_wglu3 = g_wglu.astype(BF16).reshape(NDEV, S5_WIDTH // NDEV, S5_WIDTH)
    es2 = exchange_start("exchange2_start", [g_wout3, g_wba, g_wbb, g_wgate, g_wglu3, g_win])
    dh2 = _col_bwd_a("w_in_bwd_a", dproj, win, add=dh2g)
    dx1b, dgm = _rms_bwd("rms2_bwd", dh2, x1, W["mix_norm"] + es2[4][:1, :1], dx2b, BF16)

    dabr = jnp.sum(dar8, axis=0).reshape(S5_GROUPS, S5_STATE)
    dabi = jnp.sum(dai8, axis=0).reshape(S5_GROUPS, S5_STATE)
    d_lr, d_li, d_ldt, d_brt, d_bit = _s5_prep_bwd(lr_, li_, ldt_, brt, bit, dabr, dabi,
                                                   _bd_b_extract(dbdbr), _bd_b_extract(dbdbi))
    small_g = {
        "mix_norm": dgm, "ffn2_norm": dg3, "final_norm": dgf,
        "s5_a_re": d_lr, "s5_a_im": d_li, "s5_log_dt": d_ldt,
        "s5_d": ddv, "s5_b_glu": dbglu, "sgu_ln_g": dlng, "sgu_ln_b": dlnb,
        "sgu_b_s": dbst[:, :, 0], "b_gate": dbgate,
    }
    to_cgp = lambda a: jnp.transpose(a[0], (2, 0, 1))
    from_cgp = lambda a: jnp.transpose(a, (1, 2, 0))[None]
    natural = [
        ("s5_b_re", d_brt, to_cgp, from_cgp), ("s5_b_im", d_bit, to_cgp, from_cgp),
        ("s5_c_re", _bd_c_extract(dbdcr), lambda a: a[0], lambda a: a[None]),
        ("s5_c_im", -_bd_c_extract(dbdci), lambda a: a[0], lambda a: a[None]),
        ("sgu_w_s", jnp.where(mask[None], dws, 0.0), lambda a: a[0], lambda a: a[None]),
    ]
    sizes = [W[n].size for n in _TINY]
    total = sum(sizes) + 1
    rows = -(-total // 128)
    rows = -(-rows // 8) * 8
    pad = rows * 128 - total

    def pack(d, extra):
        return jnp.concatenate([d[n].reshape(-1).astype(F32) for n in _TINY] + [extra, jnp.zeros((pad,), F32)]
                               ).reshape(rows, 128)

    gsm = gather_start("gather_small_start", [pack(small_g, loss_p[0, :1])] + [g for _, g, _, _ in natural])

    dab1 = _ffn_down_bwd_act("ffn1_down_bwd_a", dx1b, wd1, ab1, after=gsm[4])
    g_gu1 = _ffn_up_bwd_w("ffn1_up_bwd_w", h1, dab1)
    es3 = exchange_start("exchange3_start", [g_gu1])
    g_wd1 = _ffn_down_bwd_w("ffn1_down_bwd_w", f1, dx1b, after=es3[4])
    es4 = exchange_start("exchange4_start", [g_wd1])
    dh1 = _ffn_up_bwd_h("ffn1_up_bwd_h", dab1, wgu1, es4[4])
    dx0, dg1 = _rms_bwd("rms1_bwd", dh1, x0, W["ffn1_norm"], dx1b, F32)

    G, Dl, Mn, Vn = {}, {}, {}, {}

    def adam(plan):
        last = None
        for n, recv, sub in plan:
            if sub is None:
                g, d, mn, vn = _adam_sharded("adam_" + n, recv, sub, W[n][0], M[n][0], V[n][0])
                G[n], Dl[n], Mn[n], Vn[n] = g[None], d[None], mn[None], vn[None]
            else:
                tr = jnp.transpose
                g, d, mn, vn = _adam_sharded("adam_" + n, recv, sub, tr(W[n][0]), tr(M[n][0]), tr(V[n][0]))
                G[n], Dl[n], Mn[n], Vn[n] = tr(g)[None], tr(d)[None], tr(mn)[None], tr(vn)[None]
            last = g
        return last

    r_wd2, r_gu2 = _split_wait("exchange1_wait", es1, dx0, False)
    done = adam([("ffn2_w_down", r_wd2, None), ("ffn2_w_gate", r_gu2, 0), ("ffn2_w_up", r_gu2, 1)])
    r_wout, r_wba, r_wbb, r_wgate, r_wglu, r_win = _split_wait("exchange2_wait", es2, done, False)
    done = adam([("w_out", r_wout, None), ("w_branch_a", r_wba, None), ("w_branch_b", r_wbb, None),
                 ("w_gate", r_wgate, None), ("s5_w_glu", r_wglu, None), ("w_in", r_win, None)])

    late = dg1 + 0.0 * done.reshape(-1)[:1]
    zero1 = jnp.zeros((1,), F32)
    parts = _split_wait("gather_small_wait", gsm, late, True)
    (parts_g1,) = _all_gather("gather_ffn1_norm_grad", [late])
    groups = [(parts[0], pack(W, zero1), pack(M, zero1), pack(V, zero1))]
    groups += [(parts[1 + a], view(W[n]), view(M[n]), view(V[n])) for a, (n, _, view, _) in enumerate(natural)]
    groups += [(parts_g1, W["ffn1_norm"], M["ffn1_norm"], V["ffn1_norm"])]
    res = _adam_small(groups)
    sg, sd, sm, sv = res[0]
    for (n, _, _, back), (g, d, mn, vn) in zip(natural, res[1:-1]):
        G[n], Dl[n], Mn[n], Vn[n] = back(g), back(d), back(mn), back(vn)
    G["ffn1_norm"], Dl["ffn1_norm"], Mn["ffn1_norm"], Vn["ffn1_norm"] = res[-1]

    def unpack(flat2d, into):
        flat = flat2d.reshape(-1)
        off = 0
        for n, s in zip(_TINY, sizes):
            into[n] = flat[off:off + s].reshape(W[n].shape)
            off += s
        return flat[off]

    loss = unpack(sg, G)
    unpack(sd, Dl)
    unpack(sm, Mn)
    unpack(sv, Vn)

    (r_gu1,) = _split_wait("exchange3_wait", es3, sg, False)
    done = adam([("ffn1_w_gate", r_gu1, 0), ("ffn1_w_up", r_gu1, 1)])
    (r_wd1,) = _split_wait("exchange4_wait", es4, done, False)
    adam([("ffn1_w_down", r_wd1, None)])

    return loss, dx0[None], G, Dl, Mn, Vn


def kernel(x, ffn1_norm, ffn1_w_gate, ffn1_w_up, ffn1_w_down, mix_norm, w_in, s5_a_re, s5_a_im, s5_log_dt, s5_b_re, s5_b_im, s5_c_re, s5_c_im, s5_d, s5_w_glu, s5_b_glu, sgu_ln_g, sgu_ln_b, sgu_w_s, sgu_b_s, w_branch_a, w_branch_b, w_gate, b_gate, w_out, ffn2_norm, ffn2_w_gate, ffn2_w_up, ffn2_w_down, final_norm, loss_target, m_ffn1_norm, m_ffn1_w_gate, m_ffn1_w_up, m_ffn1_w_down, m_mix_norm, m_w_in, m_s5_a_re, m_s5_a_im, m_s5_log_dt, m_s5_b_re, m_s5_b_im, m_s5_c_re, m_s5_c_im, m_s5_d, m_s5_w_glu, m_s5_b_glu, m_sgu_ln_g, m_sgu_ln_b, m_sgu_w_s, m_sgu_b_s, m_w_branch_a, m_w_branch_b, m_w_gate, m_b_gate, m_w_out, m_ffn2_norm, m_ffn2_w_gate, m_ffn2_w_up, m_ffn2_w_down, m_final_norm, v_ffn1_norm, v_ffn1_w_gate, v_ffn1_w_up, v_ffn1_w_down, v_mix_norm, v_w_in, v_s5_a_re, v_s5_a_im, v_s5_log_dt, v_s5_b_re, v_s5_b_im, v_s5_c_re, v_s5_c_im, v_s5_d, v_s5_w_glu, v_s5_b_glu, v_sgu_ln_g, v_sgu_ln_b, v_sgu_w_s, v_sgu_b_s, v_w_branch_a, v_w_branch_b, v_w_gate, v_b_gate, v_w_out, v_ffn2_norm, v_ffn2_w_gate, v_ffn2_w_up, v_ffn2_w_down, v_final_norm):
    a = locals()
    W = {n: a[n] for n in _ORDER}
    M = {n: a["m_" + n] for n in _ORDER}
    V = {n: a["v_" + n] for n in _ORDER}
    loss, gx, G, Dl, Mn, Vn = _step(x, loss_target, W, M, V)
    return (loss, gx, *[G[n] for n in _ORDER], *[Dl[n] for n in _ORDER], *[Mn[n] for n in _ORDER],
            *[Vn[n] for n in _ORDER])
```

```python
import functools
import math

import jax
import jax.numpy as jnp
from jax import lax
from jax.experimental import pallas as pl
from jax.experimental.pallas import tpu as pltpu

F32 = jnp.float32
BF16 = jnp.bfloat16
NDEV = 8
NORM_EPS = 1e-6
D_MODEL = 2048
D_FF = 5632
FF_SHARD = D_FF // NDEV
S5_WIDTH = 1024
S5_GROUPS = 64
S5_GROUP_WIDTH = 16
S5_STATE = 64
S5_NS = S5_GROUPS * S5_STATE
SGU_WIDTH = 1024
SGU_HEADS = 8
MLP_CHUNK = 128
CHUNK = 64
ADAM_LR, ADAM_B1, ADAM_B2, ADAM_EPS, ADAM_WD, ADAM_STEP = 0.001, 0.9, 0.999, 1e-08, 0.01, 10
S5_TC = 256
S5_SEG = S5_TC // 8
S5_LG = 512
S5_UNROLL = True
VMEM_BIG = 56 * 1024 * 1024

MESH = pl.DeviceIdType.MESH
SDS = jax.ShapeDtypeStruct
Block = pl.BlockSpec
ANY = pl.BlockSpec(memory_space=pl.ANY)


def _cparams(sem=None, vmem=None):
    return pltpu.CompilerParams(dimension_semantics=sem, vmem_limit_bytes=vmem)


def _const(shape):
    nd = len(shape)
    return pl.BlockSpec(shape, lambda i: (0,) * nd, pipeline_mode=pl.Buffered(1))


def _sigmoid(x):
    return 0.5 * jnp.tanh(0.5 * x) + 0.5


_GELU_C = math.sqrt(2.0 / math.pi)


def _gelu(x):
    return 0.5 * x * (1.0 + jnp.tanh(_GELU_C * (x + 0.044715 * x * x * x)))


def _gelu_grad(x):
    t = jnp.tanh(_GELU_C * (x + 0.044715 * x * x * x))
    return 0.5 * (1.0 + t) + 0.5 * x * (1.0 - t * t) * _GELU_C * (1.0 + 3.0 * 0.044715 * x * x)


NN = (((1,), (0,)), ((), ()))
NT = (((1,), (1,)), ((), ()))
TN = (((0,), (0,)), ((), ()))


def _dot(a, b, dims=NN):
    return lax.dot_general(a, b, dims, preferred_element_type=F32)


def _matmul(name, a, b, extras, *, grid, a_spec, b_spec, extra_specs, out_shapes, out_specs, acc_shape,
            epilogue, dims=NN, nb=None, compute=None, after=None, vmem=VMEM_BIG):
    nk = grid[2]
    if after is not None:
        extras = tuple(extras) + (after,)
        extra_specs = list(extra_specs) + [Block((8, 128), lambda i, j, k: (0, 0))]
    ne, no = len(extras), len(out_shapes)
    nacc = nb or 1
    if compute is None:
        def compute(a_ref, b_ref, q):
            return _dot(a_ref[...], b_ref[q] if nb else b_ref[...], dims)

    def body(*refs):
        a_ref, b_ref = refs[0], refs[1]
        ex = refs[2:2 + ne]
        outs = refs[2 + ne:2 + ne + no]
        if nk == 1:
            epilogue([compute(a_ref, b_ref, q) for q in range(nacc)], ex, outs)
            return
        acc_ref = refs[2 + ne + no]
        k = pl.program_id(2)

        @pl.when(k == 0)
        def _():
            acc_ref[...] = jnp.zeros_like(acc_ref)

        for q in range(nacc):
            acc_ref[q] += compute(a_ref, b_ref, q)

        @pl.when(k == nk - 1)
        def _():
            epilogue([acc_ref[q] for q in range(nacc)], ex, outs)

    scratch = [] if nk == 1 else [pltpu.VMEM((nacc,) + tuple(acc_shape), F32)]
    res = pl.pallas_call(
        body, name=name, grid=grid,
        in_specs=[a_spec, b_spec] + list(extra_specs),
        out_specs=list(out_specs), out_shape=list(out_shapes), scratch_shapes=scratch,
        compiler_params=_cparams(("parallel", "parallel", "arbitrary"), vmem),
    )(a, b, *extras)
    return res


def _store(dtype_outs=None):
    def ep(accs, ex, outs):
        outs[0][...] = accs[0].astype(outs[0].dtype)
    return ep


def _tile(n, t):
    t = min(n, t)
    assert n % t == 0, (n, t)
    return t


def _ksum(kq, dims):
    def compute(a_ref, b_ref, _):
        part = _dot(a_ref[0], b_ref[0], dims)
        for q in range(1, kq):
            part = part + _dot(a_ref[q], b_ref[q], dims)
        return part
    return compute


def _ksum_lanes(kq, ns, dims):
    def compute(a_ref, b_ref, _):
        part = _dot(a_ref[:, 0:ns], b_ref[0], dims)
        for q in range(1, kq):
            part = part + _dot(a_ref[:, q * ns:(q + 1) * ns], b_ref[q], dims)
        return part
    return compute


def _wide_b(g):
    def compute(a_ref, b_ref, _):
        bw = b_ref[0] if g == 1 else jnp.concatenate([b_ref[q] for q in range(g)], axis=1)
        return _dot(a_ref[...], bw, NN)
    return compute


TT_DEEP = 2048


HIDDEN = NDEV * FF_SHARD


def _ffn_up(name, h, wgu, after=None):
    T, D = h.shape
    tm = _tile(T, 1024)

    def ep(accs, ex, outs):
        a, b = accs
        outs[0][0] = a.astype(BF16)
        outs[0][1] = b.astype(BF16)
        outs[1][...] = (a * _sigmoid(a) * b).astype(BF16)

    return _matmul(
        name, wgu, h, (), after=after, grid=(NDEV, T // tm, 1),
        a_spec=Block((None, 2, FF_SHARD, D), lambda j, i, k: (j, 0, 0, 0)),
        b_spec=Block((tm, D), lambda j, i, k: (i, 0)),
        extra_specs=(),
        out_shapes=[SDS((NDEV, 2, FF_SHARD, T), BF16), SDS((NDEV, FF_SHARD, T), BF16)],
        out_specs=[Block((None, 2, FF_SHARD, tm), lambda j, i, k: (j, 0, 0, i)),
                   Block((None, FF_SHARD, tm), lambda j, i, k: (j, 0, i))],
        acc_shape=(FF_SHARD, tm), nb=2,
        compute=lambda a_ref, b_ref, q: _dot(a_ref[q], b_ref[...], NT), epilogue=ep)


def _ffn_down(name, f, wd, xres, after=None):
    T = f.shape[2]
    tm, tn, tk = _tile(T, 1024), 1024, HIDDEN // 2

    def ep(accs, ex, outs):
        outs[0][...] = ex[0][...] + 0.5 * accs[0]

    return _matmul(
        name, f.reshape(HIDDEN, T), wd.reshape(HIDDEN, D_MODEL), (xres,), after=after,
        grid=(T // tm, D_MODEL // tn, HIDDEN // tk),
        a_spec=Block((tk, tm), lambda i, j, k: (k, i)),
        b_spec=Block((tk, tn), lambda i, j, k: (k, j)),
        extra_specs=[Block((tm, tn), lambda i, j, k: (i, j))],
        out_shapes=[SDS((T, D_MODEL), F32)],
        out_specs=[Block((tm, tn), lambda i, j, k: (i, j))],
        acc_shape=(tm, tn), dims=TN, epilogue=ep)[0]


def _ffn_down_bwd_act(name, dyb, wd, ab, after=None):
    T, D = dyb.shape
    tm, g = _tile(T, 1024), 1

    def ep(accs, ex, outs):
        for s in range(g):
            df = accs[0][s * FF_SHARD:(s + 1) * FF_SHARD, :]
            a = ex[0][s, 0].astype(F32)
            b = ex[0][s, 1].astype(F32)
            hs = 0.5 * _sigmoid(a)
            outs[0][s, 0] = (df * b * hs * (1.0 + a * (1.0 - 2.0 * hs))).astype(BF16)
            outs[0][s, 1] = (df * a * hs).astype(BF16)

    blk = Block((g, 2, FF_SHARD, tm), lambda j, i, k: (j, 0, 0, i))
    return _matmul(
        name, wd.reshape(HIDDEN, D), dyb, (ab,), after=after, grid=(NDEV // g, T // tm, 1),
        a_spec=Block((g * FF_SHARD, D), lambda j, i, k: (j, 0)),
        b_spec=Block((tm, D), lambda j, i, k: (i, 0)),
        extra_specs=[blk],
        out_shapes=[SDS((NDEV, 2, FF_SHARD, T), BF16)],
        out_specs=[blk],
        acc_shape=(g * FF_SHARD, tm), dims=NT, epilogue=ep)[0]


def _ffn_down_bwd_w(name, f, dyb, after=None):
    T = f.shape[2]
    tt, tn, tr = _tile(T, TT_DEEP), 1024, 2 * FF_SHARD

    def ep(accs, ex, outs):
        outs[0][...] = (0.5 * accs[0]).astype(BF16)

    return _matmul(
        name, f.reshape(HIDDEN, T), dyb, (), after=after, grid=(HIDDEN // tr, D_MODEL // tn, T // tt),
        a_spec=Block((tr, tt), lambda j, n, k: (j, k)),
        b_spec=Block((tt, tn), lambda j, n, k: (k, n)),
        extra_specs=(),
        out_shapes=[SDS((HIDDEN, D_MODEL), BF16)],
        out_specs=[Block((tr, tn), lambda j, n, k: (j, n))],
        acc_shape=(tr, tn), dims=NN, epilogue=ep)[0].reshape(NDEV, FF_SHARD, D_MODEL)


def _ffn_up_bwd_h(name, dab, wgu, after):
    T = dab.shape[3]
    tm = _tile(T, 1024)
    return _matmul(
        name, dab, wgu, (), after=after, grid=(T // tm, 1, NDEV),
        a_spec=Block((None, 2, FF_SHARD, tm), lambda i, j, k: (k, 0, 0, i)),
        b_spec=Block((None, 2, FF_SHARD, D_MODEL), lambda i, j, k: (k, 0, 0, 0)),
        extra_specs=(),
        out_shapes=[SDS((T, D_MODEL), BF16)],
        out_specs=[Block((tm, D_MODEL), lambda i, j, k: (i, 0))],
        acc_shape=(tm, D_MODEL), compute=_ksum(2, TN), epilogue=_store())[0]


def _ffn_up_bwd_w(name, h, dab):
    T, D = h.shape
    tt, tn = _tile(T, TT_DEEP), 1024

    def ep(accs, ex, outs):
        outs[0][0] = accs[0].astype(BF16)
        outs[0][1] = accs[1].astype(BF16)

    return _matmul(
        name, dab, h, (), grid=(NDEV, D // tn, T // tt),
        a_spec=Block((None, 2, FF_SHARD, tt), lambda j, n, k: (j, 0, 0, k)),
        b_spec=Block((tt, tn), lambda j, n, k: (k, n)),
        extra_specs=(),
        out_shapes=[SDS((NDEV, 2, FF_SHARD, D), BF16)],
        out_specs=[Block((None, 2, FF_SHARD, tn), lambda j, n, k: (j, 0, 0, n))],
        acc_shape=(FF_SHARD, tn), nb=2,
        compute=lambda a_ref, b_ref, q: _dot(a_ref[q], b_ref[...], NN), epilogue=ep)[0]


def _shards_per_step(ns):
    return max(g for g in (1, 2, 4, 8) if g * ns <= 2048)


def _split_lanes(g, ns):
    def ep(accs, ex, outs):
        for q in range(g):
            outs[0][q] = accs[0][:, q * ns:(q + 1) * ns].astype(outs[0].dtype)
    return ep


def _col_fwd(name, a, w, out_dtype=BF16, after=None):
    T, K = a.shape
    ns = w.shape[2]
    g = _shards_per_step(ns)
    tm = _tile(T, 1024)
    return _matmul(
        name, a, w, (), after=after, grid=(NDEV // g, T // tm, 1),
        a_spec=Block((tm, K), lambda j, i, k: (i, 0)),
        b_spec=Block((g, K, ns), lambda j, i, k: (j, 0, 0)),
        extra_specs=(),
        out_shapes=[SDS((T, NDEV * ns), out_dtype)],
        out_specs=[Block((tm, g * ns), lambda j, i, k: (i, j))],
        acc_shape=(tm, g * ns), compute=_wide_b(g), epilogue=_store())[0]


def _col_bwd_a(name, dy, w, add=None):
    T = dy.shape[0]
    _, K, ns = w.shape
    tm, tn = _tile(T, 1024), _tile(K, 1024)

    def ep(accs, ex, outs):
        r = accs[0]
        if add is not None:
            r = r + ex[0][...].astype(F32)
        outs[0][...] = r.astype(BF16)

    extras = () if add is None else (add,)
    return _matmul(
        name, dy, w, extras, grid=(T // tm, K // tn, 1),
        a_spec=Block((tm, NDEV * ns), lambda i, j, k: (i, 0)),
        b_spec=Block((NDEV, tn, ns), lambda i, j, k: (0, j, 0)),
        extra_specs=[Block((tm, tn), lambda i, j, k: (i, j))] * len(extras),
        out_shapes=[SDS((T, K), BF16)],
        out_specs=[Block((tm, tn), lambda i, j, k: (i, j))],
        acc_shape=(tm, tn), compute=_ksum_lanes(NDEV, ns, NT), epilogue=ep)[0]


def _col_bwd_w(name, a, dy, ns):
    T, K = a.shape
    g = _shards_per_step(ns)
    tt, tr = _tile(T, TT_DEEP), _tile(K, 1024)
    return _matmul(
        name, a, dy, (), grid=(NDEV // g, K // tr, T // tt),
        a_spec=Block((tt, tr), lambda j, n, k: (k, n)),
        b_spec=Block((tt, g * ns), lambda j, n, k: (k, j)),
        extra_specs=(),
        out_shapes=[SDS((NDEV, K, ns), BF16)],
        out_specs=[Block((g, tr, ns), lambda j, n, k: (j, n, 0))],
        acc_shape=(tr, g * ns), dims=TN, epilogue=_split_lanes(g, ns))[0]


def _gate_fwd(name, h, w, bias):
    T, K = h.shape
    ns = w.shape[2]
    g = 2
    per = D_MODEL // (g * ns)
    tm = _tile(T, 1024)

    def ep(accs, ex, outs):
        outs[0][...] = (accs[0] + ex[0][...]).astype(BF16)

    return _matmul(
        name, h, w, (bias,), grid=(NDEV // g, T // tm, 1),
        a_spec=Block((tm, K), lambda j, i, k: (i, 0)),
        b_spec=Block((g, K, ns), lambda j, i, k: (j, 0, 0)),
        extra_specs=[Block((None, 1, g * ns), lambda j, i, k: (j // per, 0, j % per))],
        out_shapes=[SDS((2, T, D_MODEL), BF16)],
        out_specs=[Block((None, tm, g * ns), lambda j, i, k: (j // per, i, j % per))],
        acc_shape=(tm, g * ns), compute=_wide_b(g), epilogue=ep)[0]


def _gate_bwd_a(name, dgl, w):
    _, T, _ = dgl.shape
    _, K, ns = w.shape
    per = D_MODEL // ns
    tm, tn = _tile(T, 1024), 1024

    def compute(a_ref, b_ref, _):
        part = None
        for q in range(NDEV):
            d = _dot(a_ref[q // per, :, (q % per) * ns:(q % per + 1) * ns], b_ref[q], NT)
            part = d if part is None else part + d
        return part

    return _matmul(
        name, dgl, w, (), grid=(T // tm, K // tn, 1),
        a_spec=Block((2, tm, D_MODEL), lambda i, j, k: (0, i, 0)),
        b_spec=Block((NDEV, tn, ns), lambda i, j, k: (0, j, 0)),
        extra_specs=(),
        out_shapes=[SDS((T, K), BF16)],
        out_specs=[Block((tm, tn), lambda i, j, k: (i, j))],
        acc_shape=(tm, tn), compute=compute, epilogue=_store())[0]


def _gate_bwd_w(name, h, dgl, ns):
    T, K = h.shape
    g = 2
    per = D_MODEL // (g * ns)
    tt, tr = _tile(T, TT_DEEP), 1024
    return _matmul(
        name, h, dgl, (), grid=(NDEV // g, K // tr, T // tt),
        a_spec=Block((tt, tr), lambda j, n, k: (k, n)),
        b_spec=Block((None, tt, g * ns), lambda j, n, k: (j // per, k, j % per)),
        extra_specs=(),
        out_shapes=[SDS((NDEV, K, ns), BF16)],
        out_specs=[Block((g, tr, ns), lambda j, n, k: (j, n, 0))],
        acc_shape=(tr, g * ns), dims=TN, epilogue=_split_lanes(g, ns))[0]


def _plain_fwd_res(name, a, w, xres):
    T, K = a.shape
    N = w.shape[1]
    tm, tn = _tile(T, 1024), _tile(N, 1024)

    def ep(accs, ex, outs):
        outs[0][...] = ex[0][...] + accs[0]

    return _matmul(
        name, a, w, (xres,), grid=(T // tm, N // tn, 1),
        a_spec=Block((tm, K), lambda i, j, k: (i, 0)),
        b_spec=Block((K, tn), lambda i, j, k: (0, j)),
        extra_specs=[Block((tm, tn), lambda i, j, k: (i, j))],
        out_shapes=[SDS((T, N), F32)],
        out_specs=[Block((tm, tn), lambda i, j, k: (i, j))],
        acc_shape=(tm, tn), dims=NN, nb=None, epilogue=ep)[0]


def _plain_bwd_a(name, dy, w):
    T, N = dy.shape
    K = w.shape[0]
    tm, tn = _tile(T, 1024), _tile(K, 1024)
    return _matmul(
        name, dy, w, (), grid=(T // tm, K // tn, 1),
        a_spec=Block((tm, N), lambda i, j, k: (i, 0)),
        b_spec=Block((tn, N), lambda i, j, k: (j, 0)),
        extra_specs=(),
        out_shapes=[SDS((T, K), BF16)],
        out_specs=[Block((tm, tn), lambda i, j, k: (i, j))],
        acc_shape=(tm, tn), dims=NT, nb=None, epilogue=_store())[0]


def _plain_bwd_w(name, a, dy):
    T, K = a.shape
    N = dy.shape[1]
    tt, tr, tn = _tile(T, TT_DEEP), _tile(K, 1024), _tile(N, 1024)
    return _matmul(
        name, a, dy, (), grid=(K // tr, N // tn, T // tt),
        a_spec=Block((tt, tr), lambda m, n, k: (k, m)),
        b_spec=Block((tt, tn), lambda m, n, k: (k, n)),
        extra_specs=(),
        out_shapes=[SDS((K, N), BF16)],
        out_specs=[Block((tr, tn), lambda m, n, k: (m, n))],
        acc_shape=(tr, tn), dims=TN, nb=None, epilogue=_store())[0]


def _rms_fwd(name, x, g):
    T, D = x.shape
    tm = _tile(T, 512)

    def body(x_ref, g_ref, h_ref):
        xv = x_ref[...]
        r = lax.rsqrt(jnp.mean(xv * xv, axis=-1, keepdims=True) + NORM_EPS)
        h_ref[...] = (xv * r * g_ref[...]).astype(BF16)

    return pl.pallas_call(
        body, name=name, grid=(T // tm,),
        in_specs=[Block((tm, D), lambda i: (i, 0)), Block((1, D), lambda i: (0, 0))],
        out_specs=Block((tm, D), lambda i: (i, 0)), out_shape=SDS((T, D), BF16),
        compiler_params=_cparams(("arbitrary",), VMEM_BIG))(x, g)


def _rms_bwd(name, dh, x, g, dxin, out_dtype):
    T, D = x.shape
    tm = _tile(T, 512)

    def body(dh_ref, x_ref, g_ref, dxin_ref, dx_ref, dg_ref):
        i = pl.program_id(0)
        xv = x_ref[...]
        dh = dh_ref[...].astype(F32)
        r = lax.rsqrt(jnp.mean(xv * xv, axis=-1, keepdims=True) + NORM_EPS)
        xh = xv * r
        gd = dh * g_ref[...]
        dx = dxin_ref[...].astype(F32) + r * (gd - xh * jnp.mean(gd * xh, axis=-1, keepdims=True))
        dx_ref[...] = dx.astype(out_dtype)
        dgp = jnp.sum(dh * xh, axis=0, keepdims=True)

        @pl.when(i == 0)
        def _():
            dg_ref[...] = dgp

        @pl.when(i > 0)
        def _():
            dg_ref[...] += dgp

    row = Block((tm, D), lambda i: (i, 0))
    vec = Block((1, D), lambda i: (0, 0))
    return pl.pallas_call(
        body, name=name, grid=(T // tm,),
        in_specs=[row, row, vec, row], out_specs=[row, vec],
        out_shape=[SDS((T, D), out_dtype), SDS((1, D), F32)],
        compiler_params=_cparams(("arbitrary",), VMEM_BIG))(dh, x, g, dxin)


def _loss_head(name, x, g, tgt):
    T, D = x.shape
    tm = _tile(T, 512)

    def body(x_ref, g_ref, t_ref, loss_ref, dxb_ref, dg_ref):
        i = pl.program_id(0)
        xv = x_ref[...]
        gv = g_ref[...]
        r = lax.rsqrt(jnp.mean(xv * xv, axis=-1, keepdims=True) + NORM_EPS)
        xh = xv * r
        err = xh * gv - t_ref[...]
        lp = 0.5 * jnp.sum(jnp.mean(err * err, axis=-1, keepdims=True), axis=0, keepdims=True)
        dout = err * (1.0 / D)
        gd = dout * gv
        dx = r * (gd - xh * jnp.mean(gd * xh, axis=-1, keepdims=True))
        dxb_ref[...] = dx.astype(BF16)
        dgp = jnp.sum(dout * xh, axis=0, keepdims=True)
        lpb = jnp.broadcast_to(lp, (1, 128))

        @pl.when(i == 0)
        def _():
            dg_ref[...] = dgp
            loss_ref[...] = lpb

        @pl.when(i > 0)
        def _():
            dg_ref[...] += dgp
            loss_ref[...] += lpb

    row = Block((tm, D), lambda i: (i, 0))
    vec = Block((1, D), lambda i: (0, 0))
    return pl.pallas_call(
        body, name=name, grid=(T // tm,),
        in_specs=[row, vec, row], out_specs=[Block((1, 128), lambda i: (0, 0)), row, vec],
        out_shape=[SDS((1, 128), F32), SDS((T, D), BF16), SDS((1, D), F32)],
        compiler_params=_cparams(("arbitrary",), VMEM_BIG))(x, g, tgt)


def _merge_fwd(name, pa, pb, gl):
    T, D = pa.shape
    tm = _tile(T, 512)

    def body(pa_ref, pb_ref, gl_ref, o_ref):
        ga = _sigmoid(gl_ref[0].astype(F32))
        gb = _sigmoid(gl_ref[1].astype(F32))
        o_ref[...] = (ga * pa_ref[...].astype(F32) + gb * pb_ref[...].astype(F32)).astype(BF16)

    row = Block((tm, D), lambda i: (i, 0))
    return pl.pallas_call(
        body, name=name, grid=(T // tm,),
        in_specs=[row, row, Block((2, tm, D), lambda i: (0, i, 0))], out_specs=row,
        out_shape=SDS((T, D), BF16), compiler_params=_cparams(("arbitrary",), VMEM_BIG))(pa, pb, gl)


def _merge_bwd(name, dm, pa, pb, gl):
    T, D = pa.shape
    tm = _tile(T, 512)

    def body(dm_ref, pa_ref, pb_ref, gl_ref, dpa_ref, dpb_ref, dgl_ref, db_ref):
        i = pl.program_id(0)
        dmv = dm_ref[...].astype(F32)
        ga = _sigmoid(gl_ref[0].astype(F32))
        gb = _sigmoid(gl_ref[1].astype(F32))
        dpa_ref[...] = (dmv * ga).astype(BF16)
        dpb_ref[...] = (dmv * gb).astype(BF16)
        dga = dmv * pa_ref[...].astype(F32) * ga * (1.0 - ga)
        dgb = dmv * pb_ref[...].astype(F32) * gb * (1.0 - gb)
        dgl_ref[0] = dga.astype(BF16)
        dgl_ref[1] = dgb.astype(BF16)
        sa = jnp.sum(dga, axis=0, keepdims=True)
        sb = jnp.sum(dgb, axis=0, keepdims=True)

        @pl.when(i == 0)
        def _():
            db_ref[0] = sa
            db_ref[1] = sb

        @pl.when(i > 0)
        def _():
            db_ref[0] += sa
            db_ref[1] += sb

    row = Block((tm, D), lambda i: (i, 0))
    two = Block((2, tm, D), lambda i: (0, i, 0))
    return pl.pallas_call(
        body, name=name, grid=(T // tm,),
        in_specs=[row, row, row, two], out_specs=[row, row, two, Block((2, 1, D), lambda i: (0, 0, 0))],
        out_shape=[SDS((T, D), BF16), SDS((T, D), BF16), SDS((2, T, D), BF16), SDS((2, 1, D), F32)],
        compiler_params=_cparams(("arbitrary",), VMEM_BIG))(dm, pa, pb, gl)


def _sgu_core(ur, vr, lng, lnb, ws_ref, bs_ref):
    tm = ur.shape[0]
    gu = _gelu(ur)
    gv = _gelu(vr)
    mu = jnp.mean(gv, axis=-1, keepdims=True)
    cen = gv - mu
    rstd = lax.rsqrt(jnp.mean(cen * cen, axis=-1, keepdims=True) + NORM_EPS)
    xhat = cen * rstd
    vn = (xhat * lng + lnb).astype(BF16)
    rows = []
    for n in range(tm // MLP_CHUNK):
        cols = []
        for h in range(SGU_HEADS):
            blk = vn[n * MLP_CHUNK:(n + 1) * MLP_CHUNK, h * 128:(h + 1) * 128]
            cols.append(_dot(ws_ref[h], blk) + bs_ref[h])
        rows.append(jnp.concatenate(cols, axis=1))
    mixed = jnp.concatenate(rows, axis=0) if len(rows) > 1 else rows[0]
    return gu, xhat, rstd, vn, mixed


def _sgu_fwd(name, proj, lng, lnb, wsm, bst):
    T = proj.shape[0]
    W = SGU_WIDTH
    tm = _tile(T, 512)

    def body(u_ref, v_ref, lng_ref, lnb_ref, ws_ref, bs_ref, o_ref):
        gu, _, _, _, mixed = _sgu_core(u_ref[...].astype(F32), v_ref[...].astype(F32), lng_ref[...], lnb_ref[...],
                                       ws_ref, bs_ref)
        o_ref[...] = (gu * mixed).astype(BF16)

    vec = Block((1, W), lambda i: (0, 0))
    return pl.pallas_call(
        body, name=name, grid=(T // tm,),
        in_specs=[Block((tm, W), lambda i: (i, 1)), Block((tm, W), lambda i: (i, 2)), vec, vec,
                  Block((SGU_HEADS, 128, 128), lambda i: (0, 0, 0)), Block((SGU_HEADS, 128, 128), lambda i: (0, 0, 0))],
        out_specs=Block((tm, W), lambda i: (i, 0)), out_shape=SDS((T, W), BF16),
        compiler_params=_cparams(("arbitrary",), VMEM_BIG))(proj, proj, lng, lnb, wsm, bst)


def _sgu_bwd(name, dyb, dua, proj, lng, lnb, wsm, wsmt, bst):
    T = proj.shape[0]
    W = SGU_WIDTH
    tm = _tile(T, 512)

    def body(dy_ref, dua_ref, u_ref, v_ref, lng_ref, lnb_ref, ws_ref, wst_ref, bs_ref,
             duv_ref, dws_ref, dbs_ref, dlng_ref, dlnb_ref):
        i = pl.program_id(0)
        duv_ref[:, :W] = dua_ref[...]
        ur = u_ref[...].astype(F32)
        vr = v_ref[...].astype(F32)
        lng_v = lng_ref[...]
        gu, xhat, rstd, vn, mixed = _sgu_core(ur, vr, lng_v, lnb_ref[...], ws_ref, bs_ref)
        dy = dy_ref[...].astype(F32)
        dgu = dy * mixed
        dmix = dy * gu
        dmb = dmix.astype(BF16)
        dws_p, dbs_p, rows = [], [], []
        for h in range(SGU_HEADS):
            acc_w = jnp.zeros((128, 128), F32)
            acc_b = jnp.zeros((128, 1), F32)
            for n in range(tm // MLP_CHUNK):
                r0 = n * MLP_CHUNK
                dmt = dmb[r0:r0 + MLP_CHUNK, h * 128:(h + 1) * 128]
                acc_w = acc_w + _dot(dmt, vn[r0:r0 + MLP_CHUNK, h * 128:(h + 1) * 128], NT)
                acc_b = acc_b + jnp.sum(dmix[r0:r0 + MLP_CHUNK, h * 128:(h + 1) * 128], axis=1, keepdims=True)
            dws_p.append(acc_w)
            dbs_p.append(jnp.broadcast_to(acc_b, (128, 128)))
        for n in range(tm // MLP_CHUNK):
            r0 = n * MLP_CHUNK
            rows.append(jnp.concatenate(
                [_dot(wst_ref[h], dmb[r0:r0 + MLP_CHUNK, h * 128:(h + 1) * 128]) for h in range(SGU_HEADS)], axis=1))
        dvn = jnp.concatenate(rows, axis=0) if len(rows) > 1 else rows[0]
        dlng_p = jnp.sum(dvn * xhat, axis=0, keepdims=True)
        dlnb_p = jnp.sum(dvn, axis=0, keepdims=True)
        dxh = dvn * lng_v
        dgv = rstd * (dxh - jnp.mean(dxh, axis=-1, keepdims=True)
                      - xhat * jnp.mean(dxh * xhat, axis=-1, keepdims=True))
        duv_ref[:, W:2 * W] = (dgu * _gelu_grad(ur)).astype(BF16)
        duv_ref[:, 2 * W:] = (dgv * _gelu_grad(vr)).astype(BF16)

        @pl.when(i == 0)
        def _():
            for h in range(SGU_HEADS):
                dws_ref[h] = dws_p[h]
                dbs_ref[h] = dbs_p[h]
            dlng_ref[...] = dlng_p
            dlnb_ref[...] = dlnb_p

        @pl.when(i > 0)
        def _():
            for h in range(SGU_HEADS):
                dws_ref[h] += dws_p[h]
                dbs_ref[h] += dbs_p[h]
            dlng_ref[...] += dlng_p
            dlnb_ref[...] += dlnb_p

    vec = Block((1, W), lambda i: (0, 0))
    wsb = Block((SGU_HEADS, 128, 128), lambda i: (0, 0, 0))
    hsq = SDS((SGU_HEADS, 128, 128), F32)
    return pl.pallas_call(
        body, name=name, grid=(T // tm,),
        in_specs=[Block((tm, W), lambda i: (i, 0)), Block((tm, W), lambda i: (i, 0)),
                  Block((tm, W), lambda i: (i, 1)), Block((tm, W), lambda i: (i, 2)),
                  vec, vec, wsb, wsb, wsb],
        out_specs=[Block((tm, 3 * W), lambda i: (i, 0)), wsb, wsb, vec, vec],
        out_shape=[SDS((T, 3 * W), BF16), hsq, hsq, SDS((1, W), F32), SDS((1, W), F32)],
        compiler_params=_cparams(("arbitrary",), VMEM_BIG))(dyb, dua, proj, proj, lng, lnb, wsm, wsmt, bst)


def _s5_disc(lr, li, ldt, brt, bit):
    dt = jnp.exp(ldt)
    decay = jnp.exp(lr * dt)
    abr = decay * jnp.cos(li * dt)
    abi = decay * jnp.sin(li * dt)
    denom = lr * lr + li * li
    nr = abr - 1.0
    ni = abi
    kr = (nr * lr + ni * li) / denom
    ki = (ni * lr - nr * li) / denom
    bkr = kr[None] * brt - ki[None] * bit
    bki = kr[None] * bit + ki[None] * brt
    return abr, abi, bkr, bki


def _s5_prep(lr, li, ldt, brt, bit):
    G, P, C = S5_GROUPS, S5_STATE, S5_GROUP_WIDTH

    def body(lr_ref, li_ref, ldt_ref, br_ref, bi_ref, abr_ref, abi_ref, pwr_ref, pwi_ref, bkr_ref, bki_ref):
        lr_, li_, ldt_ = lr_ref[...], li_ref[...], ldt_ref[...]
        res = _s5_disc(lr_, li_, ldt_, br_ref[...], bi_ref[...])
        for o, r in zip((abr_ref, abi_ref, bkr_ref, bki_ref), res):
            o[...] = r
        dt = jnp.exp(ldt_)
        n = lax.broadcasted_iota(jnp.int32, (S5_SEG, G, P), 0).astype(F32) + 1.0
        dec = jnp.exp((lr_ * dt)[None] * n)
        ang = (li_ * dt)[None] * n
        pwr_ref[...] = dec * jnp.cos(ang)
        pwi_ref[...] = dec * jnp.sin(ang)

    gp = SDS((G, P), F32)
    sgp = SDS((S5_SEG, G, P), F32)
    cgp = SDS((C, G, P), F32)
    return pl.pallas_call(body, name="s5_prep", out_shape=[gp, gp, sgp, sgp, cgp, cgp])(lr, li, ldt, brt, bit)


def _s5_prep_bwd(lr, li, ldt, brt, bit, dabr, dabi, dbkr, dbki):
    G, P, C = S5_GROUPS, S5_STATE, S5_GROUP_WIDTH

    def body(lr_ref, li_ref, ldt_ref, br_ref, bi_ref, dabr_ref, dabi_ref, dbkr_ref, dbki_ref,
             o_lr, o_li, o_ldt, o_br, o_bi):
        _, pull = jax.vjp(_s5_disc, lr_ref[...], li_ref[...], ldt_ref[...], br_ref[...], bi_ref[...])
        g = pull((dabr_ref[...], dabi_ref[...], dbkr_ref[...], dbki_ref[...]))
        for o, r in zip((o_lr, o_li, o_ldt, o_br, o_bi), g):
            o[...] = r

    gp = SDS((G, P), F32)
    cgp = SDS((C, G, P), F32)
    return pl.pallas_call(body, name="s5_prep_bwd", out_shape=[gp, gp, SDS((G, 1), F32), cgp, cgp])(
        lr, li, ldt, brt, bit, dabr, dabi, dbkr, dbki)


def _s5_scan(buf_ref, ar_row, ai_row, pwr_ref, pwi_ref, carry_ref, LG, xs_ref=None, dar_ref=None, dai_ref=None):
    reverse = xs_ref is not None
    NS, SEG = S5_NS, S5_SEG
    sgn = -1.0 if reverse else 1.0
    for lg in range(NS // LG):
        cr = slice(lg * LG, (lg + 1) * LG)
        ci = slice(NS + lg * LG, NS + (lg + 1) * LG)
        ar1, ai1 = ar_row[:, cr], sgn * ai_row[:, cr]
        asr1, asi1 = pwr_ref[SEG - 1:SEG, cr], sgn * pwi_ref[SEG - 1:SEG, cr]
        ar = jnp.broadcast_to(ar1, (8, LG))
        ai = jnp.broadcast_to(ai1, (8, LG))

        def step_of(j):
            return (SEG - 1 - j) if reverse else j

        def p1(j, st):
            sr, si = st
            rows = pl.ds(pl.multiple_of(step_of(j) * 8, 8), 8)
            nr = ar * sr - ai * si + buf_ref[rows, cr]
            ni = ar * si + ai * sr + buf_ref[rows, ci]
            buf_ref[rows, cr] = nr
            buf_ref[rows, ci] = ni
            return nr, ni

        z = jnp.zeros((8, LG), F32)
        er, ei = lax.fori_loop(0, SEG, p1, (z, z), unroll=S5_UNROLL)
        c_r = carry_ref[:, cr]
        c_i = carry_ref[:, ci]
        cs_r, cs_i = [None] * 8, [None] * 8
        order = range(7, -1, -1) if reverse else range(8)
        for s in order:
            cs_r[s], cs_i[s] = c_r, c_i
            e_r, e_i = er[s:s + 1], ei[s:s + 1]
            c_r, c_i = e_r + asr1 * c_r - asi1 * c_i, e_i + asr1 * c_i + asi1 * c_r
        carry_ref[:, cr] = c_r
        carry_ref[:, ci] = c_i
        cmr = jnp.concatenate(cs_r, axis=0)
        cmi = jnp.concatenate(cs_i, axis=0)

        def carried(j):
            pr = pwr_ref[pl.ds(j, 1), cr]
            pi = sgn * pwi_ref[pl.ds(j, 1), cr]
            return pr * cmr - pi * cmi, pr * cmi + pi * cmr

        if not reverse:
            def p2(j, st):
                rows = pl.ds(pl.multiple_of(j * 8, 8), 8)
                wr, wi = carried(j)
                buf_ref[rows, cr] += wr
                buf_ref[rows, ci] += wi
                return st

            lax.fori_loop(0, SEG, p2, 0, unroll=S5_UNROLL)
        else:
            def p2(j, st):
                pr, pi, dr, di = st
                rows = pl.ds(pl.multiple_of(step_of(j) * 8, 8), 8)
                xr = xs_ref[rows, cr]
                xi = xs_ref[rows, ci]
                dr = dr + pr * xr + pi * xi
                di = di + pi * xr - pr * xi
                wr, wi = carried(j)
                gr = buf_ref[rows, cr] + wr
                gi = buf_ref[rows, ci] + wi
                buf_ref[rows, cr] = gr
                buf_ref[rows, ci] = gi
                return gr, gi, dr, di

            st = lax.fori_loop(0, SEG, p2, (cmr, cmi, z, z), unroll=S5_UNROLL)
            dar_ref[:, cr] += st[2]
            dai_ref[:, cr] += st[3]


def _s5_fwd(proj, perm, permt, bdbr, bdbi, bdcr, bdci, abr, abi, asr, asi, dvec, wglu, bglu):
    T = proj.shape[0]
    TC, NS, W = S5_TC, S5_NS, S5_WIDTH
    nc = T // TC

    def body(u_ref, pm_ref, pmt_ref, bdbr_ref, bdbi_ref, bdcr_ref, bdci_ref, ar_ref, ai_ref, asr_ref, asi_ref,
             d_ref, wglu_ref, bglu_ref, ya_ref, xs_ref, ypre_ref, carry_ref):
        i = pl.program_id(0)

        @pl.when(i == 0)
        def _():
            carry_ref[...] = jnp.zeros_like(carry_ref)

        up = _dot(pm_ref[...], u_ref[...]).astype(BF16)
        for j in range(8):
            ut = up[:, j * 128:(j + 1) * 128]
            xs_ref[:, j * 512:(j + 1) * 512] = _dot(ut, bdbr_ref[j])
            xs_ref[:, NS + j * 512:NS + (j + 1) * 512] = _dot(ut, bdbi_ref[j])
        _s5_scan(xs_ref, ar_ref[...], ai_ref[...], asr_ref, asi_ref, carry_ref, S5_LG)
        ys = []
        for j in range(8):
            xr = xs_ref[:, j * 512:(j + 1) * 512].astype(BF16)
            xi = xs_ref[:, NS + j * 512:NS + (j + 1) * 512].astype(BF16)
            ys.append(_dot(xr, bdcr_ref[j]) + _dot(xi, bdci_ref[j]))
        ypre = jnp.concatenate(ys, axis=1) + d_ref[...] * up.astype(F32)
        ypre_ref[...] = ypre
        ya = _gelu(ypre)
        zl = _dot(ya.astype(BF16), wglu_ref[...]) + bglu_ref[...]
        outp = (ya * _sigmoid(zl)).astype(BF16)
        ya_ref[...] = _dot(pmt_ref[...], outp).astype(BF16)

    return pl.pallas_call(
        body, name="s5_fwd", grid=(nc,),
        in_specs=[Block((TC, W), lambda i: (i, 0)), _const((TC, TC)), _const((TC, TC)),
                  _const((8, 128, 512)), _const((8, 128, 512)), _const((8, 512, 128)), _const((8, 512, 128)),
                  _const((1, NS)), _const((1, NS)), _const((S5_SEG, NS)), _const((S5_SEG, NS)),
                  _const((1, W)), _const((W, W)), _const((1, W))],
        out_specs=[Block((TC, W), lambda i: (i, 0)), Block((TC, 2 * NS), lambda i: (i, 0)),
                   Block((TC, W), lambda i: (i, 0))],
        out_shape=[SDS((T, W), BF16), SDS((T, 2 * NS), F32), SDS((T, W), F32)],
        scratch_shapes=[pltpu.VMEM((1, 2 * NS), F32)],
        compiler_params=_cparams(("arbitrary",), VMEM_BIG),
    )(proj, perm, permt, bdbr, bdbi, bdcr, bdci, abr, abi, asr, asi, dvec, wglu, bglu)


def _s5_bwd(dya, proj, ypre, xs, perm, permt, bdbr, bdbi, bdcr, bdci, abr, abi, asr, asi, dvec, wglu, bglu):
    T = proj.shape[0]
    TC, NS, W = S5_TC, S5_NS, S5_WIDTH
    nc = T // TC

    def body(dya_ref, u_ref, ypre_ref, xs_ref, pm_ref, pmt_ref, bdbr_ref, bdbi_ref, bdcr_ref, bdci_ref,
             ar_ref, ai_ref, asr_ref, asi_ref, d_ref, wglu_ref, bglu_ref,
             du_ref, dar_ref, dai_ref, dd_ref, dbglu_ref, o_dbdbr, o_dbdbi, o_dbdcr, o_dbdci, o_dwglu,
             g_ref, carry_ref, dbdbr_ref, dbdbi_ref, dbdcr_ref, dbdci_ref, dwglu_ref):
        i = pl.program_id(0)

        @pl.when(i == 0)
        def _():
            carry_ref[...] = jnp.zeros_like(carry_ref)
            for r in (dbdbr_ref, dbdbi_ref, dbdcr_ref, dbdci_ref, dar_ref, dai_ref, dd_ref, dwglu_ref, dbglu_ref):
                r[...] = jnp.zeros_like(r)

        pm = pm_ref[...]
        dyo = _dot(pm, dya_ref[...])
        up = _dot(pm, u_ref[...]).astype(BF16)
        upf = up.astype(F32)
        ypre_v = ypre_ref[...]
        ya = _gelu(ypre_v)
        yab = ya.astype(BF16)
        sg = _sigmoid(_dot(yab, wglu_ref[...]) + bglu_ref[...])
        dz = dyo * ya * sg * (1.0 - sg)
        dzb = dz.astype(BF16)
        dya_t = dyo * sg + _dot(dzb, wglu_ref[...], NT)
        dwglu_ref[...] += _dot(yab, dzb, TN)
        dbglu_ref[...] += jnp.sum(dz, axis=0, keepdims=True)
        dy = dya_t * _gelu_grad(ypre_v)
        dd_ref[...] += jnp.sum(dy * upf, axis=0, keepdims=True)
        dyb = dy.astype(BF16)
        for j in range(8):
            dyj = dyb[:, j * 128:(j + 1) * 128]
            g_ref[:, j * 512:(j + 1) * 512] = _dot(dyj, bdcr_ref[j], NT)
            g_ref[:, NS + j * 512:NS + (j + 1) * 512] = _dot(dyj, bdci_ref[j], NT)
            dbdcr_ref[j] += _dot(xs_ref[:, j * 512:(j + 1) * 512].astype(BF16), dyj, TN)
            dbdci_ref[j] += _dot(xs_ref[:, NS + j * 512:NS + (j + 1) * 512].astype(BF16), dyj, TN)
        _s5_scan(g_ref, ar_ref[...], ai_ref[...], asr_ref, asi_ref, carry_ref, S5_LG,
                 xs_ref=xs_ref, dar_ref=dar_ref, dai_ref=dai_ref)
        dus = []
        for j in range(8):
            ut = up[:, j * 128:(j + 1) * 128]
            gr = g_ref[:, j * 512:(j + 1) * 512].astype(BF16)
            gi = g_ref[:, NS + j * 512:NS + (j + 1) * 512].astype(BF16)
            dbdbr_ref[j] += _dot(ut, gr, TN)
            dbdbi_ref[j] += _dot(ut, gi, TN)
            dus.append(_dot(gr, bdbr_ref[j], NT) + _dot(gi, bdbi_ref[j], NT))
        dup = jnp.concatenate(dus, axis=1) + d_ref[...] * dy
        du_ref[...] = _dot(pmt_ref[...], dup.astype(BF16)).astype(BF16)

        @pl.when(i == nc - 1)
        def _():
            for src, dst in ((dbdbr_ref, o_dbdbr), (dbdbi_ref, o_dbdbi), (dbdcr_ref, o_dbdcr),
                             (dbdci_ref, o_dbdci), (dwglu_ref, o_dwglu)):
                pltpu.sync_copy(src, dst)

    c2 = lambda i: (0, 0)
    rev = lambda i: (nc - 1 - i, 0)
    return pl.pallas_call(
        body, name="s5_bwd", grid=(nc,),
        in_specs=[Block((TC, W), rev), Block((TC, W), rev), Block((TC, W), rev), Block((TC, 2 * NS), rev),
                  _const((TC, TC)), _const((TC, TC)),
                  _const((8, 128, 512)), _const((8, 128, 512)), _const((8, 512, 128)), _const((8, 512, 128)),
                  _const((1, NS)), _const((1, NS)), _const((S5_SEG, NS)), _const((S5_SEG, NS)),
                  _const((1, W)), _const((W, W)), _const((1, W))],
        out_specs=[Block((TC, W), rev), Block((8, NS), c2), Block((8, NS), c2), Block((1, W), c2), Block((1, W), c2),
                   ANY, ANY, ANY, ANY, ANY],
        out_shape=[SDS((T, W), BF16), SDS((8, NS), F32), SDS((8, NS), F32), SDS((1, W), F32), SDS((1, W), F32),
                   SDS((8, 128, 512), F32), SDS((8, 128, 512), F32),
                   SDS((8, 512, 128), F32), SDS((8, 512, 128), F32), SDS((W, W), F32)],
        scratch_shapes=[pltpu.VMEM((TC, 2 * NS), F32), pltpu.VMEM((1, 2 * NS), F32),
                        pltpu.VMEM((8, 128, 512), F32), pltpu.VMEM((8, 128, 512), F32),
                        pltpu.VMEM((8, 512, 128), F32), pltpu.VMEM((8, 512, 128), F32), pltpu.VMEM((W, W), F32)],
        compiler_params=_cparams(("arbitrary",), VMEM_BIG),
    )(dya, proj, ypre, xs, perm, permt, bdbr, bdbi, bdcr, bdci, abr, abi, asr, asi, dvec, wglu, bglu)


def _bd_b(bk_t):
    C, P = S5_GROUP_WIDTH, S5_STATE
    t = jnp.transpose(bk_t, (1, 0, 2)).reshape(8, 8, C, P)
    eye = jnp.eye(8, dtype=t.dtype)
    return (t[:, :, :, None, :] * eye[None, :, None, :, None]).reshape(8, 8 * C, 8 * P)


def _bd_b_extract(m):
    C, P = S5_GROUP_WIDTH, S5_STATE
    t = m.reshape(8, 8, C, 8, P)
    d = jnp.stack([t[:, g, :, g, :] for g in range(8)], axis=1)
    return jnp.transpose(d.reshape(S5_GROUPS, C, P), (1, 0, 2))


def _bd_c(c):
    C, P = S5_GROUP_WIDTH, S5_STATE
    t = jnp.transpose(c, (0, 2, 1)).reshape(8, 8, P, C)
    eye = jnp.eye(8, dtype=t.dtype)
    return (t[:, :, :, None, :] * eye[None, :, None, :, None]).reshape(8, 8 * P, 8 * C)


def _bd_c_extract(m):
    C, P = S5_GROUP_WIDTH, S5_STATE
    t = m.reshape(8, 8, P, 8, C)
    d = jnp.stack([t[:, g, :, g, :] for g in range(8)], axis=1)
    return jnp.transpose(d.reshape(S5_GROUPS, P, C), (0, 2, 1))


def _perm_matrix():
    r = jnp.arange(S5_TC)
    src = (r % 8) * S5_SEG + r // 8
    return (src[:, None] == jnp.arange(S5_TC)[None, :]).astype(BF16)


def _coords():
    return lax.axis_index("x"), lax.axis_index("y"), lax.axis_index("c")


def _all_gather(name, arrs):
    n = len(arrs)

    def body(*refs):
        ins, outs = refs[:n], refs[n:2 * n]
        send_sems, recv_sems, local_sems = refs[2 * n:]
        x, y, c = _coords()
        me, sibling = (x, y, c), (x, y, 1 - c)
        chips = [(1 - x, y), (x, 1 - y), (1 - x, 1 - y)]

        def slot(p):
            return 4 * p[0] + 2 * p[1] + p[2]

        def copy(a, k, block, to, src=None):
            dst = outs[a].at[slot(block)]
            return pltpu.make_async_remote_copy(
                src_ref=dst if src is None else src, dst_ref=dst,
                send_sem=send_sems.at[a * 7 + k], recv_sem=recv_sems.at[a * 7 + k],
                device_id=to, device_id_type=MESH)

        mine = [pltpu.make_async_copy(ins[a], outs[a].at[slot(me)], local_sems.at[a]) for a in range(n)]
        for m in mine:
            m.start()
        first = []
        for a in range(n):
            first.append(copy(a, 0, me, sibling, src=ins[a]))
            first += [copy(a, 1 + j, me, (*chip, c), src=ins[a]) for j, chip in enumerate(chips)]
        for cp in first:
            cp.start()
        passed = []
        for j, chip in enumerate(chips):
            for a in range(n):
                copy(a, 1 + j, (*chip, c), me).wait_recv()
                fw = copy(a, 4 + j, (*chip, c), sibling)
                fw.start()
                passed.append(fw)
        for a in range(n):
            copy(a, 0, sibling, me).wait_recv()
            for j, chip in enumerate(chips):
                copy(a, 4 + j, (*chip, 1 - c), me).wait_recv()
        for cp in first + passed:
            cp.wait_send()
        for m in mine:
            m.wait()

    return pl.pallas_call(
        body, name=name,
        in_specs=[ANY] * n, out_specs=[ANY] * n,
        out_shape=[SDS((NDEV,) + a.shape, a.dtype) for a in arrs],
        scratch_shapes=[pltpu.SemaphoreType.DMA((7 * n,)), pltpu.SemaphoreType.DMA((7 * n,)),
                        pltpu.SemaphoreType.DMA((n,))],
    )(*arrs)


HBM = pl.BlockSpec(memory_space=pltpu.HBM)
SEM = pl.BlockSpec(memory_space=pltpu.SEMAPHORE)
EFFECT = pltpu.SideEffectType.DATAFLOW_SIDE_EFFECTING


def _peers7(x, y, c):
    return [(1 - x if fx else x, 1 - y if fy else y, 1 - c if fc else c)
            for fx in (0, 1) for fy in (0, 1) for fc in (0, 1) if fx or fy or fc]


def _slot(p):
    return 4 * p[0] + 2 * p[1] + p[2]


def _split_copies(src_refs, land_refs, send_sems, recv_sems, gather, mine):
    x, y, c = _coords()
    me = (x, y, c)
    out = []
    for a, (src, land) in enumerate(zip(src_refs, land_refs)):
        for k, p in enumerate(_peers7(x, y, c)):
            s = src if gather else src.at[_slot(p)]
            out.append(pltpu.make_async_remote_copy(
                src_ref=s, dst_ref=land.at[_slot(me) if mine else _slot(p)],
                send_sem=send_sems.at[a * 7 + k], recv_sem=recv_sems.at[a * 7 + k],
                device_id=p, device_id_type=MESH))
    return out


def _own_slab(shard):
    x, y, c = _coords()
    z = lax.empty((NDEV,) + shard.shape, shard.dtype)
    return lax.dynamic_update_slice(z, shard[None], (_slot((x, y, c)),) + (0,) * shard.ndim)


def _split_start(name, srcs, lands, gather):
    n = len(srcs)

    def body(*refs):
        src_refs, land_refs = refs[:n], refs[n:2 * n]
        send_sems, recv_sems = refs[2 * n], refs[2 * n + 1]
        token = refs[-1]
        for cp in _split_copies(src_refs, land_refs, send_sems, recv_sems, gather, True):
            cp.start()
        token[...] = jnp.zeros_like(token)

    thru = [pltpu.HBM(a.shape, a.dtype) for a in list(srcs) + list(lands)]
    res = pl.pallas_call(
        body, name=name,
        out_shape=(pltpu.SemaphoreType.DMA((7 * n,)), pltpu.SemaphoreType.DMA((7 * n,)), *thru, SDS((8, 128), F32)),
        in_specs=[HBM] * (2 * n),
        out_specs=(SEM, SEM, *([HBM] * (2 * n)), pl.BlockSpec(memory_space=pltpu.VMEM)),
        input_output_aliases={i: 2 + i for i in range(2 * n)},
        compiler_params=pltpu.CompilerParams(has_side_effects=EFFECT),
    )(*[pltpu.with_memory_space_constraint(a, pltpu.HBM) for a in list(srcs) + list(lands)])
    return res[0], res[1], list(res[2:2 + n]), list(res[2 + n:2 + 2 * n]), res[-1]


def _split_wait(name, started, after, gather):
    send_sems, recv_sems, srcs, lands, _ = started
    n = len(srcs)

    def body(*refs):
        src_refs, land_refs = refs[:n], refs[n:2 * n]
        s_sems, r_sems = refs[2 * n], refs[2 * n + 1]
        for cp in _split_copies(src_refs, land_refs, s_sems, r_sems, gather, False):
            cp.wait_send()
            cp.wait_recv()

    thru = [pltpu.HBM(a.shape, a.dtype) for a in list(srcs) + list(lands)]
    res = pl.pallas_call(
        body, name=name, out_shape=tuple(thru),
        in_specs=[HBM] * (2 * n) + [SEM, SEM, ANY], out_specs=tuple([HBM] * (2 * n)),
        input_output_aliases={i: i for i in range(2 * n)},
        compiler_params=pltpu.CompilerParams(has_side_effects=EFFECT),
    )(*srcs, *lands, send_sems, recv_sems, after)
    return list(res[n:])


def _adam_math(w, g, m, v):
    m = ADAM_B1 * m + (1.0 - ADAM_B1) * g
    v = ADAM_B2 * v + (1.0 - ADAM_B2) * (g * g)
    m_hat = m / (1.0 - ADAM_B1 ** ADAM_STEP)
    v_hat = v / (1.0 - ADAM_B2 ** ADAM_STEP)
    delta = -ADAM_LR * (m_hat / (jnp.sqrt(v_hat) + ADAM_EPS) + ADAM_WD * w)
    return delta, m, v


def _adam_sharded(name, recv, sub, w, m, v):
    R, Cc = w.shape
    tr = max(t for t in range(16, R + 1, 16) if R % t == 0 and t * Cc <= 256 * 1024)

    def body(*refs):
        parts = refs[:NDEV]
        w_ref, m_ref, v_ref, g_out, d_out, m_out, v_out = refs[NDEV:]
        g = parts[0][...].astype(F32)
        for p in parts[1:]:
            g = g + p[...].astype(F32)
        delta, mn, vn = _adam_math(w_ref[...], g, m_ref[...], v_ref[...])
        g_out[...] = g
        d_out[...] = delta
        m_out[...] = mn
        v_out[...] = vn

    if sub is None:
        pspecs = [Block((None, tr, Cc), functools.partial(lambda s, i: (s, i, 0), s)) for s in range(NDEV)]
    else:
        pspecs = [Block((None, None, tr, Cc), functools.partial(lambda s, i: (s, sub, i, 0), s)) for s in range(NDEV)]
    row = Block((tr, Cc), lambda i: (i, 0))
    o = SDS((R, Cc), F32)
    return pl.pallas_call(
        body, name=name, grid=(R // tr,),
        in_specs=pspecs + [row, row, row], out_specs=[row, row, row, row], out_shape=[o, o, o, o],
        compiler_params=_cparams(("arbitrary",), VMEM_BIG))(*([recv] * NDEV), w, m, v)


def _adam_small(groups):
    n = len(groups)

    def body(*refs):
        ins, outs = refs[:4 * n], refs[4 * n:]
        for a in range(n):
            p_ref, w_ref, m_ref, v_ref = ins[4 * a:4 * a + 4]
            g = p_ref[0]
            for s in range(1, NDEV):
                g = g + p_ref[s]
            delta, mn, vn = _adam_math(w_ref[...], g, m_ref[...], v_ref[...])
            for o, r in zip(outs[4 * a:4 * a + 4], (g, delta, mn, vn)):
                o[...] = r

    flat_in = [t for grp in groups for t in grp]
    out_shape = [SDS(grp[1].shape, F32) for grp in groups for _ in range(4)]
    res = pl.pallas_call(body, name="adam_small", out_shape=out_shape,
                         compiler_params=_cparams(None, VMEM_BIG))(*flat_in)
    return [tuple(res[4 * a:4 * a + 4]) for a in range(n)]


_TINY = ["mix_norm", "s5_a_re", "s5_a_im", "s5_log_dt", "s5_d", "s5_b_glu", "sgu_ln_g", "sgu_ln_b",
         "sgu_b_s", "b_gate", "ffn2_norm", "final_norm"]
_ORDER = ["ffn1_norm", "ffn1_w_gate", "ffn1_w_up", "ffn1_w_down", "mix_norm", "w_in", "s5_a_re", "s5_a_im",
          "s5_log_dt", "s5_b_re", "s5_b_im", "s5_c_re", "s5_c_im", "s5_d", "s5_w_glu", "s5_b_glu", "sgu_ln_g",
          "sgu_ln_b", "sgu_w_s", "sgu_b_s", "w_branch_a", "w_branch_b", "w_gate", "b_gate", "w_out", "ffn2_norm",
          "ffn2_w_gate", "ffn2_w_up", "ffn2_w_down", "final_norm"]


def _step(x, tgt, W, M, V):
    T = x.shape[1]
    x0 = x[0]
    tgt0 = tgt[0]
    bf = lambda a: a.astype(BF16)

    def gather_start(name, shards):
        return _split_start(name, shards, [_own_slab(s) for s in shards], True)

    (wgu1,) = _all_gather("gather1", [jnp.stack([bf(W["ffn1_w_gate"][0].T), bf(W["ffn1_w_up"][0].T)])])

    lr_, li_ = W["s5_a_re"][0], W["s5_a_im"][0]
    ldt_ = W["s5_log_dt"][0][:, None]
    brt = jnp.transpose(W["s5_b_re"][0], (2, 0, 1))
    bit = jnp.transpose(W["s5_b_im"][0], (2, 0, 1))
    abr, abi, pwr, pwi, bkr_t, bki_t = _s5_prep(lr_, li_, ldt_, brt, bit)
    bdbr, bdbi = bf(_bd_b(bkr_t)), bf(_bd_b(bki_t))
    bdcr, bdci = bf(_bd_c(W["s5_c_re"][0])), bf(_bd_c(-W["s5_c_im"][0]))
    flat = lambda a: a.reshape(1, S5_NS)
    s5a = (_perm_matrix(), _perm_matrix().T, bdbr, bdbi, bdcr, bdci, flat(abr), flat(abi),
           pwr.reshape(S5_SEG, S5_NS), pwi.reshape(S5_SEG, S5_NS),
           W["s5_d"][0].reshape(1, S5_WIDTH))
    blk = jnp.arange(MLP_CHUNK) // CHUNK
    mask = blk[:, None] >= blk[None, :]
    wsm = jnp.where(mask[None], W["sgu_w_s"][0], 0.0)
    wsm_b, wsmt_b = bf(wsm), bf(jnp.transpose(wsm, (0, 2, 1)))
    bst = jnp.broadcast_to(W["sgu_b_s"][0][:, :, None], (SGU_HEADS, MLP_CHUNK, 128))
    bgate2 = W["b_gate"].reshape(2, 1, D_MODEL)

    h1 = _rms_fwd("rms1", x0, W["ffn1_norm"])
    dep = (wgu1[0, 0, :1, :1] * 0).astype(BF16)

    def later(a):
        return bf(a) + dep[0]

    gs2 = gather_start("gather2_start", [later(W["ffn1_w_down"][0])])
    ab1, f1 = _ffn_up("ffn1_up", h1, wgu1, gs2[4])
    (wd1,) = _split_wait("gather2_wait", gs2, f1, True)
    dep = (wd1[0, :1, :1] * 0).astype(BF16)
    gs3 = gather_start("gather3_start", [later(W["w_in"][0]), later(W["s5_w_glu"][0])])
    x1 = _ffn_down("ffn1_down", f1, wd1, x0, after=gs3[4])
    h2 = _rms_fwd("rms2", x1, W["mix_norm"])
    win, wglu = _split_wait("gather3_wait", gs3, h2, True)
    wglu = wglu.reshape(S5_WIDTH, S5_WIDTH)
    s5c = s5a + (wglu, W["s5_b_glu"])
    dep = (win[0, :1, :1] * 0).astype(BF16)
    gs4 = gather_start("gather4_start", [later(W["w_gate"][0]), later(W["w_branch_a"][0]),
                                         later(W["w_branch_b"][0]), later(W["w_out"][0])])
    proj = _col_fwd("w_in", h2, win, after=gs4[4])
    ya, xs, ypre = _s5_fwd(proj, *s5c)
    dep = (ya[:1, :1] * 0).astype(BF16)
    gs5 = gather_start("gather5_start", [jnp.stack([later(W["ffn2_w_gate"][0].T), later(W["ffn2_w_up"][0].T)])])
    yb = _sgu_fwd("sgu_fwd", proj, W["sgu_ln_g"] + gs5[4][:1, :1], W["sgu_ln_b"], wsm_b, bst)
    wgate, wba, wbb, wout = _split_wait("gather4_wait", gs4, yb, True)
    wout = wout.reshape(D_MODEL, D_MODEL)
    pa = _col_fwd("branch_a", ya, wba)
    pb = _col_fwd("branch_b", yb, wbb)
    gl = _gate_fwd("gate", h2, wgate, bgate2)
    merged = _merge_fwd("merge", pa, pb, gl)
    x2 = _plain_fwd_res("w_out", merged, wout, x1)
    h3 = _rms_fwd("rms3", x2, W["ffn2_norm"])
    (wgu2,) = _split_wait("gather5_wait", gs5, h3, True)
    dep = (wgu2[0, 0, :1, :1] * 0).astype(BF16)
    gs6 = gather_start("gather6_start", [later(W["ffn2_w_down"][0])])
    ab2, f2 = _ffn_up("ffn2_up", h3, wgu2, gs6[4])
    (wd2,) = _split_wait("gather6_wait", gs6, f2, True)
    x3 = _ffn_down("ffn2_down", f2, wd2, x2)
    loss_p, dx3b, dgf = _loss_head("loss_head", x3, W["final_norm"].reshape(1, D_MODEL), tgt0)

    def exchange_start(name, grads):
        x_, y_, c_ = _coords()
        me = _slot((x_, y_, c_))
        return _split_start(name, grads, [_own_slab(lax.dynamic_index_in_dim(g, me, 0, keepdims=False))
                                          for g in grads], False)

    dab2 = _ffn_down_bwd_act("ffn2_down_bwd_a", dx3b, wd2, ab2)
    g_wd2 = _ffn_down_bwd_w("ffn2_down_bwd_w", f2, dx3b)
    g_gu2 = _ffn_up_bwd_w("ffn2_up_bwd_w", h3, dab2)
    es1 = exchange_start("exchange1_start", [g_wd2, g_gu2])
    dh3 = _ffn_up_bwd_h("ffn2_up_bwd_h", dab2, wgu2, es1[4])
    dx2b, dg3 = _rms_bwd("rms3_bwd", dh3, x2, W["ffn2_norm"], dx3b, BF16)

    dmerged = _plain_bwd_a("w_out_bwd_a", dx2b, wout)
    g_wout = _plain_bwd_w("w_out_bwd_w", merged, dx2b)
    dpa, dpb, dgl, dbgate = _merge_bwd("merge_bwd", dmerged, pa, pb, gl)
    dya = _col_bwd_a("branch_a_bwd_a", dpa, wba)
    g_wba = _col_bwd_w("branch_a_bwd_w", ya, dpa, 256)
    dyb = _col_bwd_a("branch_b_bwd_a", dpb, wbb)
    g_wbb = _col_bwd_w("branch_b_bwd_w", yb, dpb, 256)
    dh2g = _gate_bwd_a("gate_bwd_a", dgl, wgate)
    g_wgate = _gate_bwd_w("gate_bwd_w", h2, dgl, 512)
    (dua, dar8, dai8, ddv, dbglu, dbdbr, dbdbi, dbdcr, dbdci, g_wglu) = _s5_bwd(dya, proj, ypre, xs, *s5c)
    dproj, dws, dbst, dlng, dlnb = _sgu_bwd("sgu_bwd", dyb, dua, proj, W["sgu_ln_g"], W["sgu_ln_b"],
                                            wsm_b, wsmt_b, bst)
    g_win = _col_bwd_w("w_in_bwd_w", h2, dproj, 384)
    g_wout3 = g_wout.reshape(NDEV, D_MODEL // NDEV, D_MODEL)
    g_wglu3 = g_wglu.astype(BF16).reshape(NDEV, S5_WIDTH // NDEV, S5_WIDTH)
    es2 = exchange_start("exchange2_start", [g_wout3, g_wba, g_wbb, g_wgate, g_wglu3, g_win])
    dh2 = _col_bwd_a("w_in_bwd_a", dproj, win, add=dh2g)
    dx1b, dgm = _rms_bwd("rms2_bwd", dh2, x1, W["mix_norm"] + es2[4][:1, :1], dx2b, BF16)

    dabr = jnp.sum(dar8, axis=0).reshape(S5_GROUPS, S5_STATE)
    dabi = jnp.sum(dai8, axis=0).reshape(S5_GROUPS, S5_STATE)
    d_lr, d_li, d_ldt, d_brt, d_bit = _s5_prep_bwd(lr_, li_, ldt_, brt, bit, dabr, dabi,
                                                   _bd_b_extract(dbdbr), _bd_b_extract(dbdbi))
    small_g = {
        "mix_norm": dgm, "ffn2_norm": dg3, "final_norm": dgf,
        "s5_a_re": d_lr, "s5_a_im": d_li, "s5_log_dt": d_ldt,
        "s5_d": ddv, "s5_b_glu": dbglu, "sgu_ln_g": dlng, "sgu_ln_b": dlnb,
        "sgu_b_s": dbst[:, :, 0], "b_gate": dbgate,
    }
    to_cgp = lambda a: jnp.transpose(a[0], (2, 0, 1))
    from_cgp = lambda a: jnp.transpose(a, (1, 2, 0))[None]
    natural = [
        ("s5_b_re", d_brt, to_cgp, from_cgp), ("s5_b_im", d_bit, to_cgp, from_cgp),
        ("s5_c_re", _bd_c_extract(dbdcr), lambda a: a[0], lambda a: a[None]),
        ("s5_c_im", -_bd_c_extract(dbdci), lambda a: a[0], lambda a: a[None]),
        ("sgu_w_s", jnp.where(mask[None], dws, 0.0), lambda a: a[0], lambda a: a[None]),
    ]
    sizes = [W[n].size for n in _TINY]
    total = sum(sizes) + 1
    rows = -(-total // 128)
    rows = -(-rows // 8) * 8
    pad = rows * 128 - total

    def pack(d, extra):
        return jnp.concatenate([d[n].reshape(-1).astype(F32) for n in _TINY] + [extra, jnp.zeros((pad,), F32)]
                               ).reshape(rows, 128)

    gsm = gather_start("gather_small_start", [pack(small_g, loss_p[0, :1])] + [g for _, g, _, _ in natural])

    dab1 = _ffn_down_bwd_act("ffn1_down_bwd_a", dx1b, wd1, ab1, after=gsm[4])
    g_gu1 = _ffn_up_bwd_w("ffn1_up_bwd_w", h1, dab1)
    es3 = exchange_start("exchange3_start", [g_gu1])
    g_wd1 = _ffn_down_bwd_w("ffn1_down_bwd_w", f1, dx1b, after=es3[4])
    es4 = exchange_start("exchange4_start", [g_wd1])
    dh1 = _ffn_up_bwd_h("ffn1_up_bwd_h", dab1, wgu1, es4[4])
    dx0, dg1 = _rms_bwd("rms1_bwd", dh1, x0, W["ffn1_norm"], dx1b, F32)

    G, Dl, Mn, Vn = {}, {}, {}, {}

    def adam(plan):
        last = None
        for n, recv, sub in plan:
            if sub is None:
                g, d, mn, vn = _adam_sharded("adam_" + n, recv, sub, W[n][0], M[n][0], V[n][0])
                G[n], Dl[n], Mn[n], Vn[n] = g[None], d[None], mn[None], vn[None]
            else:
                tr = jnp.transpose
                g, d, mn, vn = _adam_sharded("adam_" + n, recv, sub, tr(W[n][0]), tr(M[n][0]), tr(V[n][0]))
                G[n], Dl[n], Mn[n], Vn[n] = tr(g)[None], tr(d)[None], tr(mn)[None], tr(vn)[None]
            last = g
        return last

    r_wd2, r_gu2 = _split_wait("exchange1_wait", es1, dx0, False)
    done = adam([("ffn2_w_down", r_wd2, None), ("ffn2_w_gate", r_gu2, 0), ("ffn2_w_up", r_gu2, 1)])
    r_wout, r_wba, r_wbb, r_wgate, r_wglu, r_win = _split_wait("exchange2_wait", es2, done, False)
    done = adam([("w_out", r_wout, None), ("w_branch_a", r_wba, None), ("w_branch_b", r_wbb, None),
                 ("w_gate", r_wgate, None), ("s5_w_glu", r_wglu, None), ("w_in", r_win, None)])

    late = dg1 + 0.0 * done.reshape(-1)[:1]
    zero1 = jnp.zeros((1,), F32)
    parts = _split_wait("gather_small_wait", gsm, late, True)
    (parts_g1,) = _all_gather("gather_ffn1_norm_grad", [late])
    groups = [(parts[0], pack(W, zero1), pack(M, zero1), pack(V, zero1))]
    groups += [(parts[1 + a], view(W[n]), view(M[n]), view(V[n])) for a, (n, _, view, _) in enumerate(natural)]
    groups += [(parts_g1, W["ffn1_norm"], M["ffn1_norm"], V["ffn1_norm"])]
    res = _adam_small(groups)
    sg, sd, sm, sv = res[0]
    for (n, _, _, back), (g, d, mn, vn) in zip(natural, res[1:-1]):
        G[n], Dl[n], Mn[n], Vn[n] = back(g), back(d), back(mn), back(vn)
    G["ffn1_norm"], Dl["ffn1_norm"], Mn["ffn1_norm"], Vn["ffn1_norm"] = res[-1]

    def unpack(flat2d, into):
        flat = flat2d.reshape(-1)
        off = 0
        for n, s in zip(_TINY, sizes):
            into[n] = flat[off:off + s].reshape(W[n].shape)
            off += s
        return flat[off]

    loss = unpack(sg, G)
    unpack(sd, Dl)
    unpack(sm, Mn)
    unpack(sv, Vn)

    (r_gu1,) = _split_wait("exchange3_wait", es3, sg, False)
    done = adam([("ffn1_w_gate", r_gu1, 0), ("ffn1_w_up", r_gu1, 1)])
    (r_wd1,) = _split_wait("exchange4_wait", es4, done, False)
    adam([("ffn1_w_down", r_wd1, None)])

    return loss, dx0[None], G, Dl, Mn, Vn


def kernel(x, ffn1_norm, ffn1_w_gate, ffn1_w_up, ffn1_w_down, mix_norm, w_in, s5_a_re, s5_a_im, s5_log_dt, s5_b_re, s5_b_im, s5_c_re, s5_c_im, s5_d, s5_w_glu, s5_b_glu, sgu_ln_g, sgu_ln_b, sgu_w_s, sgu_b_s, w_branch_a, w_branch_b, w_gate, b_gate, w_out, ffn2_norm, ffn2_w_gate, ffn2_w_up, ffn2_w_down, final_norm, loss_target, m_ffn1_norm, m_ffn1_w_gate, m_ffn1_w_up, m_ffn1_w_down, m_mix_norm, m_w_in, m_s5_a_re, m_s5_a_im, m_s5_log_dt, m_s5_b_re, m_s5_b_im, m_s5_c_re, m_s5_c_im, m_s5_d, m_s5_w_glu, m_s5_b_glu, m_sgu_ln_g, m_sgu_ln_b, m_sgu_w_s, m_sgu_b_s, m_w_branch_a, m_w_branch_b, m_w_gate, m_b_gate, m_w_out, m_ffn2_norm, m_ffn2_w_gate, m_ffn2_w_up, m_ffn2_w_down, m_final_norm, v_ffn1_norm, v_ffn1_w_gate, v_ffn1_w_up, v_ffn1_w_down, v_mix_norm, v_w_in, v_s5_a_re, v_s5_a_im, v_s5_log_dt, v_s5_b_re, v_s5_b_im, v_s5_c_re, v_s5_c_im, v_s5_d, v_s5_w_glu, v_s5_b_glu, v_sgu_ln_g, v_sgu_ln_b, v_sgu_w_s, v_sgu_b_s, v_w_branch_a, v_w_branch_b, v_w_gate, v_b_gate, v_w_out, v_ffn2_norm, v_ffn2_w_gate, v_ffn2_w_up, v_ffn2_w_down, v_final_norm):
    a = locals()
    W = {n: a[n] for n in _ORDER}
    M = {n: a["m_" + n] for n in _ORDER}
    V = {n: a["v_" + n] for n in _ORDER}
    loss, gx, G, Dl, Mn, Vn = _step(x, loss_target, W, M, V)
    return (loss, gx, *[G[n] for n in _ORDER], *[Dl[n] for n in _ORDER], *[Mn[n] for n in _ORDER],
            *[Vn[n] for n in _ORDER])
```

```python
import functools
import math

import jax
import jax.numpy as jnp
from jax import lax
from jax.experimental import pallas as pl
from jax.experimental.pallas import tpu as pltpu

F32 = jnp.float32
BF16 = jnp.bfloat16
NDEV = 8
NORM_EPS = 1e-6
D_MODEL = 2048
D_FF = 5632
FF_SHARD = D_FF // NDEV
S5_WIDTH = 1024
S5_GROUPS = 64
S5_GROUP_WIDTH = 16
S5_STATE = 64
S5_NS = S5_GROUPS * S5_STATE
SGU_WIDTH = 1024
SGU_HEADS = 8
MLP_CHUNK = 128
CHUNK = 64
ADAM_LR, ADAM_B1, ADAM_B2, ADAM_EPS, ADAM_WD, ADAM_STEP = 0.001, 0.9, 0.999, 1e-08, 0.01, 10
S5_TC = 256
S5_SEG = S5_TC // 8
S5_LG = 512
S5_UNROLL = True
VMEM_BIG = 56 * 1024 * 1024

MESH = pl.DeviceIdType.MESH
SDS = jax.ShapeDtypeStruct
Block = pl.BlockSpec
ANY = pl.BlockSpec(memory_space=pl.ANY)


def _cparams(sem=None, vmem=None):
    return pltpu.CompilerParams(dimension_semantics=sem, vmem_limit_bytes=vmem)


def _const(shape):
    nd = len(shape)
    return pl.BlockSpec(shape, lambda i: (0,) * nd, pipeline_mode=pl.Buffered(1))


def _sigmoid(x):
    return 0.5 * jnp.tanh(0.5 * x) + 0.5


_GELU_C = math.sqrt(2.0 / math.pi)


def _gelu(x):
    return 0.5 * x * (1.0 + jnp.tanh(_GELU_C * (x + 0.044715 * x * x * x)))


def _gelu_grad(x):
    t = jnp.tanh(_GELU_C * (x + 0.044715 * x * x * x))
    return 0.5 * (1.0 + t) + 0.5 * x * (1.0 - t * t) * _GELU_C * (1.0 + 3.0 * 0.044715 * x * x)


NN = (((1,), (0,)), ((), ()))
NT = (((1,), (1,)), ((), ()))
TN = (((0,), (0,)), ((), ()))


def _dot(a, b, dims=NN):
    return lax.dot_general(a, b, dims, preferred_element_type=F32)


def _matmul(name, a, b, extras, *, grid, a_spec, b_spec, extra_specs, out_shapes, out_specs, acc_shape,
            epilogue, dims=NN, nb=None, compute=None, after=None, vmem=VMEM_BIG):
    nk = grid[2]
    if after is not None:
        extras = tuple(extras) + (after,)
        extra_specs = list(extra_specs) + [Block((8, 128), lambda i, j, k: (0, 0))]
    ne, no = len(extras), len(out_shapes)
    nacc = nb or 1
    if compute is None:
        def compute(a_ref, b_ref, q):
            return _dot(a_ref[...], b_ref[q] if nb else b_ref[...], dims)

    def body(*refs):
        a_ref, b_ref = refs[0], refs[1]
        ex = refs[2:2 + ne]
        outs = refs[2 + ne:2 + ne + no]
        if nk == 1:
            epilogue([compute(a_ref, b_ref, q) for q in range(nacc)], ex, outs)
            return
        acc_ref = refs[2 + ne + no]
        k = pl.program_id(2)

        @pl.when(k == 0)
        def _():
            acc_ref[...] = jnp.zeros_like(acc_ref)

        for q in range(nacc):
            acc_ref[q] += compute(a_ref, b_ref, q)

        @pl.when(k == nk - 1)
        def _():
            epilogue([acc_ref[q] for q in range(nacc)], ex, outs)

    scratch = [] if nk == 1 else [pltpu.VMEM((nacc,) + tuple(acc_shape), F32)]
    res = pl.pallas_call(
        body, name=name, grid=grid,
        in_specs=[a_spec, b_spec] + list(extra_specs),
        out_specs=list(out_specs), out_shape=list(out_shapes), scratch_shapes=scratch,
        compiler_params=_cparams(("parallel", "parallel", "arbitrary"), vmem),
    )(a, b, *extras)
    return res


def _store(dtype_outs=None):
    def ep(accs, ex, outs):
        outs[0][...] = accs[0].astype(outs[0].dtype)
    return ep


def _tile(n, t):
    t = min(n, t)
    assert n % t == 0, (n, t)
    return t


def _ksum(kq, dims):
    def compute(a_ref, b_ref, _):
        part = _dot(a_ref[0], b_ref[0], dims)
        for q in range(1, kq):
            part = part + _dot(a_ref[q], b_ref[q], dims)
        return part
    return compute


def _ksum_lanes(kq, ns, dims):
    def compute(a_ref, b_ref, _):
        part = _dot(a_ref[:, 0:ns], b_ref[0], dims)
        for q in range(1, kq):
            part = part + _dot(a_ref[:, q * ns:(q + 1) * ns], b_ref[q], dims)
        return part
    return compute


def _wide_b(g):
    def compute(a_ref, b_ref, _):
        bw = b_ref[0] if g == 1 else jnp.concatenate([b_ref[q] for q in range(g)], axis=1)
        return _dot(a_ref[...], bw, NN)
    return compute


TT_DEEP = 2048
TT_FFN = 4096


HIDDEN = NDEV * FF_SHARD


def _ffn_up(name, h, wgu, after=None):
    T, D = h.shape
    tm = _tile(T, 1024)

    def ep(accs, ex, outs):
        a, b = accs
        outs[0][0] = a.astype(BF16)
        outs[0][1] = b.astype(BF16)
        outs[1][...] = (a * _sigmoid(a) * b).astype(BF16)

    return _matmul(
        name, wgu, h, (), after=after, grid=(NDEV, T // tm, 1),
        a_spec=Block((None, 2, FF_SHARD, D), lambda j, i, k: (j, 0, 0, 0)),
        b_spec=Block((tm, D), lambda j, i, k: (i, 0)),
        extra_specs=(),
        out_shapes=[SDS((NDEV, 2, FF_SHARD, T), BF16), SDS((NDEV, FF_SHARD, T), BF16)],
        out_specs=[Block((None, 2, FF_SHARD, tm), lambda j, i, k: (j, 0, 0, i)),
                   Block((None, FF_SHARD, tm), lambda j, i, k: (j, 0, i))],
        acc_shape=(FF_SHARD, tm), nb=2,
        compute=lambda a_ref, b_ref, q: _dot(a_ref[q], b_ref[...], NT), epilogue=ep)


def _ffn_down(name, f, wd, xres, after=None):
    T = f.shape[2]
    tm, tn, tk = _tile(T, 1024), 1024, HIDDEN // 2

    def ep(accs, ex, outs):
        outs[0][...] = ex[0][...] + 0.5 * accs[0]

    return _matmul(
        name, f.reshape(HIDDEN, T), wd.reshape(HIDDEN, D_MODEL), (xres,), after=after,
        grid=(T // tm, D_MODEL // tn, HIDDEN // tk),
        a_spec=Block((tk, tm), lambda i, j, k: (k, i)),
        b_spec=Block((tk, tn), lambda i, j, k: (k, j)),
        extra_specs=[Block((tm, tn), lambda i, j, k: (i, j))],
        out_shapes=[SDS((T, D_MODEL), F32)],
        out_specs=[Block((tm, tn), lambda i, j, k: (i, j))],
        acc_shape=(tm, tn), dims=TN, epilogue=ep)[0]


def _ffn_down_bwd_act(name, dyb, wd, ab, after=None):
    T, D = dyb.shape
    tm, g = _tile(T, 1024), 1

    def ep(accs, ex, outs):
        for s in range(g):
            df = accs[0][s * FF_SHARD:(s + 1) * FF_SHARD, :]
            a = ex[0][s, 0].astype(F32)
            b = ex[0][s, 1].astype(F32)
            hs = 0.5 * _sigmoid(a)
            outs[0][s, 0] = (df * b * hs * (1.0 + a * (1.0 - 2.0 * hs))).astype(BF16)
            outs[0][s, 1] = (df * a * hs).astype(BF16)

    blk = Block((g, 2, FF_SHARD, tm), lambda i, j, k: (j, 0, 0, i))
    return _matmul(
        name, wd.reshape(HIDDEN, D), dyb, (ab,), after=after, grid=(T // tm, NDEV // g, 1),
        a_spec=Block((g * FF_SHARD, D), lambda i, j, k: (j, 0)),
        b_spec=Block((tm, D), lambda i, j, k: (i, 0)),
        extra_specs=[blk],
        out_shapes=[SDS((NDEV, 2, FF_SHARD, T), BF16)],
        out_specs=[blk],
        acc_shape=(g * FF_SHARD, tm), dims=NT, epilogue=ep)[0]


def _ffn_down_bwd_w(name, f, dyb, after=None):
    T = f.shape[2]
    tt, tn, tr = _tile(T, TT_DEEP), 1024, 2 * FF_SHARD

    def ep(accs, ex, outs):
        outs[0][...] = (0.5 * accs[0]).astype(BF16)

    return _matmul(
        name, f.reshape(HIDDEN, T), dyb, (), after=after, grid=(HIDDEN // tr, D_MODEL // tn, T // tt),
        a_spec=Block((tr, tt), lambda j, n, k: (j, k)),
        b_spec=Block((tt, tn), lambda j, n, k: (k, n)),
        extra_specs=(),
        out_shapes=[SDS((HIDDEN, D_MODEL), BF16)],
        out_specs=[Block((tr, tn), lambda j, n, k: (j, n))],
        acc_shape=(tr, tn), dims=NN, epilogue=ep)[0].reshape(NDEV, FF_SHARD, D_MODEL)


def _ffn_up_bwd_h(name, dab, wgu, after):
    T = dab.shape[3]
    tm = _tile(T, 1024)
    return _matmul(
        name, dab, wgu, (), after=after, grid=(T // tm, 1, NDEV),
        a_spec=Block((None, 2, FF_SHARD, tm), lambda i, j, k: (k, 0, 0, i)),
        b_spec=Block((None, 2, FF_SHARD, D_MODEL), lambda i, j, k: (k, 0, 0, 0)),
        extra_specs=(),
        out_shapes=[SDS((T, D_MODEL), BF16)],
        out_specs=[Block((tm, D_MODEL), lambda i, j, k: (i, 0))],
        acc_shape=(tm, D_MODEL), compute=_ksum(2, TN), epilogue=_store())[0]


def _ffn_up_bwd_w(name, h, dab):
    T, D = h.shape
    tt, tn = _tile(T, TT_FFN), 1024

    def ep(accs, ex, outs):
        outs[0][0] = accs[0].astype(BF16)
        outs[0][1] = accs[1].astype(BF16)

    return _matmul(
        name, dab, h, (), grid=(NDEV, D // tn, T // tt),
        a_spec=Block((None, 2, FF_SHARD, tt), lambda j, n, k: (j, 0, 0, k)),
        b_spec=Block((tt, tn), lambda j, n, k: (k, n)),
        extra_specs=(),
        out_shapes=[SDS((NDEV, 2, FF_SHARD, D), BF16)],
        out_specs=[Block((None, 2, FF_SHARD, tn), lambda j, n, k: (j, 0, 0, n))],
        acc_shape=(FF_SHARD, tn), nb=2,
        compute=lambda a_ref, b_ref, q: _dot(a_ref[q], b_ref[...], NN), epilogue=ep)[0]


def _shards_per_step(ns):
    return max(g for g in (1, 2, 4, 8) if g * ns <= 2048)


def _split_lanes(g, ns):
    def ep(accs, ex, outs):
        for q in range(g):
            outs[0][q] = accs[0][:, q * ns:(q + 1) * ns].astype(outs[0].dtype)
    return ep


def _col_fwd(name, a, w, out_dtype=BF16, after=None):
    T, K = a.shape
    ns = w.shape[2]
    g = _shards_per_step(ns)
    tm = _tile(T, 1024)
    return _matmul(
        name, a, w, (), after=after, grid=(NDEV // g, T // tm, 1),
        a_spec=Block((tm, K), lambda j, i, k: (i, 0)),
        b_spec=Block((g, K, ns), lambda j, i, k: (j, 0, 0)),
        extra_specs=(),
        out_shapes=[SDS((T, NDEV * ns), out_dtype)],
        out_specs=[Block((tm, g * ns), lambda j, i, k: (i, j))],
        acc_shape=(tm, g * ns), compute=_wide_b(g), epilogue=_store())[0]


def _col_bwd_a(name, dy, w, add=None):
    T = dy.shape[0]
    _, K, ns = w.shape
    tm, tn = _tile(T, 1024), _tile(K, 1024)

    def ep(accs, ex, outs):
        r = accs[0]
        if add is not None:
            r = r + ex[0][...].astype(F32)
        outs[0][...] = r.astype(BF16)

    extras = () if add is None else (add,)
    return _matmul(
        name, dy, w, extras, grid=(T // tm, K // tn, 1),
        a_spec=Block((tm, NDEV * ns), lambda i, j, k: (i, 0)),
        b_spec=Block((NDEV, tn, ns), lambda i, j, k: (0, j, 0)),
        extra_specs=[Block((tm, tn), lambda i, j, k: (i, j))] * len(extras),
        out_shapes=[SDS((T, K), BF16)],
        out_specs=[Block((tm, tn), lambda i, j, k: (i, j))],
        acc_shape=(tm, tn), compute=_ksum_lanes(NDEV, ns, NT), epilogue=ep)[0]


def _col_bwd_w(name, a, dy, ns):
    T, K = a.shape
    g = _shards_per_step(ns)
    tt, tr = _tile(T, TT_DEEP), _tile(K, 1024)
    return _matmul(
        name, a, dy, (), grid=(NDEV // g, K // tr, T // tt),
        a_spec=Block((tt, tr), lambda j, n, k: (k, n)),
        b_spec=Block((tt, g * ns), lambda j, n, k: (k, j)),
        extra_specs=(),
        out_shapes=[SDS((NDEV, K, ns), BF16)],
        out_specs=[Block((g, tr, ns), lambda j, n, k: (j, n, 0))],
        acc_shape=(tr, g * ns), dims=TN, epilogue=_split_lanes(g, ns))[0]


def _gate_fwd(name, h, w, bias):
    T, K = h.shape
    ns = w.shape[2]
    g = 2
    per = D_MODEL // (g * ns)
    tm = _tile(T, 1024)

    def ep(accs, ex, outs):
        outs[0][...] = (accs[0] + ex[0][...]).astype(BF16)

    return _matmul(
        name, h, w, (bias,), grid=(NDEV // g, T // tm, 1),
        a_spec=Block((tm, K), lambda j, i, k: (i, 0)),
        b_spec=Block((g, K, ns), lambda j, i, k: (j, 0, 0)),
        extra_specs=[Block((None, 1, g * ns), lambda j, i, k: (j // per, 0, j % per))],
        out_shapes=[SDS((2, T, D_MODEL), BF16)],
        out_specs=[Block((None, tm, g * ns), lambda j, i, k: (j // per, i, j % per))],
        acc_shape=(tm, g * ns), compute=_wide_b(g), epilogue=ep)[0]


def _gate_bwd_a(name, dgl, w):
    _, T, _ = dgl.shape
    _, K, ns = w.shape
    per = D_MODEL // ns
    tm, tn = _tile(T, 1024), 1024

    def compute(a_ref, b_ref, _):
        part = None
        for q in range(NDEV):
            d = _dot(a_ref[q // per, :, (q % per) * ns:(q % per + 1) * ns], b_ref[q], NT)
            part = d if part is None else part + d
        return part

    return _matmul(
        name, dgl, w, (), grid=(T // tm, K // tn, 1),
        a_spec=Block((2, tm, D_MODEL), lambda i, j, k: (0, i, 0)),
        b_spec=Block((NDEV, tn, ns), lambda i, j, k: (0, j, 0)),
        extra_specs=(),
        out_shapes=[SDS((T, K), BF16)],
        out_specs=[Block((tm, tn), lambda i, j, k: (i, j))],
        acc_shape=(tm, tn), compute=compute, epilogue=_store())[0]


def _gate_bwd_w(name, h, dgl, ns):
    T, K = h.shape
    g = 2
    per = D_MODEL // (g * ns)
    tt, tr = _tile(T, TT_DEEP), 1024
    return _matmul(
        name, h, dgl, (), grid=(NDEV // g, K // tr, T // tt),
        a_spec=Block((tt, tr), lambda j, n, k: (k, n)),
        b_spec=Block((None, tt, g * ns), lambda j, n, k: (j // per, k, j % per)),
        extra_specs=(),
        out_shapes=[SDS((NDEV, K, ns), BF16)],
        out_specs=[Block((g, tr, ns), lambda j, n, k: (j, n, 0))],
        acc_shape=(tr, g * ns), dims=TN, epilogue=_split_lanes(g, ns))[0]


def _plain_fwd_res(name, a, w, xres):
    T, K = a.shape
    N = w.shape[1]
    tm, tn = _tile(T, 1024), _tile(N, 1024)

    def ep(accs, ex, outs):
        outs[0][...] = ex[0][...] + accs[0]

    return _matmul(
        name, a, w, (xres,), grid=(T // tm, N // tn, 1),
        a_spec=Block((tm, K), lambda i, j, k: (i, 0)),
        b_spec=Block((K, tn), lambda i, j, k: (0, j)),
        extra_specs=[Block((tm, tn), lambda i, j, k: (i, j))],
        out_shapes=[SDS((T, N), F32)],
        out_specs=[Block((tm, tn), lambda i, j, k: (i, j))],
        acc_shape=(tm, tn), dims=NN, nb=None, epilogue=ep)[0]


def _plain_bwd_a(name, dy, w):
    T, N = dy.shape
    K = w.shape[0]
    tm, tn = _tile(T, 1024), _tile(K, 1024)
    return _matmul(
        name, dy, w, (), grid=(T // tm, K // tn, 1),
        a_spec=Block((tm, N), lambda i, j, k: (i, 0)),
        b_spec=Block((tn, N), lambda i, j, k: (j, 0)),
        extra_specs=(),
        out_shapes=[SDS((T, K), BF16)],
        out_specs=[Block((tm, tn), lambda i, j, k: (i, j))],
        acc_shape=(tm, tn), dims=NT, nb=None, epilogue=_store())[0]


def _plain_bwd_w(name, a, dy):
    T, K = a.shape
    N = dy.shape[1]
    tt, tr, tn = _tile(T, TT_DEEP), _tile(K, 1024), _tile(N, 1024)
    return _matmul(
        name, a, dy, (), grid=(K // tr, N // tn, T // tt),
        a_spec=Block((tt, tr), lambda m, n, k: (k, m)),
        b_spec=Block((tt, tn), lambda m, n, k: (k, n)),
        extra_specs=(),
        out_shapes=[SDS((K, N), BF16)],
        out_specs=[Block((tr, tn), lambda m, n, k: (m, n))],
        acc_shape=(tr, tn), dims=TN, nb=None, epilogue=_store())[0]


def _rms_fwd(name, x, g):
    T, D = x.shape
    tm = _tile(T, 512)

    def body(x_ref, g_ref, h_ref):
        xv = x_ref[...]
        r = lax.rsqrt(jnp.mean(xv * xv, axis=-1, keepdims=True) + NORM_EPS)
        h_ref[...] = (xv * r * g_ref[...]).astype(BF16)

    return pl.pallas_call(
        body, name=name, grid=(T // tm,),
        in_specs=[Block((tm, D), lambda i: (i, 0)), Block((1, D), lambda i: (0, 0))],
        out_specs=Block((tm, D), lambda i: (i, 0)), out_shape=SDS((T, D), BF16),
        compiler_params=_cparams(("arbitrary",), VMEM_BIG))(x, g)


def _rms_bwd(name, dh, x, g, dxin, out_dtype):
    T, D = x.shape
    tm = _tile(T, 512)

    def body(dh_ref, x_ref, g_ref, dxin_ref, dx_ref, dg_ref):
        i = pl.program_id(0)
        xv = x_ref[...]
        dh = dh_ref[...].astype(F32)
        r = lax.rsqrt(jnp.mean(xv * xv, axis=-1, keepdims=True) + NORM_EPS)
        xh = xv * r
        gd = dh * g_ref[...]
        dx = dxin_ref[...].astype(F32) + r * (gd - xh * jnp.mean(gd * xh, axis=-1, keepdims=True))
        dx_ref[...] = dx.astype(out_dtype)
        dgp = jnp.sum(dh * xh, axis=0, keepdims=True)

        @pl.when(i == 0)
        def _():
            dg_ref[...] = dgp

        @pl.when(i > 0)
        def _():
            dg_ref[...] += dgp

    row = Block((tm, D), lambda i: (i, 0))
    vec = Block((1, D), lambda i: (0, 0))
    return pl.pallas_call(
        body, name=name, grid=(T // tm,),
        in_specs=[row, row, vec, row], out_specs=[row, vec],
        out_shape=[SDS((T, D), out_dtype), SDS((1, D), F32)],
        compiler_params=_cparams(("arbitrary",), VMEM_BIG))(dh, x, g, dxin)


def _loss_head(name, x, g, tgt):
    T, D = x.shape
    tm = _tile(T, 512)

    def body(x_ref, g_ref, t_ref, loss_ref, dxb_ref, dg_ref):
        i = pl.program_id(0)
        xv = x_ref[...]
        gv = g_ref[...]
        r = lax.rsqrt(jnp.mean(xv * xv, axis=-1, keepdims=True) + NORM_EPS)
        xh = xv * r
        err = xh * gv - t_ref[...]
        lp = 0.5 * jnp.sum(jnp.mean(err * err, axis=-1, keepdims=True), axis=0, keepdims=True)
        dout = err * (1.0 / D)
        gd = dout * gv
        dx = r * (gd - xh * jnp.mean(gd * xh, axis=-1, keepdims=True))
        dxb_ref[...] = dx.astype(BF16)
        dgp = jnp.sum(dout * xh, axis=0, keepdims=True)
        lpb = jnp.broadcast_to(lp, (1, 128))

        @pl.when(i == 0)
        def _():
            dg_ref[...] = dgp
            loss_ref[...] = lpb

        @pl.when(i > 0)
        def _():
            dg_ref[...] += dgp
            loss_ref[...] += lpb

    row = Block((tm, D), lambda i: (i, 0))
    vec = Block((1, D), lambda i: (0, 0))
    return pl.pallas_call(
        body, name=name, grid=(T // tm,),
        in_specs=[row, vec, row], out_specs=[Block((1, 128), lambda i: (0, 0)), row, vec],
        out_shape=[SDS((1, 128), F32), SDS((T, D), BF16), SDS((1, D), F32)],
        compiler_params=_cparams(("arbitrary",), VMEM_BIG))(x, g, tgt)


def _merge_fwd(name, pa, pb, gl):
    T, D = pa.shape
    tm = _tile(T, 512)

    def body(pa_ref, pb_ref, gl_ref, o_ref):
        ga = _sigmoid(gl_ref[0].astype(F32))
        gb = _sigmoid(gl_ref[1].astype(F32))
        o_ref[...] = (ga * pa_ref[...].astype(F32) + gb * pb_ref[...].astype(F32)).astype(BF16)

    row = Block((tm, D), lambda i: (i, 0))
    return pl.pallas_call(
        body, name=name, grid=(T // tm,),
        in_specs=[row, row, Block((2, tm, D), lambda i: (0, i, 0))], out_specs=row,
        out_shape=SDS((T, D), BF16), compiler_params=_cparams(("arbitrary",), VMEM_BIG))(pa, pb, gl)


def _merge_bwd(name, dm, pa, pb, gl):
    T, D = pa.shape
    tm = _tile(T, 512)

    def body(dm_ref, pa_ref, pb_ref, gl_ref, dpa_ref, dpb_ref, dgl_ref, db_ref):
        i = pl.program_id(0)
        dmv = dm_ref[...].astype(F32)
        ga = _sigmoid(gl_ref[0].astype(F32))
        gb = _sigmoid(gl_ref[1].astype(F32))
        dpa_ref[...] = (dmv * ga).astype(BF16)
        dpb_ref[...] = (dmv * gb).astype(BF16)
        dga = dmv * pa_ref[...].astype(F32) * ga * (1.0 - ga)
        dgb = dmv * pb_ref[...].astype(F32) * gb * (1.0 - gb)
        dgl_ref[0] = dga.astype(BF16)
        dgl_ref[1] = dgb.astype(BF16)
        sa = jnp.sum(dga, axis=0, keepdims=True)
        sb = jnp.sum(dgb, axis=0, keepdims=True)

        @pl.when(i == 0)
        def _():
            db_ref[0] = sa
            db_ref[1] = sb

        @pl.when(i > 0)
        def _():
            db_ref[0] += sa
            db_ref[1] += sb

    row = Block((tm, D), lambda i: (i, 0))
    two = Block((2, tm, D), lambda i: (0, i, 0))
    return pl.pallas_call(
        body, name=name, grid=(T // tm,),
        in_specs=[row, row, row, two], out_specs=[row, row, two, Block((2, 1, D), lambda i: (0, 0, 0))],
        out_shape=[SDS((T, D), BF16), SDS((T, D), BF16), SDS((2, T, D), BF16), SDS((2, 1, D), F32)],
        compiler_params=_cparams(("arbitrary",), VMEM_BIG))(dm, pa, pb, gl)


def _sgu_core(ur, vr, lng, lnb, ws_ref, bs_ref):
    tm = ur.shape[0]
    gu = _gelu(ur)
    gv = _gelu(vr)
    mu = jnp.mean(gv, axis=-1, keepdims=True)
    cen = gv - mu
    rstd = lax.rsqrt(jnp.mean(cen * cen, axis=-1, keepdims=True) + NORM_EPS)
    xhat = cen * rstd
    vn = (xhat * lng + lnb).astype(BF16)
    rows = []
    for n in range(tm // MLP_CHUNK):
        cols = []
        for h in range(SGU_HEADS):
            blk = vn[n * MLP_CHUNK:(n + 1) * MLP_CHUNK, h * 128:(h + 1) * 128]
            cols.append(_dot(ws_ref[h], blk) + bs_ref[h])
        rows.append(jnp.concatenate(cols, axis=1))
    mixed = jnp.concatenate(rows, axis=0) if len(rows) > 1 else rows[0]
    return gu, xhat, rstd, vn, mixed


def _sgu_fwd(name, proj, lng, lnb, wsm, bst):
    T = proj.shape[0]
    W = SGU_WIDTH
    tm = _tile(T, 512)

    def body(u_ref, v_ref, lng_ref, lnb_ref, ws_ref, bs_ref, o_ref):
        gu, _, _, _, mixed = _sgu_core(u_ref[...].astype(F32), v_ref[...].astype(F32), lng_ref[...], lnb_ref[...],
                                       ws_ref, bs_ref)
        o_ref[...] = (gu * mixed).astype(BF16)

    vec = Block((1, W), lambda i: (0, 0))
    return pl.pallas_call(
        body, name=name, grid=(T // tm,),
        in_specs=[Block((tm, W), lambda i: (i, 1)), Block((tm, W), lambda i: (i, 2)), vec, vec,
                  Block((SGU_HEADS, 128, 128), lambda i: (0, 0, 0)), Block((SGU_HEADS, 128, 128), lambda i: (0, 0, 0))],
        out_specs=Block((tm, W), lambda i: (i, 0)), out_shape=SDS((T, W), BF16),
        compiler_params=_cparams(("arbitrary",), VMEM_BIG))(proj, proj, lng, lnb, wsm, bst)


def _sgu_bwd(name, dyb, dua, proj, lng, lnb, wsm, wsmt, bst):
    T = proj.shape[0]
    W = SGU_WIDTH
    tm = _tile(T, 512)

    def body(dy_ref, dua_ref, u_ref, v_ref, lng_ref, lnb_ref, ws_ref, wst_ref, bs_ref,
             duv_ref, dws_ref, dbs_ref, dlng_ref, dlnb_ref):
        i = pl.program_id(0)
        duv_ref[:, :W] = dua_ref[...]
        ur = u_ref[...].astype(F32)
        vr = v_ref[...].astype(F32)
        lng_v = lng_ref[...]
        gu, xhat, rstd, vn, mixed = _sgu_core(ur, vr, lng_v, lnb_ref[...], ws_ref, bs_ref)
        dy = dy_ref[...].astype(F32)
        dgu = dy * mixed
        dmix = dy * gu
        dmb = dmix.astype(BF16)
        dws_p, dbs_p, rows = [], [], []
        for h in range(SGU_HEADS):
            acc_w = jnp.zeros((128, 128), F32)
            acc_b = jnp.zeros((128, 1), F32)
            for n in range(tm // MLP_CHUNK):
                r0 = n * MLP_CHUNK
                dmt = dmb[r0:r0 + MLP_CHUNK, h * 128:(h + 1) * 128]
                acc_w = acc_w + _dot(dmt, vn[r0:r0 + MLP_CHUNK, h * 128:(h + 1) * 128], NT)
                acc_b = acc_b + jnp.sum(dmix[r0:r0 + MLP_CHUNK, h * 128:(h + 1) * 128], axis=1, keepdims=True)
            dws_p.append(acc_w)
            dbs_p.append(jnp.broadcast_to(acc_b, (128, 128)))
        for n in range(tm // MLP_CHUNK):
            r0 = n * MLP_CHUNK
            rows.append(jnp.concatenate(
                [_dot(wst_ref[h], dmb[r0:r0 + MLP_CHUNK, h * 128:(h + 1) * 128]) for h in range(SGU_HEADS)], axis=1))
        dvn = jnp.concatenate(rows, axis=0) if len(rows) > 1 else rows[0]
        dlng_p = jnp.sum(dvn * xhat, axis=0, keepdims=True)
        dlnb_p = jnp.sum(dvn, axis=0, keepdims=True)
        dxh = dvn * lng_v
        dgv = rstd * (dxh - jnp.mean(dxh, axis=-1, keepdims=True)
                      - xhat * jnp.mean(dxh * xhat, axis=-1, keepdims=True))
        duv_ref[:, W:2 * W] = (dgu * _gelu_grad(ur)).astype(BF16)
        duv_ref[:, 2 * W:] = (dgv * _gelu_grad(vr)).astype(BF16)

        @pl.when(i == 0)
        def _():
            for h in range(SGU_HEADS):
                dws_ref[h] = dws_p[h]
                dbs_ref[h] = dbs_p[h]
            dlng_ref[...] = dlng_p
            dlnb_ref[...] = dlnb_p

        @pl.when(i > 0)
        def _():
            for h in range(SGU_HEADS):
                dws_ref[h] += dws_p[h]
                dbs_ref[h] += dbs_p[h]
            dlng_ref[...] += dlng_p
            dlnb_ref[...] += dlnb_p

    vec = Block((1, W), lambda i: (0, 0))
    wsb = Block((SGU_HEADS, 128, 128), lambda i: (0, 0, 0))
    hsq = SDS((SGU_HEADS, 128, 128), F32)
    return pl.pallas_call(
        body, name=name, grid=(T // tm,),
        in_specs=[Block((tm, W), lambda i: (i, 0)), Block((tm, W), lambda i: (i, 0)),
                  Block((tm, W), lambda i: (i, 1)), Block((tm, W), lambda i: (i, 2)),
                  vec, vec, wsb, wsb, wsb],
        out_specs=[Block((tm, 3 * W), lambda i: (i, 0)), wsb, wsb, vec, vec],
        out_shape=[SDS((T, 3 * W), BF16), hsq, hsq, SDS((1, W), F32), SDS((1, W), F32)],
        compiler_params=_cparams(("arbitrary",), VMEM_BIG))(dyb, dua, proj, proj, lng, lnb, wsm, wsmt, bst)


def _s5_disc(lr, li, ldt, brt, bit):
    dt = jnp.exp(ldt)
    decay = jnp.exp(lr * dt)
    abr = decay * jnp.cos(li * dt)
    abi = decay * jnp.sin(li * dt)
    denom = lr * lr + li * li
    nr = abr - 1.0
    ni = abi
    kr = (nr * lr + ni * li) / denom
    ki = (ni * lr - nr * li) / denom
    bkr = kr[None] * brt - ki[None] * bit
    bki = kr[None] * bit + ki[None] * brt
    return abr, abi, bkr, bki


def _s5_prep(lr, li, ldt, brt, bit):
    G, P, C = S5_GROUPS, S5_STATE, S5_GROUP_WIDTH

    def body(lr_ref, li_ref, ldt_ref, br_ref, bi_ref, abr_ref, abi_ref, pwr_ref, pwi_ref, bkr_ref, bki_ref):
        lr_, li_, ldt_ = lr_ref[...], li_ref[...], ldt_ref[...]
        res = _s5_disc(lr_, li_, ldt_, br_ref[...], bi_ref[...])
        for o, r in zip((abr_ref, abi_ref, bkr_ref, bki_ref), res):
            o[...] = r
        dt = jnp.exp(ldt_)
        n = lax.broadcasted_iota(jnp.int32, (S5_SEG, G, P), 0).astype(F32) + 1.0
        dec = jnp.exp((lr_ * dt)[None] * n)
        ang = (li_ * dt)[None] * n
        pwr_ref[...] = dec * jnp.cos(ang)
        pwi_ref[...] = dec * jnp.sin(ang)

    gp = SDS((G, P), F32)
    sgp = SDS((S5_SEG, G, P), F32)
    cgp = SDS((C, G, P), F32)
    return pl.pallas_call(body, name="s5_prep", out_shape=[gp, gp, sgp, sgp, cgp, cgp])(lr, li, ldt, brt, bit)


def _s5_prep_bwd(lr, li, ldt, brt, bit, dabr, dabi, dbkr, dbki):
    G, P, C = S5_GROUPS, S5_STATE, S5_GROUP_WIDTH

    def body(lr_ref, li_ref, ldt_ref, br_ref, bi_ref, dabr_ref, dabi_ref, dbkr_ref, dbki_ref,
             o_lr, o_li, o_ldt, o_br, o_bi):
        _, pull = jax.vjp(_s5_disc, lr_ref[...], li_ref[...], ldt_ref[...], br_ref[...], bi_ref[...])
        g = pull((dabr_ref[...], dabi_ref[...], dbkr_ref[...], dbki_ref[...]))
        for o, r in zip((o_lr, o_li, o_ldt, o_br, o_bi), g):
            o[...] = r

    gp = SDS((G, P), F32)
    cgp = SDS((C, G, P), F32)
    return pl.pallas_call(body, name="s5_prep_bwd", out_shape=[gp, gp, SDS((G, 1), F32), cgp, cgp])(
        lr, li, ldt, brt, bit, dabr, dabi, dbkr, dbki)


def _s5_scan(buf_ref, ar_row, ai_row, pwr_ref, pwi_ref, carry_ref, LG, xs_ref=None, dar_ref=None, dai_ref=None):
    reverse = xs_ref is not None
    NS, SEG = S5_NS, S5_SEG
    sgn = -1.0 if reverse else 1.0
    for lg in range(NS // LG):
        cr = slice(lg * LG, (lg + 1) * LG)
        ci = slice(NS + lg * LG, NS + (lg + 1) * LG)
        ar1, ai1 = ar_row[:, cr], sgn * ai_row[:, cr]
        asr1, asi1 = pwr_ref[SEG - 1:SEG, cr], sgn * pwi_ref[SEG - 1:SEG, cr]
        ar = jnp.broadcast_to(ar1, (8, LG))
        ai = jnp.broadcast_to(ai1, (8, LG))

        def step_of(j):
            return (SEG - 1 - j) if reverse else j

        def p1(j, st):
            sr, si = st
            rows = pl.ds(pl.multiple_of(step_of(j) * 8, 8), 8)
            nr = ar * sr - ai * si + buf_ref[rows, cr]
            ni = ar * si + ai * sr + buf_ref[rows, ci]
            buf_ref[rows, cr] = nr
            buf_ref[rows, ci] = ni
            return nr, ni

        z = jnp.zeros((8, LG), F32)
        er, ei = lax.fori_loop(0, SEG, p1, (z, z), unroll=S5_UNROLL)
        c_r = carry_ref[:, cr]
        c_i = carry_ref[:, ci]
        cs_r, cs_i = [None] * 8, [None] * 8
        order = range(7, -1, -1) if reverse else range(8)
        for s in order:
            cs_r[s], cs_i[s] = c_r, c_i
            e_r, e_i = er[s:s + 1], ei[s:s + 1]
            c_r, c_i = e_r + asr1 * c_r - asi1 * c_i, e_i + asr1 * c_i + asi1 * c_r
        carry_ref[:, cr] = c_r
        carry_ref[:, ci] = c_i
        cmr = jnp.concatenate(cs_r, axis=0)
        cmi = jnp.concatenate(cs_i, axis=0)

        def carried(j):
            pr = pwr_ref[pl.ds(j, 1), cr]
            pi = sgn * pwi_ref[pl.ds(j, 1), cr]
            return pr * cmr - pi * cmi, pr * cmi + pi * cmr

        if not reverse:
            def p2(j, st):
                rows = pl.ds(pl.multiple_of(j * 8, 8), 8)
                wr, wi = carried(j)
                buf_ref[rows, cr] += wr
                buf_ref[rows, ci] += wi
                return st

            lax.fori_loop(0, SEG, p2, 0, unroll=S5_UNROLL)
        else:
            def p2(j, st):
                pr, pi, dr, di = st
                rows = pl.ds(pl.multiple_of(step_of(j) * 8, 8), 8)
                xr = xs_ref[rows, cr]
                xi = xs_ref[rows, ci]
                dr = dr + pr * xr + pi * xi
                di = di + pi * xr - pr * xi
                wr, wi = carried(j)
                gr = buf_ref[rows, cr] + wr
                gi = buf_ref[rows, ci] + wi
                buf_ref[rows, cr] = gr
                buf_ref[rows, ci] = gi
                return gr, gi, dr, di

            st = lax.fori_loop(0, SEG, p2, (cmr, cmi, z, z), unroll=S5_UNROLL)
            dar_ref[:, cr] += st[2]
            dai_ref[:, cr] += st[3]


def _s5_fwd(proj, perm, permt, bdbr, bdbi, bdcr, bdci, abr, abi, asr, asi, dvec, wglu, bglu):
    T = proj.shape[0]
    TC, NS, W = S5_TC, S5_NS, S5_WIDTH
    nc = T // TC

    def body(u_ref, pm_ref, pmt_ref, bdbr_ref, bdbi_ref, bdcr_ref, bdci_ref, ar_ref, ai_ref, asr_ref, asi_ref,
             d_ref, wglu_ref, bglu_ref, ya_ref, xs_ref, ypre_ref, carry_ref):
        i = pl.program_id(0)

        @pl.when(i == 0)
        def _():
            carry_ref[...] = jnp.zeros_like(carry_ref)

        up = _dot(pm_ref[...], u_ref[...]).astype(BF16)
        for j in range(8):
            ut = up[:, j * 128:(j + 1) * 128]
            xs_ref[:, j * 512:(j + 1) * 512] = _dot(ut, bdbr_ref[j])
            xs_ref[:, NS + j * 512:NS + (j + 1) * 512] = _dot(ut, bdbi_ref[j])
        _s5_scan(xs_ref, ar_ref[...], ai_ref[...], asr_ref, asi_ref, carry_ref, S5_LG)
        ys = []
        for j in range(8):
            xr = xs_ref[:, j * 512:(j + 1) * 512].astype(BF16)
            xi = xs_ref[:, NS + j * 512:NS + (j + 1) * 512].astype(BF16)
            ys.append(_dot(xr, bdcr_ref[j]) + _dot(xi, bdci_ref[j]))
        ypre = jnp.concatenate(ys, axis=1) + d_ref[...] * up.astype(F32)
        ypre_ref[...] = ypre
        ya = _gelu(ypre)
        zl = _dot(ya.astype(BF16), wglu_ref[...]) + bglu_ref[...]
        outp = (ya * _sigmoid(zl)).astype(BF16)
        ya_ref[...] = _dot(pmt_ref[...], outp).astype(BF16)

    return pl.pallas_call(
        body, name="s5_fwd", grid=(nc,),
        in_specs=[Block((TC, W), lambda i: (i, 0)), _const((TC, TC)), _const((TC, TC)),
                  _const((8, 128, 512)), _const((8, 128, 512)), _const((8, 512, 128)), _const((8, 512, 128)),
                  _const((1, NS)), _const((1, NS)), _const((S5_SEG, NS)), _const((S5_SEG, NS)),
                  _const((1, W)), _const((W, W)), _const((1, W))],
        out_specs=[Block((TC, W), lambda i: (i, 0)), Block((TC, 2 * NS), lambda i: (i, 0)),
                   Block((TC, W), lambda i: (i, 0))],
        out_shape=[SDS((T, W), BF16), SDS((T, 2 * NS), F32), SDS((T, W), F32)],
        scratch_shapes=[pltpu.VMEM((1, 2 * NS), F32)],
        compiler_params=_cparams(("arbitrary",), VMEM_BIG),
    )(proj, perm, permt, bdbr, bdbi, bdcr, bdci, abr, abi, asr, asi, dvec, wglu, bglu)


def _s5_bwd(dya, proj, ypre, xs, perm, permt, bdbr, bdbi, bdcr, bdci, abr, abi, asr, asi, dvec, wglu, bglu):
    T = proj.shape[0]
    TC, NS, W = S5_TC, S5_NS, S5_WIDTH
    nc = T // TC

    def body(dya_ref, u_ref, ypre_ref, xs_ref, pm_ref, pmt_ref, bdbr_ref, bdbi_ref, bdcr_ref, bdci_ref,
             ar_ref, ai_ref, asr_ref, asi_ref, d_ref, wglu_ref, bglu_ref,
             du_ref, dar_ref, dai_ref, dd_ref, dbglu_ref, o_dbdbr, o_dbdbi, o_dbdcr, o_dbdci, o_dwglu,
             g_ref, carry_ref, dbdbr_ref, dbdbi_ref, dbdcr_ref, dbdci_ref, dwglu_ref):
        i = pl.program_id(0)

        @pl.when(i == 0)
        def _():
            carry_ref[...] = jnp.zeros_like(carry_ref)
            for r in (dbdbr_ref, dbdbi_ref, dbdcr_ref, dbdci_ref, dar_ref, dai_ref, dd_ref, dwglu_ref, dbglu_ref):
                r[...] = jnp.zeros_like(r)

        pm = pm_ref[...]
        dyo = _dot(pm, dya_ref[...])
        up = _dot(pm, u_ref[...]).astype(BF16)
        upf = up.astype(F32)
        ypre_v = ypre_ref[...]
        ya = _gelu(ypre_v)
        yab = ya.astype(BF16)
        sg = _sigmoid(_dot(yab, wglu_ref[...]) + bglu_ref[...])
        dz = dyo * ya * sg * (1.0 - sg)
        dzb = dz.astype(BF16)
        dya_t = dyo * sg + _dot(dzb, wglu_ref[...], NT)
        dwglu_ref[...] += _dot(yab, dzb, TN)
        dbglu_ref[...] += jnp.sum(dz, axis=0, keepdims=True)
        dy = dya_t * _gelu_grad(ypre_v)
        dd_ref[...] += jnp.sum(dy * upf, axis=0, keepdims=True)
        dyb = dy.astype(BF16)
        for j in range(8):
            dyj = dyb[:, j * 128:(j + 1) * 128]
            g_ref[:, j * 512:(j + 1) * 512] = _dot(dyj, bdcr_ref[j], NT)
            g_ref[:, NS + j * 512:NS + (j + 1) * 512] = _dot(dyj, bdci_ref[j], NT)
            dbdcr_ref[j] += _dot(xs_ref[:, j * 512:(j + 1) * 512].astype(BF16), dyj, TN)
            dbdci_ref[j] += _dot(xs_ref[:, NS + j * 512:NS + (j + 1) * 512].astype(BF16), dyj, TN)
        _s5_scan(g_ref, ar_ref[...], ai_ref[...], asr_ref, asi_ref, carry_ref, S5_LG,
                 xs_ref=xs_ref, dar_ref=dar_ref, dai_ref=dai_ref)
        dus = []
        for j in range(8):
            ut = up[:, j * 128:(j + 1) * 128]
            gr = g_ref[:, j * 512:(j + 1) * 512].astype(BF16)
            gi = g_ref[:, NS + j * 512:NS + (j + 1) * 512].astype(BF16)
            dbdbr_ref[j] += _dot(ut, gr, TN)
            dbdbi_ref[j] += _dot(ut, gi, TN)
            dus.append(_dot(gr, bdbr_ref[j], NT) + _dot(gi, bdbi_ref[j], NT))
        dup = jnp.concatenate(dus, axis=1) + d_ref[...] * dy
        du_ref[...] = _dot(pmt_ref[...], dup.astype(BF16)).astype(BF16)

        @pl.when(i == nc - 1)
        def _():
            for src, dst in ((dbdbr_ref, o_dbdbr), (dbdbi_ref, o_dbdbi), (dbdcr_ref, o_dbdcr),
                             (dbdci_ref, o_dbdci), (dwglu_ref, o_dwglu)):
                pltpu.sync_copy(src, dst)

    c2 = lambda i: (0, 0)
    rev = lambda i: (nc - 1 - i, 0)
    return pl.pallas_call(
        body, name="s5_bwd", grid=(nc,),
        in_specs=[Block((TC, W), rev), Block((TC, W), rev), Block((TC, W), rev), Block((TC, 2 * NS), rev),
                  _const((TC, TC)), _const((TC, TC)),
                  _const((8, 128, 512)), _const((8, 128, 512)), _const((8, 512, 128)), _const((8, 512, 128)),
                  _const((1, NS)), _const((1, NS)), _const((S5_SEG, NS)), _const((S5_SEG, NS)),
                  _const((1, W)), _const((W, W)), _const((1, W))],
        out_specs=[Block((TC, W), rev), Block((8, NS), c2), Block((8, NS), c2), Block((1, W), c2), Block((1, W), c2),
                   ANY, ANY, ANY, ANY, ANY],
        out_shape=[SDS((T, W), BF16), SDS((8, NS), F32), SDS((8, NS), F32), SDS((1, W), F32), SDS((1, W), F32),
                   SDS((8, 128, 512), F32), SDS((8, 128, 512), F32),
                   SDS((8, 512, 128), F32), SDS((8, 512, 128), F32), SDS((W, W), F32)],
        scratch_shapes=[pltpu.VMEM((TC, 2 * NS), F32), pltpu.VMEM((1, 2 * NS), F32),
                        pltpu.VMEM((8, 128, 512), F32), pltpu.VMEM((8, 128, 512), F32),
                        pltpu.VMEM((8, 512, 128), F32), pltpu.VMEM((8, 512, 128), F32), pltpu.VMEM((W, W), F32)],
        compiler_params=_cparams(("arbitrary",), VMEM_BIG),
    )(dya, proj, ypre, xs, perm, permt, bdbr, bdbi, bdcr, bdci, abr, abi, asr, asi, dvec, wglu, bglu)


def _bd_b(bk_t):
    C, P = S5_GROUP_WIDTH, S5_STATE
    t = jnp.transpose(bk_t, (1, 0, 2)).reshape(8, 8, C, P)
    eye = jnp.eye(8, dtype=t.dtype)
    return (t[:, :, :, None, :] * eye[None, :, None, :, None]).reshape(8, 8 * C, 8 * P)


def _bd_b_extract(m):
    C, P = S5_GROUP_WIDTH, S5_STATE
    t = m.reshape(8, 8, C, 8, P)
    d = jnp.stack([t[:, g, :, g, :] for g in range(8)], axis=1)
    return jnp.transpose(d.reshape(S5_GROUPS, C, P), (1, 0, 2))


def _bd_c(c):
    C, P = S5_GROUP_WIDTH, S5_STATE
    t = jnp.transpose(c, (0, 2, 1)).reshape(8, 8, P, C)
    eye = jnp.eye(8, dtype=t.dtype)
    return (t[:, :, :, None, :] * eye[None, :, None, :, None]).reshape(8, 8 * P, 8 * C)


def _bd_c_extract(m):
    C, P = S5_GROUP_WIDTH, S5_STATE
    t = m.reshape(8, 8, P, 8, C)
    d = jnp.stack([t[:, g, :, g, :] for g in range(8)], axis=1)
    return jnp.transpose(d.reshape(S5_GROUPS, P, C), (0, 2, 1))


def _perm_matrix():
    r = jnp.arange(S5_TC)
    src = (r % 8) * S5_SEG + r // 8
    return (src[:, None] == jnp.arange(S5_TC)[None, :]).astype(BF16)


def _coords():
    return lax.axis_index("x"), lax.axis_index("y"), lax.axis_index("c")


def _all_gather(name, arrs):
    n = len(arrs)

    def body(*refs):
        ins, outs = refs[:n], refs[n:2 * n]
        send_sems, recv_sems, local_sems = refs[2 * n:]
        x, y, c = _coords()
        me, sibling = (x, y, c), (x, y, 1 - c)
        chips = [(1 - x, y), (x, 1 - y), (1 - x, 1 - y)]

        def slot(p):
            return 4 * p[0] + 2 * p[1] + p[2]

        def copy(a, k, block, to, src=None):
            dst = outs[a].at[slot(block)]
            return pltpu.make_async_remote_copy(
                src_ref=dst if src is None else src, dst_ref=dst,
                send_sem=send_sems.at[a * 7 + k], recv_sem=recv_sems.at[a * 7 + k],
                device_id=to, device_id_type=MESH)

        mine = [pltpu.make_async_copy(ins[a], outs[a].at[slot(me)], local_sems.at[a]) for a in range(n)]
        for m in mine:
            m.start()
        first = []
        for a in range(n):
            first.append(copy(a, 0, me, sibling, src=ins[a]))
            first += [copy(a, 1 + j, me, (*chip, c), src=ins[a]) for j, chip in enumerate(chips)]
        for cp in first:
            cp.start()
        passed = []
        for j, chip in enumerate(chips):
            for a in range(n):
                copy(a, 1 + j, (*chip, c), me).wait_recv()
                fw = copy(a, 4 + j, (*chip, c), sibling)
                fw.start()
                passed.append(fw)
        for a in range(n):
            copy(a, 0, sibling, me).wait_recv()
            for j, chip in enumerate(chips):
                copy(a, 4 + j, (*chip, 1 - c), me).wait_recv()
        for cp in first + passed:
            cp.wait_send()
        for m in mine:
            m.wait()

    return pl.pallas_call(
        body, name=name,
        in_specs=[ANY] * n, out_specs=[ANY] * n,
        out_shape=[SDS((NDEV,) + a.shape, a.dtype) for a in arrs],
        scratch_shapes=[pltpu.SemaphoreType.DMA((7 * n,)), pltpu.SemaphoreType.DMA((7 * n,)),
                        pltpu.SemaphoreType.DMA((n,))],
    )(*arrs)


HBM = pl.BlockSpec(memory_space=pltpu.HBM)
SEM = pl.BlockSpec(memory_space=pltpu.SEMAPHORE)
EFFECT = pltpu.SideEffectType.DATAFLOW_SIDE_EFFECTING


def _peers7(x, y, c):
    return [(1 - x if fx else x, 1 - y if fy else y, 1 - c if fc else c)
            for fx in (0, 1) for fy in (0, 1) for fc in (0, 1) if fx or fy or fc]


def _slot(p):
    return 4 * p[0] + 2 * p[1] + p[2]


def _split_copies(src_refs, land_refs, send_sems, recv_sems, gather, mine):
    x, y, c = _coords()
    me = (x, y, c)
    out = []
    for a, (src, land) in enumerate(zip(src_refs, land_refs)):
        for k, p in enumerate(_peers7(x, y, c)):
            s = src if gather else src.at[_slot(p)]
            out.append(pltpu.make_async_remote_copy(
                src_ref=s, dst_ref=land.at[_slot(me) if mine else _slot(p)],
                send_sem=send_sems.at[a * 7 + k], recv_sem=recv_sems.at[a * 7 + k],
                device_id=p, device_id_type=MESH))
    return out


def _own_slab(shard):
    x, y, c = _coords()
    z = lax.empty((NDEV,) + shard.shape, shard.dtype)
    return lax.dynamic_update_slice(z, shard[None], (_slot((x, y, c)),) + (0,) * shard.ndim)


def _split_start(name, srcs, lands, gather):
    n = len(srcs)

    def body(*refs):
        src_refs, land_refs = refs[:n], refs[n:2 * n]
        send_sems, recv_sems = refs[2 * n], refs[2 * n + 1]
        token = refs[-1]
        for cp in _split_copies(src_refs, land_refs, send_sems, recv_sems, gather, True):
            cp.start()
        token[...] = jnp.zeros_like(token)

    thru = [pltpu.HBM(a.shape, a.dtype) for a in list(srcs) + list(lands)]
    res = pl.pallas_call(
        body, name=name,
        out_shape=(pltpu.SemaphoreType.DMA((7 * n,)), pltpu.SemaphoreType.DMA((7 * n,)), *thru, SDS((8, 128), F32)),
        in_specs=[HBM] * (2 * n),
        out_specs=(SEM, SEM, *([HBM] * (2 * n)), pl.BlockSpec(memory_space=pltpu.VMEM)),
        input_output_aliases={i: 2 + i for i in range(2 * n)},
        compiler_params=pltpu.CompilerParams(has_side_effects=EFFECT),
    )(*[pltpu.with_memory_space_constraint(a, pltpu.HBM) for a in list(srcs) + list(lands)])
    return res[0], res[1], list(res[2:2 + n]), list(res[2 + n:2 + 2 * n]), res[-1]


def _split_wait(name, started, after, gather):
    send_sems, recv_sems, srcs, lands, _ = started
    n = len(srcs)

    def body(*refs):
        src_refs, land_refs = refs[:n], refs[n:2 * n]
        s_sems, r_sems = refs[2 * n], refs[2 * n + 1]
        for cp in _split_copies(src_refs, land_refs, s_sems, r_sems, gather, False):
            cp.wait_send()
            cp.wait_recv()

    thru = [pltpu.HBM(a.shape, a.dtype) for a in list(srcs) + list(lands)]
    res = pl.pallas_call(
        body, name=name, out_shape=tuple(thru),
        in_specs=[HBM] * (2 * n) + [SEM, SEM, ANY], out_specs=tuple([HBM] * (2 * n)),
        input_output_aliases={i: i for i in range(2 * n)},
        compiler_params=pltpu.CompilerParams(has_side_effects=EFFECT),
    )(*srcs, *lands, send_sems, recv_sems, after)
    return list(res[n:])


def _adam_math(w, g, m, v):
    m = ADAM_B1 * m + (1.0 - ADAM_B1) * g
    v = ADAM_B2 * v + (1.0 - ADAM_B2) * (g * g)
    m_hat = m / (1.0 - ADAM_B1 ** ADAM_STEP)
    v_hat = v / (1.0 - ADAM_B2 ** ADAM_STEP)
    delta = -ADAM_LR * (m_hat / (jnp.sqrt(v_hat) + ADAM_EPS) + ADAM_WD * w)
    return delta, m, v


def _adam_sharded(name, recv, sub, w, m, v):
    R, Cc = w.shape
    tr = max(t for t in range(16, R + 1, 16) if R % t == 0 and t * Cc <= 256 * 1024)

    def body(*refs):
        parts = refs[:NDEV]
        w_ref, m_ref, v_ref, g_out, d_out, m_out, v_out = refs[NDEV:]
        g = parts[0][...].astype(F32)
        for p in parts[1:]:
            g = g + p[...].astype(F32)
        delta, mn, vn = _adam_math(w_ref[...], g, m_ref[...], v_ref[...])
        g_out[...] = g
        d_out[...] = delta
        m_out[...] = mn
        v_out[...] = vn

    if sub is None:
        pspecs = [Block((None, tr, Cc), functools.partial(lambda s, i: (s, i, 0), s)) for s in range(NDEV)]
    else:
        pspecs = [Block((None, None, tr, Cc), functools.partial(lambda s, i: (s, sub, i, 0), s)) for s in range(NDEV)]
    row = Block((tr, Cc), lambda i: (i, 0))
    o = SDS((R, Cc), F32)
    return pl.pallas_call(
        body, name=name, grid=(R // tr,),
        in_specs=pspecs + [row, row, row], out_specs=[row, row, row, row], out_shape=[o, o, o, o],
        compiler_params=_cparams(("arbitrary",), VMEM_BIG))(*([recv] * NDEV), w, m, v)


def _adam_small(groups):
    n = len(groups)

    def body(*refs):
        ins, outs = refs[:4 * n], refs[4 * n:]
        for a in range(n):
            p_ref, w_ref, m_ref, v_ref = ins[4 * a:4 * a + 4]
            g = p_ref[0]
            for s in range(1, NDEV):
                g = g + p_ref[s]
            delta, mn, vn = _adam_math(w_ref[...], g, m_ref[...], v_ref[...])
            for o, r in zip(outs[4 * a:4 * a + 4], (g, delta, mn, vn)):
                o[...] = r

    flat_in = [t for grp in groups for t in grp]
    out_shape = [SDS(grp[1].shape, F32) for grp in groups for _ in range(4)]
    res = pl.pallas_call(body, name="adam_small", out_shape=out_shape,
                         compiler_params=_cparams(None, VMEM_BIG))(*flat_in)
    return [tuple(res[4 * a:4 * a + 4]) for a in range(n)]


_TINY = ["mix_norm", "s5_a_re", "s5_a_im", "s5_log_dt", "s5_d", "s5_b_glu", "sgu_ln_g", "sgu_ln_b",
         "sgu_b_s", "b_gate", "ffn2_norm", "final_norm"]
_ORDER = ["ffn1_norm", "ffn1_w_gate", "ffn1_w_up", "ffn1_w_down", "mix_norm", "w_in", "s5_a_re", "s5_a_im",
          "s5_log_dt", "s5_b_re", "s5_b_im", "s5_c_re", "s5_c_im", "s5_d", "s5_w_glu", "s5_b_glu", "sgu_ln_g",
          "sgu_ln_b", "sgu_w_s", "sgu_b_s", "w_branch_a", "w_branch_b", "w_gate", "b_gate", "w_out", "ffn2_norm",
          "ffn2_w_gate", "ffn2_w_up", "ffn2_w_down", "final_norm"]


def _step(x, tgt, W, M, V):
    T = x.shape[1]
    x0 = x[0]
    tgt0 = tgt[0]
    bf = lambda a: a.astype(BF16)

    def gather_start(name, shards):
        return _split_start(name, shards, [_own_slab(s) for s in shards], True)

    (wgu1,) = _all_gather("gather1", [jnp.stack([bf(W["ffn1_w_gate"][0].T), bf(W["ffn1_w_up"][0].T)])])

    lr_, li_ = W["s5_a_re"][0], W["s5_a_im"][0]
    ldt_ = W["s5_log_dt"][0][:, None]
    brt = jnp.transpose(W["s5_b_re"][0], (2, 0, 1))
    bit = jnp.transpose(W["s5_b_im"][0], (2, 0, 1))
    abr, abi, pwr, pwi, bkr_t, bki_t = _s5_prep(lr_, li_, ldt_, brt, bit)
    bdbr, bdbi = bf(_bd_b(bkr_t)), bf(_bd_b(bki_t))
    bdcr, bdci = bf(_bd_c(W["s5_c_re"][0])), bf(_bd_c(-W["s5_c_im"][0]))
    flat = lambda a: a.reshape(1, S5_NS)
    s5a = (_perm_matrix(), _perm_matrix().T, bdbr, bdbi, bdcr, bdci, flat(abr), flat(abi),
           pwr.reshape(S5_SEG, S5_NS), pwi.reshape(S5_SEG, S5_NS),
           W["s5_d"][0].reshape(1, S5_WIDTH))
    blk = jnp.arange(MLP_CHUNK) // CHUNK
    mask = blk[:, None] >= blk[None, :]
    wsm = jnp.where(mask[None], W["sgu_w_s"][0], 0.0)
    wsm_b, wsmt_b = bf(wsm), bf(jnp.transpose(wsm, (0, 2, 1)))
    bst = jnp.broadcast_to(W["sgu_b_s"][0][:, :, None], (SGU_HEADS, MLP_CHUNK, 128))
    bgate2 = W["b_gate"].reshape(2, 1, D_MODEL)

    h1 = _rms_fwd("rms1", x0, W["ffn1_norm"])
    dep = (wgu1[0, 0, :1, :1] * 0).astype(BF16)

    def later(a):
        return bf(a) + dep[0]

    gs2 = gather_start("gather2_start", [later(W["ffn1_w_down"][0])])
    ab1, f1 = _ffn_up("ffn1_up", h1, wgu1, gs2[4])
    (wd1,) = _split_wait("gather2_wait", gs2, f1, True)
    dep = (wd1[0, :1, :1] * 0).astype(BF16)
    gs3 = gather_start("gather3_start", [later(W["w_in"][0]), later(W["s5_w_glu"][0])])
    x1 = _ffn_down("ffn1_down", f1, wd1, x0, after=gs3[4])
    h2 = _rms_fwd("rms2", x1, W["mix_norm"])
    win, wglu = _split_wait("gather3_wait", gs3, h2, True)
    wglu = wglu.reshape(S5_WIDTH, S5_WIDTH)
    s5c = s5a + (wglu, W["s5_b_glu"])
    dep = (win[0, :1, :1] * 0).astype(BF16)
    gs4 = gather_start("gather4_start", [later(W["w_gate"][0]), later(W["w_branch_a"][0]),
                                         later(W["w_branch_b"][0]), later(W["w_out"][0])])
    proj = _col_fwd("w_in", h2, win, after=gs4[4])
    ya, xs, ypre = _s5_fwd(proj, *s5c)
    dep = (ya[:1, :1] * 0).astype(BF16)
    gs5 = gather_start("gather5_start", [jnp.stack([later(W["ffn2_w_gate"][0].T), later(W["ffn2_w_up"][0].T)])])
    yb = _sgu_fwd("sgu_fwd", proj, W["sgu_ln_g"] + gs5[4][:1, :1], W["sgu_ln_b"], wsm_b, bst)
    wgate, wba, wbb, wout = _split_wait("gather4_wait", gs4, yb, True)
    wout = wout.reshape(D_MODEL, D_MODEL)
    pa = _col_fwd("branch_a", ya, wba)
    pb = _col_fwd("branch_b", yb, wbb)
    gl = _gate_fwd("gate", h2, wgate, bgate2)
    merged = _merge_fwd("merge", pa, pb, gl)
    x2 = _plain_fwd_res("w_out", merged, wout, x1)
    h3 = _rms_fwd("rms3", x2, W["ffn2_norm"])
    (wgu2,) = _split_wait("gather5_wait", gs5, h3, True)
    dep = (wgu2[0, 0, :1, :1] * 0).astype(BF16)
    gs6 = gather_start("gather6_start", [later(W["ffn2_w_down"][0])])
    ab2, f2 = _ffn_up("ffn2_up", h3, wgu2, gs6[4])
    (wd2,) = _split_wait("gather6_wait", gs6, f2, True)
    x3 = _ffn_down("ffn2_down", f2, wd2, x2)
    loss_p, dx3b, dgf = _loss_head("loss_head", x3, W["final_norm"].reshape(1, D_MODEL), tgt0)

    def exchange_start(name, grads):
        x_, y_, c_ = _coords()
        me = _slot((x_, y_, c_))
        return _split_start(name, grads, [_own_slab(lax.dynamic_index_in_dim(g, me, 0, keepdims=False))
                                          for g in grads], False)

    dab2 = _ffn_down_bwd_act("ffn2_down_bwd_a", dx3b, wd2, ab2)
    g_wd2 = _ffn_down_bwd_w("ffn2_down_bwd_w", f2, dx3b)
    g_gu2 = _ffn_up_bwd_w("ffn2_up_bwd_w", h3, dab2)
    es1 = exchange_start("exchange1_start", [g_wd2, g_gu2])
    dh3 = _ffn_up_bwd_h("ffn2_up_bwd_h", dab2, wgu2, es1[4])
    dx2b, dg3 = _rms_bwd("rms3_bwd", dh3, x2, W["ffn2_norm"], dx3b, BF16)

    dmerged = _plain_bwd_a("w_out_bwd_a", dx2b, wout)
    g_wout = _plain_bwd_w("w_out_bwd_w", merged, dx2b)
    dpa, dpb, dgl, dbgate = _merge_bwd("merge_bwd", dmerged, pa, pb, gl)
    dya = _col_bwd_a("branch_a_bwd_a", dpa, wba)
    g_wba = _col_bwd_w("branch_a_bwd_w", ya, dpa, 256)
    dyb = _col_bwd_a("branch_b_bwd_a", dpb, wbb)
    g_wbb = _col_bwd_w("branch_b_bwd_w", yb, dpb, 256)
    dh2g = _gate_bwd_a("gate_bwd_a", dgl, wgate)
    g_wgate = _gate_bwd_w("gate_bwd_w", h2, dgl, 512)
    (dua, dar8, dai8, ddv, dbglu, dbdbr, dbdbi, dbdcr, dbdci, g_wglu) = _s5_bwd(dya, proj, ypre, xs, *s5c)
    dproj, dws, dbst, dlng, dlnb = _sgu_bwd("sgu_bwd", dyb, dua, proj, W["sgu_ln_g"], W["sgu_ln_b"],
                                            wsm_b, wsmt_b, bst)
    g_win = _col_bwd_w("w_in_bwd_w", h2, dproj, 384)
    g_wout3 = g_wout.reshape(NDEV, D_MODEL // NDEV, D_MODEL)
    g_wglu3 = g_wglu.astype(BF16).reshape(NDEV, S5_WIDTH // NDEV, S5_WIDTH)
    es2 = exchange_start("exchange2_start", [g_wout3, g_wba, g_wbb, g_wgate, g_wglu3, g_win])
    dh2 = _col_bwd_a("w_in_bwd_a", dproj, win, add=dh2g)
    dx1b, dgm = _rms_bwd("rms2_bwd", dh2, x1, W["mix_norm"] + es2[4][:1, :1], dx2b, BF16)

    dabr = jnp.sum(dar8, axis=0).reshape(S5_GROUPS, S5_STATE)
    dabi = jnp.sum(dai8, axis=0).reshape(S5_GROUPS, S5_STATE)
    d_lr, d_li, d_ldt, d_brt, d_bit = _s5_prep_bwd(lr_, li_, ldt_, brt, bit, dabr, dabi,
                                                   _bd_b_extract(dbdbr), _bd_b_extract(dbdbi))
    small_g = {
        "mix_norm": dgm, "ffn2_norm": dg3, "final_norm": dgf,
        "s5_a_re": d_lr, "s5_a_im": d_li, "s5_log_dt": d_ldt,
        "s5_d": ddv, "s5_b_glu": dbglu, "sgu_ln_g": dlng, "sgu_ln_b": dlnb,
        "sgu_b_s": dbst[:, :, 0], "b_gate": dbgate,
    }
    to_cgp = lambda a: jnp.transpose(a[0], (2, 0, 1))
    from_cgp = lambda a: jnp.transpose(a, (1, 2, 0))[None]
    natural = [
        ("s5_b_re", d_brt, to_cgp, from_cgp), ("s5_b_im", d_bit, to_cgp, from_cgp),
        ("s5_c_re", _bd_c_extract(dbdcr), lambda a: a[0], lambda a: a[None]),
        ("s5_c_im", -_bd_c_extract(dbdci), lambda a: a[0], lambda a: a[None]),
        ("sgu_w_s", jnp.where(mask[None], dws, 0.0), lambda a: a[0], lambda a: a[None]),
    ]
    sizes = [W[n].size for n in _TINY]
    total = sum(sizes) + 1
    rows = -(-total // 128)
    rows = -(-rows // 8) * 8
    pad = rows * 128 - total

    def pack(d, extra):
        return jnp.concatenate([d[n].reshape(-1).astype(F32) for n in _TINY] + [extra, jnp.zeros((pad,), F32)]
                               ).reshape(rows, 128)

    gsm = gather_start("gather_small_start", [pack(small_g, loss_p[0, :1])] + [g for _, g, _, _ in natural])

    dab1 = _ffn_down_bwd_act("ffn1_down_bwd_a", dx1b, wd1, ab1, after=gsm[4])
    g_gu1 = _ffn_up_bwd_w("ffn1_up_bwd_w", h1, dab1)
    es3 = exchange_start("exchange3_start", [g_gu1])
    g_wd1 = _ffn_down_bwd_w("ffn1_down_bwd_w", f1, dx1b, after=es3[4])
    es4 = exchange_start("exchange4_start", [g_wd1])
    dh1 = _ffn_up_bwd_h("ffn1_up_bwd_h", dab1, wgu1, es4[4])
    dx0, dg1 = _rms_bwd("rms1_bwd", dh1, x0, W["ffn1_norm"], dx1b, F32)

    G, Dl, Mn, Vn = {}, {}, {}, {}

    def adam(plan):
        last = None
        for n, recv, sub in plan:
            if sub is None:
                g, d, mn, vn = _adam_sharded("adam_" + n, recv, sub, W[n][0], M[n][0], V[n][0])
                G[n], Dl[n], Mn[n], Vn[n] = g[None], d[None], mn[None], vn[None]
            else:
                tr = jnp.transpose
                g, d, mn, vn = _adam_sharded("adam_" + n, recv, sub, tr(W[n][0]), tr(M[n][0]), tr(V[n][0]))
                G[n], Dl[n], Mn[n], Vn[n] = tr(g)[None], tr(d)[None], tr(mn)[None], tr(vn)[None]
            last = g
        return last

    r_wd2, r_gu2 = _split_wait("exchange1_wait", es1, dx0, False)
    done = adam([("ffn2_w_down", r_wd2, None), ("ffn2_w_gate", r_gu2, 0), ("ffn2_w_up", r_gu2, 1)])
    r_wout, r_wba, r_wbb, r_wgate, r_wglu, r_win = _split_wait("exchange2_wait", es2, done, False)
    done = adam([("w_out", r_wout, None), ("w_branch_a", r_wba, None), ("w_branch_b", r_wbb, None),
                 ("w_gate", r_wgate, None), ("s5_w_glu", r_wglu, None), ("w_in", r_win, None)])

    late = dg1 + 0.0 * done.reshape(-1)[:1]
    zero1 = jnp.zeros((1,), F32)
    parts = _split_wait("gather_small_wait", gsm, late, True)
    (parts_g1,) = _all_gather("gather_ffn1_norm_grad", [late])
    groups = [(parts[0], pack(W, zero1), pack(M, zero1), pack(V, zero1))]
    groups += [(parts[1 + a], view(W[n]), view(M[n]), view(V[n])) for a, (n, _, view, _) in enumerate(natural)]
    groups += [(parts_g1, W["ffn1_norm"], M["ffn1_norm"], V["ffn1_norm"])]
    res = _adam_small(groups)
    sg, sd, sm, sv = res[0]
    for (n, _, _, back), (g, d, mn, vn) in zip(natural, res[1:-1]):
        G[n], Dl[n], Mn[n], Vn[n] = back(g), back(d), back(mn), back(vn)
    G["ffn1_norm"], Dl["ffn1_norm"], Mn["ffn1_norm"], Vn["ffn1_norm"] = res[-1]

    def unpack(flat2d, into):
        flat = flat2d.reshape(-1)
        off = 0
        for n, s in zip(_TINY, sizes):
            into[n] = flat[off:off + s].reshape(W[n].shape)
            off += s
        return flat[off]

    loss = unpack(sg, G)
    unpack(sd, Dl)
    unpack(sm, Mn)
    unpack(sv, Vn)

    (r_gu1,) = _split_wait("exchange3_wait", es3, sg, False)
    done = adam([("ffn1_w_gate", r_gu1, 0), ("ffn1_w_up", r_gu1, 1)])
    (r_wd1,) = _split_wait("exchange4_wait", es4, done, False)
    adam([("ffn1_w_down", r_wd1, None)])

    return loss, dx0[None], G, Dl, Mn, Vn


def kernel(x, ffn1_norm, ffn1_w_gate, ffn1_w_up, ffn1_w_down, mix_norm, w_in, s5_a_re, s5_a_im, s5_log_dt, s5_b_re, s5_b_im, s5_c_re, s5_c_im, s5_d, s5_w_glu, s5_b_glu, sgu_ln_g, sgu_ln_b, sgu_w_s, sgu_b_s, w_branch_a, w_branch_b, w_gate, b_gate, w_out, ffn2_norm, ffn2_w_gate, ffn2_w_up, ffn2_w_down, final_norm, loss_target, m_ffn1_norm, m_ffn1_w_gate, m_ffn1_w_up, m_ffn1_w_down, m_mix_norm, m_w_in, m_s5_a_re, m_s5_a_im, m_s5_log_dt, m_s5_b_re, m_s5_b_im, m_s5_c_re, m_s5_c_im, m_s5_d, m_s5_w_glu, m_s5_b_glu, m_sgu_ln_g, m_sgu_ln_b, m_sgu_w_s, m_sgu_b_s, m_w_branch_a, m_w_branch_b, m_w_gate, m_b_gate, m_w_out, m_ffn2_norm, m_ffn2_w_gate, m_ffn2_w_up, m_ffn2_w_down, m_final_norm, v_ffn1_norm, v_ffn1_w_gate, v_ffn1_w_up, v_ffn1_w_down, v_mix_norm, v_w_in, v_s5_a_re, v_s5_a_im, v_s5_log_dt, v_s5_b_re, v_s5_b_im, v_s5_c_re, v_s5_c_im, v_s5_d, v_s5_w_glu, v_s5_b_glu, v_sgu_ln_g, v_sgu_ln_b, v_sgu_w_s, v_sgu_b_s, v_w_branch_a, v_w_branch_b, v_w_gate, v_b_gate, v_w_out, v_ffn2_norm, v_ffn2_w_gate, v_ffn2_w_up, v_ffn2_w_down, v_final_norm):
    a = locals()
    W = {n: a[n] for n in _ORDER}
    M = {n: a["m_" + n] for n in _ORDER}
    V = {n: a["v_" + n] for n in _ORDER}
    loss, gx, G, Dl, Mn, Vn = _step(x, loss_target, W, M, V)
    return (loss, gx, *[G[n] for n in _ORDER], *[Dl[n] for n in _ORDER], *[Mn[n] for n in _ORDER],
            *[Vn[n] for n in _ORDER])
```

```python
import functools
import math

import jax
import jax.numpy as jnp
from jax import lax
from jax.experimental import pallas as pl
from jax.experimental.pallas import tpu as pltpu

F32 = jnp.float32
BF16 = jnp.bfloat16
NDEV = 8
NORM_EPS = 1e-6
D_MODEL = 2048
D_FF = 5632
FF_SHARD = D_FF // NDEV
S5_WIDTH = 1024
S5_GROUPS = 64
S5_GROUP_WIDTH = 16
S5_STATE = 64
S5_NS = S5_GROUPS * S5_STATE
SGU_WIDTH = 1024
SGU_HEADS = 8
MLP_CHUNK = 128
CHUNK = 64
ADAM_LR, ADAM_B1, ADAM_B2, ADAM_EPS, ADAM_WD, ADAM_STEP = 0.001, 0.9, 0.999, 1e-08, 0.01, 10
S5_TC = 256
S5_SEG = S5_TC // 8
S5_LG = 512
S5_UNROLL = True
VMEM_BIG = 56 * 1024 * 1024

MESH = pl.DeviceIdType.MESH
SDS = jax.ShapeDtypeStruct
Block = pl.BlockSpec
ANY = pl.BlockSpec(memory_space=pl.ANY)


def _cparams(sem=None, vmem=None):
    return pltpu.CompilerParams(dimension_semantics=sem, vmem_limit_bytes=vmem)


def _const(shape):
    nd = len(shape)
    return pl.BlockSpec(shape, lambda i: (0,) * nd, pipeline_mode=pl.Buffered(1))


def _sigmoid(x):
    return 0.5 * jnp.tanh(0.5 * x) + 0.5


_GELU_C = math.sqrt(2.0 / math.pi)


def _gelu(x):
    return 0.5 * x * (1.0 + jnp.tanh(_GELU_C * (x + 0.044715 * x * x * x)))


def _gelu_grad(x):
    t = jnp.tanh(_GELU_C * (x + 0.044715 * x * x * x))
    return 0.5 * (1.0 + t) + 0.5 * x * (1.0 - t * t) * _GELU_C * (1.0 + 3.0 * 0.044715 * x * x)


NN = (((1,), (0,)), ((), ()))
NT = (((1,), (1,)), ((), ()))
TN = (((0,), (0,)), ((), ()))


def _dot(a, b, dims=NN):
    return lax.dot_general(a, b, dims, preferred_element_type=F32)


def _matmul(name, a, b, extras, *, grid, a_spec, b_spec, extra_specs, out_shapes, out_specs, acc_shape,
            epilogue, dims=NN, nb=None, compute=None, after=None, vmem=VMEM_BIG):
    nk = grid[2]
    if after is not None:
        extras = tuple(extras) + (after,)
        extra_specs = list(extra_specs) + [Block((8, 128), lambda i, j, k: (0, 0))]
    ne, no = len(extras), len(out_shapes)
    nacc = nb or 1
    if compute is None:
        def compute(a_ref, b_ref, q):
            return _dot(a_ref[...], b_ref[q] if nb else b_ref[...], dims)

    def body(*refs):
        a_ref, b_ref = refs[0], refs[1]
        ex = refs[2:2 + ne]
        outs = refs[2 + ne:2 + ne + no]
        if nk == 1:
            epilogue([compute(a_ref, b_ref, q) for q in range(nacc)], ex, outs)
            return
        acc_ref = refs[2 + ne + no]
        k = pl.program_id(2)

        @pl.when(k == 0)
        def _():
            acc_ref[...] = jnp.zeros_like(acc_ref)

        for q in range(nacc):
            acc_ref[q] += compute(a_ref, b_ref, q)

        @pl.when(k == nk - 1)
        def _():
            epilogue([acc_ref[q] for q in range(nacc)], ex, outs)

    scratch = [] if nk == 1 else [pltpu.VMEM((nacc,) + tuple(acc_shape), F32)]
    res = pl.pallas_call(
        body, name=name, grid=grid,
        in_specs=[a_spec, b_spec] + list(extra_specs),
        out_specs=list(out_specs), out_shape=list(out_shapes), scratch_shapes=scratch,
        compiler_params=_cparams(("parallel", "parallel", "arbitrary"), vmem),
    )(a, b, *extras)
    return res


def _store(dtype_outs=None):
    def ep(accs, ex, outs):
        outs[0][...] = accs[0].astype(outs[0].dtype)
    return ep


def _tile(n, t):
    t = min(n, t)
    assert n % t == 0, (n, t)
    return t


def _ksum(kq, dims):
    def compute(a_ref, b_ref, _):
        part = _dot(a_ref[0], b_ref[0], dims)
        for q in range(1, kq):
            part = part + _dot(a_ref[q], b_ref[q], dims)
        return part
    return compute


def _ksum_lanes(kq, ns, dims):
    def compute(a_ref, b_ref, _):
        part = _dot(a_ref[:, 0:ns], b_ref[0], dims)
        for q in range(1, kq):
            part = part + _dot(a_ref[:, q * ns:(q + 1) * ns], b_ref[q], dims)
        return part
    return compute


def _wide_b(g):
    def compute(a_ref, b_ref, _):
        bw = b_ref[0] if g == 1 else jnp.concatenate([b_ref[q] for q in range(g)], axis=1)
        return _dot(a_ref[...], bw, NN)
    return compute


TT_DEEP = 2048
TT_FFN = 4096


HIDDEN = NDEV * FF_SHARD


def _ffn_up(name, h, wgu, after=None):
    T, D = h.shape
    tm = _tile(T, 1024)

    def ep(accs, ex, outs):
        a, b = accs
        outs[0][0] = a.astype(BF16)
        outs[0][1] = b.astype(BF16)
        outs[1][...] = (a * _sigmoid(a) * b).astype(BF16)

    return _matmul(
        name, wgu, h, (), after=after, grid=(NDEV, T // tm, 1),
        a_spec=Block((None, 2, FF_SHARD, D), lambda j, i, k: (j, 0, 0, 0)),
        b_spec=Block((tm, D), lambda j, i, k: (i, 0)),
        extra_specs=(),
        out_shapes=[SDS((NDEV, 2, FF_SHARD, T), BF16), SDS((NDEV, FF_SHARD, T), BF16)],
        out_specs=[Block((None, 2, FF_SHARD, tm), lambda j, i, k: (j, 0, 0, i)),
                   Block((None, FF_SHARD, tm), lambda j, i, k: (j, 0, i))],
        acc_shape=(FF_SHARD, tm), nb=2,
        compute=lambda a_ref, b_ref, q: _dot(a_ref[q], b_ref[...], NT), epilogue=ep)


def _ffn_down(name, f, wd, xres, after=None):
    T = f.shape[2]
    tm, tn, tk = _tile(T, 1024), 1024, HIDDEN // 2

    def ep(accs, ex, outs):
        outs[0][...] = ex[0][...] + 0.5 * accs[0]

    return _matmul(
        name, f.reshape(HIDDEN, T), wd.reshape(HIDDEN, D_MODEL), (xres,), after=after,
        grid=(T // tm, D_MODEL // tn, HIDDEN // tk),
        a_spec=Block((tk, tm), lambda i, j, k: (k, i)),
        b_spec=Block((tk, tn), lambda i, j, k: (k, j)),
        extra_specs=[Block((tm, tn), lambda i, j, k: (i, j))],
        out_shapes=[SDS((T, D_MODEL), F32)],
        out_specs=[Block((tm, tn), lambda i, j, k: (i, j))],
        acc_shape=(tm, tn), dims=TN, epilogue=ep)[0]


def _ffn_down_bwd_act(name, dyb, wd, ab, after=None):
    T, D = dyb.shape
    tm, g = _tile(T, 1024), 1

    def ep(accs, ex, outs):
        for s in range(g):
            df = accs[0][s * FF_SHARD:(s + 1) * FF_SHARD, :].astype(BF16)
            a = ex[0][s, 0]
            b = ex[0][s, 1]
            hs = 0.5 * _sigmoid(a)
            outs[0][s, 0] = df * b * hs * (1.0 + a * (1.0 - 2.0 * hs))
            outs[0][s, 1] = df * a * hs

    blk = Block((g, 2, FF_SHARD, tm), lambda i, j, k: (j, 0, 0, i))
    return _matmul(
        name, wd.reshape(HIDDEN, D), dyb, (ab,), after=after, grid=(T // tm, NDEV // g, 1),
        a_spec=Block((g * FF_SHARD, D), lambda i, j, k: (j, 0)),
        b_spec=Block((tm, D), lambda i, j, k: (i, 0)),
        extra_specs=[blk],
        out_shapes=[SDS((NDEV, 2, FF_SHARD, T), BF16)],
        out_specs=[blk],
        acc_shape=(g * FF_SHARD, tm), dims=NT, epilogue=ep)[0]


def _ffn_down_bwd_w(name, f, dyb, after=None):
    T = f.shape[2]
    tt, tn, tr = _tile(T, TT_DEEP), 1024, 2 * FF_SHARD

    def ep(accs, ex, outs):
        outs[0][...] = (0.5 * accs[0]).astype(BF16)

    return _matmul(
        name, f.reshape(HIDDEN, T), dyb, (), after=after, grid=(HIDDEN // tr, D_MODEL // tn, T // tt),
        a_spec=Block((tr, tt), lambda j, n, k: (j, k)),
        b_spec=Block((tt, tn), lambda j, n, k: (k, n)),
        extra_specs=(),
        out_shapes=[SDS((HIDDEN, D_MODEL), BF16)],
        out_specs=[Block((tr, tn), lambda j, n, k: (j, n))],
        acc_shape=(tr, tn), dims=NN, epilogue=ep)[0].reshape(NDEV, FF_SHARD, D_MODEL)


def _ffn_up_bwd_h(name, dab, wgu, after):
    T = dab.shape[3]
    tm = _tile(T, 1024)
    return _matmul(
        name, dab, wgu, (), after=after, grid=(T // tm, 1, NDEV),
        a_spec=Block((None, 2, FF_SHARD, tm), lambda i, j, k: (k, 0, 0, i)),
        b_spec=Block((None, 2, FF_SHARD, D_MODEL), lambda i, j, k: (k, 0, 0, 0)),
        extra_specs=(),
        out_shapes=[SDS((T, D_MODEL), BF16)],
        out_specs=[Block((tm, D_MODEL), lambda i, j, k: (i, 0))],
        acc_shape=(tm, D_MODEL), compute=_ksum(2, TN), epilogue=_store())[0]


def _ffn_up_bwd_w(name, h, dab):
    T, D = h.shape
    tt, tn = _tile(T, TT_FFN), 1024

    def ep(accs, ex, outs):
        outs[0][0] = accs[0].astype(BF16)
        outs[0][1] = accs[1].astype(BF16)

    return _matmul(
        name, dab, h, (), grid=(NDEV, D // tn, T // tt),
        a_spec=Block((None, 2, FF_SHARD, tt), lambda j, n, k: (j, 0, 0, k)),
        b_spec=Block((tt, tn), lambda j, n, k: (k, n)),
        extra_specs=(),
        out_shapes=[SDS((NDEV, 2, FF_SHARD, D), BF16)],
        out_specs=[Block((None, 2, FF_SHARD, tn), lambda j, n, k: (j, 0, 0, n))],
        acc_shape=(FF_SHARD, tn), nb=2,
        compute=lambda a_ref, b_ref, q: _dot(a_ref[q], b_ref[...], NN), epilogue=ep)[0]


def _shards_per_step(ns):
    return max(g for g in (1, 2, 4, 8) if g * ns <= 2048)


def _split_lanes(g, ns):
    def ep(accs, ex, outs):
        for q in range(g):
            outs[0][q] = accs[0][:, q * ns:(q + 1) * ns].astype(outs[0].dtype)
    return ep


def _col_fwd(name, a, w, out_dtype=BF16, after=None):
    T, K = a.shape
    ns = w.shape[2]
    g = _shards_per_step(ns)
    tm = _tile(T, 1024)
    return _matmul(
        name, a, w, (), after=after, grid=(NDEV // g, T // tm, 1),
        a_spec=Block((tm, K), lambda j, i, k: (i, 0)),
        b_spec=Block((g, K, ns), lambda j, i, k: (j, 0, 0)),
        extra_specs=(),
        out_shapes=[SDS((T, NDEV * ns), out_dtype)],
        out_specs=[Block((tm, g * ns), lambda j, i, k: (i, j))],
        acc_shape=(tm, g * ns), compute=_wide_b(g), epilogue=_store())[0]


def _col_bwd_a(name, dy, w, add=None):
    T = dy.shape[0]
    _, K, ns = w.shape
    tm, tn = _tile(T, 1024), _tile(K, 1024)

    def ep(accs, ex, outs):
        r = accs[0]
        if add is not None:
            r = r + ex[0][...].astype(F32)
        outs[0][...] = r.astype(BF16)

    extras = () if add is None else (add,)
    return _matmul(
        name, dy, w, extras, grid=(T // tm, K // tn, 1),
        a_spec=Block((tm, NDEV * ns), lambda i, j, k: (i, 0)),
        b_spec=Block((NDEV, tn, ns), lambda i, j, k: (0, j, 0)),
        extra_specs=[Block((tm, tn), lambda i, j, k: (i, j))] * len(extras),
        out_shapes=[SDS((T, K), BF16)],
        out_specs=[Block((tm, tn), lambda i, j, k: (i, j))],
        acc_shape=(tm, tn), compute=_ksum_lanes(NDEV, ns, NT), epilogue=ep)[0]


def _col_bwd_w(name, a, dy, ns):
    T, K = a.shape
    g = _shards_per_step(ns)
    tt, tr = _tile(T, TT_DEEP), _tile(K, 1024)
    return _matmul(
        name, a, dy, (), grid=(NDEV // g, K // tr, T // tt),
        a_spec=Block((tt, tr), lambda j, n, k: (k, n)),
        b_spec=Block((tt, g * ns), lambda j, n, k: (k, j)),
        extra_specs=(),
        out_shapes=[SDS((NDEV, K, ns), BF16)],
        out_specs=[Block((g, tr, ns), lambda j, n, k: (j, n, 0))],
        acc_shape=(tr, g * ns), dims=TN, epilogue=_split_lanes(g, ns))[0]


def _gate_fwd(name, h, w, bias):
    T, K = h.shape
    ns = w.shape[2]
    g = 2
    per = D_MODEL // (g * ns)
    tm = _tile(T, 1024)

    def ep(accs, ex, outs):
        outs[0][...] = (accs[0] + ex[0][...]).astype(BF16)

    return _matmul(
        name, h, w, (bias,), grid=(NDEV // g, T // tm, 1),
        a_spec=Block((tm, K), lambda j, i, k: (i, 0)),
        b_spec=Block((g, K, ns), lambda j, i, k: (j, 0, 0)),
        extra_specs=[Block((None, 1, g * ns), lambda j, i, k: (j // per, 0, j % per))],
        out_shapes=[SDS((2, T, D_MODEL), BF16)],
        out_specs=[Block((None, tm, g * ns), lambda j, i, k: (j // per, i, j % per))],
        acc_shape=(tm, g * ns), compute=_wide_b(g), epilogue=ep)[0]


def _gate_bwd_a(name, dgl, w):
    _, T, _ = dgl.shape
    _, K, ns = w.shape
    per = D_MODEL // ns
    tm, tn = _tile(T, 1024), 1024

    def compute(a_ref, b_ref, _):
        part = None
        for q in range(NDEV):
            d = _dot(a_ref[q // per, :, (q % per) * ns:(q % per + 1) * ns], b_ref[q], NT)
            part = d if part is None else part + d
        return part

    return _matmul(
        name, dgl, w, (), grid=(T // tm, K // tn, 1),
        a_spec=Block((2, tm, D_MODEL), lambda i, j, k: (0, i, 0)),
        b_spec=Block((NDEV, tn, ns), lambda i, j, k: (0, j, 0)),
        extra_specs=(),
        out_shapes=[SDS((T, K), BF16)],
        out_specs=[Block((tm, tn), lambda i, j, k: (i, j))],
        acc_shape=(tm, tn), compute=compute, epilogue=_store())[0]


def _gate_bwd_w(name, h, dgl, ns):
    T, K = h.shape
    g = 2
    per = D_MODEL // (g * ns)
    tt, tr = _tile(T, TT_DEEP), 1024
    return _matmul(
        name, h, dgl, (), grid=(NDEV // g, K // tr, T // tt),
        a_spec=Block((tt, tr), lambda j, n, k: (k, n)),
        b_spec=Block((None, tt, g * ns), lambda j, n, k: (j // per, k, j % per)),
        extra_specs=(),
        out_shapes=[SDS((NDEV, K, ns), BF16)],
        out_specs=[Block((g, tr, ns), lambda j, n, k: (j, n, 0))],
        acc_shape=(tr, g * ns), dims=TN, epilogue=_split_lanes(g, ns))[0]


def _plain_fwd_res(name, a, w, xres):
    T, K = a.shape
    N = w.shape[1]
    tm, tn = _tile(T, 1024), _tile(N, 1024)

    def ep(accs, ex, outs):
        outs[0][...] = ex[0][...] + accs[0]

    return _matmul(
        name, a, w, (xres,), grid=(T // tm, N // tn, 1),
        a_spec=Block((tm, K), lambda i, j, k: (i, 0)),
        b_spec=Block((K, tn), lambda i, j, k: (0, j)),
        extra_specs=[Block((tm, tn), lambda i, j, k: (i, j))],
        out_shapes=[SDS((T, N), F32)],
        out_specs=[Block((tm, tn), lambda i, j, k: (i, j))],
        acc_shape=(tm, tn), dims=NN, nb=None, epilogue=ep)[0]


def _plain_bwd_a(name, dy, w):
    T, N = dy.shape
    K = w.shape[0]
    tm, tn = _tile(T, 1024), _tile(K, 1024)
    return _matmul(
        name, dy, w, (), grid=(T // tm, K // tn, 1),
        a_spec=Block((tm, N), lambda i, j, k: (i, 0)),
        b_spec=Block((tn, N), lambda i, j, k: (j, 0)),
        extra_specs=(),
        out_shapes=[SDS((T, K), BF16)],
        out_specs=[Block((tm, tn), lambda i, j, k: (i, j))],
        acc_shape=(tm, tn), dims=NT, nb=None, epilogue=_store())[0]


def _plain_bwd_w(name, a, dy):
    T, K = a.shape
    N = dy.shape[1]
    tt, tr, tn = _tile(T, TT_DEEP), _tile(K, 1024), _tile(N, 1024)
    return _matmul(
        name, a, dy, (), grid=(K // tr, N // tn, T // tt),
        a_spec=Block((tt, tr), lambda m, n, k: (k, m)),
        b_spec=Block((tt, tn), lambda m, n, k: (k, n)),
        extra_specs=(),
        out_shapes=[SDS((K, N), BF16)],
        out_specs=[Block((tr, tn), lambda m, n, k: (m, n))],
        acc_shape=(tr, tn), dims=TN, nb=None, epilogue=_store())[0]


def _rms_fwd(name, x, g):
    T, D = x.shape
    tm = _tile(T, 512)

    def body(x_ref, g_ref, h_ref):
        xv = x_ref[...]
        r = lax.rsqrt(jnp.mean(xv * xv, axis=-1, keepdims=True) + NORM_EPS)
        h_ref[...] = (xv * r * g_ref[...]).astype(BF16)

    return pl.pallas_call(
        body, name=name, grid=(T // tm,),
        in_specs=[Block((tm, D), lambda i: (i, 0)), Block((1, D), lambda i: (0, 0))],
        out_specs=Block((tm, D), lambda i: (i, 0)), out_shape=SDS((T, D), BF16),
        compiler_params=_cparams(("arbitrary",), VMEM_BIG))(x, g)


def _rms_bwd(name, dh, x, g, dxin, out_dtype):
    T, D = x.shape
    tm = _tile(T, 512)

    def body(dh_ref, x_ref, g_ref, dxin_ref, dx_ref, dg_ref):
        i = pl.program_id(0)
        xv = x_ref[...]
        dh = dh_ref[...].astype(F32)
        r = lax.rsqrt(jnp.mean(xv * xv, axis=-1, keepdims=True) + NORM_EPS)
        xh = xv * r
        gd = dh * g_ref[...]
        dx = dxin_ref[...].astype(F32) + r * (gd - xh * jnp.mean(gd * xh, axis=-1, keepdims=True))
        dx_ref[...] = dx.astype(out_dtype)
        dgp = jnp.sum(dh * xh, axis=0, keepdims=True)

        @pl.when(i == 0)
        def _():
            dg_ref[...] = dgp

        @pl.when(i > 0)
        def _():
            dg_ref[...] += dgp

    row = Block((tm, D), lambda i: (i, 0))
    vec = Block((1, D), lambda i: (0, 0))
    return pl.pallas_call(
        body, name=name, grid=(T // tm,),
        in_specs=[row, row, vec, row], out_specs=[row, vec],
        out_shape=[SDS((T, D), out_dtype), SDS((1, D), F32)],
        compiler_params=_cparams(("arbitrary",), VMEM_BIG))(dh, x, g, dxin)


def _loss_head(name, x, g, tgt):
    T, D = x.shape
    tm = _tile(T, 512)

    def body(x_ref, g_ref, t_ref, loss_ref, dxb_ref, dg_ref):
        i = pl.program_id(0)
        xv = x_ref[...]
        gv = g_ref[...]
        r = lax.rsqrt(jnp.mean(xv * xv, axis=-1, keepdims=True) + NORM_EPS)
        xh = xv * r
        err = xh * gv - t_ref[...]
        lp = 0.5 * jnp.sum(jnp.mean(err * err, axis=-1, keepdims=True), axis=0, keepdims=True)
        dout = err * (1.0 / D)
        gd = dout * gv
        dx = r * (gd - xh * jnp.mean(gd * xh, axis=-1, keepdims=True))
        dxb_ref[...] = dx.astype(BF16)
        dgp = jnp.sum(dout * xh, axis=0, keepdims=True)
        lpb = jnp.broadcast_to(lp, (1, 128))

        @pl.when(i == 0)
        def _():
            dg_ref[...] = dgp
            loss_ref[...] = lpb

        @pl.when(i > 0)
        def _():
            dg_ref[...] += dgp
            loss_ref[...] += lpb

    row = Block((tm, D), lambda i: (i, 0))
    vec = Block((1, D), lambda i: (0, 0))
    return pl.pallas_call(
        body, name=name, grid=(T // tm,),
        in_specs=[row, vec, row], out_specs=[Block((1, 128), lambda i: (0, 0)), row, vec],
        out_shape=[SDS((1, 128), F32), SDS((T, D), BF16), SDS((1, D), F32)],
        compiler_params=_cparams(("arbitrary",), VMEM_BIG))(x, g, tgt)


def _merge_fwd(name, pa, pb, gl):
    T, D = pa.shape
    tm = _tile(T, 512)

    def body(pa_ref, pb_ref, gl_ref, o_ref):
        ga = _sigmoid(gl_ref[0].astype(F32))
        gb = _sigmoid(gl_ref[1].astype(F32))
        o_ref[...] = (ga * pa_ref[...].astype(F32) + gb * pb_ref[...].astype(F32)).astype(BF16)

    row = Block((tm, D), lambda i: (i, 0))
    return pl.pallas_call(
        body, name=name, grid=(T // tm,),
        in_specs=[row, row, Block((2, tm, D), lambda i: (0, i, 0))], out_specs=row,
        out_shape=SDS((T, D), BF16), compiler_params=_cparams(("arbitrary",), VMEM_BIG))(pa, pb, gl)


def _merge_bwd(name, dm, pa, pb, gl):
    T, D = pa.shape
    tm = _tile(T, 512)

    def body(dm_ref, pa_ref, pb_ref, gl_ref, dpa_ref, dpb_ref, dgl_ref, db_ref):
        i = pl.program_id(0)
        dmv = dm_ref[...].astype(F32)
        ga = _sigmoid(gl_ref[0].astype(F32))
        gb = _sigmoid(gl_ref[1].astype(F32))
        dpa_ref[...] = (dmv * ga).astype(BF16)
        dpb_ref[...] = (dmv * gb).astype(BF16)
        dga = dmv * pa_ref[...].astype(F32) * ga * (1.0 - ga)
        dgb = dmv * pb_ref[...].astype(F32) * gb * (1.0 - gb)
        dgl_ref[0] = dga.astype(BF16)
        dgl_ref[1] = dgb.astype(BF16)
        sa = jnp.sum(dga, axis=0, keepdims=True)
        sb = jnp.sum(dgb, axis=0, keepdims=True)

        @pl.when(i == 0)
        def _():
            db_ref[0] = sa
            db_ref[1] = sb

        @pl.when(i > 0)
        def _():
            db_ref[0] += sa
            db_ref[1] += sb

    row = Block((tm, D), lambda i: (i, 0))
    two = Block((2, tm, D), lambda i: (0, i, 0))
    return pl.pallas_call(
        body, name=name, grid=(T // tm,),
        in_specs=[row, row, row, two], out_specs=[row, row, two, Block((2, 1, D), lambda i: (0, 0, 0))],
        out_shape=[SDS((T, D), BF16), SDS((T, D), BF16), SDS((2, T, D), BF16), SDS((2, 1, D), F32)],
        compiler_params=_cparams(("arbitrary",), VMEM_BIG))(dm, pa, pb, gl)


def _sgu_core(ur, vr, lng, lnb, ws_ref, bs_ref):
    tm = ur.shape[0]
    gu = _gelu(ur)
    gv = _gelu(vr)
    mu = jnp.mean(gv, axis=-1, keepdims=True)
    cen = gv - mu
    rstd = lax.rsqrt(jnp.mean(cen * cen, axis=-1, keepdims=True) + NORM_EPS)
    xhat = cen * rstd
    vn = (xhat * lng + lnb).astype(BF16)
    rows = []
    for n in range(tm // MLP_CHUNK):
        cols = []
        for h in range(SGU_HEADS):
            blk = vn[n * MLP_CHUNK:(n + 1) * MLP_CHUNK, h * 128:(h + 1) * 128]
            cols.append(_dot(ws_ref[h], blk) + bs_ref[h])
        rows.append(jnp.concatenate(cols, axis=1))
    mixed = jnp.concatenate(rows, axis=0) if len(rows) > 1 else rows[0]
    return gu, xhat, rstd, vn, mixed


def _sgu_fwd(name, proj, lng, lnb, wsm, bst):
    T = proj.shape[0]
    W = SGU_WIDTH
    tm = _tile(T, 512)

    def body(u_ref, v_ref, lng_ref, lnb_ref, ws_ref, bs_ref, o_ref):
        gu, _, _, _, mixed = _sgu_core(u_ref[...].astype(F32), v_ref[...].astype(F32), lng_ref[...], lnb_ref[...],
                                       ws_ref, bs_ref)
        o_ref[...] = (gu * mixed).astype(BF16)

    vec = Block((1, W), lambda i: (0, 0))
    return pl.pallas_call(
        body, name=name, grid=(T // tm,),
        in_specs=[Block((tm, W), lambda i: (i, 1)), Block((tm, W), lambda i: (i, 2)), vec, vec,
                  Block((SGU_HEADS, 128, 128), lambda i: (0, 0, 0)), Block((SGU_HEADS, 128, 128), lambda i: (0, 0, 0))],
        out_specs=Block((tm, W), lambda i: (i, 0)), out_shape=SDS((T, W), BF16),
        compiler_params=_cparams(("arbitrary",), VMEM_BIG))(proj, proj, lng, lnb, wsm, bst)


def _sgu_bwd(name, dyb, dua, proj, lng, lnb, wsm, wsmt, bst):
    T = proj.shape[0]
    W = SGU_WIDTH
    tm = _tile(T, 512)

    def body(dy_ref, dua_ref, u_ref, v_ref, lng_ref, lnb_ref, ws_ref, wst_ref, bs_ref,
             duv_ref, dws_ref, dbs_ref, dlng_ref, dlnb_ref):
        i = pl.program_id(0)
        duv_ref[:, :W] = dua_ref[...]
        ur = u_ref[...].astype(F32)
        vr = v_ref[...].astype(F32)
        lng_v = lng_ref[...]
        gu, xhat, rstd, vn, mixed = _sgu_core(ur, vr, lng_v, lnb_ref[...], ws_ref, bs_ref)
        dy = dy_ref[...].astype(F32)
        dgu = dy * mixed
        dmix = dy * gu
        dmb = dmix.astype(BF16)
        dws_p, dbs_p, rows = [], [], []
        for h in range(SGU_HEADS):
            acc_w = jnp.zeros((128, 128), F32)
            acc_b = jnp.zeros((128, 1), F32)
            for n in range(tm // MLP_CHUNK):
                r0 = n * MLP_CHUNK
                dmt = dmb[r0:r0 + MLP_CHUNK, h * 128:(h + 1) * 128]
                acc_w = acc_w + _dot(dmt, vn[r0:r0 + MLP_CHUNK, h * 128:(h + 1) * 128], NT)
                acc_b = acc_b + jnp.sum(dmix[r0:r0 + MLP_CHUNK, h * 128:(h + 1) * 128], axis=1, keepdims=True)
            dws_p.append(acc_w)
            dbs_p.append(jnp.broadcast_to(acc_b, (128, 128)))
        for n in range(tm // MLP_CHUNK):
            r0 = n * MLP_CHUNK
            rows.append(jnp.concatenate(
                [_dot(wst_ref[h], dmb[r0:r0 + MLP_CHUNK, h * 128:(h + 1) * 128]) for h in range(SGU_HEADS)], axis=1))
        dvn = jnp.concatenate(rows, axis=0) if len(rows) > 1 else rows[0]
        dlng_p = jnp.sum(dvn * xhat, axis=0, keepdims=True)
        dlnb_p = jnp.sum(dvn, axis=0, keepdims=True)
        dxh = dvn * lng_v
        dgv = rstd * (dxh - jnp.mean(dxh, axis=-1, keepdims=True)
                      - xhat * jnp.mean(dxh * xhat, axis=-1, keepdims=True))
        duv_ref[:, W:2 * W] = (dgu * _gelu_grad(ur)).astype(BF16)
        duv_ref[:, 2 * W:] = (dgv * _gelu_grad(vr)).astype(BF16)

        @pl.when(i == 0)
        def _():
            for h in range(SGU_HEADS):
                dws_ref[h] = dws_p[h]
                dbs_ref[h] = dbs_p[h]
            dlng_ref[...] = dlng_p
            dlnb_ref[...] = dlnb_p

        @pl.when(i > 0)
        def _():
            for h in range(SGU_HEADS):
                dws_ref[h] += dws_p[h]
                dbs_ref[h] += dbs_p[h]
            dlng_ref[...] += dlng_p
            dlnb_ref[...] += dlnb_p

    vec = Block((1, W), lambda i: (0, 0))
    wsb = Block((SGU_HEADS, 128, 128), lambda i: (0, 0, 0))
    hsq = SDS((SGU_HEADS, 128, 128), F32)
    return pl.pallas_call(
        body, name=name, grid=(T // tm,),
        in_specs=[Block((tm, W), lambda i: (i, 0)), Block((tm, W), lambda i: (i, 0)),
                  Block((tm, W), lambda i: (i, 1)), Block((tm, W), lambda i: (i, 2)),
                  vec, vec, wsb, wsb, wsb],
        out_specs=[Block((tm, 3 * W), lambda i: (i, 0)), wsb, wsb, vec, vec],
        out_shape=[SDS((T, 3 * W), BF16), hsq, hsq, SDS((1, W), F32), SDS((1, W), F32)],
        compiler_params=_cparams(("arbitrary",), VMEM_BIG))(dyb, dua, proj, proj, lng, lnb, wsm, wsmt, bst)


def _s5_disc(lr, li, ldt, brt, bit):
    dt = jnp.exp(ldt)
    decay = jnp.exp(lr * dt)
    abr = decay * jnp.cos(li * dt)
    abi = decay * jnp.sin(li * dt)
    denom = lr * lr + li * li
    nr = abr - 1.0
    ni = abi
    kr = (nr * lr + ni * li) / denom
    ki = (ni * lr - nr * li) / denom
    bkr = kr[None] * brt - ki[None] * bit
    bki = kr[None] * bit + ki[None] * brt
    return abr, abi, bkr, bki


def _s5_prep(lr, li, ldt, brt, bit):
    G, P, C = S5_GROUPS, S5_STATE, S5_GROUP_WIDTH

    def body(lr_ref, li_ref, ldt_ref, br_ref, bi_ref, abr_ref, abi_ref, pwr_ref, pwi_ref, bkr_ref, bki_ref):
        lr_, li_, ldt_ = lr_ref[...], li_ref[...], ldt_ref[...]
        res = _s5_disc(lr_, li_, ldt_, br_ref[...], bi_ref[...])
        for o, r in zip((abr_ref, abi_ref, bkr_ref, bki_ref), res):
            o[...] = r
        dt = jnp.exp(ldt_)
        n = lax.broadcasted_iota(jnp.int32, (S5_SEG, G, P), 0).astype(F32) + 1.0
        dec = jnp.exp((lr_ * dt)[None] * n)
        ang = (li_ * dt)[None] * n
        pwr_ref[...] = dec * jnp.cos(ang)
        pwi_ref[...] = dec * jnp.sin(ang)

    gp = SDS((G, P), F32)
    sgp = SDS((S5_SEG, G, P), F32)
    cgp = SDS((C, G, P), F32)
    return pl.pallas_call(body, name="s5_prep", out_shape=[gp, gp, sgp, sgp, cgp, cgp])(lr, li, ldt, brt, bit)


def _s5_prep_bwd(lr, li, ldt, brt, bit, dabr, dabi, dbkr, dbki):
    G, P, C = S5_GROUPS, S5_STATE, S5_GROUP_WIDTH

    def body(lr_ref, li_ref, ldt_ref, br_ref, bi_ref, dabr_ref, dabi_ref, dbkr_ref, dbki_ref,
             o_lr, o_li, o_ldt, o_br, o_bi):
        _, pull = jax.vjp(_s5_disc, lr_ref[...], li_ref[...], ldt_ref[...], br_ref[...], bi_ref[...])
        g = pull((dabr_ref[...], dabi_ref[...], dbkr_ref[...], dbki_ref[...]))
        for o, r in zip((o_lr, o_li, o_ldt, o_br, o_bi), g):
            o[...] = r

    gp = SDS((G, P), F32)
    cgp = SDS((C, G, P), F32)
    return pl.pallas_call(body, name="s5_prep_bwd", out_shape=[gp, gp, SDS((G, 1), F32), cgp, cgp])(
        lr, li, ldt, brt, bit, dabr, dabi, dbkr, dbki)


def _s5_scan(buf_ref, ar_row, ai_row, pwr_ref, pwi_ref, carry_ref, LG, xs_ref=None, dar_ref=None, dai_ref=None):
    reverse = xs_ref is not None
    NS, SEG = S5_NS, S5_SEG
    sgn = -1.0 if reverse else 1.0
    for lg in range(NS // LG):
        cr = slice(lg * LG, (lg + 1) * LG)
        ci = slice(NS + lg * LG, NS + (lg + 1) * LG)
        ar1, ai1 = ar_row[:, cr], sgn * ai_row[:, cr]
        asr1, asi1 = pwr_ref[SEG - 1:SEG, cr], sgn * pwi_ref[SEG - 1:SEG, cr]
        ar = jnp.broadcast_to(ar1, (8, LG))
        ai = jnp.broadcast_to(ai1, (8, LG))

        def step_of(j):
            return (SEG - 1 - j) if reverse else j

        def p1(j, st):
            sr, si = st
            rows = pl.ds(pl.multiple_of(step_of(j) * 8, 8), 8)
            nr = ar * sr - ai * si + buf_ref[rows, cr]
            ni = ar * si + ai * sr + buf_ref[rows, ci]
            buf_ref[rows, cr] = nr
            buf_ref[rows, ci] = ni
            return nr, ni

        z = jnp.zeros((8, LG), F32)
        er, ei = lax.fori_loop(0, SEG, p1, (z, z), unroll=S5_UNROLL)
        c_r = carry_ref[:, cr]
        c_i = carry_ref[:, ci]
        cs_r, cs_i = [None] * 8, [None] * 8
        order = range(7, -1, -1) if reverse else range(8)
        for s in order:
            cs_r[s], cs_i[s] = c_r, c_i
            e_r, e_i = er[s:s + 1], ei[s:s + 1]
            c_r, c_i = e_r + asr1 * c_r - asi1 * c_i, e_i + asr1 * c_i + asi1 * c_r
        carry_ref[:, cr] = c_r
        carry_ref[:, ci] = c_i
        cmr = jnp.concatenate(cs_r, axis=0)
        cmi = jnp.concatenate(cs_i, axis=0)

        def carried(j):
            pr = pwr_ref[pl.ds(j, 1), cr]
            pi = sgn * pwi_ref[pl.ds(j, 1), cr]
            return pr * cmr - pi * cmi, pr * cmi + pi * cmr

        if not reverse:
            def p2(j, st):
                rows = pl.ds(pl.multiple_of(j * 8, 8), 8)
                wr, wi = carried(j)
                buf_ref[rows, cr] += wr
                buf_ref[rows, ci] += wi
                return st

            lax.fori_loop(0, SEG, p2, 0, unroll=S5_UNROLL)
        else:
            def p2(j, st):
                pr, pi, dr, di = st
                rows = pl.ds(pl.multiple_of(step_of(j) * 8, 8), 8)
                xr = xs_ref[rows, cr]
                xi = xs_ref[rows, ci]
                dr = dr + pr * xr + pi * xi
                di = di + pi * xr - pr * xi
                wr, wi = carried(j)
                gr = buf_ref[rows, cr] + wr
                gi = buf_ref[rows, ci] + wi
                buf_ref[rows, cr] = gr
                buf_ref[rows, ci] = gi
                return gr, gi, dr, di

            st = lax.fori_loop(0, SEG, p2, (cmr, cmi, z, z), unroll=S5_UNROLL)
            dar_ref[:, cr] += st[2]
            dai_ref[:, cr] += st[3]


def _s5_fwd(proj, perm, permt, bdbr, bdbi, bdcr, bdci, abr, abi, asr, asi, dvec, wglu, bglu):
    T = proj.shape[0]
    TC, NS, W = S5_TC, S5_NS, S5_WIDTH
    nc = T // TC

    def body(u_ref, pm_ref, pmt_ref, bdbr_ref, bdbi_ref, bdcr_ref, bdci_ref, ar_ref, ai_ref, asr_ref, asi_ref,
             d_ref, wglu_ref, bglu_ref, ya_ref, xs_ref, ypre_ref, carry_ref):
        i = pl.program_id(0)

        @pl.when(i == 0)
        def _():
            carry_ref[...] = jnp.zeros_like(carry_ref)

        up = _dot(pm_ref[...], u_ref[...]).astype(BF16)
        for j in range(8):
            ut = up[:, j * 128:(j + 1) * 128]
            xs_ref[:, j * 512:(j + 1) * 512] = _dot(ut, bdbr_ref[j])
            xs_ref[:, NS + j * 512:NS + (j + 1) * 512] = _dot(ut, bdbi_ref[j])
        _s5_scan(xs_ref, ar_ref[...], ai_ref[...], asr_ref, asi_ref, carry_ref, S5_LG)
        ys = []
        for j in range(8):
            xr = xs_ref[:, j * 512:(j + 1) * 512].astype(BF16)
            xi = xs_ref[:, NS + j * 512:NS + (j + 1) * 512].astype(BF16)
            ys.append(_dot(xr, bdcr_ref[j]) + _dot(xi, bdci_ref[j]))
        ypre = jnp.concatenate(ys, axis=1) + d_ref[...] * up.astype(F32)
        ypre_ref[...] = ypre
        ya = _gelu(ypre)
        zl = _dot(ya.astype(BF16), wglu_ref[...]) + bglu_ref[...]
        outp = (ya * _sigmoid(zl)).astype(BF16)
        ya_ref[...] = _dot(pmt_ref[...], outp).astype(BF16)

    return pl.pallas_call(
        body, name="s5_fwd", grid=(nc,),
        in_specs=[Block((TC, W), lambda i: (i, 0)), _const((TC, TC)), _const((TC, TC)),
                  _const((8, 128, 512)), _const((8, 128, 512)), _const((8, 512, 128)), _const((8, 512, 128)),
                  _const((1, NS)), _const((1, NS)), _const((S5_SEG, NS)), _const((S5_SEG, NS)),
                  _const((1, W)), _const((W, W)), _const((1, W))],
        out_specs=[Block((TC, W), lambda i: (i, 0)), Block((TC, 2 * NS), lambda i: (i, 0)),
                   Block((TC, W), lambda i: (i, 0))],
        out_shape=[SDS((T, W), BF16), SDS((T, 2 * NS), F32), SDS((T, W), F32)],
        scratch_shapes=[pltpu.VMEM((1, 2 * NS), F32)],
        compiler_params=_cparams(("arbitrary",), VMEM_BIG),
    )(proj, perm, permt, bdbr, bdbi, bdcr, bdci, abr, abi, asr, asi, dvec, wglu, bglu)


def _s5_bwd(dya, proj, ypre, xs, perm, permt, bdbr, bdbi, bdcr, bdci, abr, abi, asr, asi, dvec, wglu, bglu):
    T = proj.shape[0]
    TC, NS, W = S5_TC, S5_NS, S5_WIDTH
    nc = T // TC

    def body(dya_ref, u_ref, ypre_ref, xs_ref, pm_ref, pmt_ref, bdbr_ref, bdbi_ref, bdcr_ref, bdci_ref,
             ar_ref, ai_ref, asr_ref, asi_ref, d_ref, wglu_ref, bglu_ref,
             du_ref, dar_ref, dai_ref, dd_ref, dbglu_ref, o_dbdbr, o_dbdbi, o_dbdcr, o_dbdci, o_dwglu,
             g_ref, carry_ref, dbdbr_ref, dbdbi_ref, dbdcr_ref, dbdci_ref, dwglu_ref):
        i = pl.program_id(0)

        @pl.when(i == 0)
        def _():
            carry_ref[...] = jnp.zeros_like(carry_ref)
            for r in (dbdbr_ref, dbdbi_ref, dbdcr_ref, dbdci_ref, dar_ref, dai_ref, dd_ref, dwglu_ref, dbglu_ref):
                r[...] = jnp.zeros_like(r)

        pm = pm_ref[...]
        dyo = _dot(pm, dya_ref[...])
        up = _dot(pm, u_ref[...]).astype(BF16)
        upf = up.astype(F32)
        ypre_v = ypre_ref[...]
        ya = _gelu(ypre_v)
        yab = ya.astype(BF16)
        sg = _sigmoid(_dot(yab, wglu_ref[...]) + bglu_ref[...])
        dz = dyo * ya * sg * (1.0 - sg)
        dzb = dz.astype(BF16)
        dya_t = dyo * sg + _dot(dzb, wglu_ref[...], NT)
        dwglu_ref[...] += _dot(yab, dzb, TN)
        dbglu_ref[...] += jnp.sum(dz, axis=0, keepdims=True)
        dy = dya_t * _gelu_grad(ypre_v)
        dd_ref[...] += jnp.sum(dy * upf, axis=0, keepdims=True)
        dyb = dy.astype(BF16)
        for j in range(8):
            dyj = dyb[:, j * 128:(j + 1) * 128]
            g_ref[:, j * 512:(j + 1) * 512] = _dot(dyj, bdcr_ref[j], NT)
            g_ref[:, NS + j * 512:NS + (j + 1) * 512] = _dot(dyj, bdci_ref[j], NT)
            dbdcr_ref[j] += _dot(xs_ref[:, j * 512:(j + 1) * 512].astype(BF16), dyj, TN)
            dbdci_ref[j] += _dot(xs_ref[:, NS + j * 512:NS + (j + 1) * 512].astype(BF16), dyj, TN)
        _s5_scan(g_ref, ar_ref[...], ai_ref[...], asr_ref, asi_ref, carry_ref, S5_LG,
                 xs_ref=xs_ref, dar_ref=dar_ref, dai_ref=dai_ref)
        dus = []
        for j in range(8):
            ut = up[:, j * 128:(j + 1) * 128]
            gr = g_ref[:, j * 512:(j + 1) * 512].astype(BF16)
            gi = g_ref[:, NS + j * 512:NS + (j + 1) * 512].astype(BF16)
            dbdbr_ref[j] += _dot(ut, gr, TN)
            dbdbi_ref[j] += _dot(ut, gi, TN)
            dus.append(_dot(gr, bdbr_ref[j], NT) + _dot(gi, bdbi_ref[j], NT))
        dup = jnp.concatenate(dus, axis=1) + d_ref[...] * dy
        du_ref[...] = _dot(pmt_ref[...], dup.astype(BF16)).astype(BF16)

        @pl.when(i == nc - 1)
        def _():
            for src, dst in ((dbdbr_ref, o_dbdbr), (dbdbi_ref, o_dbdbi), (dbdcr_ref, o_dbdcr),
                             (dbdci_ref, o_dbdci), (dwglu_ref, o_dwglu)):
                pltpu.sync_copy(src, dst)

    c2 = lambda i: (0, 0)
    rev = lambda i: (nc - 1 - i, 0)
    return pl.pallas_call(
        body, name="s5_bwd", grid=(nc,),
        in_specs=[Block((TC, W), rev), Block((TC, W), rev), Block((TC, W), rev), Block((TC, 2 * NS), rev),
                  _const((TC, TC)), _const((TC, TC)),
                  _const((8, 128, 512)), _const((8, 128, 512)), _const((8, 512, 128)), _const((8, 512, 128)),
                  _const((1, NS)), _const((1, NS)), _const((S5_SEG, NS)), _const((S5_SEG, NS)),
                  _const((1, W)), _const((W, W)), _const((1, W))],
        out_specs=[Block((TC, W), rev), Block((8, NS), c2), Block((8, NS), c2), Block((1, W), c2), Block((1, W), c2),
                   ANY, ANY, ANY, ANY, ANY],
        out_shape=[SDS((T, W), BF16), SDS((8, NS), F32), SDS((8, NS), F32), SDS((1, W), F32), SDS((1, W), F32),
                   SDS((8, 128, 512), F32), SDS((8, 128, 512), F32),
                   SDS((8, 512, 128), F32), SDS((8, 512, 128), F32), SDS((W, W), F32)],
        scratch_shapes=[pltpu.VMEM((TC, 2 * NS), F32), pltpu.VMEM((1, 2 * NS), F32),
                        pltpu.VMEM((8, 128, 512), F32), pltpu.VMEM((8, 128, 512), F32),
                        pltpu.VMEM((8, 512, 128), F32), pltpu.VMEM((8, 512, 128), F32), pltpu.VMEM((W, W), F32)],
        compiler_params=_cparams(("arbitrary",), VMEM_BIG),
    )(dya, proj, ypre, xs, perm, permt, bdbr, bdbi, bdcr, bdci, abr, abi, asr, asi, dvec, wglu, bglu)


def _bd_b(bk_t):
    C, P = S5_GROUP_WIDTH, S5_STATE
    t = jnp.transpose(bk_t, (1, 0, 2)).reshape(8, 8, C, P)
    eye = jnp.eye(8, dtype=t.dtype)
    return (t[:, :, :, None, :] * eye[None, :, None, :, None]).reshape(8, 8 * C, 8 * P)


def _bd_b_extract(m):
    C, P = S5_GROUP_WIDTH, S5_STATE
    t = m.reshape(8, 8, C, 8, P)
    d = jnp.stack([t[:, g, :, g, :] for g in range(8)], axis=1)
    return jnp.transpose(d.reshape(S5_GROUPS, C, P), (1, 0, 2))


def _bd_c(c):
    C, P = S5_GROUP_WIDTH, S5_STATE
    t = jnp.transpose(c, (0, 2, 1)).reshape(8, 8, P, C)
    eye = jnp.eye(8, dtype=t.dtype)
    return (t[:, :, :, None, :] * eye[None, :, None, :, None]).reshape(8, 8 * P, 8 * C)


def _bd_c_extract(m):
    C, P = S5_GROUP_WIDTH, S5_STATE
    t = m.reshape(8, 8, P, 8, C)
    d = jnp.stack([t[:, g, :, g, :] for g in range(8)], axis=1)
    return jnp.transpose(d.reshape(S5_GROUPS, P, C), (0, 2, 1))


def _perm_matrix():
    r = jnp.arange(S5_TC)
    src = (r % 8) * S5_SEG + r // 8
    return (src[:, None] == jnp.arange(S5_TC)[None, :]).astype(BF16)


def _coords():
    return lax.axis_index("x"), lax.axis_index("y"), lax.axis_index("c")


def _all_gather(name, arrs):
    n = len(arrs)

    def body(*refs):
        ins, outs = refs[:n], refs[n:2 * n]
        send_sems, recv_sems, local_sems = refs[2 * n:]
        x, y, c = _coords()
        me, sibling = (x, y, c), (x, y, 1 - c)
        chips = [(1 - x, y), (x, 1 - y), (1 - x, 1 - y)]

        def slot(p):
            return 4 * p[0] + 2 * p[1] + p[2]

        def copy(a, k, block, to, src=None):
            dst = outs[a].at[slot(block)]
            return pltpu.make_async_remote_copy(
                src_ref=dst if src is None else src, dst_ref=dst,
                send_sem=send_sems.at[a * 7 + k], recv_sem=recv_sems.at[a * 7 + k],
                device_id=to, device_id_type=MESH)

        mine = [pltpu.make_async_copy(ins[a], outs[a].at[slot(me)], local_sems.at[a]) for a in range(n)]
        for m in mine:
            m.start()
        first = []
        for a in range(n):
            first.append(copy(a, 0, me, sibling, src=ins[a]))
            first += [copy(a, 1 + j, me, (*chip, c), src=ins[a]) for j, chip in enumerate(chips)]
        for cp in first:
            cp.start()
        passed = []
        for j, chip in enumerate(chips):
            for a in range(n):
                copy(a, 1 + j, (*chip, c), me).wait_recv()
                fw = copy(a, 4 + j, (*chip, c), sibling)
                fw.start()
                passed.append(fw)
        for a in range(n):
            copy(a, 0, sibling, me).wait_recv()
            for j, chip in enumerate(chips):
                copy(a, 4 + j, (*chip, 1 - c), me).wait_recv()
        for cp in first + passed:
            cp.wait_send()
        for m in mine:
            m.wait()

    return pl.pallas_call(
        body, name=name,
        in_specs=[ANY] * n, out_specs=[ANY] * n,
        out_shape=[SDS((NDEV,) + a.shape, a.dtype) for a in arrs],
        scratch_shapes=[pltpu.SemaphoreType.DMA((7 * n,)), pltpu.SemaphoreType.DMA((7 * n,)),
                        pltpu.SemaphoreType.DMA((n,))],
    )(*arrs)


HBM = pl.BlockSpec(memory_space=pltpu.HBM)
SEM = pl.BlockSpec(memory_space=pltpu.SEMAPHORE)
EFFECT = pltpu.SideEffectType.DATAFLOW_SIDE_EFFECTING


def _peers7(x, y, c):
    return [(1 - x if fx else x, 1 - y if fy else y, 1 - c if fc else c)
            for fx in (0, 1) for fy in (0, 1) for fc in (0, 1) if fx or fy or fc]


def _slot(p):
    return 4 * p[0] + 2 * p[1] + p[2]


def _split_copies(src_refs, land_refs, send_sems, recv_sems, gather, mine):
    x, y, c = _coords()
    me = (x, y, c)
    out = []
    for a, (src, land) in enumerate(zip(src_refs, land_refs)):
        for k, p in enumerate(_peers7(x, y, c)):
            s = src if gather else src.at[_slot(p)]
            out.append(pltpu.make_async_remote_copy(
                src_ref=s, dst_ref=land.at[_slot(me) if mine else _slot(p)],
                send_sem=send_sems.at[a * 7 + k], recv_sem=recv_sems.at[a * 7 + k],
                device_id=p, device_id_type=MESH))
    return out


def _own_slab(shard):
    x, y, c = _coords()
    z = lax.empty((NDEV,) + shard.shape, shard.dtype)
    return lax.dynamic_update_slice(z, shard[None], (_slot((x, y, c)),) + (0,) * shard.ndim)


def _split_start(name, srcs, lands, gather):
    n = len(srcs)

    def body(*refs):
        src_refs, land_refs = refs[:n], refs[n:2 * n]
        send_sems, recv_sems = refs[2 * n], refs[2 * n + 1]
        token = refs[-1]
        for cp in _split_copies(src_refs, land_refs, send_sems, recv_sems, gather, True):
            cp.start()
        token[...] = jnp.zeros_like(token)

    thru = [pltpu.HBM(a.shape, a.dtype) for a in list(srcs) + list(lands)]
    res = pl.pallas_call(
        body, name=name,
        out_shape=(pltpu.SemaphoreType.DMA((7 * n,)), pltpu.SemaphoreType.DMA((7 * n,)), *thru, SDS((8, 128), F32)),
        in_specs=[HBM] * (2 * n),
        out_specs=(SEM, SEM, *([HBM] * (2 * n)), pl.BlockSpec(memory_space=pltpu.VMEM)),
        input_output_aliases={i: 2 + i for i in range(2 * n)},
        compiler_params=pltpu.CompilerParams(has_side_effects=EFFECT),
    )(*[pltpu.with_memory_space_constraint(a, pltpu.HBM) for a in list(srcs) + list(lands)])
    return res[0], res[1], list(res[2:2 + n]), list(res[2 + n:2 + 2 * n]), res[-1]


def _split_wait(name, started, after, gather):
    send_sems, recv_sems, srcs, lands, _ = started
    n = len(srcs)

    def body(*refs):
        src_refs, land_refs = refs[:n], refs[n:2 * n]
        s_sems, r_sems = refs[2 * n], refs[2 * n + 1]
        for cp in _split_copies(src_refs, land_refs, s_sems, r_sems, gather, False):
            cp.wait_send()
            cp.wait_recv()

    thru = [pltpu.HBM(a.shape, a.dtype) for a in list(srcs) + list(lands)]
    res = pl.pallas_call(
        body, name=name, out_shape=tuple(thru),
        in_specs=[HBM] * (2 * n) + [SEM, SEM, ANY], out_specs=tuple([HBM] * (2 * n)),
        input_output_aliases={i: i for i in range(2 * n)},
        compiler_params=pltpu.CompilerParams(has_side_effects=EFFECT),
    )(*srcs, *lands, send_sems, recv_sems, after)
    return list(res[n:])


def _adam_math(w, g, m, v):
    m = ADAM_B1 * m + (1.0 - ADAM_B1) * g
    v = ADAM_B2 * v + (1.0 - ADAM_B2) * (g * g)
    m_hat = m / (1.0 - ADAM_B1 ** ADAM_STEP)
    v_hat = v / (1.0 - ADAM_B2 ** ADAM_STEP)
    delta = -ADAM_LR * (m_hat / (jnp.sqrt(v_hat) + ADAM_EPS) + ADAM_WD * w)
    return delta, m, v


def _adam_sharded(name, recv, sub, w, m, v):
    R, Cc = w.shape
    tr = max(t for t in range(16, R + 1, 16) if R % t == 0 and t * Cc <= 256 * 1024)

    def body(*refs):
        parts = refs[:NDEV]
        w_ref, m_ref, v_ref, g_out, d_out, m_out, v_out = refs[NDEV:]
        g = parts[0][...].astype(F32)
        for p in parts[1:]:
            g = g + p[...].astype(F32)
        delta, mn, vn = _adam_math(w_ref[...], g, m_ref[...], v_ref[...])
        g_out[...] = g
        d_out[...] = delta
        m_out[...] = mn
        v_out[...] = vn

    if sub is None:
        pspecs = [Block((None, tr, Cc), functools.partial(lambda s, i: (s, i, 0), s)) for s in range(NDEV)]
    else:
        pspecs = [Block((None, None, tr, Cc), functools.partial(lambda s, i: (s, sub, i, 0), s)) for s in range(NDEV)]
    row = Block((tr, Cc), lambda i: (i, 0))
    o = SDS((R, Cc), F32)
    return pl.pallas_call(
        body, name=name, grid=(R // tr,),
        in_specs=pspecs + [row, row, row], out_specs=[row, row, row, row], out_shape=[o, o, o, o],
        compiler_params=_cparams(("arbitrary",), VMEM_BIG))(*([recv] * NDEV), w, m, v)


def _adam_small(groups):
    n = len(groups)

    def body(*refs):
        ins, outs = refs[:4 * n], refs[4 * n:]
        for a in range(n):
            p_ref, w_ref, m_ref, v_ref = ins[4 * a:4 * a + 4]
            g = p_ref[0]
            for s in range(1, NDEV):
                g = g + p_ref[s]
            delta, mn, vn = _adam_math(w_ref[...], g, m_ref[...], v_ref[...])
            for o, r in zip(outs[4 * a:4 * a + 4], (g, delta, mn, vn)):
                o[...] = r

    flat_in = [t for grp in groups for t in grp]
    out_shape = [SDS(grp[1].shape, F32) for grp in groups for _ in range(4)]
    res = pl.pallas_call(body, name="adam_small", out_shape=out_shape,
                         compiler_params=_cparams(None, VMEM_BIG))(*flat_in)
    return [tuple(res[4 * a:4 * a + 4]) for a in range(n)]


_TINY = ["mix_norm", "s5_a_re", "s5_a_im", "s5_log_dt", "s5_d", "s5_b_glu", "sgu_ln_g", "sgu_ln_b",
         "sgu_b_s", "b_gate", "ffn2_norm", "final_norm"]
_ORDER = ["ffn1_norm", "ffn1_w_gate", "ffn1_w_up", "ffn1_w_down", "mix_norm", "w_in", "s5_a_re", "s5_a_im",
          "s5_log_dt", "s5_b_re", "s5_b_im", "s5_c_re", "s5_c_im", "s5_d", "s5_w_glu", "s5_b_glu", "sgu_ln_g",
          "sgu_ln_b", "sgu_w_s", "sgu_b_s", "w_branch_a", "w_branch_b", "w_gate", "b_gate", "w_out", "ffn2_norm",
          "ffn2_w_gate", "ffn2_w_up", "ffn2_w_down", "final_norm"]


def _step(x, tgt, W, M, V):
    T = x.shape[1]
    x0 = x[0]
    tgt0 = tgt[0]
    bf = lambda a: a.astype(BF16)

    def gather_start(name, shards):
        return _split_start(name, shards, [_own_slab(s) for s in shards], True)

    (wgu1,) = _all_gather("gather1", [jnp.stack([bf(W["ffn1_w_gate"][0].T), bf(W["ffn1_w_up"][0].T)])])

    lr_, li_ = W["s5_a_re"][0], W["s5_a_im"][0]
    ldt_ = W["s5_log_dt"][0][:, None]
    brt = jnp.transpose(W["s5_b_re"][0], (2, 0, 1))
    bit = jnp.transpose(W["s5_b_im"][0], (2, 0, 1))
    abr, abi, pwr, pwi, bkr_t, bki_t = _s5_prep(lr_, li_, ldt_, brt, bit)
    bdbr, bdbi = bf(_bd_b(bkr_t)), bf(_bd_b(bki_t))
    bdcr, bdci = bf(_bd_c(W["s5_c_re"][0])), bf(_bd_c(-W["s5_c_im"][0]))
    flat = lambda a: a.reshape(1, S5_NS)
    s5a = (_perm_matrix(), _perm_matrix().T, bdbr, bdbi, bdcr, bdci, flat(abr), flat(abi),
           pwr.reshape(S5_SEG, S5_NS), pwi.reshape(S5_SEG, S5_NS),
           W["s5_d"][0].reshape(1, S5_WIDTH))
    blk = jnp.arange(MLP_CHUNK) // CHUNK
    mask = blk[:, None] >= blk[None, :]
    wsm = jnp.where(mask[None], W["sgu_w_s"][0], 0.0)
    wsm_b, wsmt_b = bf(wsm), bf(jnp.transpose(wsm, (0, 2, 1)))
    bst = jnp.broadcast_to(W["sgu_b_s"][0][:, :, None], (SGU_HEADS, MLP_CHUNK, 128))
    bgate2 = W["b_gate"].reshape(2, 1, D_MODEL)

    h1 = _rms_fwd("rms1", x0, W["ffn1_norm"])
    dep = (wgu1[0, 0, :1, :1] * 0).astype(BF16)

    def later(a):
        return bf(a) + dep[0]

    gs2 = gather_start("gather2_start", [later(W["ffn1_w_down"][0])])
    ab1, f1 = _ffn_up("ffn1_up", h1, wgu1, gs2[4])
    (wd1,) = _split_wait("gather2_wait", gs2, f1, True)
    dep = (wd1[0, :1, :1] * 0).astype(BF16)
    gs3 = gather_start("gather3_start", [later(W["w_in"][0]), later(W["s5_w_glu"][0])])
    x1 = _ffn_down("ffn1_down", f1, wd1, x0, after=gs3[4])
    h2 = _rms_fwd("rms2", x1, W["mix_norm"])
    win, wglu = _split_wait("gather3_wait", gs3, h2, True)
    wglu = wglu.reshape(S5_WIDTH, S5_WIDTH)
    s5c = s5a + (wglu, W["s5_b_glu"])
    dep = (win[0, :1, :1] * 0).astype(BF16)
    gs4 = gather_start("gather4_start", [later(W["w_gate"][0]), later(W["w_branch_a"][0]),
                                         later(W["w_branch_b"][0]), later(W["w_out"][0])])
    proj = _col_fwd("w_in", h2, win, after=gs4[4])
    ya, xs, ypre = _s5_fwd(proj, *s5c)
    dep = (ya[:1, :1] * 0).astype(BF16)
    gs5 = gather_start("gather5_start", [jnp.stack([later(W["ffn2_w_gate"][0].T), later(W["ffn2_w_up"][0].T)])])
    yb = _sgu_fwd("sgu_fwd", proj, W["sgu_ln_g"] + gs5[4][:1, :1], W["sgu_ln_b"], wsm_b, bst)
    wgate, wba, wbb, wout = _split_wait("gather4_wait", gs4, yb, True)
    wout = wout.reshape(D_MODEL, D_MODEL)
    pa = _col_fwd("branch_a", ya, wba)
    pb = _col_fwd("branch_b", yb, wbb)
    gl = _gate_fwd("gate", h2, wgate, bgate2)
    merged = _merge_fwd("merge", pa, pb, gl)
    x2 = _plain_fwd_res("w_out", merged, wout, x1)
    h3 = _rms_fwd("rms3", x2, W["ffn2_norm"])
    (wgu2,) = _split_wait("gather5_wait", gs5, h3, True)
    dep = (wgu2[0, 0, :1, :1] * 0).astype(BF16)
    gs6 = gather_start("gather6_start", [later(W["ffn2_w_down"][0])])
    ab2, f2 = _ffn_up("ffn2_up", h3, wgu2, gs6[4])
    (wd2,) = _split_wait("gather6_wait", gs6, f2, True)
    x3 = _ffn_down("ffn2_down", f2, wd2, x2)
    loss_p, dx3b, dgf = _loss_head("loss_head", x3, W["final_norm"].reshape(1, D_MODEL), tgt0)

    def exchange_start(name, grads):
        x_, y_, c_ = _coords()
        me = _slot((x_, y_, c_))
        return _split_start(name, grads, [_own_slab(lax.dynamic_index_in_dim(g, me, 0, keepdims=False))
                                          for g in grads], False)

    dab2 = _ffn_down_bwd_act("ffn2_down_bwd_a", dx3b, wd2, ab2)
    g_wd2 = _ffn_down_bwd_w("ffn2_down_bwd_w", f2, dx3b)
    g_gu2 = _ffn_up_bwd_w("ffn2_up_bwd_w", h3, dab2)
    es1 = exchange_start("exchange1_start", [g_wd2, g_gu2])
    dh3 = _ffn_up_bwd_h("ffn2_up_bwd_h", dab2, wgu2, es1[4])
    dx2b, dg3 = _rms_bwd("rms3_bwd", dh3, x2, W["ffn2_norm"], dx3b, BF16)

    dmerged = _plain_bwd_a("w_out_bwd_a", dx2b, wout)
    g_wout = _plain_bwd_w("w_out_bwd_w", merged, dx2b)
    dpa, dpb, dgl, dbgate = _merge_bwd("merge_bwd", dmerged, pa, pb, gl)
    dya = _col_bwd_a("branch_a_bwd_a", dpa, wba)
    g_wba = _col_bwd_w("branch_a_bwd_w", ya, dpa, 256)
    dyb = _col_bwd_a("branch_b_bwd_a", dpb, wbb)
    g_wbb = _col_bwd_w("branch_b_bwd_w", yb, dpb, 256)
    dh2g = _gate_bwd_a("gate_bwd_a", dgl, wgate)
    g_wgate = _gate_bwd_w("gate_bwd_w", h2, dgl, 512)
    (dua, dar8, dai8, ddv, dbglu, dbdbr, dbdbi, dbdcr, dbdci, g_wglu) = _s5_bwd(dya, proj, ypre, xs, *s5c)
    dproj, dws, dbst, dlng, dlnb = _sgu_bwd("sgu_bwd", dyb, dua, proj, W["sgu_ln_g"], W["sgu_ln_b"],
                                            wsm_b, wsmt_b, bst)
    g_win = _col_bwd_w("w_in_bwd_w", h2, dproj, 384)
    g_wout3 = g_wout.reshape(NDEV, D_MODEL // NDEV, D_MODEL)
    g_wglu3 = g_wglu.astype(BF16).reshape(NDEV, S5_WIDTH // NDEV, S5_WIDTH)
    es2 = exchange_start("exchange2_start", [g_wout3, g_wba, g_wbb, g_wgate, g_wglu3, g_win])
    dh2 = _col_bwd_a("w_in_bwd_a", dproj, win, add=dh2g)
    dx1b, dgm = _rms_bwd("rms2_bwd", dh2, x1, W["mix_norm"] + es2[4][:1, :1], dx2b, BF16)

    dabr = jnp.sum(dar8, axis=0).reshape(S5_GROUPS, S5_STATE)
    dabi = jnp.sum(dai8, axis=0).reshape(S5_GROUPS, S5_STATE)
    d_lr, d_li, d_ldt, d_brt, d_bit = _s5_prep_bwd(lr_, li_, ldt_, brt, bit, dabr, dabi,
                                                   _bd_b_extract(dbdbr), _bd_b_extract(dbdbi))
    small_g = {
        "mix_norm": dgm, "ffn2_norm": dg3, "final_norm": dgf,
        "s5_a_re": d_lr, "s5_a_im": d_li, "s5_log_dt": d_ldt,
        "s5_d": ddv, "s5_b_glu": dbglu, "sgu_ln_g": dlng, "sgu_ln_b": dlnb,
        "sgu_b_s": dbst[:, :, 0], "b_gate": dbgate,
    }
    to_cgp = lambda a: jnp.transpose(a[0], (2, 0, 1))
    from_cgp = lambda a: jnp.transpose(a, (1, 2, 0))[None]
    natural = [
        ("s5_b_re", d_brt, to_cgp, from_cgp), ("s5_b_im", d_bit, to_cgp, from_cgp),
        ("s5_c_re", _bd_c_extract(dbdcr), lambda a: a[0], lambda a: a[None]),
        ("s5_c_im", -_bd_c_extract(dbdci), lambda a: a[0], lambda a: a[None]),
        ("sgu_w_s", jnp.where(mask[None], dws, 0.0), lambda a: a[0], lambda a: a[None]),
    ]
    sizes = [W[n].size for n in _TINY]
    total = sum(sizes) + 1
    rows = -(-total // 128)
    rows = -(-rows // 8) * 8
    pad = rows * 128 - total

    def pack(d, extra):
        return jnp.concatenate([d[n].reshape(-1).astype(F32) for n in _TINY] + [extra, jnp.zeros((pad,), F32)]
                               ).reshape(rows, 128)

    gsm = gather_start("gather_small_start", [pack(small_g, loss_p[0, :1])] + [g for _, g, _, _ in natural])

    dab1 = _ffn_down_bwd_act("ffn1_down_bwd_a", dx1b, wd1, ab1, after=gsm[4])
    g_gu1 = _ffn_up_bwd_w("ffn1_up_bwd_w", h1, dab1)
    es3 = exchange_start("exchange3_start", [g_gu1])
    g_wd1 = _ffn_down_bwd_w("ffn1_down_bwd_w", f1, dx1b, after=es3[4])
    es4 = exchange_start("exchange4_start", [g_wd1])
    dh1 = _ffn_up_bwd_h("ffn1_up_bwd_h", dab1, wgu1, es4[4])
    dx0, dg1 = _rms_bwd("rms1_bwd", dh1, x0, W["ffn1_norm"], dx1b, F32)

    G, Dl, Mn, Vn = {}, {}, {}, {}

    def adam(plan):
        last = None
        for n, recv, sub in plan:
            if sub is None:
                g, d, mn, vn = _adam_sharded("adam_" + n, recv, sub, W[n][0], M[n][0], V[n][0])
                G[n], Dl[n], Mn[n], Vn[n] = g[None], d[None], mn[None], vn[None]
            else:
                tr = jnp.transpose
                g, d, mn, vn = _adam_sharded("adam_" + n, recv, sub, tr(W[n][0]), tr(M[n][0]), tr(V[n][0]))
                G[n], Dl[n], Mn[n], Vn[n] = tr(g)[None], tr(d)[None], tr(mn)[None], tr(vn)[None]
            last = g
        return last

    r_wd2, r_gu2 = _split_wait("exchange1_wait", es1, dx0, False)
    done = adam([("ffn2_w_down", r_wd2, None), ("ffn2_w_gate", r_gu2, 0), ("ffn2_w_up", r_gu2, 1)])
    r_wout, r_wba, r_wbb, r_wgate, r_wglu, r_win = _split_wait("exchange2_wait", es2, done, False)
    done = adam([("w_out", r_wout, None), ("w_branch_a", r_wba, None), ("w_branch_b", r_wbb, None),
                 ("w_gate", r_wgate, None), ("s5_w_glu", r_wglu, None), ("w_in", r_win, None)])

    late = dg1 + 0.0 * done.reshape(-1)[:1]
    zero1 = jnp.zeros((1,), F32)
    parts = _split_wait("gather_small_wait", gsm, late, True)
    (parts_g1,) = _all_gather("gather_ffn1_norm_grad", [late])
    groups = [(parts[0], pack(W, zero1), pack(M, zero1), pack(V, zero1))]
    groups += [(parts[1 + a], view(W[n]), view(M[n]), view(V[n])) for a, (n, _, view, _) in enumerate(natural)]
    groups += [(parts_g1, W["ffn1_norm"], M["ffn1_norm"], V["ffn1_norm"])]
    res = _adam_small(groups)
    sg, sd, sm, sv = res[0]
    for (n, _, _, back), (g, d, mn, vn) in zip(natural, res[1:-1]):
        G[n], Dl[n], Mn[n], Vn[n] = back(g), back(d), back(mn), back(vn)
    G["ffn1_norm"], Dl["ffn1_norm"], Mn["ffn1_norm"], Vn["ffn1_norm"] = res[-1]

    def unpack(flat2d, into):
        flat = flat2d.reshape(-1)
        off = 0
        for n, s in zip(_TINY, sizes):
            into[n] = flat[off:off + s].reshape(W[n].shape)
            off += s
        return flat[off]

    loss = unpack(sg, G)
    unpack(sd, Dl)
    unpack(sm, Mn)
    unpack(sv, Vn)

    (r_gu1,) = _split_wait("exchange3_wait", es3, sg, False)
    done = adam([("ffn1_w_gate", r_gu1, 0), ("ffn1_w_up", r_gu1, 1)])
    (r_wd1,) = _split_wait("exchange4_wait", es4, done, False)
    adam([("ffn1_w_down", r_wd1, None)])

    return loss, dx0[None], G, Dl, Mn, Vn


def kernel(x, ffn1_norm, ffn1_w_gate, ffn1_w_up, ffn1_w_down, mix_norm, w_in, s5_a_re, s5_a_im, s5_log_dt, s5_b_re, s5_b_im, s5_c_re, s5_c_im, s5_d, s5_w_glu, s5_b_glu, sgu_ln_g, sgu_ln_b, sgu_w_s, sgu_b_s, w_branch_a, w_branch_b, w_gate, b_gate, w_out, ffn2_norm, ffn2_w_gate, ffn2_w_up, ffn2_w_down, final_norm, loss_target, m_ffn1_norm, m_ffn1_w_gate, m_ffn1_w_up, m_ffn1_w_down, m_mix_norm, m_w_in, m_s5_a_re, m_s5_a_im, m_s5_log_dt, m_s5_b_re, m_s5_b_im, m_s5_c_re, m_s5_c_im, m_s5_d, m_s5_w_glu, m_s5_b_glu, m_sgu_ln_g, m_sgu_ln_b, m_sgu_w_s, m_sgu_b_s, m_w_branch_a, m_w_branch_b, m_w_gate, m_b_gate, m_w_out, m_ffn2_norm, m_ffn2_w_gate, m_ffn2_w_up, m_ffn2_w_down, m_final_norm, v_ffn1_norm, v_ffn1_w_gate, v_ffn1_w_up, v_ffn1_w_down, v_mix_norm, v_w_in, v_s5_a_re, v_s5_a_im, v_s5_log_dt, v_s5_b_re, v_s5_b_im, v_s5_c_re, v_s5_c_im, v_s5_d, v_s5_w_glu, v_s5_b_glu, v_sgu_ln_g, v_sgu_ln_b, v_sgu_w_s, v_sgu_b_s, v_w_branch_a, v_w_branch_b, v_w_gate, v_b_gate, v_w_out, v_ffn2_norm, v_ffn2_w_gate, v_ffn2_w_up, v_ffn2_w_down, v_final_norm):
    a = locals()
    W = {n: a[n] for n in _ORDER}
    M = {n: a["m_" + n] for n in _ORDER}
    V = {n: a["v_" + n] for n in _ORDER}
    loss, gx, G, Dl, Mn, Vn = _step(x, loss_target, W, M, V)
    return (loss, gx, *[G[n] for n in _ORDER], *[Dl[n] for n in _ORDER], *[Mn[n] for n in _ORDER],
            *[Vn[n] for n in _ORDER])
```

```python
import functools
import math

import jax
import jax.numpy as jnp
from jax import lax
from jax.experimental import pallas as pl
from jax.experimental.pallas import tpu as pltpu

F32 = jnp.float32
BF16 = jnp.bfloat16
NDEV = 8
NORM_EPS = 1e-6
D_MODEL = 2048
D_FF = 5632
FF_SHARD = D_FF // NDEV
S5_WIDTH = 1024
S5_GROUPS = 64
S5_GROUP_WIDTH = 16
S5_STATE = 64
S5_NS = S5_GROUPS * S5_STATE
SGU_WIDTH = 1024
SGU_HEADS = 8
MLP_CHUNK = 128
CHUNK = 64
ADAM_LR, ADAM_B1, ADAM_B2, ADAM_EPS, ADAM_WD, ADAM_STEP = 0.001, 0.9, 0.999, 1e-08, 0.01, 10
S5_TC = 256
S5_SEG = S5_TC // 8
S5_LG = 512
S5_UNROLL = True
VMEM_BIG = 56 * 1024 * 1024

MESH = pl.DeviceIdType.MESH
SDS = jax.ShapeDtypeStruct
Block = pl.BlockSpec
ANY = pl.BlockSpec(memory_space=pl.ANY)


def _cparams(sem=None, vmem=None):
    return pltpu.CompilerParams(dimension_semantics=sem, vmem_limit_bytes=vmem)


def _const(shape):
    nd = len(shape)
    return pl.BlockSpec(shape, lambda i: (0,) * nd, pipeline_mode=pl.Buffered(1))


def _sigmoid(x):
    return 0.5 * jnp.tanh(0.5 * x) + 0.5


_GELU_C = math.sqrt(2.0 / math.pi)


def _gelu(x):
    return 0.5 * x * (1.0 + jnp.tanh(_GELU_C * (x + 0.044715 * x * x * x)))


def _gelu_grad(x):
    t = jnp.tanh(_GELU_C * (x + 0.044715 * x * x * x))
    return 0.5 * (1.0 + t) + 0.5 * x * (1.0 - t * t) * _GELU_C * (1.0 + 3.0 * 0.044715 * x * x)


NN = (((1,), (0,)), ((), ()))
NT = (((1,), (1,)), ((), ()))
TN = (((0,), (0,)), ((), ()))


def _dot(a, b, dims=NN):
    return lax.dot_general(a, b, dims, preferred_element_type=F32)


def _matmul(name, a, b, extras, *, grid, a_spec, b_spec, extra_specs, out_shapes, out_specs, acc_shape,
            epilogue, dims=NN, nb=None, compute=None, after=None, vmem=VMEM_BIG):
    nk = grid[2]
    if after is not None:
        extras = tuple(extras) + (after,)
        extra_specs = list(extra_specs) + [Block((8, 128), lambda i, j, k: (0, 0))]
    ne, no = len(extras), len(out_shapes)
    nacc = nb or 1
    if compute is None:
        def compute(a_ref, b_ref, q):
            return _dot(a_ref[...], b_ref[q] if nb else b_ref[...], dims)

    def body(*refs):
        a_ref, b_ref = refs[0], refs[1]
        ex = refs[2:2 + ne]
        outs = refs[2 + ne:2 + ne + no]
        if nk == 1:
            epilogue([compute(a_ref, b_ref, q) for q in range(nacc)], ex, outs)
            return
        acc_ref = refs[2 + ne + no]
        k = pl.program_id(2)

        @pl.when(k == 0)
        def _():
            acc_ref[...] = jnp.zeros_like(acc_ref)

        for q in range(nacc):
            acc_ref[q] += compute(a_ref, b_ref, q)

        @pl.when(k == nk - 1)
        def _():
            epilogue([acc_ref[q] for q in range(nacc)], ex, outs)

    scratch = [] if nk == 1 else [pltpu.VMEM((nacc,) + tuple(acc_shape), F32)]
    res = pl.pallas_call(
        body, name=name, grid=grid,
        in_specs=[a_spec, b_spec] + list(extra_specs),
        out_specs=list(out_specs), out_shape=list(out_shapes), scratch_shapes=scratch,
        compiler_params=_cparams(("parallel", "parallel", "arbitrary"), vmem),
    )(a, b, *extras)
    return res


def _store(dtype_outs=None):
    def ep(accs, ex, outs):
        outs[0][...] = accs[0].astype(outs[0].dtype)
    return ep


def _tile(n, t):
    t = min(n, t)
    assert n % t == 0, (n, t)
    return t


def _ksum(kq, dims):
    def compute(a_ref, b_ref, _):
        part = _dot(a_ref[0], b_ref[0], dims)
        for q in range(1, kq):
            part = part + _dot(a_ref[q], b_ref[q], dims)
        return part
    return compute


def _ksum_lanes(kq, ns, dims):
    def compute(a_ref, b_ref, _):
        part = _dot(a_ref[:, 0:ns], b_ref[0], dims)
        for q in range(1, kq):
            part = part + _dot(a_ref[:, q * ns:(q + 1) * ns], b_ref[q], dims)
        return part
    return compute


def _wide_b(g):
    def compute(a_ref, b_ref, _):
        bw = b_ref[0] if g == 1 else jnp.concatenate([b_ref[q] for q in range(g)], axis=1)
        return _dot(a_ref[...], bw, NN)
    return compute


TT_DEEP = 2048
TT_FFN = 4096


HIDDEN = NDEV * FF_SHARD


def _ffn_up(name, h, wgu, after=None):
    T, D = h.shape
    tm = _tile(T, 1024)

    def ep(accs, ex, outs):
        a, b = accs
        outs[0][0] = a.astype(BF16)
        outs[0][1] = b.astype(BF16)
        outs[1][...] = (a * _sigmoid(a) * b).astype(BF16)

    return _matmul(
        name, wgu, h, (), after=after, grid=(NDEV, T // tm, 1),
        a_spec=Block((None, 2, FF_SHARD, D), lambda j, i, k: (j, 0, 0, 0)),
        b_spec=Block((tm, D), lambda j, i, k: (i, 0)),
        extra_specs=(),
        out_shapes=[SDS((NDEV, 2, FF_SHARD, T), BF16), SDS((NDEV, FF_SHARD, T), BF16)],
        out_specs=[Block((None, 2, FF_SHARD, tm), lambda j, i, k: (j, 0, 0, i)),
                   Block((None, FF_SHARD, tm), lambda j, i, k: (j, 0, i))],
        acc_shape=(FF_SHARD, tm), nb=2,
        compute=lambda a_ref, b_ref, q: _dot(a_ref[q], b_ref[...], NT), epilogue=ep)


def _ffn_down(name, f, wd, xres, after=None):
    T = f.shape[2]
    tm, tn, tk = _tile(T, 1024), 1024, HIDDEN // 2

    def ep(accs, ex, outs):
        outs[0][...] = ex[0][...] + 0.5 * accs[0]

    return _matmul(
        name, f.reshape(HIDDEN, T), wd.reshape(HIDDEN, D_MODEL), (xres,), after=after,
        grid=(T // tm, D_MODEL // tn, HIDDEN // tk),
        a_spec=Block((tk, tm), lambda i, j, k: (k, i)),
        b_spec=Block((tk, tn), lambda i, j, k: (k, j)),
        extra_specs=[Block((tm, tn), lambda i, j, k: (i, j))],
        out_shapes=[SDS((T, D_MODEL), F32)],
        out_specs=[Block((tm, tn), lambda i, j, k: (i, j))],
        acc_shape=(tm, tn), dims=TN, epilogue=ep)[0]


def _ffn_down_bwd_act(name, dyb, wd, ab, after=None):
    T, D = dyb.shape
    tm, g = _tile(T, 1024), 1

    def ep(accs, ex, outs):
        for s in range(g):
            df = accs[0][s * FF_SHARD:(s + 1) * FF_SHARD, :].astype(BF16)
            a = ex[0][s, 0]
            b = ex[0][s, 1]
            hs = 0.5 * _sigmoid(a)
            outs[0][s, 0] = df * b * hs * (1.0 + a * (1.0 - 2.0 * hs))
            outs[0][s, 1] = df * a * hs

    blk = Block((g, 2, FF_SHARD, tm), lambda i, j, k: (j, 0, 0, i))
    return _matmul(
        name, wd.reshape(HIDDEN, D), dyb, (ab,), after=after, grid=(T // tm, NDEV // g, 1),
        a_spec=Block((g * FF_SHARD, D), lambda i, j, k: (j, 0)),
        b_spec=Block((tm, D), lambda i, j, k: (i, 0)),
        extra_specs=[blk],
        out_shapes=[SDS((NDEV, 2, FF_SHARD, T), BF16)],
        out_specs=[blk],
        acc_shape=(g * FF_SHARD, tm), dims=NT, epilogue=ep)[0]


def _ffn_down_bwd_w(name, f, dyb, after=None):
    T = f.shape[2]
    tt, tn, tr = _tile(T, TT_DEEP), 1024, 2 * FF_SHARD

    def ep(accs, ex, outs):
        outs[0][...] = (0.5 * accs[0]).astype(BF16)

    return _matmul(
        name, f.reshape(HIDDEN, T), dyb, (), after=after, grid=(HIDDEN // tr, D_MODEL // tn, T // tt),
        a_spec=Block((tr, tt), lambda j, n, k: (j, k)),
        b_spec=Block((tt, tn), lambda j, n, k: (k, n)),
        extra_specs=(),
        out_shapes=[SDS((HIDDEN, D_MODEL), BF16)],
        out_specs=[Block((tr, tn), lambda j, n, k: (j, n))],
        acc_shape=(tr, tn), dims=NN, epilogue=ep)[0].reshape(NDEV, FF_SHARD, D_MODEL)


def _ffn_up_bwd_h(name, dab, wgu, after):
    T = dab.shape[3]
    tm, tn, tk = _tile(T, 1024), 1024, HIDDEN // 2
    return _matmul(
        name, dab.reshape(2 * HIDDEN, T), wgu.reshape(2 * HIDDEN, D_MODEL), (), after=after,
        grid=(T // tm, D_MODEL // tn, 2 * HIDDEN // tk),
        a_spec=Block((tk, tm), lambda i, j, k: (k, i)),
        b_spec=Block((tk, tn), lambda i, j, k: (k, j)),
        extra_specs=(),
        out_shapes=[SDS((T, D_MODEL), BF16)],
        out_specs=[Block((tm, tn), lambda i, j, k: (i, j))],
        acc_shape=(tm, tn), dims=TN, epilogue=_store())[0]


def _ffn_up_bwd_w(name, h, dab):
    T, D = h.shape
    tt, tn = _tile(T, TT_FFN), 1024

    def ep(accs, ex, outs):
        outs[0][0] = accs[0].astype(BF16)
        outs[0][1] = accs[1].astype(BF16)

    return _matmul(
        name, dab, h, (), grid=(NDEV, D // tn, T // tt),
        a_spec=Block((None, 2, FF_SHARD, tt), lambda j, n, k: (j, 0, 0, k)),
        b_spec=Block((tt, tn), lambda j, n, k: (k, n)),
        extra_specs=(),
        out_shapes=[SDS((NDEV, 2, FF_SHARD, D), BF16)],
        out_specs=[Block((None, 2, FF_SHARD, tn), lambda j, n, k: (j, 0, 0, n))],
        acc_shape=(FF_SHARD, tn), nb=2,
        compute=lambda a_ref, b_ref, q: _dot(a_ref[q], b_ref[...], NN), epilogue=ep)[0]


def _shards_per_step(ns):
    return max(g for g in (1, 2, 4, 8) if g * ns <= 2048)


def _split_lanes(g, ns):
    def ep(accs, ex, outs):
        for q in range(g):
            outs[0][q] = accs[0][:, q * ns:(q + 1) * ns].astype(outs[0].dtype)
    return ep


def _col_fwd(name, a, w, out_dtype=BF16, after=None):
    T, K = a.shape
    ns = w.shape[2]
    g = _shards_per_step(ns)
    tm = _tile(T, 1024)
    return _matmul(
        name, a, w, (), after=after, grid=(NDEV // g, T // tm, 1),
        a_spec=Block((tm, K), lambda j, i, k: (i, 0)),
        b_spec=Block((g, K, ns), lambda j, i, k: (j, 0, 0)),
        extra_specs=(),
        out_shapes=[SDS((T, NDEV * ns), out_dtype)],
        out_specs=[Block((tm, g * ns), lambda j, i, k: (i, j))],
        acc_shape=(tm, g * ns), compute=_wide_b(g), epilogue=_store())[0]


def _col_bwd_a(name, dy, w, add=None):
    T = dy.shape[0]
    _, K, ns = w.shape
    tm, tn = _tile(T, 1024), _tile(K, 1024)

    def ep(accs, ex, outs):
        r = accs[0]
        if add is not None:
            r = r + ex[0][...].astype(F32)
        outs[0][...] = r.astype(BF16)

    extras = () if add is None else (add,)
    return _matmul(
        name, dy, w, extras, grid=(T // tm, K // tn, 1),
        a_spec=Block((tm, NDEV * ns), lambda i, j, k: (i, 0)),
        b_spec=Block((NDEV, tn, ns), lambda i, j, k: (0, j, 0)),
        extra_specs=[Block((tm, tn), lambda i, j, k: (i, j))] * len(extras),
        out_shapes=[SDS((T, K), BF16)],
        out_specs=[Block((tm, tn), lambda i, j, k: (i, j))],
        acc_shape=(tm, tn), compute=_ksum_lanes(NDEV, ns, NT), epilogue=ep)[0]


def _col_bwd_w(name, a, dy, ns):
    T, K = a.shape
    g = _shards_per_step(ns)
    tt, tr = _tile(T, TT_DEEP), _tile(K, 1024)
    return _matmul(
        name, a, dy, (), grid=(NDEV // g, K // tr, T // tt),
        a_spec=Block((tt, tr), lambda j, n, k: (k, n)),
        b_spec=Block((tt, g * ns), lambda j, n, k: (k, j)),
        extra_specs=(),
        out_shapes=[SDS((NDEV, K, ns), BF16)],
        out_specs=[Block((g, tr, ns), lambda j, n, k: (j, n, 0))],
        acc_shape=(tr, g * ns), dims=TN, epilogue=_split_lanes(g, ns))[0]


def _gate_fwd(name, h, w, bias):
    T, K = h.shape
    ns = w.shape[2]
    g = 2
    per = D_MODEL // (g * ns)
    tm = _tile(T, 1024)

    def ep(accs, ex, outs):
        outs[0][...] = (accs[0] + ex[0][...]).astype(BF16)

    return _matmul(
        name, h, w, (bias,), grid=(NDEV // g, T // tm, 1),
        a_spec=Block((tm, K), lambda j, i, k: (i, 0)),
        b_spec=Block((g, K, ns), lambda j, i, k: (j, 0, 0)),
        extra_specs=[Block((None, 1, g * ns), lambda j, i, k: (j // per, 0, j % per))],
        out_shapes=[SDS((2, T, D_MODEL), BF16)],
        out_specs=[Block((None, tm, g * ns), lambda j, i, k: (j // per, i, j % per))],
        acc_shape=(tm, g * ns), compute=_wide_b(g), epilogue=ep)[0]


def _gate_bwd_a(name, dgl, w):
    _, T, _ = dgl.shape
    _, K, ns = w.shape
    per = D_MODEL // ns
    tm, tn = _tile(T, 1024), 1024

    def compute(a_ref, b_ref, _):
        part = None
        for q in range(NDEV):
            d = _dot(a_ref[q // per, :, (q % per) * ns:(q % per + 1) * ns], b_ref[q], NT)
            part = d if part is None else part + d
        return part

    return _matmul(
        name, dgl, w, (), grid=(T // tm, K // tn, 1),
        a_spec=Block((2, tm, D_MODEL), lambda i, j, k: (0, i, 0)),
        b_spec=Block((NDEV, tn, ns), lambda i, j, k: (0, j, 0)),
        extra_specs=(),
        out_shapes=[SDS((T, K), BF16)],
        out_specs=[Block((tm, tn), lambda i, j, k: (i, j))],
        acc_shape=(tm, tn), compute=compute, epilogue=_store())[0]


def _gate_bwd_w(name, h, dgl, ns):
    T, K = h.shape
    g = 2
    per = D_MODEL // (g * ns)
    tt, tr = _tile(T, TT_DEEP), 1024
    return _matmul(
        name, h, dgl, (), grid=(NDEV // g, K // tr, T // tt),
        a_spec=Block((tt, tr), lambda j, n, k: (k, n)),
        b_spec=Block((None, tt, g * ns), lambda j, n, k: (j // per, k, j % per)),
        extra_specs=(),
        out_shapes=[SDS((NDEV, K, ns), BF16)],
        out_specs=[Block((g, tr, ns), lambda j, n, k: (j, n, 0))],
        acc_shape=(tr, g * ns), dims=TN, epilogue=_split_lanes(g, ns))[0]


def _plain_fwd_res(name, a, w, xres):
    T, K = a.shape
    N = w.shape[1]
    tm, tn = _tile(T, 1024), _tile(N, 1024)

    def ep(accs, ex, outs):
        outs[0][...] = ex[0][...] + accs[0]

    return _matmul(
        name, a, w, (xres,), grid=(T // tm, N // tn, 1),
        a_spec=Block((tm, K), lambda i, j, k: (i, 0)),
        b_spec=Block((K, tn), lambda i, j, k: (0, j)),
        extra_specs=[Block((tm, tn), lambda i, j, k: (i, j))],
        out_shapes=[SDS((T, N), F32)],
        out_specs=[Block((tm, tn), lambda i, j, k: (i, j))],
        acc_shape=(tm, tn), dims=NN, nb=None, epilogue=ep)[0]


def _plain_bwd_a(name, dy, w):
    T, N = dy.shape
    K = w.shape[0]
    tm, tn = _tile(T, 1024), _tile(K, 1024)
    return _matmul(
        name, dy, w, (), grid=(T // tm, K // tn, 1),
        a_spec=Block((tm, N), lambda i, j, k: (i, 0)),
        b_spec=Block((tn, N), lambda i, j, k: (j, 0)),
        extra_specs=(),
        out_shapes=[SDS((T, K), BF16)],
        out_specs=[Block((tm, tn), lambda i, j, k: (i, j))],
        acc_shape=(tm, tn), dims=NT, nb=None, epilogue=_store())[0]


def _plain_bwd_w(name, a, dy):
    T, K = a.shape
    N = dy.shape[1]
    tt, tr, tn = _tile(T, TT_DEEP), _tile(K, 1024), _tile(N, 1024)
    return _matmul(
        name, a, dy, (), grid=(K // tr, N // tn, T // tt),
        a_spec=Block((tt, tr), lambda m, n, k: (k, m)),
        b_spec=Block((tt, tn), lambda m, n, k: (k, n)),
        extra_specs=(),
        out_shapes=[SDS((K, N), BF16)],
        out_specs=[Block((tr, tn), lambda m, n, k: (m, n))],
        acc_shape=(tr, tn), dims=TN, nb=None, epilogue=_store())[0]


def _rms_fwd(name, x, g):
    T, D = x.shape
    tm = _tile(T, 512)

    def body(x_ref, g_ref, h_ref):
        xv = x_ref[...]
        r = lax.rsqrt(jnp.mean(xv * xv, axis=-1, keepdims=True) + NORM_EPS)
        h_ref[...] = (xv * r * g_ref[...]).astype(BF16)

    return pl.pallas_call(
        body, name=name, grid=(T // tm,),
        in_specs=[Block((tm, D), lambda i: (i, 0)), Block((1, D), lambda i: (0, 0))],
        out_specs=Block((tm, D), lambda i: (i, 0)), out_shape=SDS((T, D), BF16),
        compiler_params=_cparams(("arbitrary",), VMEM_BIG))(x, g)


def _rms_bwd(name, dh, x, g, dxin, out_dtype):
    T, D = x.shape
    tm = _tile(T, 512)

    def body(dh_ref, x_ref, g_ref, dxin_ref, dx_ref, dg_ref):
        i = pl.program_id(0)
        xv = x_ref[...]
        dh = dh_ref[...].astype(F32)
        r = lax.rsqrt(jnp.mean(xv * xv, axis=-1, keepdims=True) + NORM_EPS)
        xh = xv * r
        gd = dh * g_ref[...]
        dx = dxin_ref[...].astype(F32) + r * (gd - xh * jnp.mean(gd * xh, axis=-1, keepdims=True))
        dx_ref[...] = dx.astype(out_dtype)
        dgp = jnp.sum(dh * xh, axis=0, keepdims=True)

        @pl.when(i == 0)
        def _():
            dg_ref[...] = dgp

        @pl.when(i > 0)
        def _():
            dg_ref[...] += dgp

    row = Block((tm, D), lambda i: (i, 0))
    vec = Block((1, D), lambda i: (0, 0))
    return pl.pallas_call(
        body, name=name, grid=(T // tm,),
        in_specs=[row, row, vec, row], out_specs=[row, vec],
        out_shape=[SDS((T, D), out_dtype), SDS((1, D), F32)],
        compiler_params=_cparams(("arbitrary",), VMEM_BIG))(dh, x, g, dxin)


def _loss_head(name, x, g, tgt):
    T, D = x.shape
    tm = _tile(T, 512)

    def body(x_ref, g_ref, t_ref, loss_ref, dxb_ref, dg_ref):
        i = pl.program_id(0)
        xv = x_ref[...]
        gv = g_ref[...]
        r = lax.rsqrt(jnp.mean(xv * xv, axis=-1, keepdims=True) + NORM_EPS)
        xh = xv * r
        err = xh * gv - t_ref[...]
        lp = 0.5 * jnp.sum(jnp.mean(err * err, axis=-1, keepdims=True), axis=0, keepdims=True)
        dout = err * (1.0 / D)
        gd = dout * gv
        dx = r * (gd - xh * jnp.mean(gd * xh, axis=-1, keepdims=True))
        dxb_ref[...] = dx.astype(BF16)
        dgp = jnp.sum(dout * xh, axis=0, keepdims=True)
        lpb = jnp.broadcast_to(lp, (1, 128))

        @pl.when(i == 0)
        def _():
            dg_ref[...] = dgp
            loss_ref[...] = lpb

        @pl.when(i > 0)
        def _():
            dg_ref[...] += dgp
            loss_ref[...] += lpb

    row = Block((tm, D), lambda i: (i, 0))
    vec = Block((1, D), lambda i: (0, 0))
    return pl.pallas_call(
        body, name=name, grid=(T // tm,),
        in_specs=[row, vec, row], out_specs=[Block((1, 128), lambda i: (0, 0)), row, vec],
        out_shape=[SDS((1, 128), F32), SDS((T, D), BF16), SDS((1, D), F32)],
        compiler_params=_cparams(("arbitrary",), VMEM_BIG))(x, g, tgt)


def _merge_fwd(name, pa, pb, gl):
    T, D = pa.shape
    tm = _tile(T, 512)

    def body(pa_ref, pb_ref, gl_ref, o_ref):
        ga = _sigmoid(gl_ref[0].astype(F32))
        gb = _sigmoid(gl_ref[1].astype(F32))
        o_ref[...] = (ga * pa_ref[...].astype(F32) + gb * pb_ref[...].astype(F32)).astype(BF16)

    row = Block((tm, D), lambda i: (i, 0))
    return pl.pallas_call(
        body, name=name, grid=(T // tm,),
        in_specs=[row, row, Block((2, tm, D), lambda i: (0, i, 0))], out_specs=row,
        out_shape=SDS((T, D), BF16), compiler_params=_cparams(("arbitrary",), VMEM_BIG))(pa, pb, gl)


def _merge_bwd(name, dm, pa, pb, gl):
    T, D = pa.shape
    tm = _tile(T, 512)

    def body(dm_ref, pa_ref, pb_ref, gl_ref, dpa_ref, dpb_ref, dgl_ref, db_ref):
        i = pl.program_id(0)
        dmv = dm_ref[...].astype(F32)
        ga = _sigmoid(gl_ref[0].astype(F32))
        gb = _sigmoid(gl_ref[1].astype(F32))
        dpa_ref[...] = (dmv * ga).astype(BF16)
        dpb_ref[...] = (dmv * gb).astype(BF16)
        dga = dmv * pa_ref[...].astype(F32) * ga * (1.0 - ga)
        dgb = dmv * pb_ref[...].astype(F32) * gb * (1.0 - gb)
        dgl_ref[0] = dga.astype(BF16)
        dgl_ref[1] = dgb.astype(BF16)
        sa = jnp.sum(dga, axis=0, keepdims=True)
        sb = jnp.sum(dgb, axis=0, keepdims=True)

        @pl.when(i == 0)
        def _():
            db_ref[0] = sa
            db_ref[1] = sb

        @pl.when(i > 0)
        def _():
            db_ref[0] += sa
            db_ref[1] += sb

    row = Block((tm, D), lambda i: (i, 0))
    two = Block((2, tm, D), lambda i: (0, i, 0))
    return pl.pallas_call(
        body, name=name, grid=(T // tm,),
        in_specs=[row, row, row, two], out_specs=[row, row, two, Block((2, 1, D), lambda i: (0, 0, 0))],
        out_shape=[SDS((T, D), BF16), SDS((T, D), BF16), SDS((2, T, D), BF16), SDS((2, 1, D), F32)],
        compiler_params=_cparams(("arbitrary",), VMEM_BIG))(dm, pa, pb, gl)


def _sgu_core(ur, vr, lng, lnb, ws_ref, bs_ref):
    tm = ur.shape[0]
    gu = _gelu(ur)
    gv = _gelu(vr)
    mu = jnp.mean(gv, axis=-1, keepdims=True)
    cen = gv - mu
    rstd = lax.rsqrt(jnp.mean(cen * cen, axis=-1, keepdims=True) + NORM_EPS)
    xhat = cen * rstd
    vn = (xhat * lng + lnb).astype(BF16)
    rows = []
    for n in range(tm // MLP_CHUNK):
        cols = []
        for h in range(SGU_HEADS):
            blk = vn[n * MLP_CHUNK:(n + 1) * MLP_CHUNK, h * 128:(h + 1) * 128]
            cols.append(_dot(ws_ref[h], blk) + bs_ref[h])
        rows.append(jnp.concatenate(cols, axis=1))
    mixed = jnp.concatenate(rows, axis=0) if len(rows) > 1 else rows[0]
    return gu, xhat, rstd, vn, mixed


def _sgu_fwd(name, proj, lng, lnb, wsm, bst):
    T = proj.shape[0]
    W = SGU_WIDTH
    tm = _tile(T, 512)

    def body(u_ref, v_ref, lng_ref, lnb_ref, ws_ref, bs_ref, o_ref):
        gu, _, _, _, mixed = _sgu_core(u_ref[...].astype(F32), v_ref[...].astype(F32), lng_ref[...], lnb_ref[...],
                                       ws_ref, bs_ref)
        o_ref[...] = (gu * mixed).astype(BF16)

    vec = Block((1, W), lambda i: (0, 0))
    return pl.pallas_call(
        body, name=name, grid=(T // tm,),
        in_specs=[Block((tm, W), lambda i: (i, 1)), Block((tm, W), lambda i: (i, 2)), vec, vec,
                  Block((SGU_HEADS, 128, 128), lambda i: (0, 0, 0)), Block((SGU_HEADS, 128, 128), lambda i: (0, 0, 0))],
        out_specs=Block((tm, W), lambda i: (i, 0)), out_shape=SDS((T, W), BF16),
        compiler_params=_cparams(("arbitrary",), VMEM_BIG))(proj, proj, lng, lnb, wsm, bst)


def _sgu_bwd(name, dyb, dua, proj, lng, lnb, wsm, wsmt, bst):
    T = proj.shape[0]
    W = SGU_WIDTH
    tm = _tile(T, 512)

    def body(dy_ref, dua_ref, u_ref, v_ref, lng_ref, lnb_ref, ws_ref, wst_ref, bs_ref,
             duv_ref, dws_ref, dbs_ref, dlng_ref, dlnb_ref):
        i = pl.program_id(0)
        duv_ref[:, :W] = dua_ref[...]
        ur = u_ref[...].astype(F32)
        vr = v_ref[...].astype(F32)
        lng_v = lng_ref[...]
        gu, xhat, rstd, vn, mixed = _sgu_core(ur, vr, lng_v, lnb_ref[...], ws_ref, bs_ref)
        dy = dy_ref[...].astype(F32)
        dgu = dy * mixed
        dmix = dy * gu
        dmb = dmix.astype(BF16)
        dws_p, dbs_p, rows = [], [], []
        for h in range(SGU_HEADS):
            acc_w = jnp.zeros((128, 128), F32)
            acc_b = jnp.zeros((128, 1), F32)
            for n in range(tm // MLP_CHUNK):
                r0 = n * MLP_CHUNK
                dmt = dmb[r0:r0 + MLP_CHUNK, h * 128:(h + 1) * 128]
                acc_w = acc_w + _dot(dmt, vn[r0:r0 + MLP_CHUNK, h * 128:(h + 1) * 128], NT)
                acc_b = acc_b + jnp.sum(dmix[r0:r0 + MLP_CHUNK, h * 128:(h + 1) * 128], axis=1, keepdims=True)
            dws_p.append(acc_w)
            dbs_p.append(jnp.broadcast_to(acc_b, (128, 128)))
        for n in range(tm // MLP_CHUNK):
            r0 = n * MLP_CHUNK
            rows.append(jnp.concatenate(
                [_dot(wst_ref[h], dmb[r0:r0 + MLP_CHUNK, h * 128:(h + 1) * 128]) for h in range(SGU_HEADS)], axis=1))
        dvn = jnp.concatenate(rows, axis=0) if len(rows) > 1 else rows[0]
        dlng_p = jnp.sum(dvn * xhat, axis=0, keepdims=True)
        dlnb_p = jnp.sum(dvn, axis=0, keepdims=True)
        dxh = dvn * lng_v
        dgv = rstd * (dxh - jnp.mean(dxh, axis=-1, keepdims=True)
                      - xhat * jnp.mean(dxh * xhat, axis=-1, keepdims=True))
        duv_ref[:, W:2 * W] = (dgu * _gelu_grad(ur)).astype(BF16)
        duv_ref[:, 2 * W:] = (dgv * _gelu_grad(vr)).astype(BF16)

        @pl.when(i == 0)
        def _():
            for h in range(SGU_HEADS):
                dws_ref[h] = dws_p[h]
                dbs_ref[h] = dbs_p[h]
            dlng_ref[...] = dlng_p
            dlnb_ref[...] = dlnb_p

        @pl.when(i > 0)
        def _():
            for h in range(SGU_HEADS):
                dws_ref[h] += dws_p[h]
                dbs_ref[h] += dbs_p[h]
            dlng_ref[...] += dlng_p
            dlnb_ref[...] += dlnb_p

    vec = Block((1, W), lambda i: (0, 0))
    wsb = Block((SGU_HEADS, 128, 128), lambda i: (0, 0, 0))
    hsq = SDS((SGU_HEADS, 128, 128), F32)
    return pl.pallas_call(
        body, name=name, grid=(T // tm,),
        in_specs=[Block((tm, W), lambda i: (i, 0)), Block((tm, W), lambda i: (i, 0)),
                  Block((tm, W), lambda i: (i, 1)), Block((tm, W), lambda i: (i, 2)),
                  vec, vec, wsb, wsb, wsb],
        out_specs=[Block((tm, 3 * W), lambda i: (i, 0)), wsb, wsb, vec, vec],
        out_shape=[SDS((T, 3 * W), BF16), hsq, hsq, SDS((1, W), F32), SDS((1, W), F32)],
        compiler_params=_cparams(("arbitrary",), VMEM_BIG))(dyb, dua, proj, proj, lng, lnb, wsm, wsmt, bst)


def _s5_disc(lr, li, ldt, brt, bit):
    dt = jnp.exp(ldt)
    decay = jnp.exp(lr * dt)
    abr = decay * jnp.cos(li * dt)
    abi = decay * jnp.sin(li * dt)
    denom = lr * lr + li * li
    nr = abr - 1.0
    ni = abi
    kr = (nr * lr + ni * li) / denom
    ki = (ni * lr - nr * li) / denom
    bkr = kr[None] * brt - ki[None] * bit
    bki = kr[None] * bit + ki[None] * brt
    return abr, abi, bkr, bki


def _s5_prep(lr, li, ldt, brt, bit):
    G, P, C = S5_GROUPS, S5_STATE, S5_GROUP_WIDTH

    def body(lr_ref, li_ref, ldt_ref, br_ref, bi_ref, abr_ref, abi_ref, pwr_ref, pwi_ref, bkr_ref, bki_ref):
        lr_, li_, ldt_ = lr_ref[...], li_ref[...], ldt_ref[...]
        res = _s5_disc(lr_, li_, ldt_, br_ref[...], bi_ref[...])
        for o, r in zip((abr_ref, abi_ref, bkr_ref, bki_ref), res):
            o[...] = r
        dt = jnp.exp(ldt_)
        n = lax.broadcasted_iota(jnp.int32, (S5_SEG, G, P), 0).astype(F32) + 1.0
        dec = jnp.exp((lr_ * dt)[None] * n)
        ang = (li_ * dt)[None] * n
        pwr_ref[...] = dec * jnp.cos(ang)
        pwi_ref[...] = dec * jnp.sin(ang)

    gp = SDS((G, P), F32)
    sgp = SDS((S5_SEG, G, P), F32)
    cgp = SDS((C, G, P), F32)
    return pl.pallas_call(body, name="s5_prep", out_shape=[gp, gp, sgp, sgp, cgp, cgp])(lr, li, ldt, brt, bit)


def _s5_prep_bwd(lr, li, ldt, brt, bit, dabr, dabi, dbkr, dbki):
    G, P, C = S5_GROUPS, S5_STATE, S5_GROUP_WIDTH

    def body(lr_ref, li_ref, ldt_ref, br_ref, bi_ref, dabr_ref, dabi_ref, dbkr_ref, dbki_ref,
             o_lr, o_li, o_ldt, o_br, o_bi):
        _, pull = jax.vjp(_s5_disc, lr_ref[...], li_ref[...], ldt_ref[...], br_ref[...], bi_ref[...])
        g = pull((dabr_ref[...], dabi_ref[...], dbkr_ref[...], dbki_ref[...]))
        for o, r in zip((o_lr, o_li, o_ldt, o_br, o_bi), g):
            o[...] = r

    gp = SDS((G, P), F32)
    cgp = SDS((C, G, P), F32)
    return pl.pallas_call(body, name="s5_prep_bwd", out_shape=[gp, gp, SDS((G, 1), F32), cgp, cgp])(
        lr, li, ldt, brt, bit, dabr, dabi, dbkr, dbki)


def _s5_scan(buf_ref, ar_row, ai_row, pwr_ref, pwi_ref, carry_ref, LG, xs_ref=None, dar_ref=None, dai_ref=None):
    reverse = xs_ref is not None
    NS, SEG = S5_NS, S5_SEG
    sgn = -1.0 if reverse else 1.0
    for lg in range(NS // LG):
        cr = slice(lg * LG, (lg + 1) * LG)
        ci = slice(NS + lg * LG, NS + (lg + 1) * LG)
        ar1, ai1 = ar_row[:, cr], sgn * ai_row[:, cr]
        asr1, asi1 = pwr_ref[SEG - 1:SEG, cr], sgn * pwi_ref[SEG - 1:SEG, cr]
        ar = jnp.broadcast_to(ar1, (8, LG))
        ai = jnp.broadcast_to(ai1, (8, LG))

        def step_of(j):
            return (SEG - 1 - j) if reverse else j

        def p1(j, st):
            sr, si = st
            rows = pl.ds(pl.multiple_of(step_of(j) * 8, 8), 8)
            nr = ar * sr - ai * si + buf_ref[rows, cr]
            ni = ar * si + ai * sr + buf_ref[rows, ci]
            buf_ref[rows, cr] = nr
            buf_ref[rows, ci] = ni
            return nr, ni

        z = jnp.zeros((8, LG), F32)
        er, ei = lax.fori_loop(0, SEG, p1, (z, z), unroll=S5_UNROLL)
        c_r = carry_ref[:, cr]
        c_i = carry_ref[:, ci]
        cs_r, cs_i = [None] * 8, [None] * 8
        order = range(7, -1, -1) if reverse else range(8)
        for s in order:
            cs_r[s], cs_i[s] = c_r, c_i
            e_r, e_i = er[s:s + 1], ei[s:s + 1]
            c_r, c_i = e_r + asr1 * c_r - asi1 * c_i, e_i + asr1 * c_i + asi1 * c_r
        carry_ref[:, cr] = c_r
        carry_ref[:, ci] = c_i
        cmr = jnp.concatenate(cs_r, axis=0)
        cmi = jnp.concatenate(cs_i, axis=0)

        def carried(j):
            pr = pwr_ref[pl.ds(j, 1), cr]
            pi = sgn * pwi_ref[pl.ds(j, 1), cr]
            return pr * cmr - pi * cmi, pr * cmi + pi * cmr

        if not reverse:
            def p2(j, st):
                rows = pl.ds(pl.multiple_of(j * 8, 8), 8)
                wr, wi = carried(j)
                buf_ref[rows, cr] += wr
                buf_ref[rows, ci] += wi
                return st

            lax.fori_loop(0, SEG, p2, 0, unroll=S5_UNROLL)
        else:
            def p2(j, st):
                pr, pi, dr, di = st
                rows = pl.ds(pl.multiple_of(step_of(j) * 8, 8), 8)
                xr = xs_ref[rows, cr]
                xi = xs_ref[rows, ci]
                dr = dr + pr * xr + pi * xi
                di = di + pi * xr - pr * xi
                wr, wi = carried(j)
                gr = buf_ref[rows, cr] + wr
                gi = buf_ref[rows, ci] + wi
                buf_ref[rows, cr] = gr
                buf_ref[rows, ci] = gi
                return gr, gi, dr, di

            st = lax.fori_loop(0, SEG, p2, (cmr, cmi, z, z), unroll=S5_UNROLL)
            dar_ref[:, cr] += st[2]
            dai_ref[:, cr] += st[3]


def _s5_fwd(proj, perm, permt, bdbr, bdbi, bdcr, bdci, abr, abi, asr, asi, dvec, wglu, bglu):
    T = proj.shape[0]
    TC, NS, W = S5_TC, S5_NS, S5_WIDTH
    nc = T // TC

    def body(u_ref, pm_ref, pmt_ref, bdbr_ref, bdbi_ref, bdcr_ref, bdci_ref, ar_ref, ai_ref, asr_ref, asi_ref,
             d_ref, wglu_ref, bglu_ref, ya_ref, xs_ref, ypre_ref, carry_ref):
        i = pl.program_id(0)

        @pl.when(i == 0)
        def _():
            carry_ref[...] = jnp.zeros_like(carry_ref)

        up = _dot(pm_ref[...], u_ref[...]).astype(BF16)
        for j in range(8):
            ut = up[:, j * 128:(j + 1) * 128]
            xs_ref[:, j * 512:(j + 1) * 512] = _dot(ut, bdbr_ref[j])
            xs_ref[:, NS + j * 512:NS + (j + 1) * 512] = _dot(ut, bdbi_ref[j])
        _s5_scan(xs_ref, ar_ref[...], ai_ref[...], asr_ref, asi_ref, carry_ref, S5_LG)
        ys = []
        for j in range(8):
            xr = xs_ref[:, j * 512:(j + 1) * 512].astype(BF16)
            xi = xs_ref[:, NS + j * 512:NS + (j + 1) * 512].astype(BF16)
            ys.append(_dot(xr, bdcr_ref[j]) + _dot(xi, bdci_ref[j]))
        ypre = jnp.concatenate(ys, axis=1) + d_ref[...] * up.astype(F32)
        ypre_ref[...] = ypre
        ya = _gelu(ypre)
        zl = _dot(ya.astype(BF16), wglu_ref[...]) + bglu_ref[...]
        outp = (ya * _sigmoid(zl)).astype(BF16)
        ya_ref[...] = _dot(pmt_ref[...], outp).astype(BF16)

    return pl.pallas_call(
        body, name="s5_fwd", grid=(nc,),
        in_specs=[Block((TC, W), lambda i: (i, 0)), _const((TC, TC)), _const((TC, TC)),
                  _const((8, 128, 512)), _const((8, 128, 512)), _const((8, 512, 128)), _const((8, 512, 128)),
                  _const((1, NS)), _const((1, NS)), _const((S5_SEG, NS)), _const((S5_SEG, NS)),
                  _const((1, W)), _const((W, W)), _const((1, W))],
        out_specs=[Block((TC, W), lambda i: (i, 0)), Block((TC, 2 * NS), lambda i: (i, 0)),
                   Block((TC, W), lambda i: (i, 0))],
        out_shape=[SDS((T, W), BF16), SDS((T, 2 * NS), F32), SDS((T, W), F32)],
        scratch_shapes=[pltpu.VMEM((1, 2 * NS), F32)],
        compiler_params=_cparams(("arbitrary",), VMEM_BIG),
    )(proj, perm, permt, bdbr, bdbi, bdcr, bdci, abr, abi, asr, asi, dvec, wglu, bglu)


def _s5_bwd(dya, proj, ypre, xs, perm, permt, bdbr, bdbi, bdcr, bdci, abr, abi, asr, asi, dvec, wglu, bglu):
    T = proj.shape[0]
    TC, NS, W = S5_TC, S5_NS, S5_WIDTH
    nc = T // TC

    def body(dya_ref, u_ref, ypre_ref, xs_ref, pm_ref, pmt_ref, bdbr_ref, bdbi_ref, bdcr_ref, bdci_ref,
             ar_ref, ai_ref, asr_ref, asi_ref, d_ref, wglu_ref, bglu_ref,
             du_ref, dar_ref, dai_ref, dd_ref, dbglu_ref, o_dbdbr, o_dbdbi, o_dbdcr, o_dbdci, o_dwglu,
             g_ref, carry_ref, dbdbr_ref, dbdbi_ref, dbdcr_ref, dbdci_ref, dwglu_ref):
        i = pl.program_id(0)

        @pl.when(i == 0)
        def _():
            carry_ref[...] = jnp.zeros_like(carry_ref)
            for r in (dbdbr_ref, dbdbi_ref, dbdcr_ref, dbdci_ref, dar_ref, dai_ref, dd_ref, dwglu_ref, dbglu_ref):
                r[...] = jnp.zeros_like(r)

        pm = pm_ref[...]
        dyo = _dot(pm, dya_ref[...])
        up = _dot(pm, u_ref[...]).astype(BF16)
        upf = up.astype(F32)
        ypre_v = ypre_ref[...]
        ya = _gelu(ypre_v)
        yab = ya.astype(BF16)
        sg = _sigmoid(_dot(yab, wglu_ref[...]) + bglu_ref[...])
        dz = dyo * ya * sg * (1.0 - sg)
        dzb = dz.astype(BF16)
        dya_t = dyo * sg + _dot(dzb, wglu_ref[...], NT)
        dwglu_ref[...] += _dot(yab, dzb, TN)
        dbglu_ref[...] += jnp.sum(dz, axis=0, keepdims=True)
        dy = dya_t * _gelu_grad(ypre_v)
        dd_ref[...] += jnp.sum(dy * upf, axis=0, keepdims=True)
        dyb = dy.astype(BF16)
        for j in range(8):
            dyj = dyb[:, j * 128:(j + 1) * 128]
            g_ref[:, j * 512:(j + 1) * 512] = _dot(dyj, bdcr_ref[j], NT)
            g_ref[:, NS + j * 512:NS + (j + 1) * 512] = _dot(dyj, bdci_ref[j], NT)
            dbdcr_ref[j] += _dot(xs_ref[:, j * 512:(j + 1) * 512].astype(BF16), dyj, TN)
            dbdci_ref[j] += _dot(xs_ref[:, NS + j * 512:NS + (j + 1) * 512].astype(BF16), dyj, TN)
        _s5_scan(g_ref, ar_ref[...], ai_ref[...], asr_ref, asi_ref, carry_ref, S5_LG,
                 xs_ref=xs_ref, dar_ref=dar_ref, dai_ref=dai_ref)
        dus = []
        for j in range(8):
            ut = up[:, j * 128:(j + 1) * 128]
            gr = g_ref[:, j * 512:(j + 1) * 512].astype(BF16)
            gi = g_ref[:, NS + j * 512:NS + (j + 1) * 512].astype(BF16)
            dbdbr_ref[j] += _dot(ut, gr, TN)
            dbdbi_ref[j] += _dot(ut, gi, TN)
            dus.append(_dot(gr, bdbr_ref[j], NT) + _dot(gi, bdbi_ref[j], NT))
        dup = jnp.concatenate(dus, axis=1) + d_ref[...] * dy
        du_ref[...] = _dot(pmt_ref[...], dup.astype(BF16)).astype(BF16)

        @pl.when(i == nc - 1)
        def _():
            for src, dst in ((dbdbr_ref, o_dbdbr), (dbdbi_ref, o_dbdbi), (dbdcr_ref, o_dbdcr),
                             (dbdci_ref, o_dbdci), (dwglu_ref, o_dwglu)):
                pltpu.sync_copy(src, dst)

    c2 = lambda i: (0, 0)
    rev = lambda i: (nc - 1 - i, 0)
    return pl.pallas_call(
        body, name="s5_bwd", grid=(nc,),
        in_specs=[Block((TC, W), rev), Block((TC, W), rev), Block((TC, W), rev), Block((TC, 2 * NS), rev),
                  _const((TC, TC)), _const((TC, TC)),
                  _const((8, 128, 512)), _const((8, 128, 512)), _const((8, 512, 128)), _const((8, 512, 128)),
                  _const((1, NS)), _const((1, NS)), _const((S5_SEG, NS)), _const((S5_SEG, NS)),
                  _const((1, W)), _const((W, W)), _const((1, W))],
        out_specs=[Block((TC, W), rev), Block((8, NS), c2), Block((8, NS), c2), Block((1, W), c2), Block((1, W), c2),
                   ANY, ANY, ANY, ANY, ANY],
        out_shape=[SDS((T, W), BF16), SDS((8, NS), F32), SDS((8, NS), F32), SDS((1, W), F32), SDS((1, W), F32),
                   SDS((8, 128, 512), F32), SDS((8, 128, 512), F32),
                   SDS((8, 512, 128), F32), SDS((8, 512, 128), F32), SDS((W, W), F32)],
        scratch_shapes=[pltpu.VMEM((TC, 2 * NS), F32), pltpu.VMEM((1, 2 * NS), F32),
                        pltpu.VMEM((8, 128, 512), F32), pltpu.VMEM((8, 128, 512), F32),
                        pltpu.VMEM((8, 512, 128), F32), pltpu.VMEM((8, 512, 128), F32), pltpu.VMEM((W, W), F32)],
        compiler_params=_cparams(("arbitrary",), VMEM_BIG),
    )(dya, proj, ypre, xs, perm, permt, bdbr, bdbi, bdcr, bdci, abr, abi, asr, asi, dvec, wglu, bglu)


def _bd_b(bk_t):
    C, P = S5_GROUP_WIDTH, S5_STATE
    t = jnp.transpose(bk_t, (1, 0, 2)).reshape(8, 8, C, P)
    eye = jnp.eye(8, dtype=t.dtype)
    return (t[:, :, :, None, :] * eye[None, :, None, :, None]).reshape(8, 8 * C, 8 * P)


def _bd_b_extract(m):
    C, P = S5_GROUP_WIDTH, S5_STATE
    t = m.reshape(8, 8, C, 8, P)
    d = jnp.stack([t[:, g, :, g, :] for g in range(8)], axis=1)
    return jnp.transpose(d.reshape(S5_GROUPS, C, P), (1, 0, 2))


def _bd_c(c):
    C, P = S5_GROUP_WIDTH, S5_STATE
    t = jnp.transpose(c, (0, 2, 1)).reshape(8, 8, P, C)
    eye = jnp.eye(8, dtype=t.dtype)
    return (t[:, :, :, None, :] * eye[None, :, None, :, None]).reshape(8, 8 * P, 8 * C)


def _bd_c_extract(m):
    C, P = S5_GROUP_WIDTH, S5_STATE
    t = m.reshape(8, 8, P, 8, C)
    d = jnp.stack([t[:, g, :, g, :] for g in range(8)], axis=1)
    return jnp.transpose(d.reshape(S5_GROUPS, P, C), (0, 2, 1))


def _perm_matrix():
    r = jnp.arange(S5_TC)
    src = (r % 8) * S5_SEG + r // 8
    return (src[:, None] == jnp.arange(S5_TC)[None, :]).astype(BF16)


def _coords():
    return lax.axis_index("x"), lax.axis_index("y"), lax.axis_index("c")


def _all_gather(name, arrs):
    n = len(arrs)

    def body(*refs):
        ins, outs = refs[:n], refs[n:2 * n]
        send_sems, recv_sems, local_sems = refs[2 * n:]
        x, y, c = _coords()
        me, sibling = (x, y, c), (x, y, 1 - c)
        chips = [(1 - x, y), (x, 1 - y), (1 - x, 1 - y)]

        def slot(p):
            return 4 * p[0] + 2 * p[1] + p[2]

        def copy(a, k, block, to, src=None):
            dst = outs[a].at[slot(block)]
            return pltpu.make_async_remote_copy(
                src_ref=dst if src is None else src, dst_ref=dst,
                send_sem=send_sems.at[a * 7 + k], recv_sem=recv_sems.at[a * 7 + k],
                device_id=to, device_id_type=MESH)

        mine = [pltpu.make_async_copy(ins[a], outs[a].at[slot(me)], local_sems.at[a]) for a in range(n)]
        for m in mine:
            m.start()
        first = []
        for a in range(n):
            first.append(copy(a, 0, me, sibling, src=ins[a]))
            first += [copy(a, 1 + j, me, (*chip, c), src=ins[a]) for j, chip in enumerate(chips)]
        for cp in first:
            cp.start()
        passed = []
        for j, chip in enumerate(chips):
            for a in range(n):
                copy(a, 1 + j, (*chip, c), me).wait_recv()
                fw = copy(a, 4 + j, (*chip, c), sibling)
                fw.start()
                passed.append(fw)
        for a in range(n):
            copy(a, 0, sibling, me).wait_recv()
            for j, chip in enumerate(chips):
                copy(a, 4 + j, (*chip, 1 - c), me).wait_recv()
        for cp in first + passed:
            cp.wait_send()
        for m in mine:
            m.wait()

    return pl.pallas_call(
        body, name=name,
        in_specs=[ANY] * n, out_specs=[ANY] * n,
        out_shape=[SDS((NDEV,) + a.shape, a.dtype) for a in arrs],
        scratch_shapes=[pltpu.SemaphoreType.DMA((7 * n,)), pltpu.SemaphoreType.DMA((7 * n,)),
                        pltpu.SemaphoreType.DMA((n,))],
    )(*arrs)


HBM = pl.BlockSpec(memory_space=pltpu.HBM)
SEM = pl.BlockSpec(memory_space=pltpu.SEMAPHORE)
EFFECT = pltpu.SideEffectType.DATAFLOW_SIDE_EFFECTING


def _peers7(x, y, c):
    return [(1 - x if fx else x, 1 - y if fy else y, 1 - c if fc else c)
            for fx in (0, 1) for fy in (0, 1) for fc in (0, 1) if fx or fy or fc]


def _slot(p):
    return 4 * p[0] + 2 * p[1] + p[2]


def _split_copies(src_refs, land_refs, send_sems, recv_sems, gather, mine):
    x, y, c = _coords()
    me = (x, y, c)
    out = []
    for a, (src, land) in enumerate(zip(src_refs, land_refs)):
        for k, p in enumerate(_peers7(x, y, c)):
            s = src if gather else src.at[_slot(p)]
            out.append(pltpu.make_async_remote_copy(
                src_ref=s, dst_ref=land.at[_slot(me) if mine else _slot(p)],
                send_sem=send_sems.at[a * 7 + k], recv_sem=recv_sems.at[a * 7 + k],
                device_id=p, device_id_type=MESH))
    return out


def _own_slab(shard):
    x, y, c = _coords()
    z = lax.empty((NDEV,) + shard.shape, shard.dtype)
    return lax.dynamic_update_slice(z, shard[None], (_slot((x, y, c)),) + (0,) * shard.ndim)


def _split_start(name, srcs, lands, gather):
    n = len(srcs)

    def body(*refs):
        src_refs, land_refs = refs[:n], refs[n:2 * n]
        send_sems, recv_sems = refs[2 * n], refs[2 * n + 1]
        token = refs[-1]
        for cp in _split_copies(src_refs, land_refs, send_sems, recv_sems, gather, True):
            cp.start()
        token[...] = jnp.zeros_like(token)

    thru = [pltpu.HBM(a.shape, a.dtype) for a in list(srcs) + list(lands)]
    res = pl.pallas_call(
        body, name=name,
        out_shape=(pltpu.SemaphoreType.DMA((7 * n,)), pltpu.SemaphoreType.DMA((7 * n,)), *thru, SDS((8, 128), F32)),
        in_specs=[HBM] * (2 * n),
        out_specs=(SEM, SEM, *([HBM] * (2 * n)), pl.BlockSpec(memory_space=pltpu.VMEM)),
        input_output_aliases={i: 2 + i for i in range(2 * n)},
        compiler_params=pltpu.CompilerParams(has_side_effects=EFFECT),
    )(*[pltpu.with_memory_space_constraint(a, pltpu.HBM) for a in list(srcs) + list(lands)])
    return res[0], res[1], list(res[2:2 + n]), list(res[2 + n:2 + 2 * n]), res[-1]


def _split_wait(name, started, after, gather):
    send_sems, recv_sems, srcs, lands, _ = started
    n = len(srcs)

    def body(*refs):
        src_refs, land_refs = refs[:n], refs[n:2 * n]
        s_sems, r_sems = refs[2 * n], refs[2 * n + 1]
        for cp in _split_copies(src_refs, land_refs, s_sems, r_sems, gather, False):
            cp.wait_send()
            cp.wait_recv()

    thru = [pltpu.HBM(a.shape, a.dtype) for a in list(srcs) + list(lands)]
    res = pl.pallas_call(
        body, name=name, out_shape=tuple(thru),
        in_specs=[HBM] * (2 * n) + [SEM, SEM, ANY], out_specs=tuple([HBM] * (2 * n)),
        input_output_aliases={i: i for i in range(2 * n)},
        compiler_params=pltpu.CompilerParams(has_side_effects=EFFECT),
    )(*srcs, *lands, send_sems, recv_sems, after)
    return list(res[n:])


def _adam_math(w, g, m, v):
    m = ADAM_B1 * m + (1.0 - ADAM_B1) * g
    v = ADAM_B2 * v + (1.0 - ADAM_B2) * (g * g)
    m_hat = m / (1.0 - ADAM_B1 ** ADAM_STEP)
    v_hat = v / (1.0 - ADAM_B2 ** ADAM_STEP)
    delta = -ADAM_LR * (m_hat / (jnp.sqrt(v_hat) + ADAM_EPS) + ADAM_WD * w)
    return delta, m, v


def _adam_sharded(name, recv, sub, w, m, v):
    R, Cc = w.shape
    tr = max(t for t in range(16, R + 1, 16) if R % t == 0 and t * Cc <= 256 * 1024)

    def body(*refs):
        parts = refs[:NDEV]
        w_ref, m_ref, v_ref, g_out, d_out, m_out, v_out = refs[NDEV:]
        g = parts[0][...].astype(F32)
        for p in parts[1:]:
            g = g + p[...].astype(F32)
        delta, mn, vn = _adam_math(w_ref[...], g, m_ref[...], v_ref[...])
        g_out[...] = g
        d_out[...] = delta
        m_out[...] = mn
        v_out[...] = vn

    if sub is None:
        pspecs = [Block((None, tr, Cc), functools.partial(lambda s, i: (s, i, 0), s)) for s in range(NDEV)]
    else:
        pspecs = [Block((None, None, tr, Cc), functools.partial(lambda s, i: (s, sub, i, 0), s)) for s in range(NDEV)]
    row = Block((tr, Cc), lambda i: (i, 0))
    o = SDS((R, Cc), F32)
    return pl.pallas_call(
        body, name=name, grid=(R // tr,),
        in_specs=pspecs + [row, row, row], out_specs=[row, row, row, row], out_shape=[o, o, o, o],
        compiler_params=_cparams(("arbitrary",), VMEM_BIG))(*([recv] * NDEV), w, m, v)


def _adam_small(groups):
    n = len(groups)

    def body(*refs):
        ins, outs = refs[:4 * n], refs[4 * n:]
        for a in range(n):
            p_ref, w_ref, m_ref, v_ref = ins[4 * a:4 * a + 4]
            g = p_ref[0]
            for s in range(1, NDEV):
                g = g + p_ref[s]
            delta, mn, vn = _adam_math(w_ref[...], g, m_ref[...], v_ref[...])
            for o, r in zip(outs[4 * a:4 * a + 4], (g, delta, mn, vn)):
                o[...] = r

    flat_in = [t for grp in groups for t in grp]
    out_shape = [SDS(grp[1].shape, F32) for grp in groups for _ in range(4)]
    res = pl.pallas_call(body, name="adam_small", out_shape=out_shape,
                         compiler_params=_cparams(None, VMEM_BIG))(*flat_in)
    return [tuple(res[4 * a:4 * a + 4]) for a in range(n)]


_TINY = ["mix_norm", "s5_a_re", "s5_a_im", "s5_log_dt", "s5_d", "s5_b_glu", "sgu_ln_g", "sgu_ln_b",
         "sgu_b_s", "b_gate", "ffn2_norm", "final_norm"]
_ORDER = ["ffn1_norm", "ffn1_w_gate", "ffn1_w_up", "ffn1_w_down", "mix_norm", "w_in", "s5_a_re", "s5_a_im",
          "s5_log_dt", "s5_b_re", "s5_b_im", "s5_c_re", "s5_c_im", "s5_d", "s5_w_glu", "s5_b_glu", "sgu_ln_g",
          "sgu_ln_b", "sgu_w_s", "sgu_b_s", "w_branch_a", "w_branch_b", "w_gate", "b_gate", "w_out", "ffn2_norm",
          "ffn2_w_gate", "ffn2_w_up", "ffn2_w_down", "final_norm"]


def _step(x, tgt, W, M, V):
    T = x.shape[1]
    x0 = x[0]
    tgt0 = tgt[0]
    bf = lambda a: a.astype(BF16)

    def gather_start(name, shards):
        return _split_start(name, shards, [_own_slab(s) for s in shards], True)

    (wgu1,) = _all_gather("gather1", [jnp.stack([bf(W["ffn1_w_gate"][0].T), bf(W["ffn1_w_up"][0].T)])])

    lr_, li_ = W["s5_a_re"][0], W["s5_a_im"][0]
    ldt_ = W["s5_log_dt"][0][:, None]
    brt = jnp.transpose(W["s5_b_re"][0], (2, 0, 1))
    bit = jnp.transpose(W["s5_b_im"][0], (2, 0, 1))
    abr, abi, pwr, pwi, bkr_t, bki_t = _s5_prep(lr_, li_, ldt_, brt, bit)
    bdbr, bdbi = bf(_bd_b(bkr_t)), bf(_bd_b(bki_t))
    bdcr, bdci = bf(_bd_c(W["s5_c_re"][0])), bf(_bd_c(-W["s5_c_im"][0]))
    flat = lambda a: a.reshape(1, S5_NS)
    s5a = (_perm_matrix(), _perm_matrix().T, bdbr, bdbi, bdcr, bdci, flat(abr), flat(abi),
           pwr.reshape(S5_SEG, S5_NS), pwi.reshape(S5_SEG, S5_NS),
           W["s5_d"][0].reshape(1, S5_WIDTH))
    blk = jnp.arange(MLP_CHUNK) // CHUNK
    mask = blk[:, None] >= blk[None, :]
    wsm = jnp.where(mask[None], W["sgu_w_s"][0], 0.0)
    wsm_b, wsmt_b = bf(wsm), bf(jnp.transpose(wsm, (0, 2, 1)))
    bst = jnp.broadcast_to(W["sgu_b_s"][0][:, :, None], (SGU_HEADS, MLP_CHUNK, 128))
    bgate2 = W["b_gate"].reshape(2, 1, D_MODEL)

    h1 = _rms_fwd("rms1", x0, W["ffn1_norm"])
    dep = (wgu1[0, 0, :1, :1] * 0).astype(BF16)

    def later(a):
        return bf(a) + dep[0]

    gs2 = gather_start("gather2_start", [later(W["ffn1_w_down"][0])])
    ab1, f1 = _ffn_up("ffn1_up", h1, wgu1, gs2[4])
    (wd1,) = _split_wait("gather2_wait", gs2, f1, True)
    dep = (wd1[0, :1, :1] * 0).astype(BF16)
    gs3 = gather_start("gather3_start", [later(W["w_in"][0]), later(W["s5_w_glu"][0])])
    x1 = _ffn_down("ffn1_down", f1, wd1, x0, after=gs3[4])
    h2 = _rms_fwd("rms2", x1, W["mix_norm"])
    win, wglu = _split_wait("gather3_wait", gs3, h2, True)
    wglu = wglu.reshape(S5_WIDTH, S5_WIDTH)
    s5c = s5a + (wglu, W["s5_b_glu"])
    dep = (win[0, :1, :1] * 0).astype(BF16)
    gs4 = gather_start("gather4_start", [later(W["w_gate"][0]), later(W["w_branch_a"][0]),
                                         later(W["w_branch_b"][0]), later(W["w_out"][0])])
    proj = _col_fwd("w_in", h2, win, after=gs4[4])
    ya, xs, ypre = _s5_fwd(proj, *s5c)
    dep = (ya[:1, :1] * 0).astype(BF16)
    gs5 = gather_start("gather5_start", [jnp.stack([later(W["ffn2_w_gate"][0].T), later(W["ffn2_w_up"][0].T)])])
    yb = _sgu_fwd("sgu_fwd", proj, W["sgu_ln_g"] + gs5[4][:1, :1], W["sgu_ln_b"], wsm_b, bst)
    wgate, wba, wbb, wout = _split_wait("gather4_wait", gs4, yb, True)
    wout = wout.reshape(D_MODEL, D_MODEL)
    pa = _col_fwd("branch_a", ya, wba)
    pb = _col_fwd("branch_b", yb, wbb)
    gl = _gate_fwd("gate", h2, wgate, bgate2)
    merged = _merge_fwd("merge", pa, pb, gl)
    x2 = _plain_fwd_res("w_out", merged, wout, x1)
    h3 = _rms_fwd("rms3", x2, W["ffn2_norm"])
    (wgu2,) = _split_wait("gather5_wait", gs5, h3, True)
    dep = (wgu2[0, 0, :1, :1] * 0).astype(BF16)
    gs6 = gather_start("gather6_start", [later(W["ffn2_w_down"][0])])
    ab2, f2 = _ffn_up("ffn2_up", h3, wgu2, gs6[4])
    (wd2,) = _split_wait("gather6_wait", gs6, f2, True)
    x3 = _ffn_down("ffn2_down", f2, wd2, x2)
    loss_p, dx3b, dgf = _loss_head("loss_head", x3, W["final_norm"].reshape(1, D_MODEL), tgt0)

    def exchange_start(name, grads):
        x_, y_, c_ = _coords()
        me = _slot((x_, y_, c_))
        return _split_start(name, grads, [_own_slab(lax.dynamic_index_in_dim(g, me, 0, keepdims=False))
                                          for g in grads], False)

    dab2 = _ffn_down_bwd_act("ffn2_down_bwd_a", dx3b, wd2, ab2)
    g_wd2 = _ffn_down_bwd_w("ffn2_down_bwd_w", f2, dx3b)
    g_gu2 = _ffn_up_bwd_w("ffn2_up_bwd_w", h3, dab2)
    es1 = exchange_start("exchange1_start", [g_wd2, g_gu2])
    dh3 = _ffn_up_bwd_h("ffn2_up_bwd_h", dab2, wgu2, es1[4])
    dx2b, dg3 = _rms_bwd("rms3_bwd", dh3, x2, W["ffn2_norm"], dx3b, BF16)

    dmerged = _plain_bwd_a("w_out_bwd_a", dx2b, wout)
    g_wout = _plain_bwd_w("w_out_bwd_w", merged, dx2b)
    dpa, dpb, dgl, dbgate = _merge_bwd("merge_bwd", dmerged, pa, pb, gl)
    dya = _col_bwd_a("branch_a_bwd_a", dpa, wba)
    g_wba = _col_bwd_w("branch_a_bwd_w", ya, dpa, 256)
    dyb = _col_bwd_a("branch_b_bwd_a", dpb, wbb)
    g_wbb = _col_bwd_w("branch_b_bwd_w", yb, dpb, 256)
    dh2g = _gate_bwd_a("gate_bwd_a", dgl, wgate)
    g_wgate = _gate_bwd_w("gate_bwd_w", h2, dgl, 512)
    (dua, dar8, dai8, ddv, dbglu, dbdbr, dbdbi, dbdcr, dbdci, g_wglu) = _s5_bwd(dya, proj, ypre, xs, *s5c)
    dproj, dws, dbst, dlng, dlnb = _sgu_bwd("sgu_bwd", dyb, dua, proj, W["sgu_ln_g"], W["sgu_ln_b"],
                                            wsm_b, wsmt_b, bst)
    g_win = _col_bwd_w("w_in_bwd_w", h2, dproj, 384)
    g_wout3 = g_wout.reshape(NDEV, D_MODEL // NDEV, D_MODEL)
    g_wglu3 = g_wglu.astype(BF16).reshape(NDEV, S5_WIDTH // NDEV, S5_WIDTH)
    es2 = exchange_start("exchange2_start", [g_wout3, g_wba, g_wbb, g_wgate, g_wglu3, g_win])
    dh2 = _col_bwd_a("w_in_bwd_a", dproj, win, add=dh2g)
    dx1b, dgm = _rms_bwd("rms2_bwd", dh2, x1, W["mix_norm"] + es2[4][:1, :1], dx2b, BF16)

    dabr = jnp.sum(dar8, axis=0).reshape(S5_GROUPS, S5_STATE)
    dabi = jnp.sum(dai8, axis=0).reshape(S5_GROUPS, S5_STATE)
    d_lr, d_li, d_ldt, d_brt, d_bit = _s5_prep_bwd(lr_, li_, ldt_, brt, bit, dabr, dabi,
                                                   _bd_b_extract(dbdbr), _bd_b_extract(dbdbi))
    small_g = {
        "mix_norm": dgm, "ffn2_norm": dg3, "final_norm": dgf,
        "s5_a_re": d_lr, "s5_a_im": d_li, "s5_log_dt": d_ldt,
        "s5_d": ddv, "s5_b_glu": dbglu, "sgu_ln_g": dlng, "sgu_ln_b": dlnb,
        "sgu_b_s": dbst[:, :, 0], "b_gate": dbgate,
    }
    to_cgp = lambda a: jnp.transpose(a[0], (2, 0, 1))
    from_cgp = lambda a: jnp.transpose(a, (1, 2, 0))[None]
    natural = [
        ("s5_b_re", d_brt, to_cgp, from_cgp), ("s5_b_im", d_bit, to_cgp, from_cgp),
        ("s5_c_re", _bd_c_extract(dbdcr), lambda a: a[0], lambda a: a[None]),
        ("s5_c_im", -_bd_c_extract(dbdci), lambda a: a[0], lambda a: a[None]),
        ("sgu_w_s", jnp.where(mask[None], dws, 0.0), lambda a: a[0], lambda a: a[None]),
    ]
    sizes = [W[n].size for n in _TINY]
    total = sum(sizes) + 1
    rows = -(-total // 128)
    rows = -(-rows // 8) * 8
    pad = rows * 128 - total

    def pack(d, extra):
        return jnp.concatenate([d[n].reshape(-1).astype(F32) for n in _TINY] + [extra, jnp.zeros((pad,), F32)]
                               ).reshape(rows, 128)

    gsm = gather_start("gather_small_start", [pack(small_g, loss_p[0, :1])] + [g for _, g, _, _ in natural])

    dab1 = _ffn_down_bwd_act("ffn1_down_bwd_a", dx1b, wd1, ab1, after=gsm[4])
    g_gu1 = _ffn_up_bwd_w("ffn1_up_bwd_w", h1, dab1)
    es3 = exchange_start("exchange3_start", [g_gu1])
    g_wd1 = _ffn_down_bwd_w("ffn1_down_bwd_w", f1, dx1b, after=es3[4])
    es4 = exchange_start("exchange4_start", [g_wd1])
    dh1 = _ffn_up_bwd_h("ffn1_up_bwd_h", dab1, wgu1, es4[4])
    dx0, dg1 = _rms_bwd("rms1_bwd", dh1, x0, W["ffn1_norm"], dx1b, F32)

    G, Dl, Mn, Vn = {}, {}, {}, {}

    def adam(plan):
        last = None
        for n, recv, sub in plan:
            if sub is None:
                g, d, mn, vn = _adam_sharded("adam_" + n, recv, sub, W[n][0], M[n][0], V[n][0])
                G[n], Dl[n], Mn[n], Vn[n] = g[None], d[None], mn[None], vn[None]
            else:
                tr = jnp.transpose
                g, d, mn, vn = _adam_sharded("adam_" + n, recv, sub, tr(W[n][0]), tr(M[n][0]), tr(V[n][0]))
                G[n], Dl[n], Mn[n], Vn[n] = tr(g)[None], tr(d)[None], tr(mn)[None], tr(vn)[None]
            last = g
        return last

    r_wd2, r_gu2 = _split_wait("exchange1_wait", es1, dx0, False)
    done = adam([("ffn2_w_down", r_wd2, None), ("ffn2_w_gate", r_gu2, 0), ("ffn2_w_up", r_gu2, 1)])
    r_wout, r_wba, r_wbb, r_wgate, r_wglu, r_win = _split_wait("exchange2_wait", es2, done, False)
    done = adam([("w_out", r_wout, None), ("w_branch_a", r_wba, None), ("w_branch_b", r_wbb, None),
                 ("w_gate", r_wgate, None), ("s5_w_glu", r_wglu, None), ("w_in", r_win, None)])

    late = dg1 + 0.0 * done.reshape(-1)[:1]
    zero1 = jnp.zeros((1,), F32)
    parts = _split_wait("gather_small_wait", gsm, late, True)
    (parts_g1,) = _all_gather("gather_ffn1_norm_grad", [late])
    groups = [(parts[0], pack(W, zero1), pack(M, zero1), pack(V, zero1))]
    groups += [(parts[1 + a], view(W[n]), view(M[n]), view(V[n])) for a, (n, _, view, _) in enumerate(natural)]
    groups += [(parts_g1, W["ffn1_norm"], M["ffn1_norm"], V["ffn1_norm"])]
    res = _adam_small(groups)
    sg, sd, sm, sv = res[0]
    for (n, _, _, back), (g, d, mn, vn) in zip(natural, res[1:-1]):
        G[n], Dl[n], Mn[n], Vn[n] = back(g), back(d), back(mn), back(vn)
    G["ffn1_norm"], Dl["ffn1_norm"], Mn["ffn1_norm"], Vn["ffn1_norm"] = res[-1]

    def unpack(flat2d, into):
        flat = flat2d.reshape(-1)
        off = 0
        for n, s in zip(_TINY, sizes):
            into[n] = flat[off:off + s].reshape(W[n].shape)
            off += s
        return flat[off]

    loss = unpack(sg, G)
    unpack(sd, Dl)
    unpack(sm, Mn)
    unpack(sv, Vn)

    (r_gu1,) = _split_wait("exchange3_wait", es3, sg, False)
    done = adam([("ffn1_w_gate", r_gu1, 0), ("ffn1_w_up", r_gu1, 1)])
    (r_wd1,) = _split_wait("exchange4_wait", es4, done, False)
    adam([("ffn1_w_down", r_wd1, None)])

    return loss, dx0[None], G, Dl, Mn, Vn


def kernel(x, ffn1_norm, ffn1_w_gate, ffn1_w_up, ffn1_w_down, mix_norm, w_in, s5_a_re, s5_a_im, s5_log_dt, s5_b_re, s5_b_im, s5_c_re, s5_c_im, s5_d, s5_w_glu, s5_b_glu, sgu_ln_g, sgu_ln_b, sgu_w_s, sgu_b_s, w_branch_a, w_branch_b, w_gate, b_gate, w_out, ffn2_norm, ffn2_w_gate, ffn2_w_up, ffn2_w_down, final_norm, loss_target, m_ffn1_norm, m_ffn1_w_gate, m_ffn1_w_up, m_ffn1_w_down, m_mix_norm, m_w_in, m_s5_a_re, m_s5_a_im, m_s5_log_dt, m_s5_b_re, m_s5_b_im, m_s5_c_re, m_s5_c_im, m_s5_d, m_s5_w_glu, m_s5_b_glu, m_sgu_ln_g, m_sgu_ln_b, m_sgu_w_s, m_sgu_b_s, m_w_branch_a, m_w_branch_b, m_w_gate, m_b_gate, m_w_out, m_ffn2_norm, m_ffn2_w_gate, m_ffn2_w_up, m_ffn2_w_down, m_final_norm, v_ffn1_norm, v_ffn1_w_gate, v_ffn1_w_up, v_ffn1_w_down, v_mix_norm, v_w_in, v_s5_a_re, v_s5_a_im, v_s5_log_dt, v_s5_b_re, v_s5_b_im, v_s5_c_re, v_s5_c_im, v_s5_d, v_s5_w_glu, v_s5_b_glu, v_sgu_ln_g, v_sgu_ln_b, v_sgu_w_s, v_sgu_b_s, v_w_branch_a, v_w_branch_b, v_w_gate, v_b_gate, v_w_out, v_ffn2_norm, v_ffn2_w_gate, v_ffn2_w_up, v_ffn2_w_down, v_final_norm):
    a = locals()
    W = {n: a[n] for n in _ORDER}
    M = {n: a["m_" + n] for n in _ORDER}
    V = {n: a["v_" + n] for n in _ORDER}
    loss, gx, G, Dl, Mn, Vn = _step(x, loss_target, W, M, V)
    return (loss, gx, *[G[n] for n in _ORDER], *[Dl[n] for n in _ORDER], *[Mn[n] for n in _ORDER],
            *[Vn[n] for n in _ORDER])
```

```python
import functools
import math

import jax
import jax.numpy as jnp
from jax import lax
from jax.experimental import pallas as pl
from jax.experimental.pallas import tpu as pltpu

F32 = jnp.float32
BF16 = jnp.bfloat16
NDEV = 8
NORM_EPS = 1e-6
D_MODEL = 2048
D_FF = 5632
FF_SHARD = D_FF // NDEV
S5_WIDTH = 1024
S5_GROUPS = 64
S5_GROUP_WIDTH = 16
S5_STATE = 64
S5_NS = S5_GROUPS * S5_STATE
SGU_WIDTH = 1024
SGU_HEADS = 8
MLP_CHUNK = 128
CHUNK = 64
ADAM_LR, ADAM_B1, ADAM_B2, ADAM_EPS, ADAM_WD, ADAM_STEP = 0.001, 0.9, 0.999, 1e-08, 0.01, 10
S5_TC = 256
S5_SEG = S5_TC // 8
S5_LG = 512
S5_UNROLL = True
VMEM_BIG = 56 * 1024 * 1024

MESH = pl.DeviceIdType.MESH
SDS = jax.ShapeDtypeStruct
Block = pl.BlockSpec
ANY = pl.BlockSpec(memory_space=pl.ANY)


def _cparams(sem=None, vmem=None):
    return pltpu.CompilerParams(dimension_semantics=sem, vmem_limit_bytes=vmem)


def _const(shape):
    nd = len(shape)
    return pl.BlockSpec(shape, lambda i: (0,) * nd, pipeline_mode=pl.Buffered(1))


def _sigmoid(x):
    return 0.5 * jnp.tanh(0.5 * x) + 0.5


_GELU_C = math.sqrt(2.0 / math.pi)


def _gelu(x):
    return 0.5 * x * (1.0 + jnp.tanh(_GELU_C * (x + 0.044715 * x * x * x)))


def _gelu_grad(x):
    t = jnp.tanh(_GELU_C * (x + 0.044715 * x * x * x))
    return 0.5 * (1.0 + t) + 0.5 * x * (1.0 - t * t) * _GELU_C * (1.0 + 3.0 * 0.044715 * x * x)


NN = (((1,), (0,)), ((), ()))
NT = (((1,), (1,)), ((), ()))
TN = (((0,), (0,)), ((), ()))


def _dot(a, b, dims=NN):
    return lax.dot_general(a, b, dims, preferred_element_type=F32)


def _matmul(name, a, b, extras, *, grid, a_spec, b_spec, extra_specs, out_shapes, out_specs, acc_shape,
            epilogue, dims=NN, nb=None, compute=None, after=None, vmem=VMEM_BIG):
    nk = grid[2]
    if after is not None:
        extras = tuple(extras) + (after,)
        extra_specs = list(extra_specs) + [Block((8, 128), lambda i, j, k: (0, 0))]
    ne, no = len(extras), len(out_shapes)
    nacc = nb or 1
    if compute is None:
        def compute(a_ref, b_ref, q):
            return _dot(a_ref[...], b_ref[q] if nb else b_ref[...], dims)

    def body(*refs):
        a_ref, b_ref = refs[0], refs[1]
        ex = refs[2:2 + ne]
        outs = refs[2 + ne:2 + ne + no]
        if nk == 1:
            epilogue([compute(a_ref, b_ref, q) for q in range(nacc)], ex, outs)
            return
        acc_ref = refs[2 + ne + no]
        k = pl.program_id(2)

        @pl.when(k == 0)
        def _():
            acc_ref[...] = jnp.zeros_like(acc_ref)

        for q in range(nacc):
            acc_ref[q] += compute(a_ref, b_ref, q)

        @pl.when(k == nk - 1)
        def _():
            epilogue([acc_ref[q] for q in range(nacc)], ex, outs)

    scratch = [] if nk == 1 else [pltpu.VMEM((nacc,) + tuple(acc_shape), F32)]
    res = pl.pallas_call(
        body, name=name, grid=grid,
        in_specs=[a_spec, b_spec] + list(extra_specs),
        out_specs=list(out_specs), out_shape=list(out_shapes), scratch_shapes=scratch,
        compiler_params=_cparams(("parallel", "parallel", "arbitrary"), vmem),
    )(a, b, *extras)
    return res


def _store(dtype_outs=None):
    def ep(accs, ex, outs):
        outs[0][...] = accs[0].astype(outs[0].dtype)
    return ep


def _tile(n, t):
    t = min(n, t)
    assert n % t == 0, (n, t)
    return t


def _ksum(kq, dims):
    def compute(a_ref, b_ref, _):
        part = _dot(a_ref[0], b_ref[0], dims)
        for q in range(1, kq):
            part = part + _dot(a_ref[q], b_ref[q], dims)
        return part
    return compute


def _ksum_lanes(kq, ns, dims):
    def compute(a_ref, b_ref, _):
        part = _dot(a_ref[:, 0:ns], b_ref[0], dims)
        for q in range(1, kq):
            part = part + _dot(a_ref[:, q * ns:(q + 1) * ns], b_ref[q], dims)
        return part
    return compute


def _wide_b(g):
    def compute(a_ref, b_ref, _):
        bw = b_ref[0] if g == 1 else jnp.concatenate([b_ref[q] for q in range(g)], axis=1)
        return _dot(a_ref[...], bw, NN)
    return compute


TT_DEEP = 2048
TT_FFN = 4096


HIDDEN = NDEV * FF_SHARD


def _ffn_up(name, h, wgu, after=None):
    T, D = h.shape
    tm = _tile(T, 1024)

    def ep(accs, ex, outs):
        a, b = accs
        outs[0][0] = a.astype(BF16)
        outs[0][1] = b.astype(BF16)
        outs[1][...] = (a * _sigmoid(a) * b).astype(BF16)

    return _matmul(
        name, wgu, h, (), after=after, grid=(NDEV, T // tm, 1),
        a_spec=Block((None, 2, FF_SHARD, D), lambda j, i, k: (j, 0, 0, 0)),
        b_spec=Block((tm, D), lambda j, i, k: (i, 0)),
        extra_specs=(),
        out_shapes=[SDS((NDEV, 2, FF_SHARD, T), BF16), SDS((NDEV, FF_SHARD, T), BF16)],
        out_specs=[Block((None, 2, FF_SHARD, tm), lambda j, i, k: (j, 0, 0, i)),
                   Block((None, FF_SHARD, tm), lambda j, i, k: (j, 0, i))],
        acc_shape=(FF_SHARD, tm), nb=2,
        compute=lambda a_ref, b_ref, q: _dot(a_ref[q], b_ref[...], NT), epilogue=ep)


def _ffn_down(name, f, wd, xres, after=None):
    T = f.shape[2]
    tm, tn, tk = _tile(T, 1024), 1024, HIDDEN // 2

    def ep(accs, ex, outs):
        outs[0][...] = ex[0][...] + 0.5 * accs[0]

    return _matmul(
        name, f.reshape(HIDDEN, T), wd.reshape(HIDDEN, D_MODEL), (xres,), after=after,
        grid=(T // tm, D_MODEL // tn, HIDDEN // tk),
        a_spec=Block((tk, tm), lambda i, j, k: (k, i)),
        b_spec=Block((tk, tn), lambda i, j, k: (k, j)),
        extra_specs=[Block((tm, tn), lambda i, j, k: (i, j))],
        out_shapes=[SDS((T, D_MODEL), F32)],
        out_specs=[Block((tm, tn), lambda i, j, k: (i, j))],
        acc_shape=(tm, tn), dims=TN, epilogue=ep)[0]


def _ffn_down_bwd_act(name, dyb, wd, ab, after=None):
    T, D = dyb.shape
    tm, g = _tile(T, 1024), 1

    def ep(accs, ex, outs):
        for s in range(g):
            df = accs[0][s * FF_SHARD:(s + 1) * FF_SHARD, :].astype(BF16)
            a = ex[0][s, 0]
            b = ex[0][s, 1]
            hs = 0.5 * _sigmoid(a)
            outs[0][s, 0] = df * b * hs * (1.0 + a * (1.0 - 2.0 * hs))
            outs[0][s, 1] = df * a * hs

    blk = Block((g, 2, FF_SHARD, tm), lambda i, j, k: (j, 0, 0, i))
    return _matmul(
        name, wd.reshape(HIDDEN, D), dyb, (ab,), after=after, grid=(T // tm, NDEV // g, 1),
        a_spec=Block((g * FF_SHARD, D), lambda i, j, k: (j, 0)),
        b_spec=Block((tm, D), lambda i, j, k: (i, 0)),
        extra_specs=[blk],
        out_shapes=[SDS((NDEV, 2, FF_SHARD, T), BF16)],
        out_specs=[blk],
        acc_shape=(g * FF_SHARD, tm), dims=NT, epilogue=ep)[0]


def _ffn_down_bwd_w(name, f, dyb, after=None):
    T = f.shape[2]
    tt, tn, tr = _tile(T, TT_DEEP), 1024, 2 * FF_SHARD

    def ep(accs, ex, outs):
        outs[0][...] = (0.5 * accs[0]).astype(BF16)

    return _matmul(
        name, f.reshape(HIDDEN, T), dyb, (), after=after, grid=(HIDDEN // tr, D_MODEL // tn, T // tt),
        a_spec=Block((tr, tt), lambda j, n, k: (j, k)),
        b_spec=Block((tt, tn), lambda j, n, k: (k, n)),
        extra_specs=(),
        out_shapes=[SDS((HIDDEN, D_MODEL), BF16)],
        out_specs=[Block((tr, tn), lambda j, n, k: (j, n))],
        acc_shape=(tr, tn), dims=NN, epilogue=ep)[0].reshape(NDEV, FF_SHARD, D_MODEL)


def _ffn_up_bwd_h(name, dab, wgu, after):
    T = dab.shape[3]
    tm, tn, tk = _tile(T, 1024), 1024, HIDDEN // 2
    return _matmul(
        name, dab.reshape(2 * HIDDEN, T), wgu.reshape(2 * HIDDEN, D_MODEL), (), after=after,
        grid=(T // tm, D_MODEL // tn, 2 * HIDDEN // tk),
        a_spec=Block((tk, tm), lambda i, j, k: (k, i)),
        b_spec=Block((tk, tn), lambda i, j, k: (k, j)),
        extra_specs=(),
        out_shapes=[SDS((T, D_MODEL), BF16)],
        out_specs=[Block((tm, tn), lambda i, j, k: (i, j))],
        acc_shape=(tm, tn), dims=TN, epilogue=_store())[0]


def _ffn_up_bwd_w(name, h, dab):
    T, D = h.shape
    tt, tn = _tile(T, TT_FFN), 1024

    def ep(accs, ex, outs):
        outs[0][0] = accs[0].astype(BF16)
        outs[0][1] = accs[1].astype(BF16)

    return _matmul(
        name, dab, h, (), grid=(NDEV, D // tn, T // tt),
        a_spec=Block((None, 2, FF_SHARD, tt), lambda j, n, k: (j, 0, 0, k)),
        b_spec=Block((tt, tn), lambda j, n, k: (k, n)),
        extra_specs=(),
        out_shapes=[SDS((NDEV, 2, FF_SHARD, D), BF16)],
        out_specs=[Block((None, 2, FF_SHARD, tn), lambda j, n, k: (j, 0, 0, n))],
        acc_shape=(FF_SHARD, tn), nb=2,
        compute=lambda a_ref, b_ref, q: _dot(a_ref[q], b_ref[...], NN), epilogue=ep)[0]


def _shards_per_step(ns):
    return max(g for g in (1, 2, 4, 8) if g * ns <= 2048)


def _split_lanes(g, ns):
    def ep(accs, ex, outs):
        for q in range(g):
            outs[0][q] = accs[0][:, q * ns:(q + 1) * ns].astype(outs[0].dtype)
    return ep


def _rows_fwd(name, a, wt, after=None):
    T, K = a.shape
    N = wt.shape[0]
    tm, tn = _tile(T, 1024), _tile(N, 1536)
    return _matmul(
        name, a, wt, (), after=after, grid=(N // tn, T // tm, 1),
        a_spec=Block((tm, K), lambda j, i, k: (i, 0)),
        b_spec=Block((tn, K), lambda j, i, k: (j, 0)),
        extra_specs=(),
        out_shapes=[SDS((T, N), BF16)],
        out_specs=[Block((tm, tn), lambda j, i, k: (i, j))],
        acc_shape=(tm, tn), dims=NT, epilogue=_store())[0]


def _rows_bwd_a(name, dy, wt, add):
    T, N = dy.shape
    K = wt.shape[1]
    tm, tn = _tile(T, 1024), _tile(K, 1024)

    def ep(accs, ex, outs):
        outs[0][...] = (accs[0] + ex[0][...].astype(F32)).astype(BF16)

    return _matmul(
        name, dy, wt, (add,), grid=(T // tm, K // tn, 1),
        a_spec=Block((tm, N), lambda i, j, k: (i, 0)),
        b_spec=Block((N, tn), lambda i, j, k: (0, j)),
        extra_specs=[Block((tm, tn), lambda i, j, k: (i, j))],
        out_shapes=[SDS((T, K), BF16)],
        out_specs=[Block((tm, tn), lambda i, j, k: (i, j))],
        acc_shape=(tm, tn), dims=NN, epilogue=ep)[0]


def _col_fwd(name, a, w, out_dtype=BF16, after=None):
    T, K = a.shape
    ns = w.shape[2]
    g = _shards_per_step(ns)
    tm = _tile(T, 1024)
    return _matmul(
        name, a, w, (), after=after, grid=(NDEV // g, T // tm, 1),
        a_spec=Block((tm, K), lambda j, i, k: (i, 0)),
        b_spec=Block((g, K, ns), lambda j, i, k: (j, 0, 0)),
        extra_specs=(),
        out_shapes=[SDS((T, NDEV * ns), out_dtype)],
        out_specs=[Block((tm, g * ns), lambda j, i, k: (i, j))],
        acc_shape=(tm, g * ns), compute=_wide_b(g), epilogue=_store())[0]


def _col_bwd_a(name, dy, w, add=None):
    T = dy.shape[0]
    _, K, ns = w.shape
    tm, tn = _tile(T, 1024), _tile(K, 1024)

    def ep(accs, ex, outs):
        r = accs[0]
        if add is not None:
            r = r + ex[0][...].astype(F32)
        outs[0][...] = r.astype(BF16)

    extras = () if add is None else (add,)
    return _matmul(
        name, dy, w, extras, grid=(T // tm, K // tn, 1),
        a_spec=Block((tm, NDEV * ns), lambda i, j, k: (i, 0)),
        b_spec=Block((NDEV, tn, ns), lambda i, j, k: (0, j, 0)),
        extra_specs=[Block((tm, tn), lambda i, j, k: (i, j))] * len(extras),
        out_shapes=[SDS((T, K), BF16)],
        out_specs=[Block((tm, tn), lambda i, j, k: (i, j))],
        acc_shape=(tm, tn), compute=_ksum_lanes(NDEV, ns, NT), epilogue=ep)[0]


def _col_bwd_w(name, a, dy, ns):
    T, K = a.shape
    g = _shards_per_step(ns)
    tt, tr = _tile(T, TT_DEEP), _tile(K, 1024)
    return _matmul(
        name, a, dy, (), grid=(NDEV // g, K // tr, T // tt),
        a_spec=Block((tt, tr), lambda j, n, k: (k, n)),
        b_spec=Block((tt, g * ns), lambda j, n, k: (k, j)),
        extra_specs=(),
        out_shapes=[SDS((NDEV, K, ns), BF16)],
        out_specs=[Block((g, tr, ns), lambda j, n, k: (j, n, 0))],
        acc_shape=(tr, g * ns), dims=TN, epilogue=_split_lanes(g, ns))[0]


def _gate_fwd(name, h, w, bias):
    T, K = h.shape
    ns = w.shape[2]
    g = 2
    per = D_MODEL // (g * ns)
    tm = _tile(T, 1024)

    def ep(accs, ex, outs):
        outs[0][...] = (accs[0] + ex[0][...]).astype(BF16)

    return _matmul(
        name, h, w, (bias,), grid=(NDEV // g, T // tm, 1),
        a_spec=Block((tm, K), lambda j, i, k: (i, 0)),
        b_spec=Block((g, K, ns), lambda j, i, k: (j, 0, 0)),
        extra_specs=[Block((None, 1, g * ns), lambda j, i, k: (j // per, 0, j % per))],
        out_shapes=[SDS((2, T, D_MODEL), BF16)],
        out_specs=[Block((None, tm, g * ns), lambda j, i, k: (j // per, i, j % per))],
        acc_shape=(tm, g * ns), compute=_wide_b(g), epilogue=ep)[0]


def _gate_bwd_a(name, dgl, w):
    _, T, _ = dgl.shape
    _, K, ns = w.shape
    per = D_MODEL // ns
    tm, tn = _tile(T, 1024), 1024

    def compute(a_ref, b_ref, _):
        part = None
        for q in range(NDEV):
            d = _dot(a_ref[q // per, :, (q % per) * ns:(q % per + 1) * ns], b_ref[q], NT)
            part = d if part is None else part + d
        return part

    return _matmul(
        name, dgl, w, (), grid=(T // tm, K // tn, 1),
        a_spec=Block((2, tm, D_MODEL), lambda i, j, k: (0, i, 0)),
        b_spec=Block((NDEV, tn, ns), lambda i, j, k: (0, j, 0)),
        extra_specs=(),
        out_shapes=[SDS((T, K), BF16)],
        out_specs=[Block((tm, tn), lambda i, j, k: (i, j))],
        acc_shape=(tm, tn), compute=compute, epilogue=_store())[0]


def _gate_bwd_w(name, h, dgl, ns):
    T, K = h.shape
    g = 2
    per = D_MODEL // (g * ns)
    tt, tr = _tile(T, TT_DEEP), 1024
    return _matmul(
        name, h, dgl, (), grid=(NDEV // g, K // tr, T // tt),
        a_spec=Block((tt, tr), lambda j, n, k: (k, n)),
        b_spec=Block((None, tt, g * ns), lambda j, n, k: (j // per, k, j % per)),
        extra_specs=(),
        out_shapes=[SDS((NDEV, K, ns), BF16)],
        out_specs=[Block((g, tr, ns), lambda j, n, k: (j, n, 0))],
        acc_shape=(tr, g * ns), dims=TN, epilogue=_split_lanes(g, ns))[0]


def _plain_fwd_res(name, a, w, xres):
    T, K = a.shape
    N = w.shape[1]
    tm, tn = _tile(T, 1024), _tile(N, 1024)

    def ep(accs, ex, outs):
        outs[0][...] = ex[0][...] + accs[0]

    return _matmul(
        name, a, w, (xres,), grid=(T // tm, N // tn, 1),
        a_spec=Block((tm, K), lambda i, j, k: (i, 0)),
        b_spec=Block((K, tn), lambda i, j, k: (0, j)),
        extra_specs=[Block((tm, tn), lambda i, j, k: (i, j))],
        out_shapes=[SDS((T, N), F32)],
        out_specs=[Block((tm, tn), lambda i, j, k: (i, j))],
        acc_shape=(tm, tn), dims=NN, nb=None, epilogue=ep)[0]


def _plain_bwd_a(name, dy, w):
    T, N = dy.shape
    K = w.shape[0]
    tm, tn = _tile(T, 1024), _tile(K, 1024)
    return _matmul(
        name, dy, w, (), grid=(T // tm, K // tn, 1),
        a_spec=Block((tm, N), lambda i, j, k: (i, 0)),
        b_spec=Block((tn, N), lambda i, j, k: (j, 0)),
        extra_specs=(),
        out_shapes=[SDS((T, K), BF16)],
        out_specs=[Block((tm, tn), lambda i, j, k: (i, j))],
        acc_shape=(tm, tn), dims=NT, nb=None, epilogue=_store())[0]


def _plain_bwd_w(name, a, dy):
    T, K = a.shape
    N = dy.shape[1]
    tt, tr, tn = _tile(T, TT_DEEP), _tile(K, 1024), _tile(N, 1024)
    return _matmul(
        name, a, dy, (), grid=(K // tr, N // tn, T // tt),
        a_spec=Block((tt, tr), lambda m, n, k: (k, m)),
        b_spec=Block((tt, tn), lambda m, n, k: (k, n)),
        extra_specs=(),
        out_shapes=[SDS((K, N), BF16)],
        out_specs=[Block((tr, tn), lambda m, n, k: (m, n))],
        acc_shape=(tr, tn), dims=TN, nb=None, epilogue=_store())[0]


def _rms_fwd(name, x, g):
    T, D = x.shape
    tm = _tile(T, 512)

    def body(x_ref, g_ref, h_ref):
        xv = x_ref[...]
        r = lax.rsqrt(jnp.mean(xv * xv, axis=-1, keepdims=True) + NORM_EPS)
        h_ref[...] = (xv * r * g_ref[...]).astype(BF16)

    return pl.pallas_call(
        body, name=name, grid=(T // tm,),
        in_specs=[Block((tm, D), lambda i: (i, 0)), Block((1, D), lambda i: (0, 0))],
        out_specs=Block((tm, D), lambda i: (i, 0)), out_shape=SDS((T, D), BF16),
        compiler_params=_cparams(("arbitrary",), VMEM_BIG))(x, g)


def _rms_bwd(name, dh, x, g, dxin, out_dtype):
    T, D = x.shape
    tm = _tile(T, 512)

    def body(dh_ref, x_ref, g_ref, dxin_ref, dx_ref, dg_ref):
        i = pl.program_id(0)
        xv = x_ref[...]
        dh = dh_ref[...].astype(F32)
        r = lax.rsqrt(jnp.mean(xv * xv, axis=-1, keepdims=True) + NORM_EPS)
        xh = xv * r
        gd = dh * g_ref[...]
        dx = dxin_ref[...].astype(F32) + r * (gd - xh * jnp.mean(gd * xh, axis=-1, keepdims=True))
        dx_ref[...] = dx.astype(out_dtype)
        dgp = jnp.sum(dh * xh, axis=0, keepdims=True)

        @pl.when(i == 0)
        def _():
            dg_ref[...] = dgp

        @pl.when(i > 0)
        def _():
            dg_ref[...] += dgp

    row = Block((tm, D), lambda i: (i, 0))
    vec = Block((1, D), lambda i: (0, 0))
    return pl.pallas_call(
        body, name=name, grid=(T // tm,),
        in_specs=[row, row, vec, row], out_specs=[row, vec],
        out_shape=[SDS((T, D), out_dtype), SDS((1, D), F32)],
        compiler_params=_cparams(("arbitrary",), VMEM_BIG))(dh, x, g, dxin)


def _loss_head(name, x, g, tgt):
    T, D = x.shape
    tm = _tile(T, 512)

    def body(x_ref, g_ref, t_ref, loss_ref, dxb_ref, dg_ref):
        i = pl.program_id(0)
        xv = x_ref[...]
        gv = g_ref[...]
        r = lax.rsqrt(jnp.mean(xv * xv, axis=-1, keepdims=True) + NORM_EPS)
        xh = xv * r
        err = xh * gv - t_ref[...]
        lp = 0.5 * jnp.sum(jnp.mean(err * err, axis=-1, keepdims=True), axis=0, keepdims=True)
        dout = err * (1.0 / D)
        gd = dout * gv
        dx = r * (gd - xh * jnp.mean(gd * xh, axis=-1, keepdims=True))
        dxb_ref[...] = dx.astype(BF16)
        dgp = jnp.sum(dout * xh, axis=0, keepdims=True)
        lpb = jnp.broadcast_to(lp, (1, 128))

        @pl.when(i == 0)
        def _():
            dg_ref[...] = dgp
            loss_ref[...] = lpb

        @pl.when(i > 0)
        def _():
            dg_ref[...] += dgp
            loss_ref[...] += lpb

    row = Block((tm, D), lambda i: (i, 0))
    vec = Block((1, D), lambda i: (0, 0))
    return pl.pallas_call(
        body, name=name, grid=(T // tm,),
        in_specs=[row, vec, row], out_specs=[Block((1, 128), lambda i: (0, 0)), row, vec],
        out_shape=[SDS((1, 128), F32), SDS((T, D), BF16), SDS((1, D), F32)],
        compiler_params=_cparams(("arbitrary",), VMEM_BIG))(x, g, tgt)


def _merge_fwd(name, pa, pb, gl):
    T, D = pa.shape
    tm = _tile(T, 512)

    def body(pa_ref, pb_ref, gl_ref, o_ref):
        ga = _sigmoid(gl_ref[0].astype(F32))
        gb = _sigmoid(gl_ref[1].astype(F32))
        o_ref[...] = (ga * pa_ref[...].astype(F32) + gb * pb_ref[...].astype(F32)).astype(BF16)

    row = Block((tm, D), lambda i: (i, 0))
    return pl.pallas_call(
        body, name=name, grid=(T // tm,),
        in_specs=[row, row, Block((2, tm, D), lambda i: (0, i, 0))], out_specs=row,
        out_shape=SDS((T, D), BF16), compiler_params=_cparams(("arbitrary",), VMEM_BIG))(pa, pb, gl)


def _merge_bwd(name, dm, pa, pb, gl):
    T, D = pa.shape
    tm = _tile(T, 512)

    def body(dm_ref, pa_ref, pb_ref, gl_ref, dpa_ref, dpb_ref, dgl_ref, db_ref):
        i = pl.program_id(0)
        dmv = dm_ref[...].astype(F32)
        ga = _sigmoid(gl_ref[0].astype(F32))
        gb = _sigmoid(gl_ref[1].astype(F32))
        dpa_ref[...] = (dmv * ga).astype(BF16)
        dpb_ref[...] = (dmv * gb).astype(BF16)
        dga = dmv * pa_ref[...].astype(F32) * ga * (1.0 - ga)
        dgb = dmv * pb_ref[...].astype(F32) * gb * (1.0 - gb)
        dgl_ref[0] = dga.astype(BF16)
        dgl_ref[1] = dgb.astype(BF16)
        sa = jnp.sum(dga, axis=0, keepdims=True)
        sb = jnp.sum(dgb, axis=0, keepdims=True)

        @pl.when(i == 0)
        def _():
            db_ref[0] = sa
            db_ref[1] = sb

        @pl.when(i > 0)
        def _():
            db_ref[0] += sa
            db_ref[1] += sb

    row = Block((tm, D), lambda i: (i, 0))
    two = Block((2, tm, D), lambda i: (0, i, 0))
    return pl.pallas_call(
        body, name=name, grid=(T // tm,),
        in_specs=[row, row, row, two], out_specs=[row, row, two, Block((2, 1, D), lambda i: (0, 0, 0))],
        out_shape=[SDS((T, D), BF16), SDS((T, D), BF16), SDS((2, T, D), BF16), SDS((2, 1, D), F32)],
        compiler_params=_cparams(("arbitrary",), VMEM_BIG))(dm, pa, pb, gl)


def _sgu_core(ur, vr, lng, lnb, ws_ref, bs_ref):
    tm = ur.shape[0]
    gu = _gelu(ur)
    gv = _gelu(vr)
    mu = jnp.mean(gv, axis=-1, keepdims=True)
    cen = gv - mu
    rstd = lax.rsqrt(jnp.mean(cen * cen, axis=-1, keepdims=True) + NORM_EPS)
    xhat = cen * rstd
    vn = (xhat * lng + lnb).astype(BF16)
    rows = []
    for n in range(tm // MLP_CHUNK):
        cols = []
        for h in range(SGU_HEADS):
            blk = vn[n * MLP_CHUNK:(n + 1) * MLP_CHUNK, h * 128:(h + 1) * 128]
            cols.append(_dot(ws_ref[h], blk) + bs_ref[h])
        rows.append(jnp.concatenate(cols, axis=1))
    mixed = jnp.concatenate(rows, axis=0) if len(rows) > 1 else rows[0]
    return gu, xhat, rstd, vn, mixed


def _sgu_fwd(name, proj, lng, lnb, wsm, bst):
    T = proj.shape[0]
    W = SGU_WIDTH
    tm = _tile(T, 512)

    def body(u_ref, v_ref, lng_ref, lnb_ref, ws_ref, bs_ref, o_ref):
        gu, _, _, _, mixed = _sgu_core(u_ref[...].astype(F32), v_ref[...].astype(F32), lng_ref[...], lnb_ref[...],
                                       ws_ref, bs_ref)
        o_ref[...] = (gu * mixed).astype(BF16)

    vec = Block((1, W), lambda i: (0, 0))
    return pl.pallas_call(
        body, name=name, grid=(T // tm,),
        in_specs=[Block((tm, W), lambda i: (i, 1)), Block((tm, W), lambda i: (i, 2)), vec, vec,
                  Block((SGU_HEADS, 128, 128), lambda i: (0, 0, 0)), Block((SGU_HEADS, 128, 128), lambda i: (0, 0, 0))],
        out_specs=Block((tm, W), lambda i: (i, 0)), out_shape=SDS((T, W), BF16),
        compiler_params=_cparams(("arbitrary",), VMEM_BIG))(proj, proj, lng, lnb, wsm, bst)


def _sgu_bwd(name, dyb, dua, proj, lng, lnb, wsm, wsmt, bst):
    T = proj.shape[0]
    W = SGU_WIDTH
    tm = _tile(T, 512)

    def body(dy_ref, dua_ref, u_ref, v_ref, lng_ref, lnb_ref, ws_ref, wst_ref, bs_ref,
             duv_ref, dws_ref, dbs_ref, dlng_ref, dlnb_ref):
        i = pl.program_id(0)
        duv_ref[:, :W] = dua_ref[...]
        ur = u_ref[...].astype(F32)
        vr = v_ref[...].astype(F32)
        lng_v = lng_ref[...]
        gu, xhat, rstd, vn, mixed = _sgu_core(ur, vr, lng_v, lnb_ref[...], ws_ref, bs_ref)
        dy = dy_ref[...].astype(F32)
        dgu = dy * mixed
        dmix = dy * gu
        dmb = dmix.astype(BF16)
        dws_p, dbs_p, rows = [], [], []
        for h in range(SGU_HEADS):
            acc_w = jnp.zeros((128, 128), F32)
            acc_b = jnp.zeros((128, 1), F32)
            for n in range(tm // MLP_CHUNK):
                r0 = n * MLP_CHUNK
                dmt = dmb[r0:r0 + MLP_CHUNK, h * 128:(h + 1) * 128]
                acc_w = acc_w + _dot(dmt, vn[r0:r0 + MLP_CHUNK, h * 128:(h + 1) * 128], NT)
                acc_b = acc_b + jnp.sum(dmix[r0:r0 + MLP_CHUNK, h * 128:(h + 1) * 128], axis=1, keepdims=True)
            dws_p.append(acc_w)
            dbs_p.append(jnp.broadcast_to(acc_b, (128, 128)))
        for n in range(tm // MLP_CHUNK):
            r0 = n * MLP_CHUNK
            rows.append(jnp.concatenate(
                [_dot(wst_ref[h], dmb[r0:r0 + MLP_CHUNK, h * 128:(h + 1) * 128]) for h in range(SGU_HEADS)], axis=1))
        dvn = jnp.concatenate(rows, axis=0) if len(rows) > 1 else rows[0]
        dlng_p = jnp.sum(dvn * xhat, axis=0, keepdims=True)
        dlnb_p = jnp.sum(dvn, axis=0, keepdims=True)
        dxh = dvn * lng_v
        dgv = rstd * (dxh - jnp.mean(dxh, axis=-1, keepdims=True)
                      - xhat * jnp.mean(dxh * xhat, axis=-1, keepdims=True))
        duv_ref[:, W:2 * W] = (dgu * _gelu_grad(ur)).astype(BF16)
        duv_ref[:, 2 * W:] = (dgv * _gelu_grad(vr)).astype(BF16)

        @pl.when(i == 0)
        def _():
            for h in range(SGU_HEADS):
                dws_ref[h] = dws_p[h]
                dbs_ref[h] = dbs_p[h]
            dlng_ref[...] = dlng_p
            dlnb_ref[...] = dlnb_p

        @pl.when(i > 0)
        def _():
            for h in range(SGU_HEADS):
                dws_ref[h] += dws_p[h]
                dbs_ref[h] += dbs_p[h]
            dlng_ref[...] += dlng_p
            dlnb_ref[...] += dlnb_p

    vec = Block((1, W), lambda i: (0, 0))
    wsb = Block((SGU_HEADS, 128, 128), lambda i: (0, 0, 0))
    hsq = SDS((SGU_HEADS, 128, 128), F32)
    return pl.pallas_call(
        body, name=name, grid=(T // tm,),
        in_specs=[Block((tm, W), lambda i: (i, 0)), Block((tm, W), lambda i: (i, 0)),
                  Block((tm, W), lambda i: (i, 1)), Block((tm, W), lambda i: (i, 2)),
                  vec, vec, wsb, wsb, wsb],
        out_specs=[Block((tm, 3 * W), lambda i: (i, 0)), wsb, wsb, vec, vec],
        out_shape=[SDS((T, 3 * W), BF16), hsq, hsq, SDS((1, W), F32), SDS((1, W), F32)],
        compiler_params=_cparams(("arbitrary",), VMEM_BIG))(dyb, dua, proj, proj, lng, lnb, wsm, wsmt, bst)


def _s5_disc(lr, li, ldt, brt, bit):
    dt = jnp.exp(ldt)
    decay = jnp.exp(lr * dt)
    abr = decay * jnp.cos(li * dt)
    abi = decay * jnp.sin(li * dt)
    denom = lr * lr + li * li
    nr = abr - 1.0
    ni = abi
    kr = (nr * lr + ni * li) / denom
    ki = (ni * lr - nr * li) / denom
    bkr = kr[None] * brt - ki[None] * bit
    bki = kr[None] * bit + ki[None] * brt
    return abr, abi, bkr, bki


def _s5_prep(lr, li, ldt, brt, bit):
    G, P, C = S5_GROUPS, S5_STATE, S5_GROUP_WIDTH

    def body(lr_ref, li_ref, ldt_ref, br_ref, bi_ref, abr_ref, abi_ref, pwr_ref, pwi_ref, bkr_ref, bki_ref):
        lr_, li_, ldt_ = lr_ref[...], li_ref[...], ldt_ref[...]
        res = _s5_disc(lr_, li_, ldt_, br_ref[...], bi_ref[...])
        for o, r in zip((abr_ref, abi_ref, bkr_ref, bki_ref), res):
            o[...] = r
        dt = jnp.exp(ldt_)
        n = lax.broadcasted_iota(jnp.int32, (S5_SEG, G, P), 0).astype(F32) + 1.0
        dec = jnp.exp((lr_ * dt)[None] * n)
        ang = (li_ * dt)[None] * n
        pwr_ref[...] = dec * jnp.cos(ang)
        pwi_ref[...] = dec * jnp.sin(ang)

    gp = SDS((G, P), F32)
    sgp = SDS((S5_SEG, G, P), F32)
    cgp = SDS((C, G, P), F32)
    return pl.pallas_call(body, name="s5_prep", out_shape=[gp, gp, sgp, sgp, cgp, cgp])(lr, li, ldt, brt, bit)


def _s5_prep_bwd(lr, li, ldt, brt, bit, dabr, dabi, dbkr, dbki):
    G, P, C = S5_GROUPS, S5_STATE, S5_GROUP_WIDTH

    def body(lr_ref, li_ref, ldt_ref, br_ref, bi_ref, dabr_ref, dabi_ref, dbkr_ref, dbki_ref,
             o_lr, o_li, o_ldt, o_br, o_bi):
        _, pull = jax.vjp(_s5_disc, lr_ref[...], li_ref[...], ldt_ref[...], br_ref[...], bi_ref[...])
        g = pull((dabr_ref[...], dabi_ref[...], dbkr_ref[...], dbki_ref[...]))
        for o, r in zip((o_lr, o_li, o_ldt, o_br, o_bi), g):
            o[...] = r

    gp = SDS((G, P), F32)
    cgp = SDS((C, G, P), F32)
    return pl.pallas_call(body, name="s5_prep_bwd", out_shape=[gp, gp, SDS((G, 1), F32), cgp, cgp])(
        lr, li, ldt, brt, bit, dabr, dabi, dbkr, dbki)


def _s5_scan(buf_ref, ar_row, ai_row, pwr_ref, pwi_ref, carry_ref, LG, xs_ref=None, dar_ref=None, dai_ref=None):
    reverse = xs_ref is not None
    NS, SEG = S5_NS, S5_SEG
    sgn = -1.0 if reverse else 1.0
    for lg in range(NS // LG):
        cr = slice(lg * LG, (lg + 1) * LG)
        ci = slice(NS + lg * LG, NS + (lg + 1) * LG)
        ar1, ai1 = ar_row[:, cr], sgn * ai_row[:, cr]
        asr1, asi1 = pwr_ref[SEG - 1:SEG, cr], sgn * pwi_ref[SEG - 1:SEG, cr]
        ar = jnp.broadcast_to(ar1, (8, LG))
        ai = jnp.broadcast_to(ai1, (8, LG))

        def step_of(j):
            return (SEG - 1 - j) if reverse else j

        def p1(j, st):
            sr, si = st
            rows = pl.ds(pl.multiple_of(step_of(j) * 8, 8), 8)
            nr = ar * sr - ai * si + buf_ref[rows, cr]
            ni = ar * si + ai * sr + buf_ref[rows, ci]
            buf_ref[rows, cr] = nr
            buf_ref[rows, ci] = ni
            return nr, ni

        z = jnp.zeros((8, LG), F32)
        er, ei = lax.fori_loop(0, SEG, p1, (z, z), unroll=S5_UNROLL)
        c_r = carry_ref[:, cr]
        c_i = carry_ref[:, ci]
        cs_r, cs_i = [None] * 8, [None] * 8
        order = range(7, -1, -1) if reverse else range(8)
        for s in order:
            cs_r[s], cs_i[s] = c_r, c_i
            e_r, e_i = er[s:s + 1], ei[s:s + 1]
            c_r, c_i = e_r + asr1 * c_r - asi1 * c_i, e_i + asr1 * c_i + asi1 * c_r
        carry_ref[:, cr] = c_r
        carry_ref[:, ci] = c_i
        cmr = jnp.concatenate(cs_r, axis=0)
        cmi = jnp.concatenate(cs_i, axis=0)

        def carried(j):
            pr = pwr_ref[pl.ds(j, 1), cr]
            pi = sgn * pwi_ref[pl.ds(j, 1), cr]
            return pr * cmr - pi * cmi, pr * cmi + pi * cmr

        if not reverse:
            def p2(j, st):
                rows = pl.ds(pl.multiple_of(j * 8, 8), 8)
                wr, wi = carried(j)
                buf_ref[rows, cr] += wr
                buf_ref[rows, ci] += wi
                return st

            lax.fori_loop(0, SEG, p2, 0, unroll=S5_UNROLL)
        else:
            def p2(j, st):
                pr, pi, dr, di = st
                rows = pl.ds(pl.multiple_of(step_of(j) * 8, 8), 8)
                xr = xs_ref[rows, cr]
                xi = xs_ref[rows, ci]
                dr = dr + pr * xr + pi * xi
                di = di + pi * xr - pr * xi
                wr, wi = carried(j)
                gr = buf_ref[rows, cr] + wr
                gi = buf_ref[rows, ci] + wi
                buf_ref[rows, cr] = gr
                buf_ref[rows, ci] = gi
                return gr, gi, dr, di

            st = lax.fori_loop(0, SEG, p2, (cmr, cmi, z, z), unroll=S5_UNROLL)
            dar_ref[:, cr] += st[2]
            dai_ref[:, cr] += st[3]


def _s5_fwd(proj, perm, permt, bdbr, bdbi, bdcr, bdci, abr, abi, asr, asi, dvec, wglu, bglu):
    T = proj.shape[0]
    TC, NS, W = S5_TC, S5_NS, S5_WIDTH
    nc = T // TC

    def body(u_ref, pm_ref, pmt_ref, bdbr_ref, bdbi_ref, bdcr_ref, bdci_ref, ar_ref, ai_ref, asr_ref, asi_ref,
             d_ref, wglu_ref, bglu_ref, ya_ref, xs_ref, ypre_ref, carry_ref):
        i = pl.program_id(0)

        @pl.when(i == 0)
        def _():
            carry_ref[...] = jnp.zeros_like(carry_ref)

        up = _dot(pm_ref[...], u_ref[...]).astype(BF16)
        for j in range(8):
            ut = up[:, j * 128:(j + 1) * 128]
            xs_ref[:, j * 512:(j + 1) * 512] = _dot(ut, bdbr_ref[j])
            xs_ref[:, NS + j * 512:NS + (j + 1) * 512] = _dot(ut, bdbi_ref[j])
        _s5_scan(xs_ref, ar_ref[...], ai_ref[...], asr_ref, asi_ref, carry_ref, S5_LG)
        ys = []
        for j in range(8):
            xr = xs_ref[:, j * 512:(j + 1) * 512].astype(BF16)
            xi = xs_ref[:, NS + j * 512:NS + (j + 1) * 512].astype(BF16)
            ys.append(_dot(xr, bdcr_ref[j]) + _dot(xi, bdci_ref[j]))
        ypre = jnp.concatenate(ys, axis=1) + d_ref[...] * up.astype(F32)
        ypre_ref[...] = ypre
        ya = _gelu(ypre)
        zl = _dot(ya.astype(BF16), wglu_ref[...]) + bglu_ref[...]
        outp = (ya * _sigmoid(zl)).astype(BF16)
        ya_ref[...] = _dot(pmt_ref[...], outp).astype(BF16)

    return pl.pallas_call(
        body, name="s5_fwd", grid=(nc,),
        in_specs=[Block((TC, W), lambda i: (i, 0)), _const((TC, TC)), _const((TC, TC)),
                  _const((8, 128, 512)), _const((8, 128, 512)), _const((8, 512, 128)), _const((8, 512, 128)),
                  _const((1, NS)), _const((1, NS)), _const((S5_SEG, NS)), _const((S5_SEG, NS)),
                  _const((1, W)), _const((W, W)), _const((1, W))],
        out_specs=[Block((TC, W), lambda i: (i, 0)), Block((TC, 2 * NS), lambda i: (i, 0)),
                   Block((TC, W), lambda i: (i, 0))],
        out_shape=[SDS((T, W), BF16), SDS((T, 2 * NS), F32), SDS((T, W), F32)],
        scratch_shapes=[pltpu.VMEM((1, 2 * NS), F32)],
        compiler_params=_cparams(("arbitrary",), VMEM_BIG),
    )(proj, perm, permt, bdbr, bdbi, bdcr, bdci, abr, abi, asr, asi, dvec, wglu, bglu)


def _s5_bwd(dya, proj, ypre, xs, perm, permt, bdbr, bdbi, bdcr, bdci, abr, abi, asr, asi, dvec, wglu, bglu):
    T = proj.shape[0]
    TC, NS, W = S5_TC, S5_NS, S5_WIDTH
    nc = T // TC

    def body(dya_ref, u_ref, ypre_ref, xs_ref, pm_ref, pmt_ref, bdbr_ref, bdbi_ref, bdcr_ref, bdci_ref,
             ar_ref, ai_ref, asr_ref, asi_ref, d_ref, wglu_ref, bglu_ref,
             du_ref, dar_ref, dai_ref, dd_ref, dbglu_ref, o_dbdbr, o_dbdbi, o_dbdcr, o_dbdci, o_dwglu,
             g_ref, carry_ref, dbdbr_ref, dbdbi_ref, dbdcr_ref, dbdci_ref, dwglu_ref):
        i = pl.program_id(0)

        @pl.when(i == 0)
        def _():
            carry_ref[...] = jnp.zeros_like(carry_ref)
            for r in (dbdbr_ref, dbdbi_ref, dbdcr_ref, dbdci_ref, dar_ref, dai_ref, dd_ref, dwglu_ref, dbglu_ref):
                r[...] = jnp.zeros_like(r)

        pm = pm_ref[...]
        dyo = _dot(pm, dya_ref[...])
        up = _dot(pm, u_ref[...]).astype(BF16)
        upf = up.astype(F32)
        ypre_v = ypre_ref[...]
        ya = _gelu(ypre_v)
        yab = ya.astype(BF16)
        sg = _sigmoid(_dot(yab, wglu_ref[...]) + bglu_ref[...])
        dz = dyo * ya * sg * (1.0 - sg)
        dzb = dz.astype(BF16)
        dya_t = dyo * sg + _dot(dzb, wglu_ref[...], NT)
        dwglu_ref[...] += _dot(yab, dzb, TN)
        dbglu_ref[...] += jnp.sum(dz, axis=0, keepdims=True)
        dy = dya_t * _gelu_grad(ypre_v)
        dd_ref[...] += jnp.sum(dy * upf, axis=0, keepdims=True)
        dyb = dy.astype(BF16)
        for j in range(8):
            dyj = dyb[:, j * 128:(j + 1) * 128]
            g_ref[:, j * 512:(j + 1) * 512] = _dot(dyj, bdcr_ref[j], NT)
            g_ref[:, NS + j * 512:NS + (j + 1) * 512] = _dot(dyj, bdci_ref[j], NT)
            dbdcr_ref[j] += _dot(xs_ref[:, j * 512:(j + 1) * 512].astype(BF16), dyj, TN)
            dbdci_ref[j] += _dot(xs_ref[:, NS + j * 512:NS + (j + 1) * 512].astype(BF16), dyj, TN)
        _s5_scan(g_ref, ar_ref[...], ai_ref[...], asr_ref, asi_ref, carry_ref, S5_LG,
                 xs_ref=xs_ref, dar_ref=dar_ref, dai_ref=dai_ref)
        dus = []
        for j in range(8):
            ut = up[:, j * 128:(j + 1) * 128]
            gr = g_ref[:, j * 512:(j + 1) * 512].astype(BF16)
            gi = g_ref[:, NS + j * 512:NS + (j + 1) * 512].astype(BF16)
            dbdbr_ref[j] += _dot(ut, gr, TN)
            dbdbi_ref[j] += _dot(ut, gi, TN)
            dus.append(_dot(gr, bdbr_ref[j], NT) + _dot(gi, bdbi_ref[j], NT))
        dup = jnp.concatenate(dus, axis=1) + d_ref[...] * dy
        du_ref[...] = _dot(pmt_ref[...], dup.astype(BF16)).astype(BF16)

        @pl.when(i == nc - 1)
        def _():
            for src, dst in ((dbdbr_ref, o_dbdbr), (dbdbi_ref, o_dbdbi), (dbdcr_ref, o_dbdcr),
                             (dbdci_ref, o_dbdci), (dwglu_ref, o_dwglu)):
                pltpu.sync_copy(src, dst)

    c2 = lambda i: (0, 0)
    rev = lambda i: (nc - 1 - i, 0)
    return pl.pallas_call(
        body, name="s5_bwd", grid=(nc,),
        in_specs=[Block((TC, W), rev), Block((TC, W), rev), Block((TC, W), rev), Block((TC, 2 * NS), rev),
                  _const((TC, TC)), _const((TC, TC)),
                  _const((8, 128, 512)), _const((8, 128, 512)), _const((8, 512, 128)), _const((8, 512, 128)),
                  _const((1, NS)), _const((1, NS)), _const((S5_SEG, NS)), _const((S5_SEG, NS)),
                  _const((1, W)), _const((W, W)), _const((1, W))],
        out_specs=[Block((TC, W), rev), Block((8, NS), c2), Block((8, NS), c2), Block((1, W), c2), Block((1, W), c2),
                   ANY, ANY, ANY, ANY, ANY],
        out_shape=[SDS((T, W), BF16), SDS((8, NS), F32), SDS((8, NS), F32), SDS((1, W), F32), SDS((1, W), F32),
                   SDS((8, 128, 512), F32), SDS((8, 128, 512), F32),
                   SDS((8, 512, 128), F32), SDS((8, 512, 128), F32), SDS((W, W), F32)],
        scratch_shapes=[pltpu.VMEM((TC, 2 * NS), F32), pltpu.VMEM((1, 2 * NS), F32),
                        pltpu.VMEM((8, 128, 512), F32), pltpu.VMEM((8, 128, 512), F32),
                        pltpu.VMEM((8, 512, 128), F32), pltpu.VMEM((8, 512, 128), F32), pltpu.VMEM((W, W), F32)],
        compiler_params=_cparams(("arbitrary",), VMEM_BIG),
    )(dya, proj, ypre, xs, perm, permt, bdbr, bdbi, bdcr, bdci, abr, abi, asr, asi, dvec, wglu, bglu)


def _bd_b(bk_t):
    C, P = S5_GROUP_WIDTH, S5_STATE
    t = jnp.transpose(bk_t, (1, 0, 2)).reshape(8, 8, C, P)
    eye = jnp.eye(8, dtype=t.dtype)
    return (t[:, :, :, None, :] * eye[None, :, None, :, None]).reshape(8, 8 * C, 8 * P)


def _bd_b_extract(m):
    C, P = S5_GROUP_WIDTH, S5_STATE
    t = m.reshape(8, 8, C, 8, P)
    d = jnp.stack([t[:, g, :, g, :] for g in range(8)], axis=1)
    return jnp.transpose(d.reshape(S5_GROUPS, C, P), (1, 0, 2))


def _bd_c(c):
    C, P = S5_GROUP_WIDTH, S5_STATE
    t = jnp.transpose(c, (0, 2, 1)).reshape(8, 8, P, C)
    eye = jnp.eye(8, dtype=t.dtype)
    return (t[:, :, :, None, :] * eye[None, :, None, :, None]).reshape(8, 8 * P, 8 * C)


def _bd_c_extract(m):
    C, P = S5_GROUP_WIDTH, S5_STATE
    t = m.reshape(8, 8, P, 8, C)
    d = jnp.stack([t[:, g, :, g, :] for g in range(8)], axis=1)
    return jnp.transpose(d.reshape(S5_GROUPS, P, C), (0, 2, 1))


def _perm_matrix():
    r = jnp.arange(S5_TC)
    src = (r % 8) * S5_SEG + r // 8
    return (src[:, None] == jnp.arange(S5_TC)[None, :]).astype(BF16)


def _coords():
    return lax.axis_index("x"), lax.axis_index("y"), lax.axis_index("c")


def _all_gather(name, arrs):
    n = len(arrs)

    def body(*refs):
        ins, outs = refs[:n], refs[n:2 * n]
        send_sems, recv_sems, local_sems = refs[2 * n:]
        x, y, c = _coords()
        me, sibling = (x, y, c), (x, y, 1 - c)
        chips = [(1 - x, y), (x, 1 - y), (1 - x, 1 - y)]

        def slot(p):
            return 4 * p[0] + 2 * p[1] + p[2]

        def copy(a, k, block, to, src=None):
            dst = outs[a].at[slot(block)]
            return pltpu.make_async_remote_copy(
                src_ref=dst if src is None else src, dst_ref=dst,
                send_sem=send_sems.at[a * 7 + k], recv_sem=recv_sems.at[a * 7 + k],
                device_id=to, device_id_type=MESH)

        mine = [pltpu.make_async_copy(ins[a], outs[a].at[slot(me)], local_sems.at[a]) for a in range(n)]
        for m in mine:
            m.start()
        first = []
        for a in range(n):
            first.append(copy(a, 0, me, sibling, src=ins[a]))
            first += [copy(a, 1 + j, me, (*chip, c), src=ins[a]) for j, chip in enumerate(chips)]
        for cp in first:
            cp.start()
        passed = []
        for j, chip in enumerate(chips):
            for a in range(n):
                copy(a, 1 + j, (*chip, c), me).wait_recv()
                fw = copy(a, 4 + j, (*chip, c), sibling)
                fw.start()
                passed.append(fw)
        for a in range(n):
            copy(a, 0, sibling, me).wait_recv()
            for j, chip in enumerate(chips):
                copy(a, 4 + j, (*chip, 1 - c), me).wait_recv()
        for cp in first + passed:
            cp.wait_send()
        for m in mine:
            m.wait()

    return pl.pallas_call(
        body, name=name,
        in_specs=[ANY] * n, out_specs=[ANY] * n,
        out_shape=[SDS((NDEV,) + a.shape, a.dtype) for a in arrs],
        scratch_shapes=[pltpu.SemaphoreType.DMA((7 * n,)), pltpu.SemaphoreType.DMA((7 * n,)),
                        pltpu.SemaphoreType.DMA((n,))],
    )(*arrs)


HBM = pl.BlockSpec(memory_space=pltpu.HBM)
SEM = pl.BlockSpec(memory_space=pltpu.SEMAPHORE)
EFFECT = pltpu.SideEffectType.DATAFLOW_SIDE_EFFECTING


def _peers7(x, y, c):
    return [(1 - x if fx else x, 1 - y if fy else y, 1 - c if fc else c)
            for fx in (0, 1) for fy in (0, 1) for fc in (0, 1) if fx or fy or fc]


def _slot(p):
    return 4 * p[0] + 2 * p[1] + p[2]


def _split_copies(src_refs, land_refs, send_sems, recv_sems, gather, mine):
    x, y, c = _coords()
    me = (x, y, c)
    out = []
    for a, (src, land) in enumerate(zip(src_refs, land_refs)):
        for k, p in enumerate(_peers7(x, y, c)):
            s = src if gather else src.at[_slot(p)]
            out.append(pltpu.make_async_remote_copy(
                src_ref=s, dst_ref=land.at[_slot(me) if mine else _slot(p)],
                send_sem=send_sems.at[a * 7 + k], recv_sem=recv_sems.at[a * 7 + k],
                device_id=p, device_id_type=MESH))
    return out


def _own_slab(shard):
    x, y, c = _coords()
    z = lax.empty((NDEV,) + shard.shape, shard.dtype)
    return lax.dynamic_update_slice(z, shard[None], (_slot((x, y, c)),) + (0,) * shard.ndim)


def _split_start(name, srcs, lands, gather):
    n = len(srcs)

    def body(*refs):
        src_refs, land_refs = refs[:n], refs[n:2 * n]
        send_sems, recv_sems = refs[2 * n], refs[2 * n + 1]
        token = refs[-1]
        for cp in _split_copies(src_refs, land_refs, send_sems, recv_sems, gather, True):
            cp.start()
        token[...] = jnp.zeros_like(token)

    thru = [pltpu.HBM(a.shape, a.dtype) for a in list(srcs) + list(lands)]
    res = pl.pallas_call(
        body, name=name,
        out_shape=(pltpu.SemaphoreType.DMA((7 * n,)), pltpu.SemaphoreType.DMA((7 * n,)), *thru, SDS((8, 128), F32)),
        in_specs=[HBM] * (2 * n),
        out_specs=(SEM, SEM, *([HBM] * (2 * n)), pl.BlockSpec(memory_space=pltpu.VMEM)),
        input_output_aliases={i: 2 + i for i in range(2 * n)},
        compiler_params=pltpu.CompilerParams(has_side_effects=EFFECT),
    )(*[pltpu.with_memory_space_constraint(a, pltpu.HBM) for a in list(srcs) + list(lands)])
    return res[0], res[1], list(res[2:2 + n]), list(res[2 + n:2 + 2 * n]), res[-1]


def _split_wait(name, started, after, gather):
    send_sems, recv_sems, srcs, lands, _ = started
    n = len(srcs)

    def body(*refs):
        src_refs, land_refs = refs[:n], refs[n:2 * n]
        s_sems, r_sems = refs[2 * n], refs[2 * n + 1]
        for cp in _split_copies(src_refs, land_refs, s_sems, r_sems, gather, False):
            cp.wait_send()
            cp.wait_recv()

    thru = [pltpu.HBM(a.shape, a.dtype) for a in list(srcs) + list(lands)]
    res = pl.pallas_call(
        body, name=name, out_shape=tuple(thru),
        in_specs=[HBM] * (2 * n) + [SEM, SEM, ANY], out_specs=tuple([HBM] * (2 * n)),
        input_output_aliases={i: i for i in range(2 * n)},
        compiler_params=pltpu.CompilerParams(has_side_effects=EFFECT),
    )(*srcs, *lands, send_sems, recv_sems, after)
    return list(res[n:])


def _adam_math(w, g, m, v):
    m = ADAM_B1 * m + (1.0 - ADAM_B1) * g
    v = ADAM_B2 * v + (1.0 - ADAM_B2) * (g * g)
    m_hat = m / (1.0 - ADAM_B1 ** ADAM_STEP)
    v_hat = v / (1.0 - ADAM_B2 ** ADAM_STEP)
    delta = -ADAM_LR * (m_hat / (jnp.sqrt(v_hat) + ADAM_EPS) + ADAM_WD * w)
    return delta, m, v


def _adam_sharded(name, recv, sub, w, m, v):
    R, Cc = w.shape
    tr = max(t for t in range(16, R + 1, 16) if R % t == 0 and t * Cc <= 256 * 1024)

    def body(*refs):
        parts = refs[:NDEV]
        w_ref, m_ref, v_ref, g_out, d_out, m_out, v_out = refs[NDEV:]
        g = parts[0][...].astype(F32)
        for p in parts[1:]:
            g = g + p[...].astype(F32)
        delta, mn, vn = _adam_math(w_ref[...], g, m_ref[...], v_ref[...])
        g_out[...] = g
        d_out[...] = delta
        m_out[...] = mn
        v_out[...] = vn

    if sub is None:
        pspecs = [Block((None, tr, Cc), functools.partial(lambda s, i: (s, i, 0), s)) for s in range(NDEV)]
    else:
        pspecs = [Block((None, None, tr, Cc), functools.partial(lambda s, i: (s, sub, i, 0), s)) for s in range(NDEV)]
    row = Block((tr, Cc), lambda i: (i, 0))
    o = SDS((R, Cc), F32)
    return pl.pallas_call(
        body, name=name, grid=(R // tr,),
        in_specs=pspecs + [row, row, row], out_specs=[row, row, row, row], out_shape=[o, o, o, o],
        compiler_params=_cparams(("arbitrary",), VMEM_BIG))(*([recv] * NDEV), w, m, v)


def _adam_small(groups):
    n = len(groups)

    def body(*refs):
        ins, outs = refs[:4 * n], refs[4 * n:]
        for a in range(n):
            p_ref, w_ref, m_ref, v_ref = ins[4 * a:4 * a + 4]
            g = p_ref[0]
            for s in range(1, NDEV):
                g = g + p_ref[s]
            delta, mn, vn = _adam_math(w_ref[...], g, m_ref[...], v_ref[...])
            for o, r in zip(outs[4 * a:4 * a + 4], (g, delta, mn, vn)):
                o[...] = r

    flat_in = [t for grp in groups for t in grp]
    out_shape = [SDS(grp[1].shape, F32) for grp in groups for _ in range(4)]
    res = pl.pallas_call(body, name="adam_small", out_shape=out_shape,
                         compiler_params=_cparams(None, VMEM_BIG))(*flat_in)
    return [tuple(res[4 * a:4 * a + 4]) for a in range(n)]


_TINY = ["mix_norm", "s5_a_re", "s5_a_im", "s5_log_dt", "s5_d", "s5_b_glu", "sgu_ln_g", "sgu_ln_b",
         "sgu_b_s", "b_gate", "ffn2_norm", "final_norm"]
_ORDER = ["ffn1_norm", "ffn1_w_gate", "ffn1_w_up", "ffn1_w_down", "mix_norm", "w_in", "s5_a_re", "s5_a_im",
          "s5_log_dt", "s5_b_re", "s5_b_im", "s5_c_re", "s5_c_im", "s5_d", "s5_w_glu", "s5_b_glu", "sgu_ln_g",
          "sgu_ln_b", "sgu_w_s", "sgu_b_s", "w_branch_a", "w_branch_b", "w_gate", "b_gate", "w_out", "ffn2_norm",
          "ffn2_w_gate", "ffn2_w_up", "ffn2_w_down", "final_norm"]


def _step(x, tgt, W, M, V):
    T = x.shape[1]
    x0 = x[0]
    tgt0 = tgt[0]
    bf = lambda a: a.astype(BF16)

    def gather_start(name, shards):
        return _split_start(name, shards, [_own_slab(s) for s in shards], True)

    (wgu1,) = _all_gather("gather1", [jnp.stack([bf(W["ffn1_w_gate"][0].T), bf(W["ffn1_w_up"][0].T)])])

    lr_, li_ = W["s5_a_re"][0], W["s5_a_im"][0]
    ldt_ = W["s5_log_dt"][0][:, None]
    brt = jnp.transpose(W["s5_b_re"][0], (2, 0, 1))
    bit = jnp.transpose(W["s5_b_im"][0], (2, 0, 1))
    abr, abi, pwr, pwi, bkr_t, bki_t = _s5_prep(lr_, li_, ldt_, brt, bit)
    bdbr, bdbi = bf(_bd_b(bkr_t)), bf(_bd_b(bki_t))
    bdcr, bdci = bf(_bd_c(W["s5_c_re"][0])), bf(_bd_c(-W["s5_c_im"][0]))
    flat = lambda a: a.reshape(1, S5_NS)
    s5a = (_perm_matrix(), _perm_matrix().T, bdbr, bdbi, bdcr, bdci, flat(abr), flat(abi),
           pwr.reshape(S5_SEG, S5_NS), pwi.reshape(S5_SEG, S5_NS),
           W["s5_d"][0].reshape(1, S5_WIDTH))
    blk = jnp.arange(MLP_CHUNK) // CHUNK
    mask = blk[:, None] >= blk[None, :]
    wsm = jnp.where(mask[None], W["sgu_w_s"][0], 0.0)
    wsm_b, wsmt_b = bf(wsm), bf(jnp.transpose(wsm, (0, 2, 1)))
    bst = jnp.broadcast_to(W["sgu_b_s"][0][:, :, None], (SGU_HEADS, MLP_CHUNK, 128))
    bgate2 = W["b_gate"].reshape(2, 1, D_MODEL)

    h1 = _rms_fwd("rms1", x0, W["ffn1_norm"])
    dep = (wgu1[0, 0, :1, :1] * 0).astype(BF16)

    def later(a):
        return bf(a) + dep[0]

    gs2 = gather_start("gather2_start", [later(W["ffn1_w_down"][0])])
    ab1, f1 = _ffn_up("ffn1_up", h1, wgu1, gs2[4])
    (wd1,) = _split_wait("gather2_wait", gs2, f1, True)
    dep = (wd1[0, :1, :1] * 0).astype(BF16)
    gs3 = gather_start("gather3_start", [later(W["w_in"][0].T), later(W["s5_w_glu"][0])])
    x1 = _ffn_down("ffn1_down", f1, wd1, x0, after=gs3[4])
    h2 = _rms_fwd("rms2", x1, W["mix_norm"])
    win, wglu = _split_wait("gather3_wait", gs3, h2, True)
    wglu = wglu.reshape(S5_WIDTH, S5_WIDTH)
    s5c = s5a + (wglu, W["s5_b_glu"])
    dep = (win[0, :1, :1] * 0).astype(BF16)
    win = win.reshape(S5_WIDTH + 2 * SGU_WIDTH, D_MODEL)
    gs4 = gather_start("gather4_start", [later(W["w_gate"][0]), later(W["w_branch_a"][0]),
                                         later(W["w_branch_b"][0]), later(W["w_out"][0])])
    proj = _rows_fwd("w_in", h2, win, after=gs4[4])
    ya, xs, ypre = _s5_fwd(proj, *s5c)
    dep = (ya[:1, :1] * 0).astype(BF16)
    gs5 = gather_start("gather5_start", [jnp.stack([later(W["ffn2_w_gate"][0].T), later(W["ffn2_w_up"][0].T)])])
    yb = _sgu_fwd("sgu_fwd", proj, W["sgu_ln_g"] + gs5[4][:1, :1], W["sgu_ln_b"], wsm_b, bst)
    wgate, wba, wbb, wout = _split_wait("gather4_wait", gs4, yb, True)
    wout = wout.reshape(D_MODEL, D_MODEL)
    pa = _col_fwd("branch_a", ya, wba)
    pb = _col_fwd("branch_b", yb, wbb)
    gl = _gate_fwd("gate", h2, wgate, bgate2)
    merged = _merge_fwd("merge", pa, pb, gl)
    x2 = _plain_fwd_res("w_out", merged, wout, x1)
    h3 = _rms_fwd("rms3", x2, W["ffn2_norm"])
    (wgu2,) = _split_wait("gather5_wait", gs5, h3, True)
    dep = (wgu2[0, 0, :1, :1] * 0).astype(BF16)
    gs6 = gather_start("gather6_start", [later(W["ffn2_w_down"][0])])
    ab2, f2 = _ffn_up("ffn2_up", h3, wgu2, gs6[4])
    (wd2,) = _split_wait("gather6_wait", gs6, f2, True)
    x3 = _ffn_down("ffn2_down", f2, wd2, x2)
    loss_p, dx3b, dgf = _loss_head("loss_head", x3, W["final_norm"].reshape(1, D_MODEL), tgt0)

    def exchange_start(name, grads):
        x_, y_, c_ = _coords()
        me = _slot((x_, y_, c_))
        return _split_start(name, grads, [_own_slab(lax.dynamic_index_in_dim(g, me, 0, keepdims=False))
                                          for g in grads], False)

    dab2 = _ffn_down_bwd_act("ffn2_down_bwd_a", dx3b, wd2, ab2)
    g_wd2 = _ffn_down_bwd_w("ffn2_down_bwd_w", f2, dx3b)
    g_gu2 = _ffn_up_bwd_w("ffn2_up_bwd_w", h3, dab2)
    es1 = exchange_start("exchange1_start", [g_wd2, g_gu2])
    dh3 = _ffn_up_bwd_h("ffn2_up_bwd_h", dab2, wgu2, es1[4])
    dx2b, dg3 = _rms_bwd("rms3_bwd", dh3, x2, W["ffn2_norm"], dx3b, BF16)

    dmerged = _plain_bwd_a("w_out_bwd_a", dx2b, wout)
    g_wout = _plain_bwd_w("w_out_bwd_w", merged, dx2b)
    dpa, dpb, dgl, dbgate = _merge_bwd("merge_bwd", dmerged, pa, pb, gl)
    dya = _col_bwd_a("branch_a_bwd_a", dpa, wba)
    g_wba = _col_bwd_w("branch_a_bwd_w", ya, dpa, 256)
    dyb = _col_bwd_a("branch_b_bwd_a", dpb, wbb)
    g_wbb = _col_bwd_w("branch_b_bwd_w", yb, dpb, 256)
    dh2g = _gate_bwd_a("gate_bwd_a", dgl, wgate)
    g_wgate = _gate_bwd_w("gate_bwd_w", h2, dgl, 512)
    (dua, dar8, dai8, ddv, dbglu, dbdbr, dbdbi, dbdcr, dbdci, g_wglu) = _s5_bwd(dya, proj, ypre, xs, *s5c)
    dproj, dws, dbst, dlng, dlnb = _sgu_bwd("sgu_bwd", dyb, dua, proj, W["sgu_ln_g"], W["sgu_ln_b"],
                                            wsm_b, wsmt_b, bst)
    g_win = _col_bwd_w("w_in_bwd_w", h2, dproj, 384)
    g_wout3 = g_wout.reshape(NDEV, D_MODEL // NDEV, D_MODEL)
    g_wglu3 = g_wglu.astype(BF16).reshape(NDEV, S5_WIDTH // NDEV, S5_WIDTH)
    es2 = exchange_start("exchange2_start", [g_wout3, g_wba, g_wbb, g_wgate, g_wglu3, g_win])
    dh2 = _rows_bwd_a("w_in_bwd_a", dproj, win, dh2g)
    dx1b, dgm = _rms_bwd("rms2_bwd", dh2, x1, W["mix_norm"] + es2[4][:1, :1], dx2b, BF16)

    dabr = jnp.sum(dar8, axis=0).reshape(S5_GROUPS, S5_STATE)
    dabi = jnp.sum(dai8, axis=0).reshape(S5_GROUPS, S5_STATE)
    d_lr, d_li, d_ldt, d_brt, d_bit = _s5_prep_bwd(lr_, li_, ldt_, brt, bit, dabr, dabi,
                                                   _bd_b_extract(dbdbr), _bd_b_extract(dbdbi))
    small_g = {
        "mix_norm": dgm, "ffn2_norm": dg3, "final_norm": dgf,
        "s5_a_re": d_lr, "s5_a_im": d_li, "s5_log_dt": d_ldt,
        "s5_d": ddv, "s5_b_glu": dbglu, "sgu_ln_g": dlng, "sgu_ln_b": dlnb,
        "sgu_b_s": dbst[:, :, 0], "b_gate": dbgate,
    }
    to_cgp = lambda a: jnp.transpose(a[0], (2, 0, 1))
    from_cgp = lambda a: jnp.transpose(a, (1, 2, 0))[None]
    natural = [
        ("s5_b_re", d_brt, to_cgp, from_cgp), ("s5_b_im", d_bit, to_cgp, from_cgp),
        ("s5_c_re", _bd_c_extract(dbdcr), lambda a: a[0], lambda a: a[None]),
        ("s5_c_im", -_bd_c_extract(dbdci), lambda a: a[0], lambda a: a[None]),
        ("sgu_w_s", jnp.where(mask[None], dws, 0.0), lambda a: a[0], lambda a: a[None]),
    ]
    sizes = [W[n].size for n in _TINY]
    total = sum(sizes) + 1
    rows = -(-total // 128)
    rows = -(-rows // 8) * 8
    pad = rows * 128 - total

    def pack(d, extra):
        return jnp.concatenate([d[n].reshape(-1).astype(F32) for n in _TINY] + [extra, jnp.zeros((pad,), F32)]
                               ).reshape(rows, 128)

    gsm = gather_start("gather_small_start", [pack(small_g, loss_p[0, :1])] + [g for _, g, _, _ in natural])

    dab1 = _ffn_down_bwd_act("ffn1_down_bwd_a", dx1b, wd1, ab1, after=gsm[4])
    g_gu1 = _ffn_up_bwd_w("ffn1_up_bwd_w", h1, dab1)
    es3 = exchange_start("exchange3_start", [g_gu1])
    g_wd1 = _ffn_down_bwd_w("ffn1_down_bwd_w", f1, dx1b, after=es3[4])
    es4 = exchange_start("exchange4_start", [g_wd1])
    dh1 = _ffn_up_bwd_h("ffn1_up_bwd_h", dab1, wgu1, es4[4])
    dx0, dg1 = _rms_bwd("rms1_bwd", dh1, x0, W["ffn1_norm"], dx1b, F32)

    G, Dl, Mn, Vn = {}, {}, {}, {}

    def adam(plan):
        last = None
        for n, recv, sub in plan:
            if sub is None:
                g, d, mn, vn = _adam_sharded("adam_" + n, recv, sub, W[n][0], M[n][0], V[n][0])
                G[n], Dl[n], Mn[n], Vn[n] = g[None], d[None], mn[None], vn[None]
            else:
                tr = jnp.transpose
                g, d, mn, vn = _adam_sharded("adam_" + n, recv, sub, tr(W[n][0]), tr(M[n][0]), tr(V[n][0]))
                G[n], Dl[n], Mn[n], Vn[n] = tr(g)[None], tr(d)[None], tr(mn)[None], tr(vn)[None]
            last = g
        return last

    r_wd2, r_gu2 = _split_wait("exchange1_wait", es1, dx0, False)
    done = adam([("ffn2_w_down", r_wd2, None), ("ffn2_w_gate", r_gu2, 0), ("ffn2_w_up", r_gu2, 1)])
    r_wout, r_wba, r_wbb, r_wgate, r_wglu, r_win = _split_wait("exchange2_wait", es2, done, False)
    done = adam([("w_out", r_wout, None), ("w_branch_a", r_wba, None), ("w_branch_b", r_wbb, None),
                 ("w_gate", r_wgate, None), ("s5_w_glu", r_wglu, None), ("w_in", r_win, None)])

    late = dg1 + 0.0 * done.reshape(-1)[:1]
    zero1 = jnp.zeros((1,), F32)
    parts = _split_wait("gather_small_wait", gsm, late, True)
    (parts_g1,) = _all_gather("gather_ffn1_norm_grad", [late])
    groups = [(parts[0], pack(W, zero1), pack(M, zero1), pack(V, zero1))]
    groups += [(parts[1 + a], view(W[n]), view(M[n]), view(V[n])) for a, (n, _, view, _) in enumerate(natural)]
    groups += [(parts_g1, W["ffn1_norm"], M["ffn1_norm"], V["ffn1_norm"])]
    res = _adam_small(groups)
    sg, sd, sm, sv = res[0]
    for (n, _, _, back), (g, d, mn, vn) in zip(natural, res[1:-1]):
        G[n], Dl[n], Mn[n], Vn[n] = back(g), back(d), back(mn), back(vn)
    G["ffn1_norm"], Dl["ffn1_norm"], Mn["ffn1_norm"], Vn["ffn1_norm"] = res[-1]

    def unpack(flat2d, into):
        flat = flat2d.reshape(-1)
        off = 0
        for n, s in zip(_TINY, sizes):
            into[n] = flat[off:off + s].reshape(W[n].shape)
            off += s
        return flat[off]

    loss = unpack(sg, G)
    unpack(sd, Dl)
    unpack(sm, Mn)
    unpack(sv, Vn)

    (r_gu1,) = _split_wait("exchange3_wait", es3, sg, False)
    done = adam([("ffn1_w_gate", r_gu1, 0), ("ffn1_w_up", r_gu1, 1)])
    (r_wd1,) = _split_wait("exchange4_wait", es4, done, False)
    adam([("ffn1_w_down", r_wd1, None)])

    return loss, dx0[None], G, Dl, Mn, Vn


def kernel(x, ffn1_norm, ffn1_w_gate, ffn1_w_up, ffn1_w_down, mix_norm, w_in, s5_a_re, s5_a_im, s5_log_dt, s5_b_re, s5_b_im, s5_c_re, s5_c_im, s5_d, s5_w_glu, s5_b_glu, sgu_ln_g, sgu_ln_b, sgu_w_s, sgu_b_s, w_branch_a, w_branch_b, w_gate, b_gate, w_out, ffn2_norm, ffn2_w_gate, ffn2_w_up, ffn2_w_down, final_norm, loss_target, m_ffn1_norm, m_ffn1_w_gate, m_ffn1_w_up, m_ffn1_w_down, m_mix_norm, m_w_in, m_s5_a_re, m_s5_a_im, m_s5_log_dt, m_s5_b_re, m_s5_b_im, m_s5_c_re, m_s5_c_im, m_s5_d, m_s5_w_glu, m_s5_b_glu, m_sgu_ln_g, m_sgu_ln_b, m_sgu_w_s, m_sgu_b_s, m_w_branch_a, m_w_branch_b, m_w_gate, m_b_gate, m_w_out, m_ffn2_norm, m_ffn2_w_gate, m_ffn2_w_up, m_ffn2_w_down, m_final_norm, v_ffn1_norm, v_ffn1_w_gate, v_ffn1_w_up, v_ffn1_w_down, v_mix_norm, v_w_in, v_s5_a_re, v_s5_a_im, v_s5_log_dt, v_s5_b_re, v_s5_b_im, v_s5_c_re, v_s5_c_im, v_s5_d, v_s5_w_glu, v_s5_b_glu, v_sgu_ln_g, v_sgu_ln_b, v_sgu_w_s, v_sgu_b_s, v_w_branch_a, v_w_branch_b, v_w_gate, v_b_gate, v_w_out, v_ffn2_norm, v_ffn2_w_gate, v_ffn2_w_up, v_ffn2_w_down, v_final_norm):
    a = locals()
    W = {n: a[n] for n in _ORDER}
    M = {n: a["m_" + n] for n in _ORDER}
    V = {n: a["v_" + n] for n in _ORDER}
    loss, gx, G, Dl, Mn, Vn = _step(x, loss_target, W, M, V)
    return (loss, gx, *[G[n] for n in _ORDER], *[Dl[n] for n in _ORDER], *[Mn[n] for n in _ORDER],
            *[Vn[n] for n in _ORDER])
```

```python
import functools
import math

import jax
import jax.numpy as jnp
from jax import lax
from jax.experimental import pallas as pl
from jax.experimental.pallas import tpu as pltpu

F32 = jnp.float32
BF16 = jnp.bfloat16
NDEV = 8
NORM_EPS = 1e-6
D_MODEL = 2048
D_FF = 5632
FF_SHARD = D_FF // NDEV
S5_WIDTH = 1024
S5_GROUPS = 64
S5_GROUP_WIDTH = 16
S5_STATE = 64
S5_NS = S5_GROUPS * S5_STATE
SGU_WIDTH = 1024
SGU_HEADS = 8
MLP_CHUNK = 128
CHUNK = 64
ADAM_LR, ADAM_B1, ADAM_B2, ADAM_EPS, ADAM_WD, ADAM_STEP = 0.001, 0.9, 0.999, 1e-08, 0.01, 10
S5_TC = 256
S5_SEG = S5_TC // 8
S5_LG = 512
S5_UNROLL = True
VMEM_BIG = 56 * 1024 * 1024

MESH = pl.DeviceIdType.MESH
SDS = jax.ShapeDtypeStruct
Block = pl.BlockSpec
ANY = pl.BlockSpec(memory_space=pl.ANY)


def _cparams(sem=None, vmem=None):
    return pltpu.CompilerParams(dimension_semantics=sem, vmem_limit_bytes=vmem)


def _const(shape):
    nd = len(shape)
    return pl.BlockSpec(shape, lambda i: (0,) * nd, pipeline_mode=pl.Buffered(1))


def _sigmoid(x):
    return 0.5 * jnp.tanh(0.5 * x) + 0.5


_GELU_C = math.sqrt(2.0 / math.pi)


def _gelu(x):
    return 0.5 * x * (1.0 + jnp.tanh(_GELU_C * (x + 0.044715 * x * x * x)))


def _gelu_grad(x):
    t = jnp.tanh(_GELU_C * (x + 0.044715 * x * x * x))
    return 0.5 * (1.0 + t) + 0.5 * x * (1.0 - t * t) * _GELU_C * (1.0 + 3.0 * 0.044715 * x * x)


NN = (((1,), (0,)), ((), ()))
NT = (((1,), (1,)), ((), ()))
TN = (((0,), (0,)), ((), ()))


def _dot(a, b, dims=NN):
    return lax.dot_general(a, b, dims, preferred_element_type=F32)


def _matmul(name, a, b, extras, *, grid, a_spec, b_spec, extra_specs, out_shapes, out_specs, acc_shape,
            epilogue, dims=NN, nb=None, compute=None, after=None, vmem=VMEM_BIG):
    nk = grid[2]
    if after is not None:
        extras = tuple(extras) + (after,)
        extra_specs = list(extra_specs) + [Block((8, 128), lambda i, j, k: (0, 0))]
    ne, no = len(extras), len(out_shapes)
    nacc = nb or 1
    if compute is None:
        def compute(a_ref, b_ref, q):
            return _dot(a_ref[...], b_ref[q] if nb else b_ref[...], dims)

    def body(*refs):
        a_ref, b_ref = refs[0], refs[1]
        ex = refs[2:2 + ne]
        outs = refs[2 + ne:2 + ne + no]
        if nk == 1:
            epilogue([compute(a_ref, b_ref, q) for q in range(nacc)], ex, outs)
            return
        acc_ref = refs[2 + ne + no]
        k = pl.program_id(2)

        @pl.when(k == 0)
        def _():
            acc_ref[...] = jnp.zeros_like(acc_ref)

        for q in range(nacc):
            acc_ref[q] += compute(a_ref, b_ref, q)

        @pl.when(k == nk - 1)
        def _():
            epilogue([acc_ref[q] for q in range(nacc)], ex, outs)

    scratch = [] if nk == 1 else [pltpu.VMEM((nacc,) + tuple(acc_shape), F32)]
    res = pl.pallas_call(
        body, name=name, grid=grid,
        in_specs=[a_spec, b_spec] + list(extra_specs),
        out_specs=list(out_specs), out_shape=list(out_shapes), scratch_shapes=scratch,
        compiler_params=_cparams(("parallel", "parallel", "arbitrary"), vmem),
    )(a, b, *extras)
    return res


def _store(dtype_outs=None):
    def ep(accs, ex, outs):
        outs[0][...] = accs[0].astype(outs[0].dtype)
    return ep


def _tile(n, t):
    t = min(n, t)
    assert n % t == 0, (n, t)
    return t


def _ksum(kq, dims):
    def compute(a_ref, b_ref, _):
        part = _dot(a_ref[0], b_ref[0], dims)
        for q in range(1, kq):
            part = part + _dot(a_ref[q], b_ref[q], dims)
        return part
    return compute


def _ksum_lanes(kq, ns, dims):
    def compute(a_ref, b_ref, _):
        part = _dot(a_ref[:, 0:ns], b_ref[0], dims)
        for q in range(1, kq):
            part = part + _dot(a_ref[:, q * ns:(q + 1) * ns], b_ref[q], dims)
        return part
    return compute


def _wide_b(g):
    def compute(a_ref, b_ref, _):
        bw = b_ref[0] if g == 1 else jnp.concatenate([b_ref[q] for q in range(g)], axis=1)
        return _dot(a_ref[...], bw, NN)
    return compute


TT_DEEP = 2048
TT_FFN = 4096


HIDDEN = NDEV * FF_SHARD


def _ffn_up(name, h, wgu, after=None):
    T, D = h.shape
    tm = _tile(T, 1024)

    def ep(accs, ex, outs):
        a, b = accs
        outs[0][0] = a.astype(BF16)
        outs[0][1] = b.astype(BF16)
        outs[1][...] = (a * _sigmoid(a) * b).astype(BF16)

    return _matmul(
        name, wgu, h, (), after=after, grid=(NDEV, T // tm, 1),
        a_spec=Block((None, 2, FF_SHARD, D), lambda j, i, k: (j, 0, 0, 0)),
        b_spec=Block((tm, D), lambda j, i, k: (i, 0)),
        extra_specs=(),
        out_shapes=[SDS((NDEV, 2, FF_SHARD, T), BF16), SDS((NDEV, FF_SHARD, T), BF16)],
        out_specs=[Block((None, 2, FF_SHARD, tm), lambda j, i, k: (j, 0, 0, i)),
                   Block((None, FF_SHARD, tm), lambda j, i, k: (j, 0, i))],
        acc_shape=(FF_SHARD, tm), nb=2,
        compute=lambda a_ref, b_ref, q: _dot(a_ref[q], b_ref[...], NT), epilogue=ep)


def _ffn_down(name, f, wd, xres, after=None):
    T = f.shape[2]
    tm, tn, tk = _tile(T, 1024), 1024, HIDDEN // 2

    def ep(accs, ex, outs):
        outs[0][...] = ex[0][...] + 0.5 * accs[0]

    return _matmul(
        name, f.reshape(HIDDEN, T), wd.reshape(HIDDEN, D_MODEL), (xres,), after=after,
        grid=(T // tm, D_MODEL // tn, HIDDEN // tk),
        a_spec=Block((tk, tm), lambda i, j, k: (k, i)),
        b_spec=Block((tk, tn), lambda i, j, k: (k, j)),
        extra_specs=[Block((tm, tn), lambda i, j, k: (i, j))],
        out_shapes=[SDS((T, D_MODEL), F32)],
        out_specs=[Block((tm, tn), lambda i, j, k: (i, j))],
        acc_shape=(tm, tn), dims=TN, epilogue=ep)[0]


def _ffn_down_bwd_act(name, dyb, wd, ab, after=None):
    T, D = dyb.shape
    tm, g = _tile(T, 1024), 1

    def ep(accs, ex, outs):
        for s in range(g):
            df = accs[0][s * FF_SHARD:(s + 1) * FF_SHARD, :].astype(BF16)
            a = ex[0][s, 0]
            b = ex[0][s, 1]
            hs = 0.5 * _sigmoid(a)
            outs[0][s, 0] = df * b * hs * (1.0 + a * (1.0 - 2.0 * hs))
            outs[0][s, 1] = df * a * hs

    blk = Block((g, 2, FF_SHARD, tm), lambda i, j, k: (j, 0, 0, i))
    return _matmul(
        name, wd.reshape(HIDDEN, D), dyb, (ab,), after=after, grid=(T // tm, NDEV // g, 1),
        a_spec=Block((g * FF_SHARD, D), lambda i, j, k: (j, 0)),
        b_spec=Block((tm, D), lambda i, j, k: (i, 0)),
        extra_specs=[blk],
        out_shapes=[SDS((NDEV, 2, FF_SHARD, T), BF16)],
        out_specs=[blk],
        acc_shape=(g * FF_SHARD, tm), dims=NT, epilogue=ep)[0]


def _ffn_down_bwd_w(name, f, dyb, after=None):
    T = f.shape[2]
    tt, tn, tr = _tile(T, TT_DEEP), 1024, 2 * FF_SHARD

    def ep(accs, ex, outs):
        outs[0][...] = (0.5 * accs[0]).astype(BF16)

    return _matmul(
        name, f.reshape(HIDDEN, T), dyb, (), after=after, grid=(HIDDEN // tr, D_MODEL // tn, T // tt),
        a_spec=Block((tr, tt), lambda j, n, k: (j, k)),
        b_spec=Block((tt, tn), lambda j, n, k: (k, n)),
        extra_specs=(),
        out_shapes=[SDS((HIDDEN, D_MODEL), BF16)],
        out_specs=[Block((tr, tn), lambda j, n, k: (j, n))],
        acc_shape=(tr, tn), dims=NN, epilogue=ep)[0].reshape(NDEV, FF_SHARD, D_MODEL)


def _ffn_up_bwd_h(name, dab, wgu, after):
    T = dab.shape[3]
    tm, tn, tk = _tile(T, 1024), 2048, HIDDEN // 2
    return _matmul(
        name, dab.reshape(2 * HIDDEN, T), wgu.reshape(2 * HIDDEN, D_MODEL), (), after=after,
        grid=(T // tm, D_MODEL // tn, 2 * HIDDEN // tk),
        a_spec=Block((tk, tm), lambda i, j, k: (k, i)),
        b_spec=Block((tk, tn), lambda i, j, k: (k, j)),
        extra_specs=(),
        out_shapes=[SDS((T, D_MODEL), BF16)],
        out_specs=[Block((tm, tn), lambda i, j, k: (i, j))],
        acc_shape=(tm, tn), dims=TN, epilogue=_store())[0]


def _ffn_up_bwd_w(name, h, dab):
    T, D = h.shape
    tt, tn = _tile(T, TT_FFN), 1024

    def ep(accs, ex, outs):
        outs[0][0] = accs[0].astype(BF16)
        outs[0][1] = accs[1].astype(BF16)

    return _matmul(
        name, dab, h, (), grid=(NDEV, D // tn, T // tt),
        a_spec=Block((None, 2, FF_SHARD, tt), lambda j, n, k: (j, 0, 0, k)),
        b_spec=Block((tt, tn), lambda j, n, k: (k, n)),
        extra_specs=(),
        out_shapes=[SDS((NDEV, 2, FF_SHARD, D), BF16)],
        out_specs=[Block((None, 2, FF_SHARD, tn), lambda j, n, k: (j, 0, 0, n))],
        acc_shape=(FF_SHARD, tn), nb=2,
        compute=lambda a_ref, b_ref, q: _dot(a_ref[q], b_ref[...], NN), epilogue=ep)[0]


def _shards_per_step(ns):
    return max(g for g in (1, 2, 4, 8) if g * ns <= 2048)


def _split_lanes(g, ns):
    def ep(accs, ex, outs):
        for q in range(g):
            outs[0][q] = accs[0][:, q * ns:(q + 1) * ns].astype(outs[0].dtype)
    return ep


def _rows_fwd(name, a, wt, after=None):
    T, K = a.shape
    N = wt.shape[0]
    tm, tn = _tile(T, 1024), _tile(N, 1536)
    return _matmul(
        name, a, wt, (), after=after, grid=(N // tn, T // tm, 1),
        a_spec=Block((tm, K), lambda j, i, k: (i, 0)),
        b_spec=Block((tn, K), lambda j, i, k: (j, 0)),
        extra_specs=(),
        out_shapes=[SDS((T, N), BF16)],
        out_specs=[Block((tm, tn), lambda j, i, k: (i, j))],
        acc_shape=(tm, tn), dims=NT, epilogue=_store())[0]


def _rows_bwd_a(name, dy, wt, add):
    T, N = dy.shape
    K = wt.shape[1]
    tm, tn = _tile(T, 1024), _tile(K, 1024)

    def ep(accs, ex, outs):
        outs[0][...] = (accs[0] + ex[0][...].astype(F32)).astype(BF16)

    return _matmul(
        name, dy, wt, (add,), grid=(T // tm, K // tn, 1),
        a_spec=Block((tm, N), lambda i, j, k: (i, 0)),
        b_spec=Block((N, tn), lambda i, j, k: (0, j)),
        extra_specs=[Block((tm, tn), lambda i, j, k: (i, j))],
        out_shapes=[SDS((T, K), BF16)],
        out_specs=[Block((tm, tn), lambda i, j, k: (i, j))],
        acc_shape=(tm, tn), dims=NN, epilogue=ep)[0]


def _col_fwd(name, a, w, out_dtype=BF16, after=None):
    T, K = a.shape
    ns = w.shape[2]
    g = _shards_per_step(ns)
    tm = _tile(T, 1024)
    return _matmul(
        name, a, w, (), after=after, grid=(NDEV // g, T // tm, 1),
        a_spec=Block((tm, K), lambda j, i, k: (i, 0)),
        b_spec=Block((g, K, ns), lambda j, i, k: (j, 0, 0)),
        extra_specs=(),
        out_shapes=[SDS((T, NDEV * ns), out_dtype)],
        out_specs=[Block((tm, g * ns), lambda j, i, k: (i, j))],
        acc_shape=(tm, g * ns), compute=_wide_b(g), epilogue=_store())[0]


def _col_bwd_a(name, dy, w, add=None):
    T = dy.shape[0]
    _, K, ns = w.shape
    tm, tn = _tile(T, 1024), _tile(K, 1024)

    def ep(accs, ex, outs):
        r = accs[0]
        if add is not None:
            r = r + ex[0][...].astype(F32)
        outs[0][...] = r.astype(BF16)

    extras = () if add is None else (add,)
    return _matmul(
        name, dy, w, extras, grid=(T // tm, K // tn, 1),
        a_spec=Block((tm, NDEV * ns), lambda i, j, k: (i, 0)),
        b_spec=Block((NDEV, tn, ns), lambda i, j, k: (0, j, 0)),
        extra_specs=[Block((tm, tn), lambda i, j, k: (i, j))] * len(extras),
        out_shapes=[SDS((T, K), BF16)],
        out_specs=[Block((tm, tn), lambda i, j, k: (i, j))],
        acc_shape=(tm, tn), compute=_ksum_lanes(NDEV, ns, NT), epilogue=ep)[0]


def _col_bwd_w(name, a, dy, ns):
    T, K = a.shape
    g = _shards_per_step(ns)
    tt, tr = _tile(T, TT_DEEP), _tile(K, 1024)
    return _matmul(
        name, a, dy, (), grid=(NDEV // g, K // tr, T // tt),
        a_spec=Block((tt, tr), lambda j, n, k: (k, n)),
        b_spec=Block((tt, g * ns), lambda j, n, k: (k, j)),
        extra_specs=(),
        out_shapes=[SDS((NDEV, K, ns), BF16)],
        out_specs=[Block((g, tr, ns), lambda j, n, k: (j, n, 0))],
        acc_shape=(tr, g * ns), dims=TN, epilogue=_split_lanes(g, ns))[0]


def _gate_fwd(name, h, w, bias):
    T, K = h.shape
    ns = w.shape[2]
    g = 2
    per = D_MODEL // (g * ns)
    tm = _tile(T, 1024)

    def ep(accs, ex, outs):
        outs[0][...] = (accs[0] + ex[0][...]).astype(BF16)

    return _matmul(
        name, h, w, (bias,), grid=(NDEV // g, T // tm, 1),
        a_spec=Block((tm, K), lambda j, i, k: (i, 0)),
        b_spec=Block((g, K, ns), lambda j, i, k: (j, 0, 0)),
        extra_specs=[Block((None, 1, g * ns), lambda j, i, k: (j // per, 0, j % per))],
        out_shapes=[SDS((2, T, D_MODEL), BF16)],
        out_specs=[Block((None, tm, g * ns), lambda j, i, k: (j // per, i, j % per))],
        acc_shape=(tm, g * ns), compute=_wide_b(g), epilogue=ep)[0]


def _gate_bwd_a(name, dgl, w):
    _, T, _ = dgl.shape
    _, K, ns = w.shape
    per = D_MODEL // ns
    tm, tn = _tile(T, 1024), 1024

    def compute(a_ref, b_ref, _):
        part = None
        for q in range(NDEV):
            d = _dot(a_ref[q // per, :, (q % per) * ns:(q % per + 1) * ns], b_ref[q], NT)
            part = d if part is None else part + d
        return part

    return _matmul(
        name, dgl, w, (), grid=(T // tm, K // tn, 1),
        a_spec=Block((2, tm, D_MODEL), lambda i, j, k: (0, i, 0)),
        b_spec=Block((NDEV, tn, ns), lambda i, j, k: (0, j, 0)),
        extra_specs=(),
        out_shapes=[SDS((T, K), BF16)],
        out_specs=[Block((tm, tn), lambda i, j, k: (i, j))],
        acc_shape=(tm, tn), compute=compute, epilogue=_store())[0]


def _gate_bwd_w(name, h, dgl, ns):
    T, K = h.shape
    g = 2
    per = D_MODEL // (g * ns)
    tt, tr = _tile(T, TT_DEEP), 1024
    return _matmul(
        name, h, dgl, (), grid=(NDEV // g, K // tr, T // tt),
        a_spec=Block((tt, tr), lambda j, n, k: (k, n)),
        b_spec=Block((None, tt, g * ns), lambda j, n, k: (j // per, k, j % per)),
        extra_specs=(),
        out_shapes=[SDS((NDEV, K, ns), BF16)],
        out_specs=[Block((g, tr, ns), lambda j, n, k: (j, n, 0))],
        acc_shape=(tr, g * ns), dims=TN, epilogue=_split_lanes(g, ns))[0]


def _plain_fwd_res(name, a, w, xres):
    T, K = a.shape
    N = w.shape[1]
    tm, tn = _tile(T, 1024), _tile(N, 1024)

    def ep(accs, ex, outs):
        outs[0][...] = ex[0][...] + accs[0]

    return _matmul(
        name, a, w, (xres,), grid=(T // tm, N // tn, 1),
        a_spec=Block((tm, K), lambda i, j, k: (i, 0)),
        b_spec=Block((K, tn), lambda i, j, k: (0, j)),
        extra_specs=[Block((tm, tn), lambda i, j, k: (i, j))],
        out_shapes=[SDS((T, N), F32)],
        out_specs=[Block((tm, tn), lambda i, j, k: (i, j))],
        acc_shape=(tm, tn), dims=NN, nb=None, epilogue=ep)[0]


def _plain_bwd_a(name, dy, w):
    T, N = dy.shape
    K = w.shape[0]
    tm, tn = _tile(T, 1024), _tile(K, 1024)
    return _matmul(
        name, dy, w, (), grid=(T // tm, K // tn, 1),
        a_spec=Block((tm, N), lambda i, j, k: (i, 0)),
        b_spec=Block((tn, N), lambda i, j, k: (j, 0)),
        extra_specs=(),
        out_shapes=[SDS((T, K), BF16)],
        out_specs=[Block((tm, tn), lambda i, j, k: (i, j))],
        acc_shape=(tm, tn), dims=NT, nb=None, epilogue=_store())[0]


def _plain_bwd_w(name, a, dy):
    T, K = a.shape
    N = dy.shape[1]
    tt, tr, tn = _tile(T, TT_DEEP), _tile(K, 1024), _tile(N, 1024)
    return _matmul(
        name, a, dy, (), grid=(K // tr, N // tn, T // tt),
        a_spec=Block((tt, tr), lambda m, n, k: (k, m)),
        b_spec=Block((tt, tn), lambda m, n, k: (k, n)),
        extra_specs=(),
        out_shapes=[SDS((K, N), BF16)],
        out_specs=[Block((tr, tn), lambda m, n, k: (m, n))],
        acc_shape=(tr, tn), dims=TN, nb=None, epilogue=_store())[0]


def _rms_fwd(name, x, g):
    T, D = x.shape
    tm = _tile(T, 512)

    def body(x_ref, g_ref, h_ref):
        xv = x_ref[...]
        r = lax.rsqrt(jnp.mean(xv * xv, axis=-1, keepdims=True) + NORM_EPS)
        h_ref[...] = (xv * r * g_ref[...]).astype(BF16)

    return pl.pallas_call(
        body, name=name, grid=(T // tm,),
        in_specs=[Block((tm, D), lambda i: (i, 0)), Block((1, D), lambda i: (0, 0))],
        out_specs=Block((tm, D), lambda i: (i, 0)), out_shape=SDS((T, D), BF16),
        compiler_params=_cparams(("arbitrary",), VMEM_BIG))(x, g)


def _rms_bwd(name, dh, x, g, dxin, out_dtype):
    T, D = x.shape
    tm = _tile(T, 512)

    def body(dh_ref, x_ref, g_ref, dxin_ref, dx_ref, dg_ref):
        i = pl.program_id(0)
        xv = x_ref[...]
        dh = dh_ref[...].astype(F32)
        r = lax.rsqrt(jnp.mean(xv * xv, axis=-1, keepdims=True) + NORM_EPS)
        xh = xv * r
        gd = dh * g_ref[...]
        dx = dxin_ref[...].astype(F32) + r * (gd - xh * jnp.mean(gd * xh, axis=-1, keepdims=True))
        dx_ref[...] = dx.astype(out_dtype)
        dgp = jnp.sum(dh * xh, axis=0, keepdims=True)

        @pl.when(i == 0)
        def _():
            dg_ref[...] = dgp

        @pl.when(i > 0)
        def _():
            dg_ref[...] += dgp

    row = Block((tm, D), lambda i: (i, 0))
    vec = Block((1, D), lambda i: (0, 0))
    return pl.pallas_call(
        body, name=name, grid=(T // tm,),
        in_specs=[row, row, vec, row], out_specs=[row, vec],
        out_shape=[SDS((T, D), out_dtype), SDS((1, D), F32)],
        compiler_params=_cparams(("arbitrary",), VMEM_BIG))(dh, x, g, dxin)


def _loss_head(name, x, g, tgt):
    T, D = x.shape
    tm = _tile(T, 512)

    def body(x_ref, g_ref, t_ref, loss_ref, dxb_ref, dg_ref):
        i = pl.program_id(0)
        xv = x_ref[...]
        gv = g_ref[...]
        r = lax.rsqrt(jnp.mean(xv * xv, axis=-1, keepdims=True) + NORM_EPS)
        xh = xv * r
        err = xh * gv - t_ref[...]
        lp = 0.5 * jnp.sum(jnp.mean(err * err, axis=-1, keepdims=True), axis=0, keepdims=True)
        dout = err * (1.0 / D)
        gd = dout * gv
        dx = r * (gd - xh * jnp.mean(gd * xh, axis=-1, keepdims=True))
        dxb_ref[...] = dx.astype(BF16)
        dgp = jnp.sum(dout * xh, axis=0, keepdims=True)
        lpb = jnp.broadcast_to(lp, (1, 128))

        @pl.when(i == 0)
        def _():
            dg_ref[...] = dgp
            loss_ref[...] = lpb

        @pl.when(i > 0)
        def _():
            dg_ref[...] += dgp
            loss_ref[...] += lpb

    row = Block((tm, D), lambda i: (i, 0))
    vec = Block((1, D), lambda i: (0, 0))
    return pl.pallas_call(
        body, name=name, grid=(T // tm,),
        in_specs=[row, vec, row], out_specs=[Block((1, 128), lambda i: (0, 0)), row, vec],
        out_shape=[SDS((1, 128), F32), SDS((T, D), BF16), SDS((1, D), F32)],
        compiler_params=_cparams(("arbitrary",), VMEM_BIG))(x, g, tgt)


def _merge_fwd(name, pa, pb, gl):
    T, D = pa.shape
    tm = _tile(T, 512)

    def body(pa_ref, pb_ref, gl_ref, o_ref):
        ga = _sigmoid(gl_ref[0].astype(F32))
        gb = _sigmoid(gl_ref[1].astype(F32))
        o_ref[...] = (ga * pa_ref[...].astype(F32) + gb * pb_ref[...].astype(F32)).astype(BF16)

    row = Block((tm, D), lambda i: (i, 0))
    return pl.pallas_call(
        body, name=name, grid=(T // tm,),
        in_specs=[row, row, Block((2, tm, D), lambda i: (0, i, 0))], out_specs=row,
        out_shape=SDS((T, D), BF16), compiler_params=_cparams(("arbitrary",), VMEM_BIG))(pa, pb, gl)


def _merge_bwd(name, dm, pa, pb, gl):
    T, D = pa.shape
    tm = _tile(T, 512)

    def body(dm_ref, pa_ref, pb_ref, gl_ref, dpa_ref, dpb_ref, dgl_ref, db_ref):
        i = pl.program_id(0)
        dmv = dm_ref[...].astype(F32)
        ga = _sigmoid(gl_ref[0].astype(F32))
        gb = _sigmoid(gl_ref[1].astype(F32))
        dpa_ref[...] = (dmv * ga).astype(BF16)
        dpb_ref[...] = (dmv * gb).astype(BF16)
        dga = dmv * pa_ref[...].astype(F32) * ga * (1.0 - ga)
        dgb = dmv * pb_ref[...].astype(F32) * gb * (1.0 - gb)
        dgl_ref[0] = dga.astype(BF16)
        dgl_ref[1] = dgb.astype(BF16)
        sa = jnp.sum(dga, axis=0, keepdims=True)
        sb = jnp.sum(dgb, axis=0, keepdims=True)

        @pl.when(i == 0)
        def _():
            db_ref[0] = sa
            db_ref[1] = sb

        @pl.when(i > 0)
        def _():
            db_ref[0] += sa
            db_ref[1] += sb

    row = Block((tm, D), lambda i: (i, 0))
    two = Block((2, tm, D), lambda i: (0, i, 0))
    return pl.pallas_call(
        body, name=name, grid=(T // tm,),
        in_specs=[row, row, row, two], out_specs=[row, row, two, Block((2, 1, D), lambda i: (0, 0, 0))],
        out_shape=[SDS((T, D), BF16), SDS((T, D), BF16), SDS((2, T, D), BF16), SDS((2, 1, D), F32)],
        compiler_params=_cparams(("arbitrary",), VMEM_BIG))(dm, pa, pb, gl)


def _sgu_core(ur, vr, lng, lnb, ws_ref, bs_ref):
    tm = ur.shape[0]
    gu = _gelu(ur)
    gv = _gelu(vr)
    mu = jnp.mean(gv, axis=-1, keepdims=True)
    cen = gv - mu
    rstd = lax.rsqrt(jnp.mean(cen * cen, axis=-1, keepdims=True) + NORM_EPS)
    xhat = cen * rstd
    vn = (xhat * lng + lnb).astype(BF16)
    rows = []
    for n in range(tm // MLP_CHUNK):
        cols = []
        for h in range(SGU_HEADS):
            blk = vn[n * MLP_CHUNK:(n + 1) * MLP_CHUNK, h * 128:(h + 1) * 128]
            cols.append(_dot(ws_ref[h], blk) + bs_ref[h])
        rows.append(jnp.concatenate(cols, axis=1))
    mixed = jnp.concatenate(rows, axis=0) if len(rows) > 1 else rows[0]
    return gu, xhat, rstd, vn, mixed


def _sgu_fwd(name, proj, lng, lnb, wsm, bst):
    T = proj.shape[0]
    W = SGU_WIDTH
    tm = _tile(T, 512)

    def body(u_ref, v_ref, lng_ref, lnb_ref, ws_ref, bs_ref, o_ref):
        gu, _, _, _, mixed = _sgu_core(u_ref[...].astype(F32), v_ref[...].astype(F32), lng_ref[...], lnb_ref[...],
                                       ws_ref, bs_ref)
        o_ref[...] = (gu * mixed).astype(BF16)

    vec = Block((1, W), lambda i: (0, 0))
    return pl.pallas_call(
        body, name=name, grid=(T // tm,),
        in_specs=[Block((tm, W), lambda i: (i, 1)), Block((tm, W), lambda i: (i, 2)), vec, vec,
                  Block((SGU_HEADS, 128, 128), lambda i: (0, 0, 0)), Block((SGU_HEADS, 128, 128), lambda i: (0, 0, 0))],
        out_specs=Block((tm, W), lambda i: (i, 0)), out_shape=SDS((T, W), BF16),
        compiler_params=_cparams(("arbitrary",), VMEM_BIG))(proj, proj, lng, lnb, wsm, bst)


def _sgu_bwd(name, dyb, dua, proj, lng, lnb, wsm, wsmt, bst):
    T = proj.shape[0]
    W = SGU_WIDTH
    tm = _tile(T, 512)

    def body(dy_ref, dua_ref, u_ref, v_ref, lng_ref, lnb_ref, ws_ref, wst_ref, bs_ref,
             duv_ref, dws_ref, dbs_ref, dlng_ref, dlnb_ref):
        i = pl.program_id(0)
        duv_ref[:, :W] = dua_ref[...]
        ur = u_ref[...].astype(F32)
        vr = v_ref[...].astype(F32)
        lng_v = lng_ref[...]
        gu, xhat, rstd, vn, mixed = _sgu_core(ur, vr, lng_v, lnb_ref[...], ws_ref, bs_ref)
        dy = dy_ref[...].astype(F32)
        dgu = dy * mixed
        dmix = dy * gu
        dmb = dmix.astype(BF16)
        dws_p, dbs_p, rows = [], [], []
        for h in range(SGU_HEADS):
            acc_w = jnp.zeros((128, 128), F32)
            acc_b = jnp.zeros((128, 1), F32)
            for n in range(tm // MLP_CHUNK):
                r0 = n * MLP_CHUNK
                dmt = dmb[r0:r0 + MLP_CHUNK, h * 128:(h + 1) * 128]
                acc_w = acc_w + _dot(dmt, vn[r0:r0 + MLP_CHUNK, h * 128:(h + 1) * 128], NT)
                acc_b = acc_b + jnp.sum(dmix[r0:r0 + MLP_CHUNK, h * 128:(h + 1) * 128], axis=1, keepdims=True)
            dws_p.append(acc_w)
            dbs_p.append(jnp.broadcast_to(acc_b, (128, 128)))
        for n in range(tm // MLP_CHUNK):
            r0 = n * MLP_CHUNK
            rows.append(jnp.concatenate(
                [_dot(wst_ref[h], dmb[r0:r0 + MLP_CHUNK, h * 128:(h + 1) * 128]) for h in range(SGU_HEADS)], axis=1))
        dvn = jnp.concatenate(rows, axis=0) if len(rows) > 1 else rows[0]
        dlng_p = jnp.sum(dvn * xhat, axis=0, keepdims=True)
        dlnb_p = jnp.sum(dvn, axis=0, keepdims=True)
        dxh = dvn * lng_v
        dgv = rstd * (dxh - jnp.mean(dxh, axis=-1, keepdims=True)
                      - xhat * jnp.mean(dxh * xhat, axis=-1, keepdims=True))
        duv_ref[:, W:2 * W] = (dgu * _gelu_grad(ur)).astype(BF16)
        duv_ref[:, 2 * W:] = (dgv * _gelu_grad(vr)).astype(BF16)

        @pl.when(i == 0)
        def _():
            for h in range(SGU_HEADS):
                dws_ref[h] = dws_p[h]
                dbs_ref[h] = dbs_p[h]
            dlng_ref[...] = dlng_p
            dlnb_ref[...] = dlnb_p

        @pl.when(i > 0)
        def _():
            for h in range(SGU_HEADS):
                dws_ref[h] += dws_p[h]
                dbs_ref[h] += dbs_p[h]
            dlng_ref[...] += dlng_p
            dlnb_ref[...] += dlnb_p

    vec = Block((1, W), lambda i: (0, 0))
    wsb = Block((SGU_HEADS, 128, 128), lambda i: (0, 0, 0))
    hsq = SDS((SGU_HEADS, 128, 128), F32)
    return pl.pallas_call(
        body, name=name, grid=(T // tm,),
        in_specs=[Block((tm, W), lambda i: (i, 0)), Block((tm, W), lambda i: (i, 0)),
                  Block((tm, W), lambda i: (i, 1)), Block((tm, W), lambda i: (i, 2)),
                  vec, vec, wsb, wsb, wsb],
        out_specs=[Block((tm, 3 * W), lambda i: (i, 0)), wsb, wsb, vec, vec],
        out_shape=[SDS((T, 3 * W), BF16), hsq, hsq, SDS((1, W), F32), SDS((1, W), F32)],
        compiler_params=_cparams(("arbitrary",), VMEM_BIG))(dyb, dua, proj, proj, lng, lnb, wsm, wsmt, bst)


def _s5_disc(lr, li, ldt, brt, bit):
    dt = jnp.exp(ldt)
    decay = jnp.exp(lr * dt)
    abr = decay * jnp.cos(li * dt)
    abi = decay * jnp.sin(li * dt)
    denom = lr * lr + li * li
    nr = abr - 1.0
    ni = abi
    kr = (nr * lr + ni * li) / denom
    ki = (ni * lr - nr * li) / denom
    bkr = kr[None] * brt - ki[None] * bit
    bki = kr[None] * bit + ki[None] * brt
    return abr, abi, bkr, bki


def _s5_prep(lr, li, ldt, brt, bit):
    G, P, C = S5_GROUPS, S5_STATE, S5_GROUP_WIDTH

    def body(lr_ref, li_ref, ldt_ref, br_ref, bi_ref, abr_ref, abi_ref, pwr_ref, pwi_ref, bkr_ref, bki_ref):
        lr_, li_, ldt_ = lr_ref[...], li_ref[...], ldt_ref[...]
        res = _s5_disc(lr_, li_, ldt_, br_ref[...], bi_ref[...])
        for o, r in zip((abr_ref, abi_ref, bkr_ref, bki_ref), res):
            o[...] = r
        dt = jnp.exp(ldt_)
        n = lax.broadcasted_iota(jnp.int32, (S5_SEG, G, P), 0).astype(F32) + 1.0
        dec = jnp.exp((lr_ * dt)[None] * n)
        ang = (li_ * dt)[None] * n
        pwr_ref[...] = dec * jnp.cos(ang)
        pwi_ref[...] = dec * jnp.sin(ang)

    gp = SDS((G, P), F32)
    sgp = SDS((S5_SEG, G, P), F32)
    cgp = SDS((C, G, P), F32)
    return pl.pallas_call(body, name="s5_prep", out_shape=[gp, gp, sgp, sgp, cgp, cgp])(lr, li, ldt, brt, bit)


def _s5_prep_bwd(lr, li, ldt, brt, bit, dabr, dabi, dbkr, dbki):
    G, P, C = S5_GROUPS, S5_STATE, S5_GROUP_WIDTH

    def body(lr_ref, li_ref, ldt_ref, br_ref, bi_ref, dabr_ref, dabi_ref, dbkr_ref, dbki_ref,
             o_lr, o_li, o_ldt, o_br, o_bi):
        _, pull = jax.vjp(_s5_disc, lr_ref[...], li_ref[...], ldt_ref[...], br_ref[...], bi_ref[...])
        g = pull((dabr_ref[...], dabi_ref[...], dbkr_ref[...], dbki_ref[...]))
        for o, r in zip((o_lr, o_li, o_ldt, o_br, o_bi), g):
            o[...] = r

    gp = SDS((G, P), F32)
    cgp = SDS((C, G, P), F32)
    return pl.pallas_call(body, name="s5_prep_bwd", out_shape=[gp, gp, SDS((G, 1), F32), cgp, cgp])(
        lr, li, ldt, brt, bit, dabr, dabi, dbkr, dbki)


def _s5_scan(buf_ref, ar_row, ai_row, pwr_ref, pwi_ref, carry_ref, LG, xs_ref=None, dar_ref=None, dai_ref=None):
    reverse = xs_ref is not None
    NS, SEG = S5_NS, S5_SEG
    sgn = -1.0 if reverse else 1.0
    for lg in range(NS // LG):
        cr = slice(lg * LG, (lg + 1) * LG)
        ci = slice(NS + lg * LG, NS + (lg + 1) * LG)
        ar1, ai1 = ar_row[:, cr], sgn * ai_row[:, cr]
        asr1, asi1 = pwr_ref[SEG - 1:SEG, cr], sgn * pwi_ref[SEG - 1:SEG, cr]
        ar = jnp.broadcast_to(ar1, (8, LG))
        ai = jnp.broadcast_to(ai1, (8, LG))

        def step_of(j):
            return (SEG - 1 - j) if reverse else j

        def p1(j, st):
            sr, si = st
            rows = pl.ds(pl.multiple_of(step_of(j) * 8, 8), 8)
            nr = ar * sr - ai * si + buf_ref[rows, cr]
            ni = ar * si + ai * sr + buf_ref[rows, ci]
            buf_ref[rows, cr] = nr
            buf_ref[rows, ci] = ni
            return nr, ni

        z = jnp.zeros((8, LG), F32)
        er, ei = lax.fori_loop(0, SEG, p1, (z, z), unroll=S5_UNROLL)
        c_r = carry_ref[:, cr]
        c_i = carry_ref[:, ci]
        cs_r, cs_i = [None] * 8, [None] * 8
        order = range(7, -1, -1) if reverse else range(8)
        for s in order:
            cs_r[s], cs_i[s] = c_r, c_i
            e_r, e_i = er[s:s + 1], ei[s:s + 1]
            c_r, c_i = e_r + asr1 * c_r - asi1 * c_i, e_i + asr1 * c_i + asi1 * c_r
        carry_ref[:, cr] = c_r
        carry_ref[:, ci] = c_i
        cmr = jnp.concatenate(cs_r, axis=0)
        cmi = jnp.concatenate(cs_i, axis=0)

        def carried(j):
            pr = pwr_ref[pl.ds(j, 1), cr]
            pi = sgn * pwi_ref[pl.ds(j, 1), cr]
            return pr * cmr - pi * cmi, pr * cmi + pi * cmr

        if not reverse:
            def p2(j, st):
                rows = pl.ds(pl.multiple_of(j * 8, 8), 8)
                wr, wi = carried(j)
                buf_ref[rows, cr] += wr
                buf_ref[rows, ci] += wi
                return st

            lax.fori_loop(0, SEG, p2, 0, unroll=S5_UNROLL)
        else:
            def p2(j, st):
                pr, pi, dr, di = st
                rows = pl.ds(pl.multiple_of(step_of(j) * 8, 8), 8)
                xr = xs_ref[rows, cr]
                xi = xs_ref[rows, ci]
                dr = dr + pr * xr + pi * xi
                di = di + pi * xr - pr * xi
                wr, wi = carried(j)
                gr = buf_ref[rows, cr] + wr
                gi = buf_ref[rows, ci] + wi
                buf_ref[rows, cr] = gr
                buf_ref[rows, ci] = gi
                return gr, gi, dr, di

            st = lax.fori_loop(0, SEG, p2, (cmr, cmi, z, z), unroll=S5_UNROLL)
            dar_ref[:, cr] += st[2]
            dai_ref[:, cr] += st[3]


def _s5_fwd(proj, perm, permt, bdbr, bdbi, bdcr, bdci, abr, abi, asr, asi, dvec, wglu, bglu):
    T = proj.shape[0]
    TC, NS, W = S5_TC, S5_NS, S5_WIDTH
    nc = T // TC

    def body(u_ref, pm_ref, pmt_ref, bdbr_ref, bdbi_ref, bdcr_ref, bdci_ref, ar_ref, ai_ref, asr_ref, asi_ref,
             d_ref, wglu_ref, bglu_ref, ya_ref, xs_ref, ypre_ref, carry_ref):
        i = pl.program_id(0)

        @pl.when(i == 0)
        def _():
            carry_ref[...] = jnp.zeros_like(carry_ref)

        up = _dot(pm_ref[...], u_ref[...]).astype(BF16)
        for j in range(8):
            ut = up[:, j * 128:(j + 1) * 128]
            xs_ref[:, j * 512:(j + 1) * 512] = _dot(ut, bdbr_ref[j])
            xs_ref[:, NS + j * 512:NS + (j + 1) * 512] = _dot(ut, bdbi_ref[j])
        _s5_scan(xs_ref, ar_ref[...], ai_ref[...], asr_ref, asi_ref, carry_ref, S5_LG)
        ys = []
        for j in range(8):
            xr = xs_ref[:, j * 512:(j + 1) * 512].astype(BF16)
            xi = xs_ref[:, NS + j * 512:NS + (j + 1) * 512].astype(BF16)
            ys.append(_dot(xr, bdcr_ref[j]) + _dot(xi, bdci_ref[j]))
        ypre = jnp.concatenate(ys, axis=1) + d_ref[...] * up.astype(F32)
        ypre_ref[...] = ypre
        ya = _gelu(ypre)
        zl = _dot(ya.astype(BF16), wglu_ref[...]) + bglu_ref[...]
        outp = (ya * _sigmoid(zl)).astype(BF16)
        ya_ref[...] = _dot(pmt_ref[...], outp).astype(BF16)

    return pl.pallas_call(
        body, name="s5_fwd", grid=(nc,),
        in_specs=[Block((TC, W), lambda i: (i, 0)), _const((TC, TC)), _const((TC, TC)),
                  _const((8, 128, 512)), _const((8, 128, 512)), _const((8, 512, 128)), _const((8, 512, 128)),
                  _const((1, NS)), _const((1, NS)), _const((S5_SEG, NS)), _const((S5_SEG, NS)),
                  _const((1, W)), _const((W, W)), _const((1, W))],
        out_specs=[Block((TC, W), lambda i: (i, 0)), Block((TC, 2 * NS), lambda i: (i, 0)),
                   Block((TC, W), lambda i: (i, 0))],
        out_shape=[SDS((T, W), BF16), SDS((T, 2 * NS), F32), SDS((T, W), F32)],
        scratch_shapes=[pltpu.VMEM((1, 2 * NS), F32)],
        compiler_params=_cparams(("arbitrary",), VMEM_BIG),
    )(proj, perm, permt, bdbr, bdbi, bdcr, bdci, abr, abi, asr, asi, dvec, wglu, bglu)


def _s5_bwd(dya, proj, ypre, xs, perm, permt, bdbr, bdbi, bdcr, bdci, abr, abi, asr, asi, dvec, wglu, bglu):
    T = proj.shape[0]
    TC, NS, W = S5_TC, S5_NS, S5_WIDTH
    nc = T // TC

    def body(dya_ref, u_ref, ypre_ref, xs_ref, pm_ref, pmt_ref, bdbr_ref, bdbi_ref, bdcr_ref, bdci_ref,
             ar_ref, ai_ref, asr_ref, asi_ref, d_ref, wglu_ref, bglu_ref,
             du_ref, dar_ref, dai_ref, dd_ref, dbglu_ref, o_dbdbr, o_dbdbi, o_dbdcr, o_dbdci, o_dwglu,
             g_ref, carry_ref, dbdbr_ref, dbdbi_ref, dbdcr_ref, dbdci_ref, dwglu_ref):
        i = pl.program_id(0)

        @pl.when(i == 0)
        def _():
            carry_ref[...] = jnp.zeros_like(carry_ref)
            for r in (dbdbr_ref, dbdbi_ref, dbdcr_ref, dbdci_ref, dar_ref, dai_ref, dd_ref, dwglu_ref, dbglu_ref):
                r[...] = jnp.zeros_like(r)

        pm = pm_ref[...]
        dyo = _dot(pm, dya_ref[...])
        up = _dot(pm, u_ref[...]).astype(BF16)
        upf = up.astype(F32)
        ypre_v = ypre_ref[...]
        ya = _gelu(ypre_v)
        yab = ya.astype(BF16)
        sg = _sigmoid(_dot(yab, wglu_ref[...]) + bglu_ref[...])
        dz = dyo * ya * sg * (1.0 - sg)
        dzb = dz.astype(BF16)
        dya_t = dyo * sg + _dot(dzb, wglu_ref[...], NT)
        dwglu_ref[...] += _dot(yab, dzb, TN)
        dbglu_ref[...] += jnp.sum(dz, axis=0, keepdims=True)
        dy = dya_t * _gelu_grad(ypre_v)
        dd_ref[...] += jnp.sum(dy * upf, axis=0, keepdims=True)
        dyb = dy.astype(BF16)
        for j in range(8):
            dyj = dyb[:, j * 128:(j + 1) * 128]
            g_ref[:, j * 512:(j + 1) * 512] = _dot(dyj, bdcr_ref[j], NT)
            g_ref[:, NS + j * 512:NS + (j + 1) * 512] = _dot(dyj, bdci_ref[j], NT)
            dbdcr_ref[j] += _dot(xs_ref[:, j * 512:(j + 1) * 512].astype(BF16), dyj, TN)
            dbdci_ref[j] += _dot(xs_ref[:, NS + j * 512:NS + (j + 1) * 512].astype(BF16), dyj, TN)
        _s5_scan(g_ref, ar_ref[...], ai_ref[...], asr_ref, asi_ref, carry_ref, S5_LG,
                 xs_ref=xs_ref, dar_ref=dar_ref, dai_ref=dai_ref)
        dus = []
        for j in range(8):
            ut = up[:, j * 128:(j + 1) * 128]
            gr = g_ref[:, j * 512:(j + 1) * 512].astype(BF16)
            gi = g_ref[:, NS + j * 512:NS + (j + 1) * 512].astype(BF16)
            dbdbr_ref[j] += _dot(ut, gr, TN)
            dbdbi_ref[j] += _dot(ut, gi, TN)
            dus.append(_dot(gr, bdbr_ref[j], NT) + _dot(gi, bdbi_ref[j], NT))
        dup = jnp.concatenate(dus, axis=1) + d_ref[...] * dy
        du_ref[...] = _dot(pmt_ref[...], dup.astype(BF16)).astype(BF16)

        @pl.when(i == nc - 1)
        def _():
            for src, dst in ((dbdbr_ref, o_dbdbr), (dbdbi_ref, o_dbdbi), (dbdcr_ref, o_dbdcr),
                             (dbdci_ref, o_dbdci), (dwglu_ref, o_dwglu)):
                pltpu.sync_copy(src, dst)

    c2 = lambda i: (0, 0)
    rev = lambda i: (nc - 1 - i, 0)
    return pl.pallas_call(
        body, name="s5_bwd", grid=(nc,),
        in_specs=[Block((TC, W), rev), Block((TC, W), rev), Block((TC, W), rev), Block((TC, 2 * NS), rev),
                  _const((TC, TC)), _const((TC, TC)),
                  _const((8, 128, 512)), _const((8, 128, 512)), _const((8, 512, 128)), _const((8, 512, 128)),
                  _const((1, NS)), _const((1, NS)), _const((S5_SEG, NS)), _const((S5_SEG, NS)),
                  _const((1, W)), _const((W, W)), _const((1, W))],
        out_specs=[Block((TC, W), rev), Block((8, NS), c2), Block((8, NS), c2), Block((1, W), c2), Block((1, W), c2),
                   ANY, ANY, ANY, ANY, ANY],
        out_shape=[SDS((T, W), BF16), SDS((8, NS), F32), SDS((8, NS), F32), SDS((1, W), F32), SDS((1, W), F32),
                   SDS((8, 128, 512), F32), SDS((8, 128, 512), F32),
                   SDS((8, 512, 128), F32), SDS((8, 512, 128), F32), SDS((W, W), F32)],
        scratch_shapes=[pltpu.VMEM((TC, 2 * NS), F32), pltpu.VMEM((1, 2 * NS), F32),
                        pltpu.VMEM((8, 128, 512), F32), pltpu.VMEM((8, 128, 512), F32),
                        pltpu.VMEM((8, 512, 128), F32), pltpu.VMEM((8, 512, 128), F32), pltpu.VMEM((W, W), F32)],
        compiler_params=_cparams(("arbitrary",), VMEM_BIG),
    )(dya, proj, ypre, xs, perm, permt, bdbr, bdbi, bdcr, bdci, abr, abi, asr, asi, dvec, wglu, bglu)


def _bd_b(bk_t):
    C, P = S5_GROUP_WIDTH, S5_STATE
    t = jnp.transpose(bk_t, (1, 0, 2)).reshape(8, 8, C, P)
    eye = jnp.eye(8, dtype=t.dtype)
    return (t[:, :, :, None, :] * eye[None, :, None, :, None]).reshape(8, 8 * C, 8 * P)


def _bd_b_extract(m):
    C, P = S5_GROUP_WIDTH, S5_STATE
    t = m.reshape(8, 8, C, 8, P)
    d = jnp.stack([t[:, g, :, g, :] for g in range(8)], axis=1)
    return jnp.transpose(d.reshape(S5_GROUPS, C, P), (1, 0, 2))


def _bd_c(c):
    C, P = S5_GROUP_WIDTH, S5_STATE
    t = jnp.transpose(c, (0, 2, 1)).reshape(8, 8, P, C)
    eye = jnp.eye(8, dtype=t.dtype)
    return (t[:, :, :, None, :] * eye[None, :, None, :, None]).reshape(8, 8 * P, 8 * C)


def _bd_c_extract(m):
    C, P = S5_GROUP_WIDTH, S5_STATE
    t = m.reshape(8, 8, P, 8, C)
    d = jnp.stack([t[:, g, :, g, :] for g in range(8)], axis=1)
    return jnp.transpose(d.reshape(S5_GROUPS, P, C), (0, 2, 1))


def _perm_matrix():
    r = jnp.arange(S5_TC)
    src = (r % 8) * S5_SEG + r // 8
    return (src[:, None] == jnp.arange(S5_TC)[None, :]).astype(BF16)


def _coords():
    return lax.axis_index("x"), lax.axis_index("y"), lax.axis_index("c")


def _all_gather(name, arrs):
    n = len(arrs)

    def body(*refs):
        ins, outs = refs[:n], refs[n:2 * n]
        send_sems, recv_sems, local_sems = refs[2 * n:]
        x, y, c = _coords()
        me, sibling = (x, y, c), (x, y, 1 - c)
        chips = [(1 - x, y), (x, 1 - y), (1 - x, 1 - y)]

        def slot(p):
            return 4 * p[0] + 2 * p[1] + p[2]

        def copy(a, k, block, to, src=None):
            dst = outs[a].at[slot(block)]
            return pltpu.make_async_remote_copy(
                src_ref=dst if src is None else src, dst_ref=dst,
                send_sem=send_sems.at[a * 7 + k], recv_sem=recv_sems.at[a * 7 + k],
                device_id=to, device_id_type=MESH)

        mine = [pltpu.make_async_copy(ins[a], outs[a].at[slot(me)], local_sems.at[a]) for a in range(n)]
        for m in mine:
            m.start()
        first = []
        for a in range(n):
            first.append(copy(a, 0, me, sibling, src=ins[a]))
            first += [copy(a, 1 + j, me, (*chip, c), src=ins[a]) for j, chip in enumerate(chips)]
        for cp in first:
            cp.start()
        passed = []
        for j, chip in enumerate(chips):
            for a in range(n):
                copy(a, 1 + j, (*chip, c), me).wait_recv()
                fw = copy(a, 4 + j, (*chip, c), sibling)
                fw.start()
                passed.append(fw)
        for a in range(n):
            copy(a, 0, sibling, me).wait_recv()
            for j, chip in enumerate(chips):
                copy(a, 4 + j, (*chip, 1 - c), me).wait_recv()
        for cp in first + passed:
            cp.wait_send()
        for m in mine:
            m.wait()

    return pl.pallas_call(
        body, name=name,
        in_specs=[ANY] * n, out_specs=[ANY] * n,
        out_shape=[SDS((NDEV,) + a.shape, a.dtype) for a in arrs],
        scratch_shapes=[pltpu.SemaphoreType.DMA((7 * n,)), pltpu.SemaphoreType.DMA((7 * n,)),
                        pltpu.SemaphoreType.DMA((n,))],
    )(*arrs)


HBM = pl.BlockSpec(memory_space=pltpu.HBM)
SEM = pl.BlockSpec(memory_space=pltpu.SEMAPHORE)
EFFECT = pltpu.SideEffectType.DATAFLOW_SIDE_EFFECTING


def _peers7(x, y, c):
    return [(1 - x if fx else x, 1 - y if fy else y, 1 - c if fc else c)
            for fx in (0, 1) for fy in (0, 1) for fc in (0, 1) if fx or fy or fc]


def _slot(p):
    return 4 * p[0] + 2 * p[1] + p[2]


def _split_copies(src_refs, land_refs, send_sems, recv_sems, gather, mine):
    x, y, c = _coords()
    me = (x, y, c)
    out = []
    for a, (src, land) in enumerate(zip(src_refs, land_refs)):
        for k, p in enumerate(_peers7(x, y, c)):
            s = src if gather else src.at[_slot(p)]
            out.append(pltpu.make_async_remote_copy(
                src_ref=s, dst_ref=land.at[_slot(me) if mine else _slot(p)],
                send_sem=send_sems.at[a * 7 + k], recv_sem=recv_sems.at[a * 7 + k],
                device_id=p, device_id_type=MESH))
    return out


def _own_slab(shard):
    x, y, c = _coords()
    z = lax.empty((NDEV,) + shard.shape, shard.dtype)
    return lax.dynamic_update_slice(z, shard[None], (_slot((x, y, c)),) + (0,) * shard.ndim)


def _split_start(name, srcs, lands, gather):
    n = len(srcs)

    def body(*refs):
        src_refs, land_refs = refs[:n], refs[n:2 * n]
        send_sems, recv_sems = refs[2 * n], refs[2 * n + 1]
        token = refs[-1]
        for cp in _split_copies(src_refs, land_refs, send_sems, recv_sems, gather, True):
            cp.start()
        token[...] = jnp.zeros_like(token)

    thru = [pltpu.HBM(a.shape, a.dtype) for a in list(srcs) + list(lands)]
    res = pl.pallas_call(
        body, name=name,
        out_shape=(pltpu.SemaphoreType.DMA((7 * n,)), pltpu.SemaphoreType.DMA((7 * n,)), *thru, SDS((8, 128), F32)),
        in_specs=[HBM] * (2 * n),
        out_specs=(SEM, SEM, *([HBM] * (2 * n)), pl.BlockSpec(memory_space=pltpu.VMEM)),
        input_output_aliases={i: 2 + i for i in range(2 * n)},
        compiler_params=pltpu.CompilerParams(has_side_effects=EFFECT),
    )(*[pltpu.with_memory_space_constraint(a, pltpu.HBM) for a in list(srcs) + list(lands)])
    return res[0], res[1], list(res[2:2 + n]), list(res[2 + n:2 + 2 * n]), res[-1]


def _split_wait(name, started, after, gather):
    send_sems, recv_sems, srcs, lands, _ = started
    n = len(srcs)

    def body(*refs):
        src_refs, land_refs = refs[:n], refs[n:2 * n]
        s_sems, r_sems = refs[2 * n], refs[2 * n + 1]
        for cp in _split_copies(src_refs, land_refs, s_sems, r_sems, gather, False):
            cp.wait_send()
            cp.wait_recv()

    thru = [pltpu.HBM(a.shape, a.dtype) for a in list(srcs) + list(lands)]
    res = pl.pallas_call(
        body, name=name, out_shape=tuple(thru),
        in_specs=[HBM] * (2 * n) + [SEM, SEM, ANY], out_specs=tuple([HBM] * (2 * n)),
        input_output_aliases={i: i for i in range(2 * n)},
        compiler_params=pltpu.CompilerParams(has_side_effects=EFFECT),
    )(*srcs, *lands, send_sems, recv_sems, after)
    return list(res[n:])


def _adam_math(w, g, m, v):
    m = ADAM_B1 * m + (1.0 - ADAM_B1) * g
    v = ADAM_B2 * v + (1.0 - ADAM_B2) * (g * g)
    m_hat = m / (1.0 - ADAM_B1 ** ADAM_STEP)
    v_hat = v / (1.0 - ADAM_B2 ** ADAM_STEP)
    delta = -ADAM_LR * (m_hat / (jnp.sqrt(v_hat) + ADAM_EPS) + ADAM_WD * w)
    return delta, m, v


def _adam_sharded(name, recv, sub, w, m, v):
    R, Cc = w.shape
    tr = max(t for t in range(16, R + 1, 16) if R % t == 0 and t * Cc <= 256 * 1024)

    def body(*refs):
        parts = refs[:NDEV]
        w_ref, m_ref, v_ref, g_out, d_out, m_out, v_out = refs[NDEV:]
        g = parts[0][...].astype(F32)
        for p in parts[1:]:
            g = g + p[...].astype(F32)
        delta, mn, vn = _adam_math(w_ref[...], g, m_ref[...], v_ref[...])
        g_out[...] = g
        d_out[...] = delta
        m_out[...] = mn
        v_out[...] = vn

    if sub is None:
        pspecs = [Block((None, tr, Cc), functools.partial(lambda s, i: (s, i, 0), s)) for s in range(NDEV)]
    else:
        pspecs = [Block((None, None, tr, Cc), functools.partial(lambda s, i: (s, sub, i, 0), s)) for s in range(NDEV)]
    row = Block((tr, Cc), lambda i: (i, 0))
    o = SDS((R, Cc), F32)
    return pl.pallas_call(
        body, name=name, grid=(R // tr,),
        in_specs=pspecs + [row, row, row], out_specs=[row, row, row, row], out_shape=[o, o, o, o],
        compiler_params=_cparams(("arbitrary",), VMEM_BIG))(*([recv] * NDEV), w, m, v)


def _adam_small(groups):
    n = len(groups)

    def body(*refs):
        ins, outs = refs[:4 * n], refs[4 * n:]
        for a in range(n):
            p_ref, w_ref, m_ref, v_ref = ins[4 * a:4 * a + 4]
            g = p_ref[0]
            for s in range(1, NDEV):
                g = g + p_ref[s]
            delta, mn, vn = _adam_math(w_ref[...], g, m_ref[...], v_ref[...])
            for o, r in zip(outs[4 * a:4 * a + 4], (g, delta, mn, vn)):
                o[...] = r

    flat_in = [t for grp in groups for t in grp]
    out_shape = [SDS(grp[1].shape, F32) for grp in groups for _ in range(4)]
    res = pl.pallas_call(body, name="adam_small", out_shape=out_shape,
                         compiler_params=_cparams(None, VMEM_BIG))(*flat_in)
    return [tuple(res[4 * a:4 * a + 4]) for a in range(n)]


_TINY = ["mix_norm", "s5_a_re", "s5_a_im", "s5_log_dt", "s5_d", "s5_b_glu", "sgu_ln_g", "sgu_ln_b",
         "sgu_b_s", "b_gate", "ffn2_norm", "final_norm"]
_ORDER = ["ffn1_norm", "ffn1_w_gate", "ffn1_w_up", "ffn1_w_down", "mix_norm", "w_in", "s5_a_re", "s5_a_im",
          "s5_log_dt", "s5_b_re", "s5_b_im", "s5_c_re", "s5_c_im", "s5_d", "s5_w_glu", "s5_b_glu", "sgu_ln_g",
          "sgu_ln_b", "sgu_w_s", "sgu_b_s", "w_branch_a", "w_branch_b", "w_gate", "b_gate", "w_out", "ffn2_norm",
          "ffn2_w_gate", "ffn2_w_up", "ffn2_w_down", "final_norm"]


def _step(x, tgt, W, M, V):
    T = x.shape[1]
    x0 = x[0]
    tgt0 = tgt[0]
    bf = lambda a: a.astype(BF16)

    def gather_start(name, shards):
        return _split_start(name, shards, [_own_slab(s) for s in shards], True)

    (wgu1,) = _all_gather("gather1", [jnp.stack([bf(W["ffn1_w_gate"][0].T), bf(W["ffn1_w_up"][0].T)])])

    lr_, li_ = W["s5_a_re"][0], W["s5_a_im"][0]
    ldt_ = W["s5_log_dt"][0][:, None]
    brt = jnp.transpose(W["s5_b_re"][0], (2, 0, 1))
    bit = jnp.transpose(W["s5_b_im"][0], (2, 0, 1))
    abr, abi, pwr, pwi, bkr_t, bki_t = _s5_prep(lr_, li_, ldt_, brt, bit)
    bdbr, bdbi = bf(_bd_b(bkr_t)), bf(_bd_b(bki_t))
    bdcr, bdci = bf(_bd_c(W["s5_c_re"][0])), bf(_bd_c(-W["s5_c_im"][0]))
    flat = lambda a: a.reshape(1, S5_NS)
    s5a = (_perm_matrix(), _perm_matrix().T, bdbr, bdbi, bdcr, bdci, flat(abr), flat(abi),
           pwr.reshape(S5_SEG, S5_NS), pwi.reshape(S5_SEG, S5_NS),
           W["s5_d"][0].reshape(1, S5_WIDTH))
    blk = jnp.arange(MLP_CHUNK) // CHUNK
    mask = blk[:, None] >= blk[None, :]
    wsm = jnp.where(mask[None], W["sgu_w_s"][0], 0.0)
    wsm_b, wsmt_b = bf(wsm), bf(jnp.transpose(wsm, (0, 2, 1)))
    bst = jnp.broadcast_to(W["sgu_b_s"][0][:, :, None], (SGU_HEADS, MLP_CHUNK, 128))
    bgate2 = W["b_gate"].reshape(2, 1, D_MODEL)

    h1 = _rms_fwd("rms1", x0, W["ffn1_norm"])
    dep = (wgu1[0, 0, :1, :1] * 0).astype(BF16)

    def later(a):
        return bf(a) + dep[0]

    gs2 = gather_start("gather2_start", [later(W["ffn1_w_down"][0])])
    ab1, f1 = _ffn_up("ffn1_up", h1, wgu1, gs2[4])
    (wd1,) = _split_wait("gather2_wait", gs2, f1, True)
    dep = (wd1[0, :1, :1] * 0).astype(BF16)
    gs3 = gather_start("gather3_start", [later(W["w_in"][0].T), later(W["s5_w_glu"][0])])
    x1 = _ffn_down("ffn1_down", f1, wd1, x0, after=gs3[4])
    h2 = _rms_fwd("rms2", x1, W["mix_norm"])
    win, wglu = _split_wait("gather3_wait", gs3, h2, True)
    wglu = wglu.reshape(S5_WIDTH, S5_WIDTH)
    s5c = s5a + (wglu, W["s5_b_glu"])
    dep = (win[0, :1, :1] * 0).astype(BF16)
    win = win.reshape(S5_WIDTH + 2 * SGU_WIDTH, D_MODEL)
    gs4 = gather_start("gather4_start", [later(W["w_gate"][0]), later(W["w_branch_a"][0]),
                                         later(W["w_branch_b"][0]), later(W["w_out"][0])])
    proj = _rows_fwd("w_in", h2, win, after=gs4[4])
    ya, xs, ypre = _s5_fwd(proj, *s5c)
    dep = (ya[:1, :1] * 0).astype(BF16)
    gs5 = gather_start("gather5_start", [jnp.stack([later(W["ffn2_w_gate"][0].T), later(W["ffn2_w_up"][0].T)])])
    yb = _sgu_fwd("sgu_fwd", proj, W["sgu_ln_g"] + gs5[4][:1, :1], W["sgu_ln_b"], wsm_b, bst)
    wgate, wba, wbb, wout = _split_wait("gather4_wait", gs4, yb, True)
    wout = wout.reshape(D_MODEL, D_MODEL)
    pa = _col_fwd("branch_a", ya, wba)
    pb = _col_fwd("branch_b", yb, wbb)
    gl = _gate_fwd("gate", h2, wgate, bgate2)
    merged = _merge_fwd("merge", pa, pb, gl)
    x2 = _plain_fwd_res("w_out", merged, wout, x1)
    h3 = _rms_fwd("rms3", x2, W["ffn2_norm"])
    (wgu2,) = _split_wait("gather5_wait", gs5, h3, True)
    dep = (wgu2[0, 0, :1, :1] * 0).astype(BF16)
    gs6 = gather_start("gather6_start", [later(W["ffn2_w_down"][0])])
    ab2, f2 = _ffn_up("ffn2_up", h3, wgu2, gs6[4])
    (wd2,) = _split_wait("gather6_wait", gs6, f2, True)
    x3 = _ffn_down("ffn2_down", f2, wd2, x2)
    loss_p, dx3b, dgf = _loss_head("loss_head", x3, W["final_norm"].reshape(1, D_MODEL), tgt0)

    def exchange_start(name, grads):
        x_, y_, c_ = _coords()
        me = _slot((x_, y_, c_))
        return _split_start(name, grads, [_own_slab(lax.dynamic_index_in_dim(g, me, 0, keepdims=False))
                                          for g in grads], False)

    dab2 = _ffn_down_bwd_act("ffn2_down_bwd_a", dx3b, wd2, ab2)
    g_wd2 = _ffn_down_bwd_w("ffn2_down_bwd_w", f2, dx3b)
    g_gu2 = _ffn_up_bwd_w("ffn2_up_bwd_w", h3, dab2)
    es1 = exchange_start("exchange1_start", [g_wd2, g_gu2])
    dh3 = _ffn_up_bwd_h("ffn2_up_bwd_h", dab2, wgu2, es1[4])
    dx2b, dg3 = _rms_bwd("rms3_bwd", dh3, x2, W["ffn2_norm"], dx3b, BF16)

    dmerged = _plain_bwd_a("w_out_bwd_a", dx2b, wout)
    g_wout = _plain_bwd_w("w_out_bwd_w", merged, dx2b)
    dpa, dpb, dgl, dbgate = _merge_bwd("merge_bwd", dmerged, pa, pb, gl)
    dya = _col_bwd_a("branch_a_bwd_a", dpa, wba)
    g_wba = _col_bwd_w("branch_a_bwd_w", ya, dpa, 256)
    dyb = _col_bwd_a("branch_b_bwd_a", dpb, wbb)
    g_wbb = _col_bwd_w("branch_b_bwd_w", yb, dpb, 256)
    dh2g = _gate_bwd_a("gate_bwd_a", dgl, wgate)
    g_wgate = _gate_bwd_w("gate_bwd_w", h2, dgl, 512)
    (dua, dar8, dai8, ddv, dbglu, dbdbr, dbdbi, dbdcr, dbdci, g_wglu) = _s5_bwd(dya, proj, ypre, xs, *s5c)
    dproj, dws, dbst, dlng, dlnb = _sgu_bwd("sgu_bwd", dyb, dua, proj, W["sgu_ln_g"], W["sgu_ln_b"],
                                            wsm_b, wsmt_b, bst)
    g_win = _col_bwd_w("w_in_bwd_w", h2, dproj, 384)
    g_wout3 = g_wout.reshape(NDEV, D_MODEL // NDEV, D_MODEL)
    g_wglu3 = g_wglu.astype(BF16).reshape(NDEV, S5_WIDTH // NDEV, S5_WIDTH)
    es2 = exchange_start("exchange2_start", [g_wout3, g_wba, g_wbb, g_wgate, g_wglu3, g_win])
    dh2 = _rows_bwd_a("w_in_bwd_a", dproj, win, dh2g)
    dx1b, dgm = _rms_bwd("rms2_bwd", dh2, x1, W["mix_norm"] + es2[4][:1, :1], dx2b, BF16)

    dabr = jnp.sum(dar8, axis=0).reshape(S5_GROUPS, S5_STATE)
    dabi = jnp.sum(dai8, axis=0).reshape(S5_GROUPS, S5_STATE)
    d_lr, d_li, d_ldt, d_brt, d_bit = _s5_prep_bwd(lr_, li_, ldt_, brt, bit, dabr, dabi,
                                                   _bd_b_extract(dbdbr), _bd_b_extract(dbdbi))
    small_g = {
        "mix_norm": dgm, "ffn2_norm": dg3, "final_norm": dgf,
        "s5_a_re": d_lr, "s5_a_im": d_li, "s5_log_dt": d_ldt,
        "s5_d": ddv, "s5_b_glu": dbglu, "sgu_ln_g": dlng, "sgu_ln_b": dlnb,
        "sgu_b_s": dbst[:, :, 0], "b_gate": dbgate,
    }
    to_cgp = lambda a: jnp.transpose(a[0], (2, 0, 1))
    from_cgp = lambda a: jnp.transpose(a, (1, 2, 0))[None]
    natural = [
        ("s5_b_re", d_brt, to_cgp, from_cgp), ("s5_b_im", d_bit, to_cgp, from_cgp),
        ("s5_c_re", _bd_c_extract(dbdcr), lambda a: a[0], lambda a: a[None]),
        ("s5_c_im", -_bd_c_extract(dbdci), lambda a: a[0], lambda a: a[None]),
        ("sgu_w_s", jnp.where(mask[None], dws, 0.0), lambda a: a[0], lambda a: a[None]),
    ]
    sizes = [W[n].size for n in _TINY]
    total = sum(sizes) + 1
    rows = -(-total // 128)
    rows = -(-rows // 8) * 8
    pad = rows * 128 - total

    def pack(d, extra):
        return jnp.concatenate([d[n].reshape(-1).astype(F32) for n in _TINY] + [extra, jnp.zeros((pad,), F32)]
                               ).reshape(rows, 128)

    gsm = gather_start("gather_small_start", [pack(small_g, loss_p[0, :1])] + [g for _, g, _, _ in natural])

    dab1 = _ffn_down_bwd_act("ffn1_down_bwd_a", dx1b, wd1, ab1, after=gsm[4])
    g_gu1 = _ffn_up_bwd_w("ffn1_up_bwd_w", h1, dab1)
    es3 = exchange_start("exchange3_start", [g_gu1])
    g_wd1 = _ffn_down_bwd_w("ffn1_down_bwd_w", f1, dx1b, after=es3[4])
    es4 = exchange_start("exchange4_start", [g_wd1])
    dh1 = _ffn_up_bwd_h("ffn1_up_bwd_h", dab1, wgu1, es4[4])
    dx0, dg1 = _rms_bwd("rms1_bwd", dh1, x0, W["ffn1_norm"], dx1b, F32)

    G, Dl, Mn, Vn = {}, {}, {}, {}

    def adam(plan):
        last = None
        for n, recv, sub in plan:
            if sub is None:
                g, d, mn, vn = _adam_sharded("adam_" + n, recv, sub, W[n][0], M[n][0], V[n][0])
                G[n], Dl[n], Mn[n], Vn[n] = g[None], d[None], mn[None], vn[None]
            else:
                tr = jnp.transpose
                g, d, mn, vn = _adam_sharded("adam_" + n, recv, sub, tr(W[n][0]), tr(M[n][0]), tr(V[n][0]))
                G[n], Dl[n], Mn[n], Vn[n] = tr(g)[None], tr(d)[None], tr(mn)[None], tr(vn)[None]
            last = g
        return last

    r_wd2, r_gu2 = _split_wait("exchange1_wait", es1, dx0, False)
    done = adam([("ffn2_w_down", r_wd2, None), ("ffn2_w_gate", r_gu2, 0), ("ffn2_w_up", r_gu2, 1)])
    r_wout, r_wba, r_wbb, r_wgate, r_wglu, r_win = _split_wait("exchange2_wait", es2, done, False)
    done = adam([("w_out", r_wout, None), ("w_branch_a", r_wba, None), ("w_branch_b", r_wbb, None),
                 ("w_gate", r_wgate, None), ("s5_w_glu", r_wglu, None), ("w_in", r_win, None)])

    late = dg1 + 0.0 * done.reshape(-1)[:1]
    zero1 = jnp.zeros((1,), F32)
    parts = _split_wait("gather_small_wait", gsm, late, True)
    (parts_g1,) = _all_gather("gather_ffn1_norm_grad", [late])
    groups = [(parts[0], pack(W, zero1), pack(M, zero1), pack(V, zero1))]
    groups += [(parts[1 + a], view(W[n]), view(M[n]), view(V[n])) for a, (n, _, view, _) in enumerate(natural)]
    groups += [(parts_g1, W["ffn1_norm"], M["ffn1_norm"], V["ffn1_norm"])]
    res = _adam_small(groups)
    sg, sd, sm, sv = res[0]
    for (n, _, _, back), (g, d, mn, vn) in zip(natural, res[1:-1]):
        G[n], Dl[n], Mn[n], Vn[n] = back(g), back(d), back(mn), back(vn)
    G["ffn1_norm"], Dl["ffn1_norm"], Mn["ffn1_norm"], Vn["ffn1_norm"] = res[-1]

    def unpack(flat2d, into):
        flat = flat2d.reshape(-1)
        off = 0
        for n, s in zip(_TINY, sizes):
            into[n] = flat[off:off + s].reshape(W[n].shape)
            off += s
        return flat[off]

    loss = unpack(sg, G)
    unpack(sd, Dl)
    unpack(sm, Mn)
    unpack(sv, Vn)

    (r_gu1,) = _split_wait("exchange3_wait", es3, sg, False)
    done = adam([("ffn1_w_gate", r_gu1, 0), ("ffn1_w_up", r_gu1, 1)])
    (r_wd1,) = _split_wait("exchange4_wait", es4, done, False)
    adam([("ffn1_w_down", r_wd1, None)])

    return loss, dx0[None], G, Dl, Mn, Vn


def kernel(x, ffn1_norm, ffn1_w_gate, ffn1_w_up, ffn1_w_down, mix_norm, w_in, s5_a_re, s5_a_im, s5_log_dt, s5_b_re, s5_b_im, s5_c_re, s5_c_im, s5_d, s5_w_glu, s5_b_glu, sgu_ln_g, sgu_ln_b, sgu_w_s, sgu_b_s, w_branch_a, w_branch_b, w_gate, b_gate, w_out, ffn2_norm, ffn2_w_gate, ffn2_w_up, ffn2_w_down, final_norm, loss_target, m_ffn1_norm, m_ffn1_w_gate, m_ffn1_w_up, m_ffn1_w_down, m_mix_norm, m_w_in, m_s5_a_re, m_s5_a_im, m_s5_log_dt, m_s5_b_re, m_s5_b_im, m_s5_c_re, m_s5_c_im, m_s5_d, m_s5_w_glu, m_s5_b_glu, m_sgu_ln_g, m_sgu_ln_b, m_sgu_w_s, m_sgu_b_s, m_w_branch_a, m_w_branch_b, m_w_gate, m_b_gate, m_w_out, m_ffn2_norm, m_ffn2_w_gate, m_ffn2_w_up, m_ffn2_w_down, m_final_norm, v_ffn1_norm, v_ffn1_w_gate, v_ffn1_w_up, v_ffn1_w_down, v_mix_norm, v_w_in, v_s5_a_re, v_s5_a_im, v_s5_log_dt, v_s5_b_re, v_s5_b_im, v_s5_c_re, v_s5_c_im, v_s5_d, v_s5_w_glu, v_s5_b_glu, v_sgu_ln_g, v_sgu_ln_b, v_sgu_w_s, v_sgu_b_s, v_w_branch_a, v_w_branch_b, v_w_gate, v_b_gate, v_w_out, v_ffn2_norm, v_ffn2_w_gate, v_ffn2_w_up, v_ffn2_w_down, v_final_norm):
    a = locals()
    W = {n: a[n] for n in _ORDER}
    M = {n: a["m_" + n] for n in _ORDER}
    V = {n: a["v_" + n] for n in _ORDER}
    loss, gx, G, Dl, Mn, Vn = _step(x, loss_target, W, M, V)
    return (loss, gx, *[G[n] for n in _ORDER], *[Dl[n] for n in _ORDER], *[Mn[n] for n in _ORDER],
            *[Vn[n] for n in _ORDER])
```

```python
import functools
import math

import jax
import jax.numpy as jnp
from jax import lax
from jax.experimental import pallas as pl
from jax.experimental.pallas import tpu as pltpu

F32 = jnp.float32
BF16 = jnp.bfloat16
NDEV = 8
NORM_EPS = 1e-6
D_MODEL = 2048
D_FF = 5632
FF_SHARD = D_FF // NDEV
S5_WIDTH = 1024
S5_GROUPS = 64
S5_GROUP_WIDTH = 16
S5_STATE = 64
S5_NS = S5_GROUPS * S5_STATE
SGU_WIDTH = 1024
SGU_HEADS = 8
MLP_CHUNK = 128
CHUNK = 64
ADAM_LR, ADAM_B1, ADAM_B2, ADAM_EPS, ADAM_WD, ADAM_STEP = 0.001, 0.9, 0.999, 1e-08, 0.01, 10
S5_TC = 256
S5_SEG = S5_TC // 8
S5_LG = 512
S5_UNROLL = True
VMEM_BIG = 56 * 1024 * 1024

MESH = pl.DeviceIdType.MESH
SDS = jax.ShapeDtypeStruct
Block = pl.BlockSpec
ANY = pl.BlockSpec(memory_space=pl.ANY)


def _cparams(sem=None, vmem=None):
    return pltpu.CompilerParams(dimension_semantics=sem, vmem_limit_bytes=vmem)


def _const(shape):
    nd = len(shape)
    return pl.BlockSpec(shape, lambda i: (0,) * nd, pipeline_mode=pl.Buffered(1))


def _sigmoid(x):
    return 0.5 * jnp.tanh(0.5 * x) + 0.5


_GELU_C = math.sqrt(2.0 / math.pi)


def _gelu(x):
    return 0.5 * x * (1.0 + jnp.tanh(_GELU_C * (x + 0.044715 * x * x * x)))


def _gelu_grad(x):
    t = jnp.tanh(_GELU_C * (x + 0.044715 * x * x * x))
    return 0.5 * (1.0 + t) + 0.5 * x * (1.0 - t * t) * _GELU_C * (1.0 + 3.0 * 0.044715 * x * x)


NN = (((1,), (0,)), ((), ()))
NT = (((1,), (1,)), ((), ()))
TN = (((0,), (0,)), ((), ()))


def _dot(a, b, dims=NN):
    return lax.dot_general(a, b, dims, preferred_element_type=F32)


def _matmul(name, a, b, extras, *, grid, a_spec, b_spec, extra_specs, out_shapes, out_specs, acc_shape,
            epilogue, dims=NN, nb=None, compute=None, after=None, vmem=VMEM_BIG):
    nk = grid[2]
    if after is not None:
        extras = tuple(extras) + (after,)
        extra_specs = list(extra_specs) + [Block((8, 128), lambda i, j, k: (0, 0))]
    ne, no = len(extras), len(out_shapes)
    nacc = nb or 1
    if compute is None:
        def compute(a_ref, b_ref, q):
            return _dot(a_ref[...], b_ref[q] if nb else b_ref[...], dims)

    def body(*refs):
        a_ref, b_ref = refs[0], refs[1]
        ex = refs[2:2 + ne]
        outs = refs[2 + ne:2 + ne + no]
        if nk == 1:
            epilogue([compute(a_ref, b_ref, q) for q in range(nacc)], ex, outs)
            return
        acc_ref = refs[2 + ne + no]
        k = pl.program_id(2)

        @pl.when(k == 0)
        def _():
            acc_ref[...] = jnp.zeros_like(acc_ref)

        for q in range(nacc):
            acc_ref[q] += compute(a_ref, b_ref, q)

        @pl.when(k == nk - 1)
        def _():
            epilogue([acc_ref[q] for q in range(nacc)], ex, outs)

    scratch = [] if nk == 1 else [pltpu.VMEM((nacc,) + tuple(acc_shape), F32)]
    res = pl.pallas_call(
        body, name=name, grid=grid,
        in_specs=[a_spec, b_spec] + list(extra_specs),
        out_specs=list(out_specs), out_shape=list(out_shapes), scratch_shapes=scratch,
        compiler_params=_cparams(("parallel", "parallel", "arbitrary"), vmem),
    )(a, b, *extras)
    return res


def _store(dtype_outs=None):
    def ep(accs, ex, outs):
        outs[0][...] = accs[0].astype(outs[0].dtype)
    return ep


def _tile(n, t):
    t = min(n, t)
    assert n % t == 0, (n, t)
    return t


def _ksum(kq, dims):
    def compute(a_ref, b_ref, _):
        part = _dot(a_ref[0], b_ref[0], dims)
        for q in range(1, kq):
            part = part + _dot(a_ref[q], b_ref[q], dims)
        return part
    return compute


def _ksum_lanes(kq, ns, dims):
    def compute(a_ref, b_ref, _):
        part = _dot(a_ref[:, 0:ns], b_ref[0], dims)
        for q in range(1, kq):
            part = part + _dot(a_ref[:, q * ns:(q + 1) * ns], b_ref[q], dims)
        return part
    return compute


def _wide_b(g):
    def compute(a_ref, b_ref, _):
        bw = b_ref[0] if g == 1 else jnp.concatenate([b_ref[q] for q in range(g)], axis=1)
        return _dot(a_ref[...], bw, NN)
    return compute


TT_DEEP = 2048
TT_FFN = 4096


HIDDEN = NDEV * FF_SHARD


def _ffn_up(name, h, wgu, after=None):
    T, D = h.shape
    tm = _tile(T, 1024)

    def ep(accs, ex, outs):
        a, b = accs
        outs[0][0] = a.astype(BF16)
        outs[0][1] = b.astype(BF16)
        outs[1][...] = (a * _sigmoid(a) * b).astype(BF16)

    return _matmul(
        name, wgu, h, (), after=after, grid=(NDEV, T // tm, 1),
        a_spec=Block((None, 2, FF_SHARD, D), lambda j, i, k: (j, 0, 0, 0)),
        b_spec=Block((tm, D), lambda j, i, k: (i, 0)),
        extra_specs=(),
        out_shapes=[SDS((NDEV, 2, FF_SHARD, T), BF16), SDS((NDEV, FF_SHARD, T), BF16)],
        out_specs=[Block((None, 2, FF_SHARD, tm), lambda j, i, k: (j, 0, 0, i)),
                   Block((None, FF_SHARD, tm), lambda j, i, k: (j, 0, i))],
        acc_shape=(FF_SHARD, tm), nb=2,
        compute=lambda a_ref, b_ref, q: _dot(a_ref[q], b_ref[...], NT), epilogue=ep)


def _ffn_down(name, f, wd, xres, after=None):
    T = f.shape[2]
    tm, tn, tk = _tile(T, 1024), 1024, HIDDEN // 2

    def ep(accs, ex, outs):
        outs[0][...] = ex[0][...] + 0.5 * accs[0]

    return _matmul(
        name, f.reshape(HIDDEN, T), wd.reshape(HIDDEN, D_MODEL), (xres,), after=after,
        grid=(T // tm, D_MODEL // tn, HIDDEN // tk),
        a_spec=Block((tk, tm), lambda i, j, k: (k, i)),
        b_spec=Block((tk, tn), lambda i, j, k: (k, j)),
        extra_specs=[Block((tm, tn), lambda i, j, k: (i, j))],
        out_shapes=[SDS((T, D_MODEL), F32)],
        out_specs=[Block((tm, tn), lambda i, j, k: (i, j))],
        acc_shape=(tm, tn), dims=TN, epilogue=ep)[0]


def _ffn_down_bwd_act(name, dyb, wd, ab, after=None):
    T, D = dyb.shape
    tm, g = _tile(T, 2048), 1

    def ep(accs, ex, outs):
        for s in range(g):
            df = accs[0][s * FF_SHARD:(s + 1) * FF_SHARD, :].astype(BF16)
            a = ex[0][s, 0]
            b = ex[0][s, 1]
            hs = 0.5 * _sigmoid(a)
            outs[0][s, 0] = df * b * hs * (1.0 + a * (1.0 - 2.0 * hs))
            outs[0][s, 1] = df * a * hs

    blk = Block((g, 2, FF_SHARD, tm), lambda i, j, k: (j, 0, 0, i))
    return _matmul(
        name, wd.reshape(HIDDEN, D), dyb, (ab,), after=after, grid=(T // tm, NDEV // g, 1),
        a_spec=Block((g * FF_SHARD, D), lambda i, j, k: (j, 0)),
        b_spec=Block((tm, D), lambda i, j, k: (i, 0)),
        extra_specs=[blk],
        out_shapes=[SDS((NDEV, 2, FF_SHARD, T), BF16)],
        out_specs=[blk],
        acc_shape=(g * FF_SHARD, tm), dims=NT, epilogue=ep)[0]


def _ffn_down_bwd_w(name, f, dyb, after=None):
    T = f.shape[2]
    tt, tn, tr = _tile(T, TT_DEEP), 1024, 2 * FF_SHARD

    def ep(accs, ex, outs):
        outs[0][...] = (0.5 * accs[0]).astype(BF16)

    return _matmul(
        name, f.reshape(HIDDEN, T), dyb, (), after=after, grid=(HIDDEN // tr, D_MODEL // tn, T // tt),
        a_spec=Block((tr, tt), lambda j, n, k: (j, k)),
        b_spec=Block((tt, tn), lambda j, n, k: (k, n)),
        extra_specs=(),
        out_shapes=[SDS((HIDDEN, D_MODEL), BF16)],
        out_specs=[Block((tr, tn), lambda j, n, k: (j, n))],
        acc_shape=(tr, tn), dims=NN, epilogue=ep)[0].reshape(NDEV, FF_SHARD, D_MODEL)


def _ffn_up_bwd_h(name, dab, wgu, after):
    T = dab.shape[3]
    tm, tn, tk = _tile(T, 1024), 2048, HIDDEN // 2
    return _matmul(
        name, dab.reshape(2 * HIDDEN, T), wgu.reshape(2 * HIDDEN, D_MODEL), (), after=after,
        grid=(T // tm, D_MODEL // tn, 2 * HIDDEN // tk),
        a_spec=Block((tk, tm), lambda i, j, k: (k, i)),
        b_spec=Block((tk, tn), lambda i, j, k: (k, j)),
        extra_specs=(),
        out_shapes=[SDS((T, D_MODEL), BF16)],
        out_specs=[Block((tm, tn), lambda i, j, k: (i, j))],
        acc_shape=(tm, tn), dims=TN, epilogue=_store())[0]


def _ffn_up_bwd_w(name, h, dab):
    T, D = h.shape
    tt, tn = _tile(T, TT_FFN), 1024

    def ep(accs, ex, outs):
        outs[0][0] = accs[0].astype(BF16)
        outs[0][1] = accs[1].astype(BF16)

    return _matmul(
        name, dab, h, (), grid=(NDEV, D // tn, T // tt),
        a_spec=Block((None, 2, FF_SHARD, tt), lambda j, n, k: (j, 0, 0, k)),
        b_spec=Block((tt, tn), lambda j, n, k: (k, n)),
        extra_specs=(),
        out_shapes=[SDS((NDEV, 2, FF_SHARD, D), BF16)],
        out_specs=[Block((None, 2, FF_SHARD, tn), lambda j, n, k: (j, 0, 0, n))],
        acc_shape=(FF_SHARD, tn), nb=2,
        compute=lambda a_ref, b_ref, q: _dot(a_ref[q], b_ref[...], NN), epilogue=ep)[0]


def _shards_per_step(ns):
    return max(g for g in (1, 2, 4, 8) if g * ns <= 2048)


def _split_lanes(g, ns):
    def ep(accs, ex, outs):
        for q in range(g):
            outs[0][q] = accs[0][:, q * ns:(q + 1) * ns].astype(outs[0].dtype)
    return ep


def _rows_fwd(name, a, wt, after=None):
    T, K = a.shape
    N = wt.shape[0]
    tm, tn = _tile(T, 1024), _tile(N, 1536)
    return _matmul(
        name, a, wt, (), after=after, grid=(N // tn, T // tm, 1),
        a_spec=Block((tm, K), lambda j, i, k: (i, 0)),
        b_spec=Block((tn, K), lambda j, i, k: (j, 0)),
        extra_specs=(),
        out_shapes=[SDS((T, N), BF16)],
        out_specs=[Block((tm, tn), lambda j, i, k: (i, j))],
        acc_shape=(tm, tn), dims=NT, epilogue=_store())[0]


def _rows_bwd_a(name, dy, wt, add):
    T, N = dy.shape
    K = wt.shape[1]
    tm, tn = _tile(T, 1024), _tile(K, 1024)

    def ep(accs, ex, outs):
        outs[0][...] = (accs[0] + ex[0][...].astype(F32)).astype(BF16)

    return _matmul(
        name, dy, wt, (add,), grid=(T // tm, K // tn, 1),
        a_spec=Block((tm, N), lambda i, j, k: (i, 0)),
        b_spec=Block((N, tn), lambda i, j, k: (0, j)),
        extra_specs=[Block((tm, tn), lambda i, j, k: (i, j))],
        out_shapes=[SDS((T, K), BF16)],
        out_specs=[Block((tm, tn), lambda i, j, k: (i, j))],
        acc_shape=(tm, tn), dims=NN, epilogue=ep)[0]


def _col_fwd(name, a, w, out_dtype=BF16, after=None):
    T, K = a.shape
    ns = w.shape[2]
    g = _shards_per_step(ns)
    tm = _tile(T, 1024)
    return _matmul(
        name, a, w, (), after=after, grid=(NDEV // g, T // tm, 1),
        a_spec=Block((tm, K), lambda j, i, k: (i, 0)),
        b_spec=Block((g, K, ns), lambda j, i, k: (j, 0, 0)),
        extra_specs=(),
        out_shapes=[SDS((T, NDEV * ns), out_dtype)],
        out_specs=[Block((tm, g * ns), lambda j, i, k: (i, j))],
        acc_shape=(tm, g * ns), compute=_wide_b(g), epilogue=_store())[0]


def _col_bwd_a(name, dy, w, add=None):
    T = dy.shape[0]
    _, K, ns = w.shape
    tm, tn = _tile(T, 1024), _tile(K, 1024)

    def ep(accs, ex, outs):
        r = accs[0]
        if add is not None:
            r = r + ex[0][...].astype(F32)
        outs[0][...] = r.astype(BF16)

    extras = () if add is None else (add,)
    return _matmul(
        name, dy, w, extras, grid=(T // tm, K // tn, 1),
        a_spec=Block((tm, NDEV * ns), lambda i, j, k: (i, 0)),
        b_spec=Block((NDEV, tn, ns), lambda i, j, k: (0, j, 0)),
        extra_specs=[Block((tm, tn), lambda i, j, k: (i, j))] * len(extras),
        out_shapes=[SDS((T, K), BF16)],
        out_specs=[Block((tm, tn), lambda i, j, k: (i, j))],
        acc_shape=(tm, tn), compute=_ksum_lanes(NDEV, ns, NT), epilogue=ep)[0]


def _col_bwd_w(name, a, dy, ns):
    T, K = a.shape
    g = _shards_per_step(ns)
    tt, tr = _tile(T, TT_DEEP), _tile(K, 1024)
    return _matmul(
        name, a, dy, (), grid=(NDEV // g, K // tr, T // tt),
        a_spec=Block((tt, tr), lambda j, n, k: (k, n)),
        b_spec=Block((tt, g * ns), lambda j, n, k: (k, j)),
        extra_specs=(),
        out_shapes=[SDS((NDEV, K, ns), BF16)],
        out_specs=[Block((g, tr, ns), lambda j, n, k: (j, n, 0))],
        acc_shape=(tr, g * ns), dims=TN, epilogue=_split_lanes(g, ns))[0]


def _gate_fwd(name, h, w, bias):
    T, K = h.shape
    ns = w.shape[2]
    g = 2
    per = D_MODEL // (g * ns)
    tm = _tile(T, 1024)

    def ep(accs, ex, outs):
        outs[0][...] = (accs[0] + ex[0][...]).astype(BF16)

    return _matmul(
        name, h, w, (bias,), grid=(NDEV // g, T // tm, 1),
        a_spec=Block((tm, K), lambda j, i, k: (i, 0)),
        b_spec=Block((g, K, ns), lambda j, i, k: (j, 0, 0)),
        extra_specs=[Block((None, 1, g * ns), lambda j, i, k: (j // per, 0, j % per))],
        out_shapes=[SDS((2, T, D_MODEL), BF16)],
        out_specs=[Block((None, tm, g * ns), lambda j, i, k: (j // per, i, j % per))],
        acc_shape=(tm, g * ns), compute=_wide_b(g), epilogue=ep)[0]


def _gate_bwd_a(name, dgl, w):
    _, T, _ = dgl.shape
    _, K, ns = w.shape
    per = D_MODEL // ns
    tm, tn = _tile(T, 1024), 1024

    def compute(a_ref, b_ref, _):
        part = None
        for q in range(NDEV):
            d = _dot(a_ref[q // per, :, (q % per) * ns:(q % per + 1) * ns], b_ref[q], NT)
            part = d if part is None else part + d
        return part

    return _matmul(
        name, dgl, w, (), grid=(T // tm, K // tn, 1),
        a_spec=Block((2, tm, D_MODEL), lambda i, j, k: (0, i, 0)),
        b_spec=Block((NDEV, tn, ns), lambda i, j, k: (0, j, 0)),
        extra_specs=(),
        out_shapes=[SDS((T, K), BF16)],
        out_specs=[Block((tm, tn), lambda i, j, k: (i, j))],
        acc_shape=(tm, tn), compute=compute, epilogue=_store())[0]


def _gate_bwd_w(name, h, dgl, ns):
    T, K = h.shape
    g = 2
    per = D_MODEL // (g * ns)
    tt, tr = _tile(T, TT_DEEP), 1024
    return _matmul(
        name, h, dgl, (), grid=(NDEV // g, K // tr, T // tt),
        a_spec=Block((tt, tr), lambda j, n, k: (k, n)),
        b_spec=Block((None, tt, g * ns), lambda j, n, k: (j // per, k, j % per)),
        extra_specs=(),
        out_shapes=[SDS((NDEV, K, ns), BF16)],
        out_specs=[Block((g, tr, ns), lambda j, n, k: (j, n, 0))],
        acc_shape=(tr, g * ns), dims=TN, epilogue=_split_lanes(g, ns))[0]


def _plain_fwd_res(name, a, w, xres):
    T, K = a.shape
    N = w.shape[1]
    tm, tn = _tile(T, 1024), _tile(N, 1024)

    def ep(accs, ex, outs):
        outs[0][...] = ex[0][...] + accs[0]

    return _matmul(
        name, a, w, (xres,), grid=(T // tm, N // tn, 1),
        a_spec=Block((tm, K), lambda i, j, k: (i, 0)),
        b_spec=Block((K, tn), lambda i, j, k: (0, j)),
        extra_specs=[Block((tm, tn), lambda i, j, k: (i, j))],
        out_shapes=[SDS((T, N), F32)],
        out_specs=[Block((tm, tn), lambda i, j, k: (i, j))],
        acc_shape=(tm, tn), dims=NN, nb=None, epilogue=ep)[0]


def _plain_bwd_a(name, dy, w):
    T, N = dy.shape
    K = w.shape[0]
    tm, tn = _tile(T, 1024), _tile(K, 1024)
    return _matmul(
        name, dy, w, (), grid=(T // tm, K // tn, 1),
        a_spec=Block((tm, N), lambda i, j, k: (i, 0)),
        b_spec=Block((tn, N), lambda i, j, k: (j, 0)),
        extra_specs=(),
        out_shapes=[SDS((T, K), BF16)],
        out_specs=[Block((tm, tn), lambda i, j, k: (i, j))],
        acc_shape=(tm, tn), dims=NT, nb=None, epilogue=_store())[0]


def _plain_bwd_w(name, a, dy):
    T, K = a.shape
    N = dy.shape[1]
    tt, tr, tn = _tile(T, TT_DEEP), _tile(K, 1024), _tile(N, 1024)
    return _matmul(
        name, a, dy, (), grid=(K // tr, N // tn, T // tt),
        a_spec=Block((tt, tr), lambda m, n, k: (k, m)),
        b_spec=Block((tt, tn), lambda m, n, k: (k, n)),
        extra_specs=(),
        out_shapes=[SDS((K, N), BF16)],
        out_specs=[Block((tr, tn), lambda m, n, k: (m, n))],
        acc_shape=(tr, tn), dims=TN, nb=None, epilogue=_store())[0]


def _rms_fwd(name, x, g):
    T, D = x.shape
    tm = _tile(T, 512)

    def body(x_ref, g_ref, h_ref):
        xv = x_ref[...]
        r = lax.rsqrt(jnp.mean(xv * xv, axis=-1, keepdims=True) + NORM_EPS)
        h_ref[...] = (xv * r * g_ref[...]).astype(BF16)

    return pl.pallas_call(
        body, name=name, grid=(T // tm,),
        in_specs=[Block((tm, D), lambda i: (i, 0)), Block((1, D), lambda i: (0, 0))],
        out_specs=Block((tm, D), lambda i: (i, 0)), out_shape=SDS((T, D), BF16),
        compiler_params=_cparams(("arbitrary",), VMEM_BIG))(x, g)


def _rms_bwd(name, dh, x, g, dxin, out_dtype):
    T, D = x.shape
    tm = _tile(T, 512)

    def body(dh_ref, x_ref, g_ref, dxin_ref, dx_ref, dg_ref):
        i = pl.program_id(0)
        xv = x_ref[...]
        dh = dh_ref[...].astype(F32)
        r = lax.rsqrt(jnp.mean(xv * xv, axis=-1, keepdims=True) + NORM_EPS)
        xh = xv * r
        gd = dh * g_ref[...]
        dx = dxin_ref[...].astype(F32) + r * (gd - xh * jnp.mean(gd * xh, axis=-1, keepdims=True))
        dx_ref[...] = dx.astype(out_dtype)
        dgp = jnp.sum(dh * xh, axis=0, keepdims=True)

        @pl.when(i == 0)
        def _():
            dg_ref[...] = dgp

        @pl.when(i > 0)
        def _():
            dg_ref[...] += dgp

    row = Block((tm, D), lambda i: (i, 0))
    vec = Block((1, D), lambda i: (0, 0))
    return pl.pallas_call(
        body, name=name, grid=(T // tm,),
        in_specs=[row, row, vec, row], out_specs=[row, vec],
        out_shape=[SDS((T, D), out_dtype), SDS((1, D), F32)],
        compiler_params=_cparams(("arbitrary",), VMEM_BIG))(dh, x, g, dxin)


def _loss_head(name, x, g, tgt):
    T, D = x.shape
    tm = _tile(T, 512)

    def body(x_ref, g_ref, t_ref, loss_ref, dxb_ref, dg_ref):
        i = pl.program_id(0)
        xv = x_ref[...]
        gv = g_ref[...]
        r = lax.rsqrt(jnp.mean(xv * xv, axis=-1, keepdims=True) + NORM_EPS)
        xh = xv * r
        err = xh * gv - t_ref[...]
        lp = 0.5 * jnp.sum(jnp.mean(err * err, axis=-1, keepdims=True), axis=0, keepdims=True)
        dout = err * (1.0 / D)
        gd = dout * gv
        dx = r * (gd - xh * jnp.mean(gd * xh, axis=-1, keepdims=True))
        dxb_ref[...] = dx.astype(BF16)
        dgp = jnp.sum(dout * xh, axis=0, keepdims=True)
        lpb = jnp.broadcast_to(lp, (1, 128))

        @pl.when(i == 0)
        def _():
            dg_ref[...] = dgp
            loss_ref[...] = lpb

        @pl.when(i > 0)
        def _():
            dg_ref[...] += dgp
            loss_ref[...] += lpb

    row = Block((tm, D), lambda i: (i, 0))
    vec = Block((1, D), lambda i: (0, 0))
    return pl.pallas_call(
        body, name=name, grid=(T // tm,),
        in_specs=[row, vec, row], out_specs=[Block((1, 128), lambda i: (0, 0)), row, vec],
        out_shape=[SDS((1, 128), F32), SDS((T, D), BF16), SDS((1, D), F32)],
        compiler_params=_cparams(("arbitrary",), VMEM_BIG))(x, g, tgt)


def _merge_fwd(name, pa, pb, gl):
    T, D = pa.shape
    tm = _tile(T, 512)

    def body(pa_ref, pb_ref, gl_ref, o_ref):
        ga = _sigmoid(gl_ref[0].astype(F32))
        gb = _sigmoid(gl_ref[1].astype(F32))
        o_ref[...] = (ga * pa_ref[...].astype(F32) + gb * pb_ref[...].astype(F32)).astype(BF16)

    row = Block((tm, D), lambda i: (i, 0))
    return pl.pallas_call(
        body, name=name, grid=(T // tm,),
        in_specs=[row, row, Block((2, tm, D), lambda i: (0, i, 0))], out_specs=row,
        out_shape=SDS((T, D), BF16), compiler_params=_cparams(("arbitrary",), VMEM_BIG))(pa, pb, gl)


def _merge_bwd(name, dm, pa, pb, gl):
    T, D = pa.shape
    tm = _tile(T, 512)

    def body(dm_ref, pa_ref, pb_ref, gl_ref, dpa_ref, dpb_ref, dgl_ref, db_ref):
        i = pl.program_id(0)
        dmv = dm_ref[...].astype(F32)
        ga = _sigmoid(gl_ref[0].astype(F32))
        gb = _sigmoid(gl_ref[1].astype(F32))
        dpa_ref[...] = (dmv * ga).astype(BF16)
        dpb_ref[...] = (dmv * gb).astype(BF16)
        dga = dmv * pa_ref[...].astype(F32) * ga * (1.0 - ga)
        dgb = dmv * pb_ref[...].astype(F32) * gb * (1.0 - gb)
        dgl_ref[0] = dga.astype(BF16)
        dgl_ref[1] = dgb.astype(BF16)
        sa = jnp.sum(dga, axis=0, keepdims=True)
        sb = jnp.sum(dgb, axis=0, keepdims=True)

        @pl.when(i == 0)
        def _():
            db_ref[0] = sa
            db_ref[1] = sb

        @pl.when(i > 0)
        def _():
            db_ref[0] += sa
            db_ref[1] += sb

    row = Block((tm, D), lambda i: (i, 0))
    two = Block((2, tm, D), lambda i: (0, i, 0))
    return pl.pallas_call(
        body, name=name, grid=(T // tm,),
        in_specs=[row, row, row, two], out_specs=[row, row, two, Block((2, 1, D), lambda i: (0, 0, 0))],
        out_shape=[SDS((T, D), BF16), SDS((T, D), BF16), SDS((2, T, D), BF16), SDS((2, 1, D), F32)],
        compiler_params=_cparams(("arbitrary",), VMEM_BIG))(dm, pa, pb, gl)


def _sgu_core(ur, vr, lng, lnb, ws_ref, bs_ref):
    tm = ur.shape[0]
    gu = _gelu(ur)
    gv = _gelu(vr)
    mu = jnp.mean(gv, axis=-1, keepdims=True)
    cen = gv - mu
    rstd = lax.rsqrt(jnp.mean(cen * cen, axis=-1, keepdims=True) + NORM_EPS)
    xhat = cen * rstd
    vn = (xhat * lng + lnb).astype(BF16)
    rows = []
    for n in range(tm // MLP_CHUNK):
        cols = []
        for h in range(SGU_HEADS):
            blk = vn[n * MLP_CHUNK:(n + 1) * MLP_CHUNK, h * 128:(h + 1) * 128]
            cols.append(_dot(ws_ref[h], blk) + bs_ref[h])
        rows.append(jnp.concatenate(cols, axis=1))
    mixed = jnp.concatenate(rows, axis=0) if len(rows) > 1 else rows[0]
    return gu, xhat, rstd, vn, mixed


def _sgu_fwd(name, proj, lng, lnb, wsm, bst):
    T = proj.shape[0]
    W = SGU_WIDTH
    tm = _tile(T, 512)

    def body(u_ref, v_ref, lng_ref, lnb_ref, ws_ref, bs_ref, o_ref):
        gu, _, _, _, mixed = _sgu_core(u_ref[...].astype(F32), v_ref[...].astype(F32), lng_ref[...], lnb_ref[...],
                                       ws_ref, bs_ref)
        o_ref[...] = (gu * mixed).astype(BF16)

    vec = Block((1, W), lambda i: (0, 0))
    return pl.pallas_call(
        body, name=name, grid=(T // tm,),
        in_specs=[Block((tm, W), lambda i: (i, 1)), Block((tm, W), lambda i: (i, 2)), vec, vec,
                  Block((SGU_HEADS, 128, 128), lambda i: (0, 0, 0)), Block((SGU_HEADS, 128, 128), lambda i: (0, 0, 0))],
        out_specs=Block((tm, W), lambda i: (i, 0)), out_shape=SDS((T, W), BF16),
        compiler_params=_cparams(("arbitrary",), VMEM_BIG))(proj, proj, lng, lnb, wsm, bst)


def _sgu_bwd(name, dyb, dua, proj, lng, lnb, wsm, wsmt, bst):
    T = proj.shape[0]
    W = SGU_WIDTH
    tm = _tile(T, 512)

    def body(dy_ref, dua_ref, u_ref, v_ref, lng_ref, lnb_ref, ws_ref, wst_ref, bs_ref,
             duv_ref, dws_ref, dbs_ref, dlng_ref, dlnb_ref):
        i = pl.program_id(0)
        duv_ref[:, :W] = dua_ref[...]
        ur = u_ref[...].astype(F32)
        vr = v_ref[...].astype(F32)
        lng_v = lng_ref[...]
        gu, xhat, rstd, vn, mixed = _sgu_core(ur, vr, lng_v, lnb_ref[...], ws_ref, bs_ref)
        dy = dy_ref[...].astype(F32)
        dgu = dy * mixed
        dmix = dy * gu
        dmb = dmix.astype(BF16)
        dws_p, dbs_p, rows = [], [], []
        for h in range(SGU_HEADS):
            acc_w = jnp.zeros((128, 128), F32)
            acc_b = jnp.zeros((128, 1), F32)
            for n in range(tm // MLP_CHUNK):
                r0 = n * MLP_CHUNK
                dmt = dmb[r0:r0 + MLP_CHUNK, h * 128:(h + 1) * 128]
                acc_w = acc_w + _dot(dmt, vn[r0:r0 + MLP_CHUNK, h * 128:(h + 1) * 128], NT)
                acc_b = acc_b + jnp.sum(dmix[r0:r0 + MLP_CHUNK, h * 128:(h + 1) * 128], axis=1, keepdims=True)
            dws_p.append(acc_w)
            dbs_p.append(jnp.broadcast_to(acc_b, (128, 128)))
        for n in range(tm // MLP_CHUNK):
            r0 = n * MLP_CHUNK
            rows.append(jnp.concatenate(
                [_dot(wst_ref[h], dmb[r0:r0 + MLP_CHUNK, h * 128:(h + 1) * 128]) for h in range(SGU_HEADS)], axis=1))
        dvn = jnp.concatenate(rows, axis=0) if len(rows) > 1 else rows[0]
        dlng_p = jnp.sum(dvn * xhat, axis=0, keepdims=True)
        dlnb_p = jnp.sum(dvn, axis=0, keepdims=True)
        dxh = dvn * lng_v
        dgv = rstd * (dxh - jnp.mean(dxh, axis=-1, keepdims=True)
                      - xhat * jnp.mean(dxh * xhat, axis=-1, keepdims=True))
        duv_ref[:, W:2 * W] = (dgu * _gelu_grad(ur)).astype(BF16)
        duv_ref[:, 2 * W:] = (dgv * _gelu_grad(vr)).astype(BF16)

        @pl.when(i == 0)
        def _():
            for h in range(SGU_HEADS):
                dws_ref[h] = dws_p[h]
                dbs_ref[h] = dbs_p[h]
            dlng_ref[...] = dlng_p
            dlnb_ref[...] = dlnb_p

        @pl.when(i > 0)
        def _():
            for h in range(SGU_HEADS):
                dws_ref[h] += dws_p[h]
                dbs_ref[h] += dbs_p[h]
            dlng_ref[...] += dlng_p
            dlnb_ref[...] += dlnb_p

    vec = Block((1, W), lambda i: (0, 0))
    wsb = Block((SGU_HEADS, 128, 128), lambda i: (0, 0, 0))
    hsq = SDS((SGU_HEADS, 128, 128), F32)
    return pl.pallas_call(
        body, name=name, grid=(T // tm,),
        in_specs=[Block((tm, W), lambda i: (i, 0)), Block((tm, W), lambda i: (i, 0)),
                  Block((tm, W), lambda i: (i, 1)), Block((tm, W), lambda i: (i, 2)),
                  vec, vec, wsb, wsb, wsb],
        out_specs=[Block((tm, 3 * W), lambda i: (i, 0)), wsb, wsb, vec, vec],
        out_shape=[SDS((T, 3 * W), BF16), hsq, hsq, SDS((1, W), F32), SDS((1, W), F32)],
        compiler_params=_cparams(("arbitrary",), VMEM_BIG))(dyb, dua, proj, proj, lng, lnb, wsm, wsmt, bst)


def _s5_disc(lr, li, ldt, brt, bit):
    dt = jnp.exp(ldt)
    decay = jnp.exp(lr * dt)
    abr = decay * jnp.cos(li * dt)
    abi = decay * jnp.sin(li * dt)
    denom = lr * lr + li * li
    nr = abr - 1.0
    ni = abi
    kr = (nr * lr + ni * li) / denom
    ki = (ni * lr - nr * li) / denom
    bkr = kr[None] * brt - ki[None] * bit
    bki = kr[None] * bit + ki[None] * brt
    return abr, abi, bkr, bki


def _s5_prep(lr, li, ldt, brt, bit):
    G, P, C = S5_GROUPS, S5_STATE, S5_GROUP_WIDTH

    def body(lr_ref, li_ref, ldt_ref, br_ref, bi_ref, abr_ref, abi_ref, pwr_ref, pwi_ref, bkr_ref, bki_ref):
        lr_, li_, ldt_ = lr_ref[...], li_ref[...], ldt_ref[...]
        res = _s5_disc(lr_, li_, ldt_, br_ref[...], bi_ref[...])
        for o, r in zip((abr_ref, abi_ref, bkr_ref, bki_ref), res):
            o[...] = r
        dt = jnp.exp(ldt_)
        n = lax.broadcasted_iota(jnp.int32, (S5_SEG, G, P), 0).astype(F32) + 1.0
        dec = jnp.exp((lr_ * dt)[None] * n)
        ang = (li_ * dt)[None] * n
        pwr_ref[...] = dec * jnp.cos(ang)
        pwi_ref[...] = dec * jnp.sin(ang)

    gp = SDS((G, P), F32)
    sgp = SDS((S5_SEG, G, P), F32)
    cgp = SDS((C, G, P), F32)
    return pl.pallas_call(body, name="s5_prep", out_shape=[gp, gp, sgp, sgp, cgp, cgp])(lr, li, ldt, brt, bit)


def _s5_prep_bwd(lr, li, ldt, brt, bit, dabr, dabi, dbkr, dbki):
    G, P, C = S5_GROUPS, S5_STATE, S5_GROUP_WIDTH

    def body(lr_ref, li_ref, ldt_ref, br_ref, bi_ref, dabr_ref, dabi_ref, dbkr_ref, dbki_ref,
             o_lr, o_li, o_ldt, o_br, o_bi):
        _, pull = jax.vjp(_s5_disc, lr_ref[...], li_ref[...], ldt_ref[...], br_ref[...], bi_ref[...])
        g = pull((dabr_ref[...], dabi_ref[...], dbkr_ref[...], dbki_ref[...]))
        for o, r in zip((o_lr, o_li, o_ldt, o_br, o_bi), g):
            o[...] = r

    gp = SDS((G, P), F32)
    cgp = SDS((C, G, P), F32)
    return pl.pallas_call(body, name="s5_prep_bwd", out_shape=[gp, gp, SDS((G, 1), F32), cgp, cgp])(
        lr, li, ldt, brt, bit, dabr, dabi, dbkr, dbki)


def _s5_scan(buf_ref, ar_row, ai_row, pwr_ref, pwi_ref, carry_ref, LG, xs_ref=None, dar_ref=None, dai_ref=None):
    reverse = xs_ref is not None
    NS, SEG = S5_NS, S5_SEG
    sgn = -1.0 if reverse else 1.0
    for lg in range(NS // LG):
        cr = slice(lg * LG, (lg + 1) * LG)
        ci = slice(NS + lg * LG, NS + (lg + 1) * LG)
        ar1, ai1 = ar_row[:, cr], sgn * ai_row[:, cr]
        asr1, asi1 = pwr_ref[SEG - 1:SEG, cr], sgn * pwi_ref[SEG - 1:SEG, cr]
        ar = jnp.broadcast_to(ar1, (8, LG))
        ai = jnp.broadcast_to(ai1, (8, LG))

        def step_of(j):
            return (SEG - 1 - j) if reverse else j

        def p1(j, st):
            sr, si = st
            rows = pl.ds(pl.multiple_of(step_of(j) * 8, 8), 8)
            nr = ar * sr - ai * si + buf_ref[rows, cr]
            ni = ar * si + ai * sr + buf_ref[rows, ci]
            buf_ref[rows, cr] = nr
            buf_ref[rows, ci] = ni
            return nr, ni

        z = jnp.zeros((8, LG), F32)
        er, ei = lax.fori_loop(0, SEG, p1, (z, z), unroll=S5_UNROLL)
        c_r = carry_ref[:, cr]
        c_i = carry_ref[:, ci]
        cs_r, cs_i = [None] * 8, [None] * 8
        order = range(7, -1, -1) if reverse else range(8)
        for s in order:
            cs_r[s], cs_i[s] = c_r, c_i
            e_r, e_i = er[s:s + 1], ei[s:s + 1]
            c_r, c_i = e_r + asr1 * c_r - asi1 * c_i, e_i + asr1 * c_i + asi1 * c_r
        carry_ref[:, cr] = c_r
        carry_ref[:, ci] = c_i
        cmr = jnp.concatenate(cs_r, axis=0)
        cmi = jnp.concatenate(cs_i, axis=0)

        def carried(j):
            pr = pwr_ref[pl.ds(j, 1), cr]
            pi = sgn * pwi_ref[pl.ds(j, 1), cr]
            return pr * cmr - pi * cmi, pr * cmi + pi * cmr

        if not reverse:
            def p2(j, st):
                rows = pl.ds(pl.multiple_of(j * 8, 8), 8)
                wr, wi = carried(j)
                buf_ref[rows, cr] += wr
                buf_ref[rows, ci] += wi
                return st

            lax.fori_loop(0, SEG, p2, 0, unroll=S5_UNROLL)
        else:
            def p2(j, st):
                pr, pi, dr, di = st
                rows = pl.ds(pl.multiple_of(step_of(j) * 8, 8), 8)
                xr = xs_ref[rows, cr]
                xi = xs_ref[rows, ci]
                dr = dr + pr * xr + pi * xi
                di = di + pi * xr - pr * xi
                wr, wi = carried(j)
                gr = buf_ref[rows, cr] + wr
                gi = buf_ref[rows, ci] + wi
                buf_ref[rows, cr] = gr
                buf_ref[rows, ci] = gi
                return gr, gi, dr, di

            st = lax.fori_loop(0, SEG, p2, (cmr, cmi, z, z), unroll=S5_UNROLL)
            dar_ref[:, cr] += st[2]
            dai_ref[:, cr] += st[3]


def _s5_fwd(proj, perm, permt, bdbr, bdbi, bdcr, bdci, abr, abi, asr, asi, dvec, wglu, bglu):
    T = proj.shape[0]
    TC, NS, W = S5_TC, S5_NS, S5_WIDTH
    nc = T // TC

    def body(u_ref, pm_ref, pmt_ref, bdbr_ref, bdbi_ref, bdcr_ref, bdci_ref, ar_ref, ai_ref, asr_ref, asi_ref,
             d_ref, wglu_ref, bglu_ref, ya_ref, xs_ref, ypre_ref, carry_ref):
        i = pl.program_id(0)

        @pl.when(i == 0)
        def _():
            carry_ref[...] = jnp.zeros_like(carry_ref)

        up = _dot(pm_ref[...], u_ref[...]).astype(BF16)
        for j in range(8):
            ut = up[:, j * 128:(j + 1) * 128]
            xs_ref[:, j * 512:(j + 1) * 512] = _dot(ut, bdbr_ref[j])
            xs_ref[:, NS + j * 512:NS + (j + 1) * 512] = _dot(ut, bdbi_ref[j])
        _s5_scan(xs_ref, ar_ref[...], ai_ref[...], asr_ref, asi_ref, carry_ref, S5_LG)
        ys = []
        for j in range(8):
            xr = xs_ref[:, j * 512:(j + 1) * 512].astype(BF16)
            xi = xs_ref[:, NS + j * 512:NS + (j + 1) * 512].astype(BF16)
            ys.append(_dot(xr, bdcr_ref[j]) + _dot(xi, bdci_ref[j]))
        ypre = jnp.concatenate(ys, axis=1) + d_ref[...] * up.astype(F32)
        ypre_ref[...] = ypre
        ya = _gelu(ypre)
        zl = _dot(ya.astype(BF16), wglu_ref[...]) + bglu_ref[...]
        outp = (ya * _sigmoid(zl)).astype(BF16)
        ya_ref[...] = _dot(pmt_ref[...], outp).astype(BF16)

    return pl.pallas_call(
        body, name="s5_fwd", grid=(nc,),
        in_specs=[Block((TC, W), lambda i: (i, 0)), _const((TC, TC)), _const((TC, TC)),
                  _const((8, 128, 512)), _const((8, 128, 512)), _const((8, 512, 128)), _const((8, 512, 128)),
                  _const((1, NS)), _const((1, NS)), _const((S5_SEG, NS)), _const((S5_SEG, NS)),
                  _const((1, W)), _const((W, W)), _const((1, W))],
        out_specs=[Block((TC, W), lambda i: (i, 0)), Block((TC, 2 * NS), lambda i: (i, 0)),
                   Block((TC, W), lambda i: (i, 0))],
        out_shape=[SDS((T, W), BF16), SDS((T, 2 * NS), F32), SDS((T, W), F32)],
        scratch_shapes=[pltpu.VMEM((1, 2 * NS), F32)],
        compiler_params=_cparams(("arbitrary",), VMEM_BIG),
    )(proj, perm, permt, bdbr, bdbi, bdcr, bdci, abr, abi, asr, asi, dvec, wglu, bglu)


def _s5_bwd(dya, proj, ypre, xs, perm, permt, bdbr, bdbi, bdcr, bdci, abr, abi, asr, asi, dvec, wglu, bglu):
    T = proj.shape[0]
    TC, NS, W = S5_TC, S5_NS, S5_WIDTH
    nc = T // TC

    def body(dya_ref, u_ref, ypre_ref, xs_ref, pm_ref, pmt_ref, bdbr_ref, bdbi_ref, bdcr_ref, bdci_ref,
             ar_ref, ai_ref, asr_ref, asi_ref, d_ref, wglu_ref, bglu_ref,
             du_ref, dar_ref, dai_ref, dd_ref, dbglu_ref, o_dbdbr, o_dbdbi, o_dbdcr, o_dbdci, o_dwglu,
             g_ref, carry_ref, dbdbr_ref, dbdbi_ref, dbdcr_ref, dbdci_ref, dwglu_ref):
        i = pl.program_id(0)

        @pl.when(i == 0)
        def _():
            carry_ref[...] = jnp.zeros_like(carry_ref)
            for r in (dbdbr_ref, dbdbi_ref, dbdcr_ref, dbdci_ref, dar_ref, dai_ref, dd_ref, dwglu_ref, dbglu_ref):
                r[...] = jnp.zeros_like(r)

        pm = pm_ref[...]
        dyo = _dot(pm, dya_ref[...])
        up = _dot(pm, u_ref[...]).astype(BF16)
        upf = up.astype(F32)
        ypre_v = ypre_ref[...]
        ya = _gelu(ypre_v)
        yab = ya.astype(BF16)
        sg = _sigmoid(_dot(yab, wglu_ref[...]) + bglu_ref[...])
        dz = dyo * ya * sg * (1.0 - sg)
        dzb = dz.astype(BF16)
        dya_t = dyo * sg + _dot(dzb, wglu_ref[...], NT)
        dwglu_ref[...] += _dot(yab, dzb, TN)
        dbglu_ref[...] += jnp.sum(dz, axis=0, keepdims=True)
        dy = dya_t * _gelu_grad(ypre_v)
        dd_ref[...] += jnp.sum(dy * upf, axis=0, keepdims=True)
        dyb = dy.astype(BF16)
        for j in range(8):
            dyj = dyb[:, j * 128:(j + 1) * 128]
            g_ref[:, j * 512:(j + 1) * 512] = _dot(dyj, bdcr_ref[j], NT)
            g_ref[:, NS + j * 512:NS + (j + 1) * 512] = _dot(dyj, bdci_ref[j], NT)
            dbdcr_ref[j] += _dot(xs_ref[:, j * 512:(j + 1) * 512].astype(BF16), dyj, TN)
            dbdci_ref[j] += _dot(xs_ref[:, NS + j * 512:NS + (j + 1) * 512].astype(BF16), dyj, TN)
        _s5_scan(g_ref, ar_ref[...], ai_ref[...], asr_ref, asi_ref, carry_ref, S5_LG,
                 xs_ref=xs_ref, dar_ref=dar_ref, dai_ref=dai_ref)
        dus = []
        for j in range(8):
            ut = up[:, j * 128:(j + 1) * 128]
            gr = g_ref[:, j * 512:(j + 1) * 512].astype(BF16)
            gi = g_ref[:, NS + j * 512:NS + (j + 1) * 512].astype(BF16)
            dbdbr_ref[j] += _dot(ut, gr, TN)
            dbdbi_ref[j] += _dot(ut, gi, TN)
            dus.append(_dot(gr, bdbr_ref[j], NT) + _dot(gi, bdbi_ref[j], NT))
        dup = jnp.concatenate(dus, axis=1) + d_ref[...] * dy
        du_ref[...] = _dot(pmt_ref[...], dup.astype(BF16)).astype(BF16)

        @pl.when(i == nc - 1)
        def _():
            for src, dst in ((dbdbr_ref, o_dbdbr), (dbdbi_ref, o_dbdbi), (dbdcr_ref, o_dbdcr),
                             (dbdci_ref, o_dbdci), (dwglu_ref, o_dwglu)):
                pltpu.sync_copy(src, dst)

    c2 = lambda i: (0, 0)
    rev = lambda i: (nc - 1 - i, 0)
    return pl.pallas_call(
        body, name="s5_bwd", grid=(nc,),
        in_specs=[Block((TC, W), rev), Block((TC, W), rev), Block((TC, W), rev), Block((TC, 2 * NS), rev),
                  _const((TC, TC)), _const((TC, TC)),
                  _const((8, 128, 512)), _const((8, 128, 512)), _const((8, 512, 128)), _const((8, 512, 128)),
                  _const((1, NS)), _const((1, NS)), _const((S5_SEG, NS)), _const((S5_SEG, NS)),
                  _const((1, W)), _const((W, W)), _const((1, W))],
        out_specs=[Block((TC, W), rev), Block((8, NS), c2), Block((8, NS), c2), Block((1, W), c2), Block((1, W), c2),
                   ANY, ANY, ANY, ANY, ANY],
        out_shape=[SDS((T, W), BF16), SDS((8, NS), F32), SDS((8, NS), F32), SDS((1, W), F32), SDS((1, W), F32),
                   SDS((8, 128, 512), F32), SDS((8, 128, 512), F32),
                   SDS((8, 512, 128), F32), SDS((8, 512, 128), F32), SDS((W, W), F32)],
        scratch_shapes=[pltpu.VMEM((TC, 2 * NS), F32), pltpu.VMEM((1, 2 * NS), F32),
                        pltpu.VMEM((8, 128, 512), F32), pltpu.VMEM((8, 128, 512), F32),
                        pltpu.VMEM((8, 512, 128), F32), pltpu.VMEM((8, 512, 128), F32), pltpu.VMEM((W, W), F32)],
        compiler_params=_cparams(("arbitrary",), VMEM_BIG),
    )(dya, proj, ypre, xs, perm, permt, bdbr, bdbi, bdcr, bdci, abr, abi, asr, asi, dvec, wglu, bglu)


def _bd_b(bk_t):
    C, P = S5_GROUP_WIDTH, S5_STATE
    t = jnp.transpose(bk_t, (1, 0, 2)).reshape(8, 8, C, P)
    eye = jnp.eye(8, dtype=t.dtype)
    return (t[:, :, :, None, :] * eye[None, :, None, :, None]).reshape(8, 8 * C, 8 * P)


def _bd_b_extract(m):
    C, P = S5_GROUP_WIDTH, S5_STATE
    t = m.reshape(8, 8, C, 8, P)
    d = jnp.stack([t[:, g, :, g, :] for g in range(8)], axis=1)
    return jnp.transpose(d.reshape(S5_GROUPS, C, P), (1, 0, 2))


def _bd_c(c):
    C, P = S5_GROUP_WIDTH, S5_STATE
    t = jnp.transpose(c, (0, 2, 1)).reshape(8, 8, P, C)
    eye = jnp.eye(8, dtype=t.dtype)
    return (t[:, :, :, None, :] * eye[None, :, None, :, None]).reshape(8, 8 * P, 8 * C)


def _bd_c_extract(m):
    C, P = S5_GROUP_WIDTH, S5_STATE
    t = m.reshape(8, 8, P, 8, C)
    d = jnp.stack([t[:, g, :, g, :] for g in range(8)], axis=1)
    return jnp.transpose(d.reshape(S5_GROUPS, P, C), (0, 2, 1))


def _perm_matrix():
    r = jnp.arange(S5_TC)
    src = (r % 8) * S5_SEG + r // 8
    return (src[:, None] == jnp.arange(S5_TC)[None, :]).astype(BF16)


def _coords():
    return lax.axis_index("x"), lax.axis_index("y"), lax.axis_index("c")


def _all_gather(name, arrs):
    n = len(arrs)

    def body(*refs):
        ins, outs = refs[:n], refs[n:2 * n]
        send_sems, recv_sems, local_sems = refs[2 * n:]
        x, y, c = _coords()
        me, sibling = (x, y, c), (x, y, 1 - c)
        chips = [(1 - x, y), (x, 1 - y), (1 - x, 1 - y)]

        def slot(p):
            return 4 * p[0] + 2 * p[1] + p[2]

        def copy(a, k, block, to, src=None):
            dst = outs[a].at[slot(block)]
            return pltpu.make_async_remote_copy(
                src_ref=dst if src is None else src, dst_ref=dst,
                send_sem=send_sems.at[a * 7 + k], recv_sem=recv_sems.at[a * 7 + k],
                device_id=to, device_id_type=MESH)

        mine = [pltpu.make_async_copy(ins[a], outs[a].at[slot(me)], local_sems.at[a]) for a in range(n)]
        for m in mine:
            m.start()
        first = []
        for a in range(n):
            first.append(copy(a, 0, me, sibling, src=ins[a]))
            first += [copy(a, 1 + j, me, (*chip, c), src=ins[a]) for j, chip in enumerate(chips)]
        for cp in first:
            cp.start()
        passed = []
        for j, chip in enumerate(chips):
            for a in range(n):
                copy(a, 1 + j, (*chip, c), me).wait_recv()
                fw = copy(a, 4 + j, (*chip, c), sibling)
                fw.start()
                passed.append(fw)
        for a in range(n):
            copy(a, 0, sibling, me).wait_recv()
            for j, chip in enumerate(chips):
                copy(a, 4 + j, (*chip, 1 - c), me).wait_recv()
        for cp in first + passed:
            cp.wait_send()
        for m in mine:
            m.wait()

    return pl.pallas_call(
        body, name=name,
        in_specs=[ANY] * n, out_specs=[ANY] * n,
        out_shape=[SDS((NDEV,) + a.shape, a.dtype) for a in arrs],
        scratch_shapes=[pltpu.SemaphoreType.DMA((7 * n,)), pltpu.SemaphoreType.DMA((7 * n,)),
                        pltpu.SemaphoreType.DMA((n,))],
    )(*arrs)


HBM = pl.BlockSpec(memory_space=pltpu.HBM)
SEM = pl.BlockSpec(memory_space=pltpu.SEMAPHORE)
EFFECT = pltpu.SideEffectType.DATAFLOW_SIDE_EFFECTING


def _peers7(x, y, c):
    return [(1 - x if fx else x, 1 - y if fy else y, 1 - c if fc else c)
            for fx in (0, 1) for fy in (0, 1) for fc in (0, 1) if fx or fy or fc]


def _slot(p):
    return 4 * p[0] + 2 * p[1] + p[2]


def _split_copies(src_refs, land_refs, send_sems, recv_sems, gather, mine):
    x, y, c = _coords()
    me = (x, y, c)
    out = []
    for a, (src, land) in enumerate(zip(src_refs, land_refs)):
        for k, p in enumerate(_peers7(x, y, c)):
            s = src if gather else src.at[_slot(p)]
            out.append(pltpu.make_async_remote_copy(
                src_ref=s, dst_ref=land.at[_slot(me) if mine else _slot(p)],
                send_sem=send_sems.at[a * 7 + k], recv_sem=recv_sems.at[a * 7 + k],
                device_id=p, device_id_type=MESH))
    return out


def _own_slab(shard):
    x, y, c = _coords()
    z = lax.empty((NDEV,) + shard.shape, shard.dtype)
    return lax.dynamic_update_slice(z, shard[None], (_slot((x, y, c)),) + (0,) * shard.ndim)


def _split_start(name, srcs, lands, gather):
    n = len(srcs)

    def body(*refs):
        src_refs, land_refs = refs[:n], refs[n:2 * n]
        send_sems, recv_sems = refs[2 * n], refs[2 * n + 1]
        token = refs[-1]
        for cp in _split_copies(src_refs, land_refs, send_sems, recv_sems, gather, True):
            cp.start()
        token[...] = jnp.zeros_like(token)

    thru = [pltpu.HBM(a.shape, a.dtype) for a in list(srcs) + list(lands)]
    res = pl.pallas_call(
        body, name=name,
        out_shape=(pltpu.SemaphoreType.DMA((7 * n,)), pltpu.SemaphoreType.DMA((7 * n,)), *thru, SDS((8, 128), F32)),
        in_specs=[HBM] * (2 * n),
        out_specs=(SEM, SEM, *([HBM] * (2 * n)), pl.BlockSpec(memory_space=pltpu.VMEM)),
        input_output_aliases={i: 2 + i for i in range(2 * n)},
        compiler_params=pltpu.CompilerParams(has_side_effects=EFFECT),
    )(*[pltpu.with_memory_space_constraint(a, pltpu.HBM) for a in list(srcs) + list(lands)])
    return res[0], res[1], list(res[2:2 + n]), list(res[2 + n:2 + 2 * n]), res[-1]


def _split_wait(name, started, after, gather):
    send_sems, recv_sems, srcs, lands, _ = started
    n = len(srcs)

    def body(*refs):
        src_refs, land_refs = refs[:n], refs[n:2 * n]
        s_sems, r_sems = refs[2 * n], refs[2 * n + 1]
        for cp in _split_copies(src_refs, land_refs, s_sems, r_sems, gather, False):
            cp.wait_send()
            cp.wait_recv()

    thru = [pltpu.HBM(a.shape, a.dtype) for a in list(srcs) + list(lands)]
    res = pl.pallas_call(
        body, name=name, out_shape=tuple(thru),
        in_specs=[HBM] * (2 * n) + [SEM, SEM, ANY], out_specs=tuple([HBM] * (2 * n)),
        input_output_aliases={i: i for i in range(2 * n)},
        compiler_params=pltpu.CompilerParams(has_side_effects=EFFECT),
    )(*srcs, *lands, send_sems, recv_sems, after)
    return list(res[n:])


def _adam_math(w, g, m, v):
    m = ADAM_B1 * m + (1.0 - ADAM_B1) * g
    v = ADAM_B2 * v + (1.0 - ADAM_B2) * (g * g)
    m_hat = m / (1.0 - ADAM_B1 ** ADAM_STEP)
    v_hat = v / (1.0 - ADAM_B2 ** ADAM_STEP)
    delta = -ADAM_LR * (m_hat / (jnp.sqrt(v_hat) + ADAM_EPS) + ADAM_WD * w)
    return delta, m, v


def _adam_sharded(name, recv, sub, w, m, v):
    R, Cc = w.shape
    tr = max(t for t in range(16, R + 1, 16) if R % t == 0 and t * Cc <= 256 * 1024)

    def body(*refs):
        parts = refs[:NDEV]
        w_ref, m_ref, v_ref, g_out, d_out, m_out, v_out = refs[NDEV:]
        g = parts[0][...].astype(F32)
        for p in parts[1:]:
            g = g + p[...].astype(F32)
        delta, mn, vn = _adam_math(w_ref[...], g, m_ref[...], v_ref[...])
        g_out[...] = g
        d_out[...] = delta
        m_out[...] = mn
        v_out[...] = vn

    if sub is None:
        pspecs = [Block((None, tr, Cc), functools.partial(lambda s, i: (s, i, 0), s)) for s in range(NDEV)]
    else:
        pspecs = [Block((None, None, tr, Cc), functools.partial(lambda s, i: (s, sub, i, 0), s)) for s in range(NDEV)]
    row = Block((tr, Cc), lambda i: (i, 0))
    o = SDS((R, Cc), F32)
    return pl.pallas_call(
        body, name=name, grid=(R // tr,),
        in_specs=pspecs + [row, row, row], out_specs=[row, row, row, row], out_shape=[o, o, o, o],
        compiler_params=_cparams(("arbitrary",), VMEM_BIG))(*([recv] * NDEV), w, m, v)


def _adam_small(groups):
    n = len(groups)

    def body(*refs):
        ins, outs = refs[:4 * n], refs[4 * n:]
        for a in range(n):
            p_ref, w_ref, m_ref, v_ref = ins[4 * a:4 * a + 4]
            g = p_ref[0]
            for s in range(1, NDEV):
                g = g + p_ref[s]
            delta, mn, vn = _adam_math(w_ref[...], g, m_ref[...], v_ref[...])
            for o, r in zip(outs[4 * a:4 * a + 4], (g, delta, mn, vn)):
                o[...] = r

    flat_in = [t for grp in groups for t in grp]
    out_shape = [SDS(grp[1].shape, F32) for grp in groups for _ in range(4)]
    res = pl.pallas_call(body, name="adam_small", out_shape=out_shape,
                         compiler_params=_cparams(None, VMEM_BIG))(*flat_in)
    return [tuple(res[4 * a:4 * a + 4]) for a in range(n)]


_TINY = ["mix_norm", "s5_a_re", "s5_a_im", "s5_log_dt", "s5_d", "s5_b_glu", "sgu_ln_g", "sgu_ln_b",
         "sgu_b_s", "b_gate", "ffn2_norm", "final_norm"]
_ORDER = ["ffn1_norm", "ffn1_w_gate", "ffn1_w_up", "ffn1_w_down", "mix_norm", "w_in", "s5_a_re", "s5_a_im",
          "s5_log_dt", "s5_b_re", "s5_b_im", "s5_c_re", "s5_c_im", "s5_d", "s5_w_glu", "s5_b_glu", "sgu_ln_g",
          "sgu_ln_b", "sgu_w_s", "sgu_b_s", "w_branch_a", "w_branch_b", "w_gate", "b_gate", "w_out", "ffn2_norm",
          "ffn2_w_gate", "ffn2_w_up", "ffn2_w_down", "final_norm"]


def _step(x, tgt, W, M, V):
    T = x.shape[1]
    x0 = x[0]
    tgt0 = tgt[0]
    bf = lambda a: a.astype(BF16)

    def gather_start(name, shards):
        return _split_start(name, shards, [_own_slab(s) for s in shards], True)

    (wgu1,) = _all_gather("gather1", [jnp.stack([bf(W["ffn1_w_gate"][0].T), bf(W["ffn1_w_up"][0].T)])])

    lr_, li_ = W["s5_a_re"][0], W["s5_a_im"][0]
    ldt_ = W["s5_log_dt"][0][:, None]
    brt = jnp.transpose(W["s5_b_re"][0], (2, 0, 1))
    bit = jnp.transpose(W["s5_b_im"][0], (2, 0, 1))
    abr, abi, pwr, pwi, bkr_t, bki_t = _s5_prep(lr_, li_, ldt_, brt, bit)
    bdbr, bdbi = bf(_bd_b(bkr_t)), bf(_bd_b(bki_t))
    bdcr, bdci = bf(_bd_c(W["s5_c_re"][0])), bf(_bd_c(-W["s5_c_im"][0]))
    flat = lambda a: a.reshape(1, S5_NS)
    s5a = (_perm_matrix(), _perm_matrix().T, bdbr, bdbi, bdcr, bdci, flat(abr), flat(abi),
           pwr.reshape(S5_SEG, S5_NS), pwi.reshape(S5_SEG, S5_NS),
           W["s5_d"][0].reshape(1, S5_WIDTH))
    blk = jnp.arange(MLP_CHUNK) // CHUNK
    mask = blk[:, None] >= blk[None, :]
    wsm = jnp.where(mask[None], W["sgu_w_s"][0], 0.0)
    wsm_b, wsmt_b = bf(wsm), bf(jnp.transpose(wsm, (0, 2, 1)))
    bst = jnp.broadcast_to(W["sgu_b_s"][0][:, :, None], (SGU_HEADS, MLP_CHUNK, 128))
    bgate2 = W["b_gate"].reshape(2, 1, D_MODEL)

    h1 = _rms_fwd("rms1", x0, W["ffn1_norm"])
    dep = (wgu1[0, 0, :1, :1] * 0).astype(BF16)

    def later(a):
        return bf(a) + dep[0]

    gs2 = gather_start("gather2_start", [later(W["ffn1_w_down"][0])])
    ab1, f1 = _ffn_up("ffn1_up", h1, wgu1, gs2[4])
    (wd1,) = _split_wait("gather2_wait", gs2, f1, True)
    dep = (wd1[0, :1, :1] * 0).astype(BF16)
    gs3 = gather_start("gather3_start", [later(W["w_in"][0].T), later(W["s5_w_glu"][0])])
    x1 = _ffn_down("ffn1_down", f1, wd1, x0, after=gs3[4])
    h2 = _rms_fwd("rms2", x1, W["mix_norm"])
    win, wglu = _split_wait("gather3_wait", gs3, h2, True)
    wglu = wglu.reshape(S5_WIDTH, S5_WIDTH)
    s5c = s5a + (wglu, W["s5_b_glu"])
    dep = (win[0, :1, :1] * 0).astype(BF16)
    win = win.reshape(S5_WIDTH + 2 * SGU_WIDTH, D_MODEL)
    gs4 = gather_start("gather4_start", [later(W["w_gate"][0]), later(W["w_branch_a"][0]),
                                         later(W["w_branch_b"][0]), later(W["w_out"][0])])
    proj = _rows_fwd("w_in", h2, win, after=gs4[4])
    ya, xs, ypre = _s5_fwd(proj, *s5c)
    dep = (ya[:1, :1] * 0).astype(BF16)
    gs5 = gather_start("gather5_start", [jnp.stack([later(W["ffn2_w_gate"][0].T), later(W["ffn2_w_up"][0].T)])])
    yb = _sgu_fwd("sgu_fwd", proj, W["sgu_ln_g"] + gs5[4][:1, :1], W["sgu_ln_b"], wsm_b, bst)
    wgate, wba, wbb, wout = _split_wait("gather4_wait", gs4, yb, True)
    wout = wout.reshape(D_MODEL, D_MODEL)
    pa = _col_fwd("branch_a", ya, wba)
    pb = _col_fwd("branch_b", yb, wbb)
    gl = _gate_fwd("gate", h2, wgate, bgate2)
    merged = _merge_fwd("merge", pa, pb, gl)
    x2 = _plain_fwd_res("w_out", merged, wout, x1)
    h3 = _rms_fwd("rms3", x2, W["ffn2_norm"])
    (wgu2,) = _split_wait("gather5_wait", gs5, h3, True)
    dep = (wgu2[0, 0, :1, :1] * 0).astype(BF16)
    gs6 = gather_start("gather6_start", [later(W["ffn2_w_down"][0])])
    ab2, f2 = _ffn_up("ffn2_up", h3, wgu2, gs6[4])
    (wd2,) = _split_wait("gather6_wait", gs6, f2, True)
    x3 = _ffn_down("ffn2_down", f2, wd2, x2)
    loss_p, dx3b, dgf = _loss_head("loss_head", x3, W["final_norm"].reshape(1, D_MODEL), tgt0)

    def exchange_start(name, grads):
        x_, y_, c_ = _coords()
        me = _slot((x_, y_, c_))
        return _split_start(name, grads, [_own_slab(lax.dynamic_index_in_dim(g, me, 0, keepdims=False))
                                          for g in grads], False)

    dab2 = _ffn_down_bwd_act("ffn2_down_bwd_a", dx3b, wd2, ab2)
    g_wd2 = _ffn_down_bwd_w("ffn2_down_bwd_w", f2, dx3b)
    g_gu2 = _ffn_up_bwd_w("ffn2_up_bwd_w", h3, dab2)
    es1 = exchange_start("exchange1_start", [g_wd2, g_gu2])
    dh3 = _ffn_up_bwd_h("ffn2_up_bwd_h", dab2, wgu2, es1[4])
    dx2b, dg3 = _rms_bwd("rms3_bwd", dh3, x2, W["ffn2_norm"], dx3b, BF16)

    dmerged = _plain_bwd_a("w_out_bwd_a", dx2b, wout)
    g_wout = _plain_bwd_w("w_out_bwd_w", merged, dx2b)
    dpa, dpb, dgl, dbgate = _merge_bwd("merge_bwd", dmerged, pa, pb, gl)
    dya = _col_bwd_a("branch_a_bwd_a", dpa, wba)
    g_wba = _col_bwd_w("branch_a_bwd_w", ya, dpa, 256)
    dyb = _col_bwd_a("branch_b_bwd_a", dpb, wbb)
    g_wbb = _col_bwd_w("branch_b_bwd_w", yb, dpb, 256)
    dh2g = _gate_bwd_a("gate_bwd_a", dgl, wgate)
    g_wgate = _gate_bwd_w("gate_bwd_w", h2, dgl, 512)
    (dua, dar8, dai8, ddv, dbglu, dbdbr, dbdbi, dbdcr, dbdci, g_wglu) = _s5_bwd(dya, proj, ypre, xs, *s5c)
    dproj, dws, dbst, dlng, dlnb = _sgu_bwd("sgu_bwd", dyb, dua, proj, W["sgu_ln_g"], W["sgu_ln_b"],
                                            wsm_b, wsmt_b, bst)
    g_win = _col_bwd_w("w_in_bwd_w", h2, dproj, 384)
    g_wout3 = g_wout.reshape(NDEV, D_MODEL // NDEV, D_MODEL)
    g_wglu3 = g_wglu.astype(BF16).reshape(NDEV, S5_WIDTH // NDEV, S5_WIDTH)
    es2 = exchange_start("exchange2_start", [g_wout3, g_wba, g_wbb, g_wgate, g_wglu3, g_win])
    dh2 = _rows_bwd_a("w_in_bwd_a", dproj, win, dh2g)
    dx1b, dgm = _rms_bwd("rms2_bwd", dh2, x1, W["mix_norm"] + es2[4][:1, :1], dx2b, BF16)

    dabr = jnp.sum(dar8, axis=0).reshape(S5_GROUPS, S5_STATE)
    dabi = jnp.sum(dai8, axis=0).reshape(S5_GROUPS, S5_STATE)
    d_lr, d_li, d_ldt, d_brt, d_bit = _s5_prep_bwd(lr_, li_, ldt_, brt, bit, dabr, dabi,
                                                   _bd_b_extract(dbdbr), _bd_b_extract(dbdbi))
    small_g = {
        "mix_norm": dgm, "ffn2_norm": dg3, "final_norm": dgf,
        "s5_a_re": d_lr, "s5_a_im": d_li, "s5_log_dt": d_ldt,
        "s5_d": ddv, "s5_b_glu": dbglu, "sgu_ln_g": dlng, "sgu_ln_b": dlnb,
        "sgu_b_s": dbst[:, :, 0], "b_gate": dbgate,
    }
    to_cgp = lambda a: jnp.transpose(a[0], (2, 0, 1))
    from_cgp = lambda a: jnp.transpose(a, (1, 2, 0))[None]
    natural = [
        ("s5_b_re", d_brt, to_cgp, from_cgp), ("s5_b_im", d_bit, to_cgp, from_cgp),
        ("s5_c_re", _bd_c_extract(dbdcr), lambda a: a[0], lambda a: a[None]),
        ("s5_c_im", -_bd_c_extract(dbdci), lambda a: a[0], lambda a: a[None]),
        ("sgu_w_s", jnp.where(mask[None], dws, 0.0), lambda a: a[0], lambda a: a[None]),
    ]
    sizes = [W[n].size for n in _TINY]
    total = sum(sizes) + 1
    rows = -(-total // 128)
    rows = -(-rows // 8) * 8
    pad = rows * 128 - total

    def pack(d, extra):
        return jnp.concatenate([d[n].reshape(-1).astype(F32) for n in _TINY] + [extra, jnp.zeros((pad,), F32)]
                               ).reshape(rows, 128)

    gsm = gather_start("gather_small_start", [pack(small_g, loss_p[0, :1])] + [g for _, g, _, _ in natural])

    dab1 = _ffn_down_bwd_act("ffn1_down_bwd_a", dx1b, wd1, ab1, after=gsm[4])
    g_gu1 = _ffn_up_bwd_w("ffn1_up_bwd_w", h1, dab1)
    es3 = exchange_start("exchange3_start", [g_gu1])
    g_wd1 = _ffn_down_bwd_w("ffn1_down_bwd_w", f1, dx1b, after=es3[4])
    es4 = exchange_start("exchange4_start", [g_wd1])
    dh1 = _ffn_up_bwd_h("ffn1_up_bwd_h", dab1, wgu1, es4[4])
    dx0, dg1 = _rms_bwd("rms1_bwd", dh1, x0, W["ffn1_norm"], dx1b, F32)

    G, Dl, Mn, Vn = {}, {}, {}, {}

    def adam(plan):
        last = None
        for n, recv, sub in plan:
            if sub is None:
                g, d, mn, vn = _adam_sharded("adam_" + n, recv, sub, W[n][0], M[n][0], V[n][0])
                G[n], Dl[n], Mn[n], Vn[n] = g[None], d[None], mn[None], vn[None]
            else:
                tr = jnp.transpose
                g, d, mn, vn = _adam_sharded("adam_" + n, recv, sub, tr(W[n][0]), tr(M[n][0]), tr(V[n][0]))
                G[n], Dl[n], Mn[n], Vn[n] = tr(g)[None], tr(d)[None], tr(mn)[None], tr(vn)[None]
            last = g
        return last

    r_wd2, r_gu2 = _split_wait("exchange1_wait", es1, dx0, False)
    done = adam([("ffn2_w_down", r_wd2, None), ("ffn2_w_gate", r_gu2, 0), ("ffn2_w_up", r_gu2, 1)])
    r_wout, r_wba, r_wbb, r_wgate, r_wglu, r_win = _split_wait("exchange2_wait", es2, done, False)
    done = adam([("w_out", r_wout, None), ("w_branch_a", r_wba, None), ("w_branch_b", r_wbb, None),
                 ("w_gate", r_wgate, None), ("s5_w_glu", r_wglu, None), ("w_in", r_win, None)])

    late = dg1 + 0.0 * done.reshape(-1)[:1]
    zero1 = jnp.zeros((1,), F32)
    parts = _split_wait("gather_small_wait", gsm, late, True)
    (parts_g1,) = _all_gather("gather_ffn1_norm_grad", [late])
    groups = [(parts[0], pack(W, zero1), pack(M, zero1), pack(V, zero1))]
    groups += [(parts[1 + a], view(W[n]), view(M[n]), view(V[n])) for a, (n, _, view, _) in enumerate(natural)]
    groups += [(parts_g1, W["ffn1_norm"], M["ffn1_norm"], V["ffn1_norm"])]
    res = _adam_small(groups)
    sg, sd, sm, sv = res[0]
    for (n, _, _, back), (g, d, mn, vn) in zip(natural, res[1:-1]):
        G[n], Dl[n], Mn[n], Vn[n] = back(g), back(d), back(mn), back(vn)
    G["ffn1_norm"], Dl["ffn1_norm"], Mn["ffn1_norm"], Vn["ffn1_norm"] = res[-1]

    def unpack(flat2d, into):
        flat = flat2d.reshape(-1)
        off = 0
        for n, s in zip(_TINY, sizes):
            into[n] = flat[off:off + s].reshape(W[n].shape)
            off += s
        return flat[off]

    loss = unpack(sg, G)
    unpack(sd, Dl)
    unpack(sm, Mn)
    unpack(sv, Vn)

    (r_gu1,) = _split_wait("exchange3_wait", es3, sg, False)
    done = adam([("ffn1_w_gate", r_gu1, 0), ("ffn1_w_up", r_gu1, 1)])
    (r_wd1,) = _split_wait("exchange4_wait", es4, done, False)
    adam([("ffn1_w_down", r_wd1, None)])

    return loss, dx0[None], G, Dl, Mn, Vn


def kernel(x, ffn1_norm, ffn1_w_gate, ffn1_w_up, ffn1_w_down, mix_norm, w_in, s5_a_re, s5_a_im, s5_log_dt, s5_b_re, s5_b_im, s5_c_re, s5_c_im, s5_d, s5_w_glu, s5_b_glu, sgu_ln_g, sgu_ln_b, sgu_w_s, sgu_b_s, w_branch_a, w_branch_b, w_gate, b_gate, w_out, ffn2_norm, ffn2_w_gate, ffn2_w_up, ffn2_w_down, final_norm, loss_target, m_ffn1_norm, m_ffn1_w_gate, m_ffn1_w_up, m_ffn1_w_down, m_mix_norm, m_w_in, m_s5_a_re, m_s5_a_im, m_s5_log_dt, m_s5_b_re, m_s5_b_im, m_s5_c_re, m_s5_c_im, m_s5_d, m_s5_w_glu, m_s5_b_glu, m_sgu_ln_g, m_sgu_ln_b, m_sgu_w_s, m_sgu_b_s, m_w_branch_a, m_w_branch_b, m_w_gate, m_b_gate, m_w_out, m_ffn2_norm, m_ffn2_w_gate, m_ffn2_w_up, m_ffn2_w_down, m_final_norm, v_ffn1_norm, v_ffn1_w_gate, v_ffn1_w_up, v_ffn1_w_down, v_mix_norm, v_w_in, v_s5_a_re, v_s5_a_im, v_s5_log_dt, v_s5_b_re, v_s5_b_im, v_s5_c_re, v_s5_c_im, v_s5_d, v_s5_w_glu, v_s5_b_glu, v_sgu_ln_g, v_sgu_ln_b, v_sgu_w_s, v_sgu_b_s, v_w_branch_a, v_w_branch_b, v_w_gate, v_b_gate, v_w_out, v_ffn2_norm, v_ffn2_w_gate, v_ffn2_w_up, v_ffn2_w_down, v_final_norm):
    a = locals()
    W = {n: a[n] for n in _ORDER}
    M = {n: a["m_" + n] for n in _ORDER}
    V = {n: a["v_" + n] for n in _ORDER}
    loss, gx, G, Dl, Mn, Vn = _step(x, loss_target, W, M, V)
    return (loss, gx, *[G[n] for n in _ORDER], *[Dl[n] for n in _ORDER], *[Mn[n] for n in _ORDER],
            *[Vn[n] for n in _ORDER])
```

```python
import functools
import math

import jax
import jax.numpy as jnp
from jax import lax
from jax.experimental import pallas as pl
from jax.experimental.pallas import tpu as pltpu

F32 = jnp.float32
BF16 = jnp.bfloat16
NDEV = 8
NORM_EPS = 1e-6
D_MODEL = 2048
D_FF = 5632
FF_SHARD = D_FF // NDEV
S5_WIDTH = 1024
S5_GROUPS = 64
S5_GROUP_WIDTH = 16
S5_STATE = 64
S5_NS = S5_GROUPS * S5_STATE
SGU_WIDTH = 1024
SGU_HEADS = 8
MLP_CHUNK = 128
CHUNK = 64
ADAM_LR, ADAM_B1, ADAM_B2, ADAM_EPS, ADAM_WD, ADAM_STEP = 0.001, 0.9, 0.999, 1e-08, 0.01, 10
S5_TC = 256
S5_SEG = S5_TC // 8
S5_LG = 512
S5_UNROLL = True
VMEM_BIG = 56 * 1024 * 1024

MESH = pl.DeviceIdType.MESH
SDS = jax.ShapeDtypeStruct
Block = pl.BlockSpec
ANY = pl.BlockSpec(memory_space=pl.ANY)


def _cparams(sem=None, vmem=None):
    return pltpu.CompilerParams(dimension_semantics=sem, vmem_limit_bytes=vmem)


def _const(shape):
    nd = len(shape)
    return pl.BlockSpec(shape, lambda i: (0,) * nd, pipeline_mode=pl.Buffered(1))


def _sigmoid(x):
    return 0.5 * jnp.tanh(0.5 * x) + 0.5


_GELU_C = math.sqrt(2.0 / math.pi)


def _gelu(x):
    return 0.5 * x * (1.0 + jnp.tanh(_GELU_C * (x + 0.044715 * x * x * x)))


def _gelu_grad(x):
    t = jnp.tanh(_GELU_C * (x + 0.044715 * x * x * x))
    return 0.5 * (1.0 + t) + 0.5 * x * (1.0 - t * t) * _GELU_C * (1.0 + 3.0 * 0.044715 * x * x)


NN = (((1,), (0,)), ((), ()))
NT = (((1,), (1,)), ((), ()))
TN = (((0,), (0,)), ((), ()))


def _dot(a, b, dims=NN):
    return lax.dot_general(a, b, dims, preferred_element_type=F32)


def _matmul(name, a, b, extras, *, grid, a_spec, b_spec, extra_specs, out_shapes, out_specs, acc_shape,
            epilogue, dims=NN, nb=None, compute=None, after=None, vmem=VMEM_BIG):
    nk = grid[2]
    if after is not None:
        extras = tuple(extras) + (after,)
        extra_specs = list(extra_specs) + [Block((8, 128), lambda i, j, k: (0, 0))]
    ne, no = len(extras), len(out_shapes)
    nacc = nb or 1
    if compute is None:
        def compute(a_ref, b_ref, q):
            return _dot(a_ref[...], b_ref[q] if nb else b_ref[...], dims)

    def body(*refs):
        a_ref, b_ref = refs[0], refs[1]
        ex = refs[2:2 + ne]
        outs = refs[2 + ne:2 + ne + no]
        if nk == 1:
            epilogue([compute(a_ref, b_ref, q) for q in range(nacc)], ex, outs)
            return
        acc_ref = refs[2 + ne + no]
        k = pl.program_id(2)

        @pl.when(k == 0)
        def _():
            acc_ref[...] = jnp.zeros_like(acc_ref)

        for q in range(nacc):
            acc_ref[q] += compute(a_ref, b_ref, q)

        @pl.when(k == nk - 1)
        def _():
            epilogue([acc_ref[q] for q in range(nacc)], ex, outs)

    scratch = [] if nk == 1 else [pltpu.VMEM((nacc,) + tuple(acc_shape), F32)]
    res = pl.pallas_call(
        body, name=name, grid=grid,
        in_specs=[a_spec, b_spec] + list(extra_specs),
        out_specs=list(out_specs), out_shape=list(out_shapes), scratch_shapes=scratch,
        compiler_params=_cparams(("parallel", "parallel", "arbitrary"), vmem),
    )(a, b, *extras)
    return res


def _store(dtype_outs=None):
    def ep(accs, ex, outs):
        outs[0][...] = accs[0].astype(outs[0].dtype)
    return ep


def _tile(n, t):
    t = min(n, t)
    assert n % t == 0, (n, t)
    return t


def _ksum(kq, dims):
    def compute(a_ref, b_ref, _):
        part = _dot(a_ref[0], b_ref[0], dims)
        for q in range(1, kq):
            part = part + _dot(a_ref[q], b_ref[q], dims)
        return part
    return compute


def _ksum_lanes(kq, ns, dims):
    def compute(a_ref, b_ref, _):
        part = _dot(a_ref[:, 0:ns], b_ref[0], dims)
        for q in range(1, kq):
            part = part + _dot(a_ref[:, q * ns:(q + 1) * ns], b_ref[q], dims)
        return part
    return compute


def _wide_b(g):
    def compute(a_ref, b_ref, _):
        bw = b_ref[0] if g == 1 else jnp.concatenate([b_ref[q] for q in range(g)], axis=1)
        return _dot(a_ref[...], bw, NN)
    return compute


TT_DEEP = 2048
TT_FFN = 4096


HIDDEN = NDEV * FF_SHARD


def _ffn_up(name, h, wgu, after=None):
    T, D = h.shape
    tm = _tile(T, 1024)

    def ep(accs, ex, outs):
        a, b = accs
        outs[0][0] = a.astype(BF16)
        outs[0][1] = b.astype(BF16)
        outs[1][...] = (a * _sigmoid(a) * b).astype(BF16)

    return _matmul(
        name, wgu, h, (), after=after, grid=(NDEV, T // tm, 1),
        a_spec=Block((None, 2, FF_SHARD, D), lambda j, i, k: (j, 0, 0, 0)),
        b_spec=Block((tm, D), lambda j, i, k: (i, 0)),
        extra_specs=(),
        out_shapes=[SDS((NDEV, 2, FF_SHARD, T), BF16), SDS((NDEV, FF_SHARD, T), BF16)],
        out_specs=[Block((None, 2, FF_SHARD, tm), lambda j, i, k: (j, 0, 0, i)),
                   Block((None, FF_SHARD, tm), lambda j, i, k: (j, 0, i))],
        acc_shape=(FF_SHARD, tm), nb=2,
        compute=lambda a_ref, b_ref, q: _dot(a_ref[q], b_ref[...], NT), epilogue=ep)


def _ffn_down(name, f, wd, xres, after=None):
    T = f.shape[2]
    tm, tn, tk = _tile(T, 1024), 1024, HIDDEN // 2

    def ep(accs, ex, outs):
        outs[0][...] = ex[0][...] + 0.5 * accs[0]

    return _matmul(
        name, f.reshape(HIDDEN, T), wd.reshape(HIDDEN, D_MODEL), (xres,), after=after,
        grid=(T // tm, D_MODEL // tn, HIDDEN // tk),
        a_spec=Block((tk, tm), lambda i, j, k: (k, i)),
        b_spec=Block((tk, tn), lambda i, j, k: (k, j)),
        extra_specs=[Block((tm, tn), lambda i, j, k: (i, j))],
        out_shapes=[SDS((T, D_MODEL), F32)],
        out_specs=[Block((tm, tn), lambda i, j, k: (i, j))],
        acc_shape=(tm, tn), dims=TN, epilogue=ep)[0]


def _ffn_down_bwd_act(name, dyb, wd, ab, after=None):
    T, D = dyb.shape
    tm, g = _tile(T, 2048), 1

    def ep(accs, ex, outs):
        for s in range(g):
            df = accs[0][s * FF_SHARD:(s + 1) * FF_SHARD, :].astype(BF16)
            a = ex[0][s, 0]
            b = ex[0][s, 1]
            hs = 0.5 * _sigmoid(a)
            outs[0][s, 0] = df * b * hs * (1.0 + a * (1.0 - 2.0 * hs))
            outs[0][s, 1] = df * a * hs

    blk = Block((g, 2, FF_SHARD, tm), lambda i, j, k: (j, 0, 0, i))
    return _matmul(
        name, wd.reshape(HIDDEN, D), dyb, (ab,), after=after, grid=(T // tm, NDEV // g, 1),
        a_spec=Block((g * FF_SHARD, D), lambda i, j, k: (j, 0)),
        b_spec=Block((tm, D), lambda i, j, k: (i, 0)),
        extra_specs=[blk],
        out_shapes=[SDS((NDEV, 2, FF_SHARD, T), BF16)],
        out_specs=[blk],
        acc_shape=(g * FF_SHARD, tm), dims=NT, epilogue=ep)[0]


def _ffn_down_bwd_w(name, f, dyb, after=None):
    T = f.shape[2]
    tt, tn, tr = _tile(T, TT_DEEP), 1024, 2 * FF_SHARD

    def ep(accs, ex, outs):
        outs[0][...] = (0.5 * accs[0]).astype(BF16)

    return _matmul(
        name, f.reshape(HIDDEN, T), dyb, (), after=after, grid=(HIDDEN // tr, D_MODEL // tn, T // tt),
        a_spec=Block((tr, tt), lambda j, n, k: (j, k)),
        b_spec=Block((tt, tn), lambda j, n, k: (k, n)),
        extra_specs=(),
        out_shapes=[SDS((HIDDEN, D_MODEL), BF16)],
        out_specs=[Block((tr, tn), lambda j, n, k: (j, n))],
        acc_shape=(tr, tn), dims=NN, epilogue=ep)[0].reshape(NDEV, FF_SHARD, D_MODEL)


def _ffn_up_bwd_h(name, dab, wgu, after):
    T = dab.shape[3]
    tm, tn, tk = _tile(T, 1024), 2048, HIDDEN // 2
    return _matmul(
        name, dab.reshape(2 * HIDDEN, T), wgu.reshape(2 * HIDDEN, D_MODEL), (), after=after,
        grid=(T // tm, D_MODEL // tn, 2 * HIDDEN // tk),
        a_spec=Block((tk, tm), lambda i, j, k: (k, i)),
        b_spec=Block((tk, tn), lambda i, j, k: (k, j)),
        extra_specs=(),
        out_shapes=[SDS((T, D_MODEL), BF16)],
        out_specs=[Block((tm, tn), lambda i, j, k: (i, j))],
        acc_shape=(tm, tn), dims=TN, epilogue=_store())[0]


def _ffn_up_bwd_w(name, h, dab):
    T, D = h.shape
    tt, tn = _tile(T, TT_FFN), 1024

    def ep(accs, ex, outs):
        outs[0][0] = accs[0].astype(BF16)
        outs[0][1] = accs[1].astype(BF16)

    return _matmul(
        name, dab, h, (), grid=(NDEV, D // tn, T // tt),
        a_spec=Block((None, 2, FF_SHARD, tt), lambda j, n, k: (j, 0, 0, k)),
        b_spec=Block((tt, tn), lambda j, n, k: (k, n)),
        extra_specs=(),
        out_shapes=[SDS((NDEV, 2, FF_SHARD, D), BF16)],
        out_specs=[Block((None, 2, FF_SHARD, tn), lambda j, n, k: (j, 0, 0, n))],
        acc_shape=(FF_SHARD, tn), nb=2,
        compute=lambda a_ref, b_ref, q: _dot(a_ref[q], b_ref[...], NN), epilogue=ep)[0]


def _shards_per_step(ns):
    return max(g for g in (1, 2, 4, 8) if g * ns <= 2048)


def _split_lanes(g, ns):
    def ep(accs, ex, outs):
        for q in range(g):
            outs[0][q] = accs[0][:, q * ns:(q + 1) * ns].astype(outs[0].dtype)
    return ep


def _rows_fwd(name, a, wt, after=None):
    T, K = a.shape
    N = wt.shape[0]
    tm, tn = _tile(T, 1024), _tile(N, 1536)
    return _matmul(
        name, a, wt, (), after=after, grid=(N // tn, T // tm, 1),
        a_spec=Block((tm, K), lambda j, i, k: (i, 0)),
        b_spec=Block((tn, K), lambda j, i, k: (j, 0)),
        extra_specs=(),
        out_shapes=[SDS((T, N), BF16)],
        out_specs=[Block((tm, tn), lambda j, i, k: (i, j))],
        acc_shape=(tm, tn), dims=NT, epilogue=_store())[0]


def _rows_bwd_a(name, dy, wt, add):
    T, N = dy.shape
    K = wt.shape[1]
    tm, tn = _tile(T, 1024), _tile(K, 1024)

    def ep(accs, ex, outs):
        outs[0][...] = (accs[0] + ex[0][...].astype(F32)).astype(BF16)

    return _matmul(
        name, dy, wt, (add,), grid=(T // tm, K // tn, 1),
        a_spec=Block((tm, N), lambda i, j, k: (i, 0)),
        b_spec=Block((N, tn), lambda i, j, k: (0, j)),
        extra_specs=[Block((tm, tn), lambda i, j, k: (i, j))],
        out_shapes=[SDS((T, K), BF16)],
        out_specs=[Block((tm, tn), lambda i, j, k: (i, j))],
        acc_shape=(tm, tn), dims=NN, epilogue=ep)[0]


def _col_fwd(name, a, w, out_dtype=BF16, after=None):
    T, K = a.shape
    ns = w.shape[2]
    g = _shards_per_step(ns)
    tm = _tile(T, 1024)
    return _matmul(
        name, a, w, (), after=after, grid=(NDEV // g, T // tm, 1),
        a_spec=Block((tm, K), lambda j, i, k: (i, 0)),
        b_spec=Block((g, K, ns), lambda j, i, k: (j, 0, 0)),
        extra_specs=(),
        out_shapes=[SDS((T, NDEV * ns), out_dtype)],
        out_specs=[Block((tm, g * ns), lambda j, i, k: (i, j))],
        acc_shape=(tm, g * ns), compute=_wide_b(g), epilogue=_store())[0]


def _col_bwd_a(name, dy, w, add=None):
    T = dy.shape[0]
    _, K, ns = w.shape
    tm, tn = _tile(T, 1024), _tile(K, 1024)

    def ep(accs, ex, outs):
        r = accs[0]
        if add is not None:
            r = r + ex[0][...].astype(F32)
        outs[0][...] = r.astype(BF16)

    extras = () if add is None else (add,)
    return _matmul(
        name, dy, w, extras, grid=(T // tm, K // tn, 1),
        a_spec=Block((tm, NDEV * ns), lambda i, j, k: (i, 0)),
        b_spec=Block((NDEV, tn, ns), lambda i, j, k: (0, j, 0)),
        extra_specs=[Block((tm, tn), lambda i, j, k: (i, j))] * len(extras),
        out_shapes=[SDS((T, K), BF16)],
        out_specs=[Block((tm, tn), lambda i, j, k: (i, j))],
        acc_shape=(tm, tn), compute=_ksum_lanes(NDEV, ns, NT), epilogue=ep)[0]


def _col_bwd_w(name, a, dy, ns):
    T, K = a.shape
    g = _shards_per_step(ns)
    tt, tr = _tile(T, TT_DEEP), _tile(K, 1024)
    return _matmul(
        name, a, dy, (), grid=(NDEV // g, K // tr, T // tt),
        a_spec=Block((tt, tr), lambda j, n, k: (k, n)),
        b_spec=Block((tt, g * ns), lambda j, n, k: (k, j)),
        extra_specs=(),
        out_shapes=[SDS((NDEV, K, ns), BF16)],
        out_specs=[Block((g, tr, ns), lambda j, n, k: (j, n, 0))],
        acc_shape=(tr, g * ns), dims=TN, epilogue=_split_lanes(g, ns))[0]


def _gate_fwd(name, h, w, bias):
    T, K = h.shape
    ns = w.shape[2]
    g = 2
    per = D_MODEL // (g * ns)
    tm = _tile(T, 1024)

    def ep(accs, ex, outs):
        outs[0][...] = (accs[0] + ex[0][...]).astype(BF16)

    return _matmul(
        name, h, w, (bias,), grid=(NDEV // g, T // tm, 1),
        a_spec=Block((tm, K), lambda j, i, k: (i, 0)),
        b_spec=Block((g, K, ns), lambda j, i, k: (j, 0, 0)),
        extra_specs=[Block((None, 1, g * ns), lambda j, i, k: (j // per, 0, j % per))],
        out_shapes=[SDS((2, T, D_MODEL), BF16)],
        out_specs=[Block((None, tm, g * ns), lambda j, i, k: (j // per, i, j % per))],
        acc_shape=(tm, g * ns), compute=_wide_b(g), epilogue=ep)[0]


def _gate_bwd_a(name, dgl, w):
    _, T, _ = dgl.shape
    _, K, ns = w.shape
    per = D_MODEL // ns
    tm, tn = _tile(T, 1024), 1024

    def compute(a_ref, b_ref, _):
        part = None
        for q in range(NDEV):
            d = _dot(a_ref[q // per, :, (q % per) * ns:(q % per + 1) * ns], b_ref[q], NT)
            part = d if part is None else part + d
        return part

    return _matmul(
        name, dgl, w, (), grid=(T // tm, K // tn, 1),
        a_spec=Block((2, tm, D_MODEL), lambda i, j, k: (0, i, 0)),
        b_spec=Block((NDEV, tn, ns), lambda i, j, k: (0, j, 0)),
        extra_specs=(),
        out_shapes=[SDS((T, K), BF16)],
        out_specs=[Block((tm, tn), lambda i, j, k: (i, j))],
        acc_shape=(tm, tn), compute=compute, epilogue=_store())[0]


def _gate_bwd_w(name, h, dgl, ns):
    T, K = h.shape
    g = 2
    per = D_MODEL // (g * ns)
    tt, tr = _tile(T, TT_DEEP), 1024
    return _matmul(
        name, h, dgl, (), grid=(NDEV // g, K // tr, T // tt),
        a_spec=Block((tt, tr), lambda j, n, k: (k, n)),
        b_spec=Block((None, tt, g * ns), lambda j, n, k: (j // per, k, j % per)),
        extra_specs=(),
        out_shapes=[SDS((NDEV, K, ns), BF16)],
        out_specs=[Block((g, tr, ns), lambda j, n, k: (j, n, 0))],
        acc_shape=(tr, g * ns), dims=TN, epilogue=_split_lanes(g, ns))[0]


def _plain_fwd_res(name, a, w, xres):
    T, K = a.shape
    N = w.shape[1]
    tm, tn = _tile(T, 1024), _tile(N, 1024)

    def ep(accs, ex, outs):
        outs[0][...] = ex[0][...] + accs[0]

    return _matmul(
        name, a, w, (xres,), grid=(T // tm, N // tn, 1),
        a_spec=Block((tm, K), lambda i, j, k: (i, 0)),
        b_spec=Block((K, tn), lambda i, j, k: (0, j)),
        extra_specs=[Block((tm, tn), lambda i, j, k: (i, j))],
        out_shapes=[SDS((T, N), F32)],
        out_specs=[Block((tm, tn), lambda i, j, k: (i, j))],
        acc_shape=(tm, tn), dims=NN, nb=None, epilogue=ep)[0]


def _plain_bwd_a(name, dy, w):
    T, N = dy.shape
    K = w.shape[0]
    tm, tn = _tile(T, 1024), _tile(K, 1024)
    return _matmul(
        name, dy, w, (), grid=(T // tm, K // tn, 1),
        a_spec=Block((tm, N), lambda i, j, k: (i, 0)),
        b_spec=Block((tn, N), lambda i, j, k: (j, 0)),
        extra_specs=(),
        out_shapes=[SDS((T, K), BF16)],
        out_specs=[Block((tm, tn), lambda i, j, k: (i, j))],
        acc_shape=(tm, tn), dims=NT, nb=None, epilogue=_store())[0]


def _plain_bwd_w(name, a, dy):
    T, K = a.shape
    N = dy.shape[1]
    tt, tr, tn = _tile(T, TT_DEEP), _tile(K, 1024), _tile(N, 1024)
    return _matmul(
        name, a, dy, (), grid=(K // tr, N // tn, T // tt),
        a_spec=Block((tt, tr), lambda m, n, k: (k, m)),
        b_spec=Block((tt, tn), lambda m, n, k: (k, n)),
        extra_specs=(),
        out_shapes=[SDS((K, N), BF16)],
        out_specs=[Block((tr, tn), lambda m, n, k: (m, n))],
        acc_shape=(tr, tn), dims=TN, nb=None, epilogue=_store())[0]


def _rms_fwd(name, x, g):
    T, D = x.shape
    tm = _tile(T, 512)

    def body(x_ref, g_ref, h_ref):
        xv = x_ref[...]
        r = lax.rsqrt(jnp.mean(xv * xv, axis=-1, keepdims=True) + NORM_EPS)
        h_ref[...] = (xv * r * g_ref[...]).astype(BF16)

    return pl.pallas_call(
        body, name=name, grid=(T // tm,),
        in_specs=[Block((tm, D), lambda i: (i, 0)), Block((1, D), lambda i: (0, 0))],
        out_specs=Block((tm, D), lambda i: (i, 0)), out_shape=SDS((T, D), BF16),
        compiler_params=_cparams(("arbitrary",), VMEM_BIG))(x, g)


def _rms_bwd(name, dh, x, g, dxin, out_dtype):
    T, D = x.shape
    tm = _tile(T, 512)

    def body(dh_ref, x_ref, g_ref, dxin_ref, dx_ref, dg_ref):
        i = pl.program_id(0)
        xv = x_ref[...]
        dh = dh_ref[...].astype(F32)
        r = lax.rsqrt(jnp.mean(xv * xv, axis=-1, keepdims=True) + NORM_EPS)
        xh = xv * r
        gd = dh * g_ref[...]
        dx = dxin_ref[...].astype(F32) + r * (gd - xh * jnp.mean(gd * xh, axis=-1, keepdims=True))
        dx_ref[...] = dx.astype(out_dtype)
        dgp = jnp.sum(dh * xh, axis=0, keepdims=True)

        @pl.when(i == 0)
        def _():
            dg_ref[...] = dgp

        @pl.when(i > 0)
        def _():
            dg_ref[...] += dgp

    row = Block((tm, D), lambda i: (i, 0))
    vec = Block((1, D), lambda i: (0, 0))
    return pl.pallas_call(
        body, name=name, grid=(T // tm,),
        in_specs=[row, row, vec, row], out_specs=[row, vec],
        out_shape=[SDS((T, D), out_dtype), SDS((1, D), F32)],
        compiler_params=_cparams(("arbitrary",), VMEM_BIG))(dh, x, g, dxin)


def _loss_head(name, x, g, tgt):
    T, D = x.shape
    tm = _tile(T, 512)

    def body(x_ref, g_ref, t_ref, loss_ref, dxb_ref, dg_ref):
        i = pl.program_id(0)
        xv = x_ref[...]
        gv = g_ref[...]
        r = lax.rsqrt(jnp.mean(xv * xv, axis=-1, keepdims=True) + NORM_EPS)
        xh = xv * r
        err = xh * gv - t_ref[...]
        lp = 0.5 * jnp.sum(jnp.mean(err * err, axis=-1, keepdims=True), axis=0, keepdims=True)
        dout = err * (1.0 / D)
        gd = dout * gv
        dx = r * (gd - xh * jnp.mean(gd * xh, axis=-1, keepdims=True))
        dxb_ref[...] = dx.astype(BF16)
        dgp = jnp.sum(dout * xh, axis=0, keepdims=True)
        lpb = jnp.broadcast_to(lp, (1, 128))

        @pl.when(i == 0)
        def _():
            dg_ref[...] = dgp
            loss_ref[...] = lpb

        @pl.when(i > 0)
        def _():
            dg_ref[...] += dgp
            loss_ref[...] += lpb

    row = Block((tm, D), lambda i: (i, 0))
    vec = Block((1, D), lambda i: (0, 0))
    return pl.pallas_call(
        body, name=name, grid=(T // tm,),
        in_specs=[row, vec, row], out_specs=[Block((1, 128), lambda i: (0, 0)), row, vec],
        out_shape=[SDS((1, 128), F32), SDS((T, D), BF16), SDS((1, D), F32)],
        compiler_params=_cparams(("arbitrary",), VMEM_BIG))(x, g, tgt)


def _merge_fwd(name, pa, pb, gl):
    T, D = pa.shape
    tm = _tile(T, 512)

    def body(pa_ref, pb_ref, gl_ref, o_ref):
        ga = _sigmoid(gl_ref[0].astype(F32))
        gb = _sigmoid(gl_ref[1].astype(F32))
        o_ref[...] = (ga * pa_ref[...].astype(F32) + gb * pb_ref[...].astype(F32)).astype(BF16)

    row = Block((tm, D), lambda i: (i, 0))
    return pl.pallas_call(
        body, name=name, grid=(T // tm,),
        in_specs=[row, row, Block((2, tm, D), lambda i: (0, i, 0))], out_specs=row,
        out_shape=SDS((T, D), BF16), compiler_params=_cparams(("arbitrary",), VMEM_BIG))(pa, pb, gl)


def _merge_bwd(name, dm, pa, pb, gl):
    T, D = pa.shape
    tm = _tile(T, 512)

    def body(dm_ref, pa_ref, pb_ref, gl_ref, dpa_ref, dpb_ref, dgl_ref, db_ref):
        i = pl.program_id(0)
        dmv = dm_ref[...].astype(F32)
        ga = _sigmoid(gl_ref[0].astype(F32))
        gb = _sigmoid(gl_ref[1].astype(F32))
        dpa_ref[...] = (dmv * ga).astype(BF16)
        dpb_ref[...] = (dmv * gb).astype(BF16)
        dga = dmv * pa_ref[...].astype(F32) * ga * (1.0 - ga)
        dgb = dmv * pb_ref[...].astype(F32) * gb * (1.0 - gb)
        dgl_ref[0] = dga.astype(BF16)
        dgl_ref[1] = dgb.astype(BF16)
        sa = jnp.sum(dga, axis=0, keepdims=True)
        sb = jnp.sum(dgb, axis=0, keepdims=True)

        @pl.when(i == 0)
        def _():
            db_ref[0] = sa
            db_ref[1] = sb

        @pl.when(i > 0)
        def _():
            db_ref[0] += sa
            db_ref[1] += sb

    row = Block((tm, D), lambda i: (i, 0))
    two = Block((2, tm, D), lambda i: (0, i, 0))
    return pl.pallas_call(
        body, name=name, grid=(T // tm,),
        in_specs=[row, row, row, two], out_specs=[row, row, two, Block((2, 1, D), lambda i: (0, 0, 0))],
        out_shape=[SDS((T, D), BF16), SDS((T, D), BF16), SDS((2, T, D), BF16), SDS((2, 1, D), F32)],
        compiler_params=_cparams(("arbitrary",), VMEM_BIG))(dm, pa, pb, gl)


def _sgu_core(ur, vr, lng, lnb, ws_ref, bs_ref):
    tm = ur.shape[0]
    gu = _gelu(ur)
    gv = _gelu(vr)
    mu = jnp.mean(gv, axis=-1, keepdims=True)
    cen = gv - mu
    rstd = lax.rsqrt(jnp.mean(cen * cen, axis=-1, keepdims=True) + NORM_EPS)
    xhat = cen * rstd
    vn = (xhat * lng + lnb).astype(BF16)
    rows = []
    for n in range(tm // MLP_CHUNK):
        cols = []
        for h in range(SGU_HEADS):
            blk = vn[n * MLP_CHUNK:(n + 1) * MLP_CHUNK, h * 128:(h + 1) * 128]
            cols.append(_dot(ws_ref[h], blk) + bs_ref[h])
        rows.append(jnp.concatenate(cols, axis=1))
    mixed = jnp.concatenate(rows, axis=0) if len(rows) > 1 else rows[0]
    return gu, xhat, rstd, vn, mixed


def _sgu_fwd(name, proj, lng, lnb, wsm, bst):
    T = proj.shape[0]
    W = SGU_WIDTH
    tm = _tile(T, 512)

    def body(u_ref, v_ref, lng_ref, lnb_ref, ws_ref, bs_ref, o_ref):
        gu, _, _, _, mixed = _sgu_core(u_ref[...].astype(F32), v_ref[...].astype(F32), lng_ref[...], lnb_ref[...],
                                       ws_ref, bs_ref)
        o_ref[...] = (gu * mixed).astype(BF16)

    vec = Block((1, W), lambda i: (0, 0))
    return pl.pallas_call(
        body, name=name, grid=(T // tm,),
        in_specs=[Block((tm, W), lambda i: (i, 1)), Block((tm, W), lambda i: (i, 2)), vec, vec,
                  Block((SGU_HEADS, 128, 128), lambda i: (0, 0, 0)), Block((SGU_HEADS, 128, 128), lambda i: (0, 0, 0))],
        out_specs=Block((tm, W), lambda i: (i, 0)), out_shape=SDS((T, W), BF16),
        compiler_params=_cparams(("arbitrary",), VMEM_BIG))(proj, proj, lng, lnb, wsm, bst)


def _sgu_bwd(name, dyb, dua, proj, lng, lnb, wsm, wsmt, bst):
    T = proj.shape[0]
    W = SGU_WIDTH
    tm = _tile(T, 512)

    def body(dy_ref, dua_ref, u_ref, v_ref, lng_ref, lnb_ref, ws_ref, wst_ref, bs_ref,
             duv_ref, dws_ref, dbs_ref, dlng_ref, dlnb_ref):
        i = pl.program_id(0)
        duv_ref[:, :W] = dua_ref[...]
        ur = u_ref[...].astype(F32)
        vr = v_ref[...].astype(F32)
        lng_v = lng_ref[...]
        gu, xhat, rstd, vn, mixed = _sgu_core(ur, vr, lng_v, lnb_ref[...], ws_ref, bs_ref)
        dy = dy_ref[...].astype(F32)
        dgu = dy * mixed
        dmix = dy * gu
        dmb = dmix.astype(BF16)
        dws_p, dbs_p, rows = [], [], []
        for h in range(SGU_HEADS):
            acc_w = jnp.zeros((128, 128), F32)
            acc_b = jnp.zeros((128, 1), F32)
            for n in range(tm // MLP_CHUNK):
                r0 = n * MLP_CHUNK
                dmt = dmb[r0:r0 + MLP_CHUNK, h * 128:(h + 1) * 128]
                acc_w = acc_w + _dot(dmt, vn[r0:r0 + MLP_CHUNK, h * 128:(h + 1) * 128], NT)
                acc_b = acc_b + jnp.sum(dmix[r0:r0 + MLP_CHUNK, h * 128:(h + 1) * 128], axis=1, keepdims=True)
            dws_p.append(acc_w)
            dbs_p.append(jnp.broadcast_to(acc_b, (128, 128)))
        for n in range(tm // MLP_CHUNK):
            r0 = n * MLP_CHUNK
            rows.append(jnp.concatenate(
                [_dot(wst_ref[h], dmb[r0:r0 + MLP_CHUNK, h * 128:(h + 1) * 128]) for h in range(SGU_HEADS)], axis=1))
        dvn = jnp.concatenate(rows, axis=0) if len(rows) > 1 else rows[0]
        dlng_p = jnp.sum(dvn * xhat, axis=0, keepdims=True)
        dlnb_p = jnp.sum(dvn, axis=0, keepdims=True)
        dxh = dvn * lng_v
        dgv = rstd * (dxh - jnp.mean(dxh, axis=-1, keepdims=True)
                      - xhat * jnp.mean(dxh * xhat, axis=-1, keepdims=True))
        duv_ref[:, W:2 * W] = (dgu * _gelu_grad(ur)).astype(BF16)
        duv_ref[:, 2 * W:] = (dgv * _gelu_grad(vr)).astype(BF16)

        @pl.when(i == 0)
        def _():
            for h in range(SGU_HEADS):
                dws_ref[h] = dws_p[h]
                dbs_ref[h] = dbs_p[h]
            dlng_ref[...] = dlng_p
            dlnb_ref[...] = dlnb_p

        @pl.when(i > 0)
        def _():
            for h in range(SGU_HEADS):
                dws_ref[h] += dws_p[h]
                dbs_ref[h] += dbs_p[h]
            dlng_ref[...] += dlng_p
            dlnb_ref[...] += dlnb_p

    vec = Block((1, W), lambda i: (0, 0))
    wsb = Block((SGU_HEADS, 128, 128), lambda i: (0, 0, 0))
    hsq = SDS((SGU_HEADS, 128, 128), F32)
    return pl.pallas_call(
        body, name=name, grid=(T // tm,),
        in_specs=[Block((tm, W), lambda i: (i, 0)), Block((tm, W), lambda i: (i, 0)),
                  Block((tm, W), lambda i: (i, 1)), Block((tm, W), lambda i: (i, 2)),
                  vec, vec, wsb, wsb, wsb],
        out_specs=[Block((tm, 3 * W), lambda i: (i, 0)), wsb, wsb, vec, vec],
        out_shape=[SDS((T, 3 * W), BF16), hsq, hsq, SDS((1, W), F32), SDS((1, W), F32)],
        compiler_params=_cparams(("arbitrary",), VMEM_BIG))(dyb, dua, proj, proj, lng, lnb, wsm, wsmt, bst)


def _s5_disc(lr, li, ldt, brt, bit):
    dt = jnp.exp(ldt)
    decay = jnp.exp(lr * dt)
    abr = decay * jnp.cos(li * dt)
    abi = decay * jnp.sin(li * dt)
    denom = lr * lr + li * li
    nr = abr - 1.0
    ni = abi
    kr = (nr * lr + ni * li) / denom
    ki = (ni * lr - nr * li) / denom
    bkr = kr[None] * brt - ki[None] * bit
    bki = kr[None] * bit + ki[None] * brt
    return abr, abi, bkr, bki


def _s5_prep(lr, li, ldt, brt, bit):
    G, P, C = S5_GROUPS, S5_STATE, S5_GROUP_WIDTH

    def body(lr_ref, li_ref, ldt_ref, br_ref, bi_ref, abr_ref, abi_ref, pwr_ref, pwi_ref, bkr_ref, bki_ref):
        lr_, li_, ldt_ = lr_ref[...], li_ref[...], ldt_ref[...]
        res = _s5_disc(lr_, li_, ldt_, br_ref[...], bi_ref[...])
        for o, r in zip((abr_ref, abi_ref, bkr_ref, bki_ref), res):
            o[...] = r
        dt = jnp.exp(ldt_)
        n = lax.broadcasted_iota(jnp.int32, (S5_SEG, G, P), 0).astype(F32) + 1.0
        dec = jnp.exp((lr_ * dt)[None] * n)
        ang = (li_ * dt)[None] * n
        pwr_ref[...] = dec * jnp.cos(ang)
        pwi_ref[...] = dec * jnp.sin(ang)

    gp = SDS((G, P), F32)
    sgp = SDS((S5_SEG, G, P), F32)
    cgp = SDS((C, G, P), F32)
    return pl.pallas_call(body, name="s5_prep", out_shape=[gp, gp, sgp, sgp, cgp, cgp])(lr, li, ldt, brt, bit)


def _s5_prep_bwd(lr, li, ldt, brt, bit, dabr, dabi, dbkr, dbki):
    G, P, C = S5_GROUPS, S5_STATE, S5_GROUP_WIDTH

    def body(lr_ref, li_ref, ldt_ref, br_ref, bi_ref, dabr_ref, dabi_ref, dbkr_ref, dbki_ref,
             o_lr, o_li, o_ldt, o_br, o_bi):
        _, pull = jax.vjp(_s5_disc, lr_ref[...], li_ref[...], ldt_ref[...], br_ref[...], bi_ref[...])
        g = pull((dabr_ref[...], dabi_ref[...], dbkr_ref[...], dbki_ref[...]))
        for o, r in zip((o_lr, o_li, o_ldt, o_br, o_bi), g):
            o[...] = r

    gp = SDS((G, P), F32)
    cgp = SDS((C, G, P), F32)
    return pl.pallas_call(body, name="s5_prep_bwd", out_shape=[gp, gp, SDS((G, 1), F32), cgp, cgp])(
        lr, li, ldt, brt, bit, dabr, dabi, dbkr, dbki)


def _s5_scan(buf_ref, ar_row, ai_row, pwr_ref, pwi_ref, carry_ref, LG, xs_ref=None, dar_ref=None, dai_ref=None):
    reverse = xs_ref is not None
    NS, SEG = S5_NS, S5_SEG
    sgn = -1.0 if reverse else 1.0
    for lg in range(NS // LG):
        cr = slice(lg * LG, (lg + 1) * LG)
        ci = slice(NS + lg * LG, NS + (lg + 1) * LG)
        ar1, ai1 = ar_row[:, cr], sgn * ai_row[:, cr]
        asr1, asi1 = pwr_ref[SEG - 1:SEG, cr], sgn * pwi_ref[SEG - 1:SEG, cr]
        ar = jnp.broadcast_to(ar1, (8, LG))
        ai = jnp.broadcast_to(ai1, (8, LG))

        def step_of(j):
            return (SEG - 1 - j) if reverse else j

        def p1(j, st):
            sr, si = st
            rows = pl.ds(pl.multiple_of(step_of(j) * 8, 8), 8)
            nr = ar * sr - ai * si + buf_ref[rows, cr]
            ni = ar * si + ai * sr + buf_ref[rows, ci]
            buf_ref[rows, cr] = nr
            buf_ref[rows, ci] = ni
            return nr, ni

        z = jnp.zeros((8, LG), F32)
        er, ei = lax.fori_loop(0, SEG, p1, (z, z), unroll=S5_UNROLL)
        c_r = carry_ref[:, cr]
        c_i = carry_ref[:, ci]
        cs_r, cs_i = [None] * 8, [None] * 8
        order = range(7, -1, -1) if reverse else range(8)
        for s in order:
            cs_r[s], cs_i[s] = c_r, c_i
            e_r, e_i = er[s:s + 1], ei[s:s + 1]
            c_r, c_i = e_r + asr1 * c_r - asi1 * c_i, e_i + asr1 * c_i + asi1 * c_r
        carry_ref[:, cr] = c_r
        carry_ref[:, ci] = c_i
        cmr = jnp.concatenate(cs_r, axis=0)
        cmi = jnp.concatenate(cs_i, axis=0)

        def carried(j):
            pr = pwr_ref[pl.ds(j, 1), cr]
            pi = sgn * pwi_ref[pl.ds(j, 1), cr]
            return pr * cmr - pi * cmi, pr * cmi + pi * cmr

        if not reverse:
            def p2(j, st):
                rows = pl.ds(pl.multiple_of(j * 8, 8), 8)
                wr, wi = carried(j)
                buf_ref[rows, cr] += wr
                buf_ref[rows, ci] += wi
                return st

            lax.fori_loop(0, SEG, p2, 0, unroll=S5_UNROLL)
        else:
            def p2(j, st):
                pr, pi, dr, di = st
                rows = pl.ds(pl.multiple_of(step_of(j) * 8, 8), 8)
                xr = xs_ref[rows, cr]
                xi = xs_ref[rows, ci]
                dr = dr + pr * xr + pi * xi
                di = di + pi * xr - pr * xi
                wr, wi = carried(j)
                gr = buf_ref[rows, cr] + wr
                gi = buf_ref[rows, ci] + wi
                buf_ref[rows, cr] = gr
                buf_ref[rows, ci] = gi
                return gr, gi, dr, di

            st = lax.fori_loop(0, SEG, p2, (cmr, cmi, z, z), unroll=S5_UNROLL)
            dar_ref[:, cr] += st[2]
            dai_ref[:, cr] += st[3]


def _s5_fwd(proj, perm, permt, bdbr, bdbi, bdcr, bdci, abr, abi, asr, asi, dvec, wglu, bglu):
    T = proj.shape[0]
    TC, NS, W = S5_TC, S5_NS, S5_WIDTH
    nc = T // TC

    def body(u_ref, pm_ref, pmt_ref, bdbr_ref, bdbi_ref, bdcr_ref, bdci_ref, ar_ref, ai_ref, asr_ref, asi_ref,
             d_ref, wglu_ref, bglu_ref, ya_ref, xs_ref, ypre_ref, carry_ref):
        i = pl.program_id(0)

        @pl.when(i == 0)
        def _():
            carry_ref[...] = jnp.zeros_like(carry_ref)

        up = _dot(pm_ref[...], u_ref[...]).astype(BF16)
        for j in range(8):
            ut = up[:, j * 128:(j + 1) * 128]
            xs_ref[:, j * 512:(j + 1) * 512] = _dot(ut, bdbr_ref[j])
            xs_ref[:, NS + j * 512:NS + (j + 1) * 512] = _dot(ut, bdbi_ref[j])
        _s5_scan(xs_ref, ar_ref[...], ai_ref[...], asr_ref, asi_ref, carry_ref, S5_LG)
        ys = []
        for j in range(8):
            xr = xs_ref[:, j * 512:(j + 1) * 512].astype(BF16)
            xi = xs_ref[:, NS + j * 512:NS + (j + 1) * 512].astype(BF16)
            ys.append(_dot(xr, bdcr_ref[j]) + _dot(xi, bdci_ref[j]))
        ypre = jnp.concatenate(ys, axis=1) + d_ref[...] * up.astype(F32)
        ypre_ref[...] = ypre
        ya = _gelu(ypre)
        zl = _dot(ya.astype(BF16), wglu_ref[...]) + bglu_ref[...]
        outp = (ya * _sigmoid(zl)).astype(BF16)
        ya_ref[...] = _dot(pmt_ref[...], outp).astype(BF16)

    return pl.pallas_call(
        body, name="s5_fwd", grid=(nc,),
        in_specs=[Block((TC, W), lambda i: (i, 0)), _const((TC, TC)), _const((TC, TC)),
                  _const((8, 128, 512)), _const((8, 128, 512)), _const((8, 512, 128)), _const((8, 512, 128)),
                  _const((1, NS)), _const((1, NS)), _const((S5_SEG, NS)), _const((S5_SEG, NS)),
                  _const((1, W)), _const((W, W)), _const((1, W))],
        out_specs=[Block((TC, W), lambda i: (i, 0)), Block((TC, 2 * NS), lambda i: (i, 0)),
                   Block((TC, W), lambda i: (i, 0))],
        out_shape=[SDS((T, W), BF16), SDS((T, 2 * NS), F32), SDS((T, W), F32)],
        scratch_shapes=[pltpu.VMEM((1, 2 * NS), F32)],
        compiler_params=_cparams(("arbitrary",), VMEM_BIG),
    )(proj, perm, permt, bdbr, bdbi, bdcr, bdci, abr, abi, asr, asi, dvec, wglu, bglu)


def _s5_bwd(dya, proj, ypre, xs, perm, permt, bdbr, bdbi, bdcr, bdci, abr, abi, asr, asi, dvec, wglu, bglu):
    T = proj.shape[0]
    TC, NS, W = S5_TC, S5_NS, S5_WIDTH
    nc = T // TC

    def body(dya_ref, u_ref, ypre_ref, xs_ref, pm_ref, pmt_ref, bdbr_ref, bdbi_ref, bdcr_ref, bdci_ref,
             ar_ref, ai_ref, asr_ref, asi_ref, d_ref, wglu_ref, bglu_ref,
             du_ref, dar_ref, dai_ref, dd_ref, dbglu_ref, o_dbdbr, o_dbdbi, o_dbdcr, o_dbdci, o_dwglu,
             g_ref, carry_ref, dbdbr_ref, dbdbi_ref, dbdcr_ref, dbdci_ref, dwglu_ref):
        i = pl.program_id(0)

        @pl.when(i == 0)
        def _():
            carry_ref[...] = jnp.zeros_like(carry_ref)
            for r in (dbdbr_ref, dbdbi_ref, dbdcr_ref, dbdci_ref, dar_ref, dai_ref, dd_ref, dwglu_ref, dbglu_ref):
                r[...] = jnp.zeros_like(r)

        pm = pm_ref[...]
        dyo = _dot(pm, dya_ref[...])
        up = _dot(pm, u_ref[...]).astype(BF16)
        upf = up.astype(F32)
        ypre_v = ypre_ref[...]
        ya = _gelu(ypre_v)
        yab = ya.astype(BF16)
        sg = _sigmoid(_dot(yab, wglu_ref[...]) + bglu_ref[...])
        dz = dyo * ya * sg * (1.0 - sg)
        dzb = dz.astype(BF16)
        dya_t = dyo * sg + _dot(dzb, wglu_ref[...], NT)
        dwglu_ref[...] += _dot(yab, dzb, TN)
        dbglu_ref[...] += jnp.sum(dz, axis=0, keepdims=True)
        dy = dya_t * _gelu_grad(ypre_v)
        dd_ref[...] += jnp.sum(dy * upf, axis=0, keepdims=True)
        dyb = dy.astype(BF16)
        for j in range(8):
            dyj = dyb[:, j * 128:(j + 1) * 128]
            g_ref[:, j * 512:(j + 1) * 512] = _dot(dyj, bdcr_ref[j], NT)
            g_ref[:, NS + j * 512:NS + (j + 1) * 512] = _dot(dyj, bdci_ref[j], NT)
            dbdcr_ref[j] += _dot(xs_ref[:, j * 512:(j + 1) * 512].astype(BF16), dyj, TN)
            dbdci_ref[j] += _dot(xs_ref[:, NS + j * 512:NS + (j + 1) * 512].astype(BF16), dyj, TN)
        _s5_scan(g_ref, ar_ref[...], ai_ref[...], asr_ref, asi_ref, carry_ref, S5_LG,
                 xs_ref=xs_ref, dar_ref=dar_ref, dai_ref=dai_ref)
        dus = []
        for j in range(8):
            ut = up[:, j * 128:(j + 1) * 128]
            gr = g_ref[:, j * 512:(j + 1) * 512].astype(BF16)
            gi = g_ref[:, NS + j * 512:NS + (j + 1) * 512].astype(BF16)
            dbdbr_ref[j] += _dot(ut, gr, TN)
            dbdbi_ref[j] += _dot(ut, gi, TN)
            dus.append(_dot(gr, bdbr_ref[j], NT) + _dot(gi, bdbi_ref[j], NT))
        dup = jnp.concatenate(dus, axis=1) + d_ref[...] * dy
        du_ref[...] = _dot(pmt_ref[...], dup.astype(BF16)).astype(BF16)

        @pl.when(i == nc - 1)
        def _():
            for src, dst in ((dbdbr_ref, o_dbdbr), (dbdbi_ref, o_dbdbi), (dbdcr_ref, o_dbdcr),
                             (dbdci_ref, o_dbdci), (dwglu_ref, o_dwglu)):
                pltpu.sync_copy(src, dst)

    c2 = lambda i: (0, 0)
    rev = lambda i: (nc - 1 - i, 0)
    return pl.pallas_call(
        body, name="s5_bwd", grid=(nc,),
        in_specs=[Block((TC, W), rev), Block((TC, W), rev), Block((TC, W), rev), Block((TC, 2 * NS), rev),
                  _const((TC, TC)), _const((TC, TC)),
                  _const((8, 128, 512)), _const((8, 128, 512)), _const((8, 512, 128)), _const((8, 512, 128)),
                  _const((1, NS)), _const((1, NS)), _const((S5_SEG, NS)), _const((S5_SEG, NS)),
                  _const((1, W)), _const((W, W)), _const((1, W))],
        out_specs=[Block((TC, W), rev), Block((8, NS), c2), Block((8, NS), c2), Block((1, W), c2), Block((1, W), c2),
                   ANY, ANY, ANY, ANY, ANY],
        out_shape=[SDS((T, W), BF16), SDS((8, NS), F32), SDS((8, NS), F32), SDS((1, W), F32), SDS((1, W), F32),
                   SDS((8, 128, 512), F32), SDS((8, 128, 512), F32),
                   SDS((8, 512, 128), F32), SDS((8, 512, 128), F32), SDS((W, W), F32)],
        scratch_shapes=[pltpu.VMEM((TC, 2 * NS), F32), pltpu.VMEM((1, 2 * NS), F32),
                        pltpu.VMEM((8, 128, 512), F32), pltpu.VMEM((8, 128, 512), F32),
                        pltpu.VMEM((8, 512, 128), F32), pltpu.VMEM((8, 512, 128), F32), pltpu.VMEM((W, W), F32)],
        compiler_params=_cparams(("arbitrary",), VMEM_BIG),
    )(dya, proj, ypre, xs, perm, permt, bdbr, bdbi, bdcr, bdci, abr, abi, asr, asi, dvec, wglu, bglu)


def _bd_b(bk_t):
    C, P = S5_GROUP_WIDTH, S5_STATE
    t = jnp.transpose(bk_t, (1, 0, 2)).reshape(8, 8, C, P)
    eye = jnp.eye(8, dtype=t.dtype)
    return (t[:, :, :, None, :] * eye[None, :, None, :, None]).reshape(8, 8 * C, 8 * P)


def _bd_b_extract(m):
    C, P = S5_GROUP_WIDTH, S5_STATE
    t = m.reshape(8, 8, C, 8, P)
    d = jnp.stack([t[:, g, :, g, :] for g in range(8)], axis=1)
    return jnp.transpose(d.reshape(S5_GROUPS, C, P), (1, 0, 2))


def _bd_c(c):
    C, P = S5_GROUP_WIDTH, S5_STATE
    t = jnp.transpose(c, (0, 2, 1)).reshape(8, 8, P, C)
    eye = jnp.eye(8, dtype=t.dtype)
    return (t[:, :, :, None, :] * eye[None, :, None, :, None]).reshape(8, 8 * P, 8 * C)


def _bd_c_extract(m):
    C, P = S5_GROUP_WIDTH, S5_STATE
    t = m.reshape(8, 8, P, 8, C)
    d = jnp.stack([t[:, g, :, g, :] for g in range(8)], axis=1)
    return jnp.transpose(d.reshape(S5_GROUPS, P, C), (0, 2, 1))


def _perm_matrix():
    r = jnp.arange(S5_TC)
    src = (r % 8) * S5_SEG + r // 8
    return (src[:, None] == jnp.arange(S5_TC)[None, :]).astype(BF16)


def _coords():
    return lax.axis_index("x"), lax.axis_index("y"), lax.axis_index("c")


def _all_gather(name, arrs):
    n = len(arrs)

    def body(*refs):
        ins, outs = refs[:n], refs[n:2 * n]
        send_sems, recv_sems, local_sems = refs[2 * n:]
        x, y, c = _coords()
        me, sibling = (x, y, c), (x, y, 1 - c)
        chips = [(1 - x, y), (x, 1 - y), (1 - x, 1 - y)]

        def slot(p):
            return 4 * p[0] + 2 * p[1] + p[2]

        def copy(a, k, block, to, src=None):
            dst = outs[a].at[slot(block)]
            return pltpu.make_async_remote_copy(
                src_ref=dst if src is None else src, dst_ref=dst,
                send_sem=send_sems.at[a * 7 + k], recv_sem=recv_sems.at[a * 7 + k],
                device_id=to, device_id_type=MESH)

        mine = [pltpu.make_async_copy(ins[a], outs[a].at[slot(me)], local_sems.at[a]) for a in range(n)]
        for m in mine:
            m.start()
        first = []
        for a in range(n):
            first.append(copy(a, 0, me, sibling, src=ins[a]))
            first += [copy(a, 1 + j, me, (*chip, c), src=ins[a]) for j, chip in enumerate(chips)]
        for cp in first:
            cp.start()
        passed = []
        for j, chip in enumerate(chips):
            for a in range(n):
                copy(a, 1 + j, (*chip, c), me).wait_recv()
                fw = copy(a, 4 + j, (*chip, c), sibling)
                fw.start()
                passed.append(fw)
        for a in range(n):
            copy(a, 0, sibling, me).wait_recv()
            for j, chip in enumerate(chips):
                copy(a, 4 + j, (*chip, 1 - c), me).wait_recv()
        for cp in first + passed:
            cp.wait_send()
        for m in mine:
            m.wait()

    return pl.pallas_call(
        body, name=name,
        in_specs=[ANY] * n, out_specs=[ANY] * n,
        out_shape=[SDS((NDEV,) + a.shape, a.dtype) for a in arrs],
        scratch_shapes=[pltpu.SemaphoreType.DMA((7 * n,)), pltpu.SemaphoreType.DMA((7 * n,)),
                        pltpu.SemaphoreType.DMA((n,))],
    )(*arrs)


HBM = pl.BlockSpec(memory_space=pltpu.HBM)
SEM = pl.BlockSpec(memory_space=pltpu.SEMAPHORE)
EFFECT = pltpu.SideEffectType.DATAFLOW_SIDE_EFFECTING


def _peers7(x, y, c):
    return [(1 - x if fx else x, 1 - y if fy else y, 1 - c if fc else c)
            for fx in (0, 1) for fy in (0, 1) for fc in (0, 1) if fx or fy or fc]


def _slot(p):
    return 4 * p[0] + 2 * p[1] + p[2]


def _split_copies(src_refs, land_refs, send_sems, recv_sems, gather, mine):
    x, y, c = _coords()
    me = (x, y, c)
    out = []
    if gather == "a":
        peers = [(x, y, 1 - c), (x, 1 - y, c), (1 - x, y, c), (1 - x, 1 - y, c)]
        for a, (src, land) in enumerate(zip(src_refs, land_refs)):
            for k, p in enumerate(peers):
                out.append(pltpu.make_async_remote_copy(
                    src_ref=src, dst_ref=land.at[_slot(me) if mine else _slot(p)],
                    send_sem=send_sems.at[a * 4 + k], recv_sem=recv_sems.at[a * 4 + k],
                    device_id=p, device_id_type=MESH))
        return out
    if gather == "b":
        sibling = (x, y, 1 - c)
        for a, land in enumerate(land_refs):
            for k, q in enumerate([(x, 1 - y), (1 - x, y), (1 - x, 1 - y)]):
                mine_slab = land.at[_slot((q[0], q[1], c))]
                out.append(pltpu.make_async_remote_copy(
                    src_ref=mine_slab, dst_ref=mine_slab if mine else land.at[_slot((q[0], q[1], 1 - c))],
                    send_sem=send_sems.at[a * 3 + k], recv_sem=recv_sems.at[a * 3 + k],
                    device_id=sibling, device_id_type=MESH))
        return out
    for a, (src, land) in enumerate(zip(src_refs, land_refs)):
        for k, p in enumerate(_peers7(x, y, c)):
            s = src if gather else src.at[_slot(p)]
            out.append(pltpu.make_async_remote_copy(
                src_ref=s, dst_ref=land.at[_slot(me) if mine else _slot(p)],
                send_sem=send_sems.at[a * 7 + k], recv_sem=recv_sems.at[a * 7 + k],
                device_id=p, device_id_type=MESH))
    return out


def _copies_per_array(gather):
    return {"a": 4, "b": 3}.get(gather, 7)


def _own_slab(shard):
    x, y, c = _coords()
    z = lax.empty((NDEV,) + shard.shape, shard.dtype)
    return lax.dynamic_update_slice(z, shard[None], (_slot((x, y, c)),) + (0,) * shard.ndim)


def _split_start(name, srcs, lands, gather):
    ns, nl = len(srcs), len(lands)
    nsem = _copies_per_array(gather) * nl

    def body(*refs):
        src_refs, land_refs = refs[:ns], refs[ns:ns + nl]
        send_sems, recv_sems = refs[ns + nl], refs[ns + nl + 1]
        token = refs[-1]
        for cp in _split_copies(src_refs, land_refs, send_sems, recv_sems, gather, True):
            cp.start()
        token[...] = jnp.zeros_like(token)

    thru = [pltpu.HBM(a.shape, a.dtype) for a in list(srcs) + list(lands)]
    res = pl.pallas_call(
        body, name=name,
        out_shape=(pltpu.SemaphoreType.DMA((nsem,)), pltpu.SemaphoreType.DMA((nsem,)), *thru, SDS((8, 128), F32)),
        in_specs=[HBM] * (ns + nl),
        out_specs=(SEM, SEM, *([HBM] * (ns + nl)), pl.BlockSpec(memory_space=pltpu.VMEM)),
        input_output_aliases={i: 2 + i for i in range(ns + nl)},
        compiler_params=pltpu.CompilerParams(has_side_effects=EFFECT),
    )(*[pltpu.with_memory_space_constraint(a, pltpu.HBM) for a in list(srcs) + list(lands)])
    return res[0], res[1], list(res[2:2 + ns]), list(res[2 + ns:2 + ns + nl]), res[-1]


def _split_wait(name, started, after, gather):
    send_sems, recv_sems, srcs, lands, _ = started
    ns, nl = len(srcs), len(lands)

    def body(*refs):
        src_refs, land_refs = refs[:ns], refs[ns:ns + nl]
        s_sems, r_sems = refs[ns + nl], refs[ns + nl + 1]
        for cp in _split_copies(src_refs, land_refs, s_sems, r_sems, gather, False):
            cp.wait_send()
            cp.wait_recv()

    thru = [pltpu.HBM(a.shape, a.dtype) for a in list(srcs) + list(lands)]
    res = pl.pallas_call(
        body, name=name, out_shape=tuple(thru),
        in_specs=[HBM] * (ns + nl) + [SEM, SEM, ANY], out_specs=tuple([HBM] * (ns + nl)),
        input_output_aliases={i: i for i in range(ns + nl)},
        compiler_params=pltpu.CompilerParams(has_side_effects=EFFECT),
    )(*srcs, *lands, send_sems, recv_sems, after)
    return list(res[ns:])


def _adam_math(w, g, m, v):
    m = ADAM_B1 * m + (1.0 - ADAM_B1) * g
    v = ADAM_B2 * v + (1.0 - ADAM_B2) * (g * g)
    m_hat = m / (1.0 - ADAM_B1 ** ADAM_STEP)
    v_hat = v / (1.0 - ADAM_B2 ** ADAM_STEP)
    delta = -ADAM_LR * (m_hat / (jnp.sqrt(v_hat) + ADAM_EPS) + ADAM_WD * w)
    return delta, m, v


def _adam_sharded(name, recv, sub, w, m, v):
    R, Cc = w.shape
    tr = max(t for t in range(16, R + 1, 16) if R % t == 0 and t * Cc <= 256 * 1024)

    def body(*refs):
        parts = refs[:NDEV]
        w_ref, m_ref, v_ref, g_out, d_out, m_out, v_out = refs[NDEV:]
        g = parts[0][...].astype(F32)
        for p in parts[1:]:
            g = g + p[...].astype(F32)
        delta, mn, vn = _adam_math(w_ref[...], g, m_ref[...], v_ref[...])
        g_out[...] = g
        d_out[...] = delta
        m_out[...] = mn
        v_out[...] = vn

    if sub is None:
        pspecs = [Block((None, tr, Cc), functools.partial(lambda s, i: (s, i, 0), s)) for s in range(NDEV)]
    else:
        pspecs = [Block((None, None, tr, Cc), functools.partial(lambda s, i: (s, sub, i, 0), s)) for s in range(NDEV)]
    row = Block((tr, Cc), lambda i: (i, 0))
    o = SDS((R, Cc), F32)
    return pl.pallas_call(
        body, name=name, grid=(R // tr,),
        in_specs=pspecs + [row, row, row], out_specs=[row, row, row, row], out_shape=[o, o, o, o],
        compiler_params=_cparams(("arbitrary",), VMEM_BIG))(*([recv] * NDEV), w, m, v)


def _adam_small(groups):
    n = len(groups)

    def body(*refs):
        ins, outs = refs[:4 * n], refs[4 * n:]
        for a in range(n):
            p_ref, w_ref, m_ref, v_ref = ins[4 * a:4 * a + 4]
            g = p_ref[0]
            for s in range(1, NDEV):
                g = g + p_ref[s]
            delta, mn, vn = _adam_math(w_ref[...], g, m_ref[...], v_ref[...])
            for o, r in zip(outs[4 * a:4 * a + 4], (g, delta, mn, vn)):
                o[...] = r

    flat_in = [t for grp in groups for t in grp]
    out_shape = [SDS(grp[1].shape, F32) for grp in groups for _ in range(4)]
    res = pl.pallas_call(body, name="adam_small", out_shape=out_shape,
                         compiler_params=_cparams(None, VMEM_BIG))(*flat_in)
    return [tuple(res[4 * a:4 * a + 4]) for a in range(n)]


_TINY = ["mix_norm", "s5_a_re", "s5_a_im", "s5_log_dt", "s5_d", "s5_b_glu", "sgu_ln_g", "sgu_ln_b",
         "sgu_b_s", "b_gate", "ffn2_norm", "final_norm"]
_ORDER = ["ffn1_norm", "ffn1_w_gate", "ffn1_w_up", "ffn1_w_down", "mix_norm", "w_in", "s5_a_re", "s5_a_im",
          "s5_log_dt", "s5_b_re", "s5_b_im", "s5_c_re", "s5_c_im", "s5_d", "s5_w_glu", "s5_b_glu", "sgu_ln_g",
          "sgu_ln_b", "sgu_w_s", "sgu_b_s", "w_branch_a", "w_branch_b", "w_gate", "b_gate", "w_out", "ffn2_norm",
          "ffn2_w_gate", "ffn2_w_up", "ffn2_w_down", "final_norm"]


def _step(x, tgt, W, M, V):
    T = x.shape[1]
    x0 = x[0]
    tgt0 = tgt[0]
    bf = lambda a: a.astype(BF16)

    def gather_start(name, shards):
        return _split_start(name, shards, [_own_slab(s) for s in shards], True)

    gu1_sh = jnp.stack([bf(W["ffn1_w_gate"][0].T), bf(W["ffn1_w_up"][0].T)])
    g1a = _split_start("gather1a_start", [gu1_sh], [_own_slab(gu1_sh)], "a")
    tok1 = g1a[4][:1, :1]

    lr_, li_ = W["s5_a_re"][0] + tok1, W["s5_a_im"][0]
    ldt_ = W["s5_log_dt"][0][:, None]
    brt = jnp.transpose(W["s5_b_re"][0], (2, 0, 1))
    bit = jnp.transpose(W["s5_b_im"][0], (2, 0, 1))
    abr, abi, pwr, pwi, bkr_t, bki_t = _s5_prep(lr_, li_, ldt_, brt, bit)
    bdbr, bdbi = bf(_bd_b(bkr_t)), bf(_bd_b(bki_t))
    bdcr, bdci = bf(_bd_c(W["s5_c_re"][0])), bf(_bd_c(-W["s5_c_im"][0]))
    flat = lambda a: a.reshape(1, S5_NS)
    s5a = (_perm_matrix(), _perm_matrix().T, bdbr, bdbi, bdcr, bdci, flat(abr), flat(abi),
           pwr.reshape(S5_SEG, S5_NS), pwi.reshape(S5_SEG, S5_NS),
           W["s5_d"][0].reshape(1, S5_WIDTH))
    blk = jnp.arange(MLP_CHUNK) // CHUNK
    mask = blk[:, None] >= blk[None, :]
    wsm = jnp.where(mask[None], W["sgu_w_s"][0], 0.0)
    wsm_b, wsmt_b = bf(wsm), bf(jnp.transpose(wsm, (0, 2, 1)))
    bst = jnp.broadcast_to(W["sgu_b_s"][0][:, :, None], (SGU_HEADS, MLP_CHUNK, 128))
    bgate2 = W["b_gate"].reshape(2, 1, D_MODEL)

    h1 = _rms_fwd("rms1", x0, W["ffn1_norm"] + tok1)
    (land1,) = _split_wait("gather1a_wait", g1a, h1, "a")
    g1b = _split_start("gather1b_start", [], [land1], "b")
    (wgu1,) = _split_wait("gather1b_wait", g1b, g1b[4], "b")
    dep = (wgu1[0, 0, :1, :1] * 0).astype(BF16)

    def later(a):
        return bf(a) + dep[0]

    gs2 = gather_start("gather2_start", [later(W["ffn1_w_down"][0])])
    ab1, f1 = _ffn_up("ffn1_up", h1, wgu1, gs2[4])
    (wd1,) = _split_wait("gather2_wait", gs2, f1, True)
    dep = (wd1[0, :1, :1] * 0).astype(BF16)
    gs3 = gather_start("gather3_start", [later(W["w_in"][0].T), later(W["s5_w_glu"][0])])
    x1 = _ffn_down("ffn1_down", f1, wd1, x0, after=gs3[4])
    h2 = _rms_fwd("rms2", x1, W["mix_norm"])
    win, wglu = _split_wait("gather3_wait", gs3, h2, True)
    wglu = wglu.reshape(S5_WIDTH, S5_WIDTH)
    s5c = s5a + (wglu, W["s5_b_glu"])
    dep = (win[0, :1, :1] * 0).astype(BF16)
    win = win.reshape(S5_WIDTH + 2 * SGU_WIDTH, D_MODEL)
    gs4 = gather_start("gather4_start", [later(W["w_gate"][0]), later(W["w_branch_a"][0]),
                                         later(W["w_branch_b"][0]), later(W["w_out"][0])])
    proj = _rows_fwd("w_in", h2, win, after=gs4[4])
    ya, xs, ypre = _s5_fwd(proj, *s5c)
    dep = (ya[:1, :1] * 0).astype(BF16)
    gs5 = gather_start("gather5_start", [jnp.stack([later(W["ffn2_w_gate"][0].T), later(W["ffn2_w_up"][0].T)])])
    yb = _sgu_fwd("sgu_fwd", proj, W["sgu_ln_g"] + gs5[4][:1, :1], W["sgu_ln_b"], wsm_b, bst)
    wgate, wba, wbb, wout = _split_wait("gather4_wait", gs4, yb, True)
    wout = wout.reshape(D_MODEL, D_MODEL)
    pa = _col_fwd("branch_a", ya, wba)
    pb = _col_fwd("branch_b", yb, wbb)
    gl = _gate_fwd("gate", h2, wgate, bgate2)
    merged = _merge_fwd("merge", pa, pb, gl)
    x2 = _plain_fwd_res("w_out", merged, wout, x1)
    h3 = _rms_fwd("rms3", x2, W["ffn2_norm"])
    (wgu2,) = _split_wait("gather5_wait", gs5, h3, True)
    dep = (wgu2[0, 0, :1, :1] * 0).astype(BF16)
    gs6 = gather_start("gather6_start", [later(W["ffn2_w_down"][0])])
    ab2, f2 = _ffn_up("ffn2_up", h3, wgu2, gs6[4])
    (wd2,) = _split_wait("gather6_wait", gs6, f2, True)
    x3 = _ffn_down("ffn2_down", f2, wd2, x2)
    loss_p, dx3b, dgf = _loss_head("loss_head", x3, W["final_norm"].reshape(1, D_MODEL), tgt0)

    def exchange_start(name, grads):
        x_, y_, c_ = _coords()
        me = _slot((x_, y_, c_))
        return _split_start(name, grads, [_own_slab(lax.dynamic_index_in_dim(g, me, 0, keepdims=False))
                                          for g in grads], False)

    dab2 = _ffn_down_bwd_act("ffn2_down_bwd_a", dx3b, wd2, ab2)
    g_wd2 = _ffn_down_bwd_w("ffn2_down_bwd_w", f2, dx3b)
    g_gu2 = _ffn_up_bwd_w("ffn2_up_bwd_w", h3, dab2)
    es1 = exchange_start("exchange1_start", [g_wd2, g_gu2])
    dh3 = _ffn_up_bwd_h("ffn2_up_bwd_h", dab2, wgu2, es1[4])
    dx2b, dg3 = _rms_bwd("rms3_bwd", dh3, x2, W["ffn2_norm"], dx3b, BF16)

    dmerged = _plain_bwd_a("w_out_bwd_a", dx2b, wout)
    g_wout = _plain_bwd_w("w_out_bwd_w", merged, dx2b)
    dpa, dpb, dgl, dbgate = _merge_bwd("merge_bwd", dmerged, pa, pb, gl)
    dya = _col_bwd_a("branch_a_bwd_a", dpa, wba)
    g_wba = _col_bwd_w("branch_a_bwd_w", ya, dpa, 256)
    dyb = _col_bwd_a("branch_b_bwd_a", dpb, wbb)
    g_wbb = _col_bwd_w("branch_b_bwd_w", yb, dpb, 256)
    dh2g = _gate_bwd_a("gate_bwd_a", dgl, wgate)
    g_wgate = _gate_bwd_w("gate_bwd_w", h2, dgl, 512)
    (dua, dar8, dai8, ddv, dbglu, dbdbr, dbdbi, dbdcr, dbdci, g_wglu) = _s5_bwd(dya, proj, ypre, xs, *s5c)
    dproj, dws, dbst, dlng, dlnb = _sgu_bwd("sgu_bwd", dyb, dua, proj, W["sgu_ln_g"], W["sgu_ln_b"],
                                            wsm_b, wsmt_b, bst)
    g_win = _col_bwd_w("w_in_bwd_w", h2, dproj, 384)
    g_wout3 = g_wout.reshape(NDEV, D_MODEL // NDEV, D_MODEL)
    g_wglu3 = g_wglu.astype(BF16).reshape(NDEV, S5_WIDTH // NDEV, S5_WIDTH)
    es2 = exchange_start("exchange2_start", [g_wout3, g_wba, g_wbb, g_wgate, g_wglu3, g_win])
    dh2 = _rows_bwd_a("w_in_bwd_a", dproj, win, dh2g)
    dx1b, dgm = _rms_bwd("rms2_bwd", dh2, x1, W["mix_norm"] + es2[4][:1, :1], dx2b, BF16)

    dabr = jnp.sum(dar8, axis=0).reshape(S5_GROUPS, S5_STATE)
    dabi = jnp.sum(dai8, axis=0).reshape(S5_GROUPS, S5_STATE)
    d_lr, d_li, d_ldt, d_brt, d_bit = _s5_prep_bwd(lr_, li_, ldt_, brt, bit, dabr, dabi,
                                                   _bd_b_extract(dbdbr), _bd_b_extract(dbdbi))
    small_g = {
        "mix_norm": dgm, "ffn2_norm": dg3, "final_norm": dgf,
        "s5_a_re": d_lr, "s5_a_im": d_li, "s5_log_dt": d_ldt,
        "s5_d": ddv, "s5_b_glu": dbglu, "sgu_ln_g": dlng, "sgu_ln_b": dlnb,
        "sgu_b_s": dbst[:, :, 0], "b_gate": dbgate,
    }
    to_cgp = lambda a: jnp.transpose(a[0], (2, 0, 1))
    from_cgp = lambda a: jnp.transpose(a, (1, 2, 0))[None]
    natural = [
        ("s5_b_re", d_brt, to_cgp, from_cgp), ("s5_b_im", d_bit, to_cgp, from_cgp),
        ("s5_c_re", _bd_c_extract(dbdcr), lambda a: a[0], lambda a: a[None]),
        ("s5_c_im", -_bd_c_extract(dbdci), lambda a: a[0], lambda a: a[None]),
        ("sgu_w_s", jnp.where(mask[None], dws, 0.0), lambda a: a[0], lambda a: a[None]),
    ]
    sizes = [W[n].size for n in _TINY]
    total = sum(sizes) + 1
    rows = -(-total // 128)
    rows = -(-rows // 8) * 8
    pad = rows * 128 - total

    def pack(d, extra):
        return jnp.concatenate([d[n].reshape(-1).astype(F32) for n in _TINY] + [extra, jnp.zeros((pad,), F32)]
                               ).reshape(rows, 128)

    gsm = gather_start("gather_small_start", [pack(small_g, loss_p[0, :1])] + [g for _, g, _, _ in natural])

    dab1 = _ffn_down_bwd_act("ffn1_down_bwd_a", dx1b, wd1, ab1, after=gsm[4])
    g_gu1 = _ffn_up_bwd_w("ffn1_up_bwd_w", h1, dab1)
    es3 = exchange_start("exchange3_start", [g_gu1])
    g_wd1 = _ffn_down_bwd_w("ffn1_down_bwd_w", f1, dx1b, after=es3[4])
    es4 = exchange_start("exchange4_start", [g_wd1])
    dh1 = _ffn_up_bwd_h("ffn1_up_bwd_h", dab1, wgu1, es4[4])
    dx0, dg1 = _rms_bwd("rms1_bwd", dh1, x0, W["ffn1_norm"], dx1b, F32)

    G, Dl, Mn, Vn = {}, {}, {}, {}

    def adam(plan):
        last = None
        for n, recv, sub in plan:
            if sub is None:
                g, d, mn, vn = _adam_sharded("adam_" + n, recv, sub, W[n][0], M[n][0], V[n][0])
                G[n], Dl[n], Mn[n], Vn[n] = g[None], d[None], mn[None], vn[None]
            else:
                tr = jnp.transpose
                g, d, mn, vn = _adam_sharded("adam_" + n, recv, sub, tr(W[n][0]), tr(M[n][0]), tr(V[n][0]))
                G[n], Dl[n], Mn[n], Vn[n] = tr(g)[None], tr(d)[None], tr(mn)[None], tr(vn)[None]
            last = g
        return last

    r_wd2, r_gu2 = _split_wait("exchange1_wait", es1, dx0, False)
    done = adam([("ffn2_w_down", r_wd2, None), ("ffn2_w_gate", r_gu2, 0), ("ffn2_w_up", r_gu2, 1)])
    r_wout, r_wba, r_wbb, r_wgate, r_wglu, r_win = _split_wait("exchange2_wait", es2, done, False)
    done = adam([("w_out", r_wout, None), ("w_branch_a", r_wba, None), ("w_branch_b", r_wbb, None),
                 ("w_gate", r_wgate, None), ("s5_w_glu", r_wglu, None), ("w_in", r_win, None)])

    late = dg1 + 0.0 * done.reshape(-1)[:1]
    zero1 = jnp.zeros((1,), F32)
    parts = _split_wait("gather_small_wait", gsm, late, True)
    (parts_g1,) = _all_gather("gather_ffn1_norm_grad", [late])
    groups = [(parts[0], pack(W, zero1), pack(M, zero1), pack(V, zero1))]
    groups += [(parts[1 + a], view(W[n]), view(M[n]), view(V[n])) for a, (n, _, view, _) in enumerate(natural)]
    groups += [(parts_g1, W["ffn1_norm"], M["ffn1_norm"], V["ffn1_norm"])]
    res = _adam_small(groups)
    sg, sd, sm, sv = res[0]
    for (n, _, _, back), (g, d, mn, vn) in zip(natural, res[1:-1]):
        G[n], Dl[n], Mn[n], Vn[n] = back(g), back(d), back(mn), back(vn)
    G["ffn1_norm"], Dl["ffn1_norm"], Mn["ffn1_norm"], Vn["ffn1_norm"] = res[-1]

    def unpack(flat2d, into):
        flat = flat2d.reshape(-1)
        off = 0
        for n, s in zip(_TINY, sizes):
            into[n] = flat[off:off + s].reshape(W[n].shape)
            off += s
        return flat[off]

    loss = unpack(sg, G)
    unpack(sd, Dl)
    unpack(sm, Mn)
    unpack(sv, Vn)

    (r_gu1,) = _split_wait("exchange3_wait", es3, sg, False)
    done = adam([("ffn1_w_gate", r_gu1, 0), ("ffn1_w_up", r_gu1, 1)])
    (r_wd1,) = _split_wait("exchange4_wait", es4, done, False)
    adam([("ffn1_w_down", r_wd1, None)])

    return loss, dx0[None], G, Dl, Mn, Vn


def kernel(x, ffn1_norm, ffn1_w_gate, ffn1_w_up, ffn1_w_down, mix_norm, w_in, s5_a_re, s5_a_im, s5_log_dt, s5_b_re, s5_b_im, s5_c_re, s5_c_im, s5_d, s5_w_glu, s5_b_glu, sgu_ln_g, sgu_ln_b, sgu_w_s, sgu_b_s, w_branch_a, w_branch_b, w_gate, b_gate, w_out, ffn2_norm, ffn2_w_gate, ffn2_w_up, ffn2_w_down, final_norm, loss_target, m_ffn1_norm, m_ffn1_w_gate, m_ffn1_w_up, m_ffn1_w_down, m_mix_norm, m_w_in, m_s5_a_re, m_s5_a_im, m_s5_log_dt, m_s5_b_re, m_s5_b_im, m_s5_c_re, m_s5_c_im, m_s5_d, m_s5_w_glu, m_s5_b_glu, m_sgu_ln_g, m_sgu_ln_b, m_sgu_w_s, m_sgu_b_s, m_w_branch_a, m_w_branch_b, m_w_gate, m_b_gate, m_w_out, m_ffn2_norm, m_ffn2_w_gate, m_ffn2_w_up, m_ffn2_w_down, m_final_norm, v_ffn1_norm, v_ffn1_w_gate, v_ffn1_w_up, v_ffn1_w_down, v_mix_norm, v_w_in, v_s5_a_re, v_s5_a_im, v_s5_log_dt, v_s5_b_re, v_s5_b_im, v_s5_c_re, v_s5_c_im, v_s5_d, v_s5_w_glu, v_s5_b_glu, v_sgu_ln_g, v_sgu_ln_b, v_sgu_w_s, v_sgu_b_s, v_w_branch_a, v_w_branch_b, v_w_gate, v_b_gate, v_w_out, v_ffn2_norm, v_ffn2_w_gate, v_ffn2_w_up, v_ffn2_w_down, v_final_norm):
    a = locals()
    W = {n: a[n] for n in _ORDER}
    M = {n: a["m_" + n] for n in _ORDER}
    V = {n: a["v_" + n] for n in _ORDER}
    loss, gx, G, Dl, Mn, Vn = _step(x, loss_target, W, M, V)
    return (loss, gx, *[G[n] for n in _ORDER], *[Dl[n] for n in _ORDER], *[Mn[n] for n in _ORDER],
            *[Vn[n] for n in _ORDER])
```

```python
import functools
import math

import jax
import jax.numpy as jnp
from jax import lax
from jax.experimental import pallas as pl
from jax.experimental.pallas import tpu as pltpu

F32 = jnp.float32
BF16 = jnp.bfloat16
NDEV = 8
NORM_EPS = 1e-6
D_MODEL = 2048
D_FF = 5632
FF_SHARD = D_FF // NDEV
S5_WIDTH = 1024
S5_GROUPS = 64
S5_GROUP_WIDTH = 16
S5_STATE = 64
S5_NS = S5_GROUPS * S5_STATE
SGU_WIDTH = 1024
SGU_HEADS = 8
MLP_CHUNK = 128
CHUNK = 64
ADAM_LR, ADAM_B1, ADAM_B2, ADAM_EPS, ADAM_WD, ADAM_STEP = 0.001, 0.9, 0.999, 1e-08, 0.01, 10
S5_TC = 256
S5_SEG = S5_TC // 8
S5_LG = 512
S5_UNROLL = True
VMEM_BIG = 56 * 1024 * 1024

MESH = pl.DeviceIdType.MESH
SDS = jax.ShapeDtypeStruct
Block = pl.BlockSpec
ANY = pl.BlockSpec(memory_space=pl.ANY)


def _cparams(sem=None, vmem=None):
    return pltpu.CompilerParams(dimension_semantics=sem, vmem_limit_bytes=vmem)


def _const(shape):
    nd = len(shape)
    return pl.BlockSpec(shape, lambda i: (0,) * nd, pipeline_mode=pl.Buffered(1))


def _sigmoid(x):
    return 0.5 * jnp.tanh(0.5 * x) + 0.5


_GELU_C = math.sqrt(2.0 / math.pi)


def _gelu(x):
    return 0.5 * x * (1.0 + jnp.tanh(_GELU_C * (x + 0.044715 * x * x * x)))


def _gelu_grad(x):
    t = jnp.tanh(_GELU_C * (x + 0.044715 * x * x * x))
    return 0.5 * (1.0 + t) + 0.5 * x * (1.0 - t * t) * _GELU_C * (1.0 + 3.0 * 0.044715 * x * x)


NN = (((1,), (0,)), ((), ()))
NT = (((1,), (1,)), ((), ()))
TN = (((0,), (0,)), ((), ()))


def _dot(a, b, dims=NN):
    return lax.dot_general(a, b, dims, preferred_element_type=F32)


def _matmul(name, a, b, extras, *, grid, a_spec, b_spec, extra_specs, out_shapes, out_specs, acc_shape,
            epilogue, dims=NN, nb=None, compute=None, after=None, vmem=VMEM_BIG):
    nk = grid[2]
    if after is not None:
        extras = tuple(extras) + (after,)
        extra_specs = list(extra_specs) + [Block((8, 128), lambda i, j, k: (0, 0))]
    ne, no = len(extras), len(out_shapes)
    nacc = nb or 1
    if compute is None:
        def compute(a_ref, b_ref, q):
            return _dot(a_ref[...], b_ref[q] if nb else b_ref[...], dims)

    def body(*refs):
        a_ref, b_ref = refs[0], refs[1]
        ex = refs[2:2 + ne]
        outs = refs[2 + ne:2 + ne + no]
        if nk == 1:
            epilogue([compute(a_ref, b_ref, q) for q in range(nacc)], ex, outs)
            return
        acc_ref = refs[2 + ne + no]
        k = pl.program_id(2)

        @pl.when(k == 0)
        def _():
            acc_ref[...] = jnp.zeros_like(acc_ref)

        for q in range(nacc):
            acc_ref[q] += compute(a_ref, b_ref, q)

        @pl.when(k == nk - 1)
        def _():
            epilogue([acc_ref[q] for q in range(nacc)], ex, outs)

    scratch = [] if nk == 1 else [pltpu.VMEM((nacc,) + tuple(acc_shape), F32)]
    res = pl.pallas_call(
        body, name=name, grid=grid,
        in_specs=[a_spec, b_spec] + list(extra_specs),
        out_specs=list(out_specs), out_shape=list(out_shapes), scratch_shapes=scratch,
        compiler_params=_cparams(("parallel", "parallel", "arbitrary"), vmem),
    )(a, b, *extras)
    return res


def _store(dtype_outs=None):
    def ep(accs, ex, outs):
        outs[0][...] = accs[0].astype(outs[0].dtype)
    return ep


def _tile(n, t):
    t = min(n, t)
    assert n % t == 0, (n, t)
    return t


def _ksum(kq, dims):
    def compute(a_ref, b_ref, _):
        part = _dot(a_ref[0], b_ref[0], dims)
        for q in range(1, kq):
            part = part + _dot(a_ref[q], b_ref[q], dims)
        return part
    return compute


def _ksum_lanes(kq, ns, dims):
    def compute(a_ref, b_ref, _):
        part = _dot(a_ref[:, 0:ns], b_ref[0], dims)
        for q in range(1, kq):
            part = part + _dot(a_ref[:, q * ns:(q + 1) * ns], b_ref[q], dims)
        return part
    return compute


def _wide_b(g):
    def compute(a_ref, b_ref, _):
        bw = b_ref[0] if g == 1 else jnp.concatenate([b_ref[q] for q in range(g)], axis=1)
        return _dot(a_ref[...], bw, NN)
    return compute


TT_DEEP = 2048
TT_FFN = 4096


HIDDEN = NDEV * FF_SHARD


def _ffn_up(name, h, wgu, after=None):
    T, D = h.shape
    tm = _tile(T, 1024)

    def ep(accs, ex, outs):
        a, b = accs
        outs[0][0] = a.astype(BF16)
        outs[0][1] = b.astype(BF16)
        outs[1][...] = (a * _sigmoid(a) * b).astype(BF16)

    return _matmul(
        name, wgu, h, (), after=after, grid=(NDEV, T // tm, 1),
        a_spec=Block((None, 2, FF_SHARD, D), lambda j, i, k: (j, 0, 0, 0)),
        b_spec=Block((tm, D), lambda j, i, k: (i, 0)),
        extra_specs=(),
        out_shapes=[SDS((NDEV, 2, FF_SHARD, T), BF16), SDS((NDEV, FF_SHARD, T), BF16)],
        out_specs=[Block((None, 2, FF_SHARD, tm), lambda j, i, k: (j, 0, 0, i)),
                   Block((None, FF_SHARD, tm), lambda j, i, k: (j, 0, i))],
        acc_shape=(FF_SHARD, tm), nb=2,
        compute=lambda a_ref, b_ref, q: _dot(a_ref[q], b_ref[...], NT), epilogue=ep)


def _ffn_down(name, f, wd, xres, after=None):
    T = f.shape[2]
    tm, tn, tk = _tile(T, 1024), 1024, HIDDEN // 2

    def ep(accs, ex, outs):
        outs[0][...] = ex[0][...] + 0.5 * accs[0]

    return _matmul(
        name, f.reshape(HIDDEN, T), wd.reshape(HIDDEN, D_MODEL), (xres,), after=after,
        grid=(T // tm, D_MODEL // tn, HIDDEN // tk),
        a_spec=Block((tk, tm), lambda i, j, k: (k, i)),
        b_spec=Block((tk, tn), lambda i, j, k: (k, j)),
        extra_specs=[Block((tm, tn), lambda i, j, k: (i, j))],
        out_shapes=[SDS((T, D_MODEL), F32)],
        out_specs=[Block((tm, tn), lambda i, j, k: (i, j))],
        acc_shape=(tm, tn), dims=TN, epilogue=ep)[0]


def _ffn_down_bwd_act(name, dyb, wd, ab, after=None):
    T, D = dyb.shape
    tm, g = _tile(T, 2048), 1

    def ep(accs, ex, outs):
        for s in range(g):
            df = accs[0][s * FF_SHARD:(s + 1) * FF_SHARD, :].astype(BF16)
            a = ex[0][s, 0]
            b = ex[0][s, 1]
            hs = 0.5 * _sigmoid(a)
            outs[0][s, 0] = df * b * hs * (1.0 + a * (1.0 - 2.0 * hs))
            outs[0][s, 1] = df * a * hs

    blk = Block((g, 2, FF_SHARD, tm), lambda i, j, k: (j, 0, 0, i))
    return _matmul(
        name, wd.reshape(HIDDEN, D), dyb, (ab,), after=after, grid=(T // tm, NDEV // g, 1),
        a_spec=Block((g * FF_SHARD, D), lambda i, j, k: (j, 0)),
        b_spec=Block((tm, D), lambda i, j, k: (i, 0)),
        extra_specs=[blk],
        out_shapes=[SDS((NDEV, 2, FF_SHARD, T), BF16)],
        out_specs=[blk],
        acc_shape=(g * FF_SHARD, tm), dims=NT, epilogue=ep)[0]


def _ffn_down_bwd_w(name, f, dyb, after=None):
    T = f.shape[2]
    tt, tn, tr = _tile(T, TT_DEEP), 1024, 2 * FF_SHARD

    def ep(accs, ex, outs):
        outs[0][...] = (0.5 * accs[0]).astype(BF16)

    return _matmul(
        name, f.reshape(HIDDEN, T), dyb, (), after=after, grid=(HIDDEN // tr, D_MODEL // tn, T // tt),
        a_spec=Block((tr, tt), lambda j, n, k: (j, k)),
        b_spec=Block((tt, tn), lambda j, n, k: (k, n)),
        extra_specs=(),
        out_shapes=[SDS((HIDDEN, D_MODEL), BF16)],
        out_specs=[Block((tr, tn), lambda j, n, k: (j, n))],
        acc_shape=(tr, tn), dims=NN, epilogue=ep)[0].reshape(NDEV, FF_SHARD, D_MODEL)


def _ffn_up_bwd_h(name, dab, wgu, after):
    T = dab.shape[3]
    tm, tn, tk = _tile(T, 1024), 2048, HIDDEN // 2
    return _matmul(
        name, dab.reshape(2 * HIDDEN, T), wgu.reshape(2 * HIDDEN, D_MODEL), (), after=after,
        grid=(T // tm, D_MODEL // tn, 2 * HIDDEN // tk),
        a_spec=Block((tk, tm), lambda i, j, k: (k, i)),
        b_spec=Block((tk, tn), lambda i, j, k: (k, j)),
        extra_specs=(),
        out_shapes=[SDS((T, D_MODEL), BF16)],
        out_specs=[Block((tm, tn), lambda i, j, k: (i, j))],
        acc_shape=(tm, tn), dims=TN, epilogue=_store())[0]


def _ffn_up_bwd_w(name, h, dab):
    T, D = h.shape
    tt, tn = _tile(T, TT_FFN), 1024

    def ep(accs, ex, outs):
        outs[0][0] = accs[0].astype(BF16)
        outs[0][1] = accs[1].astype(BF16)

    return _matmul(
        name, dab, h, (), grid=(NDEV, D // tn, T // tt),
        a_spec=Block((None, 2, FF_SHARD, tt), lambda j, n, k: (j, 0, 0, k)),
        b_spec=Block((tt, tn), lambda j, n, k: (k, n)),
        extra_specs=(),
        out_shapes=[SDS((NDEV, 2, FF_SHARD, D), BF16)],
        out_specs=[Block((None, 2, FF_SHARD, tn), lambda j, n, k: (j, 0, 0, n))],
        acc_shape=(FF_SHARD, tn), nb=2,
        compute=lambda a_ref, b_ref, q: _dot(a_ref[q], b_ref[...], NN), epilogue=ep)[0]


def _shards_per_step(ns):
    return max(g for g in (1, 2, 4, 8) if g * ns <= 2048)


def _split_lanes(g, ns):
    def ep(accs, ex, outs):
        for q in range(g):
            outs[0][q] = accs[0][:, q * ns:(q + 1) * ns].astype(outs[0].dtype)
    return ep


def _rows_fwd(name, a, wt, after=None):
    T, K = a.shape
    N = wt.shape[0]
    tm, tn = _tile(T, 1024), _tile(N, 1536)
    return _matmul(
        name, a, wt, (), after=after, grid=(N // tn, T // tm, 1),
        a_spec=Block((tm, K), lambda j, i, k: (i, 0)),
        b_spec=Block((tn, K), lambda j, i, k: (j, 0)),
        extra_specs=(),
        out_shapes=[SDS((T, N), BF16)],
        out_specs=[Block((tm, tn), lambda j, i, k: (i, j))],
        acc_shape=(tm, tn), dims=NT, epilogue=_store())[0]


def _rows_bwd_a(name, dy, wt, add):
    T, N = dy.shape
    K = wt.shape[1]
    tm, tn = _tile(T, 1024), _tile(K, 1024)

    def ep(accs, ex, outs):
        outs[0][...] = (accs[0] + ex[0][...].astype(F32)).astype(BF16)

    return _matmul(
        name, dy, wt, (add,), grid=(T // tm, K // tn, 1),
        a_spec=Block((tm, N), lambda i, j, k: (i, 0)),
        b_spec=Block((N, tn), lambda i, j, k: (0, j)),
        extra_specs=[Block((tm, tn), lambda i, j, k: (i, j))],
        out_shapes=[SDS((T, K), BF16)],
        out_specs=[Block((tm, tn), lambda i, j, k: (i, j))],
        acc_shape=(tm, tn), dims=NN, epilogue=ep)[0]


def _col_fwd(name, a, w, out_dtype=BF16, after=None):
    T, K = a.shape
    ns = w.shape[2]
    g = _shards_per_step(ns)
    tm = _tile(T, 1024)
    return _matmul(
        name, a, w, (), after=after, grid=(NDEV // g, T // tm, 1),
        a_spec=Block((tm, K), lambda j, i, k: (i, 0)),
        b_spec=Block((g, K, ns), lambda j, i, k: (j, 0, 0)),
        extra_specs=(),
        out_shapes=[SDS((T, NDEV * ns), out_dtype)],
        out_specs=[Block((tm, g * ns), lambda j, i, k: (i, j))],
        acc_shape=(tm, g * ns), compute=_wide_b(g), epilogue=_store())[0]


def _col_bwd_a(name, dy, w, add=None):
    T = dy.shape[0]
    _, K, ns = w.shape
    tm, tn = _tile(T, 1024), _tile(K, 1024)

    def ep(accs, ex, outs):
        r = accs[0]
        if add is not None:
            r = r + ex[0][...].astype(F32)
        outs[0][...] = r.astype(BF16)

    extras = () if add is None else (add,)
    return _matmul(
        name, dy, w, extras, grid=(T // tm, K // tn, 1),
        a_spec=Block((tm, NDEV * ns), lambda i, j, k: (i, 0)),
        b_spec=Block((NDEV, tn, ns), lambda i, j, k: (0, j, 0)),
        extra_specs=[Block((tm, tn), lambda i, j, k: (i, j))] * len(extras),
        out_shapes=[SDS((T, K), BF16)],
        out_specs=[Block((tm, tn), lambda i, j, k: (i, j))],
        acc_shape=(tm, tn), compute=_ksum_lanes(NDEV, ns, NT), epilogue=ep)[0]


def _col_bwd_w(name, a, dy, ns):
    T, K = a.shape
    g = _shards_per_step(ns)
    tt, tr = _tile(T, TT_DEEP), _tile(K, 1024)
    return _matmul(
        name, a, dy, (), grid=(NDEV // g, K // tr, T // tt),
        a_spec=Block((tt, tr), lambda j, n, k: (k, n)),
        b_spec=Block((tt, g * ns), lambda j, n, k: (k, j)),
        extra_specs=(),
        out_shapes=[SDS((NDEV, K, ns), BF16)],
        out_specs=[Block((g, tr, ns), lambda j, n, k: (j, n, 0))],
        acc_shape=(tr, g * ns), dims=TN, epilogue=_split_lanes(g, ns))[0]


def _gate_fwd(name, h, w, bias):
    T, K = h.shape
    ns = w.shape[2]
    g = 2
    per = D_MODEL // (g * ns)
    tm = _tile(T, 1024)

    def ep(accs, ex, outs):
        outs[0][...] = (accs[0] + ex[0][...]).astype(BF16)

    return _matmul(
        name, h, w, (bias,), grid=(NDEV // g, T // tm, 1),
        a_spec=Block((tm, K), lambda j, i, k: (i, 0)),
        b_spec=Block((g, K, ns), lambda j, i, k: (j, 0, 0)),
        extra_specs=[Block((None, 1, g * ns), lambda j, i, k: (j // per, 0, j % per))],
        out_shapes=[SDS((2, T, D_MODEL), BF16)],
        out_specs=[Block((None, tm, g * ns), lambda j, i, k: (j // per, i, j % per))],
        acc_shape=(tm, g * ns), compute=_wide_b(g), epilogue=ep)[0]


def _gate_bwd_a(name, dgl, w):
    _, T, _ = dgl.shape
    _, K, ns = w.shape
    per = D_MODEL // ns
    tm, tn = _tile(T, 1024), 1024

    def compute(a_ref, b_ref, _):
        part = None
        for q in range(NDEV):
            d = _dot(a_ref[q // per, :, (q % per) * ns:(q % per + 1) * ns], b_ref[q], NT)
            part = d if part is None else part + d
        return part

    return _matmul(
        name, dgl, w, (), grid=(T // tm, K // tn, 1),
        a_spec=Block((2, tm, D_MODEL), lambda i, j, k: (0, i, 0)),
        b_spec=Block((NDEV, tn, ns), lambda i, j, k: (0, j, 0)),
        extra_specs=(),
        out_shapes=[SDS((T, K), BF16)],
        out_specs=[Block((tm, tn), lambda i, j, k: (i, j))],
        acc_shape=(tm, tn), compute=compute, epilogue=_store())[0]


def _gate_bwd_w(name, h, dgl, ns):
    T, K = h.shape
    g = 2
    per = D_MODEL // (g * ns)
    tt, tr = _tile(T, TT_DEEP), 1024
    return _matmul(
        name, h, dgl, (), grid=(NDEV // g, K // tr, T // tt),
        a_spec=Block((tt, tr), lambda j, n, k: (k, n)),
        b_spec=Block((None, tt, g * ns), lambda j, n, k: (j // per, k, j % per)),
        extra_specs=(),
        out_shapes=[SDS((NDEV, K, ns), BF16)],
        out_specs=[Block((g, tr, ns), lambda j, n, k: (j, n, 0))],
        acc_shape=(tr, g * ns), dims=TN, epilogue=_split_lanes(g, ns))[0]


def _plain_fwd_res(name, a, w, xres):
    T, K = a.shape
    N = w.shape[1]
    tm, tn = _tile(T, 1024), _tile(N, 1024)

    def ep(accs, ex, outs):
        outs[0][...] = ex[0][...] + accs[0]

    return _matmul(
        name, a, w, (xres,), grid=(T // tm, N // tn, 1),
        a_spec=Block((tm, K), lambda i, j, k: (i, 0)),
        b_spec=Block((K, tn), lambda i, j, k: (0, j)),
        extra_specs=[Block((tm, tn), lambda i, j, k: (i, j))],
        out_shapes=[SDS((T, N), F32)],
        out_specs=[Block((tm, tn), lambda i, j, k: (i, j))],
        acc_shape=(tm, tn), dims=NN, nb=None, epilogue=ep)[0]


def _plain_bwd_a(name, dy, w):
    T, N = dy.shape
    K = w.shape[0]
    tm, tn = _tile(T, 1024), _tile(K, 1024)
    return _matmul(
        name, dy, w, (), grid=(T // tm, K // tn, 1),
        a_spec=Block((tm, N), lambda i, j, k: (i, 0)),
        b_spec=Block((tn, N), lambda i, j, k: (j, 0)),
        extra_specs=(),
        out_shapes=[SDS((T, K), BF16)],
        out_specs=[Block((tm, tn), lambda i, j, k: (i, j))],
        acc_shape=(tm, tn), dims=NT, nb=None, epilogue=_store())[0]


def _plain_bwd_w(name, a, dy):
    T, K = a.shape
    N = dy.shape[1]
    tt, tr, tn = _tile(T, TT_DEEP), _tile(K, 1024), _tile(N, 1024)
    return _matmul(
        name, a, dy, (), grid=(K // tr, N // tn, T // tt),
        a_spec=Block((tt, tr), lambda m, n, k: (k, m)),
        b_spec=Block((tt, tn), lambda m, n, k: (k, n)),
        extra_specs=(),
        out_shapes=[SDS((K, N), BF16)],
        out_specs=[Block((tr, tn), lambda m, n, k: (m, n))],
        acc_shape=(tr, tn), dims=TN, nb=None, epilogue=_store())[0]


def _rms_fwd(name, x, g):
    T, D = x.shape
    tm = _tile(T, 512)

    def body(x_ref, g_ref, h_ref):
        xv = x_ref[...]
        r = lax.rsqrt(jnp.mean(xv * xv, axis=-1, keepdims=True) + NORM_EPS)
        h_ref[...] = (xv * r * g_ref[...]).astype(BF16)

    return pl.pallas_call(
        body, name=name, grid=(T // tm,),
        in_specs=[Block((tm, D), lambda i: (i, 0)), Block((1, D), lambda i: (0, 0))],
        out_specs=Block((tm, D), lambda i: (i, 0)), out_shape=SDS((T, D), BF16),
        compiler_params=_cparams(("arbitrary",), VMEM_BIG))(x, g)


def _rms_bwd(name, dh, x, g, dxin, out_dtype):
    T, D = x.shape
    tm = _tile(T, 512)

    def body(dh_ref, x_ref, g_ref, dxin_ref, dx_ref, dg_ref):
        i = pl.program_id(0)
        xv = x_ref[...]
        dh = dh_ref[...].astype(F32)
        r = lax.rsqrt(jnp.mean(xv * xv, axis=-1, keepdims=True) + NORM_EPS)
        xh = xv * r
        gd = dh * g_ref[...]
        dx = dxin_ref[...].astype(F32) + r * (gd - xh * jnp.mean(gd * xh, axis=-1, keepdims=True))
        dx_ref[...] = dx.astype(out_dtype)
        dgp = jnp.sum(dh * xh, axis=0, keepdims=True)

        @pl.when(i == 0)
        def _():
            dg_ref[...] = dgp

        @pl.when(i > 0)
        def _():
            dg_ref[...] += dgp

    row = Block((tm, D), lambda i: (i, 0))
    vec = Block((1, D), lambda i: (0, 0))
    return pl.pallas_call(
        body, name=name, grid=(T // tm,),
        in_specs=[row, row, vec, row], out_specs=[row, vec],
        out_shape=[SDS((T, D), out_dtype), SDS((1, D), F32)],
        compiler_params=_cparams(("arbitrary",), VMEM_BIG))(dh, x, g, dxin)


def _loss_head(name, x, g, tgt):
    T, D = x.shape
    tm = _tile(T, 512)

    def body(x_ref, g_ref, t_ref, loss_ref, dxb_ref, dg_ref):
        i = pl.program_id(0)
        xv = x_ref[...]
        gv = g_ref[...]
        r = lax.rsqrt(jnp.mean(xv * xv, axis=-1, keepdims=True) + NORM_EPS)
        xh = xv * r
        err = xh * gv - t_ref[...]
        lp = 0.5 * jnp.sum(jnp.mean(err * err, axis=-1, keepdims=True), axis=0, keepdims=True)
        dout = err * (1.0 / D)
        gd = dout * gv
        dx = r * (gd - xh * jnp.mean(gd * xh, axis=-1, keepdims=True))
        dxb_ref[...] = dx.astype(BF16)
        dgp = jnp.sum(dout * xh, axis=0, keepdims=True)
        lpb = jnp.broadcast_to(lp, (1, 128))

        @pl.when(i == 0)
        def _():
            dg_ref[...] = dgp
            loss_ref[...] = lpb

        @pl.when(i > 0)
        def _():
            dg_ref[...] += dgp
            loss_ref[...] += lpb

    row = Block((tm, D), lambda i: (i, 0))
    vec = Block((1, D), lambda i: (0, 0))
    return pl.pallas_call(
        body, name=name, grid=(T // tm,),
        in_specs=[row, vec, row], out_specs=[Block((1, 128), lambda i: (0, 0)), row, vec],
        out_shape=[SDS((1, 128), F32), SDS((T, D), BF16), SDS((1, D), F32)],
        compiler_params=_cparams(("arbitrary",), VMEM_BIG))(x, g, tgt)


def _merge_fwd(name, pa, pb, gl):
    T, D = pa.shape
    tm = _tile(T, 512)

    def body(pa_ref, pb_ref, gl_ref, o_ref):
        ga = _sigmoid(gl_ref[0].astype(F32))
        gb = _sigmoid(gl_ref[1].astype(F32))
        o_ref[...] = (ga * pa_ref[...].astype(F32) + gb * pb_ref[...].astype(F32)).astype(BF16)

    row = Block((tm, D), lambda i: (i, 0))
    return pl.pallas_call(
        body, name=name, grid=(T // tm,),
        in_specs=[row, row, Block((2, tm, D), lambda i: (0, i, 0))], out_specs=row,
        out_shape=SDS((T, D), BF16), compiler_params=_cparams(("arbitrary",), VMEM_BIG))(pa, pb, gl)


def _merge_bwd(name, dm, pa, pb, gl):
    T, D = pa.shape
    tm = _tile(T, 512)

    def body(dm_ref, pa_ref, pb_ref, gl_ref, dpa_ref, dpb_ref, dgl_ref, db_ref):
        i = pl.program_id(0)
        dmv = dm_ref[...].astype(F32)
        ga = _sigmoid(gl_ref[0].astype(F32))
        gb = _sigmoid(gl_ref[1].astype(F32))
        dpa_ref[...] = (dmv * ga).astype(BF16)
        dpb_ref[...] = (dmv * gb).astype(BF16)
        dga = dmv * pa_ref[...].astype(F32) * ga * (1.0 - ga)
        dgb = dmv * pb_ref[...].astype(F32) * gb * (1.0 - gb)
        dgl_ref[0] = dga.astype(BF16)
        dgl_ref[1] = dgb.astype(BF16)
        sa = jnp.sum(dga, axis=0, keepdims=True)
        sb = jnp.sum(dgb, axis=0, keepdims=True)

        @pl.when(i == 0)
        def _():
            db_ref[0] = sa
            db_ref[1] = sb

        @pl.when(i > 0)
        def _():
            db_ref[0] += sa
            db_ref[1] += sb

    row = Block((tm, D), lambda i: (i, 0))
    two = Block((2, tm, D), lambda i: (0, i, 0))
    return pl.pallas_call(
        body, name=name, grid=(T // tm,),
        in_specs=[row, row, row, two], out_specs=[row, row, two, Block((2, 1, D), lambda i: (0, 0, 0))],
        out_shape=[SDS((T, D), BF16), SDS((T, D), BF16), SDS((2, T, D), BF16), SDS((2, 1, D), F32)],
        compiler_params=_cparams(("arbitrary",), VMEM_BIG))(dm, pa, pb, gl)


def _sgu_core(ur, vr, lng, lnb, ws_ref, bs_ref):
    tm = ur.shape[0]
    gu = _gelu(ur)
    gv = _gelu(vr)
    mu = jnp.mean(gv, axis=-1, keepdims=True)
    cen = gv - mu
    rstd = lax.rsqrt(jnp.mean(cen * cen, axis=-1, keepdims=True) + NORM_EPS)
    xhat = cen * rstd
    vn = (xhat * lng + lnb).astype(BF16)
    rows = []
    for n in range(tm // MLP_CHUNK):
        cols = []
        for h in range(SGU_HEADS):
            blk = vn[n * MLP_CHUNK:(n + 1) * MLP_CHUNK, h * 128:(h + 1) * 128]
            cols.append(_dot(ws_ref[h], blk) + bs_ref[h])
        rows.append(jnp.concatenate(cols, axis=1))
    mixed = jnp.concatenate(rows, axis=0) if len(rows) > 1 else rows[0]
    return gu, xhat, rstd, vn, mixed


def _sgu_fwd(name, proj, lng, lnb, wsm, bst):
    T = proj.shape[0]
    W = SGU_WIDTH
    tm = _tile(T, 512)

    def body(u_ref, v_ref, lng_ref, lnb_ref, ws_ref, bs_ref, o_ref):
        gu, _, _, _, mixed = _sgu_core(u_ref[...].astype(F32), v_ref[...].astype(F32), lng_ref[...], lnb_ref[...],
                                       ws_ref, bs_ref)
        o_ref[...] = (gu * mixed).astype(BF16)

    vec = Block((1, W), lambda i: (0, 0))
    return pl.pallas_call(
        body, name=name, grid=(T // tm,),
        in_specs=[Block((tm, W), lambda i: (i, 1)), Block((tm, W), lambda i: (i, 2)), vec, vec,
                  Block((SGU_HEADS, 128, 128), lambda i: (0, 0, 0)), Block((SGU_HEADS, 128, 128), lambda i: (0, 0, 0))],
        out_specs=Block((tm, W), lambda i: (i, 0)), out_shape=SDS((T, W), BF16),
        compiler_params=_cparams(("arbitrary",), VMEM_BIG))(proj, proj, lng, lnb, wsm, bst)


def _sgu_bwd(name, dyb, dua, proj, lng, lnb, wsm, wsmt, bst):
    T = proj.shape[0]
    W = SGU_WIDTH
    tm = _tile(T, 512)

    def body(dy_ref, dua_ref, u_ref, v_ref, lng_ref, lnb_ref, ws_ref, wst_ref, bs_ref,
             duv_ref, dws_ref, dbs_ref, dlng_ref, dlnb_ref):
        i = pl.program_id(0)
        duv_ref[:, :W] = dua_ref[...]
        ur = u_ref[...].astype(F32)
        vr = v_ref[...].astype(F32)
        lng_v = lng_ref[...]
        gu, xhat, rstd, vn, mixed = _sgu_core(ur, vr, lng_v, lnb_ref[...], ws_ref, bs_ref)
        dy = dy_ref[...].astype(F32)
        dgu = dy * mixed
        dmix = dy * gu
        dmb = dmix.astype(BF16)
        dws_p, dbs_p, rows = [], [], []
        for h in range(SGU_HEADS):
            acc_w = jnp.zeros((128, 128), F32)
            acc_b = jnp.zeros((128, 1), F32)
            for n in range(tm // MLP_CHUNK):
                r0 = n * MLP_CHUNK
                dmt = dmb[r0:r0 + MLP_CHUNK, h * 128:(h + 1) * 128]
                acc_w = acc_w + _dot(dmt, vn[r0:r0 + MLP_CHUNK, h * 128:(h + 1) * 128], NT)
                acc_b = acc_b + jnp.sum(dmix[r0:r0 + MLP_CHUNK, h * 128:(h + 1) * 128], axis=1, keepdims=True)
            dws_p.append(acc_w)
            dbs_p.append(jnp.broadcast_to(acc_b, (128, 128)))
        for n in range(tm // MLP_CHUNK):
            r0 = n * MLP_CHUNK
            rows.append(jnp.concatenate(
                [_dot(wst_ref[h], dmb[r0:r0 + MLP_CHUNK, h * 128:(h + 1) * 128]) for h in range(SGU_HEADS)], axis=1))
        dvn = jnp.concatenate(rows, axis=0) if len(rows) > 1 else rows[0]
        dlng_p = jnp.sum(dvn * xhat, axis=0, keepdims=True)
        dlnb_p = jnp.sum(dvn, axis=0, keepdims=True)
        dxh = dvn * lng_v
        dgv = rstd * (dxh - jnp.mean(dxh, axis=-1, keepdims=True)
                      - xhat * jnp.mean(dxh * xhat, axis=-1, keepdims=True))
        duv_ref[:, W:2 * W] = (dgu * _gelu_grad(ur)).astype(BF16)
        duv_ref[:, 2 * W:] = (dgv * _gelu_grad(vr)).astype(BF16)

        @pl.when(i == 0)
        def _():
            for h in range(SGU_HEADS):
                dws_ref[h] = dws_p[h]
                dbs_ref[h] = dbs_p[h]
            dlng_ref[...] = dlng_p
            dlnb_ref[...] = dlnb_p

        @pl.when(i > 0)
        def _():
            for h in range(SGU_HEADS):
                dws_ref[h] += dws_p[h]
                dbs_ref[h] += dbs_p[h]
            dlng_ref[...] += dlng_p
            dlnb_ref[...] += dlnb_p

    vec = Block((1, W), lambda i: (0, 0))
    wsb = Block((SGU_HEADS, 128, 128), lambda i: (0, 0, 0))
    hsq = SDS((SGU_HEADS, 128, 128), F32)
    return pl.pallas_call(
        body, name=name, grid=(T // tm,),
        in_specs=[Block((tm, W), lambda i: (i, 0)), Block((tm, W), lambda i: (i, 0)),
                  Block((tm, W), lambda i: (i, 1)), Block((tm, W), lambda i: (i, 2)),
                  vec, vec, wsb, wsb, wsb],
        out_specs=[Block((tm, 3 * W), lambda i: (i, 0)), wsb, wsb, vec, vec],
        out_shape=[SDS((T, 3 * W), BF16), hsq, hsq, SDS((1, W), F32), SDS((1, W), F32)],
        compiler_params=_cparams(("arbitrary",), VMEM_BIG))(dyb, dua, proj, proj, lng, lnb, wsm, wsmt, bst)


def _s5_disc(lr, li, ldt, brt, bit):
    dt = jnp.exp(ldt)
    decay = jnp.exp(lr * dt)
    abr = decay * jnp.cos(li * dt)
    abi = decay * jnp.sin(li * dt)
    denom = lr * lr + li * li
    nr = abr - 1.0
    ni = abi
    kr = (nr * lr + ni * li) / denom
    ki = (ni * lr - nr * li) / denom
    bkr = kr[None] * brt - ki[None] * bit
    bki = kr[None] * bit + ki[None] * brt
    return abr, abi, bkr, bki


def _s5_prep(lr, li, ldt, brt, bit):
    G, P, C = S5_GROUPS, S5_STATE, S5_GROUP_WIDTH

    def body(lr_ref, li_ref, ldt_ref, br_ref, bi_ref, abr_ref, abi_ref, pwr_ref, pwi_ref, bkr_ref, bki_ref):
        lr_, li_, ldt_ = lr_ref[...], li_ref[...], ldt_ref[...]
        res = _s5_disc(lr_, li_, ldt_, br_ref[...], bi_ref[...])
        for o, r in zip((abr_ref, abi_ref, bkr_ref, bki_ref), res):
            o[...] = r
        dt = jnp.exp(ldt_)
        n = lax.broadcasted_iota(jnp.int32, (S5_SEG, G, P), 0).astype(F32) + 1.0
        dec = jnp.exp((lr_ * dt)[None] * n)
        ang = (li_ * dt)[None] * n
        pwr_ref[...] = dec * jnp.cos(ang)
        pwi_ref[...] = dec * jnp.sin(ang)

    gp = SDS((G, P), F32)
    sgp = SDS((S5_SEG, G, P), F32)
    cgp = SDS((C, G, P), F32)
    return pl.pallas_call(body, name="s5_prep", out_shape=[gp, gp, sgp, sgp, cgp, cgp])(lr, li, ldt, brt, bit)


def _s5_prep_bwd(lr, li, ldt, brt, bit, dabr, dabi, dbkr, dbki):
    G, P, C = S5_GROUPS, S5_STATE, S5_GROUP_WIDTH

    def body(lr_ref, li_ref, ldt_ref, br_ref, bi_ref, dabr_ref, dabi_ref, dbkr_ref, dbki_ref,
             o_lr, o_li, o_ldt, o_br, o_bi):
        _, pull = jax.vjp(_s5_disc, lr_ref[...], li_ref[...], ldt_ref[...], br_ref[...], bi_ref[...])
        g = pull((dabr_ref[...], dabi_ref[...], dbkr_ref[...], dbki_ref[...]))
        for o, r in zip((o_lr, o_li, o_ldt, o_br, o_bi), g):
            o[...] = r

    gp = SDS((G, P), F32)
    cgp = SDS((C, G, P), F32)
    return pl.pallas_call(body, name="s5_prep_bwd", out_shape=[gp, gp, SDS((G, 1), F32), cgp, cgp])(
        lr, li, ldt, brt, bit, dabr, dabi, dbkr, dbki)


def _s5_scan(buf_ref, ar_row, ai_row, pwr_ref, pwi_ref, carry_ref, LG, xs_ref=None, dar_ref=None, dai_ref=None):
    reverse = xs_ref is not None
    NS, SEG = S5_NS, S5_SEG
    sgn = -1.0 if reverse else 1.0
    for lg in range(NS // LG):
        cr = slice(lg * LG, (lg + 1) * LG)
        ci = slice(NS + lg * LG, NS + (lg + 1) * LG)
        ar1, ai1 = ar_row[:, cr], sgn * ai_row[:, cr]
        asr1, asi1 = pwr_ref[SEG - 1:SEG, cr], sgn * pwi_ref[SEG - 1:SEG, cr]
        ar = jnp.broadcast_to(ar1, (8, LG))
        ai = jnp.broadcast_to(ai1, (8, LG))

        def step_of(j):
            return (SEG - 1 - j) if reverse else j

        def p1(j, st):
            sr, si = st
            rows = pl.ds(pl.multiple_of(step_of(j) * 8, 8), 8)
            nr = ar * sr - ai * si + buf_ref[rows, cr]
            ni = ar * si + ai * sr + buf_ref[rows, ci]
            buf_ref[rows, cr] = nr
            buf_ref[rows, ci] = ni
            return nr, ni

        z = jnp.zeros((8, LG), F32)
        er, ei = lax.fori_loop(0, SEG, p1, (z, z), unroll=S5_UNROLL)
        c_r = carry_ref[:, cr]
        c_i = carry_ref[:, ci]
        cs_r, cs_i = [None] * 8, [None] * 8
        order = range(7, -1, -1) if reverse else range(8)
        for s in order:
            cs_r[s], cs_i[s] = c_r, c_i
            e_r, e_i = er[s:s + 1], ei[s:s + 1]
            c_r, c_i = e_r + asr1 * c_r - asi1 * c_i, e_i + asr1 * c_i + asi1 * c_r
        carry_ref[:, cr] = c_r
        carry_ref[:, ci] = c_i
        cmr = jnp.concatenate(cs_r, axis=0)
        cmi = jnp.concatenate(cs_i, axis=0)

        def carried(j):
            pr = pwr_ref[pl.ds(j, 1), cr]
            pi = sgn * pwi_ref[pl.ds(j, 1), cr]
            return pr * cmr - pi * cmi, pr * cmi + pi * cmr

        if not reverse:
            def p2(j, st):
                rows = pl.ds(pl.multiple_of(j * 8, 8), 8)
                wr, wi = carried(j)
                buf_ref[rows, cr] += wr
                buf_ref[rows, ci] += wi
                return st

            lax.fori_loop(0, SEG, p2, 0, unroll=S5_UNROLL)
        else:
            def p2(j, st):
                pr, pi, dr, di = st
                rows = pl.ds(pl.multiple_of(step_of(j) * 8, 8), 8)
                xr = xs_ref[rows, cr]
                xi = xs_ref[rows, ci]
                dr = dr + pr * xr + pi * xi
                di = di + pi * xr - pr * xi
                wr, wi = carried(j)
                gr = buf_ref[rows, cr] + wr
                gi = buf_ref[rows, ci] + wi
                buf_ref[rows, cr] = gr
                buf_ref[rows, ci] = gi
                return gr, gi, dr, di

            st = lax.fori_loop(0, SEG, p2, (cmr, cmi, z, z), unroll=S5_UNROLL)
            dar_ref[:, cr] += st[2]
            dai_ref[:, cr] += st[3]


def _s5_fwd(proj, perm, permt, bdbr, bdbi, bdcr, bdci, abr, abi, asr, asi, dvec, wglu, bglu):
    T = proj.shape[0]
    TC, NS, W = S5_TC, S5_NS, S5_WIDTH
    nc = T // TC

    def body(u_ref, pm_ref, pmt_ref, bdbr_ref, bdbi_ref, bdcr_ref, bdci_ref, ar_ref, ai_ref, asr_ref, asi_ref,
             d_ref, wglu_ref, bglu_ref, ya_ref, xs_ref, ypre_ref, carry_ref):
        i = pl.program_id(0)

        @pl.when(i == 0)
        def _():
            carry_ref[...] = jnp.zeros_like(carry_ref)

        up = _dot(pm_ref[...], u_ref[...]).astype(BF16)
        for j in range(8):
            ut = up[:, j * 128:(j + 1) * 128]
            xs_ref[:, j * 512:(j + 1) * 512] = _dot(ut, bdbr_ref[j])
            xs_ref[:, NS + j * 512:NS + (j + 1) * 512] = _dot(ut, bdbi_ref[j])
        _s5_scan(xs_ref, ar_ref[...], ai_ref[...], asr_ref, asi_ref, carry_ref, S5_LG)
        ys = []
        for j in range(8):
            xr = xs_ref[:, j * 512:(j + 1) * 512].astype(BF16)
            xi = xs_ref[:, NS + j * 512:NS + (j + 1) * 512].astype(BF16)
            ys.append(_dot(xr, bdcr_ref[j]) + _dot(xi, bdci_ref[j]))
        ypre = jnp.concatenate(ys, axis=1) + d_ref[...] * up.astype(F32)
        ypre_ref[...] = ypre
        ya = _gelu(ypre)
        zl = _dot(ya.astype(BF16), wglu_ref[...]) + bglu_ref[...]
        outp = (ya * _sigmoid(zl)).astype(BF16)
        ya_ref[...] = _dot(pmt_ref[...], outp).astype(BF16)

    return pl.pallas_call(
        body, name="s5_fwd", grid=(nc,),
        in_specs=[Block((TC, W), lambda i: (i, 0)), _const((TC, TC)), _const((TC, TC)),
                  _const((8, 128, 512)), _const((8, 128, 512)), _const((8, 512, 128)), _const((8, 512, 128)),
                  _const((1, NS)), _const((1, NS)), _const((S5_SEG, NS)), _const((S5_SEG, NS)),
                  _const((1, W)), _const((W, W)), _const((1, W))],
        out_specs=[Block((TC, W), lambda i: (i, 0)), Block((TC, 2 * NS), lambda i: (i, 0)),
                   Block((TC, W), lambda i: (i, 0))],
        out_shape=[SDS((T, W), BF16), SDS((T, 2 * NS), F32), SDS((T, W), F32)],
        scratch_shapes=[pltpu.VMEM((1, 2 * NS), F32)],
        compiler_params=_cparams(("arbitrary",), VMEM_BIG),
    )(proj, perm, permt, bdbr, bdbi, bdcr, bdci, abr, abi, asr, asi, dvec, wglu, bglu)


def _s5_bwd(dya, proj, ypre, xs, perm, permt, bdbr, bdbi, bdcr, bdci, abr, abi, asr, asi, dvec, wglu, bglu):
    T = proj.shape[0]
    TC, NS, W = S5_TC, S5_NS, S5_WIDTH
    nc = T // TC

    def body(dya_ref, u_ref, ypre_ref, xs_ref, pm_ref, pmt_ref, bdbr_ref, bdbi_ref, bdcr_ref, bdci_ref,
             ar_ref, ai_ref, asr_ref, asi_ref, d_ref, wglu_ref, bglu_ref,
             du_ref, dar_ref, dai_ref, dd_ref, dbglu_ref, o_dbdbr, o_dbdbi, o_dbdcr, o_dbdci, o_dwglu,
             g_ref, carry_ref, dbdbr_ref, dbdbi_ref, dbdcr_ref, dbdci_ref, dwglu_ref):
        i = pl.program_id(0)

        @pl.when(i == 0)
        def _():
            carry_ref[...] = jnp.zeros_like(carry_ref)
            for r in (dbdbr_ref, dbdbi_ref, dbdcr_ref, dbdci_ref, dar_ref, dai_ref, dd_ref, dwglu_ref, dbglu_ref):
                r[...] = jnp.zeros_like(r)

        pm = pm_ref[...]
        dyo = _dot(pm, dya_ref[...])
        up = _dot(pm, u_ref[...]).astype(BF16)
        upf = up.astype(F32)
        ypre_v = ypre_ref[...]
        ya = _gelu(ypre_v)
        yab = ya.astype(BF16)
        sg = _sigmoid(_dot(yab, wglu_ref[...]) + bglu_ref[...])
        dz = dyo * ya * sg * (1.0 - sg)
        dzb = dz.astype(BF16)
        dya_t = dyo * sg + _dot(dzb, wglu_ref[...], NT)
        dwglu_ref[...] += _dot(yab, dzb, TN)
        dbglu_ref[...] += jnp.sum(dz, axis=0, keepdims=True)
        dy = dya_t * _gelu_grad(ypre_v)
        dd_ref[...] += jnp.sum(dy * upf, axis=0, keepdims=True)
        dyb = dy.astype(BF16)
        for j in range(8):
            dyj = dyb[:, j * 128:(j + 1) * 128]
            g_ref[:, j * 512:(j + 1) * 512] = _dot(dyj, bdcr_ref[j], NT)
            g_ref[:, NS + j * 512:NS + (j + 1) * 512] = _dot(dyj, bdci_ref[j], NT)
            dbdcr_ref[j] += _dot(xs_ref[:, j * 512:(j + 1) * 512].astype(BF16), dyj, TN)
            dbdci_ref[j] += _dot(xs_ref[:, NS + j * 512:NS + (j + 1) * 512].astype(BF16), dyj, TN)
        _s5_scan(g_ref, ar_ref[...], ai_ref[...], asr_ref, asi_ref, carry_ref, S5_LG,
                 xs_ref=xs_ref, dar_ref=dar_ref, dai_ref=dai_ref)
        dus = []
        for j in range(8):
            ut = up[:, j * 128:(j + 1) * 128]
            gr = g_ref[:, j * 512:(j + 1) * 512].astype(BF16)
            gi = g_ref[:, NS + j * 512:NS + (j + 1) * 512].astype(BF16)
            dbdbr_ref[j] += _dot(ut, gr, TN)
            dbdbi_ref[j] += _dot(ut, gi, TN)
            dus.append(_dot(gr, bdbr_ref[j], NT) + _dot(gi, bdbi_ref[j], NT))
        dup = jnp.concatenate(dus, axis=1) + d_ref[...] * dy
        du_ref[...] = _dot(pmt_ref[...], dup.astype(BF16)).astype(BF16)

        @pl.when(i == nc - 1)
        def _():
            for src, dst in ((dbdbr_ref, o_dbdbr), (dbdbi_ref, o_dbdbi), (dbdcr_ref, o_dbdcr),
                             (dbdci_ref, o_dbdci), (dwglu_ref, o_dwglu)):
                pltpu.sync_copy(src, dst)

    c2 = lambda i: (0, 0)
    rev = lambda i: (nc - 1 - i, 0)
    return pl.pallas_call(
        body, name="s5_bwd", grid=(nc,),
        in_specs=[Block((TC, W), rev), Block((TC, W), rev), Block((TC, W), rev), Block((TC, 2 * NS), rev),
                  _const((TC, TC)), _const((TC, TC)),
                  _const((8, 128, 512)), _const((8, 128, 512)), _const((8, 512, 128)), _const((8, 512, 128)),
                  _const((1, NS)), _const((1, NS)), _const((S5_SEG, NS)), _const((S5_SEG, NS)),
                  _const((1, W)), _const((W, W)), _const((1, W))],
        out_specs=[Block((TC, W), rev), Block((8, NS), c2), Block((8, NS), c2), Block((1, W), c2), Block((1, W), c2),
                   ANY, ANY, ANY, ANY, ANY],
        out_shape=[SDS((T, W), BF16), SDS((8, NS), F32), SDS((8, NS), F32), SDS((1, W), F32), SDS((1, W), F32),
                   SDS((8, 128, 512), F32), SDS((8, 128, 512), F32),
                   SDS((8, 512, 128), F32), SDS((8, 512, 128), F32), SDS((W, W), F32)],
        scratch_shapes=[pltpu.VMEM((TC, 2 * NS), F32), pltpu.VMEM((1, 2 * NS), F32),
                        pltpu.VMEM((8, 128, 512), F32), pltpu.VMEM((8, 128, 512), F32),
                        pltpu.VMEM((8, 512, 128), F32), pltpu.VMEM((8, 512, 128), F32), pltpu.VMEM((W, W), F32)],
        compiler_params=_cparams(("arbitrary",), VMEM_BIG),
    )(dya, proj, ypre, xs, perm, permt, bdbr, bdbi, bdcr, bdci, abr, abi, asr, asi, dvec, wglu, bglu)


def _bd_b(bk_t):
    C, P = S5_GROUP_WIDTH, S5_STATE
    t = jnp.transpose(bk_t, (1, 0, 2)).reshape(8, 8, C, P)
    eye = jnp.eye(8, dtype=t.dtype)
    return (t[:, :, :, None, :] * eye[None, :, None, :, None]).reshape(8, 8 * C, 8 * P)


def _bd_b_extract(m):
    C, P = S5_GROUP_WIDTH, S5_STATE
    t = m.reshape(8, 8, C, 8, P)
    d = jnp.stack([t[:, g, :, g, :] for g in range(8)], axis=1)
    return jnp.transpose(d.reshape(S5_GROUPS, C, P), (1, 0, 2))


def _bd_c(c):
    C, P = S5_GROUP_WIDTH, S5_STATE
    t = jnp.transpose(c, (0, 2, 1)).reshape(8, 8, P, C)
    eye = jnp.eye(8, dtype=t.dtype)
    return (t[:, :, :, None, :] * eye[None, :, None, :, None]).reshape(8, 8 * P, 8 * C)


def _bd_c_extract(m):
    C, P = S5_GROUP_WIDTH, S5_STATE
    t = m.reshape(8, 8, P, 8, C)
    d = jnp.stack([t[:, g, :, g, :] for g in range(8)], axis=1)
    return jnp.transpose(d.reshape(S5_GROUPS, P, C), (0, 2, 1))


def _perm_matrix():
    r = jnp.arange(S5_TC)
    src = (r % 8) * S5_SEG + r // 8
    return (src[:, None] == jnp.arange(S5_TC)[None, :]).astype(BF16)


def _coords():
    return lax.axis_index("x"), lax.axis_index("y"), lax.axis_index("c")


def _all_gather(name, arrs):
    n = len(arrs)

    def body(*refs):
        ins, outs = refs[:n], refs[n:2 * n]
        send_sems, recv_sems, local_sems = refs[2 * n:]
        x, y, c = _coords()
        me, sibling = (x, y, c), (x, y, 1 - c)
        chips = [(1 - x, y), (x, 1 - y), (1 - x, 1 - y)]

        def slot(p):
            return 4 * p[0] + 2 * p[1] + p[2]

        def copy(a, k, block, to, src=None):
            dst = outs[a].at[slot(block)]
            return pltpu.make_async_remote_copy(
                src_ref=dst if src is None else src, dst_ref=dst,
                send_sem=send_sems.at[a * 7 + k], recv_sem=recv_sems.at[a * 7 + k],
                device_id=to, device_id_type=MESH)

        mine = [pltpu.make_async_copy(ins[a], outs[a].at[slot(me)], local_sems.at[a]) for a in range(n)]
        for m in mine:
            m.start()
        first = []
        for a in range(n):
            first.append(copy(a, 0, me, sibling, src=ins[a]))
            first += [copy(a, 1 + j, me, (*chip, c), src=ins[a]) for j, chip in enumerate(chips)]
        for cp in first:
            cp.start()
        passed = []
        for j, chip in enumerate(chips):
            for a in range(n):
                copy(a, 1 + j, (*chip, c), me).wait_recv()
                fw = copy(a, 4 + j, (*chip, c), sibling)
                fw.start()
                passed.append(fw)
        for a in range(n):
            copy(a, 0, sibling, me).wait_recv()
            for j, chip in enumerate(chips):
                copy(a, 4 + j, (*chip, 1 - c), me).wait_recv()
        for cp in first + passed:
            cp.wait_send()
        for m in mine:
            m.wait()

    return pl.pallas_call(
        body, name=name,
        in_specs=[ANY] * n, out_specs=[ANY] * n,
        out_shape=[SDS((NDEV,) + a.shape, a.dtype) for a in arrs],
        scratch_shapes=[pltpu.SemaphoreType.DMA((7 * n,)), pltpu.SemaphoreType.DMA((7 * n,)),
                        pltpu.SemaphoreType.DMA((n,))],
    )(*arrs)


HBM = pl.BlockSpec(memory_space=pltpu.HBM)
SEM = pl.BlockSpec(memory_space=pltpu.SEMAPHORE)
EFFECT = pltpu.SideEffectType.DATAFLOW_SIDE_EFFECTING


def _peers7(x, y, c):
    return [(1 - x if fx else x, 1 - y if fy else y, 1 - c if fc else c)
            for fx in (0, 1) for fy in (0, 1) for fc in (0, 1) if fx or fy or fc]


def _slot(p):
    return 4 * p[0] + 2 * p[1] + p[2]


def _split_copies(src_refs, land_refs, send_sems, recv_sems, gather, mine):
    x, y, c = _coords()
    me = (x, y, c)
    out = []
    if gather == "a":
        peers = [(x, y, 1 - c), (x, 1 - y, c), (1 - x, y, c), (1 - x, 1 - y, c)]
        for a, (src, land) in enumerate(zip(src_refs, land_refs)):
            for k, p in enumerate(peers):
                out.append(pltpu.make_async_remote_copy(
                    src_ref=src, dst_ref=land.at[_slot(me) if mine else _slot(p)],
                    send_sem=send_sems.at[a * 4 + k], recv_sem=recv_sems.at[a * 4 + k],
                    device_id=p, device_id_type=MESH))
        return out
    if gather == "b":
        sibling = (x, y, 1 - c)
        for a, land in enumerate(land_refs):
            for k, q in enumerate([(x, 1 - y), (1 - x, y), (1 - x, 1 - y)]):
                mine_slab = land.at[_slot((q[0], q[1], c))]
                out.append(pltpu.make_async_remote_copy(
                    src_ref=mine_slab, dst_ref=mine_slab if mine else land.at[_slot((q[0], q[1], 1 - c))],
                    send_sem=send_sems.at[a * 3 + k], recv_sem=recv_sems.at[a * 3 + k],
                    device_id=sibling, device_id_type=MESH))
        return out
    for a, (src, land) in enumerate(zip(src_refs, land_refs)):
        for k, p in enumerate(_peers7(x, y, c)):
            s = src if gather else src.at[_slot(p)]
            out.append(pltpu.make_async_remote_copy(
                src_ref=s, dst_ref=land.at[_slot(me) if mine else _slot(p)],
                send_sem=send_sems.at[a * 7 + k], recv_sem=recv_sems.at[a * 7 + k],
                device_id=p, device_id_type=MESH))
    return out


def _copies_per_array(gather):
    return {"a": 4, "b": 3}.get(gather, 7)


def _own_slab(shard):
    x, y, c = _coords()
    z = lax.empty((NDEV,) + shard.shape, shard.dtype)
    return lax.dynamic_update_slice(z, shard[None], (_slot((x, y, c)),) + (0,) * shard.ndim)


def _split_start(name, srcs, lands, gather):
    ns, nl = len(srcs), len(lands)
    nsem = _copies_per_array(gather) * nl

    def body(*refs):
        src_refs, land_refs = refs[:ns], refs[ns:ns + nl]
        send_sems, recv_sems = refs[ns + nl], refs[ns + nl + 1]
        token = refs[-1]
        for cp in _split_copies(src_refs, land_refs, send_sems, recv_sems, gather, True):
            cp.start()
        token[...] = jnp.zeros_like(token)

    thru = [pltpu.HBM(a.shape, a.dtype) for a in list(srcs) + list(lands)]
    res = pl.pallas_call(
        body, name=name,
        out_shape=(pltpu.SemaphoreType.DMA((nsem,)), pltpu.SemaphoreType.DMA((nsem,)), *thru, SDS((8, 128), F32)),
        in_specs=[HBM] * (ns + nl),
        out_specs=(SEM, SEM, *([HBM] * (ns + nl)), pl.BlockSpec(memory_space=pltpu.VMEM)),
        input_output_aliases={i: 2 + i for i in range(ns + nl)},
        compiler_params=pltpu.CompilerParams(has_side_effects=EFFECT),
    )(*[pltpu.with_memory_space_constraint(a, pltpu.HBM) for a in list(srcs) + list(lands)])
    return res[0], res[1], list(res[2:2 + ns]), list(res[2 + ns:2 + ns + nl]), res[-1]


def _split_wait(name, started, after, gather):
    send_sems, recv_sems, srcs, lands, _ = started
    ns, nl = len(srcs), len(lands)

    def body(*refs):
        src_refs, land_refs = refs[:ns], refs[ns:ns + nl]
        s_sems, r_sems = refs[ns + nl], refs[ns + nl + 1]
        for cp in _split_copies(src_refs, land_refs, s_sems, r_sems, gather, False):
            cp.wait_send()
            cp.wait_recv()

    thru = [pltpu.HBM(a.shape, a.dtype) for a in list(srcs) + list(lands)]
    res = pl.pallas_call(
        body, name=name, out_shape=tuple(thru),
        in_specs=[HBM] * (ns + nl) + [SEM, SEM, ANY], out_specs=tuple([HBM] * (ns + nl)),
        input_output_aliases={i: i for i in range(ns + nl)},
        compiler_params=pltpu.CompilerParams(has_side_effects=EFFECT),
    )(*srcs, *lands, send_sems, recv_sems, after)
    return list(res[ns:])


def _adam_math(w, g, m, v):
    m = ADAM_B1 * m + (1.0 - ADAM_B1) * g
    v = ADAM_B2 * v + (1.0 - ADAM_B2) * (g * g)
    m_hat = m / (1.0 - ADAM_B1 ** ADAM_STEP)
    v_hat = v / (1.0 - ADAM_B2 ** ADAM_STEP)
    delta = -ADAM_LR * (m_hat / (jnp.sqrt(v_hat) + ADAM_EPS) + ADAM_WD * w)
    return delta, m, v


def _adam_sharded(name, recv, sub, w, m, v):
    R, Cc = w.shape
    tr = max(t for t in range(16, R + 1, 16) if R % t == 0 and t * Cc <= 256 * 1024)

    def body(*refs):
        parts = refs[:NDEV]
        w_ref, m_ref, v_ref, g_out, d_out, m_out, v_out = refs[NDEV:]
        g = parts[0][...].astype(F32)
        for p in parts[1:]:
            g = g + p[...].astype(F32)
        delta, mn, vn = _adam_math(w_ref[...], g, m_ref[...], v_ref[...])
        g_out[...] = g
        d_out[...] = delta
        m_out[...] = mn
        v_out[...] = vn

    if sub is None:
        pspecs = [Block((None, tr, Cc), functools.partial(lambda s, i: (s, i, 0), s)) for s in range(NDEV)]
    else:
        pspecs = [Block((None, None, tr, Cc), functools.partial(lambda s, i: (s, sub, i, 0), s)) for s in range(NDEV)]
    row = Block((tr, Cc), lambda i: (i, 0))
    o = SDS((R, Cc), F32)
    return pl.pallas_call(
        body, name=name, grid=(R // tr,),
        in_specs=pspecs + [row, row, row], out_specs=[row, row, row, row], out_shape=[o, o, o, o],
        compiler_params=_cparams(("arbitrary",), VMEM_BIG))(*([recv] * NDEV), w, m, v)


def _adam_small(groups):
    n = len(groups)

    def body(*refs):
        ins, outs = refs[:4 * n], refs[4 * n:]
        for a in range(n):
            p_ref, w_ref, m_ref, v_ref = ins[4 * a:4 * a + 4]
            g = p_ref[0]
            for s in range(1, NDEV):
                g = g + p_ref[s]
            delta, mn, vn = _adam_math(w_ref[...], g, m_ref[...], v_ref[...])
            for o, r in zip(outs[4 * a:4 * a + 4], (g, delta, mn, vn)):
                o[...] = r

    flat_in = [t for grp in groups for t in grp]
    out_shape = [SDS(grp[1].shape, F32) for grp in groups for _ in range(4)]
    res = pl.pallas_call(body, name="adam_small", out_shape=out_shape,
                         compiler_params=_cparams(None, VMEM_BIG))(*flat_in)
    return [tuple(res[4 * a:4 * a + 4]) for a in range(n)]


_TINY = ["mix_norm", "s5_a_re", "s5_a_im", "s5_log_dt", "s5_d", "s5_b_glu", "sgu_ln_g", "sgu_ln_b",
         "sgu_b_s", "b_gate", "ffn2_norm", "final_norm"]
_ORDER = ["ffn1_norm", "ffn1_w_gate", "ffn1_w_up", "ffn1_w_down", "mix_norm", "w_in", "s5_a_re", "s5_a_im",
          "s5_log_dt", "s5_b_re", "s5_b_im", "s5_c_re", "s5_c_im", "s5_d", "s5_w_glu", "s5_b_glu", "sgu_ln_g",
          "sgu_ln_b", "sgu_w_s", "sgu_b_s", "w_branch_a", "w_branch_b", "w_gate", "b_gate", "w_out", "ffn2_norm",
          "ffn2_w_gate", "ffn2_w_up", "ffn2_w_down", "final_norm"]


def _step(x, tgt, W, M, V):
    T = x.shape[1]
    x0 = x[0]
    tgt0 = tgt[0]
    bf = lambda a: a.astype(BF16)

    def gather_start(name, shards):
        return _split_start(name, shards, [_own_slab(s) for s in shards], True)

    gu1_sh = jnp.stack([bf(W["ffn1_w_gate"][0].T), bf(W["ffn1_w_up"][0].T)])
    g1a = _split_start("gather1a_start", [gu1_sh], [_own_slab(gu1_sh)], "a")
    tok1 = g1a[4][:1, :1]

    lr_, li_ = W["s5_a_re"][0] + tok1, W["s5_a_im"][0]
    ldt_ = W["s5_log_dt"][0][:, None]
    brt = jnp.transpose(W["s5_b_re"][0], (2, 0, 1))
    bit = jnp.transpose(W["s5_b_im"][0], (2, 0, 1))
    abr, abi, pwr, pwi, bkr_t, bki_t = _s5_prep(lr_, li_, ldt_, brt, bit)
    bdbr, bdbi = bf(_bd_b(bkr_t)), bf(_bd_b(bki_t))
    bdcr, bdci = bf(_bd_c(W["s5_c_re"][0])), bf(_bd_c(-W["s5_c_im"][0]))
    flat = lambda a: a.reshape(1, S5_NS)
    s5a = (_perm_matrix(), _perm_matrix().T, bdbr, bdbi, bdcr, bdci, flat(abr), flat(abi),
           pwr.reshape(S5_SEG, S5_NS), pwi.reshape(S5_SEG, S5_NS),
           W["s5_d"][0].reshape(1, S5_WIDTH))
    blk = jnp.arange(MLP_CHUNK) // CHUNK
    mask = blk[:, None] >= blk[None, :]
    wsm = jnp.where(mask[None], W["sgu_w_s"][0], 0.0)
    wsm_b, wsmt_b = bf(wsm), bf(jnp.transpose(wsm, (0, 2, 1)))
    bst = jnp.broadcast_to(W["sgu_b_s"][0][:, :, None], (SGU_HEADS, MLP_CHUNK, 128))
    bgate2 = W["b_gate"].reshape(2, 1, D_MODEL)

    tokb = tok1.astype(BF16)
    early = {n: bf(W[n][0].T if n in ("w_in", "ffn2_w_gate", "ffn2_w_up") else W[n][0]) + tokb
             for n in ("ffn1_w_down", "w_in", "s5_w_glu", "w_gate", "w_branch_a", "w_branch_b", "w_out",
                       "ffn2_w_gate", "ffn2_w_up", "ffn2_w_down")}
    early = lax.optimization_barrier(early)
    h1 = _rms_fwd("rms1", x0, W["ffn1_norm"] + tok1)
    (land1,) = _split_wait("gather1a_wait", g1a, h1, "a")
    g1b = _split_start("gather1b_start", [], [land1], "b")
    (wgu1,) = _split_wait("gather1b_wait", g1b, g1b[4], "b")
    dep = (wgu1[0, 0, :1, :1] * 0).astype(BF16)

    def later(n):
        return early[n] + dep[0]

    gs2 = gather_start("gather2_start", [later("ffn1_w_down")])
    ab1, f1 = _ffn_up("ffn1_up", h1, wgu1, gs2[4])
    (wd1,) = _split_wait("gather2_wait", gs2, f1, True)
    dep = (wd1[0, :1, :1] * 0).astype(BF16)
    gs3 = gather_start("gather3_start", [later("w_in"), later("s5_w_glu")])
    x1 = _ffn_down("ffn1_down", f1, wd1, x0, after=gs3[4])
    h2 = _rms_fwd("rms2", x1, W["mix_norm"])
    win, wglu = _split_wait("gather3_wait", gs3, h2, True)
    wglu = wglu.reshape(S5_WIDTH, S5_WIDTH)
    s5c = s5a + (wglu, W["s5_b_glu"])
    dep = (win[0, :1, :1] * 0).astype(BF16)
    win = win.reshape(S5_WIDTH + 2 * SGU_WIDTH, D_MODEL)
    gs4 = gather_start("gather4_start", [later("w_gate"), later("w_branch_a"), later("w_branch_b"), later("w_out")])
    proj = _rows_fwd("w_in", h2, win, after=gs4[4])
    ya, xs, ypre = _s5_fwd(proj, *s5c)
    dep = (ya[:1, :1] * 0).astype(BF16)
    gs5 = gather_start("gather5_start", [jnp.stack([later("ffn2_w_gate"), later("ffn2_w_up")])])
    yb = _sgu_fwd("sgu_fwd", proj, W["sgu_ln_g"] + gs5[4][:1, :1], W["sgu_ln_b"], wsm_b, bst)
    wgate, wba, wbb, wout = _split_wait("gather4_wait", gs4, yb, True)
    wout = wout.reshape(D_MODEL, D_MODEL)
    pa = _col_fwd("branch_a", ya, wba)
    pb = _col_fwd("branch_b", yb, wbb)
    gl = _gate_fwd("gate", h2, wgate, bgate2)
    merged = _merge_fwd("merge", pa, pb, gl)
    x2 = _plain_fwd_res("w_out", merged, wout, x1)
    h3 = _rms_fwd("rms3", x2, W["ffn2_norm"])
    (wgu2,) = _split_wait("gather5_wait", gs5, h3, True)
    dep = (wgu2[0, 0, :1, :1] * 0).astype(BF16)
    gs6 = gather_start("gather6_start", [later("ffn2_w_down")])
    ab2, f2 = _ffn_up("ffn2_up", h3, wgu2, gs6[4])
    (wd2,) = _split_wait("gather6_wait", gs6, f2, True)
    x3 = _ffn_down("ffn2_down", f2, wd2, x2)
    loss_p, dx3b, dgf = _loss_head("loss_head", x3, W["final_norm"].reshape(1, D_MODEL), tgt0)

    def exchange_start(name, grads):
        x_, y_, c_ = _coords()
        me = _slot((x_, y_, c_))
        return _split_start(name, grads, [_own_slab(lax.dynamic_index_in_dim(g, me, 0, keepdims=False))
                                          for g in grads], False)

    dab2 = _ffn_down_bwd_act("ffn2_down_bwd_a", dx3b, wd2, ab2)
    g_wd2 = _ffn_down_bwd_w("ffn2_down_bwd_w", f2, dx3b)
    g_gu2 = _ffn_up_bwd_w("ffn2_up_bwd_w", h3, dab2)
    es1 = exchange_start("exchange1_start", [g_wd2, g_gu2])
    dh3 = _ffn_up_bwd_h("ffn2_up_bwd_h", dab2, wgu2, es1[4])
    dx2b, dg3 = _rms_bwd("rms3_bwd", dh3, x2, W["ffn2_norm"], dx3b, BF16)

    dmerged = _plain_bwd_a("w_out_bwd_a", dx2b, wout)
    g_wout = _plain_bwd_w("w_out_bwd_w", merged, dx2b)
    dpa, dpb, dgl, dbgate = _merge_bwd("merge_bwd", dmerged, pa, pb, gl)
    dya = _col_bwd_a("branch_a_bwd_a", dpa, wba)
    g_wba = _col_bwd_w("branch_a_bwd_w", ya, dpa, 256)
    dyb = _col_bwd_a("branch_b_bwd_a", dpb, wbb)
    g_wbb = _col_bwd_w("branch_b_bwd_w", yb, dpb, 256)
    dh2g = _gate_bwd_a("gate_bwd_a", dgl, wgate)
    g_wgate = _gate_bwd_w("gate_bwd_w", h2, dgl, 512)
    (dua, dar8, dai8, ddv, dbglu, dbdbr, dbdbi, dbdcr, dbdci, g_wglu) = _s5_bwd(dya, proj, ypre, xs, *s5c)
    dproj, dws, dbst, dlng, dlnb = _sgu_bwd("sgu_bwd", dyb, dua, proj, W["sgu_ln_g"], W["sgu_ln_b"],
                                            wsm_b, wsmt_b, bst)
    g_win = _col_bwd_w("w_in_bwd_w", h2, dproj, 384)
    g_wout3 = g_wout.reshape(NDEV, D_MODEL // NDEV, D_MODEL)
    g_wglu3 = g_wglu.astype(BF16).reshape(NDEV, S5_WIDTH // NDEV, S5_WIDTH)
    es2 = exchange_start("exchange2_start", [g_wout3, g_wba, g_wbb, g_wgate, g_wglu3, g_win])
    dh2 = _rows_bwd_a("w_in_bwd_a", dproj, win, dh2g)
    dx1b, dgm = _rms_bwd("rms2_bwd", dh2, x1, W["mix_norm"] + es2[4][:1, :1], dx2b, BF16)

    dabr = jnp.sum(dar8, axis=0).reshape(S5_GROUPS, S5_STATE)
    dabi = jnp.sum(dai8, axis=0).reshape(S5_GROUPS, S5_STATE)
    d_lr, d_li, d_ldt, d_brt, d_bit = _s5_prep_bwd(lr_, li_, ldt_, brt, bit, dabr, dabi,
                                                   _bd_b_extract(dbdbr), _bd_b_extract(dbdbi))
    small_g = {
        "mix_norm": dgm, "ffn2_norm": dg3, "final_norm": dgf,
        "s5_a_re": d_lr, "s5_a_im": d_li, "s5_log_dt": d_ldt,
        "s5_d": ddv, "s5_b_glu": dbglu, "sgu_ln_g": dlng, "sgu_ln_b": dlnb,
        "sgu_b_s": dbst[:, :, 0], "b_gate": dbgate,
    }
    to_cgp = lambda a: jnp.transpose(a[0], (2, 0, 1))
    from_cgp = lambda a: jnp.transpose(a, (1, 2, 0))[None]
    natural = [
        ("s5_b_re", d_brt, to_cgp, from_cgp), ("s5_b_im", d_bit, to_cgp, from_cgp),
        ("s5_c_re", _bd_c_extract(dbdcr), lambda a: a[0], lambda a: a[None]),
        ("s5_c_im", -_bd_c_extract(dbdci), lambda a: a[0], lambda a: a[None]),
        ("sgu_w_s", jnp.where(mask[None], dws, 0.0), lambda a: a[0], lambda a: a[None]),
    ]
    sizes = [W[n].size for n in _TINY]
    total = sum(sizes) + 1
    rows = -(-total // 128)
    rows = -(-rows // 8) * 8
    pad = rows * 128 - total

    def pack(d, extra):
        return jnp.concatenate([d[n].reshape(-1).astype(F32) for n in _TINY] + [extra, jnp.zeros((pad,), F32)]
                               ).reshape(rows, 128)

    gsm = gather_start("gather_small_start", [pack(small_g, loss_p[0, :1])] + [g for _, g, _, _ in natural])

    dab1 = _ffn_down_bwd_act("ffn1_down_bwd_a", dx1b, wd1, ab1, after=gsm[4])
    g_gu1 = _ffn_up_bwd_w("ffn1_up_bwd_w", h1, dab1)
    es3 = exchange_start("exchange3_start", [g_gu1])
    g_wd1 = _ffn_down_bwd_w("ffn1_down_bwd_w", f1, dx1b, after=es3[4])
    es4 = exchange_start("exchange4_start", [g_wd1])
    dh1 = _ffn_up_bwd_h("ffn1_up_bwd_h", dab1, wgu1, es4[4])
    dx0, dg1 = _rms_bwd("rms1_bwd", dh1, x0, W["ffn1_norm"], dx1b, F32)

    G, Dl, Mn, Vn = {}, {}, {}, {}

    def adam(plan):
        last = None
        for n, recv, sub in plan:
            if sub is None:
                g, d, mn, vn = _adam_sharded("adam_" + n, recv, sub, W[n][0], M[n][0], V[n][0])
                G[n], Dl[n], Mn[n], Vn[n] = g[None], d[None], mn[None], vn[None]
            else:
                tr = jnp.transpose
                g, d, mn, vn = _adam_sharded("adam_" + n, recv, sub, tr(W[n][0]), tr(M[n][0]), tr(V[n][0]))
                G[n], Dl[n], Mn[n], Vn[n] = tr(g)[None], tr(d)[None], tr(mn)[None], tr(vn)[None]
            last = g
        return last

    r_wd2, r_gu2 = _split_wait("exchange1_wait", es1, dx0, False)
    done = adam([("ffn2_w_down", r_wd2, None), ("ffn2_w_gate", r_gu2, 0), ("ffn2_w_up", r_gu2, 1)])
    r_wout, r_wba, r_wbb, r_wgate, r_wglu, r_win = _split_wait("exchange2_wait", es2, done, False)
    done = adam([("w_out", r_wout, None), ("w_branch_a", r_wba, None), ("w_branch_b", r_wbb, None),
                 ("w_gate", r_wgate, None), ("s5_w_glu", r_wglu, None), ("w_in", r_win, None)])

    late = dg1 + 0.0 * done.reshape(-1)[:1]
    zero1 = jnp.zeros((1,), F32)
    parts = _split_wait("gather_small_wait", gsm, late, True)
    (parts_g1,) = _all_gather("gather_ffn1_norm_grad", [late])
    groups = [(parts[0], pack(W, zero1), pack(M, zero1), pack(V, zero1))]
    groups += [(parts[1 + a], view(W[n]), view(M[n]), view(V[n])) for a, (n, _, view, _) in enumerate(natural)]
    groups += [(parts_g1, W["ffn1_norm"], M["ffn1_norm"], V["ffn1_norm"])]
    res = _adam_small(groups)
    sg, sd, sm, sv = res[0]
    for (n, _, _, back), (g, d, mn, vn) in zip(natural, res[1:-1]):
        G[n], Dl[n], Mn[n], Vn[n] = back(g), back(d), back(mn), back(vn)
    G["ffn1_norm"], Dl["ffn1_norm"], Mn["ffn1_norm"], Vn["ffn1_norm"] = res[-1]

    def unpack(flat2d, into):
        flat = flat2d.reshape(-1)
        off = 0
        for n, s in zip(_TINY, sizes):
            into[n] = flat[off:off + s].reshape(W[n].shape)
            off += s
        return flat[off]

    loss = unpack(sg, G)
    unpack(sd, Dl)
    unpack(sm, Mn)
    unpack(sv, Vn)

    (r_gu1,) = _split_wait("exchange3_wait", es3, sg, False)
    done = adam([("ffn1_w_gate", r_gu1, 0), ("ffn1_w_up", r_gu1, 1)])
    (r_wd1,) = _split_wait("exchange4_wait", es4, done, False)
    adam([("ffn1_w_down", r_wd1, None)])

    return loss, dx0[None], G, Dl, Mn, Vn


def kernel(x, ffn1_norm, ffn1_w_gate, ffn1_w_up, ffn1_w_down, mix_norm, w_in, s5_a_re, s5_a_im, s5_log_dt, s5_b_re, s5_b_im, s5_c_re, s5_c_im, s5_d, s5_w_glu, s5_b_glu, sgu_ln_g, sgu_ln_b, sgu_w_s, sgu_b_s, w_branch_a, w_branch_b, w_gate, b_gate, w_out, ffn2_norm, ffn2_w_gate, ffn2_w_up, ffn2_w_down, final_norm, loss_target, m_ffn1_norm, m_ffn1_w_gate, m_ffn1_w_up, m_ffn1_w_down, m_mix_norm, m_w_in, m_s5_a_re, m_s5_a_im, m_s5_log_dt, m_s5_b_re, m_s5_b_im, m_s5_c_re, m_s5_c_im, m_s5_d, m_s5_w_glu, m_s5_b_glu, m_sgu_ln_g, m_sgu_ln_b, m_sgu_w_s, m_sgu_b_s, m_w_branch_a, m_w_branch_b, m_w_gate, m_b_gate, m_w_out, m_ffn2_norm, m_ffn2_w_gate, m_ffn2_w_up, m_ffn2_w_down, m_final_norm, v_ffn1_norm, v_ffn1_w_gate, v_ffn1_w_up, v_ffn1_w_down, v_mix_norm, v_w_in, v_s5_a_re, v_s5_a_im, v_s5_log_dt, v_s5_b_re, v_s5_b_im, v_s5_c_re, v_s5_c_im, v_s5_d, v_s5_w_glu, v_s5_b_glu, v_sgu_ln_g, v_sgu_ln_b, v_sgu_w_s, v_sgu_b_s, v_w_branch_a, v_w_branch_b, v_w_gate, v_b_gate, v_w_out, v_ffn2_norm, v_ffn2_w_gate, v_ffn2_w_up, v_ffn2_w_down, v_final_norm):
    a = locals()
    W = {n: a[n] for n in _ORDER}
    M = {n: a["m_" + n] for n in _ORDER}
    V = {n: a["v_" + n] for n in _ORDER}
    loss, gx, G, Dl, Mn, Vn = _step(x, loss_target, W, M, V)
    return (loss, gx, *[G[n] for n in _ORDER], *[Dl[n] for n in _ORDER], *[Mn[n] for n in _ORDER],
            *[Vn[n] for n in _ORDER])
```
